```python
import jax, jax.numpy as jnp
from jax import lax
import numpy as np

D_MODEL = 1024
BATCH = 8
SEQ = 8192
DEPTH = 1

GRID_W = 64
Q_BLOCK = 128
ROPE_THETA = 10000.0
EPS = 1e-6
D_MIX = D_MODEL
A_HEAD_DIM = 64
A_WIDTH = D_MIX // 2
A_HEADS = A_WIDTH // A_HEAD_DIM
A_KV_HEADS = 2
B_WIDTH = D_MIX - A_WIDTH
B_V_HEAD_DIM = 128
B_HEADS = B_WIDTH // B_V_HEAD_DIM
B_NOPE_DIM = 64
B_ROPE_DIM = 32
B_QK_DIM = B_NOPE_DIM + B_ROPE_DIM
B_Q_RANK = 384
B_KV_RANK = 256

_SIZES = (
    A_WIDTH,
    A_KV_HEADS * A_HEAD_DIM,
    A_KV_HEADS * A_HEAD_DIM,
    A_WIDTH,
    B_Q_RANK,
    B_KV_RANK,
    B_ROPE_DIM,
    B_WIDTH,
)
N_IN = int(sum(_SIZES))
_SPLITS = tuple(int(v) for v in np.cumsum(_SIZES)[:-1])

kernel_name = "hybrid_gqa_mla_parallel_groups"


def rms_norm(x, g):
    xf = x.astype(jnp.float32)
    y = xf * lax.rsqrt(jnp.mean(xf * xf, axis=-1, keepdims=True) + EPS)
    return (y * g.astype(jnp.float32)).astype(x.dtype)


def axial_rope_tables(seq_len, dim):
    rows = seq_len // GRID_W
    row = jnp.repeat(jnp.arange(rows, dtype=jnp.float32), GRID_W)
    col = jnp.tile(jnp.arange(GRID_W, dtype=jnp.float32), rows)
    half = dim // 2
    inv = 1.0 / (ROPE_THETA ** (jnp.arange(0, half, 2, dtype=jnp.float32) / half))
    ang_r = row[:, None] * inv[None, :]
    ang_c = col[:, None] * inv[None, :]
    return jnp.cos(ang_r), jnp.sin(ang_r), jnp.cos(ang_c), jnp.sin(ang_c)


def _rotate(xh, cos, sin):
    x1, x2 = jnp.split(xh, 2, axis=-1)
    return jnp.concatenate([x1 * cos - x2 * sin, x1 * sin + x2 * cos], axis=-1)


def apply_axial_rope(x, tables):
    cr, sr, cc, sc = tables
    xf = x.astype(jnp.float32)
    xr, xc = jnp.split(xf, 2, axis=-1)
    out = jnp.concatenate([_rotate(xr, cr, sr), _rotate(xc, cc, sc)], axis=-1)
    return out.astype(x.dtype)


def blocked_attention(q, k, v):
    bsz, n_heads, s_len, dk = q.shape
    n_kv = k.shape[1]
    grp = n_heads // n_kv
    dv = v.shape[-1]
    nb = s_len // Q_BLOCK
    scale = 1.0 / float(np.sqrt(dk))
    qb = q.reshape(bsz, n_kv, grp, nb, Q_BLOCK, dk).transpose(3, 0, 1, 2, 4, 5)

    def one_block(qblk):
        s = jnp.einsum('bkgqd,bksd->bkgqs', qblk, k).astype(jnp.float32) * scale
        p = jax.nn.softmax(s, axis=-1).astype(v.dtype)
        return jnp.einsum('bkgqs,bksd->bkgqd', p, v)

    out = lax.map(one_block, qb)
    out = out.transpose(1, 0, 4, 2, 3, 5)
    return out.reshape(bsz, s_len, n_heads * dv)


def _fwd_setup_inputs(seed: int = 0) -> dict:
    key = jax.random.key(seed)
    ks = jax.random.split(key, 16)
    f32 = jnp.float32

    def w(k, shape, fan_in):
        return jax.random.normal(k, shape, f32) * (fan_in ** -0.5)

    def gain(k, shape):
        return 1.0 + 0.02 * jax.random.normal(k, shape, f32)

    return {
        "x": jax.random.normal(ks[0], (BATCH, SEQ, D_MODEL), f32),
        "norm_in": gain(ks[1], (DEPTH, D_MODEL)),
        "w_in": w(ks[2], (DEPTH, D_MODEL, N_IN), D_MODEL),
        "a_q_norm": gain(ks[3], (DEPTH, A_HEAD_DIM)),
        "a_k_norm": gain(ks[4], (DEPTH, A_HEAD_DIM)),
        "b_cq_norm": gain(ks[5], (DEPTH, B_Q_RANK)),
        "b_ckv_norm": gain(ks[6], (DEPTH, B_KV_RANK)),
        "w_uq": w(ks[7], (DEPTH, B_Q_RANK, B_HEADS * B_QK_DIM), B_Q_RANK),
        "w_ukv": w(ks[8], (DEPTH, B_KV_RANK, B_HEADS * (B_NOPE_DIM + B_V_HEAD_DIM)), B_KV_RANK),
        "b_q_norm": gain(ks[9], (DEPTH, B_QK_DIM)),
        "b_k_norm": gain(ks[10], (DEPTH, B_QK_DIM)),
        "w_out": w(ks[11], (DEPTH, D_MIX, D_MODEL), D_MIX),
    }


def _fwd_reference(x, norm_in, w_in, a_q_norm, a_k_norm, b_cq_norm, b_ckv_norm,
              w_uq, w_ukv, b_q_norm, b_k_norm, w_out):
    bsz, s_len, _ = x.shape
    rope_a = axial_rope_tables(s_len, A_HEAD_DIM)
    rope_b = axial_rope_tables(s_len, B_ROPE_DIM)
    h = x
    for l in range(DEPTH):
        xn = rms_norm(h, norm_in[l])
        proj = jnp.einsum('bsd,dn->bsn', xn, w_in[l])
        a_q, a_k, a_v, a_g, b_cq, b_ckv, b_kr, b_g = jnp.split(proj, _SPLITS, axis=-1)

        q = a_q.reshape(bsz, s_len, A_HEADS, A_HEAD_DIM).transpose(0, 2, 1, 3)
        k = a_k.reshape(bsz, s_len, A_KV_HEADS, A_HEAD_DIM).transpose(0, 2, 1, 3)
        v = a_v.reshape(bsz, s_len, A_KV_HEADS, A_HEAD_DIM).transpose(0, 2, 1, 3)
        q = apply_axial_rope(rms_norm(q, a_q_norm[l]), rope_a)
        k = apply_axial_rope(rms_norm(k, a_k_norm[l]), rope_a)
        y_a = blocked_attention(q, k, v)

        c_q = rms_norm(b_cq, b_cq_norm[l])
        c_kv = rms_norm(b_ckv, b_ckv_norm[l])
        qb = jnp.einsum('bsr,rn->bsn', c_q, w_uq[l]).reshape(bsz, s_len, B_HEADS, B_QK_DIM)
        kvb = jnp.einsum('bsr,rn->bsn', c_kv, w_ukv[l]).reshape(
            bsz, s_len, B_HEADS, B_NOPE_DIM + B_V_HEAD_DIM)
        k_nope, vb = jnp.split(kvb, [B_NOPE_DIM], axis=-1)
        k_rope = jnp.broadcast_to(b_kr[:, :, None, :], (bsz, s_len, B_HEADS, B_ROPE_DIM))
        kb = jnp.concatenate([k_nope, k_rope], axis=-1)
        qb = rms_norm(qb, b_q_norm[l]).transpose(0, 2, 1, 3)
        kb = rms_norm(kb, b_k_norm[l]).transpose(0, 2, 1, 3)
        qb = jnp.concatenate([qb[..., :B_NOPE_DIM],
                              apply_axial_rope(qb[..., B_NOPE_DIM:], rope_b)], axis=-1)
        kb = jnp.concatenate([kb[..., :B_NOPE_DIM],
                              apply_axial_rope(kb[..., B_NOPE_DIM:], rope_b)], axis=-1)
        vb = vb.transpose(0, 2, 1, 3)
        y_b = blocked_attention(qb, kb, vb)

        y = jnp.concatenate([y_a * jax.nn.silu(a_g), y_b * jax.nn.silu(b_g)], axis=-1)
        h = h + jnp.einsum('bsm,md->bsd', y, w_out[l])
    return h


import jax as _jax
import jax.numpy as _jnp

TWIN_FORMAT = 'train_step'
FWD_PARAMS = ['x', 'norm_in', 'w_in', 'a_q_norm', 'a_k_norm', 'b_cq_norm', 'b_ckv_norm', 'w_uq', 'w_ukv', 'b_q_norm', 'b_k_norm', 'w_out']
TWIN_WEIGHTS = ['norm_in', 'w_in', 'a_q_norm', 'a_k_norm', 'b_cq_norm', 'b_ckv_norm', 'w_uq', 'w_ukv', 'b_q_norm', 'b_k_norm', 'w_out']
TWIN_DIFF_INPUT = 'x'
TWIN_INPUTS = ['x', 'norm_in', 'w_in', 'a_q_norm', 'a_k_norm', 'b_cq_norm', 'b_ckv_norm', 'w_uq', 'w_ukv', 'b_q_norm', 'b_k_norm', 'w_out', 'loss_target', 'm_norm_in', 'm_w_in', 'm_a_q_norm', 'm_a_k_norm', 'm_b_cq_norm', 'm_b_ckv_norm', 'm_w_uq', 'm_w_ukv', 'm_b_q_norm', 'm_b_k_norm', 'm_w_out', 'v_norm_in', 'v_w_in', 'v_a_q_norm', 'v_a_k_norm', 'v_b_cq_norm', 'v_b_ckv_norm', 'v_w_uq', 'v_w_ukv', 'v_b_q_norm', 'v_b_k_norm', 'v_w_out']
TWIN_OUTPUTS = ['loss', 'grad_x', 'grad_norm_in', 'grad_w_in', 'grad_a_q_norm', 'grad_a_k_norm', 'grad_b_cq_norm', 'grad_b_ckv_norm', 'grad_w_uq', 'grad_w_ukv', 'grad_b_q_norm', 'grad_b_k_norm', 'grad_w_out', 'delta_norm_in', 'delta_w_in', 'delta_a_q_norm', 'delta_a_k_norm', 'delta_b_cq_norm', 'delta_b_ckv_norm', 'delta_w_uq', 'delta_w_ukv', 'delta_b_q_norm', 'delta_b_k_norm', 'delta_w_out', 'new_m_norm_in', 'new_m_w_in', 'new_m_a_q_norm', 'new_m_a_k_norm', 'new_m_b_cq_norm', 'new_m_b_ckv_norm', 'new_m_w_uq', 'new_m_w_ukv', 'new_m_b_q_norm', 'new_m_b_k_norm', 'new_m_w_out', 'new_v_norm_in', 'new_v_w_in', 'new_v_a_q_norm', 'new_v_a_k_norm', 'new_v_b_cq_norm', 'new_v_b_ckv_norm', 'new_v_w_uq', 'new_v_w_ukv', 'new_v_b_q_norm', 'new_v_b_k_norm', 'new_v_w_out']
TWIN_LEAF_KINDS = {'loss': 'loss', 'grad_x': 'grad_x', 'grad_norm_in': 'grad_w', 'grad_w_in': 'grad_w', 'grad_a_q_norm': 'grad_w', 'grad_a_k_norm': 'grad_w', 'grad_b_cq_norm': 'grad_w', 'grad_b_ckv_norm': 'grad_w', 'grad_w_uq': 'grad_w', 'grad_w_ukv': 'grad_w', 'grad_b_q_norm': 'grad_w', 'grad_b_k_norm': 'grad_w', 'grad_w_out': 'grad_w', 'delta_norm_in': 'delta_w', 'delta_w_in': 'delta_w', 'delta_a_q_norm': 'delta_w', 'delta_a_k_norm': 'delta_w', 'delta_b_cq_norm': 'delta_w', 'delta_b_ckv_norm': 'delta_w', 'delta_w_uq': 'delta_w', 'delta_w_ukv': 'delta_w', 'delta_b_q_norm': 'delta_w', 'delta_b_k_norm': 'delta_w', 'delta_w_out': 'delta_w', 'new_m_norm_in': 'new_m', 'new_m_w_in': 'new_m', 'new_m_a_q_norm': 'new_m', 'new_m_a_k_norm': 'new_m', 'new_m_b_cq_norm': 'new_m', 'new_m_b_ckv_norm': 'new_m', 'new_m_w_uq': 'new_m', 'new_m_w_ukv': 'new_m', 'new_m_b_q_norm': 'new_m', 'new_m_b_k_norm': 'new_m', 'new_m_w_out': 'new_m', 'new_v_norm_in': 'new_v', 'new_v_w_in': 'new_v', 'new_v_a_q_norm': 'new_v', 'new_v_a_k_norm': 'new_v', 'new_v_b_cq_norm': 'new_v', 'new_v_b_ckv_norm': 'new_v', 'new_v_w_uq': 'new_v', 'new_v_w_ukv': 'new_v', 'new_v_b_q_norm': 'new_v', 'new_v_b_k_norm': 'new_v', 'new_v_w_out': 'new_v'}


def _forward(args):
    return _fwd_reference(*[args[k] for k in FWD_PARAMS])


def _output_shape():
    def fwd():
        inp = _fwd_setup_inputs(0)
        return _fwd_reference(*[inp[k] for k in FWD_PARAMS])
    out = _jax.eval_shape(fwd)
    return out.shape, out.dtype

N_MICROBATCH = 1
ADAM_LR = 0.001
ADAM_B1 = 0.9
ADAM_B2 = 0.999
ADAM_EPS = 1e-08
ADAM_WD = 0.01
ADAM_STEP = 10
PER_EXAMPLE_BATCH_AXIS = {'x': 0, 'loss_target': 0}
SHARED_INPUTS = []
_WEIGHT_DTYPES = {'norm_in': _jnp.float32, 'w_in': _jnp.float32, 'a_q_norm': _jnp.float32, 'a_k_norm': _jnp.float32, 'b_cq_norm': _jnp.float32, 'b_ckv_norm': _jnp.float32, 'w_uq': _jnp.float32, 'w_ukv': _jnp.float32, 'b_q_norm': _jnp.float32, 'b_k_norm': _jnp.float32, 'w_out': _jnp.float32}
MOMENT_SCALE = {'norm_in': 8.660223e-02, 'w_in': 3.017942e-02, 'a_q_norm': 2.879155e-01, 'a_k_norm': 2.896899e-01, 'b_cq_norm': 2.813938e-02, 'b_ckv_norm': 1.260592e-01, 'w_uq': 2.842769e-02, 'w_ukv': 2.767968e-02, 'b_q_norm': 2.601863e-01, 'b_k_norm': 2.604992e-01, 'w_out': 2.350466e-02}


def _to_microbatches(a, axis):
    t = _jnp.moveaxis(a, axis, 0)
    t = t.reshape((N_MICROBATCH, t.shape[0] // N_MICROBATCH) + t.shape[1:])
    return _jnp.moveaxis(t, 1, axis + 1)


def setup_inputs(seed: int = 0) -> dict:
    inp = _fwd_setup_inputs(seed)
    key = _jax.random.fold_in(_jax.random.key(seed), 7919)
    shape, _ = _output_shape()
    out = dict(inp)
    out["loss_target"] = _jax.random.normal(_jax.random.fold_in(key, 0), shape, _jnp.float32)
    for i, name in enumerate(TWIN_WEIGHTS):
        w = inp[name].astype(_jnp.float32)
        if MOMENT_SCALE is None:
            s = _jnp.sqrt(_jnp.mean(_jnp.square(w)) + 1e-30)
        else:
            s = MOMENT_SCALE[name]
        km, kv = _jax.random.split(_jax.random.fold_in(key, i + 1))
        out[name] = w
        out["m_" + name] = s * _jax.random.normal(km, w.shape, _jnp.float32)
        out["v_" + name] = (s * s) * _jax.random.uniform(kv, w.shape, _jnp.float32, 0.5, 1.5)
    if N_MICROBATCH > 1:
        for name, axis in PER_EXAMPLE_BATCH_AXIS.items():
            out[name] = _to_microbatches(out[name], axis)
    return {'x': out['x'], 'norm_in': out['norm_in'], 'w_in': out['w_in'], 'a_q_norm': out['a_q_norm'], 'a_k_norm': out['a_k_norm'], 'b_cq_norm': out['b_cq_norm'], 'b_ckv_norm': out['b_ckv_norm'], 'w_uq': out['w_uq'], 'w_ukv': out['w_ukv'], 'b_q_norm': out['b_q_norm'], 'b_k_norm': out['b_k_norm'], 'w_out': out['w_out'], 'loss_target': out['loss_target'], 'm_norm_in': out['m_norm_in'], 'm_w_in': out['m_w_in'], 'm_a_q_norm': out['m_a_q_norm'], 'm_a_k_norm': out['m_a_k_norm'], 'm_b_cq_norm': out['m_b_cq_norm'], 'm_b_ckv_norm': out['m_b_ckv_norm'], 'm_w_uq': out['m_w_uq'], 'm_w_ukv': out['m_w_ukv'], 'm_b_q_norm': out['m_b_q_norm'], 'm_b_k_norm': out['m_b_k_norm'], 'm_w_out': out['m_w_out'], 'v_norm_in': out['v_norm_in'], 'v_w_in': out['v_w_in'], 'v_a_q_norm': out['v_a_q_norm'], 'v_a_k_norm': out['v_a_k_norm'], 'v_b_cq_norm': out['v_b_cq_norm'], 'v_b_ckv_norm': out['v_b_ckv_norm'], 'v_w_uq': out['v_w_uq'], 'v_w_ukv': out['v_w_ukv'], 'v_b_q_norm': out['v_b_q_norm'], 'v_b_k_norm': out['v_b_k_norm'], 'v_w_out': out['v_w_out']}


def _loss(weights, diff, rest, loss_target):
    with _jax.named_scope("forward"):
        args = {**rest, TWIN_DIFF_INPUT: diff, **{k: w.astype(_WEIGHT_DTYPES[k]) for k, w in weights.items()}}
        y = _forward(args)
    with _jax.named_scope("loss_head"):
        err = _jnp.square(y.astype(_jnp.float32) - loss_target)
        return 0.5 * _jnp.sum(_jnp.mean(err, axis=-1)) if err.ndim else 0.5 * err


def _adamw(w, g, m, v):
    m = ADAM_B1 * m + (1.0 - ADAM_B1) * g
    v = ADAM_B2 * v + (1.0 - ADAM_B2) * _jnp.square(g)
    m_hat = m / (1.0 - ADAM_B1 ** ADAM_STEP)
    v_hat = v / (1.0 - ADAM_B2 ** ADAM_STEP)
    delta = -ADAM_LR * (m_hat / (_jnp.sqrt(v_hat) + ADAM_EPS) + ADAM_WD * w)
    return delta, m, v


def reference(x, norm_in, w_in, a_q_norm, a_k_norm, b_cq_norm, b_ckv_norm, w_uq, w_ukv, b_q_norm, b_k_norm, w_out, loss_target, m_norm_in, m_w_in, m_a_q_norm, m_a_k_norm, m_b_cq_norm, m_b_ckv_norm, m_w_uq, m_w_ukv, m_b_q_norm, m_b_k_norm, m_w_out, v_norm_in, v_w_in, v_a_q_norm, v_a_k_norm, v_b_cq_norm, v_b_ckv_norm, v_w_uq, v_w_ukv, v_b_q_norm, v_b_k_norm, v_w_out):
    given = dict(x=x, norm_in=norm_in, w_in=w_in, a_q_norm=a_q_norm, a_k_norm=a_k_norm, b_cq_norm=b_cq_norm, b_ckv_norm=b_ckv_norm, w_uq=w_uq, w_ukv=w_ukv, b_q_norm=b_q_norm, b_k_norm=b_k_norm, w_out=w_out, loss_target=loss_target, m_norm_in=m_norm_in, m_w_in=m_w_in, m_a_q_norm=m_a_q_norm, m_a_k_norm=m_a_k_norm, m_b_cq_norm=m_b_cq_norm, m_b_ckv_norm=m_b_ckv_norm, m_w_uq=m_w_uq, m_w_ukv=m_w_ukv, m_b_q_norm=m_b_q_norm, m_b_k_norm=m_b_k_norm, m_w_out=m_w_out, v_norm_in=v_norm_in, v_w_in=v_w_in, v_a_q_norm=v_a_q_norm, v_a_k_norm=v_a_k_norm, v_b_cq_norm=v_b_cq_norm, v_b_ckv_norm=v_b_ckv_norm, v_w_uq=v_w_uq, v_w_ukv=v_w_ukv, v_b_q_norm=v_b_q_norm, v_b_k_norm=v_b_k_norm, v_w_out=v_w_out)
    weights = {n: given[n] for n in TWIN_WEIGHTS}
    shared = {n: given[n] for n in SHARED_INPUTS}
    per_example = {n: given[n] for n in ['x']}
    grad_fn = _jax.value_and_grad(_loss, argnums=(0, 1))

    def one_microbatch(ex, loss_target):
        ex = dict(ex)
        diff = ex.pop(TWIN_DIFF_INPUT)
        return grad_fn(weights, diff, {**shared, **ex}, loss_target)

    if N_MICROBATCH == 1:
        loss, (grad_w, grad_x) = one_microbatch(per_example, given["loss_target"])
    else:
        def body(carry, xs):
            loss_sum, grad_sum = carry
            l_k, (gw_k, gx_k) = one_microbatch(xs[0], xs[1])
            with _jax.named_scope("update"):
                return (loss_sum + l_k, _jax.tree.map(_jnp.add, grad_sum, gw_k)), gx_k

        init = (_jnp.zeros((), _jnp.float32), _jax.tree.map(_jnp.zeros_like, weights))
        (loss, grad_w), grad_x = _jax.lax.scan(body, init, (per_example, given["loss_target"]))
    with _jax.named_scope("update"):
        delta_w, new_m, new_v = {}, {}, {}
        for n in TWIN_WEIGHTS:
            delta_w[n], new_m[n], new_v[n] = _adamw(weights[n], grad_w[n], given["m_" + n], given["v_" + n])
    return (loss, grad_x, *[grad_w[n] for n in TWIN_WEIGHTS], *[delta_w[n] for n in TWIN_WEIGHTS],
            *[new_m[n] for n in TWIN_WEIGHTS], *[new_v[n] for n in TWIN_WEIGHTS])
```

```python
import functools

import numpy as np
import jax
import jax.numpy as jnp
from jax import lax
from jax.experimental import pallas as pl
from jax.experimental.pallas import tpu as pltpu

F32 = jnp.float32
BF16 = jnp.bfloat16

D_MODEL = 1024
GRID_W = 64
ROPE_THETA = 10000.0
EPS = 1e-6
A_HEAD_DIM = 64
A_HEADS = 8
A_KV_HEADS = 2
B_HEADS = 4
B_NOPE_DIM = 64
B_ROPE_DIM = 32
B_QK_DIM = 96
B_V_DIM = 128
B_Q_RANK = 384
B_KV_RANK = 256
N_IN = 2464
N_DEV = 8

ADAM_LR = 0.001
ADAM_B1 = 0.9
ADAM_B2 = 0.999
ADAM_EPS = 1e-08
ADAM_WD = 0.01
ADAM_STEP = 10

LANES = 128
O_QA, O_KA, O_VA, O_GA, O_CQ, O_CKV, O_KR, O_GB, N_WIDE = 0, 512, 768, 1024, 1536, 1920, 2176, 2688, 3200

R_WIN = D_MODEL * (N_IN // N_DEV) // LANES
R_WUQ = (B_Q_RANK // N_DEV) * 384 // LANES
R_WUKV = B_KV_RANK * (768 // N_DEV) // LANES
R_WOUT = (D_MODEL // N_DEV) * D_MODEL // LANES
R_PACK = R_WIN + R_WUQ + R_WUKV + R_WOUT
R_SMALL = 24
ROW_LOSS = 17

VMEM_LIMIT = 56 * 1024 * 1024

NT = (((1,), (1,)), ((), ()))
TN = (((0,), (0,)), ((), ()))


def _dot(a, b):
    return jnp.dot(a, b, preferred_element_type=F32)


def _dot_nt(a, b):
    return lax.dot_general(a, b, NT, preferred_element_type=F32)


def _dot_tn(a, b):
    return lax.dot_general(a, b, TN, preferred_element_type=F32)


def _params(sem=None):
    return pltpu.CompilerParams(dimension_semantics=sem, vmem_limit_bytes=VMEM_LIMIT)


def _full(shape):
    nd = len(shape)
    return pl.BlockSpec(shape, lambda *_: (0,) * nd)


def _swap_sel(rows, shift):
    lane = lax.broadcasted_iota(jnp.int32, (rows, LANES), 1)
    return pltpu.roll(lane, shift, 1) == (lane ^ shift)


def _swap(x, shift, sel):
    return jnp.where(sel, pltpu.roll(x, shift, 1), pltpu.roll(x, LANES - shift, 1))


def _group_sum64(x, g64):
    hi = x.astype(BF16)
    lo = (x - hi.astype(F32)).astype(BF16)
    return _dot(hi, g64) + _dot(lo, g64)


def _row_sum(x):
    return jnp.sum(x, axis=-1, keepdims=True)


def _col_fwd(xs, msum, denom, gain, cos, sin, shift, sel):
    r = lax.rsqrt(msum(xs * xs) * (1.0 / denom) + EPS)
    xh = xs * r
    n = xh * gain
    return n * cos + _swap(n, shift, sel) * sin, xh, r


def _col_bwd(d_out, xh, r, msum, denom, gain, cos, sin, shift, sel):
    dn = d_out * cos + _swap(d_out * sin, shift, sel)
    dgain = jnp.sum(dn * xh, axis=0, keepdims=True)
    dxh = dn * gain
    dx = r * (dxh - xh * (msum(dxh * xh) * (1.0 / denom)))
    return dx, dgain


def _rms_fwd(x, gain):
    r = lax.rsqrt(jnp.mean(x * x, axis=-1, keepdims=True) + EPS)
    xh = x * r
    return xh * gain, xh, r


def _rms_bwd(dy, xh, r, gain):
    dgain = jnp.sum(dy * xh, axis=0, keepdims=True)
    dxh = dy * gain
    dx = r * (dxh - xh * jnp.mean(dxh * xh, axis=-1, keepdims=True))
    return dx, dgain


def _pre_forward(x, gin, w, wuq, wuk, wuv, gq, gk, gcq, gckv, gqb, gkb, ca, sa, cb, sb, g64, tm):
    sel16 = _swap_sel(tm, 16)
    sel8 = _swap_sel(tm, 8)
    xn, xh0, r0 = _rms_fwd(x, gin)
    xnb = xn.astype(BF16)
    proj = _dot(xnb, w)
    gs64 = functools.partial(_group_sum64, g64=g64)
    qa = [_col_fwd(proj[:, O_QA + LANES * s:O_QA + LANES * (s + 1)], gs64, 64.0, gq, ca, sa, 16, sel16)
          for s in range(4)]
    ka = [_col_fwd(proj[:, O_KA + LANES * s:O_KA + LANES * (s + 1)], _row_sum, 128.0, gk, ca, sa, 16, sel16)
          for s in range(2)]
    cq, cqh, rcq = _rms_fwd(proj[:, O_CQ:O_CQ + B_Q_RANK], gcq)
    cqb = cq.astype(BF16)
    qb_raw = _dot(cqb, wuq)
    qb = [_col_fwd(qb_raw[:, LANES * h:LANES * (h + 1)], _row_sum, float(B_QK_DIM), gqb, cb, sb, 8, sel8)
          for h in range(B_HEADS)]
    ckv, ckvh, rckv = _rms_fwd(proj[:, O_CKV:O_CKV + B_KV_RANK], gckv)
    ckvb = ckv.astype(BF16)
    kb_raw = _dot(ckvb, wuk) + proj[:, O_KR:O_KR + 512]
    vb = _dot(ckvb, wuv)
    kb = [_col_fwd(kb_raw[:, LANES * h:LANES * (h + 1)], _row_sum, float(B_QK_DIM), gkb, cb, sb, 8, sel8)
          for h in range(B_HEADS)]
    return dict(xh0=xh0, r0=r0, xnb=xnb, proj=proj, qa=qa, ka=ka, cqh=cqh, rcq=rcq, cqb=cqb, qb=qb,
                ckvh=ckvh, rckv=rckv, ckvb=ckvb, kb=kb, vb=vb, sel16=sel16, sel8=sel8, gs64=gs64)


_PRE_IN_NAMES = ("gin", "w", "wuq", "wuk", "wuv", "gq", "gk", "gcq", "gckv", "gqb", "gkb", "g64")


def _pre_const_specs(consts):
    return [_full(consts[n].shape) for n in _PRE_IN_NAMES]


def _pre_fwd_call(x, consts, tabs, tm):
    s_len = x.shape[0]

    def body(x_ref, ca_ref, sa_ref, cb_ref, sb_ref, gin_ref, w_ref, wuq_ref, wuk_ref, wuv_ref, gq_ref, gk_ref,
             gcq_ref, gckv_ref, gqb_ref, gkb_ref, g64_ref,
             qa_ref, ka_ref, va_ref, ga_ref, qb_ref, kb_ref, vb_ref, gb_ref):
        f = _pre_forward(x_ref[...], gin_ref[...], w_ref[...], wuq_ref[...], wuk_ref[...], wuv_ref[...],
                         gq_ref[...], gk_ref[...], gcq_ref[...], gckv_ref[...], gqb_ref[...], gkb_ref[...],
                         ca_ref[...], sa_ref[...], cb_ref[...], sb_ref[...], g64_ref[...], tm)
        proj = f["proj"]
        for s in range(4):
            qa_ref[:, LANES * s:LANES * (s + 1)] = (f["qa"][s][0] * 0.125).astype(BF16)
        for s in range(2):
            ka_ref[:, LANES * s:LANES * (s + 1)] = f["ka"][s][0].astype(BF16)
        va_ref[...] = proj[:, O_VA:O_VA + 256].astype(BF16)
        ga_ref[...] = proj[:, O_GA:O_GA + 512]
        for h in range(B_HEADS):
            qb_ref[:, LANES * h:LANES * (h + 1)] = f["qb"][h][0].astype(BF16)
            kb_ref[:, LANES * h:LANES * (h + 1)] = f["kb"][h][0].astype(BF16)
        vb_ref[...] = f["vb"].astype(BF16)
        gb_ref[...] = proj[:, O_GB:O_GB + 512]

    def rows(width):
        return pl.BlockSpec((tm, width), lambda i: (i, 0))

    outs = [(512, BF16), (256, BF16), (256, BF16), (512, F32), (512, BF16), (512, BF16), (512, BF16), (512, F32)]
    return pl.pallas_call(
        body, name="pre_fwd", grid=(s_len // tm,),
        in_specs=[rows(D_MODEL)] + [rows(LANES)] * 4 + _pre_const_specs(consts),
        out_specs=[rows(wd) for wd, _ in outs],
        out_shape=[jax.ShapeDtypeStruct((s_len, wd), dt) for wd, dt in outs],
        compiler_params=_params(("arbitrary",)),
    )(x, *tabs, *[consts[n] for n in _PRE_IN_NAMES])


def _head_masks(rows):
    lane = lax.broadcasted_iota(jnp.int32, (rows, LANES), 1)
    return lane < 64, lane >= 64


def _attn_fwd_call(q, k, v, *, groups, sub, scale, tq, tk, name):
    s_len = q.shape[0]
    masked = sub > 1
    qw = LANES * (sub // 2 if masked else 1)
    n_k = s_len // tk

    def body(q_ref, k_ref, v_ref, o_ref, lse_ref, m_sc, l_sc, acc_sc):
        j = pl.program_id(2)

        @pl.when(j == 0)
        def _():
            m_sc[...] = jnp.full(m_sc.shape, -jnp.inf, F32)
            l_sc[...] = jnp.zeros(l_sc.shape, F32)
            acc_sc[...] = jnp.zeros(acc_sc.shape, F32)

        kk = k_ref[...]
        vv = v_ref[...]
        keep = _head_masks(tq) if masked else None
        for hh in range(sub):
            if masked:
                qp = q_ref[:, LANES * (hh // 2):LANES * (hh // 2 + 1)]
                qm = jnp.where(keep[hh % 2], qp, jnp.zeros_like(qp))
            else:
                qm = q_ref[...]
            s = _dot_nt(qm, kk)
            if scale is not None:
                s = s * scale
            m_old = m_sc[hh]
            m_new = jnp.maximum(m_old, jnp.max(s, axis=1, keepdims=True))
            alpha = jnp.exp(m_old - m_new)
            p = jnp.exp(s - m_new)
            l_sc[hh] = alpha * l_sc[hh] + jnp.sum(p, axis=1, keepdims=True)
            acc_sc[hh] = alpha * acc_sc[hh] + _dot(p.astype(BF16), vv)
            m_sc[hh] = m_new

        @pl.when(j == n_k - 1)
        def _():
            outs = []
            for hh in range(sub):
                l = l_sc[hh]
                outs.append(acc_sc[hh] / l)
                lse_ref[hh] = jnp.broadcast_to(m_sc[hh] + jnp.log(l), (tq, LANES))
            if masked:
                for pr in range(sub // 2):
                    o_ref[:, LANES * pr:LANES * (pr + 1)] = jnp.where(keep[0], outs[2 * pr], outs[2 * pr + 1])
            else:
                o_ref[...] = outs[0]

    return pl.pallas_call(
        body, name=name, grid=(groups, s_len // tq, n_k),
        in_specs=[pl.BlockSpec((tq, qw), lambda g, i, j: (i, g)),
                  pl.BlockSpec((tk, LANES), lambda g, i, j: (j, g)),
                  pl.BlockSpec((tk, LANES), lambda g, i, j: (j, g))],
        out_specs=[pl.BlockSpec((tq, qw), lambda g, i, j: (i, g)),
                   pl.BlockSpec((sub, tq, LANES), lambda g, i, j: (g, i, 0))],
        out_shape=[jax.ShapeDtypeStruct((s_len, groups * qw), F32),
                   jax.ShapeDtypeStruct((groups * sub, s_len, LANES), F32)],
        scratch_shapes=[pltpu.VMEM((sub, tq, 1), F32), pltpu.VMEM((sub, tq, 1), F32),
                        pltpu.VMEM((sub, tq, LANES), F32)],
        compiler_params=_params(("arbitrary", "arbitrary", "arbitrary")),
    )(q, k, v)


def _attn_bwd_call(q, k, v, do, lse_t, delta_t, *, groups, sub, scale, tq, tk, name):
    s_len = q.shape[0]
    masked = sub > 1
    qw = LANES * (sub // 2 if masked else 1)
    n_k = s_len // tk

    def body(q_ref, k_ref, v_ref, do_ref, lse_ref, dl_ref, dq_ref, dk_ref, dv_ref, dq_sc):
        i = pl.program_id(1)
        j = pl.program_id(2)

        @pl.when((i == 0) & (j == 0))
        def _():
            dk_ref[...] = jnp.zeros(dk_ref.shape, F32)
            dv_ref[...] = jnp.zeros(dv_ref.shape, F32)

        @pl.when(j == 0)
        def _():
            dq_sc[...] = jnp.zeros(dq_sc.shape, F32)

        kk = k_ref[...]
        vv = v_ref[...]
        keep = _head_masks(tq) if masked else None
        dk_t = jnp.zeros((tk, LANES), F32)
        dv_t = jnp.zeros((tk, LANES), F32)
        for hh in range(sub):
            if masked:
                cols = slice(LANES * (hh // 2), LANES * (hh // 2 + 1))
                qp = q_ref[:, cols]
                dop = do_ref[:, cols]
                qm = jnp.where(keep[hh % 2], qp, jnp.zeros_like(qp))
                dom = jnp.where(keep[hh % 2], dop, jnp.zeros_like(dop))
            else:
                cols = slice(0, LANES)
                qm = q_ref[...]
                dom = do_ref[...]
            s_t = _dot_nt(kk, qm)
            if scale is not None:
                s_t = s_t * scale
            p_t = jnp.exp(s_t - lse_ref[hh])
            dp_t = _dot_nt(vv, dom)
            ds_t = p_t * (dp_t - dl_ref[hh])
            if scale is not None:
                ds_t = ds_t * scale
            p_b = p_t.astype(BF16)
            ds_b = ds_t.astype(BF16)
            dv_t = dv_t + _dot(p_b, dom)
            dk_t = dk_t + _dot(ds_b, qm)
            dq_h = _dot_tn(ds_b, kk)
            if masked:
                dq_h = jnp.where(keep[hh % 2], dq_h, jnp.zeros_like(dq_h))
            dq_sc[:, cols] += dq_h
        rows = pl.ds(pl.multiple_of(j * tk, tk), tk)
        dk_ref[rows, :] += dk_t
        dv_ref[rows, :] += dv_t

        @pl.when(j == n_k - 1)
        def _():
            dq_ref[...] = dq_sc[...]

    return pl.pallas_call(
        body, name=name, grid=(groups, s_len // tq, n_k),
        in_specs=[pl.BlockSpec((tq, qw), lambda g, i, j: (i, g)),
                  pl.BlockSpec((tk, LANES), lambda g, i, j: (j, g)),
                  pl.BlockSpec((tk, LANES), lambda g, i, j: (j, g)),
                  pl.BlockSpec((tq, qw), lambda g, i, j: (i, g)),
                  pl.BlockSpec((sub, 1, tq), lambda g, i, j: (g, 0, i)),
                  pl.BlockSpec((sub, 1, tq), lambda g, i, j: (g, 0, i))],
        out_specs=[pl.BlockSpec((tq, qw), lambda g, i, j: (i, g)),
                   pl.BlockSpec((s_len, LANES), lambda g, i, j: (0, g)),
                   pl.BlockSpec((s_len, LANES), lambda g, i, j: (0, g))],
        out_shape=[jax.ShapeDtypeStruct((s_len, groups * qw), F32),
                   jax.ShapeDtypeStruct((s_len, groups * LANES), F32),
                   jax.ShapeDtypeStruct((s_len, groups * LANES), F32)],
        scratch_shapes=[pltpu.VMEM((tq, qw), F32)],
        compiler_params=_params(("arbitrary", "arbitrary", "arbitrary")),
    )(q, k, v, do, lse_t, delta_t)


def _silu_parts(g):
    sig = 1.0 / (1.0 + jnp.exp(-g))
    return g * sig, sig * (1.0 + g * (1.0 - sig))


def _out_call(x, target, oa, ob, ga, gb, wout, g64, tm):
    s_len = x.shape[0]
    n_t = s_len // tm

    def body(x_ref, t_ref, oa_ref, ob_ref, ga_ref, gb_ref, w_ref, g64_ref,
             dh_ref, doa_ref, dob_ref, dga_ref, dgb_ref, dla_ref, dlb_ref, dw_ref, loss_ref):
        i = pl.program_id(0)

        @pl.when(i == 0)
        def _():
            dw_ref[...] = jnp.zeros(dw_ref.shape, F32)
            loss_ref[...] = jnp.zeros(loss_ref.shape, F32)

        oa_v, ob_v = oa_ref[...], ob_ref[...]
        silu_a, dsilu_a = _silu_parts(ga_ref[...])
        silu_b, dsilu_b = _silu_parts(gb_ref[...])
        ya = (oa_v * silu_a).astype(BF16)
        yb = (ob_v * silu_b).astype(BF16)
        h = x_ref[...] + _dot(ya, w_ref[0:512, :]) + _dot(yb, w_ref[512:1024, :])
        err = h - t_ref[...]
        part = jnp.sum(err * err, axis=0, keepdims=True)
        acc = part[:, 0:LANES]
        for c in range(1, D_MODEL // LANES):
            acc = acc + part[:, LANES * c:LANES * (c + 1)]
        loss_ref[...] += acc
        dh = err * (1.0 / D_MODEL)
        dh_ref[...] = dh
        dhb = dh.astype(BF16)
        dya = _dot_nt(dhb, w_ref[0:512, :])
        dyb = _dot_nt(dhb, w_ref[512:1024, :])
        doa = dya * silu_a
        dob = dyb * silu_b
        doa_ref[...] = doa.astype(BF16)
        dob_ref[...] = dob.astype(BF16)
        dga_ref[...] = dya * oa_v * dsilu_a
        dgb_ref[...] = dyb * ob_v * dsilu_b
        ta = doa * oa_v
        tb = dob * ob_v
        g64 = g64_ref[...]
        for s in range(4):
            dla_ref[:, LANES * s:LANES * (s + 1)] = _group_sum64(ta[:, LANES * s:LANES * (s + 1)], g64)
            dlb_ref[:, LANES * s:LANES * (s + 1)] = jnp.broadcast_to(
                _row_sum(tb[:, LANES * s:LANES * (s + 1)]), (tm, LANES))
        dw_ref[0:512, :] += _dot_tn(ya, dhb)
        dw_ref[512:1024, :] += _dot_tn(yb, dhb)

    def rows(width):
        return pl.BlockSpec((tm, width), lambda i: (i, 0))

    outs = [(D_MODEL, F32), (512, BF16), (512, BF16), (512, F32), (512, F32), (512, F32), (512, F32)]
    return pl.pallas_call(
        body, name="out_fwd", grid=(n_t,),
        in_specs=[rows(D_MODEL), rows(D_MODEL), rows(512), rows(512), rows(512), rows(512),
                  _full((D_MODEL, D_MODEL)), _full((LANES, LANES))],
        out_specs=[rows(wd) for wd, _ in outs] + [_full((D_MODEL, D_MODEL)), _full((1, LANES))],
        out_shape=[jax.ShapeDtypeStruct((s_len, wd), dt) for wd, dt in outs]
        + [jax.ShapeDtypeStruct((D_MODEL, D_MODEL), F32), jax.ShapeDtypeStruct((1, LANES), F32)],
        compiler_params=_params(("arbitrary",)),
    )(x, target, oa, ob, ga, gb, wout, g64)


def _pre_bwd_call(x, dh, dqa, dka, dva, dga, dqb, dkb, dvb, dgb, consts, tabs, tm):
    s_len = x.shape[0]

    def body(x_ref, dh_ref, dqa_ref, dka_ref, dva_ref, dga_ref, dqb_ref, dkb_ref, dvb_ref, dgb_ref,
             ca_ref, sa_ref, cb_ref, sb_ref, gin_ref, w_ref, wuq_ref, wuk_ref, wuv_ref, gq_ref, gk_ref,
             gcq_ref, gckv_ref, gqb_ref, gkb_ref, g64_ref,
             dx_ref, dproj_ref, xnb_ref, dwuq_ref, dwuk_ref, dwuv_ref, small_ref):
        i = pl.program_id(0)

        @pl.when(i == 0)
        def _():
            dwuq_ref[...] = jnp.zeros(dwuq_ref.shape, F32)
            dwuk_ref[...] = jnp.zeros(dwuk_ref.shape, F32)
            dwuv_ref[...] = jnp.zeros(dwuv_ref.shape, F32)
            small_ref[...] = jnp.zeros(small_ref.shape, F32)

        gin, gq, gk = gin_ref[...], gq_ref[...], gk_ref[...]
        gcq, gckv, gqb, gkb = gcq_ref[...], gckv_ref[...], gqb_ref[...], gkb_ref[...]
        ca, sa, cb, sb = ca_ref[...], sa_ref[...], cb_ref[...], sb_ref[...]
        w, wuq, wuk, wuv = w_ref[...], wuq_ref[...], wuk_ref[...], wuv_ref[...]
        f = _pre_forward(x_ref[...], gin, w, wuq, wuk, wuv, gq, gk, gcq, gckv, gqb, gkb,
                         ca, sa, cb, sb, g64_ref[...], tm)
        sel16, sel8, gs64 = f["sel16"], f["sel8"], f["gs64"]
        lane = lax.broadcasted_iota(jnp.int32, (tm, LANES), 1)
        low = lane < 64
        zero = jnp.zeros((tm, LANES), F32)
        pieces = []

        dgq = jnp.zeros((1, LANES), F32)
        for s in range(4):
            _, xh, r = f["qa"][s]
            d = dqa_ref[:, LANES * s:LANES * (s + 1)] * 0.125
            dx, dg = _col_bwd(d, xh, r, gs64, 64.0, gq, ca, sa, 16, sel16)
            pieces.append(dx)
            dgq = dgq + dg
        dgk = jnp.zeros((1, LANES), F32)
        for s in range(2):
            _, xh, r = f["ka"][s]
            d = dka_ref[:, LANES * s:LANES * (s + 1)]
            d = d + pltpu.roll(d, 64, 1)
            dx, dg = _col_bwd(d, xh, r, _row_sum, 128.0, gk, ca, sa, 16, sel16)
            pieces.append(jnp.where(low, dx, zero))
            dgk = dgk + dg
        for s in range(2):
            d = dva_ref[:, LANES * s:LANES * (s + 1)]
            d = d + pltpu.roll(d, 64, 1)
            pieces.append(jnp.where(low, d, zero))
        pieces.append(dga_ref[...])

        dgqb = jnp.zeros((1, LANES), F32)
        dq_cols = []
        for h in range(B_HEADS):
            _, xh, r = f["qb"][h]
            dx, dg = _col_bwd(dqb_ref[:, LANES * h:LANES * (h + 1)], xh, r, _row_sum, float(B_QK_DIM),
                              gqb, cb, sb, 8, sel8)
            dq_cols.append(dx)
            dgqb = dgqb + dg
        dqr_b = jnp.concatenate(dq_cols, axis=1).astype(BF16)
        dwuq_ref[...] += _dot_tn(f["cqb"], dqr_b)
        dcq_raw, dgcq = _rms_bwd(_dot_nt(dqr_b, wuq), f["cqh"], f["rcq"], gcq)
        pieces.append(dcq_raw)

        dgkb = jnp.zeros((1, LANES), F32)
        dk_cols = []
        dkr = zero
        for h in range(B_HEADS):
            _, xh, r = f["kb"][h]
            dx, dg = _col_bwd(dkb_ref[:, LANES * h:LANES * (h + 1)], xh, r, _row_sum, float(B_QK_DIM),
                              gkb, cb, sb, 8, sel8)
            dk_cols.append(dx)
            dkr = dkr + dx
            dgkb = dgkb + dg
        dkr_b = jnp.concatenate(dk_cols, axis=1).astype(BF16)
        dvb_b = dvb_ref[...].astype(BF16)
        dwuk_ref[...] += _dot_tn(f["ckvb"], dkr_b)
        dwuv_ref[...] += _dot_tn(f["ckvb"], dvb_b)
        dckv = _dot_nt(dkr_b, wuk) + _dot_nt(dvb_b, wuv)
        dckv_raw, dgckv = _rms_bwd(dckv, f["ckvh"], f["rckv"], gckv)
        pieces.append(dckv_raw)
        pieces.append(jnp.where((lane >= B_NOPE_DIM) & (lane < B_QK_DIM), dkr, zero))
        pieces += [zero, zero, zero]
        pieces.append(dgb_ref[...])

        dproj_b = jnp.concatenate(pieces, axis=1).astype(BF16)
        dproj_ref[...] = dproj_b
        xnb_ref[...] = f["xnb"]
        dxn = _dot_nt(dproj_b, w)
        dx, dgin = _rms_bwd(dxn, f["xh0"], f["r0"], gin)
        dx_ref[...] = dx + dh_ref[...]

        for c in range(D_MODEL // LANES):
            small_ref[c:c + 1, :] += dgin[:, LANES * c:LANES * (c + 1)]
        small_ref[8:9, :] += dgq
        small_ref[9:10, :] += dgk
        for c in range(3):
            small_ref[10 + c:11 + c, :] += dgcq[:, LANES * c:LANES * (c + 1)]
        for c in range(2):
            small_ref[13 + c:14 + c, :] += dgckv[:, LANES * c:LANES * (c + 1)]
        small_ref[15:16, :] += dgqb
        small_ref[16:17, :] += dgkb

    def rows(width):
        return pl.BlockSpec((tm, width), lambda i: (i, 0))

    return pl.pallas_call(
        body, name="pre_bwd", grid=(s_len // tm,),
        in_specs=[rows(D_MODEL), rows(D_MODEL), rows(512), rows(256), rows(256), rows(512), rows(512), rows(512),
                  rows(512), rows(512)] + [rows(LANES)] * 4 + _pre_const_specs(consts),
        out_specs=[rows(D_MODEL), rows(N_WIDE), rows(D_MODEL), _full((B_Q_RANK, 512)), _full((B_KV_RANK, 512)),
                   _full((B_KV_RANK, 512)), _full((R_SMALL, LANES))],
        out_shape=[jax.ShapeDtypeStruct((s_len, D_MODEL), F32), jax.ShapeDtypeStruct((s_len, N_WIDE), BF16),
                   jax.ShapeDtypeStruct((s_len, D_MODEL), BF16), jax.ShapeDtypeStruct((B_Q_RANK, 512), F32),
                   jax.ShapeDtypeStruct((B_KV_RANK, 512), F32), jax.ShapeDtypeStruct((B_KV_RANK, 512), F32),
                   jax.ShapeDtypeStruct((R_SMALL, LANES), F32)],
        compiler_params=_params(("arbitrary",)),
    )(x, dh, dqa, dka, dva, dga, dqb, dkb, dvb, dgb, *tabs, *[consts[n] for n in _PRE_IN_NAMES])


def _dw_in_call(xnb, dproj_b, tt, tn):
    s_len = xnb.shape[0]

    def body(a_ref, b_ref, o_ref):
        @pl.when(pl.program_id(1) == 0)
        def _():
            o_ref[...] = jnp.zeros(o_ref.shape, F32)

        o_ref[...] += _dot_tn(a_ref[...], b_ref[...])

    return pl.pallas_call(
        body, name="dw_in", grid=(N_WIDE // tn, s_len // tt),
        in_specs=[pl.BlockSpec((tt, D_MODEL), lambda n, t: (t, 0)), pl.BlockSpec((tt, tn), lambda n, t: (t, n))],
        out_specs=pl.BlockSpec((D_MODEL, tn), lambda n, t: (0, n)),
        out_shape=jax.ShapeDtypeStruct((D_MODEL, N_WIDE), F32),
        compiler_params=_params(("arbitrary", "arbitrary")),
    )(xnb, dproj_b)


def _mesh_pos():
    return lax.axis_index("x"), lax.axis_index("y"), lax.axis_index("c")


def _flip(v, bit):
    return 1 - v if bit else v


def _peer(pos, k):
    x, y, c = pos
    return _flip(x, (k >> 2) & 1), _flip(y, (k >> 1) & 1), _flip(c, k & 1)


def _logical(p):
    return 4 * p[0] + 2 * p[1] + p[2]


def _gather_weights_call(shard):
    m_per = shard.shape[0]

    def body(x_ref, out_ref, xb_ref, send_sems, recv_sems, local_sem):
        x, y, c = _mesh_pos()
        me, sibling = (x, y, c), (x, y, 1 - c)
        chips = [(1 - x, y), (x, 1 - y), (1 - x, 1 - y)]
        xb_ref[...] = x_ref[...].astype(BF16)

        def rows(p):
            return out_ref.at[pl.ds(pl.multiple_of(_logical(p) * m_per, 16), m_per), :]

        def copy(k, block, to, src=None):
            return pltpu.make_async_remote_copy(
                src_ref=rows(block) if src is None else src, dst_ref=rows(block),
                send_sem=send_sems.at[k], recv_sem=recv_sems.at[k],
                device_id=to, device_id_type=pl.DeviceIdType.MESH)

        mine = pltpu.make_async_copy(xb_ref, rows(me), local_sem)
        mine.start()
        first = [copy(0, me, sibling, src=xb_ref)]
        first += [copy(1 + j, me, (*chip, c), src=xb_ref) for j, chip in enumerate(chips)]
        for cp in first:
            cp.start()
        passed = [copy(4 + j, (*chip, c), sibling) for j, chip in enumerate(chips)]
        for j, chip in enumerate(chips):
            copy(1 + j, (*chip, c), me).wait_recv()
            passed[j].start()
        copy(0, sibling, me).wait_recv()
        for j, chip in enumerate(chips):
            copy(4 + j, (*chip, 1 - c), me).wait_recv()
        for cp in first + passed:
            cp.wait_send()
        mine.wait()

    return pl.pallas_call(
        body, name="gather_weights",
        out_shape=jax.ShapeDtypeStruct((N_DEV * m_per, LANES), BF16),
        in_specs=[pl.BlockSpec(memory_space=pltpu.VMEM)],
        out_specs=pl.BlockSpec(memory_space=pltpu.VMEM),
        scratch_shapes=[pltpu.VMEM((m_per, LANES), BF16), pltpu.SemaphoreType.DMA((7,)),
                        pltpu.SemaphoreType.DMA((7,)), pltpu.SemaphoreType.DMA],
        compiler_params=pltpu.CompilerParams(vmem_limit_bytes=VMEM_LIMIT),
    )(shard)


def _adamw(w, g, m, v):
    m = ADAM_B1 * m + (1.0 - ADAM_B1) * g
    v = ADAM_B2 * v + (1.0 - ADAM_B2) * (g * g)
    m_hat = m / (1.0 - ADAM_B1 ** ADAM_STEP)
    v_hat = v / (1.0 - ADAM_B2 ** ADAM_STEP)
    delta = -ADAM_LR * (m_hat / (jnp.sqrt(v_hat) + ADAM_EPS) + ADAM_WD * w)
    return delta, m, v


def _reduce_adamw_call(parts, small, w_pk, m_pk, v_pk, w_s, m_s, v_s):
    chunk = 16
    n_chunks = R_PACK // chunk

    def body(parts_ref, small_ref, w_ref, m_ref, v_ref, ws_ref, ms_ref, vs_ref,
             g_ref, d_ref, nm_ref, nv_ref, gs_ref, ds_ref, nms_ref, nvs_ref,
             recv_ref, recv_s_ref, send_sems, recv_sems, send_s_sems, recv_s_sems, local_sem):
        pos = _mesh_pos()
        me = _logical(pos)

        def big(k):
            peer = _peer(pos, k)
            return pltpu.make_async_remote_copy(
                src_ref=parts_ref.at[_logical(peer)], dst_ref=recv_ref.at[k],
                send_sem=send_sems.at[k], recv_sem=recv_sems.at[k],
                device_id=peer, device_id_type=pl.DeviceIdType.MESH)

        def tiny(k):
            return pltpu.make_async_remote_copy(
                src_ref=small_ref, dst_ref=recv_s_ref.at[k],
                send_sem=send_s_sems.at[k], recv_sem=recv_s_sems.at[k],
                device_id=_peer(pos, k), device_id_type=pl.DeviceIdType.MESH)

        own = pltpu.make_async_copy(parts_ref.at[me], recv_ref.at[0], local_sem)
        own.start()
        for k in range(1, N_DEV):
            tiny(k).start()
        for k in range(1, N_DEV):
            big(k).start()
        recv_s_ref[0] = small_ref[...]
        for k in range(1, N_DEV):
            tiny(k).wait_recv()
        acc = recv_s_ref[me]
        for a in range(1, N_DEV):
            acc = acc + recv_s_ref[lax.bitwise_xor(me, a)]
        row = lax.broadcasted_iota(jnp.int32, (R_SMALL, LANES), 0)
        gs = jnp.where(row == 8, acc + pltpu.roll(acc, 64, 1), acc)
        gs = jnp.where(row == ROW_LOSS, jnp.sum(acc, axis=1, keepdims=True) * (0.5 / D_MODEL), gs)
        gs_ref[...] = gs
        ds, nms, nvs = _adamw(ws_ref[...], gs, ms_ref[...], vs_ref[...])
        ds_ref[...] = ds
        nms_ref[...] = nms
        nvs_ref[...] = nvs

        own.wait()
        for k in range(1, N_DEV):
            big(k).wait_recv()

        def step(t, carry):
            rows = pl.ds(pl.multiple_of(t * chunk, chunk), chunk)
            g = recv_ref[0, rows, :]
            for k in range(1, N_DEV):
                g = g + recv_ref[k, rows, :]
            d, nm, nv = _adamw(w_ref[rows, :], g, m_ref[rows, :], v_ref[rows, :])
            g_ref[rows, :] = g
            d_ref[rows, :] = d
            nm_ref[rows, :] = nm
            nv_ref[rows, :] = nv
            return carry

        lax.fori_loop(0, n_chunks, step, 0)
        for k in range(1, N_DEV):
            tiny(k).wait_send()
            big(k).wait_send()

    vm = pl.BlockSpec(memory_space=pltpu.VMEM)
    big_shape = jax.ShapeDtypeStruct((R_PACK, LANES), F32)
    small_shape = jax.ShapeDtypeStruct((R_SMALL, LANES), F32)
    return pl.pallas_call(
        body, name="reduce_adamw",
        in_specs=[pl.BlockSpec(memory_space=pl.ANY)] + [vm] * 7,
        out_specs=[vm] * 8,
        out_shape=[big_shape] * 4 + [small_shape] * 4,
        scratch_shapes=[pltpu.VMEM((N_DEV, R_PACK, LANES), F32), pltpu.VMEM((N_DEV, R_SMALL, LANES), F32),
                        pltpu.SemaphoreType.DMA((N_DEV,)), pltpu.SemaphoreType.DMA((N_DEV,)),
                        pltpu.SemaphoreType.DMA((N_DEV,)), pltpu.SemaphoreType.DMA((N_DEV,)),
                        pltpu.SemaphoreType.DMA],
        compiler_params=pltpu.CompilerParams(vmem_limit_bytes=VMEM_LIMIT),
    )(parts, small, w_pk, m_pk, v_pk, w_s, m_s, v_s)


def _pack_shard(w_in, w_uq, w_ukv, w_out):
    return jnp.concatenate([a.reshape(-1, LANES) for a in (w_in, w_uq, w_ukv, w_out)], axis=0)


def _unpack_shard(p):
    w_in = p[0:R_WIN].reshape(1, D_MODEL, N_IN // N_DEV)
    w_uq = p[R_WIN:R_WIN + R_WUQ].reshape(1, B_Q_RANK // N_DEV, 384)
    w_ukv = p[R_WIN + R_WUQ:R_WIN + R_WUQ + R_WUKV].reshape(1, B_KV_RANK, 768 // N_DEV)
    w_out = p[R_WIN + R_WUQ + R_WUKV:].reshape(1, D_MODEL // N_DEV, D_MODEL)
    return w_in, w_uq, w_ukv, w_out


def _pack_small(norm_in, a_q, a_k, b_cq, b_ckv, b_q, b_k):
    def row(v):
        return jnp.pad(v.reshape(1, -1), ((0, 0), (0, LANES - v.size)))
    rows = [norm_in.reshape(8, LANES), row(a_q), row(a_k), b_cq.reshape(3, LANES), b_ckv.reshape(2, LANES),
            row(b_q), row(b_k), jnp.zeros((R_SMALL - 17, LANES), F32)]
    return jnp.concatenate(rows, axis=0)


def _unpack_small(s):
    return (s[0:8].reshape(1, D_MODEL), s[8:9, :64], s[9:10, :64], s[10:13].reshape(1, B_Q_RANK),
            s[13:15].reshape(1, B_KV_RANK), s[15:16, :B_QK_DIM], s[16:17, :B_QK_DIM])


def _full_weights(gathered):
    g = gathered
    w_in = g[:, 0:R_WIN].reshape(N_DEV, D_MODEL, N_IN // N_DEV).transpose(1, 0, 2).reshape(D_MODEL, N_IN)
    w_uq = g[:, R_WIN:R_WIN + R_WUQ].reshape(B_Q_RANK, 384)
    w_ukv = g[:, R_WIN + R_WUQ:R_WIN + R_WUQ + R_WUKV].reshape(N_DEV, B_KV_RANK, 768 // N_DEV)
    w_ukv = w_ukv.transpose(1, 0, 2).reshape(B_KV_RANK, 768)
    w_out = g[:, R_WIN + R_WUQ + R_WUKV:].reshape(D_MODEL, D_MODEL)
    k0, k1 = w_in[:, 512:576], w_in[:, 576:640]
    v0, v1 = w_in[:, 640:704], w_in[:, 704:768]
    kr = w_in[:, 1920:1952]
    z64 = jnp.zeros((D_MODEL, 64), BF16)
    z32 = jnp.zeros((D_MODEL, 32), BF16)
    kr_blk = jnp.concatenate([z64, kr, z32], axis=1)
    w_wide = jnp.concatenate([w_in[:, 0:512], k0, k0, k1, k1, v0, v0, v1, v1, w_in[:, 768:1280], w_in[:, 1280:1664],
                              w_in[:, 1664:1920], kr_blk, kr_blk, kr_blk, kr_blk, w_in[:, 1952:2464]], axis=1)
    wuq = jnp.pad(w_uq.reshape(B_Q_RANK, B_HEADS, B_QK_DIM), ((0, 0), (0, 0), (0, LANES - B_QK_DIM)))
    wuq = wuq.reshape(B_Q_RANK, 512)
    ukv = w_ukv.reshape(B_KV_RANK, B_HEADS, B_NOPE_DIM + B_V_DIM)
    wuk = jnp.pad(ukv[:, :, :B_NOPE_DIM], ((0, 0), (0, 0), (0, LANES - B_NOPE_DIM))).reshape(B_KV_RANK, 512)
    wuv = ukv[:, :, B_NOPE_DIM:].reshape(B_KV_RANK, 512)
    return w_wide, wuq, wuk, wuv, w_out


def _narrow_grads(dw_wide, dwuq, dwuk, dwuv):
    dw_in = jnp.concatenate([
        dw_wide[:, 0:512], dw_wide[:, 512:576], dw_wide[:, 640:704], dw_wide[:, 768:832], dw_wide[:, 896:960],
        dw_wide[:, 1024:1536], dw_wide[:, 1536:1920], dw_wide[:, 1920:2176],
        dw_wide[:, O_KR + 64:O_KR + 96], dw_wide[:, 2688:3200]], axis=1)
    dw_uq = dwuq.reshape(B_Q_RANK, B_HEADS, LANES)[:, :, :B_QK_DIM].reshape(B_Q_RANK, 384)
    dk = dwuk.reshape(B_KV_RANK, B_HEADS, LANES)[:, :, :B_NOPE_DIM]
    dv = dwuv.reshape(B_KV_RANK, B_HEADS, B_V_DIM)
    dw_ukv = jnp.concatenate([dk, dv], axis=2).reshape(B_KV_RANK, 768)
    return dw_in, dw_uq, dw_ukv


def _rope_tables(s_len):
    rows = s_len // GRID_W
    row = jnp.repeat(jnp.arange(rows, dtype=F32), GRID_W)
    col = jnp.tile(jnp.arange(GRID_W, dtype=F32), rows)

    def tables(dim):
        half = dim // 2
        inv = 1.0 / (ROPE_THETA ** (jnp.arange(0, half, 2, dtype=F32) / half))
        ar, ac = row[:, None] * inv[None, :], col[:, None] * inv[None, :]
        cos = jnp.concatenate([jnp.cos(ar), jnp.cos(ar), jnp.cos(ac), jnp.cos(ac)], axis=1)
        sin = jnp.concatenate([-jnp.sin(ar), jnp.sin(ar), -jnp.sin(ac), jnp.sin(ac)], axis=1)
        return cos, sin

    ca, sa = tables(A_HEAD_DIM)
    cb, sb = tables(B_ROPE_DIM)
    ca, sa = jnp.tile(ca, (1, 2)), jnp.tile(sa, (1, 2))
    cb = jnp.concatenate([jnp.ones((s_len, 64), F32), cb, jnp.ones((s_len, 32), F32)], axis=1)
    sb = jnp.concatenate([jnp.zeros((s_len, 64), F32), sb, jnp.zeros((s_len, 32), F32)], axis=1)
    return ca, sa, cb, sb


def _lane_major(a, step):
    return a[:, ::step].T[:, None, :]


def kernel(x, norm_in, w_in, a_q_norm, a_k_norm, b_cq_norm, b_ckv_norm, w_uq, w_ukv, b_q_norm, b_k_norm, w_out, loss_target, m_norm_in, m_w_in, m_a_q_norm, m_a_k_norm, m_b_cq_norm, m_b_ckv_norm, m_w_uq, m_w_ukv, m_b_q_norm, m_b_k_norm, m_w_out, v_norm_in, v_w_in, v_a_q_norm, v_a_k_norm, v_b_cq_norm, v_b_ckv_norm, v_w_uq, v_w_ukv, v_b_q_norm, v_b_k_norm, v_w_out):
    s_len = x.shape[1]
    tm = min(256, s_len)
    tq = tk = min(512, s_len)
    x2 = x.reshape(s_len, D_MODEL)
    t2 = loss_target.reshape(s_len, D_MODEL)

    w_pk = _pack_shard(w_in[0], w_uq[0], w_ukv[0], w_out[0])
    gathered = _gather_weights_call(w_pk).reshape(N_DEV, R_PACK, LANES)
    w_wide, wuq, wuk, wuv, wout = _full_weights(gathered)

    def dup(v, pad_to=None):
        v = v.reshape(1, -1)
        if pad_to is None:
            return jnp.concatenate([v, v], axis=1)
        return jnp.pad(v, ((0, 0), (0, pad_to - v.shape[1])))

    g64 = jnp.asarray(np.kron(np.eye(2), np.ones((64, 64))), dtype=BF16)
    consts = dict(gin=norm_in, w=w_wide, wuq=wuq, wuk=wuk, wuv=wuv, gq=dup(a_q_norm), gk=dup(a_k_norm),
                  gcq=b_cq_norm, gckv=b_ckv_norm, gqb=dup(b_q_norm, LANES), gkb=dup(b_k_norm, LANES), g64=g64)
    tabs = _rope_tables(s_len)

    qa, ka, va, ga, qb, kb, vb, gb = _pre_fwd_call(x2, consts, tabs, tm)
    scale_b = 1.0 / float(np.sqrt(B_QK_DIM))
    oa, lse_a = _attn_fwd_call(qa, ka, va, groups=A_KV_HEADS, sub=4, scale=None, tq=tq, tk=tk, name="attn_fwd_a")
    ob, lse_b = _attn_fwd_call(qb, kb, vb, groups=B_HEADS, sub=1, scale=scale_b, tq=tq, tk=tk, name="attn_fwd_b")
    dh, doa, dob, dga, dgb, dl_a, dl_b, dw_out, loss_row = _out_call(x2, t2, oa, ob, ga, gb, wout, g64, tm)

    lse_a_t = lse_a[:, :, 0][:, None, :]
    lse_b_t = lse_b[:, :, 0][:, None, :]
    dqa, dka, dva = _attn_bwd_call(qa, ka, va, doa, lse_a_t, _lane_major(dl_a, 64), groups=A_KV_HEADS, sub=4,
                                   scale=None, tq=tq, tk=tk, name="attn_bwd_a")
    dqb, dkb, dvb = _attn_bwd_call(qb, kb, vb, dob, lse_b_t, _lane_major(dl_b, LANES), groups=B_HEADS, sub=1,
                                   scale=scale_b, tq=tq, tk=tk, name="attn_bwd_b")
    dx, dproj_b, xnb, dwuq, dwuk, dwuv, small = _pre_bwd_call(
        x2, dh, dqa, dka, dva, dga, dqb, dkb, dvb, dgb, consts, tabs, tm)
    dw_wide = _dw_in_call(xnb, dproj_b, min(512, s_len), 640)

    dw_in, dw_uq, dw_ukv = _narrow_grads(dw_wide, dwuq, dwuk, dwuv)
    ci, cq, ckv, co = N_IN // N_DEV, B_Q_RANK // N_DEV, 768 // N_DEV, D_MODEL // N_DEV
    parts = jnp.stack([
        _pack_shard(dw_in[:, ci * d:ci * (d + 1)], dw_uq[cq * d:cq * (d + 1)], dw_ukv[:, ckv * d:ckv * (d + 1)],
                    dw_out[co * d:co * (d + 1)]) for d in range(N_DEV)])
    small = jnp.concatenate([small[:ROW_LOSS], loss_row, small[ROW_LOSS + 1:]], axis=0)

    m_pk = _pack_shard(m_w_in[0], m_w_uq[0], m_w_ukv[0], m_w_out[0])
    v_pk = _pack_shard(v_w_in[0], v_w_uq[0], v_w_ukv[0], v_w_out[0])
    w_s = _pack_small(norm_in, a_q_norm, a_k_norm, b_cq_norm, b_ckv_norm, b_q_norm, b_k_norm)
    m_s = _pack_small(m_norm_in, m_a_q_norm, m_a_k_norm, m_b_cq_norm, m_b_ckv_norm, m_b_q_norm, m_b_k_norm)
    v_s = _pack_small(v_norm_in, v_a_q_norm, v_a_k_norm, v_b_cq_norm, v_b_ckv_norm, v_b_q_norm, v_b_k_norm)
    g_pk, d_pk, nm_pk, nv_pk, g_s, d_s, nm_s, nv_s = _reduce_adamw_call(parts, small, w_pk, m_pk, v_pk, w_s, m_s, v_s)

    def leaves(pk, sm):
        wi, uq, ukv, wo = _unpack_shard(pk)
        n_in, aq, ak, bcq, bckv, bq, bk = _unpack_small(sm)
        return [n_in, wi, aq, ak, bcq, bckv, uq, ukv, bq, bk, wo]

    loss = g_s[ROW_LOSS, 0]
    grad_x = dx.reshape(1, s_len, D_MODEL)
    return (loss, grad_x, *leaves(g_pk, g_s), *leaves(d_pk, d_s), *leaves(nm_pk, nm_s), *leaves(nv_pk, nv_s))
```

```python
import functools

import numpy as np
import jax
import jax.numpy as jnp
from jax import lax
from jax.experimental import pallas as pl
from jax.experimental.pallas import tpu as pltpu

F32 = jnp.float32
BF16 = jnp.bfloat16

D_MODEL = 1024
GRID_W = 64
ROPE_THETA = 10000.0
EPS = 1e-6
A_HEAD_DIM = 64
A_HEADS = 8
A_KV_HEADS = 2
B_HEADS = 4
B_NOPE_DIM = 64
B_ROPE_DIM = 32
B_QK_DIM = 96
B_V_DIM = 128
B_Q_RANK = 384
B_KV_RANK = 256
N_IN = 2464
N_DEV = 8

ADAM_LR = 0.001
ADAM_B1 = 0.9
ADAM_B2 = 0.999
ADAM_EPS = 1e-08
ADAM_WD = 0.01
ADAM_STEP = 10

LANES = 128
O_QA, O_KA, O_VA, O_GA, O_CQ, O_CKV, O_KR, O_GB, N_WIDE = 0, 512, 768, 1024, 1536, 1920, 2176, 2688, 3200

R_WIN = D_MODEL * (N_IN // N_DEV) // LANES
R_WUQ = (B_Q_RANK // N_DEV) * 384 // LANES
R_WUKV = B_KV_RANK * (768 // N_DEV) // LANES
R_WOUT = (D_MODEL // N_DEV) * D_MODEL // LANES
R_PACK = R_WIN + R_WUQ + R_WUKV + R_WOUT
R_SMALL = 24
ROW_LOSS = 17

VMEM_LIMIT = 56 * 1024 * 1024

NT = (((1,), (1,)), ((), ()))
TN = (((0,), (0,)), ((), ()))


def _dot(a, b):
    return jnp.dot(a, b, preferred_element_type=F32)


def _dot_nt(a, b):
    return lax.dot_general(a, b, NT, preferred_element_type=F32)


def _dot_tn(a, b):
    return lax.dot_general(a, b, TN, preferred_element_type=F32)


def _params(sem=None):
    return pltpu.CompilerParams(dimension_semantics=sem, vmem_limit_bytes=VMEM_LIMIT)


def _full(shape):
    nd = len(shape)
    return pl.BlockSpec(shape, lambda *_: (0,) * nd)


def _swap_sel(rows, shift):
    lane = lax.broadcasted_iota(jnp.int32, (rows, LANES), 1)
    return pltpu.roll(lane, shift, 1) == (lane ^ shift)


def _swap(x, shift, sel):
    return jnp.where(sel, pltpu.roll(x, shift, 1), pltpu.roll(x, LANES - shift, 1))


def _group_sum64(x, g64):
    hi = x.astype(BF16)
    lo = (x - hi.astype(F32)).astype(BF16)
    return _dot(hi, g64) + _dot(lo, g64)


def _row_sum(x):
    return jnp.sum(x, axis=-1, keepdims=True)


def _col_fwd(xs, msum, denom, gain, cos, sin, shift, sel):
    r = lax.rsqrt(msum(xs * xs) * (1.0 / denom) + EPS)
    xh = xs * r
    n = xh * gain
    return n * cos + _swap(n, shift, sel) * sin, xh, r


def _col_bwd(d_out, xh, r, msum, denom, gain, cos, sin, shift, sel):
    dn = d_out * cos + _swap(d_out * sin, shift, sel)
    dgain = jnp.sum(dn * xh, axis=0, keepdims=True)
    dxh = dn * gain
    dx = r * (dxh - xh * (msum(dxh * xh) * (1.0 / denom)))
    return dx, dgain


def _rms_fwd(x, gain):
    r = lax.rsqrt(jnp.mean(x * x, axis=-1, keepdims=True) + EPS)
    xh = x * r
    return xh * gain, xh, r


def _rms_bwd(dy, xh, r, gain):
    dgain = jnp.sum(dy * xh, axis=0, keepdims=True)
    dxh = dy * gain
    dx = r * (dxh - xh * jnp.mean(dxh * xh, axis=-1, keepdims=True))
    return dx, dgain


def _pre_forward(x, gin, w, wuq, wuk, wuv, gq, gk, gcq, gckv, gqb, gkb, ca, sa, cb, sb, g64, tm):
    sel16 = _swap_sel(tm, 16)
    sel8 = _swap_sel(tm, 8)
    xn, xh0, r0 = _rms_fwd(x, gin)
    xnb = xn.astype(BF16)
    proj = _dot(xnb, w)
    gs64 = functools.partial(_group_sum64, g64=g64)
    qa = [_col_fwd(proj[:, O_QA + LANES * s:O_QA + LANES * (s + 1)], gs64, 64.0, gq, ca, sa, 16, sel16)
          for s in range(4)]
    ka = [_col_fwd(proj[:, O_KA + LANES * s:O_KA + LANES * (s + 1)], _row_sum, 128.0, gk, ca, sa, 16, sel16)
          for s in range(2)]
    cq, cqh, rcq = _rms_fwd(proj[:, O_CQ:O_CQ + B_Q_RANK], gcq)
    cqb = cq.astype(BF16)
    qb_raw = _dot(cqb, wuq)
    qb = [_col_fwd(qb_raw[:, LANES * h:LANES * (h + 1)], _row_sum, float(B_QK_DIM), gqb, cb, sb, 8, sel8)
          for h in range(B_HEADS)]
    ckv, ckvh, rckv = _rms_fwd(proj[:, O_CKV:O_CKV + B_KV_RANK], gckv)
    ckvb = ckv.astype(BF16)
    kb_raw = _dot(ckvb, wuk) + proj[:, O_KR:O_KR + 512]
    vb = _dot(ckvb, wuv)
    kb = [_col_fwd(kb_raw[:, LANES * h:LANES * (h + 1)], _row_sum, float(B_QK_DIM), gkb, cb, sb, 8, sel8)
          for h in range(B_HEADS)]
    return dict(xh0=xh0, r0=r0, xnb=xnb, proj=proj, qa=qa, ka=ka, cqh=cqh, rcq=rcq, cqb=cqb, qb=qb,
                ckvh=ckvh, rckv=rckv, ckvb=ckvb, kb=kb, vb=vb, sel16=sel16, sel8=sel8, gs64=gs64)


_PRE_IN_NAMES = ("gin", "w", "wuq", "wuk", "wuv", "gq", "gk", "gcq", "gckv", "gqb", "gkb", "g64")


def _pre_const_specs(consts):
    return [_full(consts[n].shape) for n in _PRE_IN_NAMES]


def _pre_fwd_call(x, consts, tabs, tm):
    s_len = x.shape[0]

    def body(x_ref, ca_ref, sa_ref, cb_ref, sb_ref, gin_ref, w_ref, wuq_ref, wuk_ref, wuv_ref, gq_ref, gk_ref,
             gcq_ref, gckv_ref, gqb_ref, gkb_ref, g64_ref,
             qa_ref, ka_ref, va_ref, ga_ref, qb_ref, kb_ref, vb_ref, gb_ref):
        f = _pre_forward(x_ref[...], gin_ref[...], w_ref[...], wuq_ref[...], wuk_ref[...], wuv_ref[...],
                         gq_ref[...], gk_ref[...], gcq_ref[...], gckv_ref[...], gqb_ref[...], gkb_ref[...],
                         ca_ref[...], sa_ref[...], cb_ref[...], sb_ref[...], g64_ref[...], tm)
        proj = f["proj"]
        for s in range(4):
            qa_ref[:, LANES * s:LANES * (s + 1)] = (f["qa"][s][0] * 0.125).astype(BF16)
        for s in range(2):
            ka_ref[:, LANES * s:LANES * (s + 1)] = f["ka"][s][0].astype(BF16)
        va_ref[...] = proj[:, O_VA:O_VA + 256].astype(BF16)
        ga_ref[...] = proj[:, O_GA:O_GA + 512]
        for h in range(B_HEADS):
            qb_ref[:, LANES * h:LANES * (h + 1)] = f["qb"][h][0].astype(BF16)
            kb_ref[:, LANES * h:LANES * (h + 1)] = f["kb"][h][0].astype(BF16)
        vb_ref[...] = f["vb"].astype(BF16)
        gb_ref[...] = proj[:, O_GB:O_GB + 512]

    def rows(width):
        return pl.BlockSpec((tm, width), lambda i: (i, 0))

    outs = [(512, BF16), (256, BF16), (256, BF16), (512, F32), (512, BF16), (512, BF16), (512, BF16), (512, F32)]
    return pl.pallas_call(
        body, name="pre_fwd", grid=(s_len // tm,),
        in_specs=[rows(D_MODEL)] + [rows(LANES)] * 4 + _pre_const_specs(consts),
        out_specs=[rows(wd) for wd, _ in outs],
        out_shape=[jax.ShapeDtypeStruct((s_len, wd), dt) for wd, dt in outs],
        compiler_params=_params(("arbitrary",)),
    )(x, *tabs, *[consts[n] for n in _PRE_IN_NAMES])


def _head_masks(rows):
    lane = lax.broadcasted_iota(jnp.int32, (rows, LANES), 1)
    return lane < 64, lane >= 64


def _lane_fold(x, op):
    out = x[:, 0:LANES]
    for c in range(1, x.shape[1] // LANES):
        out = op(out, x[:, LANES * c:LANES * (c + 1)])
    return out


def _attn_fwd_call(q, k, v, *, groups, sub, scale, tq, tk, name):
    s_len = q.shape[0]
    masked = sub > 1
    qw = LANES * (sub // 2 if masked else 1)
    n_c = s_len // tk
    log2e = float(np.log2(np.e))
    mul = log2e if scale is None else scale * log2e

    def body(q_ref, k_ref, v_ref, o_ref, lse_ref, s_sc):
        keep = _head_masks(tq) if masked else None
        outs = []
        for hh in range(sub):
            if masked:
                qp = q_ref[:, LANES * (hh // 2):LANES * (hh // 2 + 1)]
                qm = jnp.where(keep[hh % 2], qp, jnp.zeros_like(qp))
            else:
                qm = q_ref[...]

            def scores(c, mx):
                rows = pl.ds(pl.multiple_of(c * tk, tk), tk)
                s = _dot_nt(qm, k_ref[rows, :]) * mul
                s_sc[c] = s
                return jnp.maximum(mx, _lane_fold(s, jnp.maximum))

            mx = lax.fori_loop(0, n_c, scores, jnp.full((tq, LANES), -jnp.inf, F32), unroll=True)
            m = jnp.max(mx, axis=1, keepdims=True)

            def weights(c, carry):
                lsum, acc = carry
                rows = pl.ds(pl.multiple_of(c * tk, tk), tk)
                p = jnp.exp2(s_sc[c] - m)
                return lsum + _lane_fold(p, jnp.add), acc + _dot(p.astype(BF16), v_ref[rows, :])

            lsum, acc = lax.fori_loop(0, n_c, weights, (jnp.zeros((tq, LANES), F32), jnp.zeros((tq, LANES), F32)),
                                      unroll=True)
            l = jnp.sum(lsum, axis=1, keepdims=True)
            outs.append(acc / l)
            lse_ref[hh] = jnp.broadcast_to((m + jnp.log2(l)) * (1.0 / log2e), (tq, LANES))
        if masked:
            for pr in range(sub // 2):
                o_ref[:, LANES * pr:LANES * (pr + 1)] = jnp.where(keep[0], outs[2 * pr], outs[2 * pr + 1])
        else:
            o_ref[...] = outs[0]

    return pl.pallas_call(
        body, name=name, grid=(groups, s_len // tq),
        in_specs=[pl.BlockSpec((tq, qw), lambda g, i: (i, g)),
                  pl.BlockSpec((s_len, LANES), lambda g, i: (0, g)),
                  pl.BlockSpec((s_len, LANES), lambda g, i: (0, g))],
        out_specs=[pl.BlockSpec((tq, qw), lambda g, i: (i, g)),
                   pl.BlockSpec((sub, tq, LANES), lambda g, i: (g, i, 0))],
        out_shape=[jax.ShapeDtypeStruct((s_len, groups * qw), F32),
                   jax.ShapeDtypeStruct((groups * sub, s_len, LANES), F32)],
        scratch_shapes=[pltpu.VMEM((n_c, tq, tk), F32)],
        compiler_params=_params(("arbitrary", "arbitrary")),
    )(q, k, v)


def _attn_bwd_call(q, k, v, do, lse_t, delta_t, *, groups, sub, scale, tq, tk, name):
    s_len = q.shape[0]
    masked = sub > 1
    qw = LANES * (sub // 2 if masked else 1)
    n_k = s_len // tk

    def body(q_ref, k_ref, v_ref, do_ref, lse_ref, dl_ref, dq_ref, dk_ref, dv_ref, dq_sc):
        i = pl.program_id(1)
        j = pl.program_id(2)

        @pl.when((i == 0) & (j == 0))
        def _():
            dk_ref[...] = jnp.zeros(dk_ref.shape, F32)
            dv_ref[...] = jnp.zeros(dv_ref.shape, F32)

        @pl.when(j == 0)
        def _():
            dq_sc[...] = jnp.zeros(dq_sc.shape, F32)

        kk = k_ref[...]
        vv = v_ref[...]
        keep = _head_masks(tq) if masked else None
        dk_t = jnp.zeros((tk, LANES), F32)
        dv_t = jnp.zeros((tk, LANES), F32)
        for hh in range(sub):
            if masked:
                cols = slice(LANES * (hh // 2), LANES * (hh // 2 + 1))
                qp = q_ref[:, cols]
                dop = do_ref[:, cols]
                qm = jnp.where(keep[hh % 2], qp, jnp.zeros_like(qp))
                dom = jnp.where(keep[hh % 2], dop, jnp.zeros_like(dop))
            else:
                cols = slice(0, LANES)
                qm = q_ref[...]
                dom = do_ref[...]
            s_t = _dot_nt(kk, qm)
            if scale is not None:
                s_t = s_t * scale
            p_t = jnp.exp(s_t - lse_ref[hh])
            dp_t = _dot_nt(vv, dom)
            ds_t = p_t * (dp_t - dl_ref[hh])
            if scale is not None:
                ds_t = ds_t * scale
            p_b = p_t.astype(BF16)
            ds_b = ds_t.astype(BF16)
            dv_t = dv_t + _dot(p_b, dom)
            dk_t = dk_t + _dot(ds_b, qm)
            dq_h = _dot_tn(ds_b, kk)
            if masked:
                dq_h = jnp.where(keep[hh % 2], dq_h, jnp.zeros_like(dq_h))
            dq_sc[:, cols] += dq_h
        rows = pl.ds(pl.multiple_of(j * tk, tk), tk)
        dk_ref[rows, :] += dk_t
        dv_ref[rows, :] += dv_t

        @pl.when(j == n_k - 1)
        def _():
            dq_ref[...] = dq_sc[...]

    return pl.pallas_call(
        body, name=name, grid=(groups, s_len // tq, n_k),
        in_specs=[pl.BlockSpec((tq, qw), lambda g, i, j: (i, g)),
                  pl.BlockSpec((tk, LANES), lambda g, i, j: (j, g)),
                  pl.BlockSpec((tk, LANES), lambda g, i, j: (j, g)),
                  pl.BlockSpec((tq, qw), lambda g, i, j: (i, g)),
                  pl.BlockSpec((sub, 1, tq), lambda g, i, j: (g, 0, i)),
                  pl.BlockSpec((sub, 1, tq), lambda g, i, j: (g, 0, i))],
        out_specs=[pl.BlockSpec((tq, qw), lambda g, i, j: (i, g)),
                   pl.BlockSpec((s_len, LANES), lambda g, i, j: (0, g)),
                   pl.BlockSpec((s_len, LANES), lambda g, i, j: (0, g))],
        out_shape=[jax.ShapeDtypeStruct((s_len, groups * qw), F32),
                   jax.ShapeDtypeStruct((s_len, groups * LANES), F32),
                   jax.ShapeDtypeStruct((s_len, groups * LANES), F32)],
        scratch_shapes=[pltpu.VMEM((tq, qw), F32)],
        compiler_params=_params(("arbitrary", "arbitrary", "arbitrary")),
    )(q, k, v, do, lse_t, delta_t)


def _silu_parts(g):
    sig = 1.0 / (1.0 + jnp.exp(-g))
    return g * sig, sig * (1.0 + g * (1.0 - sig))


def _out_call(x, target, oa, ob, ga, gb, wout, g64, tm):
    s_len = x.shape[0]
    n_t = s_len // tm

    def body(x_ref, t_ref, oa_ref, ob_ref, ga_ref, gb_ref, w_ref, g64_ref,
             dh_ref, doa_ref, dob_ref, dga_ref, dgb_ref, dla_ref, dlb_ref, dw_ref, loss_ref):
        i = pl.program_id(0)

        @pl.when(i == 0)
        def _():
            dw_ref[...] = jnp.zeros(dw_ref.shape, F32)
            loss_ref[...] = jnp.zeros(loss_ref.shape, F32)

        oa_v, ob_v = oa_ref[...], ob_ref[...]
        silu_a, dsilu_a = _silu_parts(ga_ref[...])
        silu_b, dsilu_b = _silu_parts(gb_ref[...])
        ya = (oa_v * silu_a).astype(BF16)
        yb = (ob_v * silu_b).astype(BF16)
        h = x_ref[...] + _dot(ya, w_ref[0:512, :]) + _dot(yb, w_ref[512:1024, :])
        err = h - t_ref[...]
        part = jnp.sum(err * err, axis=0, keepdims=True)
        acc = part[:, 0:LANES]
        for c in range(1, D_MODEL // LANES):
            acc = acc + part[:, LANES * c:LANES * (c + 1)]
        loss_ref[...] += acc
        dh = err * (1.0 / D_MODEL)
        dh_ref[...] = dh
        dhb = dh.astype(BF16)
        dya = _dot_nt(dhb, w_ref[0:512, :])
        dyb = _dot_nt(dhb, w_ref[512:1024, :])
        doa = dya * silu_a
        dob = dyb * silu_b
        doa_ref[...] = doa.astype(BF16)
        dob_ref[...] = dob.astype(BF16)
        dga_ref[...] = dya * oa_v * dsilu_a
        dgb_ref[...] = dyb * ob_v * dsilu_b
        ta = doa * oa_v
        tb = dob * ob_v
        g64 = g64_ref[...]
        for s in range(4):
            dla_ref[:, LANES * s:LANES * (s + 1)] = _group_sum64(ta[:, LANES * s:LANES * (s + 1)], g64)
            dlb_ref[:, LANES * s:LANES * (s + 1)] = jnp.broadcast_to(
                _row_sum(tb[:, LANES * s:LANES * (s + 1)]), (tm, LANES))
        dw_ref[0:512, :] += _dot_tn(ya, dhb)
        dw_ref[512:1024, :] += _dot_tn(yb, dhb)

    def rows(width):
        return pl.BlockSpec((tm, width), lambda i: (i, 0))

    outs = [(D_MODEL, F32), (512, BF16), (512, BF16), (512, F32), (512, F32), (512, F32), (512, F32)]
    return pl.pallas_call(
        body, name="out_fwd", grid=(n_t,),
        in_specs=[rows(D_MODEL), rows(D_MODEL), rows(512), rows(512), rows(512), rows(512),
                  _full((D_MODEL, D_MODEL)), _full((LANES, LANES))],
        out_specs=[rows(wd) for wd, _ in outs] + [_full((D_MODEL, D_MODEL)), _full((1, LANES))],
        out_shape=[jax.ShapeDtypeStruct((s_len, wd), dt) for wd, dt in outs]
        + [jax.ShapeDtypeStruct((D_MODEL, D_MODEL), F32), jax.ShapeDtypeStruct((1, LANES), F32)],
        compiler_params=_params(("arbitrary",)),
    )(x, target, oa, ob, ga, gb, wout, g64)


def _pre_bwd_call(x, dh, dqa, dka, dva, dga, dqb, dkb, dvb, dgb, consts, tabs, tm):
    s_len = x.shape[0]

    def body(x_ref, dh_ref, dqa_ref, dka_ref, dva_ref, dga_ref, dqb_ref, dkb_ref, dvb_ref, dgb_ref,
             ca_ref, sa_ref, cb_ref, sb_ref, gin_ref, w_ref, wuq_ref, wuk_ref, wuv_ref, gq_ref, gk_ref,
             gcq_ref, gckv_ref, gqb_ref, gkb_ref, g64_ref,
             dx_ref, dproj_ref, xnb_ref, dwuq_ref, dwuk_ref, dwuv_ref, small_ref):
        i = pl.program_id(0)

        @pl.when(i == 0)
        def _():
            dwuq_ref[...] = jnp.zeros(dwuq_ref.shape, F32)
            dwuk_ref[...] = jnp.zeros(dwuk_ref.shape, F32)
            dwuv_ref[...] = jnp.zeros(dwuv_ref.shape, F32)
            small_ref[...] = jnp.zeros(small_ref.shape, F32)

        gin, gq, gk = gin_ref[...], gq_ref[...], gk_ref[...]
        gcq, gckv, gqb, gkb = gcq_ref[...], gckv_ref[...], gqb_ref[...], gkb_ref[...]
        ca, sa, cb, sb = ca_ref[...], sa_ref[...], cb_ref[...], sb_ref[...]
        w, wuq, wuk, wuv = w_ref[...], wuq_ref[...], wuk_ref[...], wuv_ref[...]
        f = _pre_forward(x_ref[...], gin, w, wuq, wuk, wuv, gq, gk, gcq, gckv, gqb, gkb,
                         ca, sa, cb, sb, g64_ref[...], tm)
        sel16, sel8, gs64 = f["sel16"], f["sel8"], f["gs64"]
        lane = lax.broadcasted_iota(jnp.int32, (tm, LANES), 1)
        low = lane < 64
        zero = jnp.zeros((tm, LANES), F32)
        pieces = []

        dgq = jnp.zeros((1, LANES), F32)
        for s in range(4):
            _, xh, r = f["qa"][s]
            d = dqa_ref[:, LANES * s:LANES * (s + 1)] * 0.125
            dx, dg = _col_bwd(d, xh, r, gs64, 64.0, gq, ca, sa, 16, sel16)
            pieces.append(dx)
            dgq = dgq + dg
        dgk = jnp.zeros((1, LANES), F32)
        for s in range(2):
            _, xh, r = f["ka"][s]
            d = dka_ref[:, LANES * s:LANES * (s + 1)]
            d = d + pltpu.roll(d, 64, 1)
            dx, dg = _col_bwd(d, xh, r, _row_sum, 128.0, gk, ca, sa, 16, sel16)
            pieces.append(jnp.where(low, dx, zero))
            dgk = dgk + dg
        for s in range(2):
            d = dva_ref[:, LANES * s:LANES * (s + 1)]
            d = d + pltpu.roll(d, 64, 1)
            pieces.append(jnp.where(low, d, zero))
        pieces.append(dga_ref[...])

        dgqb = jnp.zeros((1, LANES), F32)
        dq_cols = []
        for h in range(B_HEADS):
            _, xh, r = f["qb"][h]
            dx, dg = _col_bwd(dqb_ref[:, LANES * h:LANES * (h + 1)], xh, r, _row_sum, float(B_QK_DIM),
                              gqb, cb, sb, 8, sel8)
            dq_cols.append(dx)
            dgqb = dgqb + dg
        dqr_b = jnp.concatenate(dq_cols, axis=1).astype(BF16)
        dwuq_ref[...] += _dot_tn(f["cqb"], dqr_b)
        dcq_raw, dgcq = _rms_bwd(_dot_nt(dqr_b, wuq), f["cqh"], f["rcq"], gcq)
        pieces.append(dcq_raw)

        dgkb = jnp.zeros((1, LANES), F32)
        dk_cols = []
        dkr = zero
        for h in range(B_HEADS):
            _, xh, r = f["kb"][h]
            dx, dg = _col_bwd(dkb_ref[:, LANES * h:LANES * (h + 1)], xh, r, _row_sum, float(B_QK_DIM),
                              gkb, cb, sb, 8, sel8)
            dk_cols.append(dx)
            dkr = dkr + dx
            dgkb = dgkb + dg
        dkr_b = jnp.concatenate(dk_cols, axis=1).astype(BF16)
        dvb_b = dvb_ref[...].astype(BF16)
        dwuk_ref[...] += _dot_tn(f["ckvb"], dkr_b)
        dwuv_ref[...] += _dot_tn(f["ckvb"], dvb_b)
        dckv = _dot_nt(dkr_b, wuk) + _dot_nt(dvb_b, wuv)
        dckv_raw, dgckv = _rms_bwd(dckv, f["ckvh"], f["rckv"], gckv)
        pieces.append(dckv_raw)
        pieces.append(jnp.where((lane >= B_NOPE_DIM) & (lane < B_QK_DIM), dkr, zero))
        pieces += [zero, zero, zero]
        pieces.append(dgb_ref[...])

        dproj_b = jnp.concatenate(pieces, axis=1).astype(BF16)
        dproj_ref[...] = dproj_b
        xnb_ref[...] = f["xnb"]
        dxn = _dot_nt(dproj_b, w)
        dx, dgin = _rms_bwd(dxn, f["xh0"], f["r0"], gin)
        dx_ref[...] = dx + dh_ref[...]

        for c in range(D_MODEL // LANES):
            small_ref[c:c + 1, :] += dgin[:, LANES * c:LANES * (c + 1)]
        small_ref[8:9, :] += dgq
        small_ref[9:10, :] += dgk
        for c in range(3):
            small_ref[10 + c:11 + c, :] += dgcq[:, LANES * c:LANES * (c + 1)]
        for c in range(2):
            small_ref[13 + c:14 + c, :] += dgckv[:, LANES * c:LANES * (c + 1)]
        small_ref[15:16, :] += dgqb
        small_ref[16:17, :] += dgkb

    def rows(width):
        return pl.BlockSpec((tm, width), lambda i: (i, 0))

    return pl.pallas_call(
        body, name="pre_bwd", grid=(s_len // tm,),
        in_specs=[rows(D_MODEL), rows(D_MODEL), rows(512), rows(256), rows(256), rows(512), rows(512), rows(512),
                  rows(512), rows(512)] + [rows(LANES)] * 4 + _pre_const_specs(consts),
        out_specs=[rows(D_MODEL), rows(N_WIDE), rows(D_MODEL), _full((B_Q_RANK, 512)), _full((B_KV_RANK, 512)),
                   _full((B_KV_RANK, 512)), _full((R_SMALL, LANES))],
        out_shape=[jax.ShapeDtypeStruct((s_len, D_MODEL), F32), jax.ShapeDtypeStruct((s_len, N_WIDE), BF16),
                   jax.ShapeDtypeStruct((s_len, D_MODEL), BF16), jax.ShapeDtypeStruct((B_Q_RANK, 512), F32),
                   jax.ShapeDtypeStruct((B_KV_RANK, 512), F32), jax.ShapeDtypeStruct((B_KV_RANK, 512), F32),
                   jax.ShapeDtypeStruct((R_SMALL, LANES), F32)],
        compiler_params=_params(("arbitrary",)),
    )(x, dh, dqa, dka, dva, dga, dqb, dkb, dvb, dgb, *tabs, *[consts[n] for n in _PRE_IN_NAMES])


def _dw_in_call(xnb, dproj_b, tt, tn):
    s_len = xnb.shape[0]

    def body(a_ref, b_ref, o_ref):
        @pl.when(pl.program_id(1) == 0)
        def _():
            o_ref[...] = jnp.zeros(o_ref.shape, F32)

        o_ref[...] += _dot_tn(a_ref[...], b_ref[...])

    return pl.pallas_call(
        body, name="dw_in", grid=(N_WIDE // tn, s_len // tt),
        in_specs=[pl.BlockSpec((tt, D_MODEL), lambda n, t: (t, 0)), pl.BlockSpec((tt, tn), lambda n, t: (t, n))],
        out_specs=pl.BlockSpec((D_MODEL, tn), lambda n, t: (0, n)),
        out_shape=jax.ShapeDtypeStruct((D_MODEL, N_WIDE), F32),
        compiler_params=_params(("arbitrary", "arbitrary")),
    )(xnb, dproj_b)


def _mesh_pos():
    return lax.axis_index("x"), lax.axis_index("y"), lax.axis_index("c")


def _flip(v, bit):
    return 1 - v if bit else v


def _peer(pos, k):
    x, y, c = pos
    return _flip(x, (k >> 2) & 1), _flip(y, (k >> 1) & 1), _flip(c, k & 1)


def _logical(p):
    return 4 * p[0] + 2 * p[1] + p[2]


def _gather_weights_call(shard):
    m_per = shard.shape[0]

    def body(x_ref, out_ref, xb_ref, send_sems, recv_sems, local_sem):
        x, y, c = _mesh_pos()
        me, sibling = (x, y, c), (x, y, 1 - c)
        chips = [(1 - x, y), (x, 1 - y), (1 - x, 1 - y)]
        xb_ref[...] = x_ref[...].astype(BF16)

        def rows(p):
            return out_ref.at[pl.ds(pl.multiple_of(_logical(p) * m_per, 16), m_per), :]

        def copy(k, block, to, src=None):
            return pltpu.make_async_remote_copy(
                src_ref=rows(block) if src is None else src, dst_ref=rows(block),
                send_sem=send_sems.at[k], recv_sem=recv_sems.at[k],
                device_id=to, device_id_type=pl.DeviceIdType.MESH)

        mine = pltpu.make_async_copy(xb_ref, rows(me), local_sem)
        mine.start()
        first = [copy(0, me, sibling, src=xb_ref)]
        first += [copy(1 + j, me, (*chip, c), src=xb_ref) for j, chip in enumerate(chips)]
        for cp in first:
            cp.start()
        passed = [copy(4 + j, (*chip, c), sibling) for j, chip in enumerate(chips)]
        for j, chip in enumerate(chips):
            copy(1 + j, (*chip, c), me).wait_recv()
            passed[j].start()
        copy(0, sibling, me).wait_recv()
        for j, chip in enumerate(chips):
            copy(4 + j, (*chip, 1 - c), me).wait_recv()
        for cp in first + passed:
            cp.wait_send()
        mine.wait()

    return pl.pallas_call(
        body, name="gather_weights",
        out_shape=jax.ShapeDtypeStruct((N_DEV * m_per, LANES), BF16),
        in_specs=[pl.BlockSpec(memory_space=pltpu.VMEM)],
        out_specs=pl.BlockSpec(memory_space=pltpu.VMEM),
        scratch_shapes=[pltpu.VMEM((m_per, LANES), BF16), pltpu.SemaphoreType.DMA((7,)),
                        pltpu.SemaphoreType.DMA((7,)), pltpu.SemaphoreType.DMA],
        compiler_params=pltpu.CompilerParams(vmem_limit_bytes=VMEM_LIMIT),
    )(shard)


def _adamw(w, g, m, v):
    m = ADAM_B1 * m + (1.0 - ADAM_B1) * g
    v = ADAM_B2 * v + (1.0 - ADAM_B2) * (g * g)
    m_hat = m / (1.0 - ADAM_B1 ** ADAM_STEP)
    v_hat = v / (1.0 - ADAM_B2 ** ADAM_STEP)
    delta = -ADAM_LR * (m_hat / (jnp.sqrt(v_hat) + ADAM_EPS) + ADAM_WD * w)
    return delta, m, v


def _reduce_adamw_call(parts, small, w_pk, m_pk, v_pk, w_s, m_s, v_s):
    chunk = 16
    n_chunks = R_PACK // chunk

    def body(parts_ref, small_ref, w_ref, m_ref, v_ref, ws_ref, ms_ref, vs_ref,
             g_ref, d_ref, nm_ref, nv_ref, gs_ref, ds_ref, nms_ref, nvs_ref,
             recv_ref, recv_s_ref, send_sems, recv_sems, send_s_sems, recv_s_sems, local_sem):
        pos = _mesh_pos()
        me = _logical(pos)

        def big(k):
            peer = _peer(pos, k)
            return pltpu.make_async_remote_copy(
                src_ref=parts_ref.at[_logical(peer)], dst_ref=recv_ref.at[k],
                send_sem=send_sems.at[k], recv_sem=recv_sems.at[k],
                device_id=peer, device_id_type=pl.DeviceIdType.MESH)

        def tiny(k):
            return pltpu.make_async_remote_copy(
                src_ref=small_ref, dst_ref=recv_s_ref.at[k],
                send_sem=send_s_sems.at[k], recv_sem=recv_s_sems.at[k],
                device_id=_peer(pos, k), device_id_type=pl.DeviceIdType.MESH)

        own = pltpu.make_async_copy(parts_ref.at[me], recv_ref.at[0], local_sem)
        own.start()
        for k in range(1, N_DEV):
            tiny(k).start()
        for k in range(1, N_DEV):
            big(k).start()
        recv_s_ref[0] = small_ref[...]
        for k in range(1, N_DEV):
            tiny(k).wait_recv()
        acc = recv_s_ref[me]
        for a in range(1, N_DEV):
            acc = acc + recv_s_ref[lax.bitwise_xor(me, a)]
        row = lax.broadcasted_iota(jnp.int32, (R_SMALL, LANES), 0)
        gs = jnp.where(row == 8, acc + pltpu.roll(acc, 64, 1), acc)
        gs = jnp.where(row == ROW_LOSS, jnp.sum(acc, axis=1, keepdims=True) * (0.5 / D_MODEL), gs)
        gs_ref[...] = gs
        ds, nms, nvs = _adamw(ws_ref[...], gs, ms_ref[...], vs_ref[...])
        ds_ref[...] = ds
        nms_ref[...] = nms
        nvs_ref[...] = nvs

        own.wait()
        for k in range(1, N_DEV):
            big(k).wait_recv()

        def step(t, carry):
            rows = pl.ds(pl.multiple_of(t * chunk, chunk), chunk)
            g = recv_ref[0, rows, :]
            for k in range(1, N_DEV):
                g = g + recv_ref[k, rows, :]
            d, nm, nv = _adamw(w_ref[rows, :], g, m_ref[rows, :], v_ref[rows, :])
            g_ref[rows, :] = g
            d_ref[rows, :] = d
            nm_ref[rows, :] = nm
            nv_ref[rows, :] = nv
            return carry

        lax.fori_loop(0, n_chunks, step, 0)
        for k in range(1, N_DEV):
            tiny(k).wait_send()
            big(k).wait_send()

    vm = pl.BlockSpec(memory_space=pltpu.VMEM)
    big_shape = jax.ShapeDtypeStruct((R_PACK, LANES), F32)
    small_shape = jax.ShapeDtypeStruct((R_SMALL, LANES), F32)
    return pl.pallas_call(
        body, name="reduce_adamw",
        in_specs=[pl.BlockSpec(memory_space=pl.ANY)] + [vm] * 7,
        out_specs=[vm] * 8,
        out_shape=[big_shape] * 4 + [small_shape] * 4,
        scratch_shapes=[pltpu.VMEM((N_DEV, R_PACK, LANES), F32), pltpu.VMEM((N_DEV, R_SMALL, LANES), F32),
                        pltpu.SemaphoreType.DMA((N_DEV,)), pltpu.SemaphoreType.DMA((N_DEV,)),
                        pltpu.SemaphoreType.DMA((N_DEV,)), pltpu.SemaphoreType.DMA((N_DEV,)),
                        pltpu.SemaphoreType.DMA],
        compiler_params=pltpu.CompilerParams(vmem_limit_bytes=VMEM_LIMIT),
    )(parts, small, w_pk, m_pk, v_pk, w_s, m_s, v_s)


def _pack_shard(w_in, w_uq, w_ukv, w_out):
    return jnp.concatenate([a.reshape(-1, LANES) for a in (w_in, w_uq, w_ukv, w_out)], axis=0)


def _unpack_shard(p):
    w_in = p[0:R_WIN].reshape(1, D_MODEL, N_IN // N_DEV)
    w_uq = p[R_WIN:R_WIN + R_WUQ].reshape(1, B_Q_RANK // N_DEV, 384)
    w_ukv = p[R_WIN + R_WUQ:R_WIN + R_WUQ + R_WUKV].reshape(1, B_KV_RANK, 768 // N_DEV)
    w_out = p[R_WIN + R_WUQ + R_WUKV:].reshape(1, D_MODEL // N_DEV, D_MODEL)
    return w_in, w_uq, w_ukv, w_out


def _pack_small(norm_in, a_q, a_k, b_cq, b_ckv, b_q, b_k):
    def row(v):
        return jnp.pad(v.reshape(1, -1), ((0, 0), (0, LANES - v.size)))
    rows = [norm_in.reshape(8, LANES), row(a_q), row(a_k), b_cq.reshape(3, LANES), b_ckv.reshape(2, LANES),
            row(b_q), row(b_k), jnp.zeros((R_SMALL - 17, LANES), F32)]
    return jnp.concatenate(rows, axis=0)


def _unpack_small(s):
    return (s[0:8].reshape(1, D_MODEL), s[8:9, :64], s[9:10, :64], s[10:13].reshape(1, B_Q_RANK),
            s[13:15].reshape(1, B_KV_RANK), s[15:16, :B_QK_DIM], s[16:17, :B_QK_DIM])


def _full_weights(gathered):
    g = gathered
    w_in = g[:, 0:R_WIN].reshape(N_DEV, D_MODEL, N_IN // N_DEV).transpose(1, 0, 2).reshape(D_MODEL, N_IN)
    w_uq = g[:, R_WIN:R_WIN + R_WUQ].reshape(B_Q_RANK, 384)
    w_ukv = g[:, R_WIN + R_WUQ:R_WIN + R_WUQ + R_WUKV].reshape(N_DEV, B_KV_RANK, 768 // N_DEV)
    w_ukv = w_ukv.transpose(1, 0, 2).reshape(B_KV_RANK, 768)
    w_out = g[:, R_WIN + R_WUQ + R_WUKV:].reshape(D_MODEL, D_MODEL)
    k0, k1 = w_in[:, 512:576], w_in[:, 576:640]
    v0, v1 = w_in[:, 640:704], w_in[:, 704:768]
    kr = w_in[:, 1920:1952]
    z64 = jnp.zeros((D_MODEL, 64), BF16)
    z32 = jnp.zeros((D_MODEL, 32), BF16)
    kr_blk = jnp.concatenate([z64, kr, z32], axis=1)
    w_wide = jnp.concatenate([w_in[:, 0:512], k0, k0, k1, k1, v0, v0, v1, v1, w_in[:, 768:1280], w_in[:, 1280:1664],
                              w_in[:, 1664:1920], kr_blk, kr_blk, kr_blk, kr_blk, w_in[:, 1952:2464]], axis=1)
    wuq = jnp.pad(w_uq.reshape(B_Q_RANK, B_HEADS, B_QK_DIM), ((0, 0), (0, 0), (0, LANES - B_QK_DIM)))
    wuq = wuq.reshape(B_Q_RANK, 512)
    ukv = w_ukv.reshape(B_KV_RANK, B_HEADS, B_NOPE_DIM + B_V_DIM)
    wuk = jnp.pad(ukv[:, :, :B_NOPE_DIM], ((0, 0), (0, 0), (0, LANES - B_NOPE_DIM))).reshape(B_KV_RANK, 512)
    wuv = ukv[:, :, B_NOPE_DIM:].reshape(B_KV_RANK, 512)
    return w_wide, wuq, wuk, wuv, w_out


def _narrow_grads(dw_wide, dwuq, dwuk, dwuv):
    dw_in = jnp.concatenate([
        dw_wide[:, 0:512], dw_wide[:, 512:576], dw_wide[:, 640:704], dw_wide[:, 768:832], dw_wide[:, 896:960],
        dw_wide[:, 1024:1536], dw_wide[:, 1536:1920], dw_wide[:, 1920:2176],
        dw_wide[:, O_KR + 64:O_KR + 96], dw_wide[:, 2688:3200]], axis=1)
    dw_uq = dwuq.reshape(B_Q_RANK, B_HEADS, LANES)[:, :, :B_QK_DIM].reshape(B_Q_RANK, 384)
    dk = dwuk.reshape(B_KV_RANK, B_HEADS, LANES)[:, :, :B_NOPE_DIM]
    dv = dwuv.reshape(B_KV_RANK, B_HEADS, B_V_DIM)
    dw_ukv = jnp.concatenate([dk, dv], axis=2).reshape(B_KV_RANK, 768)
    return dw_in, dw_uq, dw_ukv


def _rope_tables(s_len):
    rows = s_len // GRID_W
    row = jnp.repeat(jnp.arange(rows, dtype=F32), GRID_W)
    col = jnp.tile(jnp.arange(GRID_W, dtype=F32), rows)

    def tables(dim):
        half = dim // 2
        inv = 1.0 / (ROPE_THETA ** (jnp.arange(0, half, 2, dtype=F32) / half))
        ar, ac = row[:, None] * inv[None, :], col[:, None] * inv[None, :]
        cos = jnp.concatenate([jnp.cos(ar), jnp.cos(ar), jnp.cos(ac), jnp.cos(ac)], axis=1)
        sin = jnp.concatenate([-jnp.sin(ar), jnp.sin(ar), -jnp.sin(ac), jnp.sin(ac)], axis=1)
        return cos, sin

    ca, sa = tables(A_HEAD_DIM)
    cb, sb = tables(B_ROPE_DIM)
    ca, sa = jnp.tile(ca, (1, 2)), jnp.tile(sa, (1, 2))
    cb = jnp.concatenate([jnp.ones((s_len, 64), F32), cb, jnp.ones((s_len, 32), F32)], axis=1)
    sb = jnp.concatenate([jnp.zeros((s_len, 64), F32), sb, jnp.zeros((s_len, 32), F32)], axis=1)
    return ca, sa, cb, sb


def _lane_major(a, step):
    return a[:, ::step].T[:, None, :]


def kernel(x, norm_in, w_in, a_q_norm, a_k_norm, b_cq_norm, b_ckv_norm, w_uq, w_ukv, b_q_norm, b_k_norm, w_out, loss_target, m_norm_in, m_w_in, m_a_q_norm, m_a_k_norm, m_b_cq_norm, m_b_ckv_norm, m_w_uq, m_w_ukv, m_b_q_norm, m_b_k_norm, m_w_out, v_norm_in, v_w_in, v_a_q_norm, v_a_k_norm, v_b_cq_norm, v_b_ckv_norm, v_w_uq, v_w_ukv, v_b_q_norm, v_b_k_norm, v_w_out):
    s_len = x.shape[1]
    tm = min(256, s_len)
    tq = tk = min(512, s_len)
    ftq, ftk = min(256, s_len), min(1024, s_len)
    x2 = x.reshape(s_len, D_MODEL)
    t2 = loss_target.reshape(s_len, D_MODEL)

    w_pk = _pack_shard(w_in[0], w_uq[0], w_ukv[0], w_out[0])
    gathered = _gather_weights_call(w_pk).reshape(N_DEV, R_PACK, LANES)
    w_wide, wuq, wuk, wuv, wout = _full_weights(gathered)

    def dup(v, pad_to=None):
        v = v.reshape(1, -1)
        if pad_to is None:
            return jnp.concatenate([v, v], axis=1)
        return jnp.pad(v, ((0, 0), (0, pad_to - v.shape[1])))

    g64 = jnp.asarray(np.kron(np.eye(2), np.ones((64, 64))), dtype=BF16)
    consts = dict(gin=norm_in, w=w_wide, wuq=wuq, wuk=wuk, wuv=wuv, gq=dup(a_q_norm), gk=dup(a_k_norm),
                  gcq=b_cq_norm, gckv=b_ckv_norm, gqb=dup(b_q_norm, LANES), gkb=dup(b_k_norm, LANES), g64=g64)
    tabs = _rope_tables(s_len)

    qa, ka, va, ga, qb, kb, vb, gb = _pre_fwd_call(x2, consts, tabs, tm)
    scale_b = 1.0 / float(np.sqrt(B_QK_DIM))
    oa, lse_a = _attn_fwd_call(qa, ka, va, groups=A_KV_HEADS, sub=4, scale=None, tq=ftq, tk=ftk, name="attn_fwd_a")
    ob, lse_b = _attn_fwd_call(qb, kb, vb, groups=B_HEADS, sub=1, scale=scale_b, tq=ftq, tk=ftk, name="attn_fwd_b")
    dh, doa, dob, dga, dgb, dl_a, dl_b, dw_out, loss_row = _out_call(x2, t2, oa, ob, ga, gb, wout, g64, tm)

    lse_a_t = lse_a[:, :, 0][:, None, :]
    lse_b_t = lse_b[:, :, 0][:, None, :]
    dqa, dka, dva = _attn_bwd_call(qa, ka, va, doa, lse_a_t, _lane_major(dl_a, 64), groups=A_KV_HEADS, sub=4,
                                   scale=None, tq=tq, tk=tk, name="attn_bwd_a")
    dqb, dkb, dvb = _attn_bwd_call(qb, kb, vb, dob, lse_b_t, _lane_major(dl_b, LANES), groups=B_HEADS, sub=1,
                                   scale=scale_b, tq=tq, tk=tk, name="attn_bwd_b")
    dx, dproj_b, xnb, dwuq, dwuk, dwuv, small = _pre_bwd_call(
        x2, dh, dqa, dka, dva, dga, dqb, dkb, dvb, dgb, consts, tabs, tm)
    dw_wide = _dw_in_call(xnb, dproj_b, min(512, s_len), 640)

    dw_in, dw_uq, dw_ukv = _narrow_grads(dw_wide, dwuq, dwuk, dwuv)
    ci, cq, ckv, co = N_IN // N_DEV, B_Q_RANK // N_DEV, 768 // N_DEV, D_MODEL // N_DEV
    parts = jnp.stack([
        _pack_shard(dw_in[:, ci * d:ci * (d + 1)], dw_uq[cq * d:cq * (d + 1)], dw_ukv[:, ckv * d:ckv * (d + 1)],
                    dw_out[co * d:co * (d + 1)]) for d in range(N_DEV)])
    small = jnp.concatenate([small[:ROW_LOSS], loss_row, small[ROW_LOSS + 1:]], axis=0)

    m_pk = _pack_shard(m_w_in[0], m_w_uq[0], m_w_ukv[0], m_w_out[0])
    v_pk = _pack_shard(v_w_in[0], v_w_uq[0], v_w_ukv[0], v_w_out[0])
    w_s = _pack_small(norm_in, a_q_norm, a_k_norm, b_cq_norm, b_ckv_norm, b_q_norm, b_k_norm)
    m_s = _pack_small(m_norm_in, m_a_q_norm, m_a_k_norm, m_b_cq_norm, m_b_ckv_norm, m_b_q_norm, m_b_k_norm)
    v_s = _pack_small(v_norm_in, v_a_q_norm, v_a_k_norm, v_b_cq_norm, v_b_ckv_norm, v_b_q_norm, v_b_k_norm)
    g_pk, d_pk, nm_pk, nv_pk, g_s, d_s, nm_s, nv_s = _reduce_adamw_call(parts, small, w_pk, m_pk, v_pk, w_s, m_s, v_s)

    def leaves(pk, sm):
        wi, uq, ukv, wo = _unpack_shard(pk)
        n_in, aq, ak, bcq, bckv, bq, bk = _unpack_small(sm)
        return [n_in, wi, aq, ak, bcq, bckv, uq, ukv, bq, bk, wo]

    loss = g_s[ROW_LOSS, 0]
    grad_x = dx.reshape(1, s_len, D_MODEL)
    return (loss, grad_x, *leaves(g_pk, g_s), *leaves(d_pk, d_s), *leaves(nm_pk, nm_s), *leaves(nv_pk, nv_s))
```

```python
import functools

import numpy as np
import jax
import jax.numpy as jnp
from jax import lax
from jax.experimental import pallas as pl
from jax.experimental.pallas import tpu as pltpu

F32 = jnp.float32
BF16 = jnp.bfloat16

D_MODEL = 1024
GRID_W = 64
ROPE_THETA = 10000.0
EPS = 1e-6
A_HEAD_DIM = 64
A_HEADS = 8
A_KV_HEADS = 2
B_HEADS = 4
B_NOPE_DIM = 64
B_ROPE_DIM = 32
B_QK_DIM = 96
B_V_DIM = 128
B_Q_RANK = 384
B_KV_RANK = 256
N_IN = 2464
N_DEV = 8

ADAM_LR = 0.001
ADAM_B1 = 0.9
ADAM_B2 = 0.999
ADAM_EPS = 1e-08
ADAM_WD = 0.01
ADAM_STEP = 10

LANES = 128
O_QA, O_KA, O_VA, O_GA, O_CQ, O_CKV, O_KR, O_GB, N_WIDE = 0, 512, 768, 1024, 1536, 1920, 2176, 2688, 3200

R_WIN = D_MODEL * (N_IN // N_DEV) // LANES
R_WUQ = (B_Q_RANK // N_DEV) * 384 // LANES
R_WUKV = B_KV_RANK * (768 // N_DEV) // LANES
R_WOUT = (D_MODEL // N_DEV) * D_MODEL // LANES
R_PACK = R_WIN + R_WUQ + R_WUKV + R_WOUT
R_SMALL = 24
ROW_LOSS = 17

VMEM_LIMIT = 56 * 1024 * 1024

NT = (((1,), (1,)), ((), ()))
TN = (((0,), (0,)), ((), ()))


def _dot(a, b):
    return jnp.dot(a, b, preferred_element_type=F32)


def _dot_nt(a, b):
    return lax.dot_general(a, b, NT, preferred_element_type=F32)


def _dot_tn(a, b):
    return lax.dot_general(a, b, TN, preferred_element_type=F32)


def _params(sem=None):
    return pltpu.CompilerParams(dimension_semantics=sem, vmem_limit_bytes=VMEM_LIMIT)


def _full(shape):
    nd = len(shape)
    return pl.BlockSpec(shape, lambda *_: (0,) * nd)


def _swap_sel(rows, shift):
    lane = lax.broadcasted_iota(jnp.int32, (rows, LANES), 1)
    return pltpu.roll(lane, shift, 1) == (lane ^ shift)


def _swap(x, shift, sel):
    return jnp.where(sel, pltpu.roll(x, shift, 1), pltpu.roll(x, LANES - shift, 1))


def _group_sum64(x, g64):
    hi = x.astype(BF16)
    lo = (x - hi.astype(F32)).astype(BF16)
    return _dot(hi, g64) + _dot(lo, g64)


def _row_sum(x):
    return jnp.sum(x, axis=-1, keepdims=True)


def _col_fwd(xs, msum, denom, gain, cos, sin, shift, sel):
    r = lax.rsqrt(msum(xs * xs) * (1.0 / denom) + EPS)
    xh = xs * r
    n = xh * gain
    return n * cos + _swap(n, shift, sel) * sin, xh, r


def _col_bwd(d_out, xh, r, msum, denom, gain, cos, sin, shift, sel):
    dn = d_out * cos + _swap(d_out * sin, shift, sel)
    dgain = jnp.sum(dn * xh, axis=0, keepdims=True)
    dxh = dn * gain
    dx = r * (dxh - xh * (msum(dxh * xh) * (1.0 / denom)))
    return dx, dgain


def _rms_fwd(x, gain):
    r = lax.rsqrt(jnp.mean(x * x, axis=-1, keepdims=True) + EPS)
    xh = x * r
    return xh * gain, xh, r


def _rms_bwd(dy, xh, r, gain):
    dgain = jnp.sum(dy * xh, axis=0, keepdims=True)
    dxh = dy * gain
    dx = r * (dxh - xh * jnp.mean(dxh * xh, axis=-1, keepdims=True))
    return dx, dgain


def _pre_forward(x, gin, w, wuq, wuk, wuv, gq, gk, gcq, gckv, gqb, gkb, ca, sa, cb, sb, g64, tm):
    sel16 = _swap_sel(tm, 16)
    sel8 = _swap_sel(tm, 8)
    xn, xh0, r0 = _rms_fwd(x, gin)
    xnb = xn.astype(BF16)
    proj = _dot(xnb, w)
    gs64 = functools.partial(_group_sum64, g64=g64)
    qa = [_col_fwd(proj[:, O_QA + LANES * s:O_QA + LANES * (s + 1)], gs64, 64.0, gq, ca, sa, 16, sel16)
          for s in range(4)]
    ka = [_col_fwd(proj[:, O_KA + LANES * s:O_KA + LANES * (s + 1)], _row_sum, 128.0, gk, ca, sa, 16, sel16)
          for s in range(2)]
    cq, cqh, rcq = _rms_fwd(proj[:, O_CQ:O_CQ + B_Q_RANK], gcq)
    cqb = cq.astype(BF16)
    qb_raw = _dot(cqb, wuq)
    qb = [_col_fwd(qb_raw[:, LANES * h:LANES * (h + 1)], _row_sum, float(B_QK_DIM), gqb, cb, sb, 8, sel8)
          for h in range(B_HEADS)]
    ckv, ckvh, rckv = _rms_fwd(proj[:, O_CKV:O_CKV + B_KV_RANK], gckv)
    ckvb = ckv.astype(BF16)
    kb_raw = _dot(ckvb, wuk) + proj[:, O_KR:O_KR + 512]
    vb = _dot(ckvb, wuv)
    kb = [_col_fwd(kb_raw[:, LANES * h:LANES * (h + 1)], _row_sum, float(B_QK_DIM), gkb, cb, sb, 8, sel8)
          for h in range(B_HEADS)]
    return dict(xh0=xh0, r0=r0, xnb=xnb, proj=proj, qa=qa, ka=ka, cqh=cqh, rcq=rcq, cqb=cqb, qb=qb,
                ckvh=ckvh, rckv=rckv, ckvb=ckvb, kb=kb, vb=vb, sel16=sel16, sel8=sel8, gs64=gs64)


_PRE_IN_NAMES = ("gin", "w", "wuq", "wuk", "wuv", "gq", "gk", "gcq", "gckv", "gqb", "gkb", "g64")


def _pre_const_specs(consts):
    return [_full(consts[n].shape) for n in _PRE_IN_NAMES]


def _pre_fwd_call(x, consts, tabs, tm):
    s_len = x.shape[0]

    def body(x_ref, ca_ref, sa_ref, cb_ref, sb_ref, gin_ref, w_ref, wuq_ref, wuk_ref, wuv_ref, gq_ref, gk_ref,
             gcq_ref, gckv_ref, gqb_ref, gkb_ref, g64_ref,
             qa_ref, ka_ref, va_ref, ga_ref, qb_ref, kb_ref, vb_ref, gb_ref):
        f = _pre_forward(x_ref[...], gin_ref[...], w_ref[...], wuq_ref[...], wuk_ref[...], wuv_ref[...],
                         gq_ref[...], gk_ref[...], gcq_ref[...], gckv_ref[...], gqb_ref[...], gkb_ref[...],
                         ca_ref[...], sa_ref[...], cb_ref[...], sb_ref[...], g64_ref[...], tm)
        proj = f["proj"]
        for s in range(4):
            qa_ref[:, LANES * s:LANES * (s + 1)] = (f["qa"][s][0] * 0.125).astype(BF16)
        for s in range(2):
            ka_ref[:, LANES * s:LANES * (s + 1)] = f["ka"][s][0].astype(BF16)
        va_ref[...] = proj[:, O_VA:O_VA + 256].astype(BF16)
        ga_ref[...] = proj[:, O_GA:O_GA + 512]
        for h in range(B_HEADS):
            qb_ref[:, LANES * h:LANES * (h + 1)] = f["qb"][h][0].astype(BF16)
            kb_ref[:, LANES * h:LANES * (h + 1)] = f["kb"][h][0].astype(BF16)
        vb_ref[...] = f["vb"].astype(BF16)
        gb_ref[...] = proj[:, O_GB:O_GB + 512]

    def rows(width):
        return pl.BlockSpec((tm, width), lambda i: (i, 0))

    outs = [(512, BF16), (256, BF16), (256, BF16), (512, F32), (512, BF16), (512, BF16), (512, BF16), (512, F32)]
    return pl.pallas_call(
        body, name="pre_fwd", grid=(s_len // tm,),
        in_specs=[rows(D_MODEL)] + [rows(LANES)] * 4 + _pre_const_specs(consts),
        out_specs=[rows(wd) for wd, _ in outs],
        out_shape=[jax.ShapeDtypeStruct((s_len, wd), dt) for wd, dt in outs],
        compiler_params=_params(("arbitrary",)),
    )(x, *tabs, *[consts[n] for n in _PRE_IN_NAMES])


def _head_masks(rows):
    lane = lax.broadcasted_iota(jnp.int32, (rows, LANES), 1)
    return lane < 64, lane >= 64


def _lane_fold(x, op):
    out = x[:, 0:LANES]
    for c in range(1, x.shape[1] // LANES):
        out = op(out, x[:, LANES * c:LANES * (c + 1)])
    return out


def _attn_fwd_call(q, k, v, *, groups, sub, scale, tq, tk, name):
    s_len = q.shape[0]
    masked = sub > 1
    qw = LANES * (sub // 2 if masked else 1)
    n_c = s_len // tk
    log2e = float(np.log2(np.e))
    mul = log2e if scale is None else scale * log2e

    def body(q_ref, k_ref, v_ref, o_ref, lse_ref, s_sc):
        keep = _head_masks(tq) if masked else None
        outs = []
        for hh in range(sub):
            if masked:
                qp = q_ref[:, LANES * (hh // 2):LANES * (hh // 2 + 1)]
                qm = jnp.where(keep[hh % 2], qp, jnp.zeros_like(qp))
            else:
                qm = q_ref[...]

            def scores(c, mx):
                rows = pl.ds(pl.multiple_of(c * tk, tk), tk)
                s = _dot_nt(qm, k_ref[rows, :]) * mul
                s_sc[c] = s
                return jnp.maximum(mx, _lane_fold(s, jnp.maximum))

            mx = lax.fori_loop(0, n_c, scores, jnp.full((tq, LANES), -jnp.inf, F32), unroll=True)
            m = jnp.max(mx, axis=1, keepdims=True)

            def weights(c, carry):
                lsum, acc = carry
                rows = pl.ds(pl.multiple_of(c * tk, tk), tk)
                p = jnp.exp2(s_sc[c] - m)
                return lsum + _lane_fold(p, jnp.add), acc + _dot(p.astype(BF16), v_ref[rows, :])

            lsum, acc = lax.fori_loop(0, n_c, weights, (jnp.zeros((tq, LANES), F32), jnp.zeros((tq, LANES), F32)),
                                      unroll=True)
            l = jnp.sum(lsum, axis=1, keepdims=True)
            outs.append(acc / l)
            lse_ref[hh] = jnp.broadcast_to((m + jnp.log2(l)) * (1.0 / log2e), (tq, LANES))
        if masked:
            for pr in range(sub // 2):
                o_ref[:, LANES * pr:LANES * (pr + 1)] = jnp.where(keep[0], outs[2 * pr], outs[2 * pr + 1])
        else:
            o_ref[...] = outs[0]

    return pl.pallas_call(
        body, name=name, grid=(groups, s_len // tq),
        in_specs=[pl.BlockSpec((tq, qw), lambda g, i: (i, g)),
                  pl.BlockSpec((s_len, LANES), lambda g, i: (0, g)),
                  pl.BlockSpec((s_len, LANES), lambda g, i: (0, g))],
        out_specs=[pl.BlockSpec((tq, qw), lambda g, i: (i, g)),
                   pl.BlockSpec((sub, tq, LANES), lambda g, i: (g, i, 0))],
        out_shape=[jax.ShapeDtypeStruct((s_len, groups * qw), F32),
                   jax.ShapeDtypeStruct((groups * sub, s_len, LANES), F32)],
        scratch_shapes=[pltpu.VMEM((n_c, tq, tk), F32)],
        compiler_params=_params(("arbitrary", "arbitrary")),
    )(q, k, v)


def _attn_bwd_call(q, k, v, do, lse_t, delta_t, *, groups, sub, scale, tq, tk, name):
    s_len = q.shape[0]
    masked = sub > 1
    qw = LANES * (sub // 2 if masked else 1)
    n_k = s_len // tk

    def body(q_ref, k_ref, v_ref, do_ref, lse_ref, dl_ref, dq_ref, dk_ref, dv_ref, dq_sc):
        i = pl.program_id(1)
        j = pl.program_id(2)

        @pl.when((i == 0) & (j == 0))
        def _():
            dk_ref[...] = jnp.zeros(dk_ref.shape, F32)
            dv_ref[...] = jnp.zeros(dv_ref.shape, F32)

        @pl.when(j == 0)
        def _():
            dq_sc[...] = jnp.zeros(dq_sc.shape, F32)

        kk = k_ref[...]
        vv = v_ref[...]
        keep = _head_masks(tq) if masked else None
        dk_t = jnp.zeros((tk, LANES), F32)
        dv_t = jnp.zeros((tk, LANES), F32)
        for hh in range(sub):
            if masked:
                cols = slice(LANES * (hh // 2), LANES * (hh // 2 + 1))
                qp = q_ref[:, cols]
                dop = do_ref[:, cols]
                qm = jnp.where(keep[hh % 2], qp, jnp.zeros_like(qp))
                dom = jnp.where(keep[hh % 2], dop, jnp.zeros_like(dop))
            else:
                cols = slice(0, LANES)
                qm = q_ref[...]
                dom = do_ref[...]
            s_t = _dot_nt(kk, qm)
            if scale is not None:
                s_t = s_t * scale
            p_t = jnp.exp(s_t - lse_ref[hh])
            dp_t = _dot_nt(vv, dom)
            ds_t = p_t * (dp_t - dl_ref[hh])
            if scale is not None:
                ds_t = ds_t * scale
            p_b = p_t.astype(BF16)
            ds_b = ds_t.astype(BF16)
            dv_t = dv_t + _dot(p_b, dom)
            dk_t = dk_t + _dot(ds_b, qm)
            dq_h = _dot_tn(ds_b, kk)
            if masked:
                dq_h = jnp.where(keep[hh % 2], dq_h, jnp.zeros_like(dq_h))
            dq_sc[:, cols] += dq_h
        rows = pl.ds(pl.multiple_of(j * tk, tk), tk)
        dk_ref[rows, :] += dk_t
        dv_ref[rows, :] += dv_t

        @pl.when(j == n_k - 1)
        def _():
            dq_ref[...] = dq_sc[...]

    return pl.pallas_call(
        body, name=name, grid=(groups, s_len // tq, n_k),
        in_specs=[pl.BlockSpec((tq, qw), lambda g, i, j: (i, g)),
                  pl.BlockSpec((tk, LANES), lambda g, i, j: (j, g)),
                  pl.BlockSpec((tk, LANES), lambda g, i, j: (j, g)),
                  pl.BlockSpec((tq, qw), lambda g, i, j: (i, g)),
                  pl.BlockSpec((sub, 1, tq), lambda g, i, j: (g, 0, i)),
                  pl.BlockSpec((sub, 1, tq), lambda g, i, j: (g, 0, i))],
        out_specs=[pl.BlockSpec((tq, qw), lambda g, i, j: (i, g)),
                   pl.BlockSpec((s_len, LANES), lambda g, i, j: (0, g)),
                   pl.BlockSpec((s_len, LANES), lambda g, i, j: (0, g))],
        out_shape=[jax.ShapeDtypeStruct((s_len, groups * qw), F32),
                   jax.ShapeDtypeStruct((s_len, groups * LANES), F32),
                   jax.ShapeDtypeStruct((s_len, groups * LANES), F32)],
        scratch_shapes=[pltpu.VMEM((tq, qw), F32)],
        compiler_params=_params(("arbitrary", "arbitrary", "arbitrary")),
    )(q, k, v, do, lse_t, delta_t)


def _silu_parts(g):
    sig = 1.0 / (1.0 + jnp.exp(-g))
    return g * sig, sig * (1.0 + g * (1.0 - sig))


def _out_call(x, target, oa, ob, ga, gb, wout, g64, tm):
    s_len = x.shape[0]
    n_t = s_len // tm

    def body(x_ref, t_ref, oa_ref, ob_ref, ga_ref, gb_ref, w_ref, g64_ref,
             dh_ref, doa_ref, dob_ref, dga_ref, dgb_ref, dla_ref, dlb_ref, dw_ref, loss_ref):
        i = pl.program_id(0)

        @pl.when(i == 0)
        def _():
            dw_ref[...] = jnp.zeros(dw_ref.shape, F32)
            loss_ref[...] = jnp.zeros(loss_ref.shape, F32)

        oa_v, ob_v = oa_ref[...], ob_ref[...]
        silu_a, dsilu_a = _silu_parts(ga_ref[...])
        silu_b, dsilu_b = _silu_parts(gb_ref[...])
        ya = (oa_v * silu_a).astype(BF16)
        yb = (ob_v * silu_b).astype(BF16)
        h = x_ref[...] + _dot(ya, w_ref[0:512, :]) + _dot(yb, w_ref[512:1024, :])
        err = h - t_ref[...]
        part = jnp.sum(err * err, axis=0, keepdims=True)
        acc = part[:, 0:LANES]
        for c in range(1, D_MODEL // LANES):
            acc = acc + part[:, LANES * c:LANES * (c + 1)]
        loss_ref[...] += acc
        dh = err * (1.0 / D_MODEL)
        dh_ref[...] = dh
        dhb = dh.astype(BF16)
        dya = _dot_nt(dhb, w_ref[0:512, :])
        dyb = _dot_nt(dhb, w_ref[512:1024, :])
        doa = dya * silu_a
        dob = dyb * silu_b
        doa_ref[...] = doa.astype(BF16)
        dob_ref[...] = dob.astype(BF16)
        dga_ref[...] = dya * oa_v * dsilu_a
        dgb_ref[...] = dyb * ob_v * dsilu_b
        ta = doa * oa_v
        tb = dob * ob_v
        g64 = g64_ref[...]
        for s in range(4):
            dla_ref[:, LANES * s:LANES * (s + 1)] = _group_sum64(ta[:, LANES * s:LANES * (s + 1)], g64)
            dlb_ref[:, LANES * s:LANES * (s + 1)] = jnp.broadcast_to(
                _row_sum(tb[:, LANES * s:LANES * (s + 1)]), (tm, LANES))
        dw_ref[0:512, :] += _dot_tn(ya, dhb)
        dw_ref[512:1024, :] += _dot_tn(yb, dhb)

    def rows(width):
        return pl.BlockSpec((tm, width), lambda i: (i, 0))

    outs = [(D_MODEL, F32), (512, BF16), (512, BF16), (512, F32), (512, F32), (512, F32), (512, F32)]
    return pl.pallas_call(
        body, name="out_fwd", grid=(n_t,),
        in_specs=[rows(D_MODEL), rows(D_MODEL), rows(512), rows(512), rows(512), rows(512),
                  _full((D_MODEL, D_MODEL)), _full((LANES, LANES))],
        out_specs=[rows(wd) for wd, _ in outs] + [_full((D_MODEL, D_MODEL)), _full((1, LANES))],
        out_shape=[jax.ShapeDtypeStruct((s_len, wd), dt) for wd, dt in outs]
        + [jax.ShapeDtypeStruct((D_MODEL, D_MODEL), F32), jax.ShapeDtypeStruct((1, LANES), F32)],
        compiler_params=_params(("arbitrary",)),
    )(x, target, oa, ob, ga, gb, wout, g64)


def _pre_bwd_call(x, dh, dqa, dka, dva, dga, dqb, dkb, dvb, dgb, consts, tabs, tm):
    s_len = x.shape[0]

    def body(x_ref, dh_ref, dqa_ref, dka_ref, dva_ref, dga_ref, dqb_ref, dkb_ref, dvb_ref, dgb_ref,
             ca_ref, sa_ref, cb_ref, sb_ref, gin_ref, w_ref, wuq_ref, wuk_ref, wuv_ref, gq_ref, gk_ref,
             gcq_ref, gckv_ref, gqb_ref, gkb_ref, g64_ref,
             dx_ref, dproj_ref, xnb_ref, dwuq_ref, dwuk_ref, dwuv_ref, small_ref):
        i = pl.program_id(0)

        @pl.when(i == 0)
        def _():
            dwuq_ref[...] = jnp.zeros(dwuq_ref.shape, F32)
            dwuk_ref[...] = jnp.zeros(dwuk_ref.shape, F32)
            dwuv_ref[...] = jnp.zeros(dwuv_ref.shape, F32)
            small_ref[...] = jnp.zeros(small_ref.shape, F32)

        gin, gq, gk = gin_ref[...], gq_ref[...], gk_ref[...]
        gcq, gckv, gqb, gkb = gcq_ref[...], gckv_ref[...], gqb_ref[...], gkb_ref[...]
        ca, sa, cb, sb = ca_ref[...], sa_ref[...], cb_ref[...], sb_ref[...]
        w, wuq, wuk, wuv = w_ref[...], wuq_ref[...], wuk_ref[...], wuv_ref[...]
        f = _pre_forward(x_ref[...], gin, w, wuq, wuk, wuv, gq, gk, gcq, gckv, gqb, gkb,
                         ca, sa, cb, sb, g64_ref[...], tm)
        sel16, sel8, gs64 = f["sel16"], f["sel8"], f["gs64"]
        lane = lax.broadcasted_iota(jnp.int32, (tm, LANES), 1)
        low = lane < 64
        zero = jnp.zeros((tm, LANES), F32)
        pieces = []

        dgq = jnp.zeros((1, LANES), F32)
        for s in range(4):
            _, xh, r = f["qa"][s]
            d = dqa_ref[:, LANES * s:LANES * (s + 1)] * 0.125
            dx, dg = _col_bwd(d, xh, r, gs64, 64.0, gq, ca, sa, 16, sel16)
            pieces.append(dx)
            dgq = dgq + dg
        dgk = jnp.zeros((1, LANES), F32)
        for s in range(2):
            _, xh, r = f["ka"][s]
            d = dka_ref[:, LANES * s:LANES * (s + 1)]
            d = d + pltpu.roll(d, 64, 1)
            dx, dg = _col_bwd(d, xh, r, _row_sum, 128.0, gk, ca, sa, 16, sel16)
            pieces.append(jnp.where(low, dx, zero))
            dgk = dgk + dg
        for s in range(2):
            d = dva_ref[:, LANES * s:LANES * (s + 1)]
            d = d + pltpu.roll(d, 64, 1)
            pieces.append(jnp.where(low, d, zero))
        pieces.append(dga_ref[...])

        dgqb = jnp.zeros((1, LANES), F32)
        dq_cols = []
        for h in range(B_HEADS):
            _, xh, r = f["qb"][h]
            dx, dg = _col_bwd(dqb_ref[:, LANES * h:LANES * (h + 1)], xh, r, _row_sum, float(B_QK_DIM),
                              gqb, cb, sb, 8, sel8)
            dq_cols.append(dx)
            dgqb = dgqb + dg
        dqr_b = jnp.concatenate(dq_cols, axis=1).astype(BF16)
        dwuq_ref[...] += _dot_tn(f["cqb"], dqr_b)
        dcq_raw, dgcq = _rms_bwd(_dot_nt(dqr_b, wuq), f["cqh"], f["rcq"], gcq)
        pieces.append(dcq_raw)

        dgkb = jnp.zeros((1, LANES), F32)
        dk_cols = []
        dkr = zero
        for h in range(B_HEADS):
            _, xh, r = f["kb"][h]
            dx, dg = _col_bwd(dkb_ref[:, LANES * h:LANES * (h + 1)], xh, r, _row_sum, float(B_QK_DIM),
                              gkb, cb, sb, 8, sel8)
            dk_cols.append(dx)
            dkr = dkr + dx
            dgkb = dgkb + dg
        dkr_b = jnp.concatenate(dk_cols, axis=1).astype(BF16)
        dvb_b = dvb_ref[...].astype(BF16)
        dwuk_ref[...] += _dot_tn(f["ckvb"], dkr_b)
        dwuv_ref[...] += _dot_tn(f["ckvb"], dvb_b)
        dckv = _dot_nt(dkr_b, wuk) + _dot_nt(dvb_b, wuv)
        dckv_raw, dgckv = _rms_bwd(dckv, f["ckvh"], f["rckv"], gckv)
        pieces.append(dckv_raw)
        pieces.append(jnp.where((lane >= B_NOPE_DIM) & (lane < B_QK_DIM), dkr, zero))
        pieces += [zero, zero, zero]
        pieces.append(dgb_ref[...])

        dproj_b = jnp.concatenate(pieces, axis=1).astype(BF16)
        dproj_ref[...] = dproj_b
        xnb_ref[...] = f["xnb"]
        dxn = _dot_nt(dproj_b, w)
        dx, dgin = _rms_bwd(dxn, f["xh0"], f["r0"], gin)
        dx_ref[...] = dx + dh_ref[...]

        for c in range(D_MODEL // LANES):
            small_ref[c:c + 1, :] += dgin[:, LANES * c:LANES * (c + 1)]
        small_ref[8:9, :] += dgq
        small_ref[9:10, :] += dgk
        for c in range(3):
            small_ref[10 + c:11 + c, :] += dgcq[:, LANES * c:LANES * (c + 1)]
        for c in range(2):
            small_ref[13 + c:14 + c, :] += dgckv[:, LANES * c:LANES * (c + 1)]
        small_ref[15:16, :] += dgqb
        small_ref[16:17, :] += dgkb

    def rows(width):
        return pl.BlockSpec((tm, width), lambda i: (i, 0))

    return pl.pallas_call(
        body, name="pre_bwd", grid=(s_len // tm,),
        in_specs=[rows(D_MODEL), rows(D_MODEL), rows(512), rows(256), rows(256), rows(512), rows(512), rows(512),
                  rows(512), rows(512)] + [rows(LANES)] * 4 + _pre_const_specs(consts),
        out_specs=[rows(D_MODEL), rows(N_WIDE), rows(D_MODEL), _full((B_Q_RANK, 512)), _full((B_KV_RANK, 512)),
                   _full((B_KV_RANK, 512)), _full((R_SMALL, LANES))],
        out_shape=[jax.ShapeDtypeStruct((s_len, D_MODEL), F32), jax.ShapeDtypeStruct((s_len, N_WIDE), BF16),
                   jax.ShapeDtypeStruct((s_len, D_MODEL), BF16), jax.ShapeDtypeStruct((B_Q_RANK, 512), F32),
                   jax.ShapeDtypeStruct((B_KV_RANK, 512), F32), jax.ShapeDtypeStruct((B_KV_RANK, 512), F32),
                   jax.ShapeDtypeStruct((R_SMALL, LANES), F32)],
        compiler_params=_params(("arbitrary",)),
    )(x, dh, dqa, dka, dva, dga, dqb, dkb, dvb, dgb, *tabs, *[consts[n] for n in _PRE_IN_NAMES])


def _dw_in_call(xnb, dproj_b, tt, tn):
    s_len = xnb.shape[0]

    def body(a_ref, b_ref, o_ref):
        @pl.when(pl.program_id(1) == 0)
        def _():
            o_ref[...] = jnp.zeros(o_ref.shape, F32)

        o_ref[...] += _dot_tn(a_ref[...], b_ref[...])

    return pl.pallas_call(
        body, name="dw_in", grid=(N_WIDE // tn, s_len // tt),
        in_specs=[pl.BlockSpec((tt, D_MODEL), lambda n, t: (t, 0)), pl.BlockSpec((tt, tn), lambda n, t: (t, n))],
        out_specs=pl.BlockSpec((D_MODEL, tn), lambda n, t: (0, n)),
        out_shape=jax.ShapeDtypeStruct((D_MODEL, N_WIDE), F32),
        compiler_params=_params(("arbitrary", "arbitrary")),
    )(xnb, dproj_b)


def _mesh_pos():
    return lax.axis_index("x"), lax.axis_index("y"), lax.axis_index("c")


def _flip(v, bit):
    return 1 - v if bit else v


def _peer(pos, k):
    x, y, c = pos
    return _flip(x, (k >> 2) & 1), _flip(y, (k >> 1) & 1), _flip(c, k & 1)


def _logical(p):
    return 4 * p[0] + 2 * p[1] + p[2]


def _gather_weights_call(shard):
    m_per = shard.shape[0]

    def body(x_ref, out_ref, xb_ref, send_sems, recv_sems, local_sem):
        x, y, c = _mesh_pos()
        me, sibling = (x, y, c), (x, y, 1 - c)
        chips = [(1 - x, y), (x, 1 - y), (1 - x, 1 - y)]
        xb_ref[...] = x_ref[...].astype(BF16)

        def rows(p):
            return out_ref.at[pl.ds(pl.multiple_of(_logical(p) * m_per, 16), m_per), :]

        def copy(k, block, to, src=None):
            return pltpu.make_async_remote_copy(
                src_ref=rows(block) if src is None else src, dst_ref=rows(block),
                send_sem=send_sems.at[k], recv_sem=recv_sems.at[k],
                device_id=to, device_id_type=pl.DeviceIdType.MESH)

        mine = pltpu.make_async_copy(xb_ref, rows(me), local_sem)
        mine.start()
        first = [copy(0, me, sibling, src=xb_ref)]
        first += [copy(1 + j, me, (*chip, c), src=xb_ref) for j, chip in enumerate(chips)]
        for cp in first:
            cp.start()
        passed = [copy(4 + j, (*chip, c), sibling) for j, chip in enumerate(chips)]
        for j, chip in enumerate(chips):
            copy(1 + j, (*chip, c), me).wait_recv()
            passed[j].start()
        copy(0, sibling, me).wait_recv()
        for j, chip in enumerate(chips):
            copy(4 + j, (*chip, 1 - c), me).wait_recv()
        for cp in first + passed:
            cp.wait_send()
        mine.wait()

    return pl.pallas_call(
        body, name="gather_weights",
        out_shape=jax.ShapeDtypeStruct((N_DEV * m_per, LANES), BF16),
        in_specs=[pl.BlockSpec(memory_space=pltpu.VMEM)],
        out_specs=pl.BlockSpec(memory_space=pltpu.VMEM),
        scratch_shapes=[pltpu.VMEM((m_per, LANES), BF16), pltpu.SemaphoreType.DMA((7,)),
                        pltpu.SemaphoreType.DMA((7,)), pltpu.SemaphoreType.DMA],
        compiler_params=pltpu.CompilerParams(vmem_limit_bytes=VMEM_LIMIT),
    )(shard)


def _adamw(w, g, m, v):
    m = ADAM_B1 * m + (1.0 - ADAM_B1) * g
    v = ADAM_B2 * v + (1.0 - ADAM_B2) * (g * g)
    m_hat = m / (1.0 - ADAM_B1 ** ADAM_STEP)
    v_hat = v / (1.0 - ADAM_B2 ** ADAM_STEP)
    delta = -ADAM_LR * (m_hat / (jnp.sqrt(v_hat) + ADAM_EPS) + ADAM_WD * w)
    return delta, m, v


def _reduce_adamw_call(parts, small, w_pk, m_pk, v_pk, w_s, m_s, v_s):
    chunk = 16
    n_chunks = R_PACK // chunk

    def body(parts_ref, small_ref, w_ref, m_ref, v_ref, ws_ref, ms_ref, vs_ref,
             g_ref, d_ref, nm_ref, nv_ref, gs_ref, ds_ref, nms_ref, nvs_ref,
             recv_ref, recv_s_ref, send_sems, recv_sems, send_s_sems, recv_s_sems, local_sem):
        pos = _mesh_pos()
        me = _logical(pos)

        def big(k):
            peer = _peer(pos, k)
            return pltpu.make_async_remote_copy(
                src_ref=parts_ref.at[_logical(peer)], dst_ref=recv_ref.at[k],
                send_sem=send_sems.at[k], recv_sem=recv_sems.at[k],
                device_id=peer, device_id_type=pl.DeviceIdType.MESH)

        def tiny(k):
            return pltpu.make_async_remote_copy(
                src_ref=small_ref, dst_ref=recv_s_ref.at[k],
                send_sem=send_s_sems.at[k], recv_sem=recv_s_sems.at[k],
                device_id=_peer(pos, k), device_id_type=pl.DeviceIdType.MESH)

        own = pltpu.make_async_copy(parts_ref.at[me], recv_ref.at[0], local_sem)
        own.start()
        for k in range(1, N_DEV):
            tiny(k).start()
        for k in range(1, N_DEV):
            big(k).start()
        recv_s_ref[0] = small_ref[...]
        for k in range(1, N_DEV):
            tiny(k).wait_recv()
        acc = recv_s_ref[me]
        for a in range(1, N_DEV):
            acc = acc + recv_s_ref[lax.bitwise_xor(me, a)]
        row = lax.broadcasted_iota(jnp.int32, (R_SMALL, LANES), 0)
        gs = jnp.where(row == 8, acc + pltpu.roll(acc, 64, 1), acc)
        gs = jnp.where(row == ROW_LOSS, jnp.sum(acc, axis=1, keepdims=True) * (0.5 / D_MODEL), gs)
        gs_ref[...] = gs
        ds, nms, nvs = _adamw(ws_ref[...], gs, ms_ref[...], vs_ref[...])
        ds_ref[...] = ds
        nms_ref[...] = nms
        nvs_ref[...] = nvs

        own.wait()
        for k in range(1, N_DEV):
            big(k).wait_recv()

        def step(t, carry):
            rows = pl.ds(pl.multiple_of(t * chunk, chunk), chunk)
            g = recv_ref[0, rows, :]
            for k in range(1, N_DEV):
                g = g + recv_ref[k, rows, :]
            d, nm, nv = _adamw(w_ref[rows, :], g, m_ref[rows, :], v_ref[rows, :])
            g_ref[rows, :] = g
            d_ref[rows, :] = d
            nm_ref[rows, :] = nm
            nv_ref[rows, :] = nv
            return carry

        lax.fori_loop(0, n_chunks, step, 0)
        for k in range(1, N_DEV):
            tiny(k).wait_send()
            big(k).wait_send()

    vm = pl.BlockSpec(memory_space=pltpu.VMEM)
    big_shape = jax.ShapeDtypeStruct((R_PACK, LANES), F32)
    small_shape = jax.ShapeDtypeStruct((R_SMALL, LANES), F32)
    return pl.pallas_call(
        body, name="reduce_adamw",
        in_specs=[pl.BlockSpec(memory_space=pl.ANY)] + [vm] * 7,
        out_specs=[vm] * 8,
        out_shape=[big_shape] * 4 + [small_shape] * 4,
        scratch_shapes=[pltpu.VMEM((N_DEV, R_PACK, LANES), F32), pltpu.VMEM((N_DEV, R_SMALL, LANES), F32),
                        pltpu.SemaphoreType.DMA((N_DEV,)), pltpu.SemaphoreType.DMA((N_DEV,)),
                        pltpu.SemaphoreType.DMA((N_DEV,)), pltpu.SemaphoreType.DMA((N_DEV,)),
                        pltpu.SemaphoreType.DMA],
        compiler_params=pltpu.CompilerParams(vmem_limit_bytes=VMEM_LIMIT),
    )(parts, small, w_pk, m_pk, v_pk, w_s, m_s, v_s)


def _pack_shard(w_in, w_uq, w_ukv, w_out):
    return jnp.concatenate([a.reshape(-1, LANES) for a in (w_in, w_uq, w_ukv, w_out)], axis=0)


def _unpack_shard(p):
    w_in = p[0:R_WIN].reshape(1, D_MODEL, N_IN // N_DEV)
    w_uq = p[R_WIN:R_WIN + R_WUQ].reshape(1, B_Q_RANK // N_DEV, 384)
    w_ukv = p[R_WIN + R_WUQ:R_WIN + R_WUQ + R_WUKV].reshape(1, B_KV_RANK, 768 // N_DEV)
    w_out = p[R_WIN + R_WUQ + R_WUKV:].reshape(1, D_MODEL // N_DEV, D_MODEL)
    return w_in, w_uq, w_ukv, w_out


def _pack_small(norm_in, a_q, a_k, b_cq, b_ckv, b_q, b_k):
    def row(v):
        return jnp.pad(v.reshape(1, -1), ((0, 0), (0, LANES - v.size)))
    rows = [norm_in.reshape(8, LANES), row(a_q), row(a_k), b_cq.reshape(3, LANES), b_ckv.reshape(2, LANES),
            row(b_q), row(b_k), jnp.zeros((R_SMALL - 17, LANES), F32)]
    return jnp.concatenate(rows, axis=0)


def _unpack_small(s):
    return (s[0:8].reshape(1, D_MODEL), s[8:9, :64], s[9:10, :64], s[10:13].reshape(1, B_Q_RANK),
            s[13:15].reshape(1, B_KV_RANK), s[15:16, :B_QK_DIM], s[16:17, :B_QK_DIM])


def _full_weights(gathered):
    g = gathered
    w_in = g[:, 0:R_WIN].reshape(N_DEV, D_MODEL, N_IN // N_DEV).transpose(1, 0, 2).reshape(D_MODEL, N_IN)
    w_uq = g[:, R_WIN:R_WIN + R_WUQ].reshape(B_Q_RANK, 384)
    w_ukv = g[:, R_WIN + R_WUQ:R_WIN + R_WUQ + R_WUKV].reshape(N_DEV, B_KV_RANK, 768 // N_DEV)
    w_ukv = w_ukv.transpose(1, 0, 2).reshape(B_KV_RANK, 768)
    w_out = g[:, R_WIN + R_WUQ + R_WUKV:].reshape(D_MODEL, D_MODEL)
    k0, k1 = w_in[:, 512:576], w_in[:, 576:640]
    v0, v1 = w_in[:, 640:704], w_in[:, 704:768]
    kr = w_in[:, 1920:1952]
    z64 = jnp.zeros((D_MODEL, 64), BF16)
    z32 = jnp.zeros((D_MODEL, 32), BF16)
    kr_blk = jnp.concatenate([z64, kr, z32], axis=1)
    w_wide = jnp.concatenate([w_in[:, 0:512], k0, k0, k1, k1, v0, v0, v1, v1, w_in[:, 768:1280], w_in[:, 1280:1664],
                              w_in[:, 1664:1920], kr_blk, kr_blk, kr_blk, kr_blk, w_in[:, 1952:2464]], axis=1)
    wuq = jnp.pad(w_uq.reshape(B_Q_RANK, B_HEADS, B_QK_DIM), ((0, 0), (0, 0), (0, LANES - B_QK_DIM)))
    wuq = wuq.reshape(B_Q_RANK, 512)
    ukv = w_ukv.reshape(B_KV_RANK, B_HEADS, B_NOPE_DIM + B_V_DIM)
    wuk = jnp.pad(ukv[:, :, :B_NOPE_DIM], ((0, 0), (0, 0), (0, LANES - B_NOPE_DIM))).reshape(B_KV_RANK, 512)
    wuv = ukv[:, :, B_NOPE_DIM:].reshape(B_KV_RANK, 512)
    return w_wide, wuq, wuk, wuv, w_out


def _narrow_grads(dw_wide, dwuq, dwuk, dwuv):
    dw_in = jnp.concatenate([
        dw_wide[:, 0:512], dw_wide[:, 512:576], dw_wide[:, 640:704], dw_wide[:, 768:832], dw_wide[:, 896:960],
        dw_wide[:, 1024:1536], dw_wide[:, 1536:1920], dw_wide[:, 1920:2176],
        dw_wide[:, O_KR + 64:O_KR + 96], dw_wide[:, 2688:3200]], axis=1)
    dw_uq = dwuq.reshape(B_Q_RANK, B_HEADS, LANES)[:, :, :B_QK_DIM].reshape(B_Q_RANK, 384)
    dk = dwuk.reshape(B_KV_RANK, B_HEADS, LANES)[:, :, :B_NOPE_DIM]
    dv = dwuv.reshape(B_KV_RANK, B_HEADS, B_V_DIM)
    dw_ukv = jnp.concatenate([dk, dv], axis=2).reshape(B_KV_RANK, 768)
    return dw_in, dw_uq, dw_ukv


def _rope_tables(s_len):
    rows = s_len // GRID_W
    row = jnp.repeat(jnp.arange(rows, dtype=F32), GRID_W)
    col = jnp.tile(jnp.arange(GRID_W, dtype=F32), rows)

    def tables(dim):
        half = dim // 2
        inv = 1.0 / (ROPE_THETA ** (jnp.arange(0, half, 2, dtype=F32) / half))
        ar, ac = row[:, None] * inv[None, :], col[:, None] * inv[None, :]
        cos = jnp.concatenate([jnp.cos(ar), jnp.cos(ar), jnp.cos(ac), jnp.cos(ac)], axis=1)
        sin = jnp.concatenate([-jnp.sin(ar), jnp.sin(ar), -jnp.sin(ac), jnp.sin(ac)], axis=1)
        return cos, sin

    ca, sa = tables(A_HEAD_DIM)
    cb, sb = tables(B_ROPE_DIM)
    ca, sa = jnp.tile(ca, (1, 2)), jnp.tile(sa, (1, 2))
    cb = jnp.concatenate([jnp.ones((s_len, 64), F32), cb, jnp.ones((s_len, 32), F32)], axis=1)
    sb = jnp.concatenate([jnp.zeros((s_len, 64), F32), sb, jnp.zeros((s_len, 32), F32)], axis=1)
    return ca, sa, cb, sb


def _lane_major(a, step):
    return a[:, ::step].T[:, None, :]


def kernel(x, norm_in, w_in, a_q_norm, a_k_norm, b_cq_norm, b_ckv_norm, w_uq, w_ukv, b_q_norm, b_k_norm, w_out, loss_target, m_norm_in, m_w_in, m_a_q_norm, m_a_k_norm, m_b_cq_norm, m_b_ckv_norm, m_w_uq, m_w_ukv, m_b_q_norm, m_b_k_norm, m_w_out, v_norm_in, v_w_in, v_a_q_norm, v_a_k_norm, v_b_cq_norm, v_b_ckv_norm, v_w_uq, v_w_ukv, v_b_q_norm, v_b_k_norm, v_w_out):
    s_len = x.shape[1]
    tm = min(256, s_len)
    tq, tk = min(512, s_len), min(2048, s_len)
    ftq, ftk = min(256, s_len), min(1024, s_len)
    x2 = x.reshape(s_len, D_MODEL)
    t2 = loss_target.reshape(s_len, D_MODEL)

    w_pk = _pack_shard(w_in[0], w_uq[0], w_ukv[0], w_out[0])
    gathered = _gather_weights_call(w_pk).reshape(N_DEV, R_PACK, LANES)
    w_wide, wuq, wuk, wuv, wout = _full_weights(gathered)

    def dup(v, pad_to=None):
        v = v.reshape(1, -1)
        if pad_to is None:
            return jnp.concatenate([v, v], axis=1)
        return jnp.pad(v, ((0, 0), (0, pad_to - v.shape[1])))

    g64 = jnp.asarray(np.kron(np.eye(2), np.ones((64, 64))), dtype=BF16)
    consts = dict(gin=norm_in, w=w_wide, wuq=wuq, wuk=wuk, wuv=wuv, gq=dup(a_q_norm), gk=dup(a_k_norm),
                  gcq=b_cq_norm, gckv=b_ckv_norm, gqb=dup(b_q_norm, LANES), gkb=dup(b_k_norm, LANES), g64=g64)
    tabs = _rope_tables(s_len)

    qa, ka, va, ga, qb, kb, vb, gb = _pre_fwd_call(x2, consts, tabs, tm)
    scale_b = 1.0 / float(np.sqrt(B_QK_DIM))
    oa, lse_a = _attn_fwd_call(qa, ka, va, groups=A_KV_HEADS, sub=4, scale=None, tq=ftq, tk=ftk, name="attn_fwd_a")
    ob, lse_b = _attn_fwd_call(qb, kb, vb, groups=B_HEADS, sub=1, scale=scale_b, tq=ftq, tk=ftk, name="attn_fwd_b")
    dh, doa, dob, dga, dgb, dl_a, dl_b, dw_out, loss_row = _out_call(x2, t2, oa, ob, ga, gb, wout, g64, tm)

    lse_a_t = lse_a[:, :, 0][:, None, :]
    lse_b_t = lse_b[:, :, 0][:, None, :]
    dqa, dka, dva = _attn_bwd_call(qa, ka, va, doa, lse_a_t, _lane_major(dl_a, 64), groups=A_KV_HEADS, sub=4,
                                   scale=None, tq=tq, tk=tk, name="attn_bwd_a")
    dqb, dkb, dvb = _attn_bwd_call(qb, kb, vb, dob, lse_b_t, _lane_major(dl_b, LANES), groups=B_HEADS, sub=1,
                                   scale=scale_b, tq=tq, tk=tk, name="attn_bwd_b")
    dx, dproj_b, xnb, dwuq, dwuk, dwuv, small = _pre_bwd_call(
        x2, dh, dqa, dka, dva, dga, dqb, dkb, dvb, dgb, consts, tabs, tm)
    dw_wide = _dw_in_call(xnb, dproj_b, min(512, s_len), 640)

    dw_in, dw_uq, dw_ukv = _narrow_grads(dw_wide, dwuq, dwuk, dwuv)
    ci, cq, ckv, co = N_IN // N_DEV, B_Q_RANK // N_DEV, 768 // N_DEV, D_MODEL // N_DEV
    parts = jnp.stack([
        _pack_shard(dw_in[:, ci * d:ci * (d + 1)], dw_uq[cq * d:cq * (d + 1)], dw_ukv[:, ckv * d:ckv * (d + 1)],
                    dw_out[co * d:co * (d + 1)]) for d in range(N_DEV)])
    small = jnp.concatenate([small[:ROW_LOSS], loss_row, small[ROW_LOSS + 1:]], axis=0)

    m_pk = _pack_shard(m_w_in[0], m_w_uq[0], m_w_ukv[0], m_w_out[0])
    v_pk = _pack_shard(v_w_in[0], v_w_uq[0], v_w_ukv[0], v_w_out[0])
    w_s = _pack_small(norm_in, a_q_norm, a_k_norm, b_cq_norm, b_ckv_norm, b_q_norm, b_k_norm)
    m_s = _pack_small(m_norm_in, m_a_q_norm, m_a_k_norm, m_b_cq_norm, m_b_ckv_norm, m_b_q_norm, m_b_k_norm)
    v_s = _pack_small(v_norm_in, v_a_q_norm, v_a_k_norm, v_b_cq_norm, v_b_ckv_norm, v_b_q_norm, v_b_k_norm)
    g_pk, d_pk, nm_pk, nv_pk, g_s, d_s, nm_s, nv_s = _reduce_adamw_call(parts, small, w_pk, m_pk, v_pk, w_s, m_s, v_s)

    def leaves(pk, sm):
        wi, uq, ukv, wo = _unpack_shard(pk)
        n_in, aq, ak, bcq, bckv, bq, bk = _unpack_small(sm)
        return [n_in, wi, aq, ak, bcq, bckv, uq, ukv, bq, bk, wo]

    loss = g_s[ROW_LOSS, 0]
    grad_x = dx.reshape(1, s_len, D_MODEL)
    return (loss, grad_x, *leaves(g_pk, g_s), *leaves(d_pk, d_s), *leaves(nm_pk, nm_s), *leaves(nv_pk, nv_s))
```

```python
import functools

import numpy as np
import jax
import jax.numpy as jnp
from jax import lax
from jax.experimental import pallas as pl
from jax.experimental.pallas import tpu as pltpu

F32 = jnp.float32
BF16 = jnp.bfloat16

D_MODEL = 1024
GRID_W = 64
ROPE_THETA = 10000.0
EPS = 1e-6
A_HEAD_DIM = 64
A_HEADS = 8
A_KV_HEADS = 2
B_HEADS = 4
B_NOPE_DIM = 64
B_ROPE_DIM = 32
B_QK_DIM = 96
B_V_DIM = 128
B_Q_RANK = 384
B_KV_RANK = 256
N_IN = 2464
N_DEV = 8

ADAM_LR = 0.001
ADAM_B1 = 0.9
ADAM_B2 = 0.999
ADAM_EPS = 1e-08
ADAM_WD = 0.01
ADAM_STEP = 10

QA_SCALE = 0.125
QB_SCALE = 1.0 / float(np.sqrt(B_QK_DIM))

LANES = 128
O_QA, O_KA, O_VA, O_GA, O_CQ, O_CKV, O_KR, O_GB, N_WIDE = 0, 512, 768, 1024, 1536, 1920, 2176, 2688, 3200

R_WIN = D_MODEL * (N_IN // N_DEV) // LANES
R_WUQ = (B_Q_RANK // N_DEV) * 384 // LANES
R_WUKV = B_KV_RANK * (768 // N_DEV) // LANES
R_WOUT = (D_MODEL // N_DEV) * D_MODEL // LANES
R_PACK = R_WIN + R_WUQ + R_WUKV + R_WOUT
R_SMALL = 24
ROW_LOSS = 17

VMEM_LIMIT = 56 * 1024 * 1024

NT = (((1,), (1,)), ((), ()))
TN = (((0,), (0,)), ((), ()))


def _dot(a, b):
    return jnp.dot(a, b, preferred_element_type=F32)


def _dot_nt(a, b):
    return lax.dot_general(a, b, NT, preferred_element_type=F32)


def _dot_tn(a, b):
    return lax.dot_general(a, b, TN, preferred_element_type=F32)


def _params(sem=None):
    return pltpu.CompilerParams(dimension_semantics=sem, vmem_limit_bytes=VMEM_LIMIT)


def _full(shape):
    nd = len(shape)
    return pl.BlockSpec(shape, lambda *_: (0,) * nd)


def _swap_sel(rows, shift):
    lane = lax.broadcasted_iota(jnp.int32, (rows, LANES), 1)
    return pltpu.roll(lane, shift, 1) == (lane ^ shift)


def _swap(x, shift, sel):
    return jnp.where(sel, pltpu.roll(x, shift, 1), pltpu.roll(x, LANES - shift, 1))


def _group_sum64(x, g64):
    hi = x.astype(BF16)
    lo = (x - hi.astype(F32)).astype(BF16)
    return _dot(hi, g64) + _dot(lo, g64)


def _row_sum(x):
    return jnp.sum(x, axis=-1, keepdims=True)


def _col_fwd(xs, msum, denom, gain, cos, sin, shift, sel):
    r = lax.rsqrt(msum(xs * xs) * (1.0 / denom) + EPS)
    xh = xs * r
    n = xh * gain
    return n * cos + _swap(n, shift, sel) * sin, xh, r


def _col_bwd(d_out, xh, r, msum, denom, gain, cos, sin, shift, sel):
    dn = d_out * cos + _swap(d_out * sin, shift, sel)
    dgain = jnp.sum(dn * xh, axis=0, keepdims=True)
    dxh = dn * gain
    dx = r * (dxh - xh * (msum(dxh * xh) * (1.0 / denom)))
    return dx, dgain


def _rms_fwd(x, gain):
    r = lax.rsqrt(jnp.mean(x * x, axis=-1, keepdims=True) + EPS)
    xh = x * r
    return xh * gain, xh, r


def _rms_bwd(dy, xh, r, gain):
    dgain = jnp.sum(dy * xh, axis=0, keepdims=True)
    dxh = dy * gain
    dx = r * (dxh - xh * jnp.mean(dxh * xh, axis=-1, keepdims=True))
    return dx, dgain


def _pre_forward(x, gin, w, wuq, wuk, wuv, gq, gk, gcq, gckv, gqb, gkb, ca, sa, cb, sb, g64, tm):
    sel16 = _swap_sel(tm, 16)
    sel8 = _swap_sel(tm, 8)
    xn, xh0, r0 = _rms_fwd(x, gin)
    xnb = xn.astype(BF16)
    proj = _dot(xnb, w)
    gs64 = functools.partial(_group_sum64, g64=g64)
    qa = [_col_fwd(proj[:, O_QA + LANES * s:O_QA + LANES * (s + 1)], gs64, 64.0, gq, ca, sa, 16, sel16)
          for s in range(4)]
    ka = [_col_fwd(proj[:, O_KA + LANES * s:O_KA + LANES * (s + 1)], _row_sum, 128.0, gk, ca, sa, 16, sel16)
          for s in range(2)]
    cq, cqh, rcq = _rms_fwd(proj[:, O_CQ:O_CQ + B_Q_RANK], gcq)
    cqb = cq.astype(BF16)
    qb_raw = _dot(cqb, wuq)
    qb = [_col_fwd(qb_raw[:, LANES * h:LANES * (h + 1)], _row_sum, float(B_QK_DIM), gqb, cb, sb, 8, sel8)
          for h in range(B_HEADS)]
    ckv, ckvh, rckv = _rms_fwd(proj[:, O_CKV:O_CKV + B_KV_RANK], gckv)
    ckvb = ckv.astype(BF16)
    kb_raw = _dot(ckvb, wuk) + proj[:, O_KR:O_KR + 512]
    vb = _dot(ckvb, wuv)
    kb = [_col_fwd(kb_raw[:, LANES * h:LANES * (h + 1)], _row_sum, float(B_QK_DIM), gkb, cb, sb, 8, sel8)
          for h in range(B_HEADS)]
    return dict(xh0=xh0, r0=r0, xnb=xnb, proj=proj, qa=qa, ka=ka, cqh=cqh, rcq=rcq, cqb=cqb, qb=qb,
                ckvh=ckvh, rckv=rckv, ckvb=ckvb, kb=kb, vb=vb, sel16=sel16, sel8=sel8, gs64=gs64)


_PRE_IN_NAMES = ("gin", "w", "wuq", "wuk", "wuv", "gq", "gk", "gcq", "gckv", "gqb", "gkb", "g64")


def _pre_const_specs(consts):
    return [_full(consts[n].shape) for n in _PRE_IN_NAMES]


def _pre_fwd_call(x, consts, tabs, tm):
    s_len = x.shape[0]

    def body(x_ref, ca_ref, sa_ref, cb_ref, sb_ref, gin_ref, w_ref, wuq_ref, wuk_ref, wuv_ref, gq_ref, gk_ref,
             gcq_ref, gckv_ref, gqb_ref, gkb_ref, g64_ref,
             qa_ref, ka_ref, va_ref, ga_ref, qb_ref, kb_ref, vb_ref, gb_ref):
        f = _pre_forward(x_ref[...], gin_ref[...], w_ref[...], wuq_ref[...], wuk_ref[...], wuv_ref[...],
                         gq_ref[...], gk_ref[...], gcq_ref[...], gckv_ref[...], gqb_ref[...], gkb_ref[...],
                         ca_ref[...], sa_ref[...], cb_ref[...], sb_ref[...], g64_ref[...], tm)
        proj = f["proj"]
        for s in range(4):
            qa_ref[:, LANES * s:LANES * (s + 1)] = (f["qa"][s][0] * QA_SCALE).astype(BF16)
        for s in range(2):
            ka_ref[:, LANES * s:LANES * (s + 1)] = f["ka"][s][0].astype(BF16)
        va_ref[...] = proj[:, O_VA:O_VA + 256].astype(BF16)
        ga_ref[...] = proj[:, O_GA:O_GA + 512]
        for h in range(B_HEADS):
            qb_ref[:, LANES * h:LANES * (h + 1)] = (f["qb"][h][0] * QB_SCALE).astype(BF16)
            kb_ref[:, LANES * h:LANES * (h + 1)] = f["kb"][h][0].astype(BF16)
        vb_ref[...] = f["vb"].astype(BF16)
        gb_ref[...] = proj[:, O_GB:O_GB + 512]

    def rows(width):
        return pl.BlockSpec((tm, width), lambda i: (i, 0))

    outs = [(512, BF16), (256, BF16), (256, BF16), (512, F32), (512, BF16), (512, BF16), (512, BF16), (512, F32)]
    return pl.pallas_call(
        body, name="pre_fwd", grid=(s_len // tm,),
        in_specs=[rows(D_MODEL)] + [rows(LANES)] * 4 + _pre_const_specs(consts),
        out_specs=[rows(wd) for wd, _ in outs],
        out_shape=[jax.ShapeDtypeStruct((s_len, wd), dt) for wd, dt in outs],
        compiler_params=_params(("arbitrary",)),
    )(x, *tabs, *[consts[n] for n in _PRE_IN_NAMES])


def _head_masks(rows):
    lane = lax.broadcasted_iota(jnp.int32, (rows, LANES), 1)
    return lane < 64, lane >= 64


def _lane_fold(x, op):
    out = x[:, 0:LANES]
    for c in range(1, x.shape[1] // LANES):
        out = op(out, x[:, LANES * c:LANES * (c + 1)])
    return out


def _row_fold(x, op):
    return op(x.reshape(x.shape[0] // 8, 8, x.shape[1]), axis=0)


def _attn_fwd_t_call(q, k, vt, *, groups, sub, masked, scale, tq, tk, name):
    s_len = q.shape[0]
    qw = LANES * (sub // 2 if masked else sub)
    kvw = LANES if masked else LANES * sub
    n_c = s_len // tk
    kv_mode = pl.Buffered(1) if groups == 1 else None

    def body(q_ref, k_ref, vt_ref, o_ref, lse_ref, s_sc):
        keep = _head_masks(tq) if masked else None

        def kv_of(hh):
            return slice(0, LANES) if masked else slice(LANES * hh, LANES * (hh + 1))

        def q_of(hh):
            if not masked:
                return q_ref[:, LANES * hh:LANES * (hh + 1)]
            qp = q_ref[:, LANES * (hh // 2):LANES * (hh // 2 + 1)]
            return jnp.where(keep[hh % 2], qp, jnp.zeros_like(qp))

        def scores(hh, qm, c, mx):
            s_t = _dot_nt(k_ref[tk * c:tk * (c + 1), kv_of(hh)], qm)
            if scale is not None:
                s_t = s_t * scale
            s_sc[hh % 2, c] = s_t
            return jnp.maximum(mx, _row_fold(s_t, jnp.max))

        neg = jnp.full((8, tq), -jnp.inf, F32)
        qm_next = q_of(0)
        mx_next = neg
        for c in range(n_c):
            mx_next = scores(0, qm_next, c, mx_next)
        outs = []
        for hh in range(sub):
            m = jnp.max(mx_next, axis=0, keepdims=True)
            if hh + 1 < sub:
                qm_next = q_of(hh + 1)
                mx_next = neg
            lsum = jnp.zeros((8, tq), F32)
            acc = jnp.zeros((LANES, tq), F32)
            for c in range(n_c):
                if hh + 1 < sub:
                    mx_next = scores(hh + 1, qm_next, c, mx_next)
                p_t = jnp.exp(s_sc[hh % 2, c] - m)
                lsum = lsum + _row_fold(p_t, jnp.sum)
                acc = acc + _dot(vt_ref[kv_of(hh), tk * c:tk * (c + 1)], p_t.astype(BF16))
            l = jnp.sum(lsum, axis=0, keepdims=True)
            outs.append((acc / l).T)
            lse_ref[hh] = m + jnp.log(l)
        if masked:
            for pr in range(sub // 2):
                o_ref[:, LANES * pr:LANES * (pr + 1)] = jnp.where(keep[0], outs[2 * pr], outs[2 * pr + 1])
        else:
            for hh in range(sub):
                o_ref[:, LANES * hh:LANES * (hh + 1)] = outs[hh]

    return pl.pallas_call(
        body, name=name, grid=(groups, s_len // tq),
        in_specs=[pl.BlockSpec((tq, qw), lambda g, i: (i, g)),
                  pl.BlockSpec((s_len, kvw), lambda g, i: (0, g), pipeline_mode=kv_mode),
                  pl.BlockSpec((kvw, s_len), lambda g, i: (g, 0), pipeline_mode=kv_mode)],
        out_specs=[pl.BlockSpec((tq, qw), lambda g, i: (i, g)),
                   pl.BlockSpec((sub, 1, tq), lambda g, i: (g, 0, i))],
        out_shape=[jax.ShapeDtypeStruct((s_len, groups * qw), F32),
                   jax.ShapeDtypeStruct((groups * sub, 1, s_len), F32)],
        scratch_shapes=[pltpu.VMEM((min(sub, 2), n_c, tk, tq), F32)],
        compiler_params=_params(("arbitrary", "arbitrary")),
    )(q, k, vt)


def _attn_fwd_call(q, k, v, *, groups, sub, masked, scale, tq, tk, name):
    s_len = q.shape[0]
    qw = LANES * (sub // 2 if masked else sub)
    n_c = s_len // tk
    log2e = float(np.log2(np.e))
    mul = log2e if scale is None else scale * log2e

    def body(q_ref, k_ref, v_ref, o_ref, lse_ref, s_sc):
        keep = _head_masks(tq) if masked else None
        outs = []
        for hh in range(sub):
            if masked:
                qp = q_ref[:, LANES * (hh // 2):LANES * (hh // 2 + 1)]
                qm = jnp.where(keep[hh % 2], qp, jnp.zeros_like(qp))
                kv_cols = slice(0, LANES)
            else:
                qm = q_ref[:, LANES * hh:LANES * (hh + 1)]
                kv_cols = slice(LANES * hh, LANES * (hh + 1))
            buf = hh % 2

            def scores(c, mx):
                rows = pl.ds(pl.multiple_of(c * tk, tk), tk)
                s = _dot_nt(qm, k_ref[rows, kv_cols]) * mul
                s_sc[buf, c] = s
                return jnp.maximum(mx, _lane_fold(s, jnp.maximum))

            mx = lax.fori_loop(0, n_c, scores, jnp.full((tq, LANES), -jnp.inf, F32), unroll=True)
            m = jnp.max(mx, axis=1, keepdims=True)

            def weights(c, carry):
                lsum, acc = carry
                rows = pl.ds(pl.multiple_of(c * tk, tk), tk)
                p = jnp.exp2(s_sc[buf, c] - m)
                return lsum + _lane_fold(p, jnp.add), acc + _dot(p.astype(BF16), v_ref[rows, kv_cols])

            lsum, acc = lax.fori_loop(0, n_c, weights, (jnp.zeros((tq, LANES), F32), jnp.zeros((tq, LANES), F32)),
                                      unroll=True)
            l = jnp.sum(lsum, axis=1, keepdims=True)
            outs.append(acc / l)
            lse_ref[hh] = jnp.broadcast_to((m + jnp.log2(l)) * (1.0 / log2e), (tq, LANES))
        if masked:
            for pr in range(sub // 2):
                o_ref[:, LANES * pr:LANES * (pr + 1)] = jnp.where(keep[0], outs[2 * pr], outs[2 * pr + 1])
        else:
            for hh in range(sub):
                o_ref[:, LANES * hh:LANES * (hh + 1)] = outs[hh]

    kvw = LANES if masked else LANES * sub
    return pl.pallas_call(
        body, name=name, grid=(groups, s_len // tq),
        in_specs=[pl.BlockSpec((tq, qw), lambda g, i: (i, g)),
                  pl.BlockSpec((s_len, kvw), lambda g, i: (0, g)),
                  pl.BlockSpec((s_len, kvw), lambda g, i: (0, g))],
        out_specs=[pl.BlockSpec((tq, qw), lambda g, i: (i, g)),
                   pl.BlockSpec((sub, tq, LANES), lambda g, i: (g, i, 0))],
        out_shape=[jax.ShapeDtypeStruct((s_len, groups * qw), F32),
                   jax.ShapeDtypeStruct((groups * sub, s_len, LANES), F32)],
        scratch_shapes=[pltpu.VMEM((min(sub, 2), n_c, tq, tk), F32)],
        compiler_params=_params(("arbitrary", "arbitrary")),
    )(q, k, v)


def _attn_bwd_call(q, k, v, do, lse_t, delta_t, *, groups, sub, scale, tq, tk, name):
    s_len = q.shape[0]
    masked = sub > 1
    qw = LANES * (sub // 2 if masked else 1)
    n_k = s_len // tk

    def body(q_ref, k_ref, v_ref, do_ref, lse_ref, dl_ref, dq_ref, dk_ref, dv_ref, dq_sc):
        i = pl.program_id(1)
        j = pl.program_id(2)

        @pl.when((i == 0) & (j == 0))
        def _():
            dk_ref[...] = jnp.zeros(dk_ref.shape, F32)
            dv_ref[...] = jnp.zeros(dv_ref.shape, F32)

        @pl.when(j == 0)
        def _():
            dq_sc[...] = jnp.zeros(dq_sc.shape, F32)

        kk = k_ref[...]
        vv = v_ref[...]
        keep = _head_masks(tq) if masked else None
        dk_t = jnp.zeros((tk, LANES), F32)
        dv_t = jnp.zeros((tk, LANES), F32)
        for hh in range(sub):
            if masked:
                cols = slice(LANES * (hh // 2), LANES * (hh // 2 + 1))
                qp = q_ref[:, cols]
                dop = do_ref[:, cols]
                qm = jnp.where(keep[hh % 2], qp, jnp.zeros_like(qp))
                dom = jnp.where(keep[hh % 2], dop, jnp.zeros_like(dop))
            else:
                cols = slice(0, LANES)
                qm = q_ref[...]
                dom = do_ref[...]
            s_t = _dot_nt(kk, qm)
            if scale is not None:
                s_t = s_t * scale
            p_t = jnp.exp(s_t - lse_ref[hh])
            dp_t = _dot_nt(vv, dom)
            ds_t = p_t * (dp_t - dl_ref[hh])
            if scale is not None:
                ds_t = ds_t * scale
            p_b = p_t.astype(BF16)
            ds_b = ds_t.astype(BF16)
            dv_t = dv_t + _dot(p_b, dom)
            dk_t = dk_t + _dot(ds_b, qm)
            dq_h = _dot_tn(ds_b, kk)
            if masked:
                dq_h = jnp.where(keep[hh % 2], dq_h, jnp.zeros_like(dq_h))
            dq_sc[:, cols] += dq_h
        rows = pl.ds(pl.multiple_of(j * tk, tk), tk)
        dk_ref[rows, :] += dk_t
        dv_ref[rows, :] += dv_t

        @pl.when(j == n_k - 1)
        def _():
            dq_ref[...] = dq_sc[...]

    return pl.pallas_call(
        body, name=name, grid=(groups, s_len // tq, n_k),
        in_specs=[pl.BlockSpec((tq, qw), lambda g, i, j: (i, g)),
                  pl.BlockSpec((tk, LANES), lambda g, i, j: (j, g)),
                  pl.BlockSpec((tk, LANES), lambda g, i, j: (j, g)),
                  pl.BlockSpec((tq, qw), lambda g, i, j: (i, g)),
                  pl.BlockSpec((sub, 1, tq), lambda g, i, j: (g, 0, i)),
                  pl.BlockSpec((sub, 1, tq), lambda g, i, j: (g, 0, i))],
        out_specs=[pl.BlockSpec((tq, qw), lambda g, i, j: (i, g)),
                   pl.BlockSpec((s_len, LANES), lambda g, i, j: (0, g)),
                   pl.BlockSpec((s_len, LANES), lambda g, i, j: (0, g))],
        out_shape=[jax.ShapeDtypeStruct((s_len, groups * qw), F32),
                   jax.ShapeDtypeStruct((s_len, groups * LANES), F32),
                   jax.ShapeDtypeStruct((s_len, groups * LANES), F32)],
        scratch_shapes=[pltpu.VMEM((tq, qw), F32)],
        compiler_params=_params(("arbitrary", "arbitrary", "arbitrary")),
    )(q, k, v, do, lse_t, delta_t)


def _silu_parts(g):
    sig = 1.0 / (1.0 + jnp.exp(-g))
    return g * sig, sig * (1.0 + g * (1.0 - sig))


def _out_call(x, target, oa, ob, ga, gb, wout, g64, tm):
    s_len = x.shape[0]
    n_t = s_len // tm

    def body(x_ref, t_ref, oa_ref, ob_ref, ga_ref, gb_ref, w_ref, g64_ref,
             dh_ref, doa_ref, dob_ref, dga_ref, dgb_ref, dla_ref, dlb_ref, dw_ref, loss_ref):
        i = pl.program_id(0)

        @pl.when(i == 0)
        def _():
            dw_ref[...] = jnp.zeros(dw_ref.shape, F32)
            loss_ref[...] = jnp.zeros(loss_ref.shape, F32)

        oa_v, ob_v = oa_ref[...], ob_ref[...]
        silu_a, dsilu_a = _silu_parts(ga_ref[...])
        silu_b, dsilu_b = _silu_parts(gb_ref[...])
        ya = (oa_v * silu_a).astype(BF16)
        yb = (ob_v * silu_b).astype(BF16)
        h = x_ref[...] + _dot(ya, w_ref[0:512, :]) + _dot(yb, w_ref[512:1024, :])
        err = h - t_ref[...]
        part = jnp.sum(err * err, axis=0, keepdims=True)
        acc = part[:, 0:LANES]
        for c in range(1, D_MODEL // LANES):
            acc = acc + part[:, LANES * c:LANES * (c + 1)]
        loss_ref[...] += acc
        dh = err * (1.0 / D_MODEL)
        dh_ref[...] = dh
        dhb = dh.astype(BF16)
        dya = _dot_nt(dhb, w_ref[0:512, :])
        dyb = _dot_nt(dhb, w_ref[512:1024, :])
        doa = dya * silu_a
        dob = dyb * silu_b
        doa_ref[...] = doa.astype(BF16)
        dob_ref[...] = dob.astype(BF16)
        dga_ref[...] = dya * oa_v * dsilu_a
        dgb_ref[...] = dyb * ob_v * dsilu_b
        ta = doa * oa_v
        tb = dob * ob_v
        g64 = g64_ref[...]
        for s in range(4):
            dla_ref[:, LANES * s:LANES * (s + 1)] = _group_sum64(ta[:, LANES * s:LANES * (s + 1)], g64)
            dlb_ref[:, LANES * s:LANES * (s + 1)] = jnp.broadcast_to(
                _row_sum(tb[:, LANES * s:LANES * (s + 1)]), (tm, LANES))
        dw_ref[0:512, :] += _dot_tn(ya, dhb)
        dw_ref[512:1024, :] += _dot_tn(yb, dhb)

    def rows(width):
        return pl.BlockSpec((tm, width), lambda i: (i, 0))

    outs = [(D_MODEL, F32), (512, BF16), (512, BF16), (512, F32), (512, F32), (512, F32), (512, F32)]
    return pl.pallas_call(
        body, name="out_fwd", grid=(n_t,),
        in_specs=[rows(D_MODEL), rows(D_MODEL), rows(512), rows(512), rows(512), rows(512),
                  _full((D_MODEL, D_MODEL)), _full((LANES, LANES))],
        out_specs=[rows(wd) for wd, _ in outs] + [_full((D_MODEL, D_MODEL)), _full((1, LANES))],
        out_shape=[jax.ShapeDtypeStruct((s_len, wd), dt) for wd, dt in outs]
        + [jax.ShapeDtypeStruct((D_MODEL, D_MODEL), F32), jax.ShapeDtypeStruct((1, LANES), F32)],
        compiler_params=_params(("arbitrary",)),
    )(x, target, oa, ob, ga, gb, wout, g64)


def _pre_bwd_call(x, dh, dqa, dka, dva, dga, dqb, dkb, dvb, dgb, consts, tabs, tm):
    s_len = x.shape[0]

    def body(x_ref, dh_ref, dqa_ref, dka_ref, dva_ref, dga_ref, dqb_ref, dkb_ref, dvb_ref, dgb_ref,
             ca_ref, sa_ref, cb_ref, sb_ref, gin_ref, w_ref, wuq_ref, wuk_ref, wuv_ref, gq_ref, gk_ref,
             gcq_ref, gckv_ref, gqb_ref, gkb_ref, g64_ref,
             dx_ref, dproj_ref, xnb_ref, dwuq_ref, dwuk_ref, dwuv_ref, small_ref):
        i = pl.program_id(0)

        @pl.when(i == 0)
        def _():
            dwuq_ref[...] = jnp.zeros(dwuq_ref.shape, F32)
            dwuk_ref[...] = jnp.zeros(dwuk_ref.shape, F32)
            dwuv_ref[...] = jnp.zeros(dwuv_ref.shape, F32)
            small_ref[...] = jnp.zeros(small_ref.shape, F32)

        gin, gq, gk = gin_ref[...], gq_ref[...], gk_ref[...]
        gcq, gckv, gqb, gkb = gcq_ref[...], gckv_ref[...], gqb_ref[...], gkb_ref[...]
        ca, sa, cb, sb = ca_ref[...], sa_ref[...], cb_ref[...], sb_ref[...]
        w, wuq, wuk, wuv = w_ref[...], wuq_ref[...], wuk_ref[...], wuv_ref[...]
        f = _pre_forward(x_ref[...], gin, w, wuq, wuk, wuv, gq, gk, gcq, gckv, gqb, gkb,
                         ca, sa, cb, sb, g64_ref[...], tm)
        sel16, sel8, gs64 = f["sel16"], f["sel8"], f["gs64"]
        lane = lax.broadcasted_iota(jnp.int32, (tm, LANES), 1)
        low = lane < 64
        zero = jnp.zeros((tm, LANES), F32)
        pieces = []

        dgq = jnp.zeros((1, LANES), F32)
        for s in range(4):
            _, xh, r = f["qa"][s]
            d = dqa_ref[:, LANES * s:LANES * (s + 1)] * QA_SCALE
            dx, dg = _col_bwd(d, xh, r, gs64, 64.0, gq, ca, sa, 16, sel16)
            pieces.append(dx)
            dgq = dgq + dg
        dgk = jnp.zeros((1, LANES), F32)
        for s in range(2):
            _, xh, r = f["ka"][s]
            d = dka_ref[:, LANES * s:LANES * (s + 1)]
            d = d + pltpu.roll(d, 64, 1)
            dx, dg = _col_bwd(d, xh, r, _row_sum, 128.0, gk, ca, sa, 16, sel16)
            pieces.append(jnp.where(low, dx, zero))
            dgk = dgk + dg
        for s in range(2):
            d = dva_ref[:, LANES * s:LANES * (s + 1)]
            d = d + pltpu.roll(d, 64, 1)
            pieces.append(jnp.where(low, d, zero))
        pieces.append(dga_ref[...])

        dgqb = jnp.zeros((1, LANES), F32)
        dq_cols = []
        for h in range(B_HEADS):
            _, xh, r = f["qb"][h]
            dx, dg = _col_bwd(dqb_ref[:, LANES * h:LANES * (h + 1)] * QB_SCALE, xh, r, _row_sum, float(B_QK_DIM),
                              gqb, cb, sb, 8, sel8)
            dq_cols.append(dx)
            dgqb = dgqb + dg
        dqr_b = jnp.concatenate(dq_cols, axis=1).astype(BF16)
        dwuq_ref[...] += _dot_tn(f["cqb"], dqr_b)
        dcq_raw, dgcq = _rms_bwd(_dot_nt(dqr_b, wuq), f["cqh"], f["rcq"], gcq)
        pieces.append(dcq_raw)

        dgkb = jnp.zeros((1, LANES), F32)
        dk_cols = []
        dkr = zero
        for h in range(B_HEADS):
            _, xh, r = f["kb"][h]
            dx, dg = _col_bwd(dkb_ref[:, LANES * h:LANES * (h + 1)], xh, r, _row_sum, float(B_QK_DIM),
                              gkb, cb, sb, 8, sel8)
            dk_cols.append(dx)
            dkr = dkr + dx
            dgkb = dgkb + dg
        dkr_b = jnp.concatenate(dk_cols, axis=1).astype(BF16)
        dvb_b = dvb_ref[...].astype(BF16)
        dwuk_ref[...] += _dot_tn(f["ckvb"], dkr_b)
        dwuv_ref[...] += _dot_tn(f["ckvb"], dvb_b)
        dckv = _dot_nt(dkr_b, wuk) + _dot_nt(dvb_b, wuv)
        dckv_raw, dgckv = _rms_bwd(dckv, f["ckvh"], f["rckv"], gckv)
        pieces.append(dckv_raw)
        pieces.append(jnp.where((lane >= B_NOPE_DIM) & (lane < B_QK_DIM), dkr, zero))
        pieces += [zero, zero, zero]
        pieces.append(dgb_ref[...])

        dproj_b = jnp.concatenate(pieces, axis=1).astype(BF16)
        dproj_ref[...] = dproj_b
        xnb_ref[...] = f["xnb"]
        dxn = _dot_nt(dproj_b, w)
        dx, dgin = _rms_bwd(dxn, f["xh0"], f["r0"], gin)
        dx_ref[...] = dx + dh_ref[...]

        for c in range(D_MODEL // LANES):
            small_ref[c:c + 1, :] += dgin[:, LANES * c:LANES * (c + 1)]
        small_ref[8:9, :] += dgq
        small_ref[9:10, :] += dgk
        for c in range(3):
            small_ref[10 + c:11 + c, :] += dgcq[:, LANES * c:LANES * (c + 1)]
        for c in range(2):
            small_ref[13 + c:14 + c, :] += dgckv[:, LANES * c:LANES * (c + 1)]
        small_ref[15:16, :] += dgqb
        small_ref[16:17, :] += dgkb

    def rows(width):
        return pl.BlockSpec((tm, width), lambda i: (i, 0))

    return pl.pallas_call(
        body, name="pre_bwd", grid=(s_len // tm,),
        in_specs=[rows(D_MODEL), rows(D_MODEL), rows(512), rows(256), rows(256), rows(512), rows(512), rows(512),
                  rows(512), rows(512)] + [rows(LANES)] * 4 + _pre_const_specs(consts),
        out_specs=[rows(D_MODEL), rows(N_WIDE), rows(D_MODEL), _full((B_Q_RANK, 512)), _full((B_KV_RANK, 512)),
                   _full((B_KV_RANK, 512)), _full((R_SMALL, LANES))],
        out_shape=[jax.ShapeDtypeStruct((s_len, D_MODEL), F32), jax.ShapeDtypeStruct((s_len, N_WIDE), BF16),
                   jax.ShapeDtypeStruct((s_len, D_MODEL), BF16), jax.ShapeDtypeStruct((B_Q_RANK, 512), F32),
                   jax.ShapeDtypeStruct((B_KV_RANK, 512), F32), jax.ShapeDtypeStruct((B_KV_RANK, 512), F32),
                   jax.ShapeDtypeStruct((R_SMALL, LANES), F32)],
        compiler_params=_params(("arbitrary",)),
    )(x, dh, dqa, dka, dva, dga, dqb, dkb, dvb, dgb, *tabs, *[consts[n] for n in _PRE_IN_NAMES])


def _dw_in_call(xnb, dproj_b, tt, tn):
    s_len = xnb.shape[0]

    def body(a_ref, b_ref, o_ref):
        @pl.when(pl.program_id(1) == 0)
        def _():
            o_ref[...] = jnp.zeros(o_ref.shape, F32)

        o_ref[...] += _dot_tn(a_ref[...], b_ref[...])

    return pl.pallas_call(
        body, name="dw_in", grid=(N_WIDE // tn, s_len // tt),
        in_specs=[pl.BlockSpec((tt, D_MODEL), lambda n, t: (t, 0)), pl.BlockSpec((tt, tn), lambda n, t: (t, n))],
        out_specs=pl.BlockSpec((D_MODEL, tn), lambda n, t: (0, n)),
        out_shape=jax.ShapeDtypeStruct((D_MODEL, N_WIDE), F32),
        compiler_params=_params(("arbitrary", "arbitrary")),
    )(xnb, dproj_b)


def _mesh_pos():
    return lax.axis_index("x"), lax.axis_index("y"), lax.axis_index("c")


def _flip(v, bit):
    return 1 - v if bit else v


def _peer(pos, k):
    x, y, c = pos
    return _flip(x, (k >> 2) & 1), _flip(y, (k >> 1) & 1), _flip(c, k & 1)


def _logical(p):
    return 4 * p[0] + 2 * p[1] + p[2]


def _gather_weights_call(shard):
    m_per = shard.shape[0]

    def body(x_ref, out_ref, xb_ref, send_sems, recv_sems, local_sem):
        x, y, c = _mesh_pos()
        me, sibling = (x, y, c), (x, y, 1 - c)
        chips = [(1 - x, y), (x, 1 - y), (1 - x, 1 - y)]
        xb_ref[...] = x_ref[...].astype(BF16)

        def rows(p):
            return out_ref.at[pl.ds(pl.multiple_of(_logical(p) * m_per, 16), m_per), :]

        def copy(k, block, to, src=None):
            return pltpu.make_async_remote_copy(
                src_ref=rows(block) if src is None else src, dst_ref=rows(block),
                send_sem=send_sems.at[k], recv_sem=recv_sems.at[k],
                device_id=to, device_id_type=pl.DeviceIdType.MESH)

        mine = pltpu.make_async_copy(xb_ref, rows(me), local_sem)
        mine.start()
        first = [copy(0, me, sibling, src=xb_ref)]
        first += [copy(1 + j, me, (*chip, c), src=xb_ref) for j, chip in enumerate(chips)]
        for cp in first:
            cp.start()
        passed = [copy(4 + j, (*chip, c), sibling) for j, chip in enumerate(chips)]
        for j, chip in enumerate(chips):
            copy(1 + j, (*chip, c), me).wait_recv()
            passed[j].start()
        copy(0, sibling, me).wait_recv()
        for j, chip in enumerate(chips):
            copy(4 + j, (*chip, 1 - c), me).wait_recv()
        for cp in first + passed:
            cp.wait_send()
        mine.wait()

    return pl.pallas_call(
        body, name="gather_weights",
        out_shape=jax.ShapeDtypeStruct((N_DEV * m_per, LANES), BF16),
        in_specs=[pl.BlockSpec(memory_space=pltpu.VMEM)],
        out_specs=pl.BlockSpec(memory_space=pltpu.VMEM),
        scratch_shapes=[pltpu.VMEM((m_per, LANES), BF16), pltpu.SemaphoreType.DMA((7,)),
                        pltpu.SemaphoreType.DMA((7,)), pltpu.SemaphoreType.DMA],
        compiler_params=pltpu.CompilerParams(vmem_limit_bytes=VMEM_LIMIT),
    )(shard)


def _adamw(w, g, m, v):
    m = ADAM_B1 * m + (1.0 - ADAM_B1) * g
    v = ADAM_B2 * v + (1.0 - ADAM_B2) * (g * g)
    m_hat = m / (1.0 - ADAM_B1 ** ADAM_STEP)
    v_hat = v / (1.0 - ADAM_B2 ** ADAM_STEP)
    delta = -ADAM_LR * (m_hat / (jnp.sqrt(v_hat) + ADAM_EPS) + ADAM_WD * w)
    return delta, m, v


def _reduce_adamw_call(parts, small, w_pk, m_pk, v_pk, w_s, m_s, v_s):
    chunk = 16
    n_chunks = R_PACK // chunk

    def body(parts_ref, small_ref, w_ref, m_ref, v_ref, ws_ref, ms_ref, vs_ref,
             g_ref, d_ref, nm_ref, nv_ref, gs_ref, ds_ref, nms_ref, nvs_ref,
             recv_ref, recv_s_ref, send_sems, recv_sems, send_s_sems, recv_s_sems, local_sem):
        pos = _mesh_pos()
        me = _logical(pos)

        def big(k):
            peer = _peer(pos, k)
            return pltpu.make_async_remote_copy(
                src_ref=parts_ref.at[_logical(peer)], dst_ref=recv_ref.at[k],
                send_sem=send_sems.at[k], recv_sem=recv_sems.at[k],
                device_id=peer, device_id_type=pl.DeviceIdType.MESH)

        def tiny(k):
            return pltpu.make_async_remote_copy(
                src_ref=small_ref, dst_ref=recv_s_ref.at[k],
                send_sem=send_s_sems.at[k], recv_sem=recv_s_sems.at[k],
                device_id=_peer(pos, k), device_id_type=pl.DeviceIdType.MESH)

        own = pltpu.make_async_copy(parts_ref.at[me], recv_ref.at[0], local_sem)
        own.start()
        for k in range(1, N_DEV):
            tiny(k).start()
        for k in range(1, N_DEV):
            big(k).start()
        recv_s_ref[0] = small_ref[...]
        for k in range(1, N_DEV):
            tiny(k).wait_recv()
        acc = recv_s_ref[me]
        for a in range(1, N_DEV):
            acc = acc + recv_s_ref[lax.bitwise_xor(me, a)]
        row = lax.broadcasted_iota(jnp.int32, (R_SMALL, LANES), 0)
        gs = jnp.where(row == 8, acc + pltpu.roll(acc, 64, 1), acc)
        gs = jnp.where(row == ROW_LOSS, jnp.sum(acc, axis=1, keepdims=True) * (0.5 / D_MODEL), gs)
        gs_ref[...] = gs
        ds, nms, nvs = _adamw(ws_ref[...], gs, ms_ref[...], vs_ref[...])
        ds_ref[...] = ds
        nms_ref[...] = nms
        nvs_ref[...] = nvs

        own.wait()
        for k in range(1, N_DEV):
            big(k).wait_recv()

        def step(t, carry):
            rows = pl.ds(pl.multiple_of(t * chunk, chunk), chunk)
            g = recv_ref[0, rows, :]
            for k in range(1, N_DEV):
                g = g + recv_ref[k, rows, :]
            d, nm, nv = _adamw(w_ref[rows, :], g, m_ref[rows, :], v_ref[rows, :])
            g_ref[rows, :] = g
            d_ref[rows, :] = d
            nm_ref[rows, :] = nm
            nv_ref[rows, :] = nv
            return carry

        lax.fori_loop(0, n_chunks, step, 0)
        for k in range(1, N_DEV):
            tiny(k).wait_send()
            big(k).wait_send()

    vm = pl.BlockSpec(memory_space=pltpu.VMEM)
    big_shape = jax.ShapeDtypeStruct((R_PACK, LANES), F32)
    small_shape = jax.ShapeDtypeStruct((R_SMALL, LANES), F32)
    return pl.pallas_call(
        body, name="reduce_adamw",
        in_specs=[pl.BlockSpec(memory_space=pl.ANY)] + [vm] * 7,
        out_specs=[vm] * 8,
        out_shape=[big_shape] * 4 + [small_shape] * 4,
        scratch_shapes=[pltpu.VMEM((N_DEV, R_PACK, LANES), F32), pltpu.VMEM((N_DEV, R_SMALL, LANES), F32),
                        pltpu.SemaphoreType.DMA((N_DEV,)), pltpu.SemaphoreType.DMA((N_DEV,)),
                        pltpu.SemaphoreType.DMA((N_DEV,)), pltpu.SemaphoreType.DMA((N_DEV,)),
                        pltpu.SemaphoreType.DMA],
        compiler_params=pltpu.CompilerParams(vmem_limit_bytes=VMEM_LIMIT),
    )(parts, small, w_pk, m_pk, v_pk, w_s, m_s, v_s)


def _pack_shard(w_in, w_uq, w_ukv, w_out):
    return jnp.concatenate([a.reshape(-1, LANES) for a in (w_in, w_uq, w_ukv, w_out)], axis=0)


def _unpack_shard(p):
    w_in = p[0:R_WIN].reshape(1, D_MODEL, N_IN // N_DEV)
    w_uq = p[R_WIN:R_WIN + R_WUQ].reshape(1, B_Q_RANK // N_DEV, 384)
    w_ukv = p[R_WIN + R_WUQ:R_WIN + R_WUQ + R_WUKV].reshape(1, B_KV_RANK, 768 // N_DEV)
    w_out = p[R_WIN + R_WUQ + R_WUKV:].reshape(1, D_MODEL // N_DEV, D_MODEL)
    return w_in, w_uq, w_ukv, w_out


def _pack_small(norm_in, a_q, a_k, b_cq, b_ckv, b_q, b_k):
    def row(v):
        return jnp.pad(v.reshape(1, -1), ((0, 0), (0, LANES - v.size)))
    rows = [norm_in.reshape(8, LANES), row(a_q), row(a_k), b_cq.reshape(3, LANES), b_ckv.reshape(2, LANES),
            row(b_q), row(b_k), jnp.zeros((R_SMALL - 17, LANES), F32)]
    return jnp.concatenate(rows, axis=0)


def _unpack_small(s):
    return (s[0:8].reshape(1, D_MODEL), s[8:9, :64], s[9:10, :64], s[10:13].reshape(1, B_Q_RANK),
            s[13:15].reshape(1, B_KV_RANK), s[15:16, :B_QK_DIM], s[16:17, :B_QK_DIM])


def _full_weights(gathered):
    g = gathered
    w_in = g[:, 0:R_WIN].reshape(N_DEV, D_MODEL, N_IN // N_DEV).transpose(1, 0, 2).reshape(D_MODEL, N_IN)
    w_uq = g[:, R_WIN:R_WIN + R_WUQ].reshape(B_Q_RANK, 384)
    w_ukv = g[:, R_WIN + R_WUQ:R_WIN + R_WUQ + R_WUKV].reshape(N_DEV, B_KV_RANK, 768 // N_DEV)
    w_ukv = w_ukv.transpose(1, 0, 2).reshape(B_KV_RANK, 768)
    w_out = g[:, R_WIN + R_WUQ + R_WUKV:].reshape(D_MODEL, D_MODEL)
    k0, k1 = w_in[:, 512:576], w_in[:, 576:640]
    v0, v1 = w_in[:, 640:704], w_in[:, 704:768]
    kr = w_in[:, 1920:1952]
    z64 = jnp.zeros((D_MODEL, 64), BF16)
    z32 = jnp.zeros((D_MODEL, 32), BF16)
    kr_blk = jnp.concatenate([z64, kr, z32], axis=1)
    w_wide = jnp.concatenate([w_in[:, 0:512], k0, k0, k1, k1, v0, v0, v1, v1, w_in[:, 768:1280], w_in[:, 1280:1664],
                              w_in[:, 1664:1920], kr_blk, kr_blk, kr_blk, kr_blk, w_in[:, 1952:2464]], axis=1)
    wuq = jnp.pad(w_uq.reshape(B_Q_RANK, B_HEADS, B_QK_DIM), ((0, 0), (0, 0), (0, LANES - B_QK_DIM)))
    wuq = wuq.reshape(B_Q_RANK, 512)
    ukv = w_ukv.reshape(B_KV_RANK, B_HEADS, B_NOPE_DIM + B_V_DIM)
    wuk = jnp.pad(ukv[:, :, :B_NOPE_DIM], ((0, 0), (0, 0), (0, LANES - B_NOPE_DIM))).reshape(B_KV_RANK, 512)
    wuv = ukv[:, :, B_NOPE_DIM:].reshape(B_KV_RANK, 512)
    return w_wide, wuq, wuk, wuv, w_out


def _narrow_grads(dw_wide, dwuq, dwuk, dwuv):
    dw_in = jnp.concatenate([
        dw_wide[:, 0:512], dw_wide[:, 512:576], dw_wide[:, 640:704], dw_wide[:, 768:832], dw_wide[:, 896:960],
        dw_wide[:, 1024:1536], dw_wide[:, 1536:1920], dw_wide[:, 1920:2176],
        dw_wide[:, O_KR + 64:O_KR + 96], dw_wide[:, 2688:3200]], axis=1)
    dw_uq = dwuq.reshape(B_Q_RANK, B_HEADS, LANES)[:, :, :B_QK_DIM].reshape(B_Q_RANK, 384)
    dk = dwuk.reshape(B_KV_RANK, B_HEADS, LANES)[:, :, :B_NOPE_DIM]
    dv = dwuv.reshape(B_KV_RANK, B_HEADS, B_V_DIM)
    dw_ukv = jnp.concatenate([dk, dv], axis=2).reshape(B_KV_RANK, 768)
    return dw_in, dw_uq, dw_ukv


def _rope_tables(s_len):
    rows = s_len // GRID_W
    row = jnp.repeat(jnp.arange(rows, dtype=F32), GRID_W)
    col = jnp.tile(jnp.arange(GRID_W, dtype=F32), rows)

    def tables(dim):
        half = dim // 2
        inv = 1.0 / (ROPE_THETA ** (jnp.arange(0, half, 2, dtype=F32) / half))
        ar, ac = row[:, None] * inv[None, :], col[:, None] * inv[None, :]
        cos = jnp.concatenate([jnp.cos(ar), jnp.cos(ar), jnp.cos(ac), jnp.cos(ac)], axis=1)
        sin = jnp.concatenate([-jnp.sin(ar), jnp.sin(ar), -jnp.sin(ac), jnp.sin(ac)], axis=1)
        return cos, sin

    ca, sa = tables(A_HEAD_DIM)
    cb, sb = tables(B_ROPE_DIM)
    ca, sa = jnp.tile(ca, (1, 2)), jnp.tile(sa, (1, 2))
    cb = jnp.concatenate([jnp.ones((s_len, 64), F32), cb, jnp.ones((s_len, 32), F32)], axis=1)
    sb = jnp.concatenate([jnp.zeros((s_len, 64), F32), sb, jnp.zeros((s_len, 32), F32)], axis=1)
    return ca, sa, cb, sb


def _lane_major(a, step):
    return a[:, ::step].T[:, None, :]


def kernel(x, norm_in, w_in, a_q_norm, a_k_norm, b_cq_norm, b_ckv_norm, w_uq, w_ukv, b_q_norm, b_k_norm, w_out, loss_target, m_norm_in, m_w_in, m_a_q_norm, m_a_k_norm, m_b_cq_norm, m_b_ckv_norm, m_w_uq, m_w_ukv, m_b_q_norm, m_b_k_norm, m_w_out, v_norm_in, v_w_in, v_a_q_norm, v_a_k_norm, v_b_cq_norm, v_b_ckv_norm, v_w_uq, v_w_ukv, v_b_q_norm, v_b_k_norm, v_w_out):
    s_len = x.shape[1]
    tm = min(256, s_len)
    tq, tk = min(512, s_len), min(2048, s_len)
    ftq, ftk = min(256, s_len), min(1024, s_len)
    x2 = x.reshape(s_len, D_MODEL)
    t2 = loss_target.reshape(s_len, D_MODEL)

    w_pk = _pack_shard(w_in[0], w_uq[0], w_ukv[0], w_out[0])
    gathered = _gather_weights_call(w_pk).reshape(N_DEV, R_PACK, LANES)
    w_wide, wuq, wuk, wuv, wout = _full_weights(gathered)

    def dup(v, pad_to=None):
        v = v.reshape(1, -1)
        if pad_to is None:
            return jnp.concatenate([v, v], axis=1)
        return jnp.pad(v, ((0, 0), (0, pad_to - v.shape[1])))

    g64 = jnp.asarray(np.kron(np.eye(2), np.ones((64, 64))), dtype=BF16)
    consts = dict(gin=norm_in, w=w_wide, wuq=wuq, wuk=wuk, wuv=wuv, gq=dup(a_q_norm), gk=dup(a_k_norm),
                  gcq=b_cq_norm, gckv=b_ckv_norm, gqb=dup(b_q_norm, LANES), gkb=dup(b_k_norm, LANES), g64=g64)
    tabs = _rope_tables(s_len)

    qa, ka, va, ga, qb, kb, vb, gb = _pre_fwd_call(x2, consts, tabs, tm)
    oa, lse_a_t = _attn_fwd_t_call(qa, ka, va.T, groups=A_KV_HEADS, sub=4, masked=True, scale=None, tq=ftq, tk=ftk,
                                   name="attn_fwd_a")
    ob, lse_b_t = _attn_fwd_t_call(qb, kb, vb.T, groups=1, sub=B_HEADS, masked=False, scale=None, tq=ftq,
                                   tk=ftk, name="attn_fwd_b")
    dh, doa, dob, dga, dgb, dl_a, dl_b, dw_out, loss_row = _out_call(x2, t2, oa, ob, ga, gb, wout, g64, tm)

    dqa, dka, dva = _attn_bwd_call(qa, ka, va, doa, lse_a_t, _lane_major(dl_a, 64), groups=A_KV_HEADS, sub=4,
                                   scale=None, tq=tq, tk=tk, name="attn_bwd_a")
    dqb, dkb, dvb = _attn_bwd_call(qb, kb, vb, dob, lse_b_t, _lane_major(dl_b, LANES), groups=B_HEADS, sub=1,
                                   scale=None, tq=tq, tk=tk, name="attn_bwd_b")
    dx, dproj_b, xnb, dwuq, dwuk, dwuv, small = _pre_bwd_call(
        x2, dh, dqa, dka, dva, dga, dqb, dkb, dvb, dgb, consts, tabs, tm)
    dw_wide = _dw_in_call(xnb, dproj_b, min(512, s_len), 640)

    dw_in, dw_uq, dw_ukv = _narrow_grads(dw_wide, dwuq, dwuk, dwuv)
    ci, cq, ckv, co = N_IN // N_DEV, B_Q_RANK // N_DEV, 768 // N_DEV, D_MODEL // N_DEV
    parts = jnp.stack([
        _pack_shard(dw_in[:, ci * d:ci * (d + 1)], dw_uq[cq * d:cq * (d + 1)], dw_ukv[:, ckv * d:ckv * (d + 1)],
                    dw_out[co * d:co * (d + 1)]) for d in range(N_DEV)])
    small = jnp.concatenate([small[:ROW_LOSS], loss_row, small[ROW_LOSS + 1:]], axis=0)

    m_pk = _pack_shard(m_w_in[0], m_w_uq[0], m_w_ukv[0], m_w_out[0])
    v_pk = _pack_shard(v_w_in[0], v_w_uq[0], v_w_ukv[0], v_w_out[0])
    w_s = _pack_small(norm_in, a_q_norm, a_k_norm, b_cq_norm, b_ckv_norm, b_q_norm, b_k_norm)
    m_s = _pack_small(m_norm_in, m_a_q_norm, m_a_k_norm, m_b_cq_norm, m_b_ckv_norm, m_b_q_norm, m_b_k_norm)
    v_s = _pack_small(v_norm_in, v_a_q_norm, v_a_k_norm, v_b_cq_norm, v_b_ckv_norm, v_b_q_norm, v_b_k_norm)
    g_pk, d_pk, nm_pk, nv_pk, g_s, d_s, nm_s, nv_s = _reduce_adamw_call(parts, small, w_pk, m_pk, v_pk, w_s, m_s, v_s)

    def leaves(pk, sm):
        wi, uq, ukv, wo = _unpack_shard(pk)
        n_in, aq, ak, bcq, bckv, bq, bk = _unpack_small(sm)
        return [n_in, wi, aq, ak, bcq, bckv, uq, ukv, bq, bk, wo]

    loss = g_s[ROW_LOSS, 0]
    grad_x = dx.reshape(1, s_len, D_MODEL)
    return (loss, grad_x, *leaves(g_pk, g_s), *leaves(d_pk, d_s), *leaves(nm_pk, nm_s), *leaves(nv_pk, nv_s))
```

```python
import functools

import numpy as np
import jax
import jax.numpy as jnp
from jax import lax
from jax.experimental import pallas as pl
from jax.experimental.pallas import tpu as pltpu

F32 = jnp.float32
BF16 = jnp.bfloat16

D_MODEL = 1024
GRID_W = 64
ROPE_THETA = 10000.0
EPS = 1e-6
A_HEAD_DIM = 64
A_HEADS = 8
A_KV_HEADS = 2
B_HEADS = 4
B_NOPE_DIM = 64
B_ROPE_DIM = 32
B_QK_DIM = 96
B_V_DIM = 128
B_Q_RANK = 384
B_KV_RANK = 256
N_IN = 2464
N_DEV = 8

ADAM_LR = 0.001
ADAM_B1 = 0.9
ADAM_B2 = 0.999
ADAM_EPS = 1e-08
ADAM_WD = 0.01
ADAM_STEP = 10

QA_SCALE = 0.125
QB_SCALE = 1.0 / float(np.sqrt(B_QK_DIM))

LANES = 128
O_QA, O_KA, O_VA, O_GA, O_CQ, O_CKV, O_KR, O_GB, N_WIDE = 0, 512, 768, 1024, 1536, 1920, 2176, 2688, 3200

R_WIN = D_MODEL * (N_IN // N_DEV) // LANES
R_WUQ = (B_Q_RANK // N_DEV) * 384 // LANES
R_WUKV = B_KV_RANK * (768 // N_DEV) // LANES
R_WOUT = (D_MODEL // N_DEV) * D_MODEL // LANES
R_PACK = R_WIN + R_WUQ + R_WUKV + R_WOUT
R_SMALL = 24
ROW_LOSS = 17

VMEM_LIMIT = 56 * 1024 * 1024

NT = (((1,), (1,)), ((), ()))
TN = (((0,), (0,)), ((), ()))


def _dot(a, b):
    return jnp.dot(a, b, preferred_element_type=F32)


def _dot_nt(a, b):
    return lax.dot_general(a, b, NT, preferred_element_type=F32)


def _dot_tn(a, b):
    return lax.dot_general(a, b, TN, preferred_element_type=F32)


def _params(sem=None):
    return pltpu.CompilerParams(dimension_semantics=sem, vmem_limit_bytes=VMEM_LIMIT)


def _full(shape):
    nd = len(shape)
    return pl.BlockSpec(shape, lambda *_: (0,) * nd)


def _swap_sel(rows, shift):
    lane = lax.broadcasted_iota(jnp.int32, (rows, LANES), 1)
    return pltpu.roll(lane, shift, 1) == (lane ^ shift)


def _swap(x, shift, sel):
    return jnp.where(sel, pltpu.roll(x, shift, 1), pltpu.roll(x, LANES - shift, 1))


def _group_sum64(x, g64):
    hi = x.astype(BF16)
    lo = (x - hi.astype(F32)).astype(BF16)
    return _dot(hi, g64) + _dot(lo, g64)


def _row_sum(x):
    return jnp.sum(x, axis=-1, keepdims=True)


def _col_fwd(xs, msum, denom, gain, cos, sin, shift, sel):
    r = lax.rsqrt(msum(xs * xs) * (1.0 / denom) + EPS)
    xh = xs * r
    n = xh * gain
    return n * cos + _swap(n, shift, sel) * sin, xh, r


def _col_bwd(d_out, xh, r, msum, denom, gain, cos, sin, shift, sel):
    dn = d_out * cos + _swap(d_out * sin, shift, sel)
    dgain = jnp.sum(dn * xh, axis=0, keepdims=True)
    dxh = dn * gain
    dx = r * (dxh - xh * (msum(dxh * xh) * (1.0 / denom)))
    return dx, dgain


def _rms_fwd(x, gain):
    r = lax.rsqrt(jnp.mean(x * x, axis=-1, keepdims=True) + EPS)
    xh = x * r
    return xh * gain, xh, r


def _rms_bwd(dy, xh, r, gain):
    dgain = jnp.sum(dy * xh, axis=0, keepdims=True)
    dxh = dy * gain
    dx = r * (dxh - xh * jnp.mean(dxh * xh, axis=-1, keepdims=True))
    return dx, dgain


def _pre_forward(x, gin, w, wuq, wuk, wuv, gq, gk, gcq, gckv, gqb, gkb, ca, sa, cb, sb, g64, tm):
    sel16 = _swap_sel(tm, 16)
    sel8 = _swap_sel(tm, 8)
    xn, xh0, r0 = _rms_fwd(x, gin)
    xnb = xn.astype(BF16)
    proj = _dot(xnb, w)
    gs64 = functools.partial(_group_sum64, g64=g64)
    qa = [_col_fwd(proj[:, O_QA + LANES * s:O_QA + LANES * (s + 1)], gs64, 64.0, gq, ca, sa, 16, sel16)
          for s in range(4)]
    ka = [_col_fwd(proj[:, O_KA + LANES * s:O_KA + LANES * (s + 1)], _row_sum, 128.0, gk, ca, sa, 16, sel16)
          for s in range(2)]
    cq, cqh, rcq = _rms_fwd(proj[:, O_CQ:O_CQ + B_Q_RANK], gcq)
    cqb = cq.astype(BF16)
    qb_raw = _dot(cqb, wuq)
    qb = [_col_fwd(qb_raw[:, LANES * h:LANES * (h + 1)], _row_sum, float(B_QK_DIM), gqb, cb, sb, 8, sel8)
          for h in range(B_HEADS)]
    ckv, ckvh, rckv = _rms_fwd(proj[:, O_CKV:O_CKV + B_KV_RANK], gckv)
    ckvb = ckv.astype(BF16)
    kb_raw = _dot(ckvb, wuk) + proj[:, O_KR:O_KR + 512]
    vb = _dot(ckvb, wuv)
    kb = [_col_fwd(kb_raw[:, LANES * h:LANES * (h + 1)], _row_sum, float(B_QK_DIM), gkb, cb, sb, 8, sel8)
          for h in range(B_HEADS)]
    return dict(xh0=xh0, r0=r0, xnb=xnb, proj=proj, qa=qa, ka=ka, cqh=cqh, rcq=rcq, cqb=cqb, qb=qb,
                ckvh=ckvh, rckv=rckv, ckvb=ckvb, kb=kb, vb=vb, sel16=sel16, sel8=sel8, gs64=gs64)


_PRE_IN_NAMES = ("gin", "w", "wuq", "wuk", "wuv", "gq", "gk", "gcq", "gckv", "gqb", "gkb", "g64")


def _pre_const_specs(consts):
    return [_full(consts[n].shape) for n in _PRE_IN_NAMES]


def _pre_fwd_call(x, consts, tabs, tm):
    s_len = x.shape[0]

    def body(x_ref, ca_ref, sa_ref, cb_ref, sb_ref, gin_ref, w_ref, wuq_ref, wuk_ref, wuv_ref, gq_ref, gk_ref,
             gcq_ref, gckv_ref, gqb_ref, gkb_ref, g64_ref,
             qa_ref, ka_ref, va_ref, ga_ref, qb_ref, kb_ref, vb_ref, gb_ref):
        f = _pre_forward(x_ref[...], gin_ref[...], w_ref[...], wuq_ref[...], wuk_ref[...], wuv_ref[...],
                         gq_ref[...], gk_ref[...], gcq_ref[...], gckv_ref[...], gqb_ref[...], gkb_ref[...],
                         ca_ref[...], sa_ref[...], cb_ref[...], sb_ref[...], g64_ref[...], tm)
        proj = f["proj"]
        for s in range(4):
            qa_ref[:, LANES * s:LANES * (s + 1)] = (f["qa"][s][0] * QA_SCALE).astype(BF16)
        for s in range(2):
            ka_ref[:, LANES * s:LANES * (s + 1)] = f["ka"][s][0].astype(BF16)
        va_ref[...] = proj[:, O_VA:O_VA + 256].astype(BF16)
        ga_ref[...] = proj[:, O_GA:O_GA + 512]
        for h in range(B_HEADS):
            qb_ref[:, LANES * h:LANES * (h + 1)] = (f["qb"][h][0] * QB_SCALE).astype(BF16)
            kb_ref[:, LANES * h:LANES * (h + 1)] = f["kb"][h][0].astype(BF16)
        vb_ref[...] = f["vb"].astype(BF16)
        gb_ref[...] = proj[:, O_GB:O_GB + 512]

    def rows(width):
        return pl.BlockSpec((tm, width), lambda i: (i, 0))

    outs = [(512, BF16), (256, BF16), (256, BF16), (512, F32), (512, BF16), (512, BF16), (512, BF16), (512, F32)]
    return pl.pallas_call(
        body, name="pre_fwd", grid=(s_len // tm,),
        in_specs=[rows(D_MODEL)] + [rows(LANES)] * 4 + _pre_const_specs(consts),
        out_specs=[rows(wd) for wd, _ in outs],
        out_shape=[jax.ShapeDtypeStruct((s_len, wd), dt) for wd, dt in outs],
        compiler_params=_params(("arbitrary",)),
    )(x, *tabs, *[consts[n] for n in _PRE_IN_NAMES])


def _head_masks(rows):
    lane = lax.broadcasted_iota(jnp.int32, (rows, LANES), 1)
    return lane < 64, lane >= 64


def _lane_fold(x, op):
    out = x[:, 0:LANES]
    for c in range(1, x.shape[1] // LANES):
        out = op(out, x[:, LANES * c:LANES * (c + 1)])
    return out


def _row_fold(x, op):
    return op(x.reshape(x.shape[0] // 8, 8, x.shape[1]), axis=0)


def _attn_fwd_t_call(q, k, vt, *, groups, sub, masked, scale, tq, tk, name):
    s_len = q.shape[0]
    qw = LANES * (sub // 2 if masked else sub)
    kvw = LANES if masked else LANES * sub
    n_c = s_len // tk
    kv_mode = pl.Buffered(1) if groups == 1 else None

    def body(q_ref, k_ref, vt_ref, o_ref, lse_ref, s_sc):
        keep = _head_masks(tq) if masked else None

        def kv_of(hh):
            return slice(0, LANES) if masked else slice(LANES * hh, LANES * (hh + 1))

        def q_of(hh):
            if not masked:
                return q_ref[:, LANES * hh:LANES * (hh + 1)]
            qp = q_ref[:, LANES * (hh // 2):LANES * (hh // 2 + 1)]
            return jnp.where(keep[hh % 2], qp, jnp.zeros_like(qp))

        def scores(hh, qm, c, mx):
            s_t = _dot_nt(k_ref[tk * c:tk * (c + 1), kv_of(hh)], qm)
            if scale is not None:
                s_t = s_t * scale
            s_sc[hh % 2, c] = s_t
            return jnp.maximum(mx, _row_fold(s_t, jnp.max))

        neg = jnp.full((8, tq), -jnp.inf, F32)
        qm_next = q_of(0)
        mx_next = neg
        for c in range(n_c):
            mx_next = scores(0, qm_next, c, mx_next)
        outs = []
        for hh in range(sub):
            m = jnp.max(mx_next, axis=0, keepdims=True)
            if hh + 1 < sub:
                qm_next = q_of(hh + 1)
                mx_next = neg
            lsum = jnp.zeros((8, tq), F32)
            acc = jnp.zeros((LANES, tq), F32)
            for c in range(n_c):
                if hh + 1 < sub:
                    mx_next = scores(hh + 1, qm_next, c, mx_next)
                p_t = jnp.exp(s_sc[hh % 2, c] - m)
                lsum = lsum + _row_fold(p_t, jnp.sum)
                acc = acc + _dot(vt_ref[kv_of(hh), tk * c:tk * (c + 1)], p_t.astype(BF16))
            l = jnp.sum(lsum, axis=0, keepdims=True)
            outs.append((acc / l).T)
            lse_ref[hh] = m + jnp.log(l)
        if masked:
            for pr in range(sub // 2):
                o_ref[:, LANES * pr:LANES * (pr + 1)] = jnp.where(keep[0], outs[2 * pr], outs[2 * pr + 1])
        else:
            for hh in range(sub):
                o_ref[:, LANES * hh:LANES * (hh + 1)] = outs[hh]

    return pl.pallas_call(
        body, name=name, grid=(groups, s_len // tq),
        in_specs=[pl.BlockSpec((tq, qw), lambda g, i: (i, g)),
                  pl.BlockSpec((s_len, kvw), lambda g, i: (0, g), pipeline_mode=kv_mode),
                  pl.BlockSpec((kvw, s_len), lambda g, i: (g, 0), pipeline_mode=kv_mode)],
        out_specs=[pl.BlockSpec((tq, qw), lambda g, i: (i, g)),
                   pl.BlockSpec((sub, 1, tq), lambda g, i: (g, 0, i))],
        out_shape=[jax.ShapeDtypeStruct((s_len, groups * qw), F32),
                   jax.ShapeDtypeStruct((groups * sub, 1, s_len), F32)],
        scratch_shapes=[pltpu.VMEM((min(sub, 2), n_c, tk, tq), F32)],
        compiler_params=_params(("arbitrary", "arbitrary")),
    )(q, k, vt)


def _attn_fwd_call(q, k, v, *, groups, sub, masked, scale, tq, tk, name):
    s_len = q.shape[0]
    qw = LANES * (sub // 2 if masked else sub)
    n_c = s_len // tk
    log2e = float(np.log2(np.e))
    mul = log2e if scale is None else scale * log2e

    def body(q_ref, k_ref, v_ref, o_ref, lse_ref, s_sc):
        keep = _head_masks(tq) if masked else None
        outs = []
        for hh in range(sub):
            if masked:
                qp = q_ref[:, LANES * (hh // 2):LANES * (hh // 2 + 1)]
                qm = jnp.where(keep[hh % 2], qp, jnp.zeros_like(qp))
                kv_cols = slice(0, LANES)
            else:
                qm = q_ref[:, LANES * hh:LANES * (hh + 1)]
                kv_cols = slice(LANES * hh, LANES * (hh + 1))
            buf = hh % 2

            def scores(c, mx):
                rows = pl.ds(pl.multiple_of(c * tk, tk), tk)
                s = _dot_nt(qm, k_ref[rows, kv_cols]) * mul
                s_sc[buf, c] = s
                return jnp.maximum(mx, _lane_fold(s, jnp.maximum))

            mx = lax.fori_loop(0, n_c, scores, jnp.full((tq, LANES), -jnp.inf, F32), unroll=True)
            m = jnp.max(mx, axis=1, keepdims=True)

            def weights(c, carry):
                lsum, acc = carry
                rows = pl.ds(pl.multiple_of(c * tk, tk), tk)
                p = jnp.exp2(s_sc[buf, c] - m)
                return lsum + _lane_fold(p, jnp.add), acc + _dot(p.astype(BF16), v_ref[rows, kv_cols])

            lsum, acc = lax.fori_loop(0, n_c, weights, (jnp.zeros((tq, LANES), F32), jnp.zeros((tq, LANES), F32)),
                                      unroll=True)
            l = jnp.sum(lsum, axis=1, keepdims=True)
            outs.append(acc / l)
            lse_ref[hh] = jnp.broadcast_to((m + jnp.log2(l)) * (1.0 / log2e), (tq, LANES))
        if masked:
            for pr in range(sub // 2):
                o_ref[:, LANES * pr:LANES * (pr + 1)] = jnp.where(keep[0], outs[2 * pr], outs[2 * pr + 1])
        else:
            for hh in range(sub):
                o_ref[:, LANES * hh:LANES * (hh + 1)] = outs[hh]

    kvw = LANES if masked else LANES * sub
    return pl.pallas_call(
        body, name=name, grid=(groups, s_len // tq),
        in_specs=[pl.BlockSpec((tq, qw), lambda g, i: (i, g)),
                  pl.BlockSpec((s_len, kvw), lambda g, i: (0, g)),
                  pl.BlockSpec((s_len, kvw), lambda g, i: (0, g))],
        out_specs=[pl.BlockSpec((tq, qw), lambda g, i: (i, g)),
                   pl.BlockSpec((sub, tq, LANES), lambda g, i: (g, i, 0))],
        out_shape=[jax.ShapeDtypeStruct((s_len, groups * qw), F32),
                   jax.ShapeDtypeStruct((groups * sub, s_len, LANES), F32)],
        scratch_shapes=[pltpu.VMEM((min(sub, 2), n_c, tq, tk), F32)],
        compiler_params=_params(("arbitrary", "arbitrary")),
    )(q, k, v)


def _attn_bwd_q_call(q, qt, k, v, do, dot_, o, lse, *, groups, sub, masked, tq, tk, ck, name):
    s_len = q.shape[0]
    qw = LANES * (sub // 2 if masked else sub)
    kvw = LANES if masked else LANES * sub
    n_c = tk // ck

    def body(q_ref, qt_ref, k_ref, v_ref, do_ref, dot_ref, o_ref, lse_ref, dq_ref, dkt_ref, dvt_ref):
        j = pl.program_id(1)
        i = pl.program_id(2)

        @pl.when((j == 0) & (i == 0))
        def _():
            dq_ref[...] = jnp.zeros(dq_ref.shape, F32)

        @pl.when(i == 0)
        def _():
            dkt_ref[...] = jnp.zeros(dkt_ref.shape, F32)
            dvt_ref[...] = jnp.zeros(dvt_ref.shape, F32)

        if masked:
            lkeep = _head_masks(tq)
            row = lax.broadcasted_iota(jnp.int32, (LANES, tq), 0)
            rkeep = (row < 64, row >= 64)
        heads = []
        for hh in range(sub):
            if masked:
                cols = slice(LANES * (hh // 2), LANES * (hh // 2 + 1))
                kv = slice(0, LANES)
                par = hh % 2
                qp, dop, qtp, dotp = q_ref[:, cols], do_ref[:, cols], qt_ref[cols, :], dot_ref[cols, :]
                qm = jnp.where(lkeep[par], qp, jnp.zeros_like(qp))
                dom = jnp.where(lkeep[par], dop, jnp.zeros_like(dop))
                qmt = jnp.where(rkeep[par], qtp, jnp.zeros_like(qtp))
                domt = jnp.where(rkeep[par], dotp, jnp.zeros_like(dotp))
            else:
                cols = kv = slice(LANES * hh, LANES * (hh + 1))
                qm, dom, qmt, domt = q_ref[:, cols], do_ref[:, cols], qt_ref[cols, :], dot_ref[cols, :]
            delta = jnp.sum(dom.astype(F32) * o_ref[:, cols], axis=1, keepdims=True)
            heads.append((cols, kv, qm, dom, qmt, domt, delta))

        def products(hh, c):
            _, kv, qm, dom, _, _, _ = heads[hh]
            return _dot_nt(qm, k_ref[ck * c:ck * (c + 1), kv]), _dot_nt(dom, v_ref[ck * c:ck * (c + 1), kv])

        items = [(hh, c) for hh in range(sub) for c in range(n_c)]
        dq_acc = [jnp.zeros((tq, LANES), F32) for _ in range(sub)]
        nxt = products(*items[0])
        for n, (hh, c) in enumerate(items):
            s, dp = nxt
            if n + 1 < len(items):
                nxt = products(*items[n + 1])
            _, kv, qm, dom, qmt, domt, delta = heads[hh]
            p = jnp.exp(s - lse_ref[hh])
            ds = p * (dp - delta)
            p_b = p.astype(BF16)
            ds_b = ds.astype(BF16)
            kcols = slice(ck * c, ck * (c + 1))
            dvt_ref[kv, kcols] += _dot(domt, p_b)
            dkt_ref[kv, kcols] += _dot(qmt, ds_b)
            dq_acc[hh] = dq_acc[hh] + _dot(ds_b, k_ref[kcols, kv])
        rows = pl.ds(pl.multiple_of(i * tq, tq), tq)
        if masked:
            for pr in range(sub // 2):
                dq_ref[rows, LANES * pr:LANES * (pr + 1)] += jnp.where(lkeep[0], dq_acc[2 * pr], dq_acc[2 * pr + 1])
        else:
            for hh in range(sub):
                dq_ref[rows, LANES * hh:LANES * (hh + 1)] += dq_acc[hh]

    return pl.pallas_call(
        body, name=name, grid=(groups, s_len // tk, s_len // tq),
        in_specs=[pl.BlockSpec((tq, qw), lambda g, j, i: (i, g)),
                  pl.BlockSpec((qw, tq), lambda g, j, i: (g, i)),
                  pl.BlockSpec((tk, kvw), lambda g, j, i: (j, g)),
                  pl.BlockSpec((tk, kvw), lambda g, j, i: (j, g)),
                  pl.BlockSpec((tq, qw), lambda g, j, i: (i, g)),
                  pl.BlockSpec((qw, tq), lambda g, j, i: (g, i)),
                  pl.BlockSpec((tq, qw), lambda g, j, i: (i, g)),
                  pl.BlockSpec((sub, tq, 1), lambda g, j, i: (g, i, 0))],
        out_specs=[pl.BlockSpec((s_len, qw), lambda g, j, i: (0, g)),
                   pl.BlockSpec((kvw, tk), lambda g, j, i: (g, j)),
                   pl.BlockSpec((kvw, tk), lambda g, j, i: (g, j))],
        out_shape=[jax.ShapeDtypeStruct((s_len, groups * qw), F32),
                   jax.ShapeDtypeStruct((groups * kvw, s_len), F32),
                   jax.ShapeDtypeStruct((groups * kvw, s_len), F32)],
        compiler_params=_params(("arbitrary", "arbitrary", "arbitrary")),
    )(q, qt, k, v, do, dot_, o, lse)


def _attn_bwd_call(q, k, v, do, lse_t, delta_t, *, groups, sub, scale, tq, tk, name):
    s_len = q.shape[0]
    masked = sub > 1
    qw = LANES * (sub // 2 if masked else 1)
    n_k = s_len // tk

    def body(q_ref, k_ref, v_ref, do_ref, lse_ref, dl_ref, dq_ref, dk_ref, dv_ref, dq_sc):
        i = pl.program_id(1)
        j = pl.program_id(2)

        @pl.when((i == 0) & (j == 0))
        def _():
            dk_ref[...] = jnp.zeros(dk_ref.shape, F32)
            dv_ref[...] = jnp.zeros(dv_ref.shape, F32)

        @pl.when(j == 0)
        def _():
            dq_sc[...] = jnp.zeros(dq_sc.shape, F32)

        kk = k_ref[...]
        vv = v_ref[...]
        keep = _head_masks(tq) if masked else None
        dk_t = jnp.zeros((tk, LANES), F32)
        dv_t = jnp.zeros((tk, LANES), F32)
        for hh in range(sub):
            if masked:
                cols = slice(LANES * (hh // 2), LANES * (hh // 2 + 1))
                qp = q_ref[:, cols]
                dop = do_ref[:, cols]
                qm = jnp.where(keep[hh % 2], qp, jnp.zeros_like(qp))
                dom = jnp.where(keep[hh % 2], dop, jnp.zeros_like(dop))
            else:
                cols = slice(0, LANES)
                qm = q_ref[...]
                dom = do_ref[...]
            s_t = _dot_nt(kk, qm)
            if scale is not None:
                s_t = s_t * scale
            p_t = jnp.exp(s_t - lse_ref[hh])
            dp_t = _dot_nt(vv, dom)
            ds_t = p_t * (dp_t - dl_ref[hh])
            if scale is not None:
                ds_t = ds_t * scale
            p_b = p_t.astype(BF16)
            ds_b = ds_t.astype(BF16)
            dv_t = dv_t + _dot(p_b, dom)
            dk_t = dk_t + _dot(ds_b, qm)
            dq_h = _dot_tn(ds_b, kk)
            if masked:
                dq_h = jnp.where(keep[hh % 2], dq_h, jnp.zeros_like(dq_h))
            dq_sc[:, cols] += dq_h
        rows = pl.ds(pl.multiple_of(j * tk, tk), tk)
        dk_ref[rows, :] += dk_t
        dv_ref[rows, :] += dv_t

        @pl.when(j == n_k - 1)
        def _():
            dq_ref[...] = dq_sc[...]

    return pl.pallas_call(
        body, name=name, grid=(groups, s_len // tq, n_k),
        in_specs=[pl.BlockSpec((tq, qw), lambda g, i, j: (i, g)),
                  pl.BlockSpec((tk, LANES), lambda g, i, j: (j, g)),
                  pl.BlockSpec((tk, LANES), lambda g, i, j: (j, g)),
                  pl.BlockSpec((tq, qw), lambda g, i, j: (i, g)),
                  pl.BlockSpec((sub, 1, tq), lambda g, i, j: (g, 0, i)),
                  pl.BlockSpec((sub, 1, tq), lambda g, i, j: (g, 0, i))],
        out_specs=[pl.BlockSpec((tq, qw), lambda g, i, j: (i, g)),
                   pl.BlockSpec((s_len, LANES), lambda g, i, j: (0, g)),
                   pl.BlockSpec((s_len, LANES), lambda g, i, j: (0, g))],
        out_shape=[jax.ShapeDtypeStruct((s_len, groups * qw), F32),
                   jax.ShapeDtypeStruct((s_len, groups * LANES), F32),
                   jax.ShapeDtypeStruct((s_len, groups * LANES), F32)],
        scratch_shapes=[pltpu.VMEM((tq, qw), F32)],
        compiler_params=_params(("arbitrary", "arbitrary", "arbitrary")),
    )(q, k, v, do, lse_t, delta_t)


def _silu_parts(g):
    sig = 1.0 / (1.0 + jnp.exp(-g))
    return g * sig, sig * (1.0 + g * (1.0 - sig))


def _out_call(x, target, oa, ob, ga, gb, wout, g64, tm):
    s_len = x.shape[0]
    n_t = s_len // tm

    def body(x_ref, t_ref, oa_ref, ob_ref, ga_ref, gb_ref, w_ref, g64_ref,
             dh_ref, doa_ref, dob_ref, dga_ref, dgb_ref, dla_ref, dlb_ref, dw_ref, loss_ref):
        i = pl.program_id(0)

        @pl.when(i == 0)
        def _():
            dw_ref[...] = jnp.zeros(dw_ref.shape, F32)
            loss_ref[...] = jnp.zeros(loss_ref.shape, F32)

        oa_v, ob_v = oa_ref[...], ob_ref[...]
        silu_a, dsilu_a = _silu_parts(ga_ref[...])
        silu_b, dsilu_b = _silu_parts(gb_ref[...])
        ya = (oa_v * silu_a).astype(BF16)
        yb = (ob_v * silu_b).astype(BF16)
        h = x_ref[...] + _dot(ya, w_ref[0:512, :]) + _dot(yb, w_ref[512:1024, :])
        err = h - t_ref[...]
        part = jnp.sum(err * err, axis=0, keepdims=True)
        acc = part[:, 0:LANES]
        for c in range(1, D_MODEL // LANES):
            acc = acc + part[:, LANES * c:LANES * (c + 1)]
        loss_ref[...] += acc
        dh = err * (1.0 / D_MODEL)
        dh_ref[...] = dh
        dhb = dh.astype(BF16)
        dya = _dot_nt(dhb, w_ref[0:512, :])
        dyb = _dot_nt(dhb, w_ref[512:1024, :])
        doa = dya * silu_a
        dob = dyb * silu_b
        doa_ref[...] = doa.astype(BF16)
        dob_ref[...] = dob.astype(BF16)
        dga_ref[...] = dya * oa_v * dsilu_a
        dgb_ref[...] = dyb * ob_v * dsilu_b
        ta = doa * oa_v
        tb = dob * ob_v
        g64 = g64_ref[...]
        for s in range(4):
            dla_ref[:, LANES * s:LANES * (s + 1)] = _group_sum64(ta[:, LANES * s:LANES * (s + 1)], g64)
            dlb_ref[:, LANES * s:LANES * (s + 1)] = jnp.broadcast_to(
                _row_sum(tb[:, LANES * s:LANES * (s + 1)]), (tm, LANES))
        dw_ref[0:512, :] += _dot_tn(ya, dhb)
        dw_ref[512:1024, :] += _dot_tn(yb, dhb)

    def rows(width):
        return pl.BlockSpec((tm, width), lambda i: (i, 0))

    outs = [(D_MODEL, F32), (512, BF16), (512, BF16), (512, F32), (512, F32), (512, F32), (512, F32)]
    return pl.pallas_call(
        body, name="out_fwd", grid=(n_t,),
        in_specs=[rows(D_MODEL), rows(D_MODEL), rows(512), rows(512), rows(512), rows(512),
                  _full((D_MODEL, D_MODEL)), _full((LANES, LANES))],
        out_specs=[rows(wd) for wd, _ in outs] + [_full((D_MODEL, D_MODEL)), _full((1, LANES))],
        out_shape=[jax.ShapeDtypeStruct((s_len, wd), dt) for wd, dt in outs]
        + [jax.ShapeDtypeStruct((D_MODEL, D_MODEL), F32), jax.ShapeDtypeStruct((1, LANES), F32)],
        compiler_params=_params(("arbitrary",)),
    )(x, target, oa, ob, ga, gb, wout, g64)


def _pre_bwd_call(x, dh, dqa, dka, dva, dga, dqb, dkb, dvb, dgb, consts, tabs, tm):
    s_len = x.shape[0]

    def body(x_ref, dh_ref, dqa_ref, dka_ref, dva_ref, dga_ref, dqb_ref, dkb_ref, dvb_ref, dgb_ref,
             ca_ref, sa_ref, cb_ref, sb_ref, gin_ref, w_ref, wuq_ref, wuk_ref, wuv_ref, gq_ref, gk_ref,
             gcq_ref, gckv_ref, gqb_ref, gkb_ref, g64_ref,
             dx_ref, dproj_ref, xnb_ref, dwuq_ref, dwuk_ref, dwuv_ref, small_ref):
        i = pl.program_id(0)

        @pl.when(i == 0)
        def _():
            dwuq_ref[...] = jnp.zeros(dwuq_ref.shape, F32)
            dwuk_ref[...] = jnp.zeros(dwuk_ref.shape, F32)
            dwuv_ref[...] = jnp.zeros(dwuv_ref.shape, F32)
            small_ref[...] = jnp.zeros(small_ref.shape, F32)

        gin, gq, gk = gin_ref[...], gq_ref[...], gk_ref[...]
        gcq, gckv, gqb, gkb = gcq_ref[...], gckv_ref[...], gqb_ref[...], gkb_ref[...]
        ca, sa, cb, sb = ca_ref[...], sa_ref[...], cb_ref[...], sb_ref[...]
        w, wuq, wuk, wuv = w_ref[...], wuq_ref[...], wuk_ref[...], wuv_ref[...]
        f = _pre_forward(x_ref[...], gin, w, wuq, wuk, wuv, gq, gk, gcq, gckv, gqb, gkb,
                         ca, sa, cb, sb, g64_ref[...], tm)
        sel16, sel8, gs64 = f["sel16"], f["sel8"], f["gs64"]
        lane = lax.broadcasted_iota(jnp.int32, (tm, LANES), 1)
        low = lane < 64
        zero = jnp.zeros((tm, LANES), F32)
        pieces = []

        dgq = jnp.zeros((1, LANES), F32)
        for s in range(4):
            _, xh, r = f["qa"][s]
            d = dqa_ref[:, LANES * s:LANES * (s + 1)] * QA_SCALE
            dx, dg = _col_bwd(d, xh, r, gs64, 64.0, gq, ca, sa, 16, sel16)
            pieces.append(dx)
            dgq = dgq + dg
        dgk = jnp.zeros((1, LANES), F32)
        for s in range(2):
            _, xh, r = f["ka"][s]
            d = dka_ref[:, LANES * s:LANES * (s + 1)]
            d = d + pltpu.roll(d, 64, 1)
            dx, dg = _col_bwd(d, xh, r, _row_sum, 128.0, gk, ca, sa, 16, sel16)
            pieces.append(jnp.where(low, dx, zero))
            dgk = dgk + dg
        for s in range(2):
            d = dva_ref[:, LANES * s:LANES * (s + 1)]
            d = d + pltpu.roll(d, 64, 1)
            pieces.append(jnp.where(low, d, zero))
        pieces.append(dga_ref[...])

        dgqb = jnp.zeros((1, LANES), F32)
        dq_cols = []
        for h in range(B_HEADS):
            _, xh, r = f["qb"][h]
            dx, dg = _col_bwd(dqb_ref[:, LANES * h:LANES * (h + 1)] * QB_SCALE, xh, r, _row_sum, float(B_QK_DIM),
                              gqb, cb, sb, 8, sel8)
            dq_cols.append(dx)
            dgqb = dgqb + dg
        dqr_b = jnp.concatenate(dq_cols, axis=1).astype(BF16)
        dwuq_ref[...] += _dot_tn(f["cqb"], dqr_b)
        dcq_raw, dgcq = _rms_bwd(_dot_nt(dqr_b, wuq), f["cqh"], f["rcq"], gcq)
        pieces.append(dcq_raw)

        dgkb = jnp.zeros((1, LANES), F32)
        dk_cols = []
        dkr = zero
        for h in range(B_HEADS):
            _, xh, r = f["kb"][h]
            dx, dg = _col_bwd(dkb_ref[:, LANES * h:LANES * (h + 1)], xh, r, _row_sum, float(B_QK_DIM),
                              gkb, cb, sb, 8, sel8)
            dk_cols.append(dx)
            dkr = dkr + dx
            dgkb = dgkb + dg
        dkr_b = jnp.concatenate(dk_cols, axis=1).astype(BF16)
        dvb_b = dvb_ref[...].astype(BF16)
        dwuk_ref[...] += _dot_tn(f["ckvb"], dkr_b)
        dwuv_ref[...] += _dot_tn(f["ckvb"], dvb_b)
        dckv = _dot_nt(dkr_b, wuk) + _dot_nt(dvb_b, wuv)
        dckv_raw, dgckv = _rms_bwd(dckv, f["ckvh"], f["rckv"], gckv)
        pieces.append(dckv_raw)
        pieces.append(jnp.where((lane >= B_NOPE_DIM) & (lane < B_QK_DIM), dkr, zero))
        pieces += [zero, zero, zero]
        pieces.append(dgb_ref[...])

        dproj_b = jnp.concatenate(pieces, axis=1).astype(BF16)
        dproj_ref[...] = dproj_b
        xnb_ref[...] = f["xnb"]
        dxn = _dot_nt(dproj_b, w)
        dx, dgin = _rms_bwd(dxn, f["xh0"], f["r0"], gin)
        dx_ref[...] = dx + dh_ref[...]

        for c in range(D_MODEL // LANES):
            small_ref[c:c + 1, :] += dgin[:, LANES * c:LANES * (c + 1)]
        small_ref[8:9, :] += dgq
        small_ref[9:10, :] += dgk
        for c in range(3):
            small_ref[10 + c:11 + c, :] += dgcq[:, LANES * c:LANES * (c + 1)]
        for c in range(2):
            small_ref[13 + c:14 + c, :] += dgckv[:, LANES * c:LANES * (c + 1)]
        small_ref[15:16, :] += dgqb
        small_ref[16:17, :] += dgkb

    def rows(width):
        return pl.BlockSpec((tm, width), lambda i: (i, 0))

    return pl.pallas_call(
        body, name="pre_bwd", grid=(s_len // tm,),
        in_specs=[rows(D_MODEL), rows(D_MODEL), rows(512), rows(256), rows(256), rows(512), rows(512), rows(512),
                  rows(512), rows(512)] + [rows(LANES)] * 4 + _pre_const_specs(consts),
        out_specs=[rows(D_MODEL), rows(N_WIDE), rows(D_MODEL), _full((B_Q_RANK, 512)), _full((B_KV_RANK, 512)),
                   _full((B_KV_RANK, 512)), _full((R_SMALL, LANES))],
        out_shape=[jax.ShapeDtypeStruct((s_len, D_MODEL), F32), jax.ShapeDtypeStruct((s_len, N_WIDE), BF16),
                   jax.ShapeDtypeStruct((s_len, D_MODEL), BF16), jax.ShapeDtypeStruct((B_Q_RANK, 512), F32),
                   jax.ShapeDtypeStruct((B_KV_RANK, 512), F32), jax.ShapeDtypeStruct((B_KV_RANK, 512), F32),
                   jax.ShapeDtypeStruct((R_SMALL, LANES), F32)],
        compiler_params=_params(("arbitrary",)),
    )(x, dh, dqa, dka, dva, dga, dqb, dkb, dvb, dgb, *tabs, *[consts[n] for n in _PRE_IN_NAMES])


def _dw_in_call(xnb, dproj_b, tt, tn):
    s_len = xnb.shape[0]

    def body(a_ref, b_ref, o_ref):
        @pl.when(pl.program_id(1) == 0)
        def _():
            o_ref[...] = jnp.zeros(o_ref.shape, F32)

        o_ref[...] += _dot_tn(a_ref[...], b_ref[...])

    return pl.pallas_call(
        body, name="dw_in", grid=(N_WIDE // tn, s_len // tt),
        in_specs=[pl.BlockSpec((tt, D_MODEL), lambda n, t: (t, 0)), pl.BlockSpec((tt, tn), lambda n, t: (t, n))],
        out_specs=pl.BlockSpec((D_MODEL, tn), lambda n, t: (0, n)),
        out_shape=jax.ShapeDtypeStruct((D_MODEL, N_WIDE), F32),
        compiler_params=_params(("arbitrary", "arbitrary")),
    )(xnb, dproj_b)


def _mesh_pos():
    return lax.axis_index("x"), lax.axis_index("y"), lax.axis_index("c")


def _flip(v, bit):
    return 1 - v if bit else v


def _peer(pos, k):
    x, y, c = pos
    return _flip(x, (k >> 2) & 1), _flip(y, (k >> 1) & 1), _flip(c, k & 1)


def _logical(p):
    return 4 * p[0] + 2 * p[1] + p[2]


def _gather_weights_call(shard):
    m_per = shard.shape[0]

    def body(x_ref, out_ref, xb_ref, send_sems, recv_sems, local_sem):
        x, y, c = _mesh_pos()
        me, sibling = (x, y, c), (x, y, 1 - c)
        chips = [(1 - x, y), (x, 1 - y), (1 - x, 1 - y)]
        xb_ref[...] = x_ref[...].astype(BF16)

        def rows(p):
            return out_ref.at[pl.ds(pl.multiple_of(_logical(p) * m_per, 16), m_per), :]

        def copy(k, block, to, src=None):
            return pltpu.make_async_remote_copy(
                src_ref=rows(block) if src is None else src, dst_ref=rows(block),
                send_sem=send_sems.at[k], recv_sem=recv_sems.at[k],
                device_id=to, device_id_type=pl.DeviceIdType.MESH)

        mine = pltpu.make_async_copy(xb_ref, rows(me), local_sem)
        mine.start()
        first = [copy(0, me, sibling, src=xb_ref)]
        first += [copy(1 + j, me, (*chip, c), src=xb_ref) for j, chip in enumerate(chips)]
        for cp in first:
            cp.start()
        passed = [copy(4 + j, (*chip, c), sibling) for j, chip in enumerate(chips)]
        for j, chip in enumerate(chips):
            copy(1 + j, (*chip, c), me).wait_recv()
            passed[j].start()
        copy(0, sibling, me).wait_recv()
        for j, chip in enumerate(chips):
            copy(4 + j, (*chip, 1 - c), me).wait_recv()
        for cp in first + passed:
            cp.wait_send()
        mine.wait()

    return pl.pallas_call(
        body, name="gather_weights",
        out_shape=jax.ShapeDtypeStruct((N_DEV * m_per, LANES), BF16),
        in_specs=[pl.BlockSpec(memory_space=pltpu.VMEM)],
        out_specs=pl.BlockSpec(memory_space=pltpu.VMEM),
        scratch_shapes=[pltpu.VMEM((m_per, LANES), BF16), pltpu.SemaphoreType.DMA((7,)),
                        pltpu.SemaphoreType.DMA((7,)), pltpu.SemaphoreType.DMA],
        compiler_params=pltpu.CompilerParams(vmem_limit_bytes=VMEM_LIMIT),
    )(shard)


def _adamw(w, g, m, v):
    m = ADAM_B1 * m + (1.0 - ADAM_B1) * g
    v = ADAM_B2 * v + (1.0 - ADAM_B2) * (g * g)
    m_hat = m / (1.0 - ADAM_B1 ** ADAM_STEP)
    v_hat = v / (1.0 - ADAM_B2 ** ADAM_STEP)
    delta = -ADAM_LR * (m_hat / (jnp.sqrt(v_hat) + ADAM_EPS) + ADAM_WD * w)
    return delta, m, v


def _reduce_adamw_call(parts, small, w_pk, m_pk, v_pk, w_s, m_s, v_s):
    chunk = 16
    n_chunks = R_PACK // chunk

    def body(parts_ref, small_ref, w_ref, m_ref, v_ref, ws_ref, ms_ref, vs_ref,
             g_ref, d_ref, nm_ref, nv_ref, gs_ref, ds_ref, nms_ref, nvs_ref,
             recv_ref, recv_s_ref, send_sems, recv_sems, send_s_sems, recv_s_sems, local_sem):
        pos = _mesh_pos()
        me = _logical(pos)

        def big(k):
            peer = _peer(pos, k)
            return pltpu.make_async_remote_copy(
                src_ref=parts_ref.at[_logical(peer)], dst_ref=recv_ref.at[k],
                send_sem=send_sems.at[k], recv_sem=recv_sems.at[k],
                device_id=peer, device_id_type=pl.DeviceIdType.MESH)

        def tiny(k):
            return pltpu.make_async_remote_copy(
                src_ref=small_ref, dst_ref=recv_s_ref.at[k],
                send_sem=send_s_sems.at[k], recv_sem=recv_s_sems.at[k],
                device_id=_peer(pos, k), device_id_type=pl.DeviceIdType.MESH)

        own = pltpu.make_async_copy(parts_ref.at[me], recv_ref.at[0], local_sem)
        own.start()
        for k in range(1, N_DEV):
            tiny(k).start()
        for k in range(1, N_DEV):
            big(k).start()
        recv_s_ref[0] = small_ref[...]
        for k in range(1, N_DEV):
            tiny(k).wait_recv()
        acc = recv_s_ref[me]
        for a in range(1, N_DEV):
            acc = acc + recv_s_ref[lax.bitwise_xor(me, a)]
        row = lax.broadcasted_iota(jnp.int32, (R_SMALL, LANES), 0)
        gs = jnp.where(row == 8, acc + pltpu.roll(acc, 64, 1), acc)
        gs = jnp.where(row == ROW_LOSS, jnp.sum(acc, axis=1, keepdims=True) * (0.5 / D_MODEL), gs)
        gs_ref[...] = gs
        ds, nms, nvs = _adamw(ws_ref[...], gs, ms_ref[...], vs_ref[...])
        ds_ref[...] = ds
        nms_ref[...] = nms
        nvs_ref[...] = nvs

        own.wait()
        for k in range(1, N_DEV):
            big(k).wait_recv()

        def step(t, carry):
            rows = pl.ds(pl.multiple_of(t * chunk, chunk), chunk)
            g = recv_ref[0, rows, :]
            for k in range(1, N_DEV):
                g = g + recv_ref[k, rows, :]
            d, nm, nv = _adamw(w_ref[rows, :], g, m_ref[rows, :], v_ref[rows, :])
            g_ref[rows, :] = g
            d_ref[rows, :] = d
            nm_ref[rows, :] = nm
            nv_ref[rows, :] = nv
            return carry

        lax.fori_loop(0, n_chunks, step, 0)
        for k in range(1, N_DEV):
            tiny(k).wait_send()
            big(k).wait_send()

    vm = pl.BlockSpec(memory_space=pltpu.VMEM)
    big_shape = jax.ShapeDtypeStruct((R_PACK, LANES), F32)
    small_shape = jax.ShapeDtypeStruct((R_SMALL, LANES), F32)
    return pl.pallas_call(
        body, name="reduce_adamw",
        in_specs=[pl.BlockSpec(memory_space=pl.ANY)] + [vm] * 7,
        out_specs=[vm] * 8,
        out_shape=[big_shape] * 4 + [small_shape] * 4,
        scratch_shapes=[pltpu.VMEM((N_DEV, R_PACK, LANES), F32), pltpu.VMEM((N_DEV, R_SMALL, LANES), F32),
                        pltpu.SemaphoreType.DMA((N_DEV,)), pltpu.SemaphoreType.DMA((N_DEV,)),
                        pltpu.SemaphoreType.DMA((N_DEV,)), pltpu.SemaphoreType.DMA((N_DEV,)),
                        pltpu.SemaphoreType.DMA],
        compiler_params=pltpu.CompilerParams(vmem_limit_bytes=VMEM_LIMIT),
    )(parts, small, w_pk, m_pk, v_pk, w_s, m_s, v_s)


def _pack_shard(w_in, w_uq, w_ukv, w_out):
    return jnp.concatenate([a.reshape(-1, LANES) for a in (w_in, w_uq, w_ukv, w_out)], axis=0)


def _unpack_shard(p):
    w_in = p[0:R_WIN].reshape(1, D_MODEL, N_IN // N_DEV)
    w_uq = p[R_WIN:R_WIN + R_WUQ].reshape(1, B_Q_RANK // N_DEV, 384)
    w_ukv = p[R_WIN + R_WUQ:R_WIN + R_WUQ + R_WUKV].reshape(1, B_KV_RANK, 768 // N_DEV)
    w_out = p[R_WIN + R_WUQ + R_WUKV:].reshape(1, D_MODEL // N_DEV, D_MODEL)
    return w_in, w_uq, w_ukv, w_out


def _pack_small(norm_in, a_q, a_k, b_cq, b_ckv, b_q, b_k):
    def row(v):
        return jnp.pad(v.reshape(1, -1), ((0, 0), (0, LANES - v.size)))
    rows = [norm_in.reshape(8, LANES), row(a_q), row(a_k), b_cq.reshape(3, LANES), b_ckv.reshape(2, LANES),
            row(b_q), row(b_k), jnp.zeros((R_SMALL - 17, LANES), F32)]
    return jnp.concatenate(rows, axis=0)


def _unpack_small(s):
    return (s[0:8].reshape(1, D_MODEL), s[8:9, :64], s[9:10, :64], s[10:13].reshape(1, B_Q_RANK),
            s[13:15].reshape(1, B_KV_RANK), s[15:16, :B_QK_DIM], s[16:17, :B_QK_DIM])


def _full_weights(gathered):
    g = gathered
    w_in = g[:, 0:R_WIN].reshape(N_DEV, D_MODEL, N_IN // N_DEV).transpose(1, 0, 2).reshape(D_MODEL, N_IN)
    w_uq = g[:, R_WIN:R_WIN + R_WUQ].reshape(B_Q_RANK, 384)
    w_ukv = g[:, R_WIN + R_WUQ:R_WIN + R_WUQ + R_WUKV].reshape(N_DEV, B_KV_RANK, 768 // N_DEV)
    w_ukv = w_ukv.transpose(1, 0, 2).reshape(B_KV_RANK, 768)
    w_out = g[:, R_WIN + R_WUQ + R_WUKV:].reshape(D_MODEL, D_MODEL)
    k0, k1 = w_in[:, 512:576], w_in[:, 576:640]
    v0, v1 = w_in[:, 640:704], w_in[:, 704:768]
    kr = w_in[:, 1920:1952]
    z64 = jnp.zeros((D_MODEL, 64), BF16)
    z32 = jnp.zeros((D_MODEL, 32), BF16)
    kr_blk = jnp.concatenate([z64, kr, z32], axis=1)
    w_wide = jnp.concatenate([w_in[:, 0:512], k0, k0, k1, k1, v0, v0, v1, v1, w_in[:, 768:1280], w_in[:, 1280:1664],
                              w_in[:, 1664:1920], kr_blk, kr_blk, kr_blk, kr_blk, w_in[:, 1952:2464]], axis=1)
    wuq = jnp.pad(w_uq.reshape(B_Q_RANK, B_HEADS, B_QK_DIM), ((0, 0), (0, 0), (0, LANES - B_QK_DIM)))
    wuq = wuq.reshape(B_Q_RANK, 512)
    ukv = w_ukv.reshape(B_KV_RANK, B_HEADS, B_NOPE_DIM + B_V_DIM)
    wuk = jnp.pad(ukv[:, :, :B_NOPE_DIM], ((0, 0), (0, 0), (0, LANES - B_NOPE_DIM))).reshape(B_KV_RANK, 512)
    wuv = ukv[:, :, B_NOPE_DIM:].reshape(B_KV_RANK, 512)
    return w_wide, wuq, wuk, wuv, w_out


def _narrow_grads(dw_wide, dwuq, dwuk, dwuv):
    dw_in = jnp.concatenate([
        dw_wide[:, 0:512], dw_wide[:, 512:576], dw_wide[:, 640:704], dw_wide[:, 768:832], dw_wide[:, 896:960],
        dw_wide[:, 1024:1536], dw_wide[:, 1536:1920], dw_wide[:, 1920:2176],
        dw_wide[:, O_KR + 64:O_KR + 96], dw_wide[:, 2688:3200]], axis=1)
    dw_uq = dwuq.reshape(B_Q_RANK, B_HEADS, LANES)[:, :, :B_QK_DIM].reshape(B_Q_RANK, 384)
    dk = dwuk.reshape(B_KV_RANK, B_HEADS, LANES)[:, :, :B_NOPE_DIM]
    dv = dwuv.reshape(B_KV_RANK, B_HEADS, B_V_DIM)
    dw_ukv = jnp.concatenate([dk, dv], axis=2).reshape(B_KV_RANK, 768)
    return dw_in, dw_uq, dw_ukv


def _rope_tables(s_len):
    rows = s_len // GRID_W
    row = jnp.repeat(jnp.arange(rows, dtype=F32), GRID_W)
    col = jnp.tile(jnp.arange(GRID_W, dtype=F32), rows)

    def tables(dim):
        half = dim // 2
        inv = 1.0 / (ROPE_THETA ** (jnp.arange(0, half, 2, dtype=F32) / half))
        ar, ac = row[:, None] * inv[None, :], col[:, None] * inv[None, :]
        cos = jnp.concatenate([jnp.cos(ar), jnp.cos(ar), jnp.cos(ac), jnp.cos(ac)], axis=1)
        sin = jnp.concatenate([-jnp.sin(ar), jnp.sin(ar), -jnp.sin(ac), jnp.sin(ac)], axis=1)
        return cos, sin

    ca, sa = tables(A_HEAD_DIM)
    cb, sb = tables(B_ROPE_DIM)
    ca, sa = jnp.tile(ca, (1, 2)), jnp.tile(sa, (1, 2))
    cb = jnp.concatenate([jnp.ones((s_len, 64), F32), cb, jnp.ones((s_len, 32), F32)], axis=1)
    sb = jnp.concatenate([jnp.zeros((s_len, 64), F32), sb, jnp.zeros((s_len, 32), F32)], axis=1)
    return ca, sa, cb, sb


def _lane_major(a, step):
    return a[:, ::step].T[:, None, :]


def kernel(x, norm_in, w_in, a_q_norm, a_k_norm, b_cq_norm, b_ckv_norm, w_uq, w_ukv, b_q_norm, b_k_norm, w_out, loss_target, m_norm_in, m_w_in, m_a_q_norm, m_a_k_norm, m_b_cq_norm, m_b_ckv_norm, m_w_uq, m_w_ukv, m_b_q_norm, m_b_k_norm, m_w_out, v_norm_in, v_w_in, v_a_q_norm, v_a_k_norm, v_b_cq_norm, v_b_ckv_norm, v_w_uq, v_w_ukv, v_b_q_norm, v_b_k_norm, v_w_out):
    s_len = x.shape[1]
    tm = min(256, s_len)
    tq, tk = min(512, s_len), min(2048, s_len)
    ftq, ftk = min(256, s_len), min(1024, s_len)
    x2 = x.reshape(s_len, D_MODEL)
    t2 = loss_target.reshape(s_len, D_MODEL)

    w_pk = _pack_shard(w_in[0], w_uq[0], w_ukv[0], w_out[0])
    gathered = _gather_weights_call(w_pk).reshape(N_DEV, R_PACK, LANES)
    w_wide, wuq, wuk, wuv, wout = _full_weights(gathered)

    def dup(v, pad_to=None):
        v = v.reshape(1, -1)
        if pad_to is None:
            return jnp.concatenate([v, v], axis=1)
        return jnp.pad(v, ((0, 0), (0, pad_to - v.shape[1])))

    g64 = jnp.asarray(np.kron(np.eye(2), np.ones((64, 64))), dtype=BF16)
    consts = dict(gin=norm_in, w=w_wide, wuq=wuq, wuk=wuk, wuv=wuv, gq=dup(a_q_norm), gk=dup(a_k_norm),
                  gcq=b_cq_norm, gckv=b_ckv_norm, gqb=dup(b_q_norm, LANES), gkb=dup(b_k_norm, LANES), g64=g64)
    tabs = _rope_tables(s_len)

    qa, ka, va, ga, qb, kb, vb, gb = _pre_fwd_call(x2, consts, tabs, tm)
    oa, lse_a_t = _attn_fwd_t_call(qa, ka, va.T, groups=A_KV_HEADS, sub=4, masked=True, scale=None, tq=ftq, tk=ftk,
                                   name="attn_fwd_a")
    ob, lse_b_t = _attn_fwd_t_call(qb, kb, vb.T, groups=1, sub=B_HEADS, masked=False, scale=None, tq=ftq,
                                   tk=ftk, name="attn_fwd_b")
    dh, doa, dob, dga, dgb, dl_a, dl_b, dw_out, loss_row = _out_call(x2, t2, oa, ob, ga, gb, wout, g64, tm)

    ck = min(512, s_len)
    dqa, dka_t, dva_t = _attn_bwd_q_call(qa, qa.T, ka, va, doa, doa.T, oa, lse_a_t.reshape(A_HEADS, s_len, 1),
                                         groups=A_KV_HEADS, sub=4, masked=True, tq=tq, tk=tk, ck=ck,
                                         name="attn_bwd_a")
    dqb, dkb_t, dvb_t = _attn_bwd_q_call(qb, qb.T, kb, vb, dob, dob.T, ob, lse_b_t.reshape(B_HEADS, s_len, 1),
                                         groups=2, sub=2, masked=False, tq=tq, tk=tk, ck=ck, name="attn_bwd_b")
    dka, dva, dkb, dvb = dka_t.T, dva_t.T, dkb_t.T, dvb_t.T
    dx, dproj_b, xnb, dwuq, dwuk, dwuv, small = _pre_bwd_call(
        x2, dh, dqa, dka, dva, dga, dqb, dkb, dvb, dgb, consts, tabs, tm)
    dw_wide = _dw_in_call(xnb, dproj_b, min(512, s_len), 640)

    dw_in, dw_uq, dw_ukv = _narrow_grads(dw_wide, dwuq, dwuk, dwuv)
    ci, cq, ckv, co = N_IN // N_DEV, B_Q_RANK // N_DEV, 768 // N_DEV, D_MODEL // N_DEV
    parts = jnp.stack([
        _pack_shard(dw_in[:, ci * d:ci * (d + 1)], dw_uq[cq * d:cq * (d + 1)], dw_ukv[:, ckv * d:ckv * (d + 1)],
                    dw_out[co * d:co * (d + 1)]) for d in range(N_DEV)])
    small = jnp.concatenate([small[:ROW_LOSS], loss_row, small[ROW_LOSS + 1:]], axis=0)

    m_pk = _pack_shard(m_w_in[0], m_w_uq[0], m_w_ukv[0], m_w_out[0])
    v_pk = _pack_shard(v_w_in[0], v_w_uq[0], v_w_ukv[0], v_w_out[0])
    w_s = _pack_small(norm_in, a_q_norm, a_k_norm, b_cq_norm, b_ckv_norm, b_q_norm, b_k_norm)
    m_s = _pack_small(m_norm_in, m_a_q_norm, m_a_k_norm, m_b_cq_norm, m_b_ckv_norm, m_b_q_norm, m_b_k_norm)
    v_s = _pack_small(v_norm_in, v_a_q_norm, v_a_k_norm, v_b_cq_norm, v_b_ckv_norm, v_b_q_norm, v_b_k_norm)
    g_pk, d_pk, nm_pk, nv_pk, g_s, d_s, nm_s, nv_s = _reduce_adamw_call(parts, small, w_pk, m_pk, v_pk, w_s, m_s, v_s)

    def leaves(pk, sm):
        wi, uq, ukv, wo = _unpack_shard(pk)
        n_in, aq, ak, bcq, bckv, bq, bk = _unpack_small(sm)
        return [n_in, wi, aq, ak, bcq, bckv, uq, ukv, bq, bk, wo]

    loss = g_s[ROW_LOSS, 0]
    grad_x = dx.reshape(1, s_len, D_MODEL)
    return (loss, grad_x, *leaves(g_pk, g_s), *leaves(d_pk, d_s), *leaves(nm_pk, nm_s), *leaves(nv_pk, nv_s))
```

```python
import functools

import numpy as np
import jax
import jax.numpy as jnp
from jax import lax
from jax.experimental import pallas as pl
from jax.experimental.pallas import tpu as pltpu

F32 = jnp.float32
BF16 = jnp.bfloat16

D_MODEL = 1024
GRID_W = 64
ROPE_THETA = 10000.0
EPS = 1e-6
A_HEAD_DIM = 64
A_HEADS = 8
A_KV_HEADS = 2
B_HEADS = 4
B_NOPE_DIM = 64
B_ROPE_DIM = 32
B_QK_DIM = 96
B_V_DIM = 128
B_Q_RANK = 384
B_KV_RANK = 256
N_IN = 2464
N_DEV = 8

ADAM_LR = 0.001
ADAM_B1 = 0.9
ADAM_B2 = 0.999
ADAM_EPS = 1e-08
ADAM_WD = 0.01
ADAM_STEP = 10

QA_SCALE = 0.125
QB_SCALE = 1.0 / float(np.sqrt(B_QK_DIM))

LANES = 128
O_QA, O_KA, O_VA, O_GA, O_CQ, O_CKV, O_KR, O_GB, N_WIDE = 0, 512, 768, 1024, 1536, 1920, 2176, 2688, 3200

R_WIN = D_MODEL * (N_IN // N_DEV) // LANES
R_WUQ = (B_Q_RANK // N_DEV) * 384 // LANES
R_WUKV = B_KV_RANK * (768 // N_DEV) // LANES
R_WOUT = (D_MODEL // N_DEV) * D_MODEL // LANES
R_PACK = R_WIN + R_WUQ + R_WUKV + R_WOUT
R_SMALL = 24
ROW_LOSS = 17

VMEM_LIMIT = 56 * 1024 * 1024

NT = (((1,), (1,)), ((), ()))
TN = (((0,), (0,)), ((), ()))


def _dot(a, b):
    return jnp.dot(a, b, preferred_element_type=F32)


def _dot_nt(a, b):
    return lax.dot_general(a, b, NT, preferred_element_type=F32)


def _dot_tn(a, b):
    return lax.dot_general(a, b, TN, preferred_element_type=F32)


def _params(sem=None):
    return pltpu.CompilerParams(dimension_semantics=sem, vmem_limit_bytes=VMEM_LIMIT)


def _full(shape):
    nd = len(shape)
    return pl.BlockSpec(shape, lambda *_: (0,) * nd)


def _swap_sel(rows, shift):
    lane = lax.broadcasted_iota(jnp.int32, (rows, LANES), 1)
    return pltpu.roll(lane, shift, 1) == (lane ^ shift)


def _swap(x, shift, sel):
    return jnp.where(sel, pltpu.roll(x, shift, 1), pltpu.roll(x, LANES - shift, 1))


def _group_sum64(x, g64):
    hi = x.astype(BF16)
    lo = (x - hi.astype(F32)).astype(BF16)
    return _dot(hi, g64) + _dot(lo, g64)


def _row_sum(x):
    return jnp.sum(x, axis=-1, keepdims=True)


def _col_fwd(xs, msum, denom, gain, cos, sin, shift, sel):
    r = lax.rsqrt(msum(xs * xs) * (1.0 / denom) + EPS)
    xh = xs * r
    n = xh * gain
    return n * cos + _swap(n, shift, sel) * sin, xh, r


def _col_bwd(d_out, xh, r, msum, denom, gain, cos, sin, shift, sel):
    dn = d_out * cos + _swap(d_out * sin, shift, sel)
    dgain = jnp.sum(dn * xh, axis=0, keepdims=True)
    dxh = dn * gain
    dx = r * (dxh - xh * (msum(dxh * xh) * (1.0 / denom)))
    return dx, dgain


def _rms_fwd(x, gain):
    r = lax.rsqrt(jnp.mean(x * x, axis=-1, keepdims=True) + EPS)
    xh = x * r
    return xh * gain, xh, r


def _rms_bwd(dy, xh, r, gain):
    dgain = jnp.sum(dy * xh, axis=0, keepdims=True)
    dxh = dy * gain
    dx = r * (dxh - xh * jnp.mean(dxh * xh, axis=-1, keepdims=True))
    return dx, dgain


def _pre_forward(x, gin, w, wuq, wuk, wuv, gq, gk, gcq, gckv, gqb, gkb, ca, sa, cb, sb, g64, tm):
    sel16 = _swap_sel(tm, 16)
    sel8 = _swap_sel(tm, 8)
    xn, xh0, r0 = _rms_fwd(x, gin)
    xnb = xn.astype(BF16)
    proj = _dot(xnb, w)
    gs64 = functools.partial(_group_sum64, g64=g64)
    qa = [_col_fwd(proj[:, O_QA + LANES * s:O_QA + LANES * (s + 1)], gs64, 64.0, gq, ca, sa, 16, sel16)
          for s in range(4)]
    ka = [_col_fwd(proj[:, O_KA + LANES * s:O_KA + LANES * (s + 1)], _row_sum, 128.0, gk, ca, sa, 16, sel16)
          for s in range(2)]
    cq, cqh, rcq = _rms_fwd(proj[:, O_CQ:O_CQ + B_Q_RANK], gcq)
    cqb = cq.astype(BF16)
    qb_raw = _dot(cqb, wuq)
    qb = [_col_fwd(qb_raw[:, LANES * h:LANES * (h + 1)], _row_sum, float(B_QK_DIM), gqb, cb, sb, 8, sel8)
          for h in range(B_HEADS)]
    ckv, ckvh, rckv = _rms_fwd(proj[:, O_CKV:O_CKV + B_KV_RANK], gckv)
    ckvb = ckv.astype(BF16)
    kb_raw = _dot(ckvb, wuk) + proj[:, O_KR:O_KR + 512]
    vb = _dot(ckvb, wuv)
    kb = [_col_fwd(kb_raw[:, LANES * h:LANES * (h + 1)], _row_sum, float(B_QK_DIM), gkb, cb, sb, 8, sel8)
          for h in range(B_HEADS)]
    return dict(xh0=xh0, r0=r0, xnb=xnb, proj=proj, qa=qa, ka=ka, cqh=cqh, rcq=rcq, cqb=cqb, qb=qb,
                ckvh=ckvh, rckv=rckv, ckvb=ckvb, kb=kb, vb=vb, sel16=sel16, sel8=sel8, gs64=gs64)


def _rope_tiles(tab_refs, i, tm):
    per_tile = tm // GRID_W
    out = []
    for t in range(4):
        col_ref, row_ref = tab_refs[2 * t], tab_refs[2 * t + 1]
        col = col_ref[...]
        out.append(jnp.concatenate([col + row_ref[pl.ds(i * per_tile + b, 1), :] for b in range(per_tile)], axis=0))
    return out


_PRE_IN_NAMES = ("gin", "w", "wuq", "wuk", "wuv", "gq", "gk", "gcq", "gckv", "gqb", "gkb", "g64")


def _pre_const_specs(consts):
    return [_full(consts[n].shape) for n in _PRE_IN_NAMES]


def _pre_fwd_call(x, consts, tabs, tm):
    s_len = x.shape[0]

    def body(x_ref, *refs):
        tab_refs, refs = refs[:8], refs[8:]
        (gin_ref, w_ref, wuq_ref, wuk_ref, wuv_ref, gq_ref, gk_ref, gcq_ref, gckv_ref, gqb_ref, gkb_ref, g64_ref,
         qa_ref, ka_ref, va_ref, vat_ref, ga_ref, qb_ref, kb_ref, vb_ref, vbt_ref, gb_ref) = refs
        ca, sa, cb, sb = _rope_tiles(tab_refs, pl.program_id(0), tm)
        f = _pre_forward(x_ref[...], gin_ref[...], w_ref[...], wuq_ref[...], wuk_ref[...], wuv_ref[...],
                         gq_ref[...], gk_ref[...], gcq_ref[...], gckv_ref[...], gqb_ref[...], gkb_ref[...],
                         ca, sa, cb, sb, g64_ref[...], tm)
        proj = f["proj"]
        for s in range(4):
            qa_ref[:, LANES * s:LANES * (s + 1)] = (f["qa"][s][0] * QA_SCALE).astype(BF16)
        for s in range(2):
            ka_ref[:, LANES * s:LANES * (s + 1)] = f["ka"][s][0].astype(BF16)
        va = proj[:, O_VA:O_VA + 256]
        va_ref[...] = va.astype(BF16)
        vat_ref[...] = va.T.astype(BF16)
        ga_ref[...] = proj[:, O_GA:O_GA + 512]
        for h in range(B_HEADS):
            qb_ref[:, LANES * h:LANES * (h + 1)] = (f["qb"][h][0] * QB_SCALE).astype(BF16)
            kb_ref[:, LANES * h:LANES * (h + 1)] = f["kb"][h][0].astype(BF16)
        vb_ref[...] = f["vb"].astype(BF16)
        vbt_ref[...] = f["vb"].T.astype(BF16)
        gb_ref[...] = proj[:, O_GB:O_GB + 512]

    def rows(width):
        return pl.BlockSpec((tm, width), lambda i: (i, 0))

    def cols(height):
        return pl.BlockSpec((height, tm), lambda i: (0, i))

    outs = [((s_len, 512), BF16, rows(512)), ((s_len, 256), BF16, rows(256)), ((s_len, 256), BF16, rows(256)),
            ((256, s_len), BF16, cols(256)), ((s_len, 512), F32, rows(512)), ((s_len, 512), BF16, rows(512)),
            ((s_len, 512), BF16, rows(512)), ((s_len, 512), BF16, rows(512)), ((512, s_len), BF16, cols(512)),
            ((s_len, 512), F32, rows(512))]
    return pl.pallas_call(
        body, name="pre_fwd", grid=(s_len // tm,),
        in_specs=[rows(D_MODEL)] + [_full(t.shape) for t in tabs] + _pre_const_specs(consts),
        out_specs=[sp for _, _, sp in outs],
        out_shape=[jax.ShapeDtypeStruct(sh, dt) for sh, dt, _ in outs],
        compiler_params=_params(("arbitrary",)),
    )(x, *tabs, *[consts[n] for n in _PRE_IN_NAMES])


def _head_masks(rows):
    lane = lax.broadcasted_iota(jnp.int32, (rows, LANES), 1)
    return lane < 64, lane >= 64


def _lane_fold(x, op):
    out = x[:, 0:LANES]
    for c in range(1, x.shape[1] // LANES):
        out = op(out, x[:, LANES * c:LANES * (c + 1)])
    return out


def _row_fold(x, op):
    return op(x.reshape(x.shape[0] // 8, 8, x.shape[1]), axis=0)


def _attn_fwd_t_call(q, k, vt, *, groups, sub, masked, scale, tq, tk, name):
    s_len = q.shape[0]
    qw = LANES * (sub // 2 if masked else sub)
    kvw = LANES if masked else LANES * sub
    n_c = s_len // tk
    kv_mode = pl.Buffered(1) if groups == 1 else None

    def body(q_ref, k_ref, vt_ref, o_ref, lse_ref, s_sc):
        keep = _head_masks(tq) if masked else None

        def kv_of(hh):
            return slice(0, LANES) if masked else slice(LANES * hh, LANES * (hh + 1))

        def q_of(hh):
            if not masked:
                return q_ref[:, LANES * hh:LANES * (hh + 1)]
            qp = q_ref[:, LANES * (hh // 2):LANES * (hh // 2 + 1)]
            return jnp.where(keep[hh % 2], qp, jnp.zeros_like(qp))

        def scores(hh, qm, c, mx):
            s_t = _dot_nt(k_ref[tk * c:tk * (c + 1), kv_of(hh)], qm)
            if scale is not None:
                s_t = s_t * scale
            s_sc[hh % 2, c] = s_t
            return jnp.maximum(mx, _row_fold(s_t, jnp.max))

        neg = jnp.full((8, tq), -jnp.inf, F32)
        qm_next = q_of(0)
        mx_next = neg
        for c in range(n_c):
            mx_next = scores(0, qm_next, c, mx_next)
        outs = []
        for hh in range(sub):
            m = jnp.max(mx_next, axis=0, keepdims=True)
            if hh + 1 < sub:
                qm_next = q_of(hh + 1)
                mx_next = neg
            lsum = jnp.zeros((8, tq), F32)
            acc = jnp.zeros((LANES, tq), F32)
            for c in range(n_c):
                if hh + 1 < sub:
                    mx_next = scores(hh + 1, qm_next, c, mx_next)
                p_t = jnp.exp(s_sc[hh % 2, c] - m)
                lsum = lsum + _row_fold(p_t, jnp.sum)
                acc = acc + _dot(vt_ref[kv_of(hh), tk * c:tk * (c + 1)], p_t.astype(BF16))
            l = jnp.sum(lsum, axis=0, keepdims=True)
            outs.append((acc / l).T)
            lse_ref[hh] = m + jnp.log(l)
        if masked:
            for pr in range(sub // 2):
                o_ref[:, LANES * pr:LANES * (pr + 1)] = jnp.where(keep[0], outs[2 * pr], outs[2 * pr + 1])
        else:
            for hh in range(sub):
                o_ref[:, LANES * hh:LANES * (hh + 1)] = outs[hh]

    return pl.pallas_call(
        body, name=name, grid=(groups, s_len // tq),
        in_specs=[pl.BlockSpec((tq, qw), lambda g, i: (i, g)),
                  pl.BlockSpec((s_len, kvw), lambda g, i: (0, g), pipeline_mode=kv_mode),
                  pl.BlockSpec((kvw, s_len), lambda g, i: (g, 0), pipeline_mode=kv_mode)],
        out_specs=[pl.BlockSpec((tq, qw), lambda g, i: (i, g)),
                   pl.BlockSpec((sub, 1, tq), lambda g, i: (g, 0, i))],
        out_shape=[jax.ShapeDtypeStruct((s_len, groups * qw), F32),
                   jax.ShapeDtypeStruct((groups * sub, 1, s_len), F32)],
        scratch_shapes=[pltpu.VMEM((min(sub, 2), n_c, tk, tq), F32)],
        compiler_params=_params(("arbitrary", "arbitrary")),
    )(q, k, vt)


def _attn_fwd_call(q, k, v, *, groups, sub, masked, scale, tq, tk, name):
    s_len = q.shape[0]
    qw = LANES * (sub // 2 if masked else sub)
    n_c = s_len // tk
    log2e = float(np.log2(np.e))
    mul = log2e if scale is None else scale * log2e

    def body(q_ref, k_ref, v_ref, o_ref, lse_ref, s_sc):
        keep = _head_masks(tq) if masked else None
        outs = []
        for hh in range(sub):
            if masked:
                qp = q_ref[:, LANES * (hh // 2):LANES * (hh // 2 + 1)]
                qm = jnp.where(keep[hh % 2], qp, jnp.zeros_like(qp))
                kv_cols = slice(0, LANES)
            else:
                qm = q_ref[:, LANES * hh:LANES * (hh + 1)]
                kv_cols = slice(LANES * hh, LANES * (hh + 1))
            buf = hh % 2

            def scores(c, mx):
                rows = pl.ds(pl.multiple_of(c * tk, tk), tk)
                s = _dot_nt(qm, k_ref[rows, kv_cols]) * mul
                s_sc[buf, c] = s
                return jnp.maximum(mx, _lane_fold(s, jnp.maximum))

            mx = lax.fori_loop(0, n_c, scores, jnp.full((tq, LANES), -jnp.inf, F32), unroll=True)
            m = jnp.max(mx, axis=1, keepdims=True)

            def weights(c, carry):
                lsum, acc = carry
                rows = pl.ds(pl.multiple_of(c * tk, tk), tk)
                p = jnp.exp2(s_sc[buf, c] - m)
                return lsum + _lane_fold(p, jnp.add), acc + _dot(p.astype(BF16), v_ref[rows, kv_cols])

            lsum, acc = lax.fori_loop(0, n_c, weights, (jnp.zeros((tq, LANES), F32), jnp.zeros((tq, LANES), F32)),
                                      unroll=True)
            l = jnp.sum(lsum, axis=1, keepdims=True)
            outs.append(acc / l)
            lse_ref[hh] = jnp.broadcast_to((m + jnp.log2(l)) * (1.0 / log2e), (tq, LANES))
        if masked:
            for pr in range(sub // 2):
                o_ref[:, LANES * pr:LANES * (pr + 1)] = jnp.where(keep[0], outs[2 * pr], outs[2 * pr + 1])
        else:
            for hh in range(sub):
                o_ref[:, LANES * hh:LANES * (hh + 1)] = outs[hh]

    kvw = LANES if masked else LANES * sub
    return pl.pallas_call(
        body, name=name, grid=(groups, s_len // tq),
        in_specs=[pl.BlockSpec((tq, qw), lambda g, i: (i, g)),
                  pl.BlockSpec((s_len, kvw), lambda g, i: (0, g)),
                  pl.BlockSpec((s_len, kvw), lambda g, i: (0, g))],
        out_specs=[pl.BlockSpec((tq, qw), lambda g, i: (i, g)),
                   pl.BlockSpec((sub, tq, LANES), lambda g, i: (g, i, 0))],
        out_shape=[jax.ShapeDtypeStruct((s_len, groups * qw), F32),
                   jax.ShapeDtypeStruct((groups * sub, s_len, LANES), F32)],
        scratch_shapes=[pltpu.VMEM((min(sub, 2), n_c, tq, tk), F32)],
        compiler_params=_params(("arbitrary", "arbitrary")),
    )(q, k, v)


def _attn_bwd_q_call(q, k, v, do, o, lse_t, *, groups, sub, masked, tq, tk, ck, name):
    s_len = q.shape[0]
    qw = LANES * (sub // 2 if masked else sub)
    kvw = LANES if masked else LANES * sub
    n_c = tk // ck

    def body(q_ref, k_ref, v_ref, do_ref, o_ref, lse_ref, dq_ref, dkt_ref, dvt_ref):
        j = pl.program_id(1)
        i = pl.program_id(2)

        @pl.when((j == 0) & (i == 0))
        def _():
            dq_ref[...] = jnp.zeros(dq_ref.shape, F32)

        @pl.when(i == 0)
        def _():
            dkt_ref[...] = jnp.zeros(dkt_ref.shape, F32)
            dvt_ref[...] = jnp.zeros(dvt_ref.shape, F32)

        lkeep = _head_masks(tq) if masked else None
        heads = []
        for hh in range(sub):
            if masked:
                cols = slice(LANES * (hh // 2), LANES * (hh // 2 + 1))
                kv = slice(0, LANES)
                qp, dop = q_ref[:, cols], do_ref[:, cols]
                qm = jnp.where(lkeep[hh % 2], qp, jnp.zeros_like(qp))
                dom = jnp.where(lkeep[hh % 2], dop, jnp.zeros_like(dop))
            else:
                cols = kv = slice(LANES * hh, LANES * (hh + 1))
                qm, dom = q_ref[:, cols], do_ref[:, cols]
            delta = jnp.sum(dom.astype(F32) * o_ref[:, cols], axis=1, keepdims=True)
            lse = jnp.broadcast_to(lse_ref[hh], (LANES, tq)).T[:, 0:1]
            heads.append((cols, kv, qm, dom, qm.T, dom.T, delta, lse))

        def products(hh, c):
            _, kv, qm, dom, _, _, _, _ = heads[hh]
            return _dot_nt(qm, k_ref[ck * c:ck * (c + 1), kv]), _dot_nt(dom, v_ref[ck * c:ck * (c + 1), kv])

        items = [(hh, c) for hh in range(sub) for c in range(n_c)]
        dq_acc = [jnp.zeros((tq, LANES), F32) for _ in range(sub)]
        nxt = products(*items[0])
        for n, (hh, c) in enumerate(items):
            s, dp = nxt
            if n + 1 < len(items):
                nxt = products(*items[n + 1])
            _, kv, qm, dom, qmt, domt, delta, lse = heads[hh]
            p = jnp.exp(s - lse)
            ds = p * (dp - delta)
            p_b = p.astype(BF16)
            ds_b = ds.astype(BF16)
            kcols = slice(ck * c, ck * (c + 1))
            dvt_ref[kv, kcols] += _dot(domt, p_b)
            dkt_ref[kv, kcols] += _dot(qmt, ds_b)
            dq_acc[hh] = dq_acc[hh] + _dot(ds_b, k_ref[kcols, kv])
        rows = pl.ds(pl.multiple_of(i * tq, tq), tq)
        if masked:
            for pr in range(sub // 2):
                dq_ref[rows, LANES * pr:LANES * (pr + 1)] += jnp.where(lkeep[0], dq_acc[2 * pr], dq_acc[2 * pr + 1])
        else:
            for hh in range(sub):
                dq_ref[rows, LANES * hh:LANES * (hh + 1)] += dq_acc[hh]

    return pl.pallas_call(
        body, name=name, grid=(groups, s_len // tk, s_len // tq),
        in_specs=[pl.BlockSpec((tq, qw), lambda g, j, i: (i, g)),
                  pl.BlockSpec((tk, kvw), lambda g, j, i: (j, g)),
                  pl.BlockSpec((tk, kvw), lambda g, j, i: (j, g)),
                  pl.BlockSpec((tq, qw), lambda g, j, i: (i, g)),
                  pl.BlockSpec((tq, qw), lambda g, j, i: (i, g)),
                  pl.BlockSpec((sub, 1, tq), lambda g, j, i: (g, 0, i))],
        out_specs=[pl.BlockSpec((s_len, qw), lambda g, j, i: (0, g)),
                   pl.BlockSpec((kvw, tk), lambda g, j, i: (g, j)),
                   pl.BlockSpec((kvw, tk), lambda g, j, i: (g, j))],
        out_shape=[jax.ShapeDtypeStruct((s_len, groups * qw), F32),
                   jax.ShapeDtypeStruct((groups * kvw, s_len), F32),
                   jax.ShapeDtypeStruct((groups * kvw, s_len), F32)],
        compiler_params=_params(("arbitrary", "arbitrary", "arbitrary")),
    )(q, k, v, do, o, lse_t)


def _attn_bwd_call(q, k, v, do, lse_t, delta_t, *, groups, sub, scale, tq, tk, name):
    s_len = q.shape[0]
    masked = sub > 1
    qw = LANES * (sub // 2 if masked else 1)
    n_k = s_len // tk

    def body(q_ref, k_ref, v_ref, do_ref, lse_ref, dl_ref, dq_ref, dk_ref, dv_ref, dq_sc):
        i = pl.program_id(1)
        j = pl.program_id(2)

        @pl.when((i == 0) & (j == 0))
        def _():
            dk_ref[...] = jnp.zeros(dk_ref.shape, F32)
            dv_ref[...] = jnp.zeros(dv_ref.shape, F32)

        @pl.when(j == 0)
        def _():
            dq_sc[...] = jnp.zeros(dq_sc.shape, F32)

        kk = k_ref[...]
        vv = v_ref[...]
        keep = _head_masks(tq) if masked else None
        dk_t = jnp.zeros((tk, LANES), F32)
        dv_t = jnp.zeros((tk, LANES), F32)
        for hh in range(sub):
            if masked:
                cols = slice(LANES * (hh // 2), LANES * (hh // 2 + 1))
                qp = q_ref[:, cols]
                dop = do_ref[:, cols]
                qm = jnp.where(keep[hh % 2], qp, jnp.zeros_like(qp))
                dom = jnp.where(keep[hh % 2], dop, jnp.zeros_like(dop))
            else:
                cols = slice(0, LANES)
                qm = q_ref[...]
                dom = do_ref[...]
            s_t = _dot_nt(kk, qm)
            if scale is not None:
                s_t = s_t * scale
            p_t = jnp.exp(s_t - lse_ref[hh])
            dp_t = _dot_nt(vv, dom)
            ds_t = p_t * (dp_t - dl_ref[hh])
            if scale is not None:
                ds_t = ds_t * scale
            p_b = p_t.astype(BF16)
            ds_b = ds_t.astype(BF16)
            dv_t = dv_t + _dot(p_b, dom)
            dk_t = dk_t + _dot(ds_b, qm)
            dq_h = _dot_tn(ds_b, kk)
            if masked:
                dq_h = jnp.where(keep[hh % 2], dq_h, jnp.zeros_like(dq_h))
            dq_sc[:, cols] += dq_h
        rows = pl.ds(pl.multiple_of(j * tk, tk), tk)
        dk_ref[rows, :] += dk_t
        dv_ref[rows, :] += dv_t

        @pl.when(j == n_k - 1)
        def _():
            dq_ref[...] = dq_sc[...]

    return pl.pallas_call(
        body, name=name, grid=(groups, s_len // tq, n_k),
        in_specs=[pl.BlockSpec((tq, qw), lambda g, i, j: (i, g)),
                  pl.BlockSpec((tk, LANES), lambda g, i, j: (j, g)),
                  pl.BlockSpec((tk, LANES), lambda g, i, j: (j, g)),
                  pl.BlockSpec((tq, qw), lambda g, i, j: (i, g)),
                  pl.BlockSpec((sub, 1, tq), lambda g, i, j: (g, 0, i)),
                  pl.BlockSpec((sub, 1, tq), lambda g, i, j: (g, 0, i))],
        out_specs=[pl.BlockSpec((tq, qw), lambda g, i, j: (i, g)),
                   pl.BlockSpec((s_len, LANES), lambda g, i, j: (0, g)),
                   pl.BlockSpec((s_len, LANES), lambda g, i, j: (0, g))],
        out_shape=[jax.ShapeDtypeStruct((s_len, groups * qw), F32),
                   jax.ShapeDtypeStruct((s_len, groups * LANES), F32),
                   jax.ShapeDtypeStruct((s_len, groups * LANES), F32)],
        scratch_shapes=[pltpu.VMEM((tq, qw), F32)],
        compiler_params=_params(("arbitrary", "arbitrary", "arbitrary")),
    )(q, k, v, do, lse_t, delta_t)


def _silu_parts(g):
    sig = 1.0 / (1.0 + jnp.exp(-g))
    return g * sig, sig * (1.0 + g * (1.0 - sig))


def _out_call(x, target, oa, ob, ga, gb, wout, tm):
    s_len = x.shape[0]
    n_t = s_len // tm

    def body(x_ref, t_ref, oa_ref, ob_ref, ga_ref, gb_ref, w_ref,
             dh_ref, doa_ref, dob_ref, dga_ref, dgb_ref, dw_ref, loss_ref):
        i = pl.program_id(0)

        @pl.when(i == 0)
        def _():
            dw_ref[...] = jnp.zeros(dw_ref.shape, F32)
            loss_ref[...] = jnp.zeros(loss_ref.shape, F32)

        oa_v, ob_v = oa_ref[...], ob_ref[...]
        silu_a, dsilu_a = _silu_parts(ga_ref[...])
        silu_b, dsilu_b = _silu_parts(gb_ref[...])
        ya = (oa_v * silu_a).astype(BF16)
        yb = (ob_v * silu_b).astype(BF16)
        h = x_ref[...] + _dot(ya, w_ref[0:512, :]) + _dot(yb, w_ref[512:1024, :])
        err = h - t_ref[...]
        part = jnp.sum(err * err, axis=0, keepdims=True)
        acc = part[:, 0:LANES]
        for c in range(1, D_MODEL // LANES):
            acc = acc + part[:, LANES * c:LANES * (c + 1)]
        loss_ref[...] += acc
        dh = err * (1.0 / D_MODEL)
        dh_ref[...] = dh
        dhb = dh.astype(BF16)
        dya = _dot_nt(dhb, w_ref[0:512, :])
        dyb = _dot_nt(dhb, w_ref[512:1024, :])
        doa = dya * silu_a
        dob = dyb * silu_b
        doa_ref[...] = doa.astype(BF16)
        dob_ref[...] = dob.astype(BF16)
        dga_ref[...] = dya * oa_v * dsilu_a
        dgb_ref[...] = dyb * ob_v * dsilu_b
        dw_ref[0:512, :] += _dot_tn(ya, dhb)
        dw_ref[512:1024, :] += _dot_tn(yb, dhb)

    def rows(width):
        return pl.BlockSpec((tm, width), lambda i: (i, 0))

    outs = [(D_MODEL, F32), (512, BF16), (512, BF16), (512, F32), (512, F32)]
    return pl.pallas_call(
        body, name="out_fwd", grid=(n_t,),
        in_specs=[rows(D_MODEL), rows(D_MODEL), rows(512), rows(512), rows(512), rows(512),
                  _full((D_MODEL, D_MODEL))],
        out_specs=[rows(wd) for wd, _ in outs] + [_full((D_MODEL, D_MODEL)), _full((1, LANES))],
        out_shape=[jax.ShapeDtypeStruct((s_len, wd), dt) for wd, dt in outs]
        + [jax.ShapeDtypeStruct((D_MODEL, D_MODEL), F32), jax.ShapeDtypeStruct((1, LANES), F32)],
        compiler_params=_params(("arbitrary",)),
    )(x, target, oa, ob, ga, gb, wout)


def _pre_bwd_call(x, dh, dqa, dka, dva, dga, dqb, dkb, dvb, dgb, consts, tabs, tm):
    s_len = x.shape[0]

    def body(x_ref, dh_ref, dqa_ref, dkat_ref, dvat_ref, dga_ref, dqb_ref, dkbt_ref, dvbt_ref, dgb_ref, *refs):
        tab_refs, refs = refs[:8], refs[8:]
        (gin_ref, w_ref, wuq_ref, wuk_ref, wuv_ref, gq_ref, gk_ref, gcq_ref, gckv_ref, gqb_ref, gkb_ref, g64_ref,
         dx_ref, dproj_ref, xnb_ref, dwuq_ref, dwuk_ref, dwuv_ref, small_ref) = refs
        i = pl.program_id(0)
        dka_v, dva_v = dkat_ref[...].T, dvat_ref[...].T
        dkb_v, dvb_v = dkbt_ref[...].T, dvbt_ref[...].T

        @pl.when(i == 0)
        def _():
            dwuq_ref[...] = jnp.zeros(dwuq_ref.shape, F32)
            dwuk_ref[...] = jnp.zeros(dwuk_ref.shape, F32)
            dwuv_ref[...] = jnp.zeros(dwuv_ref.shape, F32)
            small_ref[...] = jnp.zeros(small_ref.shape, F32)

        gin, gq, gk = gin_ref[...], gq_ref[...], gk_ref[...]
        gcq, gckv, gqb, gkb = gcq_ref[...], gckv_ref[...], gqb_ref[...], gkb_ref[...]
        ca, sa, cb, sb = _rope_tiles(tab_refs, i, tm)
        w, wuq, wuk, wuv = w_ref[...], wuq_ref[...], wuk_ref[...], wuv_ref[...]
        f = _pre_forward(x_ref[...], gin, w, wuq, wuk, wuv, gq, gk, gcq, gckv, gqb, gkb,
                         ca, sa, cb, sb, g64_ref[...], tm)
        sel16, sel8, gs64 = f["sel16"], f["sel8"], f["gs64"]
        lane = lax.broadcasted_iota(jnp.int32, (tm, LANES), 1)
        low = lane < 64
        zero = jnp.zeros((tm, LANES), F32)
        pieces = []

        dgq = jnp.zeros((1, LANES), F32)
        for s in range(4):
            _, xh, r = f["qa"][s]
            d = dqa_ref[:, LANES * s:LANES * (s + 1)] * QA_SCALE
            dx, dg = _col_bwd(d, xh, r, gs64, 64.0, gq, ca, sa, 16, sel16)
            pieces.append(dx)
            dgq = dgq + dg
        dgk = jnp.zeros((1, LANES), F32)
        for s in range(2):
            _, xh, r = f["ka"][s]
            d = dka_v[:, LANES * s:LANES * (s + 1)]
            d = d + pltpu.roll(d, 64, 1)
            dx, dg = _col_bwd(d, xh, r, _row_sum, 128.0, gk, ca, sa, 16, sel16)
            pieces.append(jnp.where(low, dx, zero))
            dgk = dgk + dg
        for s in range(2):
            d = dva_v[:, LANES * s:LANES * (s + 1)]
            d = d + pltpu.roll(d, 64, 1)
            pieces.append(jnp.where(low, d, zero))
        pieces.append(dga_ref[...])

        dgqb = jnp.zeros((1, LANES), F32)
        dq_cols = []
        for h in range(B_HEADS):
            _, xh, r = f["qb"][h]
            dx, dg = _col_bwd(dqb_ref[:, LANES * h:LANES * (h + 1)] * QB_SCALE, xh, r, _row_sum, float(B_QK_DIM),
                              gqb, cb, sb, 8, sel8)
            dq_cols.append(dx)
            dgqb = dgqb + dg
        dqr_b = jnp.concatenate(dq_cols, axis=1).astype(BF16)
        dwuq_ref[...] += _dot_tn(f["cqb"], dqr_b)
        dcq_raw, dgcq = _rms_bwd(_dot_nt(dqr_b, wuq), f["cqh"], f["rcq"], gcq)
        pieces.append(dcq_raw)

        dgkb = jnp.zeros((1, LANES), F32)
        dk_cols = []
        dkr = zero
        for h in range(B_HEADS):
            _, xh, r = f["kb"][h]
            dx, dg = _col_bwd(dkb_v[:, LANES * h:LANES * (h + 1)], xh, r, _row_sum, float(B_QK_DIM),
                              gkb, cb, sb, 8, sel8)
            dk_cols.append(dx)
            dkr = dkr + dx
            dgkb = dgkb + dg
        dkr_b = jnp.concatenate(dk_cols, axis=1).astype(BF16)
        dvb_b = dvb_v.astype(BF16)
        dwuk_ref[...] += _dot_tn(f["ckvb"], dkr_b)
        dwuv_ref[...] += _dot_tn(f["ckvb"], dvb_b)
        dckv = _dot_nt(dkr_b, wuk) + _dot_nt(dvb_b, wuv)
        dckv_raw, dgckv = _rms_bwd(dckv, f["ckvh"], f["rckv"], gckv)
        pieces.append(dckv_raw)
        pieces.append(jnp.where((lane >= B_NOPE_DIM) & (lane < B_QK_DIM), dkr, zero))
        pieces += [zero, zero, zero]
        pieces.append(dgb_ref[...])

        dproj_b = jnp.concatenate(pieces, axis=1).astype(BF16)
        dproj_ref[...] = dproj_b
        xnb_ref[...] = f["xnb"]
        dxn = _dot_nt(dproj_b, w)
        dx, dgin = _rms_bwd(dxn, f["xh0"], f["r0"], gin)
        dx_ref[...] = dx + dh_ref[...]

        for c in range(D_MODEL // LANES):
            small_ref[c:c + 1, :] += dgin[:, LANES * c:LANES * (c + 1)]
        small_ref[8:9, :] += dgq
        small_ref[9:10, :] += dgk
        for c in range(3):
            small_ref[10 + c:11 + c, :] += dgcq[:, LANES * c:LANES * (c + 1)]
        for c in range(2):
            small_ref[13 + c:14 + c, :] += dgckv[:, LANES * c:LANES * (c + 1)]
        small_ref[15:16, :] += dgqb
        small_ref[16:17, :] += dgkb

    def rows(width):
        return pl.BlockSpec((tm, width), lambda i: (i, 0))

    def cols(height):
        return pl.BlockSpec((height, tm), lambda i: (0, i))

    return pl.pallas_call(
        body, name="pre_bwd", grid=(s_len // tm,),
        in_specs=[rows(D_MODEL), rows(D_MODEL), rows(512), cols(256), cols(256), rows(512), rows(512), cols(512),
                  cols(512), rows(512)] + [_full(t.shape) for t in tabs] + _pre_const_specs(consts),
        out_specs=[rows(D_MODEL), rows(N_WIDE), rows(D_MODEL), _full((B_Q_RANK, 512)), _full((B_KV_RANK, 512)),
                   _full((B_KV_RANK, 512)), _full((R_SMALL, LANES))],
        out_shape=[jax.ShapeDtypeStruct((s_len, D_MODEL), F32), jax.ShapeDtypeStruct((s_len, N_WIDE), BF16),
                   jax.ShapeDtypeStruct((s_len, D_MODEL), BF16), jax.ShapeDtypeStruct((B_Q_RANK, 512), F32),
                   jax.ShapeDtypeStruct((B_KV_RANK, 512), F32), jax.ShapeDtypeStruct((B_KV_RANK, 512), F32),
                   jax.ShapeDtypeStruct((R_SMALL, LANES), F32)],
        compiler_params=_params(("arbitrary",)),
    )(x, dh, dqa, dka, dva, dga, dqb, dkb, dvb, dgb, *tabs, *[consts[n] for n in _PRE_IN_NAMES])


def _dw_in_call(xnb, dproj_b, tt, tn):
    s_len = xnb.shape[0]

    def body(a_ref, b_ref, o_ref):
        @pl.when(pl.program_id(1) == 0)
        def _():
            o_ref[...] = jnp.zeros(o_ref.shape, F32)

        o_ref[...] += _dot_tn(a_ref[...], b_ref[...])

    return pl.pallas_call(
        body, name="dw_in", grid=(N_WIDE // tn, s_len // tt),
        in_specs=[pl.BlockSpec((tt, D_MODEL), lambda n, t: (t, 0)), pl.BlockSpec((tt, tn), lambda n, t: (t, n))],
        out_specs=pl.BlockSpec((D_MODEL, tn), lambda n, t: (0, n)),
        out_shape=jax.ShapeDtypeStruct((D_MODEL, N_WIDE), F32),
        compiler_params=_params(("arbitrary", "arbitrary")),
    )(xnb, dproj_b)


def _mesh_pos():
    return lax.axis_index("x"), lax.axis_index("y"), lax.axis_index("c")


def _flip(v, bit):
    return 1 - v if bit else v


def _peer(pos, k):
    x, y, c = pos
    return _flip(x, (k >> 2) & 1), _flip(y, (k >> 1) & 1), _flip(c, k & 1)


def _logical(p):
    return 4 * p[0] + 2 * p[1] + p[2]


def _gather_weights_call(shard):
    m_per = shard.shape[0]

    def body(x_ref, out_ref, xb_ref, send_sems, recv_sems, local_sem):
        x, y, c = _mesh_pos()
        me, sibling = (x, y, c), (x, y, 1 - c)
        chips = [(1 - x, y), (x, 1 - y), (1 - x, 1 - y)]
        xb_ref[...] = x_ref[...].astype(BF16)

        def rows(p):
            return out_ref.at[pl.ds(pl.multiple_of(_logical(p) * m_per, 16), m_per), :]

        def copy(k, block, to, src=None):
            return pltpu.make_async_remote_copy(
                src_ref=rows(block) if src is None else src, dst_ref=rows(block),
                send_sem=send_sems.at[k], recv_sem=recv_sems.at[k],
                device_id=to, device_id_type=pl.DeviceIdType.MESH)

        mine = pltpu.make_async_copy(xb_ref, rows(me), local_sem)
        mine.start()
        first = [copy(0, me, sibling, src=xb_ref)]
        first += [copy(1 + j, me, (*chip, c), src=xb_ref) for j, chip in enumerate(chips)]
        for cp in first:
            cp.start()
        passed = [copy(4 + j, (*chip, c), sibling) for j, chip in enumerate(chips)]
        for j, chip in enumerate(chips):
            copy(1 + j, (*chip, c), me).wait_recv()
            passed[j].start()
        copy(0, sibling, me).wait_recv()
        for j, chip in enumerate(chips):
            copy(4 + j, (*chip, 1 - c), me).wait_recv()
        for cp in first + passed:
            cp.wait_send()
        mine.wait()

    return pl.pallas_call(
        body, name="gather_weights",
        out_shape=jax.ShapeDtypeStruct((N_DEV * m_per, LANES), BF16),
        in_specs=[pl.BlockSpec(memory_space=pltpu.VMEM)],
        out_specs=pl.BlockSpec(memory_space=pltpu.VMEM),
        scratch_shapes=[pltpu.VMEM((m_per, LANES), BF16), pltpu.SemaphoreType.DMA((7,)),
                        pltpu.SemaphoreType.DMA((7,)), pltpu.SemaphoreType.DMA],
        compiler_params=pltpu.CompilerParams(vmem_limit_bytes=VMEM_LIMIT),
    )(shard)


def _adamw(w, g, m, v):
    m = ADAM_B1 * m + (1.0 - ADAM_B1) * g
    v = ADAM_B2 * v + (1.0 - ADAM_B2) * (g * g)
    m_hat = m / (1.0 - ADAM_B1 ** ADAM_STEP)
    v_hat = v / (1.0 - ADAM_B2 ** ADAM_STEP)
    delta = -ADAM_LR * (m_hat / (jnp.sqrt(v_hat) + ADAM_EPS) + ADAM_WD * w)
    return delta, m, v


def _reduce_adamw_call(parts, small, w_pk, m_pk, v_pk, w_s, m_s, v_s):
    chunk = 16
    n_chunks = R_PACK // chunk

    def body(parts_ref, small_ref, w_ref, m_ref, v_ref, ws_ref, ms_ref, vs_ref,
             g_ref, d_ref, nm_ref, nv_ref, gs_ref, ds_ref, nms_ref, nvs_ref,
             recv_ref, recv_s_ref, send_sems, recv_sems, send_s_sems, recv_s_sems, local_sem):
        pos = _mesh_pos()
        me = _logical(pos)

        def big(k):
            peer = _peer(pos, k)
            return pltpu.make_async_remote_copy(
                src_ref=parts_ref.at[_logical(peer)], dst_ref=recv_ref.at[k],
                send_sem=send_sems.at[k], recv_sem=recv_sems.at[k],
                device_id=peer, device_id_type=pl.DeviceIdType.MESH)

        def tiny(k):
            return pltpu.make_async_remote_copy(
                src_ref=small_ref, dst_ref=recv_s_ref.at[k],
                send_sem=send_s_sems.at[k], recv_sem=recv_s_sems.at[k],
                device_id=_peer(pos, k), device_id_type=pl.DeviceIdType.MESH)

        own = pltpu.make_async_copy(parts_ref.at[me], recv_ref.at[0], local_sem)
        own.start()
        for k in range(1, N_DEV):
            tiny(k).start()
        for k in range(1, N_DEV):
            big(k).start()
        recv_s_ref[0] = small_ref[...]
        for k in range(1, N_DEV):
            tiny(k).wait_recv()
        acc = recv_s_ref[me]
        for a in range(1, N_DEV):
            acc = acc + recv_s_ref[lax.bitwise_xor(me, a)]
        row = lax.broadcasted_iota(jnp.int32, (R_SMALL, LANES), 0)
        gs = jnp.where(row == 8, acc + pltpu.roll(acc, 64, 1), acc)
        gs = jnp.where(row == ROW_LOSS, jnp.sum(acc, axis=1, keepdims=True) * (0.5 / D_MODEL), gs)
        gs_ref[...] = gs
        ds, nms, nvs = _adamw(ws_ref[...], gs, ms_ref[...], vs_ref[...])
        ds_ref[...] = ds
        nms_ref[...] = nms
        nvs_ref[...] = nvs

        own.wait()
        for k in range(1, N_DEV):
            big(k).wait_recv()

        def step(t, carry):
            rows = pl.ds(pl.multiple_of(t * chunk, chunk), chunk)
            g = recv_ref[0, rows, :]
            for k in range(1, N_DEV):
                g = g + recv_ref[k, rows, :]
            d, nm, nv = _adamw(w_ref[rows, :], g, m_ref[rows, :], v_ref[rows, :])
            g_ref[rows, :] = g
            d_ref[rows, :] = d
            nm_ref[rows, :] = nm
            nv_ref[rows, :] = nv
            return carry

        lax.fori_loop(0, n_chunks, step, 0)
        for k in range(1, N_DEV):
            tiny(k).wait_send()
            big(k).wait_send()

    vm = pl.BlockSpec(memory_space=pltpu.VMEM)
    big_shape = jax.ShapeDtypeStruct((R_PACK, LANES), F32)
    small_shape = jax.ShapeDtypeStruct((R_SMALL, LANES), F32)
    return pl.pallas_call(
        body, name="reduce_adamw",
        in_specs=[pl.BlockSpec(memory_space=pl.ANY)] + [vm] * 7,
        out_specs=[vm] * 8,
        out_shape=[big_shape] * 4 + [small_shape] * 4,
        scratch_shapes=[pltpu.VMEM((N_DEV, R_PACK, LANES), F32), pltpu.VMEM((N_DEV, R_SMALL, LANES), F32),
                        pltpu.SemaphoreType.DMA((N_DEV,)), pltpu.SemaphoreType.DMA((N_DEV,)),
                        pltpu.SemaphoreType.DMA((N_DEV,)), pltpu.SemaphoreType.DMA((N_DEV,)),
                        pltpu.SemaphoreType.DMA],
        compiler_params=pltpu.CompilerParams(vmem_limit_bytes=VMEM_LIMIT),
    )(parts, small, w_pk, m_pk, v_pk, w_s, m_s, v_s)


def _pack_shard(w_in, w_uq, w_ukv, w_out):
    return jnp.concatenate([a.reshape(-1, LANES) for a in (w_in, w_uq, w_ukv, w_out)], axis=0)


def _unpack_shard(p):
    w_in = p[0:R_WIN].reshape(1, D_MODEL, N_IN // N_DEV)
    w_uq = p[R_WIN:R_WIN + R_WUQ].reshape(1, B_Q_RANK // N_DEV, 384)
    w_ukv = p[R_WIN + R_WUQ:R_WIN + R_WUQ + R_WUKV].reshape(1, B_KV_RANK, 768 // N_DEV)
    w_out = p[R_WIN + R_WUQ + R_WUKV:].reshape(1, D_MODEL // N_DEV, D_MODEL)
    return w_in, w_uq, w_ukv, w_out


def _pack_small(norm_in, a_q, a_k, b_cq, b_ckv, b_q, b_k):
    def row(v):
        return jnp.pad(v.reshape(1, -1), ((0, 0), (0, LANES - v.size)))
    rows = [norm_in.reshape(8, LANES), row(a_q), row(a_k), b_cq.reshape(3, LANES), b_ckv.reshape(2, LANES),
            row(b_q), row(b_k), jnp.zeros((R_SMALL - 17, LANES), F32)]
    return jnp.concatenate(rows, axis=0)


def _unpack_small(s):
    return (s[0:8].reshape(1, D_MODEL), s[8:9, :64], s[9:10, :64], s[10:13].reshape(1, B_Q_RANK),
            s[13:15].reshape(1, B_KV_RANK), s[15:16, :B_QK_DIM], s[16:17, :B_QK_DIM])


def _full_weights(gathered):
    g = gathered
    w_in = g[:, 0:R_WIN].reshape(N_DEV, D_MODEL, N_IN // N_DEV).transpose(1, 0, 2).reshape(D_MODEL, N_IN)
    w_uq = g[:, R_WIN:R_WIN + R_WUQ].reshape(B_Q_RANK, 384)
    w_ukv = g[:, R_WIN + R_WUQ:R_WIN + R_WUQ + R_WUKV].reshape(N_DEV, B_KV_RANK, 768 // N_DEV)
    w_ukv = w_ukv.transpose(1, 0, 2).reshape(B_KV_RANK, 768)
    w_out = g[:, R_WIN + R_WUQ + R_WUKV:].reshape(D_MODEL, D_MODEL)
    k0, k1 = w_in[:, 512:576], w_in[:, 576:640]
    v0, v1 = w_in[:, 640:704], w_in[:, 704:768]
    kr = w_in[:, 1920:1952]
    z64 = jnp.zeros((D_MODEL, 64), BF16)
    z32 = jnp.zeros((D_MODEL, 32), BF16)
    kr_blk = jnp.concatenate([z64, kr, z32], axis=1)
    w_wide = jnp.concatenate([w_in[:, 0:512], k0, k0, k1, k1, v0, v0, v1, v1, w_in[:, 768:1280], w_in[:, 1280:1664],
                              w_in[:, 1664:1920], kr_blk, kr_blk, kr_blk, kr_blk, w_in[:, 1952:2464]], axis=1)
    wuq = jnp.pad(w_uq.reshape(B_Q_RANK, B_HEADS, B_QK_DIM), ((0, 0), (0, 0), (0, LANES - B_QK_DIM)))
    wuq = wuq.reshape(B_Q_RANK, 512)
    ukv = w_ukv.reshape(B_KV_RANK, B_HEADS, B_NOPE_DIM + B_V_DIM)
    wuk = jnp.pad(ukv[:, :, :B_NOPE_DIM], ((0, 0), (0, 0), (0, LANES - B_NOPE_DIM))).reshape(B_KV_RANK, 512)
    wuv = ukv[:, :, B_NOPE_DIM:].reshape(B_KV_RANK, 512)
    return w_wide, wuq, wuk, wuv, w_out


def _narrow_grads(dw_wide, dwuq, dwuk, dwuv):
    dw_in = jnp.concatenate([
        dw_wide[:, 0:512], dw_wide[:, 512:576], dw_wide[:, 640:704], dw_wide[:, 768:832], dw_wide[:, 896:960],
        dw_wide[:, 1024:1536], dw_wide[:, 1536:1920], dw_wide[:, 1920:2176],
        dw_wide[:, O_KR + 64:O_KR + 96], dw_wide[:, 2688:3200]], axis=1)
    dw_uq = dwuq.reshape(B_Q_RANK, B_HEADS, LANES)[:, :, :B_QK_DIM].reshape(B_Q_RANK, 384)
    dk = dwuk.reshape(B_KV_RANK, B_HEADS, LANES)[:, :, :B_NOPE_DIM]
    dv = dwuv.reshape(B_KV_RANK, B_HEADS, B_V_DIM)
    dw_ukv = jnp.concatenate([dk, dv], axis=2).reshape(B_KV_RANK, 768)
    return dw_in, dw_uq, dw_ukv


def _rope_tables(s_len):
    row = jnp.arange(s_len // GRID_W, dtype=F32)
    col = jnp.arange(GRID_W, dtype=F32)

    def parts(dim):
        half = dim // 2
        inv = 1.0 / (ROPE_THETA ** (jnp.arange(0, half, 2, dtype=F32) / half))
        ar, ac = row[:, None] * inv[None, :], col[:, None] * inv[None, :]
        zr, zc = jnp.zeros_like(ar), jnp.zeros_like(ac)
        cos_c = jnp.concatenate([zc, zc, jnp.cos(ac), jnp.cos(ac)], axis=1)
        cos_r = jnp.concatenate([jnp.cos(ar), jnp.cos(ar), zr, zr], axis=1)
        sin_c = jnp.concatenate([zc, zc, -jnp.sin(ac), jnp.sin(ac)], axis=1)
        sin_r = jnp.concatenate([-jnp.sin(ar), jnp.sin(ar), zr, zr], axis=1)
        return cos_c, cos_r, sin_c, sin_r

    tabs = [jnp.tile(t, (1, 2)) for t in parts(A_HEAD_DIM)]
    for n, t in enumerate(parts(B_ROPE_DIM)):
        lead = jnp.full((t.shape[0], B_NOPE_DIM), 1.0 if n == 0 else 0.0, F32)
        tail = jnp.full((t.shape[0], LANES - B_QK_DIM), 1.0 if n == 0 else 0.0, F32)
        tabs.append(jnp.concatenate([lead, t, tail], axis=1))
    return tuple(tabs)


def _lane_major(a, step):
    return a[:, ::step].T[:, None, :]


def kernel(x, norm_in, w_in, a_q_norm, a_k_norm, b_cq_norm, b_ckv_norm, w_uq, w_ukv, b_q_norm, b_k_norm, w_out, loss_target, m_norm_in, m_w_in, m_a_q_norm, m_a_k_norm, m_b_cq_norm, m_b_ckv_norm, m_w_uq, m_w_ukv, m_b_q_norm, m_b_k_norm, m_w_out, v_norm_in, v_w_in, v_a_q_norm, v_a_k_norm, v_b_cq_norm, v_b_ckv_norm, v_w_uq, v_w_ukv, v_b_q_norm, v_b_k_norm, v_w_out):
    s_len = x.shape[1]
    tm = min(256, s_len)
    tq, tk = min(512, s_len), min(2048, s_len)
    ftq, ftk = min(256, s_len), min(1024, s_len)
    x2 = x.reshape(s_len, D_MODEL)
    t2 = loss_target.reshape(s_len, D_MODEL)

    w_pk = _pack_shard(w_in[0], w_uq[0], w_ukv[0], w_out[0])
    gathered = _gather_weights_call(w_pk).reshape(N_DEV, R_PACK, LANES)
    w_wide, wuq, wuk, wuv, wout = _full_weights(gathered)

    def dup(v, pad_to=None):
        v = v.reshape(1, -1)
        if pad_to is None:
            return jnp.concatenate([v, v], axis=1)
        return jnp.pad(v, ((0, 0), (0, pad_to - v.shape[1])))

    g64 = jnp.asarray(np.kron(np.eye(2), np.ones((64, 64))), dtype=BF16)
    consts = dict(gin=norm_in, w=w_wide, wuq=wuq, wuk=wuk, wuv=wuv, gq=dup(a_q_norm), gk=dup(a_k_norm),
                  gcq=b_cq_norm, gckv=b_ckv_norm, gqb=dup(b_q_norm, LANES), gkb=dup(b_k_norm, LANES), g64=g64)
    tabs = _rope_tables(s_len)

    qa, ka, va, va_t, ga, qb, kb, vb, vb_t, gb = _pre_fwd_call(x2, consts, tabs, tm)
    oa, lse_a_t = _attn_fwd_t_call(qa, ka, va_t, groups=A_KV_HEADS, sub=4, masked=True, scale=None, tq=ftq, tk=ftk,
                                   name="attn_fwd_a")
    ob, lse_b_t = _attn_fwd_t_call(qb, kb, vb_t, groups=1, sub=B_HEADS, masked=False, scale=None, tq=ftq,
                                   tk=ftk, name="attn_fwd_b")
    dh, doa, dob, dga, dgb, dw_out, loss_row = _out_call(x2, t2, oa, ob, ga, gb, wout, tm)

    ck = min(512, s_len)
    dqa, dka_t, dva_t = _attn_bwd_q_call(qa, ka, va, doa, oa, lse_a_t, groups=A_KV_HEADS, sub=4, masked=True,
                                         tq=tq, tk=tk, ck=ck, name="attn_bwd_a")
    dqb, dkb_t, dvb_t = _attn_bwd_q_call(qb, kb, vb, dob, ob, lse_b_t, groups=2, sub=2, masked=False,
                                         tq=tq, tk=tk, ck=ck, name="attn_bwd_b")
    dx, dproj_b, xnb, dwuq, dwuk, dwuv, small = _pre_bwd_call(
        x2, dh, dqa, dka_t, dva_t, dga, dqb, dkb_t, dvb_t, dgb, consts, tabs, tm)
    dw_wide = _dw_in_call(xnb, dproj_b, min(512, s_len), 640)

    dw_in, dw_uq, dw_ukv = _narrow_grads(dw_wide, dwuq, dwuk, dwuv)
    ci, cq, ckv, co = N_IN // N_DEV, B_Q_RANK // N_DEV, 768 // N_DEV, D_MODEL // N_DEV
    parts = jnp.stack([
        _pack_shard(dw_in[:, ci * d:ci * (d + 1)], dw_uq[cq * d:cq * (d + 1)], dw_ukv[:, ckv * d:ckv * (d + 1)],
                    dw_out[co * d:co * (d + 1)]) for d in range(N_DEV)])
    small = jnp.concatenate([small[:ROW_LOSS], loss_row, small[ROW_LOSS + 1:]], axis=0)

    m_pk = _pack_shard(m_w_in[0], m_w_uq[0], m_w_ukv[0], m_w_out[0])
    v_pk = _pack_shard(v_w_in[0], v_w_uq[0], v_w_ukv[0], v_w_out[0])
    w_s = _pack_small(norm_in, a_q_norm, a_k_norm, b_cq_norm, b_ckv_norm, b_q_norm, b_k_norm)
    m_s = _pack_small(m_norm_in, m_a_q_norm, m_a_k_norm, m_b_cq_norm, m_b_ckv_norm, m_b_q_norm, m_b_k_norm)
    v_s = _pack_small(v_norm_in, v_a_q_norm, v_a_k_norm, v_b_cq_norm, v_b_ckv_norm, v_b_q_norm, v_b_k_norm)
    g_pk, d_pk, nm_pk, nv_pk, g_s, d_s, nm_s, nv_s = _reduce_adamw_call(parts, small, w_pk, m_pk, v_pk, w_s, m_s, v_s)

    def leaves(pk, sm):
        wi, uq, ukv, wo = _unpack_shard(pk)
        n_in, aq, ak, bcq, bckv, bq, bk = _unpack_small(sm)
        return [n_in, wi, aq, ak, bcq, bckv, uq, ukv, bq, bk, wo]

    loss = g_s[ROW_LOSS, 0]
    grad_x = dx.reshape(1, s_len, D_MODEL)
    return (loss, grad_x, *leaves(g_pk, g_s), *leaves(d_pk, d_s), *leaves(nm_pk, nm_s), *leaves(nv_pk, nv_s))
```

```python
import functools

import numpy as np
import jax
import jax.numpy as jnp
from jax import lax
from jax.experimental import pallas as pl
from jax.experimental.pallas import tpu as pltpu

F32 = jnp.float32
BF16 = jnp.bfloat16

D_MODEL = 1024
GRID_W = 64
ROPE_THETA = 10000.0
EPS = 1e-6
A_HEAD_DIM = 64
A_HEADS = 8
A_KV_HEADS = 2
B_HEADS = 4
B_NOPE_DIM = 64
B_ROPE_DIM = 32
B_QK_DIM = 96
B_V_DIM = 128
B_Q_RANK = 384
B_KV_RANK = 256
N_IN = 2464
N_DEV = 8

ADAM_LR = 0.001
ADAM_B1 = 0.9
ADAM_B2 = 0.999
ADAM_EPS = 1e-08
ADAM_WD = 0.01
ADAM_STEP = 10

QA_SCALE = 0.125
QB_SCALE = 1.0 / float(np.sqrt(B_QK_DIM))

LANES = 128
O_QA, O_KA, O_VA, O_GA, O_CQ, O_CKV, O_KR, O_GB, N_WIDE = 0, 512, 768, 1024, 1536, 1920, 2176, 2688, 3200
R_QA, R_KA, R_CQ, R_CKV, R_KR, R_WIDTH = 0, 512, 768, 1152, 1408, 1920

R_WIN = D_MODEL * (N_IN // N_DEV) // LANES
R_WUQ = (B_Q_RANK // N_DEV) * 384 // LANES
R_WUKV = B_KV_RANK * (768 // N_DEV) // LANES
R_WOUT = (D_MODEL // N_DEV) * D_MODEL // LANES
R_PACK = R_WIN + R_WUQ + R_WUKV + R_WOUT
R_SMALL = 24
ROW_LOSS = 17

VMEM_LIMIT = 56 * 1024 * 1024

NT = (((1,), (1,)), ((), ()))
TN = (((0,), (0,)), ((), ()))


def _dot(a, b):
    return jnp.dot(a, b, preferred_element_type=F32)


def _dot_nt(a, b):
    return lax.dot_general(a, b, NT, preferred_element_type=F32)


def _dot_tn(a, b):
    return lax.dot_general(a, b, TN, preferred_element_type=F32)


def _params(sem=None):
    return pltpu.CompilerParams(dimension_semantics=sem, vmem_limit_bytes=VMEM_LIMIT)


def _full(shape):
    nd = len(shape)
    return pl.BlockSpec(shape, lambda *_: (0,) * nd)


def _swap_sel(rows, shift):
    lane = lax.broadcasted_iota(jnp.int32, (rows, LANES), 1)
    return pltpu.roll(lane, shift, 1) == (lane ^ shift)


def _swap(x, shift, sel):
    return jnp.where(sel, pltpu.roll(x, shift, 1), pltpu.roll(x, LANES - shift, 1))


def _group_sum64(x, g64):
    hi = x.astype(BF16)
    lo = (x - hi.astype(F32)).astype(BF16)
    return _dot(hi, g64) + _dot(lo, g64)


def _row_sum(x):
    return jnp.sum(x, axis=-1, keepdims=True)


def _col_fwd(xs, msum, denom, gain, cos, sin, shift, sel):
    r = lax.rsqrt(msum(xs * xs) * (1.0 / denom) + EPS)
    xh = xs * r
    n = xh * gain
    return n * cos + _swap(n, shift, sel) * sin, xh, r


def _col_bwd(d_out, xh, r, msum, denom, gain, cos, sin, shift, sel):
    dn = d_out * cos + _swap(d_out * sin, shift, sel)
    dgain = jnp.sum(dn * xh, axis=0, keepdims=True)
    dxh = dn * gain
    dx = r * (dxh - xh * (msum(dxh * xh) * (1.0 / denom)))
    return dx, dgain


def _rms_fwd(x, gain):
    r = lax.rsqrt(jnp.mean(x * x, axis=-1, keepdims=True) + EPS)
    xh = x * r
    return xh * gain, xh, r


def _rms_bwd(dy, xh, r, gain):
    dgain = jnp.sum(dy * xh, axis=0, keepdims=True)
    dxh = dy * gain
    dx = r * (dxh - xh * jnp.mean(dxh * xh, axis=-1, keepdims=True))
    return dx, dgain


def _pre_forward(x, gin, w, wuq, wuk, wuv, gq, gk, gcq, gckv, gqb, gkb, ca, sa, cb, sb, g64, tm, raw=None):
    sel16 = _swap_sel(tm, 16)
    sel8 = _swap_sel(tm, 8)
    xn, xh0, r0 = _rms_fwd(x, gin)
    xnb = xn.astype(BF16)
    proj = None
    if raw is None:
        proj = _dot(xnb, w)
        raw = jnp.concatenate([proj[:, O_QA:O_QA + 512], proj[:, O_KA:O_KA + 256], proj[:, O_CQ:O_CQ + B_Q_RANK],
                               proj[:, O_CKV:O_CKV + B_KV_RANK], proj[:, O_KR:O_KR + 512]], axis=1)
    gs64 = functools.partial(_group_sum64, g64=g64)
    qa = [_col_fwd(raw[:, R_QA + LANES * s:R_QA + LANES * (s + 1)], gs64, 64.0, gq, ca, sa, 16, sel16)
          for s in range(4)]
    ka = [_col_fwd(raw[:, R_KA + LANES * s:R_KA + LANES * (s + 1)], _row_sum, 128.0, gk, ca, sa, 16, sel16)
          for s in range(2)]
    cq, cqh, rcq = _rms_fwd(raw[:, R_CQ:R_CQ + B_Q_RANK], gcq)
    cqb = cq.astype(BF16)
    qb_raw = _dot(cqb, wuq)
    qb = [_col_fwd(qb_raw[:, LANES * h:LANES * (h + 1)], _row_sum, float(B_QK_DIM), gqb, cb, sb, 8, sel8)
          for h in range(B_HEADS)]
    ckv, ckvh, rckv = _rms_fwd(raw[:, R_CKV:R_CKV + B_KV_RANK], gckv)
    ckvb = ckv.astype(BF16)
    kb_raw = _dot(ckvb, wuk) + raw[:, R_KR:R_KR + 512]
    vb = _dot(ckvb, wuv)
    kb = [_col_fwd(kb_raw[:, LANES * h:LANES * (h + 1)], _row_sum, float(B_QK_DIM), gkb, cb, sb, 8, sel8)
          for h in range(B_HEADS)]
    return dict(xh0=xh0, r0=r0, xnb=xnb, proj=proj, raw=raw, qa=qa, ka=ka, cqh=cqh, rcq=rcq, cqb=cqb, qb=qb,
                ckvh=ckvh, rckv=rckv, ckvb=ckvb, kb=kb, vb=vb, sel16=sel16, sel8=sel8, gs64=gs64)


def _rope_tiles(tab_refs, i, tm):
    per_tile = tm // GRID_W
    out = []
    for t in range(4):
        col_ref, row_ref = tab_refs[2 * t], tab_refs[2 * t + 1]
        col = col_ref[...]
        out.append(jnp.concatenate([col + row_ref[pl.ds(i * per_tile + b, 1), :] for b in range(per_tile)], axis=0))
    return out


_PRE_IN_NAMES = ("gin", "w", "wuq", "wuk", "wuv", "gq", "gk", "gcq", "gckv", "gqb", "gkb", "g64")


def _pre_const_specs(consts):
    return [_full(consts[n].shape) for n in _PRE_IN_NAMES]


def _pre_fwd_call(x, consts, tabs, tm):
    s_len = x.shape[0]

    def body(x_ref, *refs):
        tab_refs, refs = refs[:8], refs[8:]
        (gin_ref, w_ref, wuq_ref, wuk_ref, wuv_ref, gq_ref, gk_ref, gcq_ref, gckv_ref, gqb_ref, gkb_ref, g64_ref,
         qa_ref, ka_ref, va_ref, vat_ref, ga_ref, qb_ref, kb_ref, vb_ref, vbt_ref, gb_ref, raw_ref, xnb_ref) = refs
        ca, sa, cb, sb = _rope_tiles(tab_refs, pl.program_id(0), tm)
        f = _pre_forward(x_ref[...], gin_ref[...], w_ref[...], wuq_ref[...], wuk_ref[...], wuv_ref[...],
                         gq_ref[...], gk_ref[...], gcq_ref[...], gckv_ref[...], gqb_ref[...], gkb_ref[...],
                         ca, sa, cb, sb, g64_ref[...], tm)
        proj = f["proj"]
        raw_ref[...] = f["raw"]
        xnb_ref[...] = f["xnb"]
        for s in range(4):
            qa_ref[:, LANES * s:LANES * (s + 1)] = (f["qa"][s][0] * QA_SCALE).astype(BF16)
        for s in range(2):
            ka_ref[:, LANES * s:LANES * (s + 1)] = f["ka"][s][0].astype(BF16)
        va = proj[:, O_VA:O_VA + 256]
        va_ref[...] = va.astype(BF16)
        vat_ref[...] = va.T.astype(BF16)
        ga_ref[...] = proj[:, O_GA:O_GA + 512]
        for h in range(B_HEADS):
            qb_ref[:, LANES * h:LANES * (h + 1)] = (f["qb"][h][0] * QB_SCALE).astype(BF16)
            kb_ref[:, LANES * h:LANES * (h + 1)] = f["kb"][h][0].astype(BF16)
        vb_ref[...] = f["vb"].astype(BF16)
        vbt_ref[...] = f["vb"].T.astype(BF16)
        gb_ref[...] = proj[:, O_GB:O_GB + 512]

    def rows(width):
        return pl.BlockSpec((tm, width), lambda i: (i, 0))

    def cols(height):
        return pl.BlockSpec((height, tm), lambda i: (0, i))

    outs = [((s_len, 512), BF16, rows(512)), ((s_len, 256), BF16, rows(256)), ((s_len, 256), BF16, rows(256)),
            ((256, s_len), BF16, cols(256)), ((s_len, 512), F32, rows(512)), ((s_len, 512), BF16, rows(512)),
            ((s_len, 512), BF16, rows(512)), ((s_len, 512), BF16, rows(512)), ((512, s_len), BF16, cols(512)),
            ((s_len, 512), F32, rows(512)), ((s_len, R_WIDTH), F32, rows(R_WIDTH)),
            ((s_len, D_MODEL), BF16, rows(D_MODEL))]
    return pl.pallas_call(
        body, name="pre_fwd", grid=(s_len // tm,),
        in_specs=[rows(D_MODEL)] + [_full(t.shape) for t in tabs] + _pre_const_specs(consts),
        out_specs=[sp for _, _, sp in outs],
        out_shape=[jax.ShapeDtypeStruct(sh, dt) for sh, dt, _ in outs],
        compiler_params=_params(("arbitrary",)),
    )(x, *tabs, *[consts[n] for n in _PRE_IN_NAMES])


def _head_masks(rows):
    lane = lax.broadcasted_iota(jnp.int32, (rows, LANES), 1)
    return lane < 64, lane >= 64


def _lane_fold(x, op):
    out = x[:, 0:LANES]
    for c in range(1, x.shape[1] // LANES):
        out = op(out, x[:, LANES * c:LANES * (c + 1)])
    return out


def _row_fold(x, op):
    return op(x.reshape(x.shape[0] // 8, 8, x.shape[1]), axis=0)


def _attn_fwd_t_call(q, k, vt, *, groups, sub, masked, scale, tq, tk, name):
    s_len = q.shape[0]
    qw = LANES * (sub // 2 if masked else sub)
    kvw = LANES if masked else LANES * sub
    n_c = s_len // tk
    kv_mode = pl.Buffered(1) if groups == 1 else None

    def body(q_ref, k_ref, vt_ref, o_ref, lse_ref, s_sc):
        keep = _head_masks(tq) if masked else None

        def kv_of(hh):
            return slice(0, LANES) if masked else slice(LANES * hh, LANES * (hh + 1))

        def q_of(hh):
            if not masked:
                return q_ref[:, LANES * hh:LANES * (hh + 1)]
            qp = q_ref[:, LANES * (hh // 2):LANES * (hh // 2 + 1)]
            return jnp.where(keep[hh % 2], qp, jnp.zeros_like(qp))

        def scores(hh, qm, c, mx):
            s_t = _dot_nt(k_ref[tk * c:tk * (c + 1), kv_of(hh)], qm)
            if scale is not None:
                s_t = s_t * scale
            s_sc[hh % 2, c] = s_t
            return jnp.maximum(mx, _row_fold(s_t, jnp.max))

        neg = jnp.full((8, tq), -jnp.inf, F32)
        qm_next = q_of(0)
        mx_next = neg
        for c in range(n_c):
            mx_next = scores(0, qm_next, c, mx_next)
        outs = []
        for hh in range(sub):
            m = jnp.max(mx_next, axis=0, keepdims=True)
            if hh + 1 < sub:
                qm_next = q_of(hh + 1)
                mx_next = neg
            lsum = jnp.zeros((8, tq), F32)
            acc = jnp.zeros((LANES, tq), F32)
            for c in range(n_c):
                if hh + 1 < sub:
                    mx_next = scores(hh + 1, qm_next, c, mx_next)
                p_t = jnp.exp(s_sc[hh % 2, c] - m)
                lsum = lsum + _row_fold(p_t, jnp.sum)
                acc = acc + _dot(vt_ref[kv_of(hh), tk * c:tk * (c + 1)], p_t.astype(BF16))
            l = jnp.sum(lsum, axis=0, keepdims=True)
            outs.append((acc / l).T)
            lse_ref[hh] = m + jnp.log(l)
        if masked:
            for pr in range(sub // 2):
                o_ref[:, LANES * pr:LANES * (pr + 1)] = jnp.where(keep[0], outs[2 * pr], outs[2 * pr + 1])
        else:
            for hh in range(sub):
                o_ref[:, LANES * hh:LANES * (hh + 1)] = outs[hh]

    return pl.pallas_call(
        body, name=name, grid=(groups, s_len // tq),
        in_specs=[pl.BlockSpec((tq, qw), lambda g, i: (i, g)),
                  pl.BlockSpec((s_len, kvw), lambda g, i: (0, g), pipeline_mode=kv_mode),
                  pl.BlockSpec((kvw, s_len), lambda g, i: (g, 0), pipeline_mode=kv_mode)],
        out_specs=[pl.BlockSpec((tq, qw), lambda g, i: (i, g)),
                   pl.BlockSpec((sub, 1, tq), lambda g, i: (g, 0, i))],
        out_shape=[jax.ShapeDtypeStruct((s_len, groups * qw), F32),
                   jax.ShapeDtypeStruct((groups * sub, 1, s_len), F32)],
        scratch_shapes=[pltpu.VMEM((min(sub, 2), n_c, tk, tq), F32)],
        compiler_params=_params(("arbitrary", "arbitrary")),
    )(q, k, vt)


def _attn_fwd_call(q, k, v, *, groups, sub, masked, scale, tq, tk, name):
    s_len = q.shape[0]
    qw = LANES * (sub // 2 if masked else sub)
    n_c = s_len // tk
    log2e = float(np.log2(np.e))
    mul = log2e if scale is None else scale * log2e

    def body(q_ref, k_ref, v_ref, o_ref, lse_ref, s_sc):
        keep = _head_masks(tq) if masked else None
        outs = []
        for hh in range(sub):
            if masked:
                qp = q_ref[:, LANES * (hh // 2):LANES * (hh // 2 + 1)]
                qm = jnp.where(keep[hh % 2], qp, jnp.zeros_like(qp))
                kv_cols = slice(0, LANES)
            else:
                qm = q_ref[:, LANES * hh:LANES * (hh + 1)]
                kv_cols = slice(LANES * hh, LANES * (hh + 1))
            buf = hh % 2

            def scores(c, mx):
                rows = pl.ds(pl.multiple_of(c * tk, tk), tk)
                s = _dot_nt(qm, k_ref[rows, kv_cols]) * mul
                s_sc[buf, c] = s
                return jnp.maximum(mx, _lane_fold(s, jnp.maximum))

            mx = lax.fori_loop(0, n_c, scores, jnp.full((tq, LANES), -jnp.inf, F32), unroll=True)
            m = jnp.max(mx, axis=1, keepdims=True)

            def weights(c, carry):
                lsum, acc = carry
                rows = pl.ds(pl.multiple_of(c * tk, tk), tk)
                p = jnp.exp2(s_sc[buf, c] - m)
                return lsum + _lane_fold(p, jnp.add), acc + _dot(p.astype(BF16), v_ref[rows, kv_cols])

            lsum, acc = lax.fori_loop(0, n_c, weights, (jnp.zeros((tq, LANES), F32), jnp.zeros((tq, LANES), F32)),
                                      unroll=True)
            l = jnp.sum(lsum, axis=1, keepdims=True)
            outs.append(acc / l)
            lse_ref[hh] = jnp.broadcast_to((m + jnp.log2(l)) * (1.0 / log2e), (tq, LANES))
        if masked:
            for pr in range(sub // 2):
                o_ref[:, LANES * pr:LANES * (pr + 1)] = jnp.where(keep[0], outs[2 * pr], outs[2 * pr + 1])
        else:
            for hh in range(sub):
                o_ref[:, LANES * hh:LANES * (hh + 1)] = outs[hh]

    kvw = LANES if masked else LANES * sub
    return pl.pallas_call(
        body, name=name, grid=(groups, s_len // tq),
        in_specs=[pl.BlockSpec((tq, qw), lambda g, i: (i, g)),
                  pl.BlockSpec((s_len, kvw), lambda g, i: (0, g)),
                  pl.BlockSpec((s_len, kvw), lambda g, i: (0, g))],
        out_specs=[pl.BlockSpec((tq, qw), lambda g, i: (i, g)),
                   pl.BlockSpec((sub, tq, LANES), lambda g, i: (g, i, 0))],
        out_shape=[jax.ShapeDtypeStruct((s_len, groups * qw), F32),
                   jax.ShapeDtypeStruct((groups * sub, s_len, LANES), F32)],
        scratch_shapes=[pltpu.VMEM((min(sub, 2), n_c, tq, tk), F32)],
        compiler_params=_params(("arbitrary", "arbitrary")),
    )(q, k, v)


def _attn_bwd_q_call(q, k, v, do, o, lse_t, *, groups, sub, masked, tq, tk, ck, name):
    s_len = q.shape[0]
    qw = LANES * (sub // 2 if masked else sub)
    kvw = LANES if masked else LANES * sub
    n_c = tk // ck

    def body(q_ref, k_ref, v_ref, do_ref, o_ref, lse_ref, dq_ref, dkt_ref, dvt_ref):
        j = pl.program_id(1)
        i = pl.program_id(2)

        @pl.when((j == 0) & (i == 0))
        def _():
            dq_ref[...] = jnp.zeros(dq_ref.shape, F32)

        @pl.when(i == 0)
        def _():
            dkt_ref[...] = jnp.zeros(dkt_ref.shape, F32)
            dvt_ref[...] = jnp.zeros(dvt_ref.shape, F32)

        lkeep = _head_masks(tq) if masked else None
        heads = []
        for hh in range(sub):
            if masked:
                cols = slice(LANES * (hh // 2), LANES * (hh // 2 + 1))
                kv = slice(0, LANES)
                qp, dop = q_ref[:, cols], do_ref[:, cols]
                qm = jnp.where(lkeep[hh % 2], qp, jnp.zeros_like(qp))
                dom = jnp.where(lkeep[hh % 2], dop, jnp.zeros_like(dop))
            else:
                cols = kv = slice(LANES * hh, LANES * (hh + 1))
                qm, dom = q_ref[:, cols], do_ref[:, cols]
            delta = jnp.sum(dom.astype(F32) * o_ref[:, cols], axis=1, keepdims=True)
            lse = jnp.broadcast_to(lse_ref[hh], (LANES, tq)).T[:, 0:1]
            heads.append((cols, kv, qm, dom, qm.T, dom.T, delta, lse))

        def products(hh, c):
            _, kv, qm, dom, _, _, _, _ = heads[hh]
            return _dot_nt(qm, k_ref[ck * c:ck * (c + 1), kv]), _dot_nt(dom, v_ref[ck * c:ck * (c + 1), kv])

        items = [(hh, c) for hh in range(sub) for c in range(n_c)]
        dq_acc = [jnp.zeros((tq, LANES), F32) for _ in range(sub)]
        nxt = products(*items[0])
        for n, (hh, c) in enumerate(items):
            s, dp = nxt
            if n + 1 < len(items):
                nxt = products(*items[n + 1])
            _, kv, qm, dom, qmt, domt, delta, lse = heads[hh]
            p = jnp.exp(s - lse)
            ds = p * (dp - delta)
            p_b = p.astype(BF16)
            ds_b = ds.astype(BF16)
            kcols = slice(ck * c, ck * (c + 1))
            dvt_ref[kv, kcols] += _dot(domt, p_b)
            dkt_ref[kv, kcols] += _dot(qmt, ds_b)
            dq_acc[hh] = dq_acc[hh] + _dot(ds_b, k_ref[kcols, kv])
        rows = pl.ds(pl.multiple_of(i * tq, tq), tq)
        if masked:
            for pr in range(sub // 2):
                dq_ref[rows, LANES * pr:LANES * (pr + 1)] += jnp.where(lkeep[0], dq_acc[2 * pr], dq_acc[2 * pr + 1])
        else:
            for hh in range(sub):
                dq_ref[rows, LANES * hh:LANES * (hh + 1)] += dq_acc[hh]

    return pl.pallas_call(
        body, name=name, grid=(groups, s_len // tk, s_len // tq),
        in_specs=[pl.BlockSpec((tq, qw), lambda g, j, i: (i, g)),
                  pl.BlockSpec((tk, kvw), lambda g, j, i: (j, g)),
                  pl.BlockSpec((tk, kvw), lambda g, j, i: (j, g)),
                  pl.BlockSpec((tq, qw), lambda g, j, i: (i, g)),
                  pl.BlockSpec((tq, qw), lambda g, j, i: (i, g)),
                  pl.BlockSpec((sub, 1, tq), lambda g, j, i: (g, 0, i))],
        out_specs=[pl.BlockSpec((s_len, qw), lambda g, j, i: (0, g)),
                   pl.BlockSpec((kvw, tk), lambda g, j, i: (g, j)),
                   pl.BlockSpec((kvw, tk), lambda g, j, i: (g, j))],
        out_shape=[jax.ShapeDtypeStruct((s_len, groups * qw), F32),
                   jax.ShapeDtypeStruct((groups * kvw, s_len), F32),
                   jax.ShapeDtypeStruct((groups * kvw, s_len), F32)],
        compiler_params=_params(("arbitrary", "arbitrary", "arbitrary")),
    )(q, k, v, do, o, lse_t)


def _attn_bwd_call(q, k, v, do, lse_t, delta_t, *, groups, sub, scale, tq, tk, name):
    s_len = q.shape[0]
    masked = sub > 1
    qw = LANES * (sub // 2 if masked else 1)
    n_k = s_len // tk

    def body(q_ref, k_ref, v_ref, do_ref, lse_ref, dl_ref, dq_ref, dk_ref, dv_ref, dq_sc):
        i = pl.program_id(1)
        j = pl.program_id(2)

        @pl.when((i == 0) & (j == 0))
        def _():
            dk_ref[...] = jnp.zeros(dk_ref.shape, F32)
            dv_ref[...] = jnp.zeros(dv_ref.shape, F32)

        @pl.when(j == 0)
        def _():
            dq_sc[...] = jnp.zeros(dq_sc.shape, F32)

        kk = k_ref[...]
        vv = v_ref[...]
        keep = _head_masks(tq) if masked else None
        dk_t = jnp.zeros((tk, LANES), F32)
        dv_t = jnp.zeros((tk, LANES), F32)
        for hh in range(sub):
            if masked:
                cols = slice(LANES * (hh // 2), LANES * (hh // 2 + 1))
                qp = q_ref[:, cols]
                dop = do_ref[:, cols]
                qm = jnp.where(keep[hh % 2], qp, jnp.zeros_like(qp))
                dom = jnp.where(keep[hh % 2], dop, jnp.zeros_like(dop))
            else:
                cols = slice(0, LANES)
                qm = q_ref[...]
                dom = do_ref[...]
            s_t = _dot_nt(kk, qm)
            if scale is not None:
                s_t = s_t * scale
            p_t = jnp.exp(s_t - lse_ref[hh])
            dp_t = _dot_nt(vv, dom)
            ds_t = p_t * (dp_t - dl_ref[hh])
            if scale is not None:
                ds_t = ds_t * scale
            p_b = p_t.astype(BF16)
            ds_b = ds_t.astype(BF16)
            dv_t = dv_t + _dot(p_b, dom)
            dk_t = dk_t + _dot(ds_b, qm)
            dq_h = _dot_tn(ds_b, kk)
            if masked:
                dq_h = jnp.where(keep[hh % 2], dq_h, jnp.zeros_like(dq_h))
            dq_sc[:, cols] += dq_h
        rows = pl.ds(pl.multiple_of(j * tk, tk), tk)
        dk_ref[rows, :] += dk_t
        dv_ref[rows, :] += dv_t

        @pl.when(j == n_k - 1)
        def _():
            dq_ref[...] = dq_sc[...]

    return pl.pallas_call(
        body, name=name, grid=(groups, s_len // tq, n_k),
        in_specs=[pl.BlockSpec((tq, qw), lambda g, i, j: (i, g)),
                  pl.BlockSpec((tk, LANES), lambda g, i, j: (j, g)),
                  pl.BlockSpec((tk, LANES), lambda g, i, j: (j, g)),
                  pl.BlockSpec((tq, qw), lambda g, i, j: (i, g)),
                  pl.BlockSpec((sub, 1, tq), lambda g, i, j: (g, 0, i)),
                  pl.BlockSpec((sub, 1, tq), lambda g, i, j: (g, 0, i))],
        out_specs=[pl.BlockSpec((tq, qw), lambda g, i, j: (i, g)),
                   pl.BlockSpec((s_len, LANES), lambda g, i, j: (0, g)),
                   pl.BlockSpec((s_len, LANES), lambda g, i, j: (0, g))],
        out_shape=[jax.ShapeDtypeStruct((s_len, groups * qw), F32),
                   jax.ShapeDtypeStruct((s_len, groups * LANES), F32),
                   jax.ShapeDtypeStruct((s_len, groups * LANES), F32)],
        scratch_shapes=[pltpu.VMEM((tq, qw), F32)],
        compiler_params=_params(("arbitrary", "arbitrary", "arbitrary")),
    )(q, k, v, do, lse_t, delta_t)


def _silu_parts(g):
    sig = 1.0 / (1.0 + jnp.exp(-g))
    return g * sig, sig * (1.0 + g * (1.0 - sig))


def _out_call(x, target, oa, ob, ga, gb, wout, tm):
    s_len = x.shape[0]
    n_t = s_len // tm

    def body(x_ref, t_ref, oa_ref, ob_ref, ga_ref, gb_ref, w_ref,
             dh_ref, doa_ref, dob_ref, dga_ref, dgb_ref, dw_ref, loss_ref):
        i = pl.program_id(0)

        @pl.when(i == 0)
        def _():
            dw_ref[...] = jnp.zeros(dw_ref.shape, F32)
            loss_ref[...] = jnp.zeros(loss_ref.shape, F32)

        oa_v, ob_v = oa_ref[...], ob_ref[...]
        silu_a, dsilu_a = _silu_parts(ga_ref[...])
        silu_b, dsilu_b = _silu_parts(gb_ref[...])
        ya = (oa_v * silu_a).astype(BF16)
        yb = (ob_v * silu_b).astype(BF16)
        h = x_ref[...] + _dot(ya, w_ref[0:512, :]) + _dot(yb, w_ref[512:1024, :])
        err = h - t_ref[...]
        part = jnp.sum(err * err, axis=0, keepdims=True)
        acc = part[:, 0:LANES]
        for c in range(1, D_MODEL // LANES):
            acc = acc + part[:, LANES * c:LANES * (c + 1)]
        loss_ref[...] += acc
        dh = err * (1.0 / D_MODEL)
        dh_ref[...] = dh
        dhb = dh.astype(BF16)
        dya = _dot_nt(dhb, w_ref[0:512, :])
        dyb = _dot_nt(dhb, w_ref[512:1024, :])
        doa = dya * silu_a
        dob = dyb * silu_b
        doa_ref[...] = doa.astype(BF16)
        dob_ref[...] = dob.astype(BF16)
        dga_ref[...] = dya * oa_v * dsilu_a
        dgb_ref[...] = dyb * ob_v * dsilu_b
        dw_ref[0:512, :] += _dot_tn(ya, dhb)
        dw_ref[512:1024, :] += _dot_tn(yb, dhb)

    def rows(width):
        return pl.BlockSpec((tm, width), lambda i: (i, 0))

    outs = [(D_MODEL, F32), (512, BF16), (512, BF16), (512, F32), (512, F32)]
    return pl.pallas_call(
        body, name="out_fwd", grid=(n_t,),
        in_specs=[rows(D_MODEL), rows(D_MODEL), rows(512), rows(512), rows(512), rows(512),
                  _full((D_MODEL, D_MODEL))],
        out_specs=[rows(wd) for wd, _ in outs] + [_full((D_MODEL, D_MODEL)), _full((1, LANES))],
        out_shape=[jax.ShapeDtypeStruct((s_len, wd), dt) for wd, dt in outs]
        + [jax.ShapeDtypeStruct((D_MODEL, D_MODEL), F32), jax.ShapeDtypeStruct((1, LANES), F32)],
        compiler_params=_params(("arbitrary",)),
    )(x, target, oa, ob, ga, gb, wout)


def _pre_bwd_call(x, raw, dh, dqa, dka, dva, dga, dqb, dkb, dvb, dgb, consts, tabs, tm):
    s_len = x.shape[0]

    def body(x_ref, raw_ref, dh_ref, dqa_ref, dkat_ref, dvat_ref, dga_ref, dqb_ref, dkbt_ref, dvbt_ref, dgb_ref,
             *refs):
        tab_refs, refs = refs[:8], refs[8:]
        (gin_ref, w_ref, wuq_ref, wuk_ref, wuv_ref, gq_ref, gk_ref, gcq_ref, gckv_ref, gqb_ref, gkb_ref, g64_ref,
         dx_ref, dproj_ref, dwuq_ref, dwuk_ref, dwuv_ref, small_ref) = refs
        i = pl.program_id(0)
        dka_v, dva_v = dkat_ref[...].T, dvat_ref[...].T
        dkb_v, dvb_v = dkbt_ref[...].T, dvbt_ref[...].T

        @pl.when(i == 0)
        def _():
            dwuq_ref[...] = jnp.zeros(dwuq_ref.shape, F32)
            dwuk_ref[...] = jnp.zeros(dwuk_ref.shape, F32)
            dwuv_ref[...] = jnp.zeros(dwuv_ref.shape, F32)
            small_ref[...] = jnp.zeros(small_ref.shape, F32)

        gin, gq, gk = gin_ref[...], gq_ref[...], gk_ref[...]
        gcq, gckv, gqb, gkb = gcq_ref[...], gckv_ref[...], gqb_ref[...], gkb_ref[...]
        ca, sa, cb, sb = _rope_tiles(tab_refs, i, tm)
        w, wuq, wuk, wuv = w_ref[...], wuq_ref[...], wuk_ref[...], wuv_ref[...]
        f = _pre_forward(x_ref[...], gin, w, wuq, wuk, wuv, gq, gk, gcq, gckv, gqb, gkb,
                         ca, sa, cb, sb, g64_ref[...], tm, raw=raw_ref[...])
        sel16, sel8, gs64 = f["sel16"], f["sel8"], f["gs64"]
        lane = lax.broadcasted_iota(jnp.int32, (tm, LANES), 1)
        low = lane < 64
        zero = jnp.zeros((tm, LANES), F32)
        pieces = []

        dgq = jnp.zeros((1, LANES), F32)
        for s in range(4):
            _, xh, r = f["qa"][s]
            d = dqa_ref[:, LANES * s:LANES * (s + 1)] * QA_SCALE
            dx, dg = _col_bwd(d, xh, r, gs64, 64.0, gq, ca, sa, 16, sel16)
            pieces.append(dx)
            dgq = dgq + dg
        dgk = jnp.zeros((1, LANES), F32)
        for s in range(2):
            _, xh, r = f["ka"][s]
            d = dka_v[:, LANES * s:LANES * (s + 1)]
            d = d + pltpu.roll(d, 64, 1)
            dx, dg = _col_bwd(d, xh, r, _row_sum, 128.0, gk, ca, sa, 16, sel16)
            pieces.append(jnp.where(low, dx, zero))
            dgk = dgk + dg
        for s in range(2):
            d = dva_v[:, LANES * s:LANES * (s + 1)]
            d = d + pltpu.roll(d, 64, 1)
            pieces.append(jnp.where(low, d, zero))
        pieces.append(dga_ref[...])

        dgqb = jnp.zeros((1, LANES), F32)
        dq_cols = []
        for h in range(B_HEADS):
            _, xh, r = f["qb"][h]
            dx, dg = _col_bwd(dqb_ref[:, LANES * h:LANES * (h + 1)] * QB_SCALE, xh, r, _row_sum, float(B_QK_DIM),
                              gqb, cb, sb, 8, sel8)
            dq_cols.append(dx)
            dgqb = dgqb + dg
        dqr_b = jnp.concatenate(dq_cols, axis=1).astype(BF16)
        dwuq_ref[...] += _dot_tn(f["cqb"], dqr_b)
        dcq_raw, dgcq = _rms_bwd(_dot_nt(dqr_b, wuq), f["cqh"], f["rcq"], gcq)
        pieces.append(dcq_raw)

        dgkb = jnp.zeros((1, LANES), F32)
        dk_cols = []
        dkr = zero
        for h in range(B_HEADS):
            _, xh, r = f["kb"][h]
            dx, dg = _col_bwd(dkb_v[:, LANES * h:LANES * (h + 1)], xh, r, _row_sum, float(B_QK_DIM),
                              gkb, cb, sb, 8, sel8)
            dk_cols.append(dx)
            dkr = dkr + dx
            dgkb = dgkb + dg
        dkr_b = jnp.concatenate(dk_cols, axis=1).astype(BF16)
        dvb_b = dvb_v.astype(BF16)
        dwuk_ref[...] += _dot_tn(f["ckvb"], dkr_b)
        dwuv_ref[...] += _dot_tn(f["ckvb"], dvb_b)
        dckv = _dot_nt(dkr_b, wuk) + _dot_nt(dvb_b, wuv)
        dckv_raw, dgckv = _rms_bwd(dckv, f["ckvh"], f["rckv"], gckv)
        pieces.append(dckv_raw)
        pieces.append(jnp.where((lane >= B_NOPE_DIM) & (lane < B_QK_DIM), dkr, zero))
        pieces += [zero, zero, zero]
        pieces.append(dgb_ref[...])

        dproj_b = jnp.concatenate(pieces, axis=1).astype(BF16)
        dproj_ref[...] = dproj_b
        dxn = _dot_nt(dproj_b, w)
        dx, dgin = _rms_bwd(dxn, f["xh0"], f["r0"], gin)
        dx_ref[...] = dx + dh_ref[...]

        for c in range(D_MODEL // LANES):
            small_ref[c:c + 1, :] += dgin[:, LANES * c:LANES * (c + 1)]
        small_ref[8:9, :] += dgq
        small_ref[9:10, :] += dgk
        for c in range(3):
            small_ref[10 + c:11 + c, :] += dgcq[:, LANES * c:LANES * (c + 1)]
        for c in range(2):
            small_ref[13 + c:14 + c, :] += dgckv[:, LANES * c:LANES * (c + 1)]
        small_ref[15:16, :] += dgqb
        small_ref[16:17, :] += dgkb

    def rows(width):
        return pl.BlockSpec((tm, width), lambda i: (i, 0))

    def cols(height):
        return pl.BlockSpec((height, tm), lambda i: (0, i))

    return pl.pallas_call(
        body, name="pre_bwd", grid=(s_len // tm,),
        in_specs=[rows(D_MODEL), rows(R_WIDTH), rows(D_MODEL), rows(512), cols(256), cols(256), rows(512), rows(512),
                  cols(512), cols(512), rows(512)] + [_full(t.shape) for t in tabs] + _pre_const_specs(consts),
        out_specs=[rows(D_MODEL), rows(N_WIDE), _full((B_Q_RANK, 512)), _full((B_KV_RANK, 512)),
                   _full((B_KV_RANK, 512)), _full((R_SMALL, LANES))],
        out_shape=[jax.ShapeDtypeStruct((s_len, D_MODEL), F32), jax.ShapeDtypeStruct((s_len, N_WIDE), BF16),
                   jax.ShapeDtypeStruct((B_Q_RANK, 512), F32),
                   jax.ShapeDtypeStruct((B_KV_RANK, 512), F32), jax.ShapeDtypeStruct((B_KV_RANK, 512), F32),
                   jax.ShapeDtypeStruct((R_SMALL, LANES), F32)],
        compiler_params=_params(("arbitrary",)),
    )(x, raw, dh, dqa, dka, dva, dga, dqb, dkb, dvb, dgb, *tabs, *[consts[n] for n in _PRE_IN_NAMES])


def _dw_in_call(xnb, dproj_b, tt, tn):
    s_len = xnb.shape[0]

    def body(a_ref, b_ref, o_ref):
        @pl.when(pl.program_id(1) == 0)
        def _():
            o_ref[...] = jnp.zeros(o_ref.shape, F32)

        o_ref[...] += _dot_tn(a_ref[...], b_ref[...])

    return pl.pallas_call(
        body, name="dw_in", grid=(N_WIDE // tn, s_len // tt),
        in_specs=[pl.BlockSpec((tt, D_MODEL), lambda n, t: (t, 0)), pl.BlockSpec((tt, tn), lambda n, t: (t, n))],
        out_specs=pl.BlockSpec((D_MODEL, tn), lambda n, t: (0, n)),
        out_shape=jax.ShapeDtypeStruct((D_MODEL, N_WIDE), F32),
        compiler_params=_params(("arbitrary", "arbitrary")),
    )(xnb, dproj_b)


def _mesh_pos():
    return lax.axis_index("x"), lax.axis_index("y"), lax.axis_index("c")


def _flip(v, bit):
    return 1 - v if bit else v


def _peer(pos, k):
    x, y, c = pos
    return _flip(x, (k >> 2) & 1), _flip(y, (k >> 1) & 1), _flip(c, k & 1)


def _logical(p):
    return 4 * p[0] + 2 * p[1] + p[2]


def _gather_weights_call(shard):
    m_per = shard.shape[0]

    def body(x_ref, out_ref, xb_ref, send_sems, recv_sems, local_sem):
        x, y, c = _mesh_pos()
        me, sibling = (x, y, c), (x, y, 1 - c)
        chips = [(1 - x, y), (x, 1 - y), (1 - x, 1 - y)]
        xb_ref[...] = x_ref[...].astype(BF16)

        def rows(p):
            return out_ref.at[pl.ds(pl.multiple_of(_logical(p) * m_per, 16), m_per), :]

        def copy(k, block, to, src=None):
            return pltpu.make_async_remote_copy(
                src_ref=rows(block) if src is None else src, dst_ref=rows(block),
                send_sem=send_sems.at[k], recv_sem=recv_sems.at[k],
                device_id=to, device_id_type=pl.DeviceIdType.MESH)

        mine = pltpu.make_async_copy(xb_ref, rows(me), local_sem)
        mine.start()
        first = [copy(0, me, sibling, src=xb_ref)]
        first += [copy(1 + j, me, (*chip, c), src=xb_ref) for j, chip in enumerate(chips)]
        for cp in first:
            cp.start()
        passed = [copy(4 + j, (*chip, c), sibling) for j, chip in enumerate(chips)]
        for j, chip in enumerate(chips):
            copy(1 + j, (*chip, c), me).wait_recv()
            passed[j].start()
        copy(0, sibling, me).wait_recv()
        for j, chip in enumerate(chips):
            copy(4 + j, (*chip, 1 - c), me).wait_recv()
        for cp in first + passed:
            cp.wait_send()
        mine.wait()

    return pl.pallas_call(
        body, name="gather_weights",
        out_shape=jax.ShapeDtypeStruct((N_DEV * m_per, LANES), BF16),
        in_specs=[pl.BlockSpec(memory_space=pltpu.VMEM)],
        out_specs=pl.BlockSpec(memory_space=pltpu.VMEM),
        scratch_shapes=[pltpu.VMEM((m_per, LANES), BF16), pltpu.SemaphoreType.DMA((7,)),
                        pltpu.SemaphoreType.DMA((7,)), pltpu.SemaphoreType.DMA],
        compiler_params=pltpu.CompilerParams(vmem_limit_bytes=VMEM_LIMIT),
    )(shard)


def _adamw(w, g, m, v):
    m = ADAM_B1 * m + (1.0 - ADAM_B1) * g
    v = ADAM_B2 * v + (1.0 - ADAM_B2) * (g * g)
    m_hat = m / (1.0 - ADAM_B1 ** ADAM_STEP)
    v_hat = v / (1.0 - ADAM_B2 ** ADAM_STEP)
    delta = -ADAM_LR * (m_hat / (jnp.sqrt(v_hat) + ADAM_EPS) + ADAM_WD * w)
    return delta, m, v


def _reduce_adamw_call(parts, small, w_pk, m_pk, v_pk, w_s, m_s, v_s):
    chunk = 16
    n_chunks = R_PACK // chunk

    def body(parts_ref, small_ref, w_ref, m_ref, v_ref, ws_ref, ms_ref, vs_ref,
             g_ref, d_ref, nm_ref, nv_ref, gs_ref, ds_ref, nms_ref, nvs_ref,
             recv_ref, recv_s_ref, send_sems, recv_sems, send_s_sems, recv_s_sems, local_sem):
        pos = _mesh_pos()
        me = _logical(pos)

        def big(k):
            peer = _peer(pos, k)
            return pltpu.make_async_remote_copy(
                src_ref=parts_ref.at[_logical(peer)], dst_ref=recv_ref.at[k],
                send_sem=send_sems.at[k], recv_sem=recv_sems.at[k],
                device_id=peer, device_id_type=pl.DeviceIdType.MESH)

        def tiny(k):
            return pltpu.make_async_remote_copy(
                src_ref=small_ref, dst_ref=recv_s_ref.at[k],
                send_sem=send_s_sems.at[k], recv_sem=recv_s_sems.at[k],
                device_id=_peer(pos, k), device_id_type=pl.DeviceIdType.MESH)

        own = pltpu.make_async_copy(parts_ref.at[me], recv_ref.at[0], local_sem)
        own.start()
        for k in range(1, N_DEV):
            tiny(k).start()
        for k in range(1, N_DEV):
            big(k).start()
        recv_s_ref[0] = small_ref[...]
        for k in range(1, N_DEV):
            tiny(k).wait_recv()
        acc = recv_s_ref[me]
        for a in range(1, N_DEV):
            acc = acc + recv_s_ref[lax.bitwise_xor(me, a)]
        row = lax.broadcasted_iota(jnp.int32, (R_SMALL, LANES), 0)
        gs = jnp.where(row == 8, acc + pltpu.roll(acc, 64, 1), acc)
        gs = jnp.where(row == ROW_LOSS, jnp.sum(acc, axis=1, keepdims=True) * (0.5 / D_MODEL), gs)
        gs_ref[...] = gs
        ds, nms, nvs = _adamw(ws_ref[...], gs, ms_ref[...], vs_ref[...])
        ds_ref[...] = ds
        nms_ref[...] = nms
        nvs_ref[...] = nvs

        own.wait()
        for k in range(1, N_DEV):
            big(k).wait_recv()

        def step(t, carry):
            rows = pl.ds(pl.multiple_of(t * chunk, chunk), chunk)
            g = recv_ref[0, rows, :]
            for k in range(1, N_DEV):
                g = g + recv_ref[k, rows, :]
            d, nm, nv = _adamw(w_ref[rows, :], g, m_ref[rows, :], v_ref[rows, :])
            g_ref[rows, :] = g
            d_ref[rows, :] = d
            nm_ref[rows, :] = nm
            nv_ref[rows, :] = nv
            return carry

        lax.fori_loop(0, n_chunks, step, 0)
        for k in range(1, N_DEV):
            tiny(k).wait_send()
            big(k).wait_send()

    vm = pl.BlockSpec(memory_space=pltpu.VMEM)
    big_shape = jax.ShapeDtypeStruct((R_PACK, LANES), F32)
    small_shape = jax.ShapeDtypeStruct((R_SMALL, LANES), F32)
    return pl.pallas_call(
        body, name="reduce_adamw",
        in_specs=[pl.BlockSpec(memory_space=pl.ANY)] + [vm] * 7,
        out_specs=[vm] * 8,
        out_shape=[big_shape] * 4 + [small_shape] * 4,
        scratch_shapes=[pltpu.VMEM((N_DEV, R_PACK, LANES), F32), pltpu.VMEM((N_DEV, R_SMALL, LANES), F32),
                        pltpu.SemaphoreType.DMA((N_DEV,)), pltpu.SemaphoreType.DMA((N_DEV,)),
                        pltpu.SemaphoreType.DMA((N_DEV,)), pltpu.SemaphoreType.DMA((N_DEV,)),
                        pltpu.SemaphoreType.DMA],
        compiler_params=pltpu.CompilerParams(vmem_limit_bytes=VMEM_LIMIT),
    )(parts, small, w_pk, m_pk, v_pk, w_s, m_s, v_s)


def _pack_shard(w_in, w_uq, w_ukv, w_out):
    return jnp.concatenate([a.reshape(-1, LANES) for a in (w_in, w_uq, w_ukv, w_out)], axis=0)


def _unpack_shard(p):
    w_in = p[0:R_WIN].reshape(1, D_MODEL, N_IN // N_DEV)
    w_uq = p[R_WIN:R_WIN + R_WUQ].reshape(1, B_Q_RANK // N_DEV, 384)
    w_ukv = p[R_WIN + R_WUQ:R_WIN + R_WUQ + R_WUKV].reshape(1, B_KV_RANK, 768 // N_DEV)
    w_out = p[R_WIN + R_WUQ + R_WUKV:].reshape(1, D_MODEL // N_DEV, D_MODEL)
    return w_in, w_uq, w_ukv, w_out


def _pack_small(norm_in, a_q, a_k, b_cq, b_ckv, b_q, b_k):
    def row(v):
        return jnp.pad(v.reshape(1, -1), ((0, 0), (0, LANES - v.size)))
    rows = [norm_in.reshape(8, LANES), row(a_q), row(a_k), b_cq.reshape(3, LANES), b_ckv.reshape(2, LANES),
            row(b_q), row(b_k), jnp.zeros((R_SMALL - 17, LANES), F32)]
    return jnp.concatenate(rows, axis=0)


def _unpack_small(s):
    return (s[0:8].reshape(1, D_MODEL), s[8:9, :64], s[9:10, :64], s[10:13].reshape(1, B_Q_RANK),
            s[13:15].reshape(1, B_KV_RANK), s[15:16, :B_QK_DIM], s[16:17, :B_QK_DIM])


def _full_weights(gathered):
    g = gathered
    w_in = g[:, 0:R_WIN].reshape(N_DEV, D_MODEL, N_IN // N_DEV).transpose(1, 0, 2).reshape(D_MODEL, N_IN)
    w_uq = g[:, R_WIN:R_WIN + R_WUQ].reshape(B_Q_RANK, 384)
    w_ukv = g[:, R_WIN + R_WUQ:R_WIN + R_WUQ + R_WUKV].reshape(N_DEV, B_KV_RANK, 768 // N_DEV)
    w_ukv = w_ukv.transpose(1, 0, 2).reshape(B_KV_RANK, 768)
    w_out = g[:, R_WIN + R_WUQ + R_WUKV:].reshape(D_MODEL, D_MODEL)
    k0, k1 = w_in[:, 512:576], w_in[:, 576:640]
    v0, v1 = w_in[:, 640:704], w_in[:, 704:768]
    kr = w_in[:, 1920:1952]
    z64 = jnp.zeros((D_MODEL, 64), BF16)
    z32 = jnp.zeros((D_MODEL, 32), BF16)
    kr_blk = jnp.concatenate([z64, kr, z32], axis=1)
    w_wide = jnp.concatenate([w_in[:, 0:512], k0, k0, k1, k1, v0, v0, v1, v1, w_in[:, 768:1280], w_in[:, 1280:1664],
                              w_in[:, 1664:1920], kr_blk, kr_blk, kr_blk, kr_blk, w_in[:, 1952:2464]], axis=1)
    wuq = jnp.pad(w_uq.reshape(B_Q_RANK, B_HEADS, B_QK_DIM), ((0, 0), (0, 0), (0, LANES - B_QK_DIM)))
    wuq = wuq.reshape(B_Q_RANK, 512)
    ukv = w_ukv.reshape(B_KV_RANK, B_HEADS, B_NOPE_DIM + B_V_DIM)
    wuk = jnp.pad(ukv[:, :, :B_NOPE_DIM], ((0, 0), (0, 0), (0, LANES - B_NOPE_DIM))).reshape(B_KV_RANK, 512)
    wuv = ukv[:, :, B_NOPE_DIM:].reshape(B_KV_RANK, 512)
    return w_wide, wuq, wuk, wuv, w_out


def _narrow_grads(dw_wide, dwuq, dwuk, dwuv):
    dw_in = jnp.concatenate([
        dw_wide[:, 0:512], dw_wide[:, 512:576], dw_wide[:, 640:704], dw_wide[:, 768:832], dw_wide[:, 896:960],
        dw_wide[:, 1024:1536], dw_wide[:, 1536:1920], dw_wide[:, 1920:2176],
        dw_wide[:, O_KR + 64:O_KR + 96], dw_wide[:, 2688:3200]], axis=1)
    dw_uq = dwuq.reshape(B_Q_RANK, B_HEADS, LANES)[:, :, :B_QK_DIM].reshape(B_Q_RANK, 384)
    dk = dwuk.reshape(B_KV_RANK, B_HEADS, LANES)[:, :, :B_NOPE_DIM]
    dv = dwuv.reshape(B_KV_RANK, B_HEADS, B_V_DIM)
    dw_ukv = jnp.concatenate([dk, dv], axis=2).reshape(B_KV_RANK, 768)
    return dw_in, dw_uq, dw_ukv


def _rope_tables(s_len):
    row = jnp.arange(s_len // GRID_W, dtype=F32)
    col = jnp.arange(GRID_W, dtype=F32)

    def parts(dim):
        half = dim // 2
        inv = 1.0 / (ROPE_THETA ** (jnp.arange(0, half, 2, dtype=F32) / half))
        ar, ac = row[:, None] * inv[None, :], col[:, None] * inv[None, :]
        zr, zc = jnp.zeros_like(ar), jnp.zeros_like(ac)
        cos_c = jnp.concatenate([zc, zc, jnp.cos(ac), jnp.cos(ac)], axis=1)
        cos_r = jnp.concatenate([jnp.cos(ar), jnp.cos(ar), zr, zr], axis=1)
        sin_c = jnp.concatenate([zc, zc, -jnp.sin(ac), jnp.sin(ac)], axis=1)
        sin_r = jnp.concatenate([-jnp.sin(ar), jnp.sin(ar), zr, zr], axis=1)
        return cos_c, cos_r, sin_c, sin_r

    tabs = [jnp.tile(t, (1, 2)) for t in parts(A_HEAD_DIM)]
    for n, t in enumerate(parts(B_ROPE_DIM)):
        lead = jnp.full((t.shape[0], B_NOPE_DIM), 1.0 if n == 0 else 0.0, F32)
        tail = jnp.full((t.shape[0], LANES - B_QK_DIM), 1.0 if n == 0 else 0.0, F32)
        tabs.append(jnp.concatenate([lead, t, tail], axis=1))
    return tuple(tabs)


def _lane_major(a, step):
    return a[:, ::step].T[:, None, :]


def kernel(x, norm_in, w_in, a_q_norm, a_k_norm, b_cq_norm, b_ckv_norm, w_uq, w_ukv, b_q_norm, b_k_norm, w_out, loss_target, m_norm_in, m_w_in, m_a_q_norm, m_a_k_norm, m_b_cq_norm, m_b_ckv_norm, m_w_uq, m_w_ukv, m_b_q_norm, m_b_k_norm, m_w_out, v_norm_in, v_w_in, v_a_q_norm, v_a_k_norm, v_b_cq_norm, v_b_ckv_norm, v_w_uq, v_w_ukv, v_b_q_norm, v_b_k_norm, v_w_out):
    s_len = x.shape[1]
    tm = min(256, s_len)
    tq, tk = min(512, s_len), min(2048, s_len)
    ftq, ftk = min(256, s_len), min(1024, s_len)
    x2 = x.reshape(s_len, D_MODEL)
    t2 = loss_target.reshape(s_len, D_MODEL)

    w_pk = _pack_shard(w_in[0], w_uq[0], w_ukv[0], w_out[0])
    gathered = _gather_weights_call(w_pk).reshape(N_DEV, R_PACK, LANES)
    w_wide, wuq, wuk, wuv, wout = _full_weights(gathered)

    def dup(v, pad_to=None):
        v = v.reshape(1, -1)
        if pad_to is None:
            return jnp.concatenate([v, v], axis=1)
        return jnp.pad(v, ((0, 0), (0, pad_to - v.shape[1])))

    g64 = jnp.asarray(np.kron(np.eye(2), np.ones((64, 64))), dtype=BF16)
    consts = dict(gin=norm_in, w=w_wide, wuq=wuq, wuk=wuk, wuv=wuv, gq=dup(a_q_norm), gk=dup(a_k_norm),
                  gcq=b_cq_norm, gckv=b_ckv_norm, gqb=dup(b_q_norm, LANES), gkb=dup(b_k_norm, LANES), g64=g64)
    tabs = _rope_tables(s_len)

    qa, ka, va, va_t, ga, qb, kb, vb, vb_t, gb, raw, xnb = _pre_fwd_call(x2, consts, tabs, tm)
    oa, lse_a_t = _attn_fwd_t_call(qa, ka, va_t, groups=A_KV_HEADS, sub=4, masked=True, scale=None, tq=ftq, tk=ftk,
                                   name="attn_fwd_a")
    ob, lse_b_t = _attn_fwd_t_call(qb, kb, vb_t, groups=1, sub=B_HEADS, masked=False, scale=None, tq=ftq,
                                   tk=ftk, name="attn_fwd_b")
    dh, doa, dob, dga, dgb, dw_out, loss_row = _out_call(x2, t2, oa, ob, ga, gb, wout, min(512, s_len))

    dqa, dka_t, dva_t = _attn_bwd_q_call(qa, ka, va, doa, oa, lse_a_t, groups=A_KV_HEADS, sub=4, masked=True,
                                         tq=tq, tk=tk, ck=min(512, s_len), name="attn_bwd_a")
    dqb, dkb_t, dvb_t = _attn_bwd_q_call(qb, kb, vb, dob, ob, lse_b_t, groups=2, sub=2, masked=False,
                                         tq=tq, tk=tk, ck=min(256, s_len), name="attn_bwd_b")
    dx, dproj_b, dwuq, dwuk, dwuv, small = _pre_bwd_call(
        x2, raw, dh, dqa, dka_t, dva_t, dga, dqb, dkb_t, dvb_t, dgb, consts, tabs, tm)
    dw_wide = _dw_in_call(xnb, dproj_b, min(512, s_len), 640)

    dw_in, dw_uq, dw_ukv = _narrow_grads(dw_wide, dwuq, dwuk, dwuv)
    ci, cq, ckv, co = N_IN // N_DEV, B_Q_RANK // N_DEV, 768 // N_DEV, D_MODEL // N_DEV
    parts = jnp.stack([
        _pack_shard(dw_in[:, ci * d:ci * (d + 1)], dw_uq[cq * d:cq * (d + 1)], dw_ukv[:, ckv * d:ckv * (d + 1)],
                    dw_out[co * d:co * (d + 1)]) for d in range(N_DEV)])
    small = jnp.concatenate([small[:ROW_LOSS], loss_row, small[ROW_LOSS + 1:]], axis=0)

    m_pk = _pack_shard(m_w_in[0], m_w_uq[0], m_w_ukv[0], m_w_out[0])
    v_pk = _pack_shard(v_w_in[0], v_w_uq[0], v_w_ukv[0], v_w_out[0])
    w_s = _pack_small(norm_in, a_q_norm, a_k_norm, b_cq_norm, b_ckv_norm, b_q_norm, b_k_norm)
    m_s = _pack_small(m_norm_in, m_a_q_norm, m_a_k_norm, m_b_cq_norm, m_b_ckv_norm, m_b_q_norm, m_b_k_norm)
    v_s = _pack_small(v_norm_in, v_a_q_norm, v_a_k_norm, v_b_cq_norm, v_b_ckv_norm, v_b_q_norm, v_b_k_norm)
    g_pk, d_pk, nm_pk, nv_pk, g_s, d_s, nm_s, nv_s = _reduce_adamw_call(parts, small, w_pk, m_pk, v_pk, w_s, m_s, v_s)

    def leaves(pk, sm):
        wi, uq, ukv, wo = _unpack_shard(pk)
        n_in, aq, ak, bcq, bckv, bq, bk = _unpack_small(sm)
        return [n_in, wi, aq, ak, bcq, bckv, uq, ukv, bq, bk, wo]

    loss = g_s[ROW_LOSS, 0]
    grad_x = dx.reshape(1, s_len, D_MODEL)
    return (loss, grad_x, *leaves(g_pk, g_s), *leaves(d_pk, d_s), *leaves(nm_pk, nm_s), *leaves(nv_pk, nv_s))
```

```python
import functools

import numpy as np
import jax
import jax.numpy as jnp
from jax import lax
from jax.experimental import pallas as pl
from jax.experimental.pallas import tpu as pltpu

F32 = jnp.float32
BF16 = jnp.bfloat16

D_MODEL = 1024
GRID_W = 64
ROPE_THETA = 10000.0
EPS = 1e-6
A_HEAD_DIM = 64
A_HEADS = 8
A_KV_HEADS = 2
B_HEADS = 4
B_NOPE_DIM = 64
B_ROPE_DIM = 32
B_QK_DIM = 96
B_V_DIM = 128
B_Q_RANK = 384
B_KV_RANK = 256
N_IN = 2464
N_DEV = 8

ADAM_LR = 0.001
ADAM_B1 = 0.9
ADAM_B2 = 0.999
ADAM_EPS = 1e-08
ADAM_WD = 0.01
ADAM_STEP = 10

QA_SCALE = 0.125
QB_SCALE = 1.0 / float(np.sqrt(B_QK_DIM))

LANES = 128
O_QA, O_KA, O_VA, O_GA, O_CQ, O_CKV, O_KR, O_GB, N_WIDE = 0, 512, 768, 1024, 1536, 1920, 2176, 2688, 3200
R_QA, R_KA, R_CQ, R_CKV, R_KR, R_WIDTH = 0, 512, 768, 1152, 1408, 1920

R_WIN = D_MODEL * (N_IN // N_DEV) // LANES
R_WUQ = (B_Q_RANK // N_DEV) * 384 // LANES
R_WUKV = B_KV_RANK * (768 // N_DEV) // LANES
R_WOUT = (D_MODEL // N_DEV) * D_MODEL // LANES
R_PACK = R_WIN + R_WUQ + R_WUKV + R_WOUT
R_SMALL = 24
ROW_LOSS = 17

VMEM_LIMIT = 56 * 1024 * 1024

NT = (((1,), (1,)), ((), ()))
TN = (((0,), (0,)), ((), ()))


def _dot(a, b):
    return jnp.dot(a, b, preferred_element_type=F32)


def _dot_nt(a, b):
    return lax.dot_general(a, b, NT, preferred_element_type=F32)


def _dot_tn(a, b):
    return lax.dot_general(a, b, TN, preferred_element_type=F32)


def _params(sem=None):
    return pltpu.CompilerParams(dimension_semantics=sem, vmem_limit_bytes=VMEM_LIMIT)


def _full(shape):
    nd = len(shape)
    return pl.BlockSpec(shape, lambda *_: (0,) * nd)


def _swap_sel(rows, shift):
    lane = lax.broadcasted_iota(jnp.int32, (rows, LANES), 1)
    return pltpu.roll(lane, shift, 1) == (lane ^ shift)


def _swap(x, shift, sel):
    return jnp.where(sel, pltpu.roll(x, shift, 1), pltpu.roll(x, LANES - shift, 1))


def _group_sum64(x, g64):
    hi = x.astype(BF16)
    lo = (x - hi.astype(F32)).astype(BF16)
    return _dot(hi, g64) + _dot(lo, g64)


def _row_sum(x):
    return jnp.sum(x, axis=-1, keepdims=True)


def _col_fwd(xs, msum, denom, gain, cos, sin, shift, sel):
    r = lax.rsqrt(msum(xs * xs) * (1.0 / denom) + EPS)
    xh = xs * r
    n = xh * gain
    return n * cos + _swap(n, shift, sel) * sin, xh, r


def _col_bwd(d_out, xh, r, msum, denom, gain, cos, sin, shift, sel):
    dn = d_out * cos + _swap(d_out * sin, shift, sel)
    dgain = jnp.sum(dn * xh, axis=0, keepdims=True)
    dxh = dn * gain
    dx = r * (dxh - xh * (msum(dxh * xh) * (1.0 / denom)))
    return dx, dgain


def _rms_fwd(x, gain):
    r = lax.rsqrt(jnp.mean(x * x, axis=-1, keepdims=True) + EPS)
    xh = x * r
    return xh * gain, xh, r


def _rms_bwd(dy, xh, r, gain):
    dgain = jnp.sum(dy * xh, axis=0, keepdims=True)
    dxh = dy * gain
    dx = r * (dxh - xh * jnp.mean(dxh * xh, axis=-1, keepdims=True))
    return dx, dgain


def _pre_forward(x, gin, w, wuq, wuk, wuv, gq, gk, gcq, gckv, gqb, gkb, ca, sa, cb, sb, g64, tm, raw=None):
    sel16 = _swap_sel(tm, 16)
    sel8 = _swap_sel(tm, 8)
    xn, xh0, r0 = _rms_fwd(x, gin)
    xnb = xn.astype(BF16)
    proj = None
    if raw is None:
        proj = _dot(xnb, w)
        raw = jnp.concatenate([proj[:, O_QA:O_QA + 512], proj[:, O_KA:O_KA + 256], proj[:, O_CQ:O_CQ + B_Q_RANK],
                               proj[:, O_CKV:O_CKV + B_KV_RANK], proj[:, O_KR:O_KR + 512]], axis=1)
    gs64 = functools.partial(_group_sum64, g64=g64)
    qa = [_col_fwd(raw[:, R_QA + LANES * s:R_QA + LANES * (s + 1)], gs64, 64.0, gq, ca, sa, 16, sel16)
          for s in range(4)]
    ka = [_col_fwd(raw[:, R_KA + LANES * s:R_KA + LANES * (s + 1)], _row_sum, 128.0, gk, ca, sa, 16, sel16)
          for s in range(2)]
    cq, cqh, rcq = _rms_fwd(raw[:, R_CQ:R_CQ + B_Q_RANK], gcq)
    cqb = cq.astype(BF16)
    qb_raw = _dot(cqb, wuq)
    qb = [_col_fwd(qb_raw[:, LANES * h:LANES * (h + 1)], _row_sum, float(B_QK_DIM), gqb, cb, sb, 8, sel8)
          for h in range(B_HEADS)]
    ckv, ckvh, rckv = _rms_fwd(raw[:, R_CKV:R_CKV + B_KV_RANK], gckv)
    ckvb = ckv.astype(BF16)
    kb_raw = _dot(ckvb, wuk) + raw[:, R_KR:R_KR + 512]
    vb = _dot(ckvb, wuv)
    kb = [_col_fwd(kb_raw[:, LANES * h:LANES * (h + 1)], _row_sum, float(B_QK_DIM), gkb, cb, sb, 8, sel8)
          for h in range(B_HEADS)]
    return dict(xh0=xh0, r0=r0, xnb=xnb, proj=proj, raw=raw, qa=qa, ka=ka, cqh=cqh, rcq=rcq, cqb=cqb, qb=qb,
                ckvh=ckvh, rckv=rckv, ckvb=ckvb, kb=kb, vb=vb, sel16=sel16, sel8=sel8, gs64=gs64)


def _rope_tiles(tab_refs, i, tm):
    per_tile = tm // GRID_W
    out = []
    for t in range(4):
        col_ref, row_ref = tab_refs[2 * t], tab_refs[2 * t + 1]
        col = col_ref[...]
        out.append(jnp.concatenate([col + row_ref[pl.ds(i * per_tile + b, 1), :] for b in range(per_tile)], axis=0))
    return out


_PRE_IN_NAMES = ("gin", "w", "wuq", "wuk", "wuv", "gq", "gk", "gcq", "gckv", "gqb", "gkb", "g64")


def _pre_const_specs(consts):
    return [_full(consts[n].shape) for n in _PRE_IN_NAMES]


def _pre_fwd_call(x, consts, tabs, tm):
    s_len = x.shape[0]

    def body(x_ref, *refs):
        tab_refs, refs = refs[:8], refs[8:]
        (gin_ref, w_ref, wuq_ref, wuk_ref, wuv_ref, gq_ref, gk_ref, gcq_ref, gckv_ref, gqb_ref, gkb_ref, g64_ref,
         qa_ref, ka_ref, va_ref, vat_ref, ga_ref, qb_ref, kb_ref, vb_ref, vbt_ref, gb_ref, raw_ref, xnb_ref) = refs
        ca, sa, cb, sb = _rope_tiles(tab_refs, pl.program_id(0), tm)
        f = _pre_forward(x_ref[...], gin_ref[...], w_ref[...], wuq_ref[...], wuk_ref[...], wuv_ref[...],
                         gq_ref[...], gk_ref[...], gcq_ref[...], gckv_ref[...], gqb_ref[...], gkb_ref[...],
                         ca, sa, cb, sb, g64_ref[...], tm)
        proj = f["proj"]
        raw_ref[...] = f["raw"]
        xnb_ref[...] = f["xnb"]
        for s in range(4):
            qa_ref[:, LANES * s:LANES * (s + 1)] = (f["qa"][s][0] * QA_SCALE).astype(BF16)
        for s in range(2):
            ka_ref[:, LANES * s:LANES * (s + 1)] = f["ka"][s][0].astype(BF16)
        va = proj[:, O_VA:O_VA + 256]
        va_ref[...] = va.astype(BF16)
        vat_ref[...] = va.T.astype(BF16)
        ga_ref[...] = proj[:, O_GA:O_GA + 512]
        for h in range(B_HEADS):
            qb_ref[:, LANES * h:LANES * (h + 1)] = (f["qb"][h][0] * QB_SCALE).astype(BF16)
            kb_ref[:, LANES * h:LANES * (h + 1)] = f["kb"][h][0].astype(BF16)
        vb_ref[...] = f["vb"].astype(BF16)
        vbt_ref[...] = f["vb"].T.astype(BF16)
        gb_ref[...] = proj[:, O_GB:O_GB + 512]

    def rows(width):
        return pl.BlockSpec((tm, width), lambda i: (i, 0))

    def cols(height):
        return pl.BlockSpec((height, tm), lambda i: (0, i))

    outs = [((s_len, 512), BF16, rows(512)), ((s_len, 256), BF16, rows(256)), ((s_len, 256), BF16, rows(256)),
            ((256, s_len), BF16, cols(256)), ((s_len, 512), F32, rows(512)), ((s_len, 512), BF16, rows(512)),
            ((s_len, 512), BF16, rows(512)), ((s_len, 512), BF16, rows(512)), ((512, s_len), BF16, cols(512)),
            ((s_len, 512), F32, rows(512)), ((s_len, R_WIDTH), F32, rows(R_WIDTH)),
            ((s_len, D_MODEL), BF16, rows(D_MODEL))]
    return pl.pallas_call(
        body, name="pre_fwd", grid=(s_len // tm,),
        in_specs=[rows(D_MODEL)] + [_full(t.shape) for t in tabs] + _pre_const_specs(consts),
        out_specs=[sp for _, _, sp in outs],
        out_shape=[jax.ShapeDtypeStruct(sh, dt) for sh, dt, _ in outs],
        compiler_params=_params(("arbitrary",)),
    )(x, *tabs, *[consts[n] for n in _PRE_IN_NAMES])


def _head_masks(rows):
    lane = lax.broadcasted_iota(jnp.int32, (rows, LANES), 1)
    return lane < 64, lane >= 64


def _lane_fold(x, op):
    out = x[:, 0:LANES]
    for c in range(1, x.shape[1] // LANES):
        out = op(out, x[:, LANES * c:LANES * (c + 1)])
    return out


def _row_fold(x, op):
    return op(x.reshape(x.shape[0] // 8, 8, x.shape[1]), axis=0)


def _attn_fwd_t_call(q, k, vt, *, groups, sub, masked, scale, tq, tk, name):
    s_len = q.shape[0]
    qw = LANES * (sub // 2 if masked else sub)
    kvw = LANES if masked else LANES * sub
    n_c = s_len // tk
    kv_mode = pl.Buffered(1) if groups == 1 else None

    def body(q_ref, k_ref, vt_ref, o_ref, lse_ref, s_sc):
        keep = _head_masks(tq) if masked else None

        def kv_of(hh):
            return slice(0, LANES) if masked else slice(LANES * hh, LANES * (hh + 1))

        def q_of(hh):
            if not masked:
                return q_ref[:, LANES * hh:LANES * (hh + 1)]
            qp = q_ref[:, LANES * (hh // 2):LANES * (hh // 2 + 1)]
            return jnp.where(keep[hh % 2], qp, jnp.zeros_like(qp))

        def scores(hh, qm, c, mx):
            s_t = _dot_nt(k_ref[tk * c:tk * (c + 1), kv_of(hh)], qm)
            if scale is not None:
                s_t = s_t * scale
            s_sc[hh % 2, c] = s_t
            return jnp.maximum(mx, _row_fold(s_t, jnp.max))

        neg = jnp.full((8, tq), -jnp.inf, F32)
        qm_next = q_of(0)
        mx_next = neg
        for c in range(n_c):
            mx_next = scores(0, qm_next, c, mx_next)
        outs = []
        for hh in range(sub):
            m = jnp.max(mx_next, axis=0, keepdims=True)
            if hh + 1 < sub:
                qm_next = q_of(hh + 1)
                mx_next = neg
            lsum = jnp.zeros((8, tq), F32)
            acc = jnp.zeros((LANES, tq), F32)
            for c in range(n_c):
                if hh + 1 < sub:
                    mx_next = scores(hh + 1, qm_next, c, mx_next)
                p_t = jnp.exp(s_sc[hh % 2, c] - m)
                lsum = lsum + _row_fold(p_t, jnp.sum)
                acc = acc + _dot(vt_ref[kv_of(hh), tk * c:tk * (c + 1)], p_t.astype(BF16))
            l = jnp.sum(lsum, axis=0, keepdims=True)
            outs.append((acc / l).T)
            lse_ref[hh] = m + jnp.log(l)
        if masked:
            for pr in range(sub // 2):
                o_ref[:, LANES * pr:LANES * (pr + 1)] = jnp.where(keep[0], outs[2 * pr], outs[2 * pr + 1])
        else:
            for hh in range(sub):
                o_ref[:, LANES * hh:LANES * (hh + 1)] = outs[hh]

    return pl.pallas_call(
        body, name=name, grid=(groups, s_len // tq),
        in_specs=[pl.BlockSpec((tq, qw), lambda g, i: (i, g)),
                  pl.BlockSpec((s_len, kvw), lambda g, i: (0, g), pipeline_mode=kv_mode),
                  pl.BlockSpec((kvw, s_len), lambda g, i: (g, 0), pipeline_mode=kv_mode)],
        out_specs=[pl.BlockSpec((tq, qw), lambda g, i: (i, g)),
                   pl.BlockSpec((sub, 1, tq), lambda g, i: (g, 0, i))],
        out_shape=[jax.ShapeDtypeStruct((s_len, groups * qw), F32),
                   jax.ShapeDtypeStruct((groups * sub, 1, s_len), F32)],
        scratch_shapes=[pltpu.VMEM((min(sub, 2), n_c, tk, tq), F32)],
        compiler_params=_params(("arbitrary", "arbitrary")),
    )(q, k, vt)


def _attn_fwd_call(q, k, v, *, groups, sub, masked, scale, tq, tk, name):
    s_len = q.shape[0]
    qw = LANES * (sub // 2 if masked else sub)
    n_c = s_len // tk
    log2e = float(np.log2(np.e))
    mul = log2e if scale is None else scale * log2e

    def body(q_ref, k_ref, v_ref, o_ref, lse_ref, s_sc):
        keep = _head_masks(tq) if masked else None
        outs = []
        for hh in range(sub):
            if masked:
                qp = q_ref[:, LANES * (hh // 2):LANES * (hh // 2 + 1)]
                qm = jnp.where(keep[hh % 2], qp, jnp.zeros_like(qp))
                kv_cols = slice(0, LANES)
            else:
                qm = q_ref[:, LANES * hh:LANES * (hh + 1)]
                kv_cols = slice(LANES * hh, LANES * (hh + 1))
            buf = hh % 2

            def scores(c, mx):
                rows = pl.ds(pl.multiple_of(c * tk, tk), tk)
                s = _dot_nt(qm, k_ref[rows, kv_cols]) * mul
                s_sc[buf, c] = s
                return jnp.maximum(mx, _lane_fold(s, jnp.maximum))

            mx = lax.fori_loop(0, n_c, scores, jnp.full((tq, LANES), -jnp.inf, F32), unroll=True)
            m = jnp.max(mx, axis=1, keepdims=True)

            def weights(c, carry):
                lsum, acc = carry
                rows = pl.ds(pl.multiple_of(c * tk, tk), tk)
                p = jnp.exp2(s_sc[buf, c] - m)
                return lsum + _lane_fold(p, jnp.add), acc + _dot(p.astype(BF16), v_ref[rows, kv_cols])

            lsum, acc = lax.fori_loop(0, n_c, weights, (jnp.zeros((tq, LANES), F32), jnp.zeros((tq, LANES), F32)),
                                      unroll=True)
            l = jnp.sum(lsum, axis=1, keepdims=True)
            outs.append(acc / l)
            lse_ref[hh] = jnp.broadcast_to((m + jnp.log2(l)) * (1.0 / log2e), (tq, LANES))
        if masked:
            for pr in range(sub // 2):
                o_ref[:, LANES * pr:LANES * (pr + 1)] = jnp.where(keep[0], outs[2 * pr], outs[2 * pr + 1])
        else:
            for hh in range(sub):
                o_ref[:, LANES * hh:LANES * (hh + 1)] = outs[hh]

    kvw = LANES if masked else LANES * sub
    return pl.pallas_call(
        body, name=name, grid=(groups, s_len // tq),
        in_specs=[pl.BlockSpec((tq, qw), lambda g, i: (i, g)),
                  pl.BlockSpec((s_len, kvw), lambda g, i: (0, g)),
                  pl.BlockSpec((s_len, kvw), lambda g, i: (0, g))],
        out_specs=[pl.BlockSpec((tq, qw), lambda g, i: (i, g)),
                   pl.BlockSpec((sub, tq, LANES), lambda g, i: (g, i, 0))],
        out_shape=[jax.ShapeDtypeStruct((s_len, groups * qw), F32),
                   jax.ShapeDtypeStruct((groups * sub, s_len, LANES), F32)],
        scratch_shapes=[pltpu.VMEM((min(sub, 2), n_c, tq, tk), F32)],
        compiler_params=_params(("arbitrary", "arbitrary")),
    )(q, k, v)


def _attn_bwd_q_call(q, k, v, do, o, lse_t, *, groups, sub, masked, tq, tk, ck, name):
    s_len = q.shape[0]
    qw = LANES * (sub // 2 if masked else sub)
    kvw = LANES if masked else LANES * sub
    n_c = tk // ck

    def body(q_ref, k_ref, v_ref, do_ref, o_ref, lse_ref, dq_ref, dkt_ref, dvt_ref):
        j = pl.program_id(1)
        i = pl.program_id(2)

        @pl.when((j == 0) & (i == 0))
        def _():
            dq_ref[...] = jnp.zeros(dq_ref.shape, F32)

        @pl.when(i == 0)
        def _():
            dkt_ref[...] = jnp.zeros(dkt_ref.shape, F32)
            dvt_ref[...] = jnp.zeros(dvt_ref.shape, F32)

        lkeep = _head_masks(tq) if masked else None
        heads = []
        for hh in range(sub):
            if masked:
                cols = slice(LANES * (hh // 2), LANES * (hh // 2 + 1))
                kv = slice(0, LANES)
                qp, dop = q_ref[:, cols], do_ref[:, cols]
                qm = jnp.where(lkeep[hh % 2], qp, jnp.zeros_like(qp))
                dom = jnp.where(lkeep[hh % 2], dop, jnp.zeros_like(dop))
            else:
                cols = kv = slice(LANES * hh, LANES * (hh + 1))
                qm, dom = q_ref[:, cols], do_ref[:, cols]
            delta = jnp.sum(dom.astype(F32) * o_ref[:, cols], axis=1, keepdims=True)
            lse = jnp.broadcast_to(lse_ref[hh], (LANES, tq)).T[:, 0:1]
            heads.append((cols, kv, qm, dom, qm.T, dom.T, delta, lse))

        def products(hh, c):
            _, kv, qm, dom, _, _, _, _ = heads[hh]
            return _dot_nt(qm, k_ref[ck * c:ck * (c + 1), kv]), _dot_nt(dom, v_ref[ck * c:ck * (c + 1), kv])

        items = [(hh, c) for hh in range(sub) for c in range(n_c)]
        dq_acc = [jnp.zeros((tq, LANES), F32) for _ in range(sub)]
        nxt = products(*items[0])
        for n, (hh, c) in enumerate(items):
            s, dp = nxt
            if n + 1 < len(items):
                nxt = products(*items[n + 1])
            _, kv, qm, dom, qmt, domt, delta, lse = heads[hh]
            p = jnp.exp(s - lse)
            ds = p * (dp - delta)
            p_b = p.astype(BF16)
            ds_b = ds.astype(BF16)
            kcols = slice(ck * c, ck * (c + 1))
            dvt_ref[kv, kcols] += _dot(domt, p_b)
            dkt_ref[kv, kcols] += _dot(qmt, ds_b)
            dq_acc[hh] = dq_acc[hh] + _dot(ds_b, k_ref[kcols, kv])
        rows = pl.ds(pl.multiple_of(i * tq, tq), tq)
        if masked:
            for pr in range(sub // 2):
                dq_ref[rows, LANES * pr:LANES * (pr + 1)] += jnp.where(lkeep[0], dq_acc[2 * pr], dq_acc[2 * pr + 1])
        else:
            for hh in range(sub):
                dq_ref[rows, LANES * hh:LANES * (hh + 1)] += dq_acc[hh]

    return pl.pallas_call(
        body, name=name, grid=(groups, s_len // tk, s_len // tq),
        in_specs=[pl.BlockSpec((tq, qw), lambda g, j, i: (i, g)),
                  pl.BlockSpec((tk, kvw), lambda g, j, i: (j, g)),
                  pl.BlockSpec((tk, kvw), lambda g, j, i: (j, g)),
                  pl.BlockSpec((tq, qw), lambda g, j, i: (i, g)),
                  pl.BlockSpec((tq, qw), lambda g, j, i: (i, g)),
                  pl.BlockSpec((sub, 1, tq), lambda g, j, i: (g, 0, i))],
        out_specs=[pl.BlockSpec((s_len, qw), lambda g, j, i: (0, g)),
                   pl.BlockSpec((kvw, tk), lambda g, j, i: (g, j)),
                   pl.BlockSpec((kvw, tk), lambda g, j, i: (g, j))],
        out_shape=[jax.ShapeDtypeStruct((s_len, groups * qw), F32),
                   jax.ShapeDtypeStruct((groups * kvw, s_len), F32),
                   jax.ShapeDtypeStruct((groups * kvw, s_len), F32)],
        compiler_params=_params(("arbitrary", "arbitrary", "arbitrary")),
    )(q, k, v, do, o, lse_t)


def _attn_bwd_call(q, k, v, do, lse_t, delta_t, *, groups, sub, scale, tq, tk, name):
    s_len = q.shape[0]
    masked = sub > 1
    qw = LANES * (sub // 2 if masked else 1)
    n_k = s_len // tk

    def body(q_ref, k_ref, v_ref, do_ref, lse_ref, dl_ref, dq_ref, dk_ref, dv_ref, dq_sc):
        i = pl.program_id(1)
        j = pl.program_id(2)

        @pl.when((i == 0) & (j == 0))
        def _():
            dk_ref[...] = jnp.zeros(dk_ref.shape, F32)
            dv_ref[...] = jnp.zeros(dv_ref.shape, F32)

        @pl.when(j == 0)
        def _():
            dq_sc[...] = jnp.zeros(dq_sc.shape, F32)

        kk = k_ref[...]
        vv = v_ref[...]
        keep = _head_masks(tq) if masked else None
        dk_t = jnp.zeros((tk, LANES), F32)
        dv_t = jnp.zeros((tk, LANES), F32)
        for hh in range(sub):
            if masked:
                cols = slice(LANES * (hh // 2), LANES * (hh // 2 + 1))
                qp = q_ref[:, cols]
                dop = do_ref[:, cols]
                qm = jnp.where(keep[hh % 2], qp, jnp.zeros_like(qp))
                dom = jnp.where(keep[hh % 2], dop, jnp.zeros_like(dop))
            else:
                cols = slice(0, LANES)
                qm = q_ref[...]
                dom = do_ref[...]
            s_t = _dot_nt(kk, qm)
            if scale is not None:
                s_t = s_t * scale
            p_t = jnp.exp(s_t - lse_ref[hh])
            dp_t = _dot_nt(vv, dom)
            ds_t = p_t * (dp_t - dl_ref[hh])
            if scale is not None:
                ds_t = ds_t * scale
            p_b = p_t.astype(BF16)
            ds_b = ds_t.astype(BF16)
            dv_t = dv_t + _dot(p_b, dom)
            dk_t = dk_t + _dot(ds_b, qm)
            dq_h = _dot_tn(ds_b, kk)
            if masked:
                dq_h = jnp.where(keep[hh % 2], dq_h, jnp.zeros_like(dq_h))
            dq_sc[:, cols] += dq_h
        rows = pl.ds(pl.multiple_of(j * tk, tk), tk)
        dk_ref[rows, :] += dk_t
        dv_ref[rows, :] += dv_t

        @pl.when(j == n_k - 1)
        def _():
            dq_ref[...] = dq_sc[...]

    return pl.pallas_call(
        body, name=name, grid=(groups, s_len // tq, n_k),
        in_specs=[pl.BlockSpec((tq, qw), lambda g, i, j: (i, g)),
                  pl.BlockSpec((tk, LANES), lambda g, i, j: (j, g)),
                  pl.BlockSpec((tk, LANES), lambda g, i, j: (j, g)),
                  pl.BlockSpec((tq, qw), lambda g, i, j: (i, g)),
                  pl.BlockSpec((sub, 1, tq), lambda g, i, j: (g, 0, i)),
                  pl.BlockSpec((sub, 1, tq), lambda g, i, j: (g, 0, i))],
        out_specs=[pl.BlockSpec((tq, qw), lambda g, i, j: (i, g)),
                   pl.BlockSpec((s_len, LANES), lambda g, i, j: (0, g)),
                   pl.BlockSpec((s_len, LANES), lambda g, i, j: (0, g))],
        out_shape=[jax.ShapeDtypeStruct((s_len, groups * qw), F32),
                   jax.ShapeDtypeStruct((s_len, groups * LANES), F32),
                   jax.ShapeDtypeStruct((s_len, groups * LANES), F32)],
        scratch_shapes=[pltpu.VMEM((tq, qw), F32)],
        compiler_params=_params(("arbitrary", "arbitrary", "arbitrary")),
    )(q, k, v, do, lse_t, delta_t)


def _silu_parts(g):
    sig = 1.0 / (1.0 + jnp.exp(-g))
    return g * sig, sig * (1.0 + g * (1.0 - sig))


def _out_call(x, target, oa, ob, ga, gb, wout, tm):
    s_len = x.shape[0]
    n_t = s_len // tm

    def body(x_ref, t_ref, oa_ref, ob_ref, ga_ref, gb_ref, w_ref,
             dh_ref, doa_ref, dob_ref, dga_ref, dgb_ref, dw_ref, loss_ref):
        i = pl.program_id(0)

        @pl.when(i == 0)
        def _():
            dw_ref[...] = jnp.zeros(dw_ref.shape, F32)
            loss_ref[...] = jnp.zeros(loss_ref.shape, F32)

        oa_v, ob_v = oa_ref[...], ob_ref[...]
        silu_a, dsilu_a = _silu_parts(ga_ref[...])
        silu_b, dsilu_b = _silu_parts(gb_ref[...])
        ya = (oa_v * silu_a).astype(BF16)
        yb = (ob_v * silu_b).astype(BF16)
        h = x_ref[...] + _dot(ya, w_ref[0:512, :]) + _dot(yb, w_ref[512:1024, :])
        err = h - t_ref[...]
        part = jnp.sum(err * err, axis=0, keepdims=True)
        acc = part[:, 0:LANES]
        for c in range(1, D_MODEL // LANES):
            acc = acc + part[:, LANES * c:LANES * (c + 1)]
        loss_ref[...] += acc
        dh = err * (1.0 / D_MODEL)
        dh_ref[...] = dh
        dhb = dh.astype(BF16)
        dya = _dot_nt(dhb, w_ref[0:512, :])
        dyb = _dot_nt(dhb, w_ref[512:1024, :])
        doa = dya * silu_a
        dob = dyb * silu_b
        doa_ref[...] = doa.astype(BF16)
        dob_ref[...] = dob.astype(BF16)
        dga_ref[...] = dya * oa_v * dsilu_a
        dgb_ref[...] = dyb * ob_v * dsilu_b
        dw_ref[0:512, :] += _dot_tn(ya, dhb)
        dw_ref[512:1024, :] += _dot_tn(yb, dhb)

    def rows(width):
        return pl.BlockSpec((tm, width), lambda i: (i, 0))

    outs = [(D_MODEL, F32), (512, BF16), (512, BF16), (512, F32), (512, F32)]
    return pl.pallas_call(
        body, name="out_fwd", grid=(n_t,),
        in_specs=[rows(D_MODEL), rows(D_MODEL), rows(512), rows(512), rows(512), rows(512),
                  _full((D_MODEL, D_MODEL))],
        out_specs=[rows(wd) for wd, _ in outs] + [_full((D_MODEL, D_MODEL)), _full((1, LANES))],
        out_shape=[jax.ShapeDtypeStruct((s_len, wd), dt) for wd, dt in outs]
        + [jax.ShapeDtypeStruct((D_MODEL, D_MODEL), F32), jax.ShapeDtypeStruct((1, LANES), F32)],
        compiler_params=_params(("arbitrary",)),
    )(x, target, oa, ob, ga, gb, wout)


def _pre_bwd_call(x, raw, dh, dqa, dka, dva, dga, dqb, dkb, dvb, dgb, consts, tabs, tm):
    s_len = x.shape[0]

    def body(x_ref, raw_ref, dh_ref, dqa_ref, dkat_ref, dvat_ref, dga_ref, dqb_ref, dkbt_ref, dvbt_ref, dgb_ref,
             *refs):
        tab_refs, refs = refs[:8], refs[8:]
        (gin_ref, w_ref, wuq_ref, wuk_ref, wuv_ref, gq_ref, gk_ref, gcq_ref, gckv_ref, gqb_ref, gkb_ref, g64_ref,
         dx_ref, dproj_ref, dwuq_ref, dwuk_ref, dwuv_ref, small_ref) = refs
        i = pl.program_id(0)
        dka_v, dva_v = dkat_ref[...].T, dvat_ref[...].T
        dkb_v, dvb_v = dkbt_ref[...].T, dvbt_ref[...].T

        @pl.when(i == 0)
        def _():
            dwuq_ref[...] = jnp.zeros(dwuq_ref.shape, F32)
            dwuk_ref[...] = jnp.zeros(dwuk_ref.shape, F32)
            dwuv_ref[...] = jnp.zeros(dwuv_ref.shape, F32)
            small_ref[...] = jnp.zeros(small_ref.shape, F32)

        gin, gq, gk = gin_ref[...], gq_ref[...], gk_ref[...]
        gcq, gckv, gqb, gkb = gcq_ref[...], gckv_ref[...], gqb_ref[...], gkb_ref[...]
        ca, sa, cb, sb = _rope_tiles(tab_refs, i, tm)
        w, wuq, wuk, wuv = w_ref[...], wuq_ref[...], wuk_ref[...], wuv_ref[...]
        f = _pre_forward(x_ref[...], gin, w, wuq, wuk, wuv, gq, gk, gcq, gckv, gqb, gkb,
                         ca, sa, cb, sb, g64_ref[...], tm, raw=raw_ref[...])
        sel16, sel8, gs64 = f["sel16"], f["sel8"], f["gs64"]
        lane = lax.broadcasted_iota(jnp.int32, (tm, LANES), 1)
        low = lane < 64
        zero = jnp.zeros((tm, LANES), F32)
        pieces = []

        dgq = jnp.zeros((1, LANES), F32)
        for s in range(4):
            _, xh, r = f["qa"][s]
            d = dqa_ref[:, LANES * s:LANES * (s + 1)] * QA_SCALE
            dx, dg = _col_bwd(d, xh, r, gs64, 64.0, gq, ca, sa, 16, sel16)
            pieces.append(dx)
            dgq = dgq + dg
        dgk = jnp.zeros((1, LANES), F32)
        for s in range(2):
            _, xh, r = f["ka"][s]
            d = dka_v[:, LANES * s:LANES * (s + 1)]
            d = d + pltpu.roll(d, 64, 1)
            dx, dg = _col_bwd(d, xh, r, _row_sum, 128.0, gk, ca, sa, 16, sel16)
            pieces.append(jnp.where(low, dx, zero))
            dgk = dgk + dg
        for s in range(2):
            d = dva_v[:, LANES * s:LANES * (s + 1)]
            d = d + pltpu.roll(d, 64, 1)
            pieces.append(jnp.where(low, d, zero))
        pieces.append(dga_ref[...])

        dgqb = jnp.zeros((1, LANES), F32)
        dq_cols = []
        for h in range(B_HEADS):
            _, xh, r = f["qb"][h]
            dx, dg = _col_bwd(dqb_ref[:, LANES * h:LANES * (h + 1)] * QB_SCALE, xh, r, _row_sum, float(B_QK_DIM),
                              gqb, cb, sb, 8, sel8)
            dq_cols.append(dx)
            dgqb = dgqb + dg
        dqr_b = jnp.concatenate(dq_cols, axis=1).astype(BF16)
        dwuq_ref[...] += _dot_tn(f["cqb"], dqr_b)
        dcq_raw, dgcq = _rms_bwd(_dot_nt(dqr_b, wuq), f["cqh"], f["rcq"], gcq)
        pieces.append(dcq_raw)

        dgkb = jnp.zeros((1, LANES), F32)
        dk_cols = []
        dkr = zero
        for h in range(B_HEADS):
            _, xh, r = f["kb"][h]
            dx, dg = _col_bwd(dkb_v[:, LANES * h:LANES * (h + 1)], xh, r, _row_sum, float(B_QK_DIM),
                              gkb, cb, sb, 8, sel8)
            dk_cols.append(dx)
            dkr = dkr + dx
            dgkb = dgkb + dg
        dkr_b = jnp.concatenate(dk_cols, axis=1).astype(BF16)
        dvb_b = dvb_v.astype(BF16)
        dwuk_ref[...] += _dot_tn(f["ckvb"], dkr_b)
        dwuv_ref[...] += _dot_tn(f["ckvb"], dvb_b)
        dckv = _dot_nt(dkr_b, wuk) + _dot_nt(dvb_b, wuv)
        dckv_raw, dgckv = _rms_bwd(dckv, f["ckvh"], f["rckv"], gckv)
        pieces.append(dckv_raw)
        pieces.append(jnp.where((lane >= B_NOPE_DIM) & (lane < B_QK_DIM), dkr, zero))
        pieces += [zero, zero, zero]
        pieces.append(dgb_ref[...])

        dproj_b = jnp.concatenate(pieces, axis=1).astype(BF16)
        dproj_ref[...] = dproj_b
        dxn = _dot_nt(dproj_b, w)
        dx, dgin = _rms_bwd(dxn, f["xh0"], f["r0"], gin)
        dx_ref[...] = dx + dh_ref[...]

        for c in range(D_MODEL // LANES):
            small_ref[c:c + 1, :] += dgin[:, LANES * c:LANES * (c + 1)]
        small_ref[8:9, :] += dgq
        small_ref[9:10, :] += dgk
        for c in range(3):
            small_ref[10 + c:11 + c, :] += dgcq[:, LANES * c:LANES * (c + 1)]
        for c in range(2):
            small_ref[13 + c:14 + c, :] += dgckv[:, LANES * c:LANES * (c + 1)]
        small_ref[15:16, :] += dgqb
        small_ref[16:17, :] += dgkb

    def rows(width):
        return pl.BlockSpec((tm, width), lambda i: (i, 0))

    def cols(height):
        return pl.BlockSpec((height, tm), lambda i: (0, i))

    return pl.pallas_call(
        body, name="pre_bwd", grid=(s_len // tm,),
        in_specs=[rows(D_MODEL), rows(R_WIDTH), rows(D_MODEL), rows(512), cols(256), cols(256), rows(512), rows(512),
                  cols(512), cols(512), rows(512)] + [_full(t.shape) for t in tabs] + _pre_const_specs(consts),
        out_specs=[rows(D_MODEL), rows(N_WIDE), _full((B_Q_RANK, 512)), _full((B_KV_RANK, 512)),
                   _full((B_KV_RANK, 512)), _full((R_SMALL, LANES))],
        out_shape=[jax.ShapeDtypeStruct((s_len, D_MODEL), F32), jax.ShapeDtypeStruct((s_len, N_WIDE), BF16),
                   jax.ShapeDtypeStruct((B_Q_RANK, 512), F32),
                   jax.ShapeDtypeStruct((B_KV_RANK, 512), F32), jax.ShapeDtypeStruct((B_KV_RANK, 512), F32),
                   jax.ShapeDtypeStruct((R_SMALL, LANES), F32)],
        compiler_params=_params(("arbitrary",)),
    )(x, raw, dh, dqa, dka, dva, dga, dqb, dkb, dvb, dgb, *tabs, *[consts[n] for n in _PRE_IN_NAMES])


def _dw_in_call(xnb, dproj_b, tt, tn):
    s_len = xnb.shape[0]

    def body(a_ref, b_ref, o_ref):
        @pl.when(pl.program_id(1) == 0)
        def _():
            o_ref[...] = jnp.zeros(o_ref.shape, F32)

        o_ref[...] += _dot_tn(a_ref[...], b_ref[...])

    return pl.pallas_call(
        body, name="dw_in", grid=(N_WIDE // tn, s_len // tt),
        in_specs=[pl.BlockSpec((tt, D_MODEL), lambda n, t: (t, 0)), pl.BlockSpec((tt, tn), lambda n, t: (t, n))],
        out_specs=pl.BlockSpec((D_MODEL, tn), lambda n, t: (0, n)),
        out_shape=jax.ShapeDtypeStruct((D_MODEL, N_WIDE), F32),
        compiler_params=_params(("arbitrary", "arbitrary")),
    )(xnb, dproj_b)


def _mesh_pos():
    return lax.axis_index("x"), lax.axis_index("y"), lax.axis_index("c")


def _flip(v, bit):
    return 1 - v if bit else v


def _peer(pos, k):
    x, y, c = pos
    return _flip(x, (k >> 2) & 1), _flip(y, (k >> 1) & 1), _flip(c, k & 1)


def _logical(p):
    return 4 * p[0] + 2 * p[1] + p[2]


def _gather_weights_call(shard):
    m_per = shard.shape[0]

    def body(x_ref, out_ref, xb_ref, send_sems, recv_sems, local_sem):
        x, y, c = _mesh_pos()
        me, sibling = (x, y, c), (x, y, 1 - c)
        chips = [(1 - x, y), (x, 1 - y), (1 - x, 1 - y)]
        xb_ref[...] = x_ref[...].astype(BF16)

        def rows(p):
            return out_ref.at[pl.ds(pl.multiple_of(_logical(p) * m_per, 16), m_per), :]

        def copy(k, block, to, src=None):
            return pltpu.make_async_remote_copy(
                src_ref=rows(block) if src is None else src, dst_ref=rows(block),
                send_sem=send_sems.at[k], recv_sem=recv_sems.at[k],
                device_id=to, device_id_type=pl.DeviceIdType.MESH)

        mine = pltpu.make_async_copy(xb_ref, rows(me), local_sem)
        mine.start()
        first = [copy(0, me, sibling, src=xb_ref)]
        first += [copy(1 + j, me, (*chip, c), src=xb_ref) for j, chip in enumerate(chips)]
        for cp in first:
            cp.start()
        passed = [copy(4 + j, (*chip, c), sibling) for j, chip in enumerate(chips)]
        for j, chip in enumerate(chips):
            copy(1 + j, (*chip, c), me).wait_recv()
            passed[j].start()
        copy(0, sibling, me).wait_recv()
        for j, chip in enumerate(chips):
            copy(4 + j, (*chip, 1 - c), me).wait_recv()
        for cp in first + passed:
            cp.wait_send()
        mine.wait()

    return pl.pallas_call(
        body, name="gather_weights",
        out_shape=jax.ShapeDtypeStruct((N_DEV * m_per, LANES), BF16),
        in_specs=[pl.BlockSpec(memory_space=pltpu.VMEM)],
        out_specs=pl.BlockSpec(memory_space=pltpu.VMEM),
        scratch_shapes=[pltpu.VMEM((m_per, LANES), BF16), pltpu.SemaphoreType.DMA((7,)),
                        pltpu.SemaphoreType.DMA((7,)), pltpu.SemaphoreType.DMA],
        compiler_params=pltpu.CompilerParams(vmem_limit_bytes=VMEM_LIMIT),
    )(shard)


def _adamw(w, g, m, v):
    m = ADAM_B1 * m + (1.0 - ADAM_B1) * g
    v = ADAM_B2 * v + (1.0 - ADAM_B2) * (g * g)
    m_hat = m / (1.0 - ADAM_B1 ** ADAM_STEP)
    v_hat = v / (1.0 - ADAM_B2 ** ADAM_STEP)
    delta = -ADAM_LR * (m_hat / (jnp.sqrt(v_hat) + ADAM_EPS) + ADAM_WD * w)
    return delta, m, v


def _reduce_adamw_call(parts, small, w_pk, m_pk, v_pk, w_s, m_s, v_s):
    chunk = 16
    n_chunks = R_PACK // chunk

    def body(parts_ref, small_ref, w_ref, m_ref, v_ref, ws_ref, ms_ref, vs_ref,
             g_ref, d_ref, nm_ref, nv_ref, gs_ref, ds_ref, nms_ref, nvs_ref,
             recv_ref, recv_s_ref, send_sems, recv_sems, send_s_sems, recv_s_sems, local_sem):
        pos = _mesh_pos()
        me = _logical(pos)

        def big(k):
            peer = _peer(pos, k)
            return pltpu.make_async_remote_copy(
                src_ref=parts_ref.at[_logical(peer)], dst_ref=recv_ref.at[k],
                send_sem=send_sems.at[k], recv_sem=recv_sems.at[k],
                device_id=peer, device_id_type=pl.DeviceIdType.MESH)

        def tiny(k):
            return pltpu.make_async_remote_copy(
                src_ref=small_ref, dst_ref=recv_s_ref.at[k],
                send_sem=send_s_sems.at[k], recv_sem=recv_s_sems.at[k],
                device_id=_peer(pos, k), device_id_type=pl.DeviceIdType.MESH)

        own = pltpu.make_async_copy(parts_ref.at[me], recv_ref.at[0], local_sem)
        own.start()
        for k in range(1, N_DEV):
            tiny(k).start()
        for k in range(1, N_DEV):
            big(k).start()
        recv_s_ref[0] = small_ref[...]
        for k in range(1, N_DEV):
            tiny(k).wait_recv()
        acc = recv_s_ref[me]
        for a in range(1, N_DEV):
            acc = acc + recv_s_ref[lax.bitwise_xor(me, a)]
        row = lax.broadcasted_iota(jnp.int32, (R_SMALL, LANES), 0)
        gs = jnp.where(row == 8, acc + pltpu.roll(acc, 64, 1), acc)
        gs = jnp.where(row == ROW_LOSS, jnp.sum(acc, axis=1, keepdims=True) * (0.5 / D_MODEL), gs)
        gs_ref[...] = gs
        ds, nms, nvs = _adamw(ws_ref[...], gs, ms_ref[...], vs_ref[...])
        ds_ref[...] = ds
        nms_ref[...] = nms
        nvs_ref[...] = nvs

        own.wait()
        for k in range(1, N_DEV):
            big(k).wait_recv()

        def step(t, carry):
            rows = pl.ds(pl.multiple_of(t * chunk, chunk), chunk)
            g = recv_ref[0, rows, :]
            for k in range(1, N_DEV):
                g = g + recv_ref[k, rows, :]
            d, nm, nv = _adamw(w_ref[rows, :], g, m_ref[rows, :], v_ref[rows, :])
            g_ref[rows, :] = g
            d_ref[rows, :] = d
            nm_ref[rows, :] = nm
            nv_ref[rows, :] = nv
            return carry

        lax.fori_loop(0, n_chunks, step, 0)
        for k in range(1, N_DEV):
            tiny(k).wait_send()
            big(k).wait_send()

    vm = pl.BlockSpec(memory_space=pltpu.VMEM)
    big_shape = jax.ShapeDtypeStruct((R_PACK, LANES), F32)
    small_shape = jax.ShapeDtypeStruct((R_SMALL, LANES), F32)
    return pl.pallas_call(
        body, name="reduce_adamw",
        in_specs=[pl.BlockSpec(memory_space=pl.ANY)] + [vm] * 7,
        out_specs=[vm] * 8,
        out_shape=[big_shape] * 4 + [small_shape] * 4,
        scratch_shapes=[pltpu.VMEM((N_DEV, R_PACK, LANES), F32), pltpu.VMEM((N_DEV, R_SMALL, LANES), F32),
                        pltpu.SemaphoreType.DMA((N_DEV,)), pltpu.SemaphoreType.DMA((N_DEV,)),
                        pltpu.SemaphoreType.DMA((N_DEV,)), pltpu.SemaphoreType.DMA((N_DEV,)),
                        pltpu.SemaphoreType.DMA],
        compiler_params=pltpu.CompilerParams(vmem_limit_bytes=VMEM_LIMIT),
    )(parts, small, w_pk, m_pk, v_pk, w_s, m_s, v_s)


W_BLOCKS = ((D_MODEL, N_IN // N_DEV), (B_Q_RANK // N_DEV, 384), (B_KV_RANK, 768 // N_DEV), (D_MODEL // N_DEV, D_MODEL))
N_W = len(W_BLOCKS)


def _gather_blocks_call(blocks):
    def body(*refs):
        x_refs, out_refs, xb_refs = refs[0:N_W], refs[N_W:2 * N_W], refs[2 * N_W:3 * N_W]
        send_sems, recv_sems, local_sems = refs[3 * N_W:]
        x, y, c = _mesh_pos()
        me, sibling = (x, y, c), (x, y, 1 - c)
        chips = [(1 - x, y), (x, 1 - y), (1 - x, 1 - y)]
        for w in range(N_W):
            xb_refs[w][...] = x_refs[w][...].astype(BF16)

        def slot(w, p):
            return out_refs[w].at[_logical(p)]

        def copy(w, k, block, to, src=None):
            return pltpu.make_async_remote_copy(
                src_ref=slot(w, block) if src is None else src, dst_ref=slot(w, block),
                send_sem=send_sems.at[N_W * k + w], recv_sem=recv_sems.at[N_W * k + w],
                device_id=to, device_id_type=pl.DeviceIdType.MESH)

        mine = [pltpu.make_async_copy(xb_refs[w], slot(w, me), local_sems.at[w]) for w in range(N_W)]
        for cp in mine:
            cp.start()
        first = [copy(w, 0, me, sibling, src=xb_refs[w]) for w in range(N_W)]
        first += [copy(w, 1 + j, me, (*chip, c), src=xb_refs[w]) for j, chip in enumerate(chips) for w in range(N_W)]
        for cp in first:
            cp.start()
        passed = []
        for j, chip in enumerate(chips):
            for w in range(N_W):
                copy(w, 1 + j, (*chip, c), me).wait_recv()
                fwd = copy(w, 4 + j, (*chip, c), sibling)
                fwd.start()
                passed.append(fwd)
        for w in range(N_W):
            copy(w, 0, sibling, me).wait_recv()
        for j, chip in enumerate(chips):
            for w in range(N_W):
                copy(w, 4 + j, (*chip, 1 - c), me).wait_recv()
        for cp in first + passed:
            cp.wait_send()
        for cp in mine:
            cp.wait()

    vm = pl.BlockSpec(memory_space=pltpu.VMEM)
    return pl.pallas_call(
        body, name="gather_weights",
        out_shape=[jax.ShapeDtypeStruct((N_DEV,) + shp, BF16) for shp in W_BLOCKS],
        in_specs=[vm] * N_W, out_specs=[vm] * N_W,
        scratch_shapes=[pltpu.VMEM(shp, BF16) for shp in W_BLOCKS]
        + [pltpu.SemaphoreType.DMA((7 * N_W,)), pltpu.SemaphoreType.DMA((7 * N_W,)), pltpu.SemaphoreType.DMA((N_W,))],
        compiler_params=pltpu.CompilerParams(vmem_limit_bytes=VMEM_LIMIT),
    )(*blocks)


def _reduce_blocks_call(parts, small, w_blk, m_blk, v_blk, w_s, m_s, v_s):
    chunks = (32, 48, 64, 16)

    def body(*refs):
        p_refs = refs[0:4]
        small_ref = refs[4]
        w_refs, m_refs, v_refs = refs[5:9], refs[9:13], refs[13:17]
        ws_ref, ms_ref, vs_ref = refs[17:20]
        g_refs, d_refs, nm_refs, nv_refs = refs[20:24], refs[24:28], refs[28:32], refs[32:36]
        gs_ref, ds_ref, nms_ref, nvs_ref = refs[36:40]
        r_refs = refs[40:44]
        recv_s_ref = refs[44]
        send_sems, recv_sems, send_s_sems, recv_s_sems, local_sems = refs[45:50]
        pos = _mesh_pos()
        me = _logical(pos)

        def big(w, k):
            peer = _peer(pos, k)
            return pltpu.make_async_remote_copy(
                src_ref=p_refs[w].at[_logical(peer)], dst_ref=r_refs[w].at[k],
                send_sem=send_sems.at[N_W * k + w], recv_sem=recv_sems.at[N_W * k + w],
                device_id=peer, device_id_type=pl.DeviceIdType.MESH)

        def tiny(k):
            return pltpu.make_async_remote_copy(
                src_ref=small_ref, dst_ref=recv_s_ref.at[k],
                send_sem=send_s_sems.at[k], recv_sem=recv_s_sems.at[k],
                device_id=_peer(pos, k), device_id_type=pl.DeviceIdType.MESH)

        own = [pltpu.make_async_copy(p_refs[w].at[me], r_refs[w].at[0], local_sems.at[w]) for w in range(N_W)]
        for cp in own:
            cp.start()
        for k in range(1, N_DEV):
            tiny(k).start()
        for k in range(1, N_DEV):
            for w in range(N_W):
                big(w, k).start()
        recv_s_ref[0] = small_ref[...]
        for k in range(1, N_DEV):
            tiny(k).wait_recv()
        acc = recv_s_ref[me]
        for a in range(1, N_DEV):
            acc = acc + recv_s_ref[lax.bitwise_xor(me, a)]
        row = lax.broadcasted_iota(jnp.int32, (R_SMALL, LANES), 0)
        gs = jnp.where(row == 8, acc + pltpu.roll(acc, 64, 1), acc)
        gs = jnp.where(row == ROW_LOSS, jnp.sum(acc, axis=1, keepdims=True) * (0.5 / D_MODEL), gs)
        gs_ref[...] = gs
        ds, nms, nvs = _adamw(ws_ref[...], gs, ms_ref[...], vs_ref[...])
        ds_ref[...] = ds
        nms_ref[...] = nms
        nvs_ref[...] = nvs

        for w in (3, 2, 1, 0):
            own[w].wait()
            for k in range(1, N_DEV):
                big(w, k).wait_recv()
            chunk = chunks[w]

            def step(t, carry, w=w, chunk=chunk):
                rows = pl.ds(pl.multiple_of(t * chunk, chunk), chunk)
                g = r_refs[w][0, rows, :].astype(F32)
                for k in range(1, N_DEV):
                    g = g + r_refs[w][k, rows, :].astype(F32)
                d, nm, nv = _adamw(w_refs[w][rows, :], g, m_refs[w][rows, :], v_refs[w][rows, :])
                g_refs[w][rows, :] = g
                d_refs[w][rows, :] = d
                nm_refs[w][rows, :] = nm
                nv_refs[w][rows, :] = nv
                return carry

            lax.fori_loop(0, W_BLOCKS[w][0] // chunk, step, 0)
        for k in range(1, N_DEV):
            tiny(k).wait_send()
            for w in range(N_W):
                big(w, k).wait_send()

    vm = pl.BlockSpec(memory_space=pltpu.VMEM)
    blk = [jax.ShapeDtypeStruct(shp, F32) for shp in W_BLOCKS]
    small_shape = jax.ShapeDtypeStruct((R_SMALL, LANES), F32)
    return pl.pallas_call(
        body, name="reduce_adamw",
        in_specs=[pl.BlockSpec(memory_space=pl.ANY)] * N_W + [vm] * 16,
        out_specs=[vm] * 20,
        out_shape=blk * 4 + [small_shape] * 4,
        scratch_shapes=[pltpu.VMEM((N_DEV,) + shp, BF16) for shp in W_BLOCKS]
        + [pltpu.VMEM((N_DEV, R_SMALL, LANES), F32),
           pltpu.SemaphoreType.DMA((N_DEV * N_W,)), pltpu.SemaphoreType.DMA((N_DEV * N_W,)),
           pltpu.SemaphoreType.DMA((N_DEV,)), pltpu.SemaphoreType.DMA((N_DEV,)), pltpu.SemaphoreType.DMA((N_W,))],
        compiler_params=pltpu.CompilerParams(vmem_limit_bytes=VMEM_LIMIT),
    )(*parts, small, *w_blk, *m_blk, *v_blk, w_s, m_s, v_s)


def _pack_shard(w_in, w_uq, w_ukv, w_out):
    return jnp.concatenate([a.reshape(-1, LANES) for a in (w_in, w_uq, w_ukv, w_out)], axis=0)


def _unpack_shard(p):
    w_in = p[0:R_WIN].reshape(1, D_MODEL, N_IN // N_DEV)
    w_uq = p[R_WIN:R_WIN + R_WUQ].reshape(1, B_Q_RANK // N_DEV, 384)
    w_ukv = p[R_WIN + R_WUQ:R_WIN + R_WUQ + R_WUKV].reshape(1, B_KV_RANK, 768 // N_DEV)
    w_out = p[R_WIN + R_WUQ + R_WUKV:].reshape(1, D_MODEL // N_DEV, D_MODEL)
    return w_in, w_uq, w_ukv, w_out


def _pack_small(norm_in, a_q, a_k, b_cq, b_ckv, b_q, b_k):
    def row(v):
        return jnp.pad(v.reshape(1, -1), ((0, 0), (0, LANES - v.size)))
    rows = [norm_in.reshape(8, LANES), row(a_q), row(a_k), b_cq.reshape(3, LANES), b_ckv.reshape(2, LANES),
            row(b_q), row(b_k), jnp.zeros((R_SMALL - 17, LANES), F32)]
    return jnp.concatenate(rows, axis=0)


def _unpack_small(s):
    return (s[0:8].reshape(1, D_MODEL), s[8:9, :64], s[9:10, :64], s[10:13].reshape(1, B_Q_RANK),
            s[13:15].reshape(1, B_KV_RANK), s[15:16, :B_QK_DIM], s[16:17, :B_QK_DIM])


def _full_weights(gathered):
    g = gathered
    w_in = g[:, 0:R_WIN].reshape(N_DEV, D_MODEL, N_IN // N_DEV).transpose(1, 0, 2).reshape(D_MODEL, N_IN)
    w_uq = g[:, R_WIN:R_WIN + R_WUQ].reshape(B_Q_RANK, 384)
    w_ukv = g[:, R_WIN + R_WUQ:R_WIN + R_WUQ + R_WUKV].reshape(N_DEV, B_KV_RANK, 768 // N_DEV)
    w_ukv = w_ukv.transpose(1, 0, 2).reshape(B_KV_RANK, 768)
    w_out = g[:, R_WIN + R_WUQ + R_WUKV:].reshape(D_MODEL, D_MODEL)
    k0, k1 = w_in[:, 512:576], w_in[:, 576:640]
    v0, v1 = w_in[:, 640:704], w_in[:, 704:768]
    kr = w_in[:, 1920:1952]
    z64 = jnp.zeros((D_MODEL, 64), BF16)
    z32 = jnp.zeros((D_MODEL, 32), BF16)
    kr_blk = jnp.concatenate([z64, kr, z32], axis=1)
    w_wide = jnp.concatenate([w_in[:, 0:512], k0, k0, k1, k1, v0, v0, v1, v1, w_in[:, 768:1280], w_in[:, 1280:1664],
                              w_in[:, 1664:1920], kr_blk, kr_blk, kr_blk, kr_blk, w_in[:, 1952:2464]], axis=1)
    wuq = jnp.pad(w_uq.reshape(B_Q_RANK, B_HEADS, B_QK_DIM), ((0, 0), (0, 0), (0, LANES - B_QK_DIM)))
    wuq = wuq.reshape(B_Q_RANK, 512)
    ukv = w_ukv.reshape(B_KV_RANK, B_HEADS, B_NOPE_DIM + B_V_DIM)
    wuk = jnp.pad(ukv[:, :, :B_NOPE_DIM], ((0, 0), (0, 0), (0, LANES - B_NOPE_DIM))).reshape(B_KV_RANK, 512)
    wuv = ukv[:, :, B_NOPE_DIM:].reshape(B_KV_RANK, 512)
    return w_wide, wuq, wuk, wuv, w_out


def _narrow_grads(dw_wide, dwuq, dwuk, dwuv):
    dw_in = jnp.concatenate([
        dw_wide[:, 0:512], dw_wide[:, 512:576], dw_wide[:, 640:704], dw_wide[:, 768:832], dw_wide[:, 896:960],
        dw_wide[:, 1024:1536], dw_wide[:, 1536:1920], dw_wide[:, 1920:2176],
        dw_wide[:, O_KR + 64:O_KR + 96], dw_wide[:, 2688:3200]], axis=1)
    dw_uq = dwuq.reshape(B_Q_RANK, B_HEADS, LANES)[:, :, :B_QK_DIM].reshape(B_Q_RANK, 384)
    dk = dwuk.reshape(B_KV_RANK, B_HEADS, LANES)[:, :, :B_NOPE_DIM]
    dv = dwuv.reshape(B_KV_RANK, B_HEADS, B_V_DIM)
    dw_ukv = jnp.concatenate([dk, dv], axis=2).reshape(B_KV_RANK, 768)
    return dw_in, dw_uq, dw_ukv


C_IN = N_IN // N_DEV
_RUNS = ((0, 512, O_QA), (512, 576, O_KA), (576, 640, O_KA + 128), (640, 704, O_VA), (704, 768, O_VA + 128),
         (768, 1280, O_GA), (1280, 1664, O_CQ), (1664, 1920, O_CKV), (1920, 1952, O_KR + 64), (1952, 2464, O_GB))


def _in_cols(g_in, lo, hi):
    out = []
    for d in range(N_DEV):
        a, b = max(lo, C_IN * d), min(hi, C_IN * (d + 1))
        if a < b:
            out.append(g_in[d][:, a - C_IN * d:b - C_IN * d])
    return out


def _widen_weights(g_in, g_uq, g_ukv, g_out):
    z64 = jnp.zeros((D_MODEL, 64), BF16)
    z32 = jnp.zeros((D_MODEL, 32), BF16)
    k0, k1 = _in_cols(g_in, 512, 576), _in_cols(g_in, 576, 640)
    v0, v1 = _in_cols(g_in, 640, 704), _in_cols(g_in, 704, 768)
    kr_blk = [z64] + _in_cols(g_in, 1920, 1952) + [z32]
    w_wide = jnp.concatenate(
        _in_cols(g_in, 0, 512) + k0 + k0 + k1 + k1 + v0 + v0 + v1 + v1 + _in_cols(g_in, 768, 1920)
        + kr_blk * B_HEADS + _in_cols(g_in, 1952, 2464), axis=1)
    w_uq = g_uq.reshape(B_Q_RANK, 384)
    wuq = jnp.pad(w_uq.reshape(B_Q_RANK, B_HEADS, B_QK_DIM), ((0, 0), (0, 0), (0, LANES - B_QK_DIM)))
    wuq = wuq.reshape(B_Q_RANK, 512)
    ukv = g_ukv.transpose(1, 0, 2).reshape(B_KV_RANK, B_HEADS, B_NOPE_DIM + B_V_DIM)
    wuk = jnp.pad(ukv[:, :, :B_NOPE_DIM], ((0, 0), (0, 0), (0, LANES - B_NOPE_DIM))).reshape(B_KV_RANK, 512)
    wuv = ukv[:, :, B_NOPE_DIM:].reshape(B_KV_RANK, 512)
    return w_wide, wuq, wuk, wuv, g_out.reshape(D_MODEL, D_MODEL)


def _grad_blocks(dw_wide, dwuq, dwuk, dwuv, dw_out):
    blocks = []
    for d in range(N_DEV):
        pieces = []
        for lo, hi, wide in _RUNS:
            a, b = max(lo, C_IN * d), min(hi, C_IN * (d + 1))
            if a < b:
                pieces.append(dw_wide[:, wide + a - lo:wide + b - lo])
        blocks.append(jnp.concatenate(pieces, axis=1))
    p_in = jnp.stack(blocks).astype(BF16)
    dw_uq = dwuq.reshape(B_Q_RANK, B_HEADS, LANES)[:, :, :B_QK_DIM].reshape(N_DEV, B_Q_RANK // N_DEV, 384)
    dk = dwuk.reshape(B_KV_RANK, B_HEADS, LANES)[:, :, :B_NOPE_DIM]
    dv = dwuv.reshape(B_KV_RANK, B_HEADS, B_V_DIM)
    dw_ukv = jnp.concatenate([dk, dv], axis=2).reshape(B_KV_RANK, N_DEV, 768 // N_DEV).transpose(1, 0, 2)
    return (p_in, dw_uq.astype(BF16), dw_ukv.astype(BF16),
            dw_out.reshape(N_DEV, D_MODEL // N_DEV, D_MODEL).astype(BF16))


def _rope_tables(s_len):
    row = jnp.arange(s_len // GRID_W, dtype=F32)
    col = jnp.arange(GRID_W, dtype=F32)

    def parts(dim):
        half = dim // 2
        inv = 1.0 / (ROPE_THETA ** (jnp.arange(0, half, 2, dtype=F32) / half))
        ar, ac = row[:, None] * inv[None, :], col[:, None] * inv[None, :]
        zr, zc = jnp.zeros_like(ar), jnp.zeros_like(ac)
        cos_c = jnp.concatenate([zc, zc, jnp.cos(ac), jnp.cos(ac)], axis=1)
        cos_r = jnp.concatenate([jnp.cos(ar), jnp.cos(ar), zr, zr], axis=1)
        sin_c = jnp.concatenate([zc, zc, -jnp.sin(ac), jnp.sin(ac)], axis=1)
        sin_r = jnp.concatenate([-jnp.sin(ar), jnp.sin(ar), zr, zr], axis=1)
        return cos_c, cos_r, sin_c, sin_r

    tabs = [jnp.tile(t, (1, 2)) for t in parts(A_HEAD_DIM)]
    for n, t in enumerate(parts(B_ROPE_DIM)):
        lead = jnp.full((t.shape[0], B_NOPE_DIM), 1.0 if n == 0 else 0.0, F32)
        tail = jnp.full((t.shape[0], LANES - B_QK_DIM), 1.0 if n == 0 else 0.0, F32)
        tabs.append(jnp.concatenate([lead, t, tail], axis=1))
    return tuple(tabs)


def _lane_major(a, step):
    return a[:, ::step].T[:, None, :]


def kernel(x, norm_in, w_in, a_q_norm, a_k_norm, b_cq_norm, b_ckv_norm, w_uq, w_ukv, b_q_norm, b_k_norm, w_out, loss_target, m_norm_in, m_w_in, m_a_q_norm, m_a_k_norm, m_b_cq_norm, m_b_ckv_norm, m_w_uq, m_w_ukv, m_b_q_norm, m_b_k_norm, m_w_out, v_norm_in, v_w_in, v_a_q_norm, v_a_k_norm, v_b_cq_norm, v_b_ckv_norm, v_w_uq, v_w_ukv, v_b_q_norm, v_b_k_norm, v_w_out):
    s_len = x.shape[1]
    tm = min(256, s_len)
    tq, tk = min(512, s_len), min(2048, s_len)
    ftq, ftk = min(256, s_len), min(1024, s_len)
    x2 = x.reshape(s_len, D_MODEL)
    t2 = loss_target.reshape(s_len, D_MODEL)

    w_blk = (w_in[0], w_uq[0], w_ukv[0], w_out[0])
    w_wide, wuq, wuk, wuv, wout = _widen_weights(*_gather_blocks_call(w_blk))

    def dup(v, pad_to=None):
        v = v.reshape(1, -1)
        if pad_to is None:
            return jnp.concatenate([v, v], axis=1)
        return jnp.pad(v, ((0, 0), (0, pad_to - v.shape[1])))

    g64 = jnp.asarray(np.kron(np.eye(2), np.ones((64, 64))), dtype=BF16)
    consts = dict(gin=norm_in, w=w_wide, wuq=wuq, wuk=wuk, wuv=wuv, gq=dup(a_q_norm), gk=dup(a_k_norm),
                  gcq=b_cq_norm, gckv=b_ckv_norm, gqb=dup(b_q_norm, LANES), gkb=dup(b_k_norm, LANES), g64=g64)
    tabs = _rope_tables(s_len)

    qa, ka, va, va_t, ga, qb, kb, vb, vb_t, gb, raw, xnb = _pre_fwd_call(x2, consts, tabs, tm)
    oa, lse_a_t = _attn_fwd_t_call(qa, ka, va_t, groups=A_KV_HEADS, sub=4, masked=True, scale=None, tq=ftq, tk=ftk,
                                   name="attn_fwd_a")
    ob, lse_b_t = _attn_fwd_t_call(qb, kb, vb_t, groups=1, sub=B_HEADS, masked=False, scale=None, tq=ftq,
                                   tk=ftk, name="attn_fwd_b")
    dh, doa, dob, dga, dgb, dw_out, loss_row = _out_call(x2, t2, oa, ob, ga, gb, wout, min(512, s_len))

    dqa, dka_t, dva_t = _attn_bwd_q_call(qa, ka, va, doa, oa, lse_a_t, groups=A_KV_HEADS, sub=4, masked=True,
                                         tq=tq, tk=tk, ck=min(512, s_len), name="attn_bwd_a")
    dqb, dkb_t, dvb_t = _attn_bwd_q_call(qb, kb, vb, dob, ob, lse_b_t, groups=2, sub=2, masked=False,
                                         tq=tq, tk=tk, ck=min(256, s_len), name="attn_bwd_b")
    dx, dproj_b, dwuq, dwuk, dwuv, small = _pre_bwd_call(
        x2, raw, dh, dqa, dka_t, dva_t, dga, dqb, dkb_t, dvb_t, dgb, consts, tabs, tm)
    dw_wide = _dw_in_call(xnb, dproj_b, min(512, s_len), 640)

    parts = _grad_blocks(dw_wide, dwuq, dwuk, dwuv, dw_out)
    small = jnp.concatenate([small[:ROW_LOSS], loss_row, small[ROW_LOSS + 1:]], axis=0)

    m_blk = (m_w_in[0], m_w_uq[0], m_w_ukv[0], m_w_out[0])
    v_blk = (v_w_in[0], v_w_uq[0], v_w_ukv[0], v_w_out[0])
    w_s = _pack_small(norm_in, a_q_norm, a_k_norm, b_cq_norm, b_ckv_norm, b_q_norm, b_k_norm)
    m_s = _pack_small(m_norm_in, m_a_q_norm, m_a_k_norm, m_b_cq_norm, m_b_ckv_norm, m_b_q_norm, m_b_k_norm)
    v_s = _pack_small(v_norm_in, v_a_q_norm, v_a_k_norm, v_b_cq_norm, v_b_ckv_norm, v_b_q_norm, v_b_k_norm)
    res = _reduce_blocks_call(parts, small, w_blk, m_blk, v_blk, w_s, m_s, v_s)

    def leaves(blocks, sm):
        wi, uq, ukv, wo = [b[None] for b in blocks]
        n_in, aq, ak, bcq, bckv, bq, bk = _unpack_small(sm)
        return [n_in, wi, aq, ak, bcq, bckv, uq, ukv, bq, bk, wo]

    g_s = res[16]
    loss = g_s[ROW_LOSS, 0]
    grad_x = dx.reshape(1, s_len, D_MODEL)
    return (loss, grad_x, *leaves(res[0:4], res[16]), *leaves(res[4:8], res[17]), *leaves(res[8:12], res[18]),
            *leaves(res[12:16], res[19]))
```

```python
import functools

import numpy as np
import jax
import jax.numpy as jnp
from jax import lax
from jax.experimental import pallas as pl
from jax.experimental.pallas import tpu as pltpu

F32 = jnp.float32
BF16 = jnp.bfloat16

D_MODEL = 1024
GRID_W = 64
ROPE_THETA = 10000.0
EPS = 1e-6
A_HEAD_DIM = 64
A_HEADS = 8
A_KV_HEADS = 2
B_HEADS = 4
B_NOPE_DIM = 64
B_ROPE_DIM = 32
B_QK_DIM = 96
B_V_DIM = 128
B_Q_RANK = 384
B_KV_RANK = 256
N_IN = 2464
N_DEV = 8

ADAM_LR = 0.001
ADAM_B1 = 0.9
ADAM_B2 = 0.999
ADAM_EPS = 1e-08
ADAM_WD = 0.01
ADAM_STEP = 10

QA_SCALE = 0.125
QB_SCALE = 1.0 / float(np.sqrt(B_QK_DIM))

LANES = 128
O_QA, O_KA, O_VA, O_GA, O_CQ, O_CKV, O_KR, O_GB, N_WIDE = 0, 512, 768, 1024, 1536, 1920, 2176, 2688, 3200
R_QA, R_KA, R_CQ, R_CKV, R_KR, R_WIDTH = 0, 512, 768, 1152, 1408, 1920

R_WIN = D_MODEL * (N_IN // N_DEV) // LANES
R_WUQ = (B_Q_RANK // N_DEV) * 384 // LANES
R_WUKV = B_KV_RANK * (768 // N_DEV) // LANES
R_WOUT = (D_MODEL // N_DEV) * D_MODEL // LANES
R_PACK = R_WIN + R_WUQ + R_WUKV + R_WOUT
R_SMALL = 24
ROW_LOSS = 17

VMEM_LIMIT = 56 * 1024 * 1024

NT = (((1,), (1,)), ((), ()))
TN = (((0,), (0,)), ((), ()))


def _dot(a, b):
    return jnp.dot(a, b, preferred_element_type=F32)


def _dot_nt(a, b):
    return lax.dot_general(a, b, NT, preferred_element_type=F32)


def _dot_tn(a, b):
    return lax.dot_general(a, b, TN, preferred_element_type=F32)


def _params(sem=None):
    return pltpu.CompilerParams(dimension_semantics=sem, vmem_limit_bytes=VMEM_LIMIT)


def _full(shape):
    nd = len(shape)
    return pl.BlockSpec(shape, lambda *_: (0,) * nd)


def _swap_sel(rows, shift):
    lane = lax.broadcasted_iota(jnp.int32, (rows, LANES), 1)
    return pltpu.roll(lane, shift, 1) == (lane ^ shift)


def _swap(x, shift, sel):
    return jnp.where(sel, pltpu.roll(x, shift, 1), pltpu.roll(x, LANES - shift, 1))


def _group_sum64(x, g64):
    hi = x.astype(BF16)
    lo = (x - hi.astype(F32)).astype(BF16)
    return _dot(hi, g64) + _dot(lo, g64)


def _row_sum(x):
    return jnp.sum(x, axis=-1, keepdims=True)


def _col_fwd(xs, msum, denom, gain, cos, sin, shift, sel):
    r = lax.rsqrt(msum(xs * xs) * (1.0 / denom) + EPS)
    xh = xs * r
    n = xh * gain
    return n * cos + _swap(n, shift, sel) * sin, xh, r


def _col_bwd(d_out, xh, r, msum, denom, gain, cos, sin, shift, sel):
    dn = d_out * cos + _swap(d_out * sin, shift, sel)
    dgain = jnp.sum(dn * xh, axis=0, keepdims=True)
    dxh = dn * gain
    dx = r * (dxh - xh * (msum(dxh * xh) * (1.0 / denom)))
    return dx, dgain


def _rms_fwd(x, gain):
    r = lax.rsqrt(jnp.mean(x * x, axis=-1, keepdims=True) + EPS)
    xh = x * r
    return xh * gain, xh, r


def _rms_bwd(dy, xh, r, gain):
    dgain = jnp.sum(dy * xh, axis=0, keepdims=True)
    dxh = dy * gain
    dx = r * (dxh - xh * jnp.mean(dxh * xh, axis=-1, keepdims=True))
    return dx, dgain


def _pre_forward(x, gin, w, wuq, wuk, wuv, gq, gk, gcq, gckv, gqb, gkb, ca, sa, cb, sb, g64, tm, raw=None):
    sel16 = _swap_sel(tm, 16)
    sel8 = _swap_sel(tm, 8)
    xn, xh0, r0 = _rms_fwd(x, gin)
    xnb = xn.astype(BF16)
    proj = None
    if raw is None:
        proj = _dot(xnb, w)
        raw = jnp.concatenate([proj[:, O_QA:O_QA + 512], proj[:, O_KA:O_KA + 256], proj[:, O_CQ:O_CQ + B_Q_RANK],
                               proj[:, O_CKV:O_CKV + B_KV_RANK], proj[:, O_KR:O_KR + 512]], axis=1)
    gs64 = functools.partial(_group_sum64, g64=g64)
    qa = [_col_fwd(raw[:, R_QA + LANES * s:R_QA + LANES * (s + 1)], gs64, 64.0, gq, ca, sa, 16, sel16)
          for s in range(4)]
    ka = [_col_fwd(raw[:, R_KA + LANES * s:R_KA + LANES * (s + 1)], _row_sum, 128.0, gk, ca, sa, 16, sel16)
          for s in range(2)]
    cq, cqh, rcq = _rms_fwd(raw[:, R_CQ:R_CQ + B_Q_RANK], gcq)
    cqb = cq.astype(BF16)
    qb_raw = _dot(cqb, wuq)
    qb = [_col_fwd(qb_raw[:, LANES * h:LANES * (h + 1)], _row_sum, float(B_QK_DIM), gqb, cb, sb, 8, sel8)
          for h in range(B_HEADS)]
    ckv, ckvh, rckv = _rms_fwd(raw[:, R_CKV:R_CKV + B_KV_RANK], gckv)
    ckvb = ckv.astype(BF16)
    kb_raw = _dot(ckvb, wuk) + raw[:, R_KR:R_KR + 512]
    vb = _dot(ckvb, wuv)
    kb = [_col_fwd(kb_raw[:, LANES * h:LANES * (h + 1)], _row_sum, float(B_QK_DIM), gkb, cb, sb, 8, sel8)
          for h in range(B_HEADS)]
    return dict(xh0=xh0, r0=r0, xnb=xnb, proj=proj, raw=raw, qa=qa, ka=ka, cqh=cqh, rcq=rcq, cqb=cqb, qb=qb,
                ckvh=ckvh, rckv=rckv, ckvb=ckvb, kb=kb, vb=vb, sel16=sel16, sel8=sel8, gs64=gs64)


def _rope_tiles(tab_refs, i, tm):
    per_tile = tm // GRID_W
    out = []
    for t in range(4):
        col_ref, row_ref = tab_refs[2 * t], tab_refs[2 * t + 1]
        col = col_ref[...]
        out.append(jnp.concatenate([col + row_ref[pl.ds(i * per_tile + b, 1), :] for b in range(per_tile)], axis=0))
    return out


_PRE_IN_NAMES = ("gin", "w", "wuq", "wuk", "wuv", "gq", "gk", "gcq", "gckv", "gqb", "gkb", "g64")


def _pre_const_specs(consts):
    return [_full(consts[n].shape) for n in _PRE_IN_NAMES]


def _pre_fwd_call(x, consts, tabs, tm):
    s_len = x.shape[0]

    def body(x_ref, *refs):
        tab_refs, refs = refs[:8], refs[8:]
        (gin_ref, w_ref, wuq_ref, wuk_ref, wuv_ref, gq_ref, gk_ref, gcq_ref, gckv_ref, gqb_ref, gkb_ref, g64_ref,
         qa_ref, ka_ref, va_ref, vat_ref, ga_ref, qb_ref, kb_ref, vb_ref, vbt_ref, gb_ref, raw_ref, xnb_ref) = refs
        ca, sa, cb, sb = _rope_tiles(tab_refs, pl.program_id(0), tm)
        f = _pre_forward(x_ref[...], gin_ref[...], w_ref[...], wuq_ref[...], wuk_ref[...], wuv_ref[...],
                         gq_ref[...], gk_ref[...], gcq_ref[...], gckv_ref[...], gqb_ref[...], gkb_ref[...],
                         ca, sa, cb, sb, g64_ref[...], tm)
        proj = f["proj"]
        raw_ref[...] = f["raw"]
        xnb_ref[...] = f["xnb"]
        for s in range(4):
            qa_ref[:, LANES * s:LANES * (s + 1)] = (f["qa"][s][0] * QA_SCALE).astype(BF16)
        for s in range(2):
            ka_ref[:, LANES * s:LANES * (s + 1)] = f["ka"][s][0].astype(BF16)
        va = proj[:, O_VA:O_VA + 256]
        va_ref[...] = va.astype(BF16)
        vat_ref[...] = va.T.astype(BF16)
        ga_ref[...] = proj[:, O_GA:O_GA + 512]
        for h in range(B_HEADS):
            qb_ref[:, LANES * h:LANES * (h + 1)] = (f["qb"][h][0] * QB_SCALE).astype(BF16)
            kb_ref[:, LANES * h:LANES * (h + 1)] = f["kb"][h][0].astype(BF16)
        vb_ref[...] = f["vb"].astype(BF16)
        vbt_ref[...] = f["vb"].T.astype(BF16)
        gb_ref[...] = proj[:, O_GB:O_GB + 512]

    def rows(width):
        return pl.BlockSpec((tm, width), lambda i: (i, 0))

    def cols(height):
        return pl.BlockSpec((height, tm), lambda i: (0, i))

    outs = [((s_len, 512), BF16, rows(512)), ((s_len, 256), BF16, rows(256)), ((s_len, 256), BF16, rows(256)),
            ((256, s_len), BF16, cols(256)), ((s_len, 512), F32, rows(512)), ((s_len, 512), BF16, rows(512)),
            ((s_len, 512), BF16, rows(512)), ((s_len, 512), BF16, rows(512)), ((512, s_len), BF16, cols(512)),
            ((s_len, 512), F32, rows(512)), ((s_len, R_WIDTH), F32, rows(R_WIDTH)),
            ((s_len, D_MODEL), BF16, rows(D_MODEL))]
    return pl.pallas_call(
        body, name="pre_fwd", grid=(s_len // tm,),
        in_specs=[rows(D_MODEL)] + [_full(t.shape) for t in tabs] + _pre_const_specs(consts),
        out_specs=[sp for _, _, sp in outs],
        out_shape=[jax.ShapeDtypeStruct(sh, dt) for sh, dt, _ in outs],
        compiler_params=_params(("arbitrary",)),
    )(x, *tabs, *[consts[n] for n in _PRE_IN_NAMES])


def _head_masks(rows):
    lane = lax.broadcasted_iota(jnp.int32, (rows, LANES), 1)
    return lane < 64, lane >= 64


def _lane_fold(x, op):
    out = x[:, 0:LANES]
    for c in range(1, x.shape[1] // LANES):
        out = op(out, x[:, LANES * c:LANES * (c + 1)])
    return out


def _row_fold(x, op):
    return op(x.reshape(x.shape[0] // 8, 8, x.shape[1]), axis=0)


def _attn_fwd_t_call(q, k, vt, *, groups, sub, masked, scale, tq, tk, name):
    s_len = q.shape[0]
    qw = LANES * (sub // 2 if masked else sub)
    kvw = LANES if masked else LANES * sub
    n_c = s_len // tk
    kv_mode = pl.Buffered(1) if groups == 1 else None

    def body(q_ref, k_ref, vt_ref, o_ref, lse_ref, s_sc):
        keep = _head_masks(tq) if masked else None

        def kv_of(hh):
            return slice(0, LANES) if masked else slice(LANES * hh, LANES * (hh + 1))

        def q_of(hh):
            if not masked:
                return q_ref[:, LANES * hh:LANES * (hh + 1)]
            qp = q_ref[:, LANES * (hh // 2):LANES * (hh // 2 + 1)]
            return jnp.where(keep[hh % 2], qp, jnp.zeros_like(qp))

        def scores(hh, qm, c, mx):
            s_t = _dot_nt(k_ref[tk * c:tk * (c + 1), kv_of(hh)], qm)
            if scale is not None:
                s_t = s_t * scale
            s_sc[hh % 2, c] = s_t
            return jnp.maximum(mx, _row_fold(s_t, jnp.max))

        neg = jnp.full((8, tq), -jnp.inf, F32)
        qm_next = q_of(0)
        mx_next = neg
        for c in range(n_c):
            mx_next = scores(0, qm_next, c, mx_next)
        outs = []
        for hh in range(sub):
            m = jnp.max(mx_next, axis=0, keepdims=True)
            if hh + 1 < sub:
                qm_next = q_of(hh + 1)
                mx_next = neg
            lsum = jnp.zeros((8, tq), F32)
            acc = jnp.zeros((LANES, tq), F32)
            for c in range(n_c):
                if hh + 1 < sub:
                    mx_next = scores(hh + 1, qm_next, c, mx_next)
                p_t = jnp.exp(s_sc[hh % 2, c] - m)
                lsum = lsum + _row_fold(p_t, jnp.sum)
                acc = acc + _dot(vt_ref[kv_of(hh), tk * c:tk * (c + 1)], p_t.astype(BF16))
            l = jnp.sum(lsum, axis=0, keepdims=True)
            outs.append((acc / l).T)
            lse_ref[hh] = m + jnp.log(l)
        if masked:
            for pr in range(sub // 2):
                o_ref[:, LANES * pr:LANES * (pr + 1)] = jnp.where(keep[0], outs[2 * pr], outs[2 * pr + 1])
        else:
            for hh in range(sub):
                o_ref[:, LANES * hh:LANES * (hh + 1)] = outs[hh]

    return pl.pallas_call(
        body, name=name, grid=(groups, s_len // tq),
        in_specs=[pl.BlockSpec((tq, qw), lambda g, i: (i, g)),
                  pl.BlockSpec((s_len, kvw), lambda g, i: (0, g), pipeline_mode=kv_mode),
                  pl.BlockSpec((kvw, s_len), lambda g, i: (g, 0), pipeline_mode=kv_mode)],
        out_specs=[pl.BlockSpec((tq, qw), lambda g, i: (i, g)),
                   pl.BlockSpec((sub, 1, tq), lambda g, i: (g, 0, i))],
        out_shape=[jax.ShapeDtypeStruct((s_len, groups * qw), F32),
                   jax.ShapeDtypeStruct((groups * sub, 1, s_len), F32)],
        scratch_shapes=[pltpu.VMEM((min(sub, 2), n_c, tk, tq), F32)],
        compiler_params=_params(("arbitrary", "arbitrary")),
    )(q, k, vt)


def _attn_fwd_call(q, k, v, *, groups, sub, masked, scale, tq, tk, name):
    s_len = q.shape[0]
    qw = LANES * (sub // 2 if masked else sub)
    n_c = s_len // tk
    log2e = float(np.log2(np.e))
    mul = log2e if scale is None else scale * log2e

    def body(q_ref, k_ref, v_ref, o_ref, lse_ref, s_sc):
        keep = _head_masks(tq) if masked else None
        outs = []
        for hh in range(sub):
            if masked:
                qp = q_ref[:, LANES * (hh // 2):LANES * (hh // 2 + 1)]
                qm = jnp.where(keep[hh % 2], qp, jnp.zeros_like(qp))
                kv_cols = slice(0, LANES)
            else:
                qm = q_ref[:, LANES * hh:LANES * (hh + 1)]
                kv_cols = slice(LANES * hh, LANES * (hh + 1))
            buf = hh % 2

            def scores(c, mx):
                rows = pl.ds(pl.multiple_of(c * tk, tk), tk)
                s = _dot_nt(qm, k_ref[rows, kv_cols]) * mul
                s_sc[buf, c] = s
                return jnp.maximum(mx, _lane_fold(s, jnp.maximum))

            mx = lax.fori_loop(0, n_c, scores, jnp.full((tq, LANES), -jnp.inf, F32), unroll=True)
            m = jnp.max(mx, axis=1, keepdims=True)

            def weights(c, carry):
                lsum, acc = carry
                rows = pl.ds(pl.multiple_of(c * tk, tk), tk)
                p = jnp.exp2(s_sc[buf, c] - m)
                return lsum + _lane_fold(p, jnp.add), acc + _dot(p.astype(BF16), v_ref[rows, kv_cols])

            lsum, acc = lax.fori_loop(0, n_c, weights, (jnp.zeros((tq, LANES), F32), jnp.zeros((tq, LANES), F32)),
                                      unroll=True)
            l = jnp.sum(lsum, axis=1, keepdims=True)
            outs.append(acc / l)
            lse_ref[hh] = jnp.broadcast_to((m + jnp.log2(l)) * (1.0 / log2e), (tq, LANES))
        if masked:
            for pr in range(sub // 2):
                o_ref[:, LANES * pr:LANES * (pr + 1)] = jnp.where(keep[0], outs[2 * pr], outs[2 * pr + 1])
        else:
            for hh in range(sub):
                o_ref[:, LANES * hh:LANES * (hh + 1)] = outs[hh]

    kvw = LANES if masked else LANES * sub
    return pl.pallas_call(
        body, name=name, grid=(groups, s_len // tq),
        in_specs=[pl.BlockSpec((tq, qw), lambda g, i: (i, g)),
                  pl.BlockSpec((s_len, kvw), lambda g, i: (0, g)),
                  pl.BlockSpec((s_len, kvw), lambda g, i: (0, g))],
        out_specs=[pl.BlockSpec((tq, qw), lambda g, i: (i, g)),
                   pl.BlockSpec((sub, tq, LANES), lambda g, i: (g, i, 0))],
        out_shape=[jax.ShapeDtypeStruct((s_len, groups * qw), F32),
                   jax.ShapeDtypeStruct((groups * sub, s_len, LANES), F32)],
        scratch_shapes=[pltpu.VMEM((min(sub, 2), n_c, tq, tk), F32)],
        compiler_params=_params(("arbitrary", "arbitrary")),
    )(q, k, v)


def _attn_bwd_q_call(q, k, v, do, o, lse_t, *, groups, sub, masked, tq, tk, ck, name):
    s_len = q.shape[0]
    qw = LANES * (sub // 2 if masked else sub)
    kvw = LANES if masked else LANES * sub
    n_c = tk // ck

    def body(q_ref, k_ref, v_ref, do_ref, o_ref, lse_ref, dq_ref, dkt_ref, dvt_ref):
        j = pl.program_id(1)
        i = pl.program_id(2)

        @pl.when((j == 0) & (i == 0))
        def _():
            dq_ref[...] = jnp.zeros(dq_ref.shape, F32)

        @pl.when(i == 0)
        def _():
            dkt_ref[...] = jnp.zeros(dkt_ref.shape, F32)
            dvt_ref[...] = jnp.zeros(dvt_ref.shape, F32)

        lkeep = _head_masks(tq) if masked else None
        heads = []
        for hh in range(sub):
            if masked:
                cols = slice(LANES * (hh // 2), LANES * (hh // 2 + 1))
                kv = slice(0, LANES)
                qp, dop = q_ref[:, cols], do_ref[:, cols]
                qm = jnp.where(lkeep[hh % 2], qp, jnp.zeros_like(qp))
                dom = jnp.where(lkeep[hh % 2], dop, jnp.zeros_like(dop))
            else:
                cols = kv = slice(LANES * hh, LANES * (hh + 1))
                qm, dom = q_ref[:, cols], do_ref[:, cols]
            delta = jnp.sum(dom.astype(F32) * o_ref[:, cols], axis=1, keepdims=True)
            lse = jnp.broadcast_to(lse_ref[hh], (LANES, tq)).T[:, 0:1]
            heads.append((cols, kv, qm, dom, qm.T, dom.T, delta, lse))

        def products(hh, c):
            _, kv, qm, dom, _, _, _, _ = heads[hh]
            return _dot_nt(qm, k_ref[ck * c:ck * (c + 1), kv]), _dot_nt(dom, v_ref[ck * c:ck * (c + 1), kv])

        items = [(hh, c) for hh in range(sub) for c in range(n_c)]
        dq_acc = [jnp.zeros((tq, LANES), F32) for _ in range(sub)]
        nxt = products(*items[0])
        for n, (hh, c) in enumerate(items):
            s, dp = nxt
            if n + 1 < len(items):
                nxt = products(*items[n + 1])
            _, kv, qm, dom, qmt, domt, delta, lse = heads[hh]
            p = jnp.exp(s - lse)
            ds = p * (dp - delta)
            p_b = p.astype(BF16)
            ds_b = ds.astype(BF16)
            kcols = slice(ck * c, ck * (c + 1))
            dvt_ref[kv, kcols] += _dot(domt, p_b)
            dkt_ref[kv, kcols] += _dot(qmt, ds_b)
            dq_acc[hh] = dq_acc[hh] + _dot(ds_b, k_ref[kcols, kv])
        rows = pl.ds(pl.multiple_of(i * tq, tq), tq)
        if masked:
            for pr in range(sub // 2):
                dq_ref[rows, LANES * pr:LANES * (pr + 1)] += jnp.where(lkeep[0], dq_acc[2 * pr], dq_acc[2 * pr + 1])
        else:
            for hh in range(sub):
                dq_ref[rows, LANES * hh:LANES * (hh + 1)] += dq_acc[hh]

    return pl.pallas_call(
        body, name=name, grid=(groups, s_len // tk, s_len // tq),
        in_specs=[pl.BlockSpec((tq, qw), lambda g, j, i: (i, g)),
                  pl.BlockSpec((tk, kvw), lambda g, j, i: (j, g)),
                  pl.BlockSpec((tk, kvw), lambda g, j, i: (j, g)),
                  pl.BlockSpec((tq, qw), lambda g, j, i: (i, g)),
                  pl.BlockSpec((tq, qw), lambda g, j, i: (i, g)),
                  pl.BlockSpec((sub, 1, tq), lambda g, j, i: (g, 0, i))],
        out_specs=[pl.BlockSpec((s_len, qw), lambda g, j, i: (0, g)),
                   pl.BlockSpec((kvw, tk), lambda g, j, i: (g, j)),
                   pl.BlockSpec((kvw, tk), lambda g, j, i: (g, j))],
        out_shape=[jax.ShapeDtypeStruct((s_len, groups * qw), F32),
                   jax.ShapeDtypeStruct((groups * kvw, s_len), F32),
                   jax.ShapeDtypeStruct((groups * kvw, s_len), F32)],
        compiler_params=_params(("arbitrary", "arbitrary", "arbitrary")),
    )(q, k, v, do, o, lse_t)


def _attn_bwd_p_call(q, k, v, do, o, linv_t, p, *, groups, sub, masked, tq, tk, ck, name):
    s_len = q.shape[0]
    qw = LANES * (sub // 2 if masked else sub)
    kvw = LANES if masked else LANES * sub
    n_c = tk // ck

    def body(q_ref, k_ref, v_ref, do_ref, o_ref, linv_ref, p_ref, dq_ref, dkt_ref, dvt_ref):
        j = pl.program_id(1)
        i = pl.program_id(2)

        @pl.when((j == 0) & (i == 0))
        def _():
            dq_ref[...] = jnp.zeros(dq_ref.shape, F32)

        @pl.when(i == 0)
        def _():
            dkt_ref[...] = jnp.zeros(dkt_ref.shape, F32)
            dvt_ref[...] = jnp.zeros(dvt_ref.shape, F32)

        lkeep = _head_masks(tq) if masked else None
        heads = []
        for hh in range(sub):
            cols = slice(LANES * (hh // 2), LANES * (hh // 2 + 1)) if masked else slice(LANES * hh, LANES * (hh + 1))
            kv = slice(0, LANES) if masked else cols
            linv = jnp.broadcast_to(linv_ref[hh], (LANES, tq)).T
            qm = q_ref[:, cols]
            dof = do_ref[:, cols] * linv
            if masked:
                qm = jnp.where(lkeep[hh % 2], qm, jnp.zeros_like(qm))
                dof = jnp.where(lkeep[hh % 2], dof, jnp.zeros_like(dof))
            delta = jnp.sum(dof * o_ref[:, cols], axis=1, keepdims=True)
            dom = dof.astype(BF16)
            heads.append((kv, qm, dom, qm.T, dom.T, delta))

        def product(hh, c):
            kv, _, dom, _, _, _ = heads[hh]
            return _dot_nt(dom, v_ref[ck * c:ck * (c + 1), kv])

        items = [(hh, c) for hh in range(sub) for c in range(n_c)]
        dq_acc = [jnp.zeros((tq, LANES), F32) for _ in range(sub)]
        nxt = product(*items[0])
        for n, (hh, c) in enumerate(items):
            dp = nxt
            if n + 1 < len(items):
                nxt = product(*items[n + 1])
            kv, qm, dom, qmt, domt, delta = heads[hh]
            kcols = slice(ck * c, ck * (c + 1))
            p_b = p_ref[hh, :, kcols]
            ds_b = (p_b.astype(F32) * (dp - delta)).astype(BF16)
            dvt_ref[kv, kcols] += _dot(domt, p_b)
            dkt_ref[kv, kcols] += _dot(qmt, ds_b)
            dq_acc[hh] = dq_acc[hh] + _dot(ds_b, k_ref[kcols, kv])
        rows = pl.ds(pl.multiple_of(i * tq, tq), tq)
        if masked:
            for pr in range(sub // 2):
                dq_ref[rows, LANES * pr:LANES * (pr + 1)] += jnp.where(lkeep[0], dq_acc[2 * pr], dq_acc[2 * pr + 1])
        else:
            for hh in range(sub):
                dq_ref[rows, LANES * hh:LANES * (hh + 1)] += dq_acc[hh]

    return pl.pallas_call(
        body, name=name, grid=(groups, s_len // tk, s_len // tq),
        in_specs=[pl.BlockSpec((tq, qw), lambda g, j, i: (i, g)),
                  pl.BlockSpec((tk, kvw), lambda g, j, i: (j, g)),
                  pl.BlockSpec((tk, kvw), lambda g, j, i: (j, g)),
                  pl.BlockSpec((tq, qw), lambda g, j, i: (i, g)),
                  pl.BlockSpec((tq, qw), lambda g, j, i: (i, g)),
                  pl.BlockSpec((sub, 1, tq), lambda g, j, i: (g, 0, i)),
                  pl.BlockSpec((sub, tq, tk), lambda g, j, i: (g, i, j))],
        out_specs=[pl.BlockSpec((s_len, qw), lambda g, j, i: (0, g)),
                   pl.BlockSpec((kvw, tk), lambda g, j, i: (g, j)),
                   pl.BlockSpec((kvw, tk), lambda g, j, i: (g, j))],
        out_shape=[jax.ShapeDtypeStruct((s_len, groups * qw), F32),
                   jax.ShapeDtypeStruct((groups * kvw, s_len), F32),
                   jax.ShapeDtypeStruct((groups * kvw, s_len), F32)],
        compiler_params=_params(("arbitrary", "arbitrary", "arbitrary")),
    )(q, k, v, do, o, linv_t, p)


def _attn_bwd_call(q, k, v, do, lse_t, delta_t, *, groups, sub, scale, tq, tk, name):
    s_len = q.shape[0]
    masked = sub > 1
    qw = LANES * (sub // 2 if masked else 1)
    n_k = s_len // tk

    def body(q_ref, k_ref, v_ref, do_ref, lse_ref, dl_ref, dq_ref, dk_ref, dv_ref, dq_sc):
        i = pl.program_id(1)
        j = pl.program_id(2)

        @pl.when((i == 0) & (j == 0))
        def _():
            dk_ref[...] = jnp.zeros(dk_ref.shape, F32)
            dv_ref[...] = jnp.zeros(dv_ref.shape, F32)

        @pl.when(j == 0)
        def _():
            dq_sc[...] = jnp.zeros(dq_sc.shape, F32)

        kk = k_ref[...]
        vv = v_ref[...]
        keep = _head_masks(tq) if masked else None
        dk_t = jnp.zeros((tk, LANES), F32)
        dv_t = jnp.zeros((tk, LANES), F32)
        for hh in range(sub):
            if masked:
                cols = slice(LANES * (hh // 2), LANES * (hh // 2 + 1))
                qp = q_ref[:, cols]
                dop = do_ref[:, cols]
                qm = jnp.where(keep[hh % 2], qp, jnp.zeros_like(qp))
                dom = jnp.where(keep[hh % 2], dop, jnp.zeros_like(dop))
            else:
                cols = slice(0, LANES)
                qm = q_ref[...]
                dom = do_ref[...]
            s_t = _dot_nt(kk, qm)
            if scale is not None:
                s_t = s_t * scale
            p_t = jnp.exp(s_t - lse_ref[hh])
            dp_t = _dot_nt(vv, dom)
            ds_t = p_t * (dp_t - dl_ref[hh])
            if scale is not None:
                ds_t = ds_t * scale
            p_b = p_t.astype(BF16)
            ds_b = ds_t.astype(BF16)
            dv_t = dv_t + _dot(p_b, dom)
            dk_t = dk_t + _dot(ds_b, qm)
            dq_h = _dot_tn(ds_b, kk)
            if masked:
                dq_h = jnp.where(keep[hh % 2], dq_h, jnp.zeros_like(dq_h))
            dq_sc[:, cols] += dq_h
        rows = pl.ds(pl.multiple_of(j * tk, tk), tk)
        dk_ref[rows, :] += dk_t
        dv_ref[rows, :] += dv_t

        @pl.when(j == n_k - 1)
        def _():
            dq_ref[...] = dq_sc[...]

    return pl.pallas_call(
        body, name=name, grid=(groups, s_len // tq, n_k),
        in_specs=[pl.BlockSpec((tq, qw), lambda g, i, j: (i, g)),
                  pl.BlockSpec((tk, LANES), lambda g, i, j: (j, g)),
                  pl.BlockSpec((tk, LANES), lambda g, i, j: (j, g)),
                  pl.BlockSpec((tq, qw), lambda g, i, j: (i, g)),
                  pl.BlockSpec((sub, 1, tq), lambda g, i, j: (g, 0, i)),
                  pl.BlockSpec((sub, 1, tq), lambda g, i, j: (g, 0, i))],
        out_specs=[pl.BlockSpec((tq, qw), lambda g, i, j: (i, g)),
                   pl.BlockSpec((s_len, LANES), lambda g, i, j: (0, g)),
                   pl.BlockSpec((s_len, LANES), lambda g, i, j: (0, g))],
        out_shape=[jax.ShapeDtypeStruct((s_len, groups * qw), F32),
                   jax.ShapeDtypeStruct((s_len, groups * LANES), F32),
                   jax.ShapeDtypeStruct((s_len, groups * LANES), F32)],
        scratch_shapes=[pltpu.VMEM((tq, qw), F32)],
        compiler_params=_params(("arbitrary", "arbitrary", "arbitrary")),
    )(q, k, v, do, lse_t, delta_t)


def _silu_parts(g):
    sig = 1.0 / (1.0 + jnp.exp(-g))
    return g * sig, sig * (1.0 + g * (1.0 - sig))


def _out_call(x, target, oa, ob, ga, gb, wout, tm):
    s_len = x.shape[0]
    n_t = s_len // tm

    def body(x_ref, t_ref, oa_ref, ob_ref, ga_ref, gb_ref, w_ref,
             dh_ref, doa_ref, dob_ref, dga_ref, dgb_ref, dw_ref, loss_ref):
        i = pl.program_id(0)

        @pl.when(i == 0)
        def _():
            dw_ref[...] = jnp.zeros(dw_ref.shape, F32)
            loss_ref[...] = jnp.zeros(loss_ref.shape, F32)

        oa_v, ob_v = oa_ref[...], ob_ref[...]
        silu_a, dsilu_a = _silu_parts(ga_ref[...])
        silu_b, dsilu_b = _silu_parts(gb_ref[...])
        ya = (oa_v * silu_a).astype(BF16)
        yb = (ob_v * silu_b).astype(BF16)
        h = x_ref[...] + _dot(ya, w_ref[0:512, :]) + _dot(yb, w_ref[512:1024, :])
        err = h - t_ref[...]
        part = jnp.sum(err * err, axis=0, keepdims=True)
        acc = part[:, 0:LANES]
        for c in range(1, D_MODEL // LANES):
            acc = acc + part[:, LANES * c:LANES * (c + 1)]
        loss_ref[...] += acc
        dh = err * (1.0 / D_MODEL)
        dh_ref[...] = dh
        dhb = dh.astype(BF16)
        dya = _dot_nt(dhb, w_ref[0:512, :])
        dyb = _dot_nt(dhb, w_ref[512:1024, :])
        doa = dya * silu_a
        dob = dyb * silu_b
        doa_ref[...] = doa.astype(BF16)
        dob_ref[...] = dob.astype(BF16)
        dga_ref[...] = dya * oa_v * dsilu_a
        dgb_ref[...] = dyb * ob_v * dsilu_b
        dw_ref[0:512, :] += _dot_tn(ya, dhb)
        dw_ref[512:1024, :] += _dot_tn(yb, dhb)

    def rows(width):
        return pl.BlockSpec((tm, width), lambda i: (i, 0))

    outs = [(D_MODEL, F32), (512, BF16), (512, BF16), (512, F32), (512, F32)]
    return pl.pallas_call(
        body, name="out_fwd", grid=(n_t,),
        in_specs=[rows(D_MODEL), rows(D_MODEL), rows(512), rows(512), rows(512), rows(512),
                  _full((D_MODEL, D_MODEL))],
        out_specs=[rows(wd) for wd, _ in outs] + [_full((D_MODEL, D_MODEL)), _full((1, LANES))],
        out_shape=[jax.ShapeDtypeStruct((s_len, wd), dt) for wd, dt in outs]
        + [jax.ShapeDtypeStruct((D_MODEL, D_MODEL), F32), jax.ShapeDtypeStruct((1, LANES), F32)],
        compiler_params=_params(("arbitrary",)),
    )(x, target, oa, ob, ga, gb, wout)


def _pre_bwd_call(x, raw, dh, dqa, dka, dva, dga, dqb, dkb, dvb, dgb, consts, tabs, tm):
    s_len = x.shape[0]

    def body(x_ref, raw_ref, dh_ref, dqa_ref, dkat_ref, dvat_ref, dga_ref, dqb_ref, dkbt_ref, dvbt_ref, dgb_ref,
             *refs):
        tab_refs, refs = refs[:8], refs[8:]
        (gin_ref, w_ref, wuq_ref, wuk_ref, wuv_ref, gq_ref, gk_ref, gcq_ref, gckv_ref, gqb_ref, gkb_ref, g64_ref,
         dx_ref, dproj_ref, dwuq_ref, dwuk_ref, dwuv_ref, small_ref) = refs
        i = pl.program_id(0)
        dka_v, dva_v = dkat_ref[...].T, dvat_ref[...].T
        dkb_v, dvb_v = dkbt_ref[...].T, dvbt_ref[...].T

        @pl.when(i == 0)
        def _():
            dwuq_ref[...] = jnp.zeros(dwuq_ref.shape, F32)
            dwuk_ref[...] = jnp.zeros(dwuk_ref.shape, F32)
            dwuv_ref[...] = jnp.zeros(dwuv_ref.shape, F32)
            small_ref[...] = jnp.zeros(small_ref.shape, F32)

        gin, gq, gk = gin_ref[...], gq_ref[...], gk_ref[...]
        gcq, gckv, gqb, gkb = gcq_ref[...], gckv_ref[...], gqb_ref[...], gkb_ref[...]
        ca, sa, cb, sb = _rope_tiles(tab_refs, i, tm)
        w, wuq, wuk, wuv = w_ref[...], wuq_ref[...], wuk_ref[...], wuv_ref[...]
        f = _pre_forward(x_ref[...], gin, w, wuq, wuk, wuv, gq, gk, gcq, gckv, gqb, gkb,
                         ca, sa, cb, sb, g64_ref[...], tm, raw=raw_ref[...])
        sel16, sel8, gs64 = f["sel16"], f["sel8"], f["gs64"]
        lane = lax.broadcasted_iota(jnp.int32, (tm, LANES), 1)
        low = lane < 64
        zero = jnp.zeros((tm, LANES), F32)
        pieces = []

        dgq = jnp.zeros((1, LANES), F32)
        for s in range(4):
            _, xh, r = f["qa"][s]
            d = dqa_ref[:, LANES * s:LANES * (s + 1)] * QA_SCALE
            dx, dg = _col_bwd(d, xh, r, gs64, 64.0, gq, ca, sa, 16, sel16)
            pieces.append(dx)
            dgq = dgq + dg
        dgk = jnp.zeros((1, LANES), F32)
        for s in range(2):
            _, xh, r = f["ka"][s]
            d = dka_v[:, LANES * s:LANES * (s + 1)]
            d = d + pltpu.roll(d, 64, 1)
            dx, dg = _col_bwd(d, xh, r, _row_sum, 128.0, gk, ca, sa, 16, sel16)
            pieces.append(jnp.where(low, dx, zero))
            dgk = dgk + dg
        for s in range(2):
            d = dva_v[:, LANES * s:LANES * (s + 1)]
            d = d + pltpu.roll(d, 64, 1)
            pieces.append(jnp.where(low, d, zero))
        pieces.append(dga_ref[...])

        dgqb = jnp.zeros((1, LANES), F32)
        dq_cols = []
        for h in range(B_HEADS):
            _, xh, r = f["qb"][h]
            dx, dg = _col_bwd(dqb_ref[:, LANES * h:LANES * (h + 1)] * QB_SCALE, xh, r, _row_sum, float(B_QK_DIM),
                              gqb, cb, sb, 8, sel8)
            dq_cols.append(dx)
            dgqb = dgqb + dg
        dqr_b = jnp.concatenate(dq_cols, axis=1).astype(BF16)
        dwuq_ref[...] += _dot_tn(f["cqb"], dqr_b)
        dcq_raw, dgcq = _rms_bwd(_dot_nt(dqr_b, wuq), f["cqh"], f["rcq"], gcq)
        pieces.append(dcq_raw)

        dgkb = jnp.zeros((1, LANES), F32)
        dk_cols = []
        dkr = zero
        for h in range(B_HEADS):
            _, xh, r = f["kb"][h]
            dx, dg = _col_bwd(dkb_v[:, LANES * h:LANES * (h + 1)], xh, r, _row_sum, float(B_QK_DIM),
                              gkb, cb, sb, 8, sel8)
            dk_cols.append(dx)
            dkr = dkr + dx
            dgkb = dgkb + dg
        dkr_b = jnp.concatenate(dk_cols, axis=1).astype(BF16)
        dvb_b = dvb_v.astype(BF16)
        dwuk_ref[...] += _dot_tn(f["ckvb"], dkr_b)
        dwuv_ref[...] += _dot_tn(f["ckvb"], dvb_b)
        dckv = _dot_nt(dkr_b, wuk) + _dot_nt(dvb_b, wuv)
        dckv_raw, dgckv = _rms_bwd(dckv, f["ckvh"], f["rckv"], gckv)
        pieces.append(dckv_raw)
        pieces.append(jnp.where((lane >= B_NOPE_DIM) & (lane < B_QK_DIM), dkr, zero))
        pieces += [zero, zero, zero]
        pieces.append(dgb_ref[...])

        dproj_b = jnp.concatenate(pieces, axis=1).astype(BF16)
        dproj_ref[...] = dproj_b
        dxn = _dot_nt(dproj_b, w)
        dx, dgin = _rms_bwd(dxn, f["xh0"], f["r0"], gin)
        dx_ref[...] = dx + dh_ref[...]

        for c in range(D_MODEL // LANES):
            small_ref[c:c + 1, :] += dgin[:, LANES * c:LANES * (c + 1)]
        small_ref[8:9, :] += dgq
        small_ref[9:10, :] += dgk
        for c in range(3):
            small_ref[10 + c:11 + c, :] += dgcq[:, LANES * c:LANES * (c + 1)]
        for c in range(2):
            small_ref[13 + c:14 + c, :] += dgckv[:, LANES * c:LANES * (c + 1)]
        small_ref[15:16, :] += dgqb
        small_ref[16:17, :] += dgkb

    def rows(width):
        return pl.BlockSpec((tm, width), lambda i: (i, 0))

    def cols(height):
        return pl.BlockSpec((height, tm), lambda i: (0, i))

    return pl.pallas_call(
        body, name="pre_bwd", grid=(s_len // tm,),
        in_specs=[rows(D_MODEL), rows(R_WIDTH), rows(D_MODEL), rows(512), cols(256), cols(256), rows(512), rows(512),
                  cols(512), cols(512), rows(512)] + [_full(t.shape) for t in tabs] + _pre_const_specs(consts),
        out_specs=[rows(D_MODEL), rows(N_WIDE), _full((B_Q_RANK, 512)), _full((B_KV_RANK, 512)),
                   _full((B_KV_RANK, 512)), _full((R_SMALL, LANES))],
        out_shape=[jax.ShapeDtypeStruct((s_len, D_MODEL), F32), jax.ShapeDtypeStruct((s_len, N_WIDE), BF16),
                   jax.ShapeDtypeStruct((B_Q_RANK, 512), F32),
                   jax.ShapeDtypeStruct((B_KV_RANK, 512), F32), jax.ShapeDtypeStruct((B_KV_RANK, 512), F32),
                   jax.ShapeDtypeStruct((R_SMALL, LANES), F32)],
        compiler_params=_params(("arbitrary",)),
    )(x, raw, dh, dqa, dka, dva, dga, dqb, dkb, dvb, dgb, *tabs, *[consts[n] for n in _PRE_IN_NAMES])


def _dw_in_call(xnb, dproj_b, tt, tn):
    s_len = xnb.shape[0]

    def body(a_ref, b_ref, o_ref):
        @pl.when(pl.program_id(1) == 0)
        def _():
            o_ref[...] = jnp.zeros(o_ref.shape, F32)

        o_ref[...] += _dot_tn(a_ref[...], b_ref[...])

    return pl.pallas_call(
        body, name="dw_in", grid=(N_WIDE // tn, s_len // tt),
        in_specs=[pl.BlockSpec((tt, D_MODEL), lambda n, t: (t, 0)), pl.BlockSpec((tt, tn), lambda n, t: (t, n))],
        out_specs=pl.BlockSpec((D_MODEL, tn), lambda n, t: (0, n)),
        out_shape=jax.ShapeDtypeStruct((D_MODEL, N_WIDE), F32),
        compiler_params=_params(("arbitrary", "arbitrary")),
    )(xnb, dproj_b)


def _mesh_pos():
    return lax.axis_index("x"), lax.axis_index("y"), lax.axis_index("c")


def _flip(v, bit):
    return 1 - v if bit else v


def _peer(pos, k):
    x, y, c = pos
    return _flip(x, (k >> 2) & 1), _flip(y, (k >> 1) & 1), _flip(c, k & 1)


def _logical(p):
    return 4 * p[0] + 2 * p[1] + p[2]


def _gather_weights_call(shard):
    m_per = shard.shape[0]

    def body(x_ref, out_ref, xb_ref, send_sems, recv_sems, local_sem):
        x, y, c = _mesh_pos()
        me, sibling = (x, y, c), (x, y, 1 - c)
        chips = [(1 - x, y), (x, 1 - y), (1 - x, 1 - y)]
        xb_ref[...] = x_ref[...].astype(BF16)

        def rows(p):
            return out_ref.at[pl.ds(pl.multiple_of(_logical(p) * m_per, 16), m_per), :]

        def copy(k, block, to, src=None):
            return pltpu.make_async_remote_copy(
                src_ref=rows(block) if src is None else src, dst_ref=rows(block),
                send_sem=send_sems.at[k], recv_sem=recv_sems.at[k],
                device_id=to, device_id_type=pl.DeviceIdType.MESH)

        mine = pltpu.make_async_copy(xb_ref, rows(me), local_sem)
        mine.start()
        first = [copy(0, me, sibling, src=xb_ref)]
        first += [copy(1 + j, me, (*chip, c), src=xb_ref) for j, chip in enumerate(chips)]
        for cp in first:
            cp.start()
        passed = [copy(4 + j, (*chip, c), sibling) for j, chip in enumerate(chips)]
        for j, chip in enumerate(chips):
            copy(1 + j, (*chip, c), me).wait_recv()
            passed[j].start()
        copy(0, sibling, me).wait_recv()
        for j, chip in enumerate(chips):
            copy(4 + j, (*chip, 1 - c), me).wait_recv()
        for cp in first + passed:
            cp.wait_send()
        mine.wait()

    return pl.pallas_call(
        body, name="gather_weights",
        out_shape=jax.ShapeDtypeStruct((N_DEV * m_per, LANES), BF16),
        in_specs=[pl.BlockSpec(memory_space=pltpu.VMEM)],
        out_specs=pl.BlockSpec(memory_space=pltpu.VMEM),
        scratch_shapes=[pltpu.VMEM((m_per, LANES), BF16), pltpu.SemaphoreType.DMA((7,)),
                        pltpu.SemaphoreType.DMA((7,)), pltpu.SemaphoreType.DMA],
        compiler_params=pltpu.CompilerParams(vmem_limit_bytes=VMEM_LIMIT),
    )(shard)


def _adamw(w, g, m, v):
    m = ADAM_B1 * m + (1.0 - ADAM_B1) * g
    v = ADAM_B2 * v + (1.0 - ADAM_B2) * (g * g)
    m_hat = m / (1.0 - ADAM_B1 ** ADAM_STEP)
    v_hat = v / (1.0 - ADAM_B2 ** ADAM_STEP)
    delta = -ADAM_LR * (m_hat / (jnp.sqrt(v_hat) + ADAM_EPS) + ADAM_WD * w)
    return delta, m, v


def _reduce_adamw_call(parts, small, w_pk, m_pk, v_pk, w_s, m_s, v_s):
    chunk = 16
    n_chunks = R_PACK // chunk

    def body(parts_ref, small_ref, w_ref, m_ref, v_ref, ws_ref, ms_ref, vs_ref,
             g_ref, d_ref, nm_ref, nv_ref, gs_ref, ds_ref, nms_ref, nvs_ref,
             recv_ref, recv_s_ref, send_sems, recv_sems, send_s_sems, recv_s_sems, local_sem):
        pos = _mesh_pos()
        me = _logical(pos)

        def big(k):
            peer = _peer(pos, k)
            return pltpu.make_async_remote_copy(
                src_ref=parts_ref.at[_logical(peer)], dst_ref=recv_ref.at[k],
                send_sem=send_sems.at[k], recv_sem=recv_sems.at[k],
                device_id=peer, device_id_type=pl.DeviceIdType.MESH)

        def tiny(k):
            return pltpu.make_async_remote_copy(
                src_ref=small_ref, dst_ref=recv_s_ref.at[k],
                send_sem=send_s_sems.at[k], recv_sem=recv_s_sems.at[k],
                device_id=_peer(pos, k), device_id_type=pl.DeviceIdType.MESH)

        own = pltpu.make_async_copy(parts_ref.at[me], recv_ref.at[0], local_sem)
        own.start()
        for k in range(1, N_DEV):
            tiny(k).start()
        for k in range(1, N_DEV):
            big(k).start()
        recv_s_ref[0] = small_ref[...]
        for k in range(1, N_DEV):
            tiny(k).wait_recv()
        acc = recv_s_ref[me]
        for a in range(1, N_DEV):
            acc = acc + recv_s_ref[lax.bitwise_xor(me, a)]
        row = lax.broadcasted_iota(jnp.int32, (R_SMALL, LANES), 0)
        gs = jnp.where(row == 8, acc + pltpu.roll(acc, 64, 1), acc)
        gs = jnp.where(row == ROW_LOSS, jnp.sum(acc, axis=1, keepdims=True) * (0.5 / D_MODEL), gs)
        gs_ref[...] = gs
        ds, nms, nvs = _adamw(ws_ref[...], gs, ms_ref[...], vs_ref[...])
        ds_ref[...] = ds
        nms_ref[...] = nms
        nvs_ref[...] = nvs

        own.wait()
        for k in range(1, N_DEV):
            big(k).wait_recv()

        def step(t, carry):
            rows = pl.ds(pl.multiple_of(t * chunk, chunk), chunk)
            g = recv_ref[0, rows, :]
            for k in range(1, N_DEV):
                g = g + recv_ref[k, rows, :]
            d, nm, nv = _adamw(w_ref[rows, :], g, m_ref[rows, :], v_ref[rows, :])
            g_ref[rows, :] = g
            d_ref[rows, :] = d
            nm_ref[rows, :] = nm
            nv_ref[rows, :] = nv
            return carry

        lax.fori_loop(0, n_chunks, step, 0)
        for k in range(1, N_DEV):
            tiny(k).wait_send()
            big(k).wait_send()

    vm = pl.BlockSpec(memory_space=pltpu.VMEM)
    big_shape = jax.ShapeDtypeStruct((R_PACK, LANES), F32)
    small_shape = jax.ShapeDtypeStruct((R_SMALL, LANES), F32)
    return pl.pallas_call(
        body, name="reduce_adamw",
        in_specs=[pl.BlockSpec(memory_space=pl.ANY)] + [vm] * 7,
        out_specs=[vm] * 8,
        out_shape=[big_shape] * 4 + [small_shape] * 4,
        scratch_shapes=[pltpu.VMEM((N_DEV, R_PACK, LANES), F32), pltpu.VMEM((N_DEV, R_SMALL, LANES), F32),
                        pltpu.SemaphoreType.DMA((N_DEV,)), pltpu.SemaphoreType.DMA((N_DEV,)),
                        pltpu.SemaphoreType.DMA((N_DEV,)), pltpu.SemaphoreType.DMA((N_DEV,)),
                        pltpu.SemaphoreType.DMA],
        compiler_params=pltpu.CompilerParams(vmem_limit_bytes=VMEM_LIMIT),
    )(parts, small, w_pk, m_pk, v_pk, w_s, m_s, v_s)


W_BLOCKS = ((D_MODEL, N_IN // N_DEV), (B_Q_RANK // N_DEV, 384), (B_KV_RANK, 768 // N_DEV), (D_MODEL // N_DEV, D_MODEL))
N_W = len(W_BLOCKS)


def _gather_blocks_call(blocks):
    def body(*refs):
        x_refs, out_refs, xb_refs = refs[0:N_W], refs[N_W:2 * N_W], refs[2 * N_W:3 * N_W]
        send_sems, recv_sems, local_sems = refs[3 * N_W:]
        x, y, c = _mesh_pos()
        me, sibling = (x, y, c), (x, y, 1 - c)
        chips = [(1 - x, y), (x, 1 - y), (1 - x, 1 - y)]
        for w in range(N_W):
            xb_refs[w][...] = x_refs[w][...].astype(BF16)

        def slot(w, p):
            return out_refs[w].at[_logical(p)]

        def copy(w, k, block, to, src=None):
            return pltpu.make_async_remote_copy(
                src_ref=slot(w, block) if src is None else src, dst_ref=slot(w, block),
                send_sem=send_sems.at[N_W * k + w], recv_sem=recv_sems.at[N_W * k + w],
                device_id=to, device_id_type=pl.DeviceIdType.MESH)

        mine = [pltpu.make_async_copy(xb_refs[w], slot(w, me), local_sems.at[w]) for w in range(N_W)]
        for cp in mine:
            cp.start()
        first = [copy(w, 0, me, sibling, src=xb_refs[w]) for w in range(N_W)]
        first += [copy(w, 1 + j, me, (*chip, c), src=xb_refs[w]) for j, chip in enumerate(chips) for w in range(N_W)]
        for cp in first:
            cp.start()
        passed = []
        for j, chip in enumerate(chips):
            for w in range(N_W):
                copy(w, 1 + j, (*chip, c), me).wait_recv()
                fwd = copy(w, 4 + j, (*chip, c), sibling)
                fwd.start()
                passed.append(fwd)
        for w in range(N_W):
            copy(w, 0, sibling, me).wait_recv()
        for j, chip in enumerate(chips):
            for w in range(N_W):
                copy(w, 4 + j, (*chip, 1 - c), me).wait_recv()
        for cp in first + passed:
            cp.wait_send()
        for cp in mine:
            cp.wait()

    vm = pl.BlockSpec(memory_space=pltpu.VMEM)
    return pl.pallas_call(
        body, name="gather_weights",
        out_shape=[jax.ShapeDtypeStruct((N_DEV,) + shp, BF16) for shp in W_BLOCKS],
        in_specs=[vm] * N_W, out_specs=[vm] * N_W,
        scratch_shapes=[pltpu.VMEM(shp, BF16) for shp in W_BLOCKS]
        + [pltpu.SemaphoreType.DMA((7 * N_W,)), pltpu.SemaphoreType.DMA((7 * N_W,)), pltpu.SemaphoreType.DMA((N_W,))],
        compiler_params=pltpu.CompilerParams(vmem_limit_bytes=VMEM_LIMIT),
    )(*blocks)


def _reduce_two_level_call(parts, small, w_blk, m_blk, v_blk, w_s, m_s, v_s):
    chunks = (32, 48, 64, 16)
    n_chip = N_DEV // 2

    def body(*refs):
        p_refs = refs[0:4]
        small_ref = refs[4]
        w_refs, m_refs, v_refs = refs[5:9], refs[9:13], refs[13:17]
        ws_ref, ms_ref, vs_ref = refs[17:20]
        g_refs, d_refs, nm_refs, nv_refs = refs[20:24], refs[24:28], refs[28:32], refs[32:36]
        gs_ref, ds_ref, nms_ref, nvs_ref = refs[36:40]
        ra_refs, rb_refs, st_refs = refs[40:44], refs[44:48], refs[48:52]
        recv_s_ref = refs[52]
        send_a, recv_a, send_b, recv_b, send_s_sems, recv_s_sems = refs[53:59]
        pos = _mesh_pos()
        x, y, c = pos
        me = _logical(pos)
        sibling = (x, y, 1 - c)

        def chip(j):
            return _flip(x, j & 1), _flip(y, (j >> 1) & 1)

        def to_sibling(w, j):
            return pltpu.make_async_remote_copy(
                src_ref=p_refs[w].at[_logical((*chip(j), 1 - c))], dst_ref=ra_refs[w].at[j],
                send_sem=send_a.at[N_W * j + w], recv_sem=recv_a.at[N_W * j + w],
                device_id=sibling, device_id_type=pl.DeviceIdType.MESH)

        def to_chip(w, j):
            return pltpu.make_async_remote_copy(
                src_ref=st_refs[w].at[j - 1], dst_ref=rb_refs[w].at[j - 1],
                send_sem=send_b.at[N_W * (j - 1) + w], recv_sem=recv_b.at[N_W * (j - 1) + w],
                device_id=(*chip(j), c), device_id_type=pl.DeviceIdType.MESH)

        def tiny(k):
            return pltpu.make_async_remote_copy(
                src_ref=small_ref, dst_ref=recv_s_ref.at[k],
                send_sem=send_s_sems.at[k], recv_sem=recv_s_sems.at[k],
                device_id=_peer(pos, k), device_id_type=pl.DeviceIdType.MESH)

        order = (0, 3, 2, 1)
        for j in (1, 2, 3, 0):
            for w in order:
                to_sibling(w, j).start()
        for k in range(1, N_DEV):
            tiny(k).start()

        for j in (1, 2, 3):
            d = _logical((*chip(j), c))
            for w in order:
                to_sibling(w, j).wait_recv()
                chunk = chunks[w]

                def pair(t, carry, w=w, j=j, d=d, chunk=chunk):
                    rows = pl.ds(pl.multiple_of(t * chunk, chunk), chunk)
                    s = p_refs[w][d, rows, :].astype(F32) + ra_refs[w][j, rows, :].astype(F32)
                    st_refs[w][j - 1, rows, :] = s.astype(BF16)
                    return carry

                lax.fori_loop(0, W_BLOCKS[w][0] // chunk, pair, 0)
                to_chip(w, j).start()

        recv_s_ref[0] = small_ref[...]
        for k in range(1, N_DEV):
            tiny(k).wait_recv()
        acc = recv_s_ref[me]
        for a in range(1, N_DEV):
            acc = acc + recv_s_ref[lax.bitwise_xor(me, a)]
        row = lax.broadcasted_iota(jnp.int32, (R_SMALL, LANES), 0)
        gs = jnp.where(row == 8, acc + pltpu.roll(acc, 64, 1), acc)
        gs = jnp.where(row == ROW_LOSS, jnp.sum(acc, axis=1, keepdims=True) * (0.5 / D_MODEL), gs)
        gs_ref[...] = gs
        ds, nms, nvs = _adamw(ws_ref[...], gs, ms_ref[...], vs_ref[...])
        ds_ref[...] = ds
        nms_ref[...] = nms
        nvs_ref[...] = nvs

        for w in (1, 2, 3, 0):
            to_sibling(w, 0).wait_recv()
            for j in (1, 2, 3):
                to_chip(w, j).wait_recv()
            chunk = chunks[w]

            def step(t, carry, w=w, chunk=chunk):
                rows = pl.ds(pl.multiple_of(t * chunk, chunk), chunk)
                g = p_refs[w][me, rows, :].astype(F32) + ra_refs[w][0, rows, :].astype(F32)
                for j in range(n_chip - 1):
                    g = g + rb_refs[w][j, rows, :].astype(F32)
                d, nm, nv = _adamw(w_refs[w][rows, :], g, m_refs[w][rows, :], v_refs[w][rows, :])
                g_refs[w][rows, :] = g
                d_refs[w][rows, :] = d
                nm_refs[w][rows, :] = nm
                nv_refs[w][rows, :] = nv
                return carry

            lax.fori_loop(0, W_BLOCKS[w][0] // chunk, step, 0)
        for k in range(1, N_DEV):
            tiny(k).wait_send()
        for w in range(N_W):
            for j in range(n_chip):
                to_sibling(w, j).wait_send()
            for j in (1, 2, 3):
                to_chip(w, j).wait_send()

    vm = pl.BlockSpec(memory_space=pltpu.VMEM)
    blk = [jax.ShapeDtypeStruct(shp, F32) for shp in W_BLOCKS]
    small_shape = jax.ShapeDtypeStruct((R_SMALL, LANES), F32)
    return pl.pallas_call(
        body, name="reduce_adamw",
        in_specs=[vm] * 20, out_specs=[vm] * 20,
        out_shape=blk * 4 + [small_shape] * 4,
        scratch_shapes=[pltpu.VMEM((n_chip,) + shp, BF16) for shp in W_BLOCKS]
        + [pltpu.VMEM((n_chip - 1,) + shp, BF16) for shp in W_BLOCKS] * 2
        + [pltpu.VMEM((N_DEV, R_SMALL, LANES), F32),
           pltpu.SemaphoreType.DMA((n_chip * N_W,)), pltpu.SemaphoreType.DMA((n_chip * N_W,)),
           pltpu.SemaphoreType.DMA(((n_chip - 1) * N_W,)), pltpu.SemaphoreType.DMA(((n_chip - 1) * N_W,)),
           pltpu.SemaphoreType.DMA((N_DEV,)), pltpu.SemaphoreType.DMA((N_DEV,))],
        compiler_params=pltpu.CompilerParams(vmem_limit_bytes=VMEM_LIMIT),
    )(*parts, small, *w_blk, *m_blk, *v_blk, w_s, m_s, v_s)


def _reduce_blocks_call(parts, small, w_blk, m_blk, v_blk, w_s, m_s, v_s):
    chunks = (32, 48, 64, 16)

    def body(*refs):
        p_refs = refs[0:4]
        small_ref = refs[4]
        w_refs, m_refs, v_refs = refs[5:9], refs[9:13], refs[13:17]
        ws_ref, ms_ref, vs_ref = refs[17:20]
        g_refs, d_refs, nm_refs, nv_refs = refs[20:24], refs[24:28], refs[28:32], refs[32:36]
        gs_ref, ds_ref, nms_ref, nvs_ref = refs[36:40]
        r_refs = refs[40:44]
        recv_s_ref = refs[44]
        send_sems, recv_sems, send_s_sems, recv_s_sems, local_sems = refs[45:50]
        pos = _mesh_pos()
        me = _logical(pos)

        def big(w, k):
            peer = _peer(pos, k)
            return pltpu.make_async_remote_copy(
                src_ref=p_refs[w].at[_logical(peer)], dst_ref=r_refs[w].at[k],
                send_sem=send_sems.at[N_W * k + w], recv_sem=recv_sems.at[N_W * k + w],
                device_id=peer, device_id_type=pl.DeviceIdType.MESH)

        def tiny(k):
            return pltpu.make_async_remote_copy(
                src_ref=small_ref, dst_ref=recv_s_ref.at[k],
                send_sem=send_s_sems.at[k], recv_sem=recv_s_sems.at[k],
                device_id=_peer(pos, k), device_id_type=pl.DeviceIdType.MESH)

        own = [pltpu.make_async_copy(p_refs[w].at[me], r_refs[w].at[0], local_sems.at[w]) for w in range(N_W)]
        for cp in own:
            cp.start()
        for k in range(1, N_DEV):
            tiny(k).start()
        for k in range(1, N_DEV):
            for w in range(N_W):
                big(w, k).start()
        recv_s_ref[0] = small_ref[...]
        for k in range(1, N_DEV):
            tiny(k).wait_recv()
        acc = recv_s_ref[me]
        for a in range(1, N_DEV):
            acc = acc + recv_s_ref[lax.bitwise_xor(me, a)]
        row = lax.broadcasted_iota(jnp.int32, (R_SMALL, LANES), 0)
        gs = jnp.where(row == 8, acc + pltpu.roll(acc, 64, 1), acc)
        gs = jnp.where(row == ROW_LOSS, jnp.sum(acc, axis=1, keepdims=True) * (0.5 / D_MODEL), gs)
        gs_ref[...] = gs
        ds, nms, nvs = _adamw(ws_ref[...], gs, ms_ref[...], vs_ref[...])
        ds_ref[...] = ds
        nms_ref[...] = nms
        nvs_ref[...] = nvs

        for w in (3, 2, 1, 0):
            own[w].wait()
            for k in range(1, N_DEV):
                big(w, k).wait_recv()
            chunk = chunks[w]

            def step(t, carry, w=w, chunk=chunk):
                rows = pl.ds(pl.multiple_of(t * chunk, chunk), chunk)
                g = r_refs[w][0, rows, :].astype(F32)
                for k in range(1, N_DEV):
                    g = g + r_refs[w][k, rows, :].astype(F32)
                d, nm, nv = _adamw(w_refs[w][rows, :], g, m_refs[w][rows, :], v_refs[w][rows, :])
                g_refs[w][rows, :] = g
                d_refs[w][rows, :] = d
                nm_refs[w][rows, :] = nm
                nv_refs[w][rows, :] = nv
                return carry

            lax.fori_loop(0, W_BLOCKS[w][0] // chunk, step, 0)
        for k in range(1, N_DEV):
            tiny(k).wait_send()
            for w in range(N_W):
                big(w, k).wait_send()

    vm = pl.BlockSpec(memory_space=pltpu.VMEM)
    blk = [jax.ShapeDtypeStruct(shp, F32) for shp in W_BLOCKS]
    small_shape = jax.ShapeDtypeStruct((R_SMALL, LANES), F32)
    return pl.pallas_call(
        body, name="reduce_adamw",
        in_specs=[pl.BlockSpec(memory_space=pl.ANY)] * N_W + [vm] * 16,
        out_specs=[vm] * 20,
        out_shape=blk * 4 + [small_shape] * 4,
        scratch_shapes=[pltpu.VMEM((N_DEV,) + shp, BF16) for shp in W_BLOCKS]
        + [pltpu.VMEM((N_DEV, R_SMALL, LANES), F32),
           pltpu.SemaphoreType.DMA((N_DEV * N_W,)), pltpu.SemaphoreType.DMA((N_DEV * N_W,)),
           pltpu.SemaphoreType.DMA((N_DEV,)), pltpu.SemaphoreType.DMA((N_DEV,)), pltpu.SemaphoreType.DMA((N_W,))],
        compiler_params=pltpu.CompilerParams(vmem_limit_bytes=VMEM_LIMIT),
    )(*parts, small, *w_blk, *m_blk, *v_blk, w_s, m_s, v_s)


def _pack_shard(w_in, w_uq, w_ukv, w_out):
    return jnp.concatenate([a.reshape(-1, LANES) for a in (w_in, w_uq, w_ukv, w_out)], axis=0)


def _unpack_shard(p):
    w_in = p[0:R_WIN].reshape(1, D_MODEL, N_IN // N_DEV)
    w_uq = p[R_WIN:R_WIN + R_WUQ].reshape(1, B_Q_RANK // N_DEV, 384)
    w_ukv = p[R_WIN + R_WUQ:R_WIN + R_WUQ + R_WUKV].reshape(1, B_KV_RANK, 768 // N_DEV)
    w_out = p[R_WIN + R_WUQ + R_WUKV:].reshape(1, D_MODEL // N_DEV, D_MODEL)
    return w_in, w_uq, w_ukv, w_out


def _pack_small(norm_in, a_q, a_k, b_cq, b_ckv, b_q, b_k):
    def row(v):
        return jnp.pad(v.reshape(1, -1), ((0, 0), (0, LANES - v.size)))
    rows = [norm_in.reshape(8, LANES), row(a_q), row(a_k), b_cq.reshape(3, LANES), b_ckv.reshape(2, LANES),
            row(b_q), row(b_k), jnp.zeros((R_SMALL - 17, LANES), F32)]
    return jnp.concatenate(rows, axis=0)


def _unpack_small(s):
    return (s[0:8].reshape(1, D_MODEL), s[8:9, :64], s[9:10, :64], s[10:13].reshape(1, B_Q_RANK),
            s[13:15].reshape(1, B_KV_RANK), s[15:16, :B_QK_DIM], s[16:17, :B_QK_DIM])


def _full_weights(gathered):
    g = gathered
    w_in = g[:, 0:R_WIN].reshape(N_DEV, D_MODEL, N_IN // N_DEV).transpose(1, 0, 2).reshape(D_MODEL, N_IN)
    w_uq = g[:, R_WIN:R_WIN + R_WUQ].reshape(B_Q_RANK, 384)
    w_ukv = g[:, R_WIN + R_WUQ:R_WIN + R_WUQ + R_WUKV].reshape(N_DEV, B_KV_RANK, 768 // N_DEV)
    w_ukv = w_ukv.transpose(1, 0, 2).reshape(B_KV_RANK, 768)
    w_out = g[:, R_WIN + R_WUQ + R_WUKV:].reshape(D_MODEL, D_MODEL)
    k0, k1 = w_in[:, 512:576], w_in[:, 576:640]
    v0, v1 = w_in[:, 640:704], w_in[:, 704:768]
    kr = w_in[:, 1920:1952]
    z64 = jnp.zeros((D_MODEL, 64), BF16)
    z32 = jnp.zeros((D_MODEL, 32), BF16)
    kr_blk = jnp.concatenate([z64, kr, z32], axis=1)
    w_wide = jnp.concatenate([w_in[:, 0:512], k0, k0, k1, k1, v0, v0, v1, v1, w_in[:, 768:1280], w_in[:, 1280:1664],
                              w_in[:, 1664:1920], kr_blk, kr_blk, kr_blk, kr_blk, w_in[:, 1952:2464]], axis=1)
    wuq = jnp.pad(w_uq.reshape(B_Q_RANK, B_HEADS, B_QK_DIM), ((0, 0), (0, 0), (0, LANES - B_QK_DIM)))
    wuq = wuq.reshape(B_Q_RANK, 512)
    ukv = w_ukv.reshape(B_KV_RANK, B_HEADS, B_NOPE_DIM + B_V_DIM)
    wuk = jnp.pad(ukv[:, :, :B_NOPE_DIM], ((0, 0), (0, 0), (0, LANES - B_NOPE_DIM))).reshape(B_KV_RANK, 512)
    wuv = ukv[:, :, B_NOPE_DIM:].reshape(B_KV_RANK, 512)
    return w_wide, wuq, wuk, wuv, w_out


def _narrow_grads(dw_wide, dwuq, dwuk, dwuv):
    dw_in = jnp.concatenate([
        dw_wide[:, 0:512], dw_wide[:, 512:576], dw_wide[:, 640:704], dw_wide[:, 768:832], dw_wide[:, 896:960],
        dw_wide[:, 1024:1536], dw_wide[:, 1536:1920], dw_wide[:, 1920:2176],
        dw_wide[:, O_KR + 64:O_KR + 96], dw_wide[:, 2688:3200]], axis=1)
    dw_uq = dwuq.reshape(B_Q_RANK, B_HEADS, LANES)[:, :, :B_QK_DIM].reshape(B_Q_RANK, 384)
    dk = dwuk.reshape(B_KV_RANK, B_HEADS, LANES)[:, :, :B_NOPE_DIM]
    dv = dwuv.reshape(B_KV_RANK, B_HEADS, B_V_DIM)
    dw_ukv = jnp.concatenate([dk, dv], axis=2).reshape(B_KV_RANK, 768)
    return dw_in, dw_uq, dw_ukv


C_IN = N_IN // N_DEV
_RUNS = ((0, 512, O_QA), (512, 576, O_KA), (576, 640, O_KA + 128), (640, 704, O_VA), (704, 768, O_VA + 128),
         (768, 1280, O_GA), (1280, 1664, O_CQ), (1664, 1920, O_CKV), (1920, 1952, O_KR + 64), (1952, 2464, O_GB))


def _in_cols(g_in, lo, hi):
    out = []
    for d in range(N_DEV):
        a, b = max(lo, C_IN * d), min(hi, C_IN * (d + 1))
        if a < b:
            out.append(g_in[d][:, a - C_IN * d:b - C_IN * d])
    return out


def _widen_weights(g_in, g_uq, g_ukv, g_out):
    z64 = jnp.zeros((D_MODEL, 64), BF16)
    z32 = jnp.zeros((D_MODEL, 32), BF16)
    k0, k1 = _in_cols(g_in, 512, 576), _in_cols(g_in, 576, 640)
    v0, v1 = _in_cols(g_in, 640, 704), _in_cols(g_in, 704, 768)
    kr_blk = [z64] + _in_cols(g_in, 1920, 1952) + [z32]
    w_wide = jnp.concatenate(
        _in_cols(g_in, 0, 512) + k0 + k0 + k1 + k1 + v0 + v0 + v1 + v1 + _in_cols(g_in, 768, 1920)
        + kr_blk * B_HEADS + _in_cols(g_in, 1952, 2464), axis=1)
    w_uq = g_uq.reshape(B_Q_RANK, 384)
    wuq = jnp.pad(w_uq.reshape(B_Q_RANK, B_HEADS, B_QK_DIM), ((0, 0), (0, 0), (0, LANES - B_QK_DIM)))
    wuq = wuq.reshape(B_Q_RANK, 512)
    ukv = g_ukv.transpose(1, 0, 2).reshape(B_KV_RANK, B_HEADS, B_NOPE_DIM + B_V_DIM)
    wuk = jnp.pad(ukv[:, :, :B_NOPE_DIM], ((0, 0), (0, 0), (0, LANES - B_NOPE_DIM))).reshape(B_KV_RANK, 512)
    wuv = ukv[:, :, B_NOPE_DIM:].reshape(B_KV_RANK, 512)
    return w_wide, wuq, wuk, wuv, g_out.reshape(D_MODEL, D_MODEL)


def _grad_blocks(dw_wide, dwuq, dwuk, dwuv, dw_out):
    blocks = []
    for d in range(N_DEV):
        pieces = []
        for lo, hi, wide in _RUNS:
            a, b = max(lo, C_IN * d), min(hi, C_IN * (d + 1))
            if a < b:
                pieces.append(dw_wide[:, wide + a - lo:wide + b - lo])
        blocks.append(jnp.concatenate(pieces, axis=1))
    p_in = jnp.stack(blocks).astype(BF16)
    dw_uq = dwuq.reshape(B_Q_RANK, B_HEADS, LANES)[:, :, :B_QK_DIM].reshape(N_DEV, B_Q_RANK // N_DEV, 384)
    dk = dwuk.reshape(B_KV_RANK, B_HEADS, LANES)[:, :, :B_NOPE_DIM]
    dv = dwuv.reshape(B_KV_RANK, B_HEADS, B_V_DIM)
    dw_ukv = jnp.concatenate([dk, dv], axis=2).reshape(B_KV_RANK, N_DEV, 768 // N_DEV).transpose(1, 0, 2)
    return (p_in, dw_uq.astype(BF16), dw_ukv.astype(BF16),
            dw_out.reshape(N_DEV, D_MODEL // N_DEV, D_MODEL).astype(BF16))


def _rope_tables(s_len):
    row = jnp.arange(s_len // GRID_W, dtype=F32)
    col = jnp.arange(GRID_W, dtype=F32)

    def parts(dim):
        half = dim // 2
        inv = 1.0 / (ROPE_THETA ** (jnp.arange(0, half, 2, dtype=F32) / half))
        ar, ac = row[:, None] * inv[None, :], col[:, None] * inv[None, :]
        zr, zc = jnp.zeros_like(ar), jnp.zeros_like(ac)
        cos_c = jnp.concatenate([zc, zc, jnp.cos(ac), jnp.cos(ac)], axis=1)
        cos_r = jnp.concatenate([jnp.cos(ar), jnp.cos(ar), zr, zr], axis=1)
        sin_c = jnp.concatenate([zc, zc, -jnp.sin(ac), jnp.sin(ac)], axis=1)
        sin_r = jnp.concatenate([-jnp.sin(ar), jnp.sin(ar), zr, zr], axis=1)
        return cos_c, cos_r, sin_c, sin_r

    tabs = [jnp.tile(t, (1, 2)) for t in parts(A_HEAD_DIM)]
    for n, t in enumerate(parts(B_ROPE_DIM)):
        lead = jnp.full((t.shape[0], B_NOPE_DIM), 1.0 if n == 0 else 0.0, F32)
        tail = jnp.full((t.shape[0], LANES - B_QK_DIM), 1.0 if n == 0 else 0.0, F32)
        tabs.append(jnp.concatenate([lead, t, tail], axis=1))
    return tuple(tabs)


def _lane_major(a, step):
    return a[:, ::step].T[:, None, :]


def kernel(x, norm_in, w_in, a_q_norm, a_k_norm, b_cq_norm, b_ckv_norm, w_uq, w_ukv, b_q_norm, b_k_norm, w_out, loss_target, m_norm_in, m_w_in, m_a_q_norm, m_a_k_norm, m_b_cq_norm, m_b_ckv_norm, m_w_uq, m_w_ukv, m_b_q_norm, m_b_k_norm, m_w_out, v_norm_in, v_w_in, v_a_q_norm, v_a_k_norm, v_b_cq_norm, v_b_ckv_norm, v_w_uq, v_w_ukv, v_b_q_norm, v_b_k_norm, v_w_out):
    s_len = x.shape[1]
    tm = min(256, s_len)
    tq, tk = min(512, s_len), min(2048, s_len)
    ftq, ftk = min(256, s_len), min(1024, s_len)
    x2 = x.reshape(s_len, D_MODEL)
    t2 = loss_target.reshape(s_len, D_MODEL)

    w_blk = (w_in[0], w_uq[0], w_ukv[0], w_out[0])
    w_wide, wuq, wuk, wuv, wout = _widen_weights(*_gather_blocks_call(w_blk))

    def dup(v, pad_to=None):
        v = v.reshape(1, -1)
        if pad_to is None:
            return jnp.concatenate([v, v], axis=1)
        return jnp.pad(v, ((0, 0), (0, pad_to - v.shape[1])))

    g64 = jnp.asarray(np.kron(np.eye(2), np.ones((64, 64))), dtype=BF16)
    consts = dict(gin=norm_in, w=w_wide, wuq=wuq, wuk=wuk, wuv=wuv, gq=dup(a_q_norm), gk=dup(a_k_norm),
                  gcq=b_cq_norm, gckv=b_ckv_norm, gqb=dup(b_q_norm, LANES), gkb=dup(b_k_norm, LANES), g64=g64)
    tabs = _rope_tables(s_len)

    qa, ka, va, va_t, ga, qb, kb, vb, vb_t, gb, raw, xnb = _pre_fwd_call(x2, consts, tabs, tm)
    oa, lse_a_t = _attn_fwd_t_call(qa, ka, va_t, groups=A_KV_HEADS, sub=4, masked=True, scale=None, tq=ftq, tk=ftk,
                                   name="attn_fwd_a")
    ob, lse_b_t = _attn_fwd_t_call(qb, kb, vb_t, groups=1, sub=B_HEADS, masked=False, scale=None, tq=ftq,
                                   tk=ftk, name="attn_fwd_b")
    dh, doa, dob, dga, dgb, dw_out, loss_row = _out_call(x2, t2, oa, ob, ga, gb, wout, min(512, s_len))

    dqa, dka_t, dva_t = _attn_bwd_q_call(qa, ka, va, doa, oa, lse_a_t, groups=A_KV_HEADS, sub=4, masked=True,
                                         tq=tq, tk=tk, ck=min(512, s_len), name="attn_bwd_a")
    dqb, dkb_t, dvb_t = _attn_bwd_q_call(qb, kb, vb, dob, ob, lse_b_t, groups=2, sub=2, masked=False,
                                         tq=tq, tk=tk, ck=min(256, s_len), name="attn_bwd_b")
    dx, dproj_b, dwuq, dwuk, dwuv, small = _pre_bwd_call(
        x2, raw, dh, dqa, dka_t, dva_t, dga, dqb, dkb_t, dvb_t, dgb, consts, tabs, tm)
    dw_wide = _dw_in_call(xnb, dproj_b, min(1024, s_len), N_WIDE)

    parts = _grad_blocks(dw_wide, dwuq, dwuk, dwuv, dw_out)
    small = jnp.concatenate([small[:ROW_LOSS], loss_row, small[ROW_LOSS + 1:]], axis=0)

    m_blk = (m_w_in[0], m_w_uq[0], m_w_ukv[0], m_w_out[0])
    v_blk = (v_w_in[0], v_w_uq[0], v_w_ukv[0], v_w_out[0])
    w_s = _pack_small(norm_in, a_q_norm, a_k_norm, b_cq_norm, b_ckv_norm, b_q_norm, b_k_norm)
    m_s = _pack_small(m_norm_in, m_a_q_norm, m_a_k_norm, m_b_cq_norm, m_b_ckv_norm, m_b_q_norm, m_b_k_norm)
    v_s = _pack_small(v_norm_in, v_a_q_norm, v_a_k_norm, v_b_cq_norm, v_b_ckv_norm, v_b_q_norm, v_b_k_norm)
    res = _reduce_two_level_call(parts, small, w_blk, m_blk, v_blk, w_s, m_s, v_s)

    def leaves(blocks, sm):
        wi, uq, ukv, wo = [b[None] for b in blocks]
        n_in, aq, ak, bcq, bckv, bq, bk = _unpack_small(sm)
        return [n_in, wi, aq, ak, bcq, bckv, uq, ukv, bq, bk, wo]

    g_s = res[16]
    loss = g_s[ROW_LOSS, 0]
    grad_x = dx.reshape(1, s_len, D_MODEL)
    return (loss, grad_x, *leaves(res[0:4], res[16]), *leaves(res[4:8], res[17]), *leaves(res[8:12], res[18]),
            *leaves(res[12:16], res[19]))
```

```python
import functools

import numpy as np
import jax
import jax.numpy as jnp
from jax import lax
from jax.experimental import pallas as pl
from jax.experimental.pallas import tpu as pltpu

F32 = jnp.float32
BF16 = jnp.bfloat16

D_MODEL = 1024
GRID_W = 64
ROPE_THETA = 10000.0
EPS = 1e-6
A_HEAD_DIM = 64
A_HEADS = 8
A_KV_HEADS = 2
B_HEADS = 4
B_NOPE_DIM = 64
B_ROPE_DIM = 32
B_QK_DIM = 96
B_V_DIM = 128
B_Q_RANK = 384
B_KV_RANK = 256
N_IN = 2464
N_DEV = 8

ADAM_LR = 0.001
ADAM_B1 = 0.9
ADAM_B2 = 0.999
ADAM_EPS = 1e-08
ADAM_WD = 0.01
ADAM_STEP = 10

QA_SCALE = 0.125
QB_SCALE = 1.0 / float(np.sqrt(B_QK_DIM))

LANES = 128
O_QA, O_KA, O_VA, O_GA, O_CQ, O_CKV, O_KR, O_GB, N_WIDE = 0, 512, 768, 1024, 1536, 1920, 2176, 2688, 3200
R_QA, R_KA, R_CQ, R_CKV, R_KR, R_WIDTH = 0, 512, 768, 1152, 1408, 1920

R_WIN = D_MODEL * (N_IN // N_DEV) // LANES
R_WUQ = (B_Q_RANK // N_DEV) * 384 // LANES
R_WUKV = B_KV_RANK * (768 // N_DEV) // LANES
R_WOUT = (D_MODEL // N_DEV) * D_MODEL // LANES
R_PACK = R_WIN + R_WUQ + R_WUKV + R_WOUT
R_SMALL = 24
ROW_LOSS = 17

VMEM_LIMIT = 56 * 1024 * 1024

NT = (((1,), (1,)), ((), ()))
TN = (((0,), (0,)), ((), ()))


def _dot(a, b):
    return jnp.dot(a, b, preferred_element_type=F32)


def _dot_nt(a, b):
    return lax.dot_general(a, b, NT, preferred_element_type=F32)


def _dot_tn(a, b):
    return lax.dot_general(a, b, TN, preferred_element_type=F32)


def _params(sem=None):
    return pltpu.CompilerParams(dimension_semantics=sem, vmem_limit_bytes=VMEM_LIMIT)


def _full(shape):
    nd = len(shape)
    return pl.BlockSpec(shape, lambda *_: (0,) * nd)


def _swap_sel(rows, shift):
    lane = lax.broadcasted_iota(jnp.int32, (rows, LANES), 1)
    return pltpu.roll(lane, shift, 1) == (lane ^ shift)


def _swap(x, shift, sel):
    return jnp.where(sel, pltpu.roll(x, shift, 1), pltpu.roll(x, LANES - shift, 1))


def _group_sum64(x, g64):
    hi = x.astype(BF16)
    lo = (x - hi.astype(F32)).astype(BF16)
    return _dot(hi, g64) + _dot(lo, g64)


def _row_sum(x):
    return jnp.sum(x, axis=-1, keepdims=True)


def _col_fwd(xs, msum, denom, gain, cos, sin, shift, sel):
    r = lax.rsqrt(msum(xs * xs) * (1.0 / denom) + EPS)
    xh = xs * r
    n = xh * gain
    return n * cos + _swap(n, shift, sel) * sin, xh, r


def _col_bwd(d_out, xh, r, msum, denom, gain, cos, sin, shift, sel):
    dn = d_out * cos + _swap(d_out * sin, shift, sel)
    dgain = jnp.sum(dn * xh, axis=0, keepdims=True)
    dxh = dn * gain
    dx = r * (dxh - xh * (msum(dxh * xh) * (1.0 / denom)))
    return dx, dgain


def _rms_fwd(x, gain):
    r = lax.rsqrt(jnp.mean(x * x, axis=-1, keepdims=True) + EPS)
    xh = x * r
    return xh * gain, xh, r


def _rms_bwd(dy, xh, r, gain):
    dgain = jnp.sum(dy * xh, axis=0, keepdims=True)
    dxh = dy * gain
    dx = r * (dxh - xh * jnp.mean(dxh * xh, axis=-1, keepdims=True))
    return dx, dgain


def _pre_forward(x, gin, w, wuq, wuk, wuv, gq, gk, gcq, gckv, gqb, gkb, ca, sa, cb, sb, g64, tm, raw=None):
    sel16 = _swap_sel(tm, 16)
    sel8 = _swap_sel(tm, 8)
    xn, xh0, r0 = _rms_fwd(x, gin)
    xnb = xn.astype(BF16)
    proj = None
    if raw is None:
        proj = _dot(xnb, w)
        raw = jnp.concatenate([proj[:, O_QA:O_QA + 512], proj[:, O_KA:O_KA + 256], proj[:, O_CQ:O_CQ + B_Q_RANK],
                               proj[:, O_CKV:O_CKV + B_KV_RANK], proj[:, O_KR:O_KR + 512]], axis=1)
    gs64 = functools.partial(_group_sum64, g64=g64)
    qa = [_col_fwd(raw[:, R_QA + LANES * s:R_QA + LANES * (s + 1)], gs64, 64.0, gq, ca, sa, 16, sel16)
          for s in range(4)]
    ka = [_col_fwd(raw[:, R_KA + LANES * s:R_KA + LANES * (s + 1)], _row_sum, 128.0, gk, ca, sa, 16, sel16)
          for s in range(2)]
    cq, cqh, rcq = _rms_fwd(raw[:, R_CQ:R_CQ + B_Q_RANK], gcq)
    cqb = cq.astype(BF16)
    qb_raw = _dot(cqb, wuq)
    qb = [_col_fwd(qb_raw[:, LANES * h:LANES * (h + 1)], _row_sum, float(B_QK_DIM), gqb, cb, sb, 8, sel8)
          for h in range(B_HEADS)]
    ckv, ckvh, rckv = _rms_fwd(raw[:, R_CKV:R_CKV + B_KV_RANK], gckv)
    ckvb = ckv.astype(BF16)
    kb_raw = _dot(ckvb, wuk) + raw[:, R_KR:R_KR + 512]
    vb = _dot(ckvb, wuv)
    kb = [_col_fwd(kb_raw[:, LANES * h:LANES * (h + 1)], _row_sum, float(B_QK_DIM), gkb, cb, sb, 8, sel8)
          for h in range(B_HEADS)]
    return dict(xh0=xh0, r0=r0, xnb=xnb, proj=proj, raw=raw, qa=qa, ka=ka, cqh=cqh, rcq=rcq, cqb=cqb, qb=qb,
                ckvh=ckvh, rckv=rckv, ckvb=ckvb, kb=kb, vb=vb, sel16=sel16, sel8=sel8, gs64=gs64)


def _rope_tiles(tab_refs, i, tm):
    per_tile = tm // GRID_W
    out = []
    for t in range(4):
        col_ref, row_ref = tab_refs[2 * t], tab_refs[2 * t + 1]
        col = col_ref[...]
        out.append(jnp.concatenate([col + row_ref[pl.ds(i * per_tile + b, 1), :] for b in range(per_tile)], axis=0))
    return out


_PRE_IN_NAMES = ("gin", "w", "wuq", "wuk", "wuv", "gq", "gk", "gcq", "gckv", "gqb", "gkb", "g64")


def _pre_const_specs(consts):
    return [_full(consts[n].shape) for n in _PRE_IN_NAMES]


def _pre_fwd_call(x, consts, tabs, tm):
    s_len = x.shape[0]
    ts = min(256, tm)

    def body(x_ref, *refs):
        tab_refs, refs = refs[:8], refs[8:]
        (gin_ref, w_ref, wuq_ref, wuk_ref, wuv_ref, gq_ref, gk_ref, gcq_ref, gckv_ref, gqb_ref, gkb_ref, g64_ref,
         qa_ref, ka_ref, va_ref, vat_ref, ga_ref, qb_ref, kb_ref, vb_ref, vbt_ref, gb_ref, raw_ref, xnb_ref) = refs
        for part in range(tm // ts):
            r = slice(ts * part, ts * (part + 1))
            ca, sa, cb, sb = _rope_tiles(tab_refs, pl.program_id(0) * (tm // ts) + part, ts)
            f = _pre_forward(x_ref[r, :], gin_ref[...], w_ref[...], wuq_ref[...], wuk_ref[...], wuv_ref[...],
                             gq_ref[...], gk_ref[...], gcq_ref[...], gckv_ref[...], gqb_ref[...], gkb_ref[...],
                             ca, sa, cb, sb, g64_ref[...], ts)
            proj = f["proj"]
            raw_ref[r, :] = f["raw"]
            xnb_ref[r, :] = f["xnb"]
            for s in range(4):
                qa_ref[r, LANES * s:LANES * (s + 1)] = (f["qa"][s][0] * QA_SCALE).astype(BF16)
            for s in range(2):
                ka_ref[r, LANES * s:LANES * (s + 1)] = f["ka"][s][0].astype(BF16)
            va = proj[:, O_VA:O_VA + 256]
            va_ref[r, :] = va.astype(BF16)
            vat_ref[:, r] = va.T.astype(BF16)
            ga_ref[r, :] = proj[:, O_GA:O_GA + 512]
            for h in range(B_HEADS):
                qb_ref[r, LANES * h:LANES * (h + 1)] = (f["qb"][h][0] * QB_SCALE).astype(BF16)
                kb_ref[r, LANES * h:LANES * (h + 1)] = f["kb"][h][0].astype(BF16)
            vb_ref[r, :] = f["vb"].astype(BF16)
            vbt_ref[:, r] = f["vb"].T.astype(BF16)
            gb_ref[r, :] = proj[:, O_GB:O_GB + 512]

    def rows(width):
        return pl.BlockSpec((tm, width), lambda i: (i, 0))

    def cols(height):
        return pl.BlockSpec((height, tm), lambda i: (0, i))

    outs = [((s_len, 512), BF16, rows(512)), ((s_len, 256), BF16, rows(256)), ((s_len, 256), BF16, rows(256)),
            ((256, s_len), BF16, cols(256)), ((s_len, 512), F32, rows(512)), ((s_len, 512), BF16, rows(512)),
            ((s_len, 512), BF16, rows(512)), ((s_len, 512), BF16, rows(512)), ((512, s_len), BF16, cols(512)),
            ((s_len, 512), F32, rows(512)), ((s_len, R_WIDTH), F32, rows(R_WIDTH)),
            ((s_len, D_MODEL), BF16, rows(D_MODEL))]
    return pl.pallas_call(
        body, name="pre_fwd", grid=(s_len // tm,),
        in_specs=[rows(D_MODEL)] + [_full(t.shape) for t in tabs] + _pre_const_specs(consts),
        out_specs=[sp for _, _, sp in outs],
        out_shape=[jax.ShapeDtypeStruct(sh, dt) for sh, dt, _ in outs],
        compiler_params=_params(("arbitrary",)),
    )(x, *tabs, *[consts[n] for n in _PRE_IN_NAMES])


def _head_masks(rows):
    lane = lax.broadcasted_iota(jnp.int32, (rows, LANES), 1)
    return lane < 64, lane >= 64


def _lane_fold(x, op):
    out = x[:, 0:LANES]
    for c in range(1, x.shape[1] // LANES):
        out = op(out, x[:, LANES * c:LANES * (c + 1)])
    return out


def _row_fold(x, op):
    return op(x.reshape(x.shape[0] // 8, 8, x.shape[1]), axis=0)


def _attn_fwd_t_call(q, k, vt, *, groups, sub, masked, scale, tq, tk, name):
    s_len = q.shape[0]
    qw = LANES * (sub // 2 if masked else sub)
    kvw = LANES if masked else LANES * sub
    n_c = s_len // tk
    kv_mode = pl.Buffered(1) if groups == 1 else None

    def body(q_ref, k_ref, vt_ref, o_ref, lse_ref, s_sc):
        keep = _head_masks(tq) if masked else None

        def kv_of(hh):
            return slice(0, LANES) if masked else slice(LANES * hh, LANES * (hh + 1))

        def q_of(hh):
            if not masked:
                return q_ref[:, LANES * hh:LANES * (hh + 1)]
            qp = q_ref[:, LANES * (hh // 2):LANES * (hh // 2 + 1)]
            return jnp.where(keep[hh % 2], qp, jnp.zeros_like(qp))

        def scores(hh, qm, c, mx):
            s_t = _dot_nt(k_ref[tk * c:tk * (c + 1), kv_of(hh)], qm)
            if scale is not None:
                s_t = s_t * scale
            s_sc[hh % 2, c] = s_t
            return jnp.maximum(mx, _row_fold(s_t, jnp.max))

        neg = jnp.full((8, tq), -jnp.inf, F32)
        qm_next = q_of(0)
        mx_next = neg
        for c in range(n_c):
            mx_next = scores(0, qm_next, c, mx_next)
        outs = []
        for hh in range(sub):
            m = jnp.max(mx_next, axis=0, keepdims=True)
            if hh + 1 < sub:
                qm_next = q_of(hh + 1)
                mx_next = neg
            lsum = jnp.zeros((8, tq), F32)
            acc = jnp.zeros((LANES, tq), F32)
            for c in range(n_c):
                if hh + 1 < sub:
                    mx_next = scores(hh + 1, qm_next, c, mx_next)
                p_t = jnp.exp(s_sc[hh % 2, c] - m)
                lsum = lsum + _row_fold(p_t, jnp.sum)
                acc = acc + _dot(vt_ref[kv_of(hh), tk * c:tk * (c + 1)], p_t.astype(BF16))
            l = jnp.sum(lsum, axis=0, keepdims=True)
            outs.append((acc / l).T)
            lse_ref[hh] = m + jnp.log(l)
        if masked:
            for pr in range(sub // 2):
                o_ref[:, LANES * pr:LANES * (pr + 1)] = jnp.where(keep[0], outs[2 * pr], outs[2 * pr + 1])
        else:
            for hh in range(sub):
                o_ref[:, LANES * hh:LANES * (hh + 1)] = outs[hh]

    return pl.pallas_call(
        body, name=name, grid=(groups, s_len // tq),
        in_specs=[pl.BlockSpec((tq, qw), lambda g, i: (i, g)),
                  pl.BlockSpec((s_len, kvw), lambda g, i: (0, g), pipeline_mode=kv_mode),
                  pl.BlockSpec((kvw, s_len), lambda g, i: (g, 0), pipeline_mode=kv_mode)],
        out_specs=[pl.BlockSpec((tq, qw), lambda g, i: (i, g)),
                   pl.BlockSpec((sub, 1, tq), lambda g, i: (g, 0, i))],
        out_shape=[jax.ShapeDtypeStruct((s_len, groups * qw), F32),
                   jax.ShapeDtypeStruct((groups * sub, 1, s_len), F32)],
        scratch_shapes=[pltpu.VMEM((min(sub, 2), n_c, tk, tq), F32)],
        compiler_params=_params(("arbitrary", "arbitrary")),
    )(q, k, vt)


def _attn_fwd_call(q, k, v, *, groups, sub, masked, scale, tq, tk, name):
    s_len = q.shape[0]
    qw = LANES * (sub // 2 if masked else sub)
    n_c = s_len // tk
    log2e = float(np.log2(np.e))
    mul = log2e if scale is None else scale * log2e

    def body(q_ref, k_ref, v_ref, o_ref, lse_ref, s_sc):
        keep = _head_masks(tq) if masked else None
        outs = []
        for hh in range(sub):
            if masked:
                qp = q_ref[:, LANES * (hh // 2):LANES * (hh // 2 + 1)]
                qm = jnp.where(keep[hh % 2], qp, jnp.zeros_like(qp))
                kv_cols = slice(0, LANES)
            else:
                qm = q_ref[:, LANES * hh:LANES * (hh + 1)]
                kv_cols = slice(LANES * hh, LANES * (hh + 1))
            buf = hh % 2

            def scores(c, mx):
                rows = pl.ds(pl.multiple_of(c * tk, tk), tk)
                s = _dot_nt(qm, k_ref[rows, kv_cols]) * mul
                s_sc[buf, c] = s
                return jnp.maximum(mx, _lane_fold(s, jnp.maximum))

            mx = lax.fori_loop(0, n_c, scores, jnp.full((tq, LANES), -jnp.inf, F32), unroll=True)
            m = jnp.max(mx, axis=1, keepdims=True)

            def weights(c, carry):
                lsum, acc = carry
                rows = pl.ds(pl.multiple_of(c * tk, tk), tk)
                p = jnp.exp2(s_sc[buf, c] - m)
                return lsum + _lane_fold(p, jnp.add), acc + _dot(p.astype(BF16), v_ref[rows, kv_cols])

            lsum, acc = lax.fori_loop(0, n_c, weights, (jnp.zeros((tq, LANES), F32), jnp.zeros((tq, LANES), F32)),
                                      unroll=True)
            l = jnp.sum(lsum, axis=1, keepdims=True)
            outs.append(acc / l)
            lse_ref[hh] = jnp.broadcast_to((m + jnp.log2(l)) * (1.0 / log2e), (tq, LANES))
        if masked:
            for pr in range(sub // 2):
                o_ref[:, LANES * pr:LANES * (pr + 1)] = jnp.where(keep[0], outs[2 * pr], outs[2 * pr + 1])
        else:
            for hh in range(sub):
                o_ref[:, LANES * hh:LANES * (hh + 1)] = outs[hh]

    kvw = LANES if masked else LANES * sub
    return pl.pallas_call(
        body, name=name, grid=(groups, s_len // tq),
        in_specs=[pl.BlockSpec((tq, qw), lambda g, i: (i, g)),
                  pl.BlockSpec((s_len, kvw), lambda g, i: (0, g)),
                  pl.BlockSpec((s_len, kvw), lambda g, i: (0, g))],
        out_specs=[pl.BlockSpec((tq, qw), lambda g, i: (i, g)),
                   pl.BlockSpec((sub, tq, LANES), lambda g, i: (g, i, 0))],
        out_shape=[jax.ShapeDtypeStruct((s_len, groups * qw), F32),
                   jax.ShapeDtypeStruct((groups * sub, s_len, LANES), F32)],
        scratch_shapes=[pltpu.VMEM((min(sub, 2), n_c, tq, tk), F32)],
        compiler_params=_params(("arbitrary", "arbitrary")),
    )(q, k, v)


def _attn_bwd_q_call(q, k, v, do, o, lse_t, *, groups, sub, masked, tq, tk, ck, name):
    s_len = q.shape[0]
    qw = LANES * (sub // 2 if masked else sub)
    kvw = LANES if masked else LANES * sub
    n_c = tk // ck

    def body(q_ref, k_ref, v_ref, do_ref, o_ref, lse_ref, dq_ref, dkt_ref, dvt_ref):
        j = pl.program_id(1)
        i = pl.program_id(2)

        @pl.when((j == 0) & (i == 0))
        def _():
            dq_ref[...] = jnp.zeros(dq_ref.shape, F32)

        @pl.when(i == 0)
        def _():
            dkt_ref[...] = jnp.zeros(dkt_ref.shape, F32)
            dvt_ref[...] = jnp.zeros(dvt_ref.shape, F32)

        lkeep = _head_masks(tq) if masked else None
        heads = []
        for hh in range(sub):
            if masked:
                cols = slice(LANES * (hh // 2), LANES * (hh // 2 + 1))
                kv = slice(0, LANES)
                qp, dop = q_ref[:, cols], do_ref[:, cols]
                qm = jnp.where(lkeep[hh % 2], qp, jnp.zeros_like(qp))
                dom = jnp.where(lkeep[hh % 2], dop, jnp.zeros_like(dop))
            else:
                cols = kv = slice(LANES * hh, LANES * (hh + 1))
                qm, dom = q_ref[:, cols], do_ref[:, cols]
            delta = jnp.sum(dom.astype(F32) * o_ref[:, cols], axis=1, keepdims=True)
            lse = jnp.broadcast_to(lse_ref[hh], (LANES, tq)).T[:, 0:1]
            heads.append((cols, kv, qm, dom, qm.T, dom.T, delta, lse))

        def products(hh, c):
            _, kv, qm, dom, _, _, _, _ = heads[hh]
            return _dot_nt(qm, k_ref[ck * c:ck * (c + 1), kv]), _dot_nt(dom, v_ref[ck * c:ck * (c + 1), kv])

        items = [(hh, c) for hh in range(sub) for c in range(n_c)]
        dq_acc = [jnp.zeros((tq, LANES), F32) for _ in range(sub)]
        nxt = products(*items[0])
        for n, (hh, c) in enumerate(items):
            s, dp = nxt
            if n + 1 < len(items):
                nxt = products(*items[n + 1])
            _, kv, qm, dom, qmt, domt, delta, lse = heads[hh]
            p = jnp.exp(s - lse)
            ds = p * (dp - delta)
            p_b = p.astype(BF16)
            ds_b = ds.astype(BF16)
            kcols = slice(ck * c, ck * (c + 1))
            dvt_ref[kv, kcols] += _dot(domt, p_b)
            dkt_ref[kv, kcols] += _dot(qmt, ds_b)
            dq_acc[hh] = dq_acc[hh] + _dot(ds_b, k_ref[kcols, kv])
        rows = pl.ds(pl.multiple_of(i * tq, tq), tq)
        if masked:
            for pr in range(sub // 2):
                dq_ref[rows, LANES * pr:LANES * (pr + 1)] += jnp.where(lkeep[0], dq_acc[2 * pr], dq_acc[2 * pr + 1])
        else:
            for hh in range(sub):
                dq_ref[rows, LANES * hh:LANES * (hh + 1)] += dq_acc[hh]

    return pl.pallas_call(
        body, name=name, grid=(groups, s_len // tk, s_len // tq),
        in_specs=[pl.BlockSpec((tq, qw), lambda g, j, i: (i, g)),
                  pl.BlockSpec((tk, kvw), lambda g, j, i: (j, g)),
                  pl.BlockSpec((tk, kvw), lambda g, j, i: (j, g)),
                  pl.BlockSpec((tq, qw), lambda g, j, i: (i, g)),
                  pl.BlockSpec((tq, qw), lambda g, j, i: (i, g)),
                  pl.BlockSpec((sub, 1, tq), lambda g, j, i: (g, 0, i))],
        out_specs=[pl.BlockSpec((s_len, qw), lambda g, j, i: (0, g)),
                   pl.BlockSpec((kvw, tk), lambda g, j, i: (g, j)),
                   pl.BlockSpec((kvw, tk), lambda g, j, i: (g, j))],
        out_shape=[jax.ShapeDtypeStruct((s_len, groups * qw), F32),
                   jax.ShapeDtypeStruct((groups * kvw, s_len), F32),
                   jax.ShapeDtypeStruct((groups * kvw, s_len), F32)],
        compiler_params=_params(("arbitrary", "arbitrary", "arbitrary")),
    )(q, k, v, do, o, lse_t)


def _attn_bwd_p_call(q, k, v, do, o, linv_t, p, *, groups, sub, masked, tq, tk, ck, name):
    s_len = q.shape[0]
    qw = LANES * (sub // 2 if masked else sub)
    kvw = LANES if masked else LANES * sub
    n_c = tk // ck

    def body(q_ref, k_ref, v_ref, do_ref, o_ref, linv_ref, p_ref, dq_ref, dkt_ref, dvt_ref):
        j = pl.program_id(1)
        i = pl.program_id(2)

        @pl.when((j == 0) & (i == 0))
        def _():
            dq_ref[...] = jnp.zeros(dq_ref.shape, F32)

        @pl.when(i == 0)
        def _():
            dkt_ref[...] = jnp.zeros(dkt_ref.shape, F32)
            dvt_ref[...] = jnp.zeros(dvt_ref.shape, F32)

        lkeep = _head_masks(tq) if masked else None
        heads = []
        for hh in range(sub):
            cols = slice(LANES * (hh // 2), LANES * (hh // 2 + 1)) if masked else slice(LANES * hh, LANES * (hh + 1))
            kv = slice(0, LANES) if masked else cols
            linv = jnp.broadcast_to(linv_ref[hh], (LANES, tq)).T
            qm = q_ref[:, cols]
            dof = do_ref[:, cols] * linv
            if masked:
                qm = jnp.where(lkeep[hh % 2], qm, jnp.zeros_like(qm))
                dof = jnp.where(lkeep[hh % 2], dof, jnp.zeros_like(dof))
            delta = jnp.sum(dof * o_ref[:, cols], axis=1, keepdims=True)
            dom = dof.astype(BF16)
            heads.append((kv, qm, dom, qm.T, dom.T, delta))

        def product(hh, c):
            kv, _, dom, _, _, _ = heads[hh]
            return _dot_nt(dom, v_ref[ck * c:ck * (c + 1), kv])

        items = [(hh, c) for hh in range(sub) for c in range(n_c)]
        dq_acc = [jnp.zeros((tq, LANES), F32) for _ in range(sub)]
        nxt = product(*items[0])
        for n, (hh, c) in enumerate(items):
            dp = nxt
            if n + 1 < len(items):
                nxt = product(*items[n + 1])
            kv, qm, dom, qmt, domt, delta = heads[hh]
            kcols = slice(ck * c, ck * (c + 1))
            p_b = p_ref[hh, :, kcols]
            ds_b = (p_b.astype(F32) * (dp - delta)).astype(BF16)
            dvt_ref[kv, kcols] += _dot(domt, p_b)
            dkt_ref[kv, kcols] += _dot(qmt, ds_b)
            dq_acc[hh] = dq_acc[hh] + _dot(ds_b, k_ref[kcols, kv])
        rows = pl.ds(pl.multiple_of(i * tq, tq), tq)
        if masked:
            for pr in range(sub // 2):
                dq_ref[rows, LANES * pr:LANES * (pr + 1)] += jnp.where(lkeep[0], dq_acc[2 * pr], dq_acc[2 * pr + 1])
        else:
            for hh in range(sub):
                dq_ref[rows, LANES * hh:LANES * (hh + 1)] += dq_acc[hh]

    return pl.pallas_call(
        body, name=name, grid=(groups, s_len // tk, s_len // tq),
        in_specs=[pl.BlockSpec((tq, qw), lambda g, j, i: (i, g)),
                  pl.BlockSpec((tk, kvw), lambda g, j, i: (j, g)),
                  pl.BlockSpec((tk, kvw), lambda g, j, i: (j, g)),
                  pl.BlockSpec((tq, qw), lambda g, j, i: (i, g)),
                  pl.BlockSpec((tq, qw), lambda g, j, i: (i, g)),
                  pl.BlockSpec((sub, 1, tq), lambda g, j, i: (g, 0, i)),
                  pl.BlockSpec((sub, tq, tk), lambda g, j, i: (g, i, j))],
        out_specs=[pl.BlockSpec((s_len, qw), lambda g, j, i: (0, g)),
                   pl.BlockSpec((kvw, tk), lambda g, j, i: (g, j)),
                   pl.BlockSpec((kvw, tk), lambda g, j, i: (g, j))],
        out_shape=[jax.ShapeDtypeStruct((s_len, groups * qw), F32),
                   jax.ShapeDtypeStruct((groups * kvw, s_len), F32),
                   jax.ShapeDtypeStruct((groups * kvw, s_len), F32)],
        compiler_params=_params(("arbitrary", "arbitrary", "arbitrary")),
    )(q, k, v, do, o, linv_t, p)


def _attn_bwd_call(q, k, v, do, lse_t, delta_t, *, groups, sub, scale, tq, tk, name):
    s_len = q.shape[0]
    masked = sub > 1
    qw = LANES * (sub // 2 if masked else 1)
    n_k = s_len // tk

    def body(q_ref, k_ref, v_ref, do_ref, lse_ref, dl_ref, dq_ref, dk_ref, dv_ref, dq_sc):
        i = pl.program_id(1)
        j = pl.program_id(2)

        @pl.when((i == 0) & (j == 0))
        def _():
            dk_ref[...] = jnp.zeros(dk_ref.shape, F32)
            dv_ref[...] = jnp.zeros(dv_ref.shape, F32)

        @pl.when(j == 0)
        def _():
            dq_sc[...] = jnp.zeros(dq_sc.shape, F32)

        kk = k_ref[...]
        vv = v_ref[...]
        keep = _head_masks(tq) if masked else None
        dk_t = jnp.zeros((tk, LANES), F32)
        dv_t = jnp.zeros((tk, LANES), F32)
        for hh in range(sub):
            if masked:
                cols = slice(LANES * (hh // 2), LANES * (hh // 2 + 1))
                qp = q_ref[:, cols]
                dop = do_ref[:, cols]
                qm = jnp.where(keep[hh % 2], qp, jnp.zeros_like(qp))
                dom = jnp.where(keep[hh % 2], dop, jnp.zeros_like(dop))
            else:
                cols = slice(0, LANES)
                qm = q_ref[...]
                dom = do_ref[...]
            s_t = _dot_nt(kk, qm)
            if scale is not None:
                s_t = s_t * scale
            p_t = jnp.exp(s_t - lse_ref[hh])
            dp_t = _dot_nt(vv, dom)
            ds_t = p_t * (dp_t - dl_ref[hh])
            if scale is not None:
                ds_t = ds_t * scale
            p_b = p_t.astype(BF16)
            ds_b = ds_t.astype(BF16)
            dv_t = dv_t + _dot(p_b, dom)
            dk_t = dk_t + _dot(ds_b, qm)
            dq_h = _dot_tn(ds_b, kk)
            if masked:
                dq_h = jnp.where(keep[hh % 2], dq_h, jnp.zeros_like(dq_h))
            dq_sc[:, cols] += dq_h
        rows = pl.ds(pl.multiple_of(j * tk, tk), tk)
        dk_ref[rows, :] += dk_t
        dv_ref[rows, :] += dv_t

        @pl.when(j == n_k - 1)
        def _():
            dq_ref[...] = dq_sc[...]

    return pl.pallas_call(
        body, name=name, grid=(groups, s_len // tq, n_k),
        in_specs=[pl.BlockSpec((tq, qw), lambda g, i, j: (i, g)),
                  pl.BlockSpec((tk, LANES), lambda g, i, j: (j, g)),
                  pl.BlockSpec((tk, LANES), lambda g, i, j: (j, g)),
                  pl.BlockSpec((tq, qw), lambda g, i, j: (i, g)),
                  pl.BlockSpec((sub, 1, tq), lambda g, i, j: (g, 0, i)),
                  pl.BlockSpec((sub, 1, tq), lambda g, i, j: (g, 0, i))],
        out_specs=[pl.BlockSpec((tq, qw), lambda g, i, j: (i, g)),
                   pl.BlockSpec((s_len, LANES), lambda g, i, j: (0, g)),
                   pl.BlockSpec((s_len, LANES), lambda g, i, j: (0, g))],
        out_shape=[jax.ShapeDtypeStruct((s_len, groups * qw), F32),
                   jax.ShapeDtypeStruct((s_len, groups * LANES), F32),
                   jax.ShapeDtypeStruct((s_len, groups * LANES), F32)],
        scratch_shapes=[pltpu.VMEM((tq, qw), F32)],
        compiler_params=_params(("arbitrary", "arbitrary", "arbitrary")),
    )(q, k, v, do, lse_t, delta_t)


def _silu_parts(g):
    sig = 1.0 / (1.0 + jnp.exp(-g))
    return g * sig, sig * (1.0 + g * (1.0 - sig))


def _out_call(x, target, oa, ob, ga, gb, wout, tm):
    s_len = x.shape[0]
    n_t = s_len // tm

    def body(x_ref, t_ref, oa_ref, ob_ref, ga_ref, gb_ref, w_ref,
             dh_ref, doa_ref, dob_ref, dga_ref, dgb_ref, dw_ref, loss_ref):
        i = pl.program_id(0)

        @pl.when(i == 0)
        def _():
            dw_ref[...] = jnp.zeros(dw_ref.shape, F32)
            loss_ref[...] = jnp.zeros(loss_ref.shape, F32)

        oa_v, ob_v = oa_ref[...], ob_ref[...]
        silu_a, dsilu_a = _silu_parts(ga_ref[...])
        silu_b, dsilu_b = _silu_parts(gb_ref[...])
        ya = (oa_v * silu_a).astype(BF16)
        yb = (ob_v * silu_b).astype(BF16)
        h = x_ref[...] + _dot(ya, w_ref[0:512, :]) + _dot(yb, w_ref[512:1024, :])
        err = h - t_ref[...]
        part = jnp.sum(err * err, axis=0, keepdims=True)
        acc = part[:, 0:LANES]
        for c in range(1, D_MODEL // LANES):
            acc = acc + part[:, LANES * c:LANES * (c + 1)]
        loss_ref[...] += acc
        dh = err * (1.0 / D_MODEL)
        dh_ref[...] = dh
        dhb = dh.astype(BF16)
        dya = _dot_nt(dhb, w_ref[0:512, :])
        dyb = _dot_nt(dhb, w_ref[512:1024, :])
        doa = dya * silu_a
        dob = dyb * silu_b
        doa_ref[...] = doa.astype(BF16)
        dob_ref[...] = dob.astype(BF16)
        dga_ref[...] = dya * oa_v * dsilu_a
        dgb_ref[...] = dyb * ob_v * dsilu_b
        dw_ref[0:512, :] += _dot_tn(ya, dhb)
        dw_ref[512:1024, :] += _dot_tn(yb, dhb)

    def rows(width):
        return pl.BlockSpec((tm, width), lambda i: (i, 0))

    outs = [(D_MODEL, F32), (512, BF16), (512, BF16), (512, F32), (512, F32)]
    return pl.pallas_call(
        body, name="out_fwd", grid=(n_t,),
        in_specs=[rows(D_MODEL), rows(D_MODEL), rows(512), rows(512), rows(512), rows(512),
                  _full((D_MODEL, D_MODEL))],
        out_specs=[rows(wd) for wd, _ in outs] + [_full((D_MODEL, D_MODEL)), _full((1, LANES))],
        out_shape=[jax.ShapeDtypeStruct((s_len, wd), dt) for wd, dt in outs]
        + [jax.ShapeDtypeStruct((D_MODEL, D_MODEL), F32), jax.ShapeDtypeStruct((1, LANES), F32)],
        compiler_params=_params(("arbitrary",)),
    )(x, target, oa, ob, ga, gb, wout)


def _pre_bwd_call(x, raw, dh, dqa, dka, dva, dga, dqb, dkb, dvb, dgb, consts, tabs, tm):
    s_len = x.shape[0]

    def body(x_ref, raw_ref, dh_ref, dqa_ref, dkat_ref, dvat_ref, dga_ref, dqb_ref, dkbt_ref, dvbt_ref, dgb_ref,
             *refs):
        tab_refs, refs = refs[:8], refs[8:]
        (gin_ref, w_ref, wuq_ref, wuk_ref, wuv_ref, gq_ref, gk_ref, gcq_ref, gckv_ref, gqb_ref, gkb_ref, g64_ref,
         dx_ref, dproj_ref, dwuq_ref, dwuk_ref, dwuv_ref, small_ref) = refs
        i = pl.program_id(0)
        dka_v, dva_v = dkat_ref[...].T, dvat_ref[...].T
        dkb_v, dvb_v = dkbt_ref[...].T, dvbt_ref[...].T

        @pl.when(i == 0)
        def _():
            dwuq_ref[...] = jnp.zeros(dwuq_ref.shape, F32)
            dwuk_ref[...] = jnp.zeros(dwuk_ref.shape, F32)
            dwuv_ref[...] = jnp.zeros(dwuv_ref.shape, F32)
            small_ref[...] = jnp.zeros(small_ref.shape, F32)

        gin, gq, gk = gin_ref[...], gq_ref[...], gk_ref[...]
        gcq, gckv, gqb, gkb = gcq_ref[...], gckv_ref[...], gqb_ref[...], gkb_ref[...]
        ca, sa, cb, sb = _rope_tiles(tab_refs, i, tm)
        w, wuq, wuk, wuv = w_ref[...], wuq_ref[...], wuk_ref[...], wuv_ref[...]
        f = _pre_forward(x_ref[...], gin, w, wuq, wuk, wuv, gq, gk, gcq, gckv, gqb, gkb,
                         ca, sa, cb, sb, g64_ref[...], tm, raw=raw_ref[...])
        sel16, sel8, gs64 = f["sel16"], f["sel8"], f["gs64"]
        lane = lax.broadcasted_iota(jnp.int32, (tm, LANES), 1)
        low = lane < 64
        zero = jnp.zeros((tm, LANES), F32)
        pieces = []

        dgq = jnp.zeros((1, LANES), F32)
        for s in range(4):
            _, xh, r = f["qa"][s]
            d = dqa_ref[:, LANES * s:LANES * (s + 1)] * QA_SCALE
            dx, dg = _col_bwd(d, xh, r, gs64, 64.0, gq, ca, sa, 16, sel16)
            pieces.append(dx)
            dgq = dgq + dg
        dgk = jnp.zeros((1, LANES), F32)
        for s in range(2):
            _, xh, r = f["ka"][s]
            d = dka_v[:, LANES * s:LANES * (s + 1)]
            d = d + pltpu.roll(d, 64, 1)
            dx, dg = _col_bwd(d, xh, r, _row_sum, 128.0, gk, ca, sa, 16, sel16)
            pieces.append(jnp.where(low, dx, zero))
            dgk = dgk + dg
        for s in range(2):
            d = dva_v[:, LANES * s:LANES * (s + 1)]
            d = d + pltpu.roll(d, 64, 1)
            pieces.append(jnp.where(low, d, zero))
        pieces.append(dga_ref[...])

        dgqb = jnp.zeros((1, LANES), F32)
        dq_cols = []
        for h in range(B_HEADS):
            _, xh, r = f["qb"][h]
            dx, dg = _col_bwd(dqb_ref[:, LANES * h:LANES * (h + 1)] * QB_SCALE, xh, r, _row_sum, float(B_QK_DIM),
                              gqb, cb, sb, 8, sel8)
            dq_cols.append(dx)
            dgqb = dgqb + dg
        dqr_b = jnp.concatenate(dq_cols, axis=1).astype(BF16)
        dwuq_ref[...] += _dot_tn(f["cqb"], dqr_b)
        dcq_raw, dgcq = _rms_bwd(_dot_nt(dqr_b, wuq), f["cqh"], f["rcq"], gcq)
        pieces.append(dcq_raw)

        dgkb = jnp.zeros((1, LANES), F32)
        dk_cols = []
        dkr = zero
        for h in range(B_HEADS):
            _, xh, r = f["kb"][h]
            dx, dg = _col_bwd(dkb_v[:, LANES * h:LANES * (h + 1)], xh, r, _row_sum, float(B_QK_DIM),
                              gkb, cb, sb, 8, sel8)
            dk_cols.append(dx)
            dkr = dkr + dx
            dgkb = dgkb + dg
        dkr_b = jnp.concatenate(dk_cols, axis=1).astype(BF16)
        dvb_b = dvb_v.astype(BF16)
        dwuk_ref[...] += _dot_tn(f["ckvb"], dkr_b)
        dwuv_ref[...] += _dot_tn(f["ckvb"], dvb_b)
        dckv = _dot_nt(dkr_b, wuk) + _dot_nt(dvb_b, wuv)
        dckv_raw, dgckv = _rms_bwd(dckv, f["ckvh"], f["rckv"], gckv)
        pieces.append(dckv_raw)
        pieces.append(jnp.where((lane >= B_NOPE_DIM) & (lane < B_QK_DIM), dkr, zero))
        pieces += [zero, zero, zero]
        pieces.append(dgb_ref[...])

        dproj_b = jnp.concatenate(pieces, axis=1).astype(BF16)
        dproj_ref[...] = dproj_b
        dxn = _dot_nt(dproj_b, w)
        dx, dgin = _rms_bwd(dxn, f["xh0"], f["r0"], gin)
        dx_ref[...] = dx + dh_ref[...]

        for c in range(D_MODEL // LANES):
            small_ref[c:c + 1, :] += dgin[:, LANES * c:LANES * (c + 1)]
        small_ref[8:9, :] += dgq
        small_ref[9:10, :] += dgk
        for c in range(3):
            small_ref[10 + c:11 + c, :] += dgcq[:, LANES * c:LANES * (c + 1)]
        for c in range(2):
            small_ref[13 + c:14 + c, :] += dgckv[:, LANES * c:LANES * (c + 1)]
        small_ref[15:16, :] += dgqb
        small_ref[16:17, :] += dgkb

    def rows(width):
        return pl.BlockSpec((tm, width), lambda i: (i, 0))

    def cols(height):
        return pl.BlockSpec((height, tm), lambda i: (0, i))

    return pl.pallas_call(
        body, name="pre_bwd", grid=(s_len // tm,),
        in_specs=[rows(D_MODEL), rows(R_WIDTH), rows(D_MODEL), rows(512), cols(256), cols(256), rows(512), rows(512),
                  cols(512), cols(512), rows(512)] + [_full(t.shape) for t in tabs] + _pre_const_specs(consts),
        out_specs=[rows(D_MODEL), rows(N_WIDE), _full((B_Q_RANK, 512)), _full((B_KV_RANK, 512)),
                   _full((B_KV_RANK, 512)), _full((R_SMALL, LANES))],
        out_shape=[jax.ShapeDtypeStruct((s_len, D_MODEL), F32), jax.ShapeDtypeStruct((s_len, N_WIDE), BF16),
                   jax.ShapeDtypeStruct((B_Q_RANK, 512), F32),
                   jax.ShapeDtypeStruct((B_KV_RANK, 512), F32), jax.ShapeDtypeStruct((B_KV_RANK, 512), F32),
                   jax.ShapeDtypeStruct((R_SMALL, LANES), F32)],
        compiler_params=_params(("arbitrary",)),
    )(x, raw, dh, dqa, dka, dva, dga, dqb, dkb, dvb, dgb, *tabs, *[consts[n] for n in _PRE_IN_NAMES])


def _dw_in_call(xnb, dproj_b, tt, tn):
    s_len = xnb.shape[0]

    def body(a_ref, b_ref, o_ref):
        @pl.when(pl.program_id(1) == 0)
        def _():
            o_ref[...] = jnp.zeros(o_ref.shape, F32)

        o_ref[...] += _dot_tn(a_ref[...], b_ref[...])

    return pl.pallas_call(
        body, name="dw_in", grid=(N_WIDE // tn, s_len // tt),
        in_specs=[pl.BlockSpec((tt, D_MODEL), lambda n, t: (t, 0)), pl.BlockSpec((tt, tn), lambda n, t: (t, n))],
        out_specs=pl.BlockSpec((D_MODEL, tn), lambda n, t: (0, n)),
        out_shape=jax.ShapeDtypeStruct((D_MODEL, N_WIDE), F32),
        compiler_params=_params(("arbitrary", "arbitrary")),
    )(xnb, dproj_b)


def _mesh_pos():
    return lax.axis_index("x"), lax.axis_index("y"), lax.axis_index("c")


def _flip(v, bit):
    return 1 - v if bit else v


def _peer(pos, k):
    x, y, c = pos
    return _flip(x, (k >> 2) & 1), _flip(y, (k >> 1) & 1), _flip(c, k & 1)


def _logical(p):
    return 4 * p[0] + 2 * p[1] + p[2]


def _gather_weights_call(shard):
    m_per = shard.shape[0]

    def body(x_ref, out_ref, xb_ref, send_sems, recv_sems, local_sem):
        x, y, c = _mesh_pos()
        me, sibling = (x, y, c), (x, y, 1 - c)
        chips = [(1 - x, y), (x, 1 - y), (1 - x, 1 - y)]
        xb_ref[...] = x_ref[...].astype(BF16)

        def rows(p):
            return out_ref.at[pl.ds(pl.multiple_of(_logical(p) * m_per, 16), m_per), :]

        def copy(k, block, to, src=None):
            return pltpu.make_async_remote_copy(
                src_ref=rows(block) if src is None else src, dst_ref=rows(block),
                send_sem=send_sems.at[k], recv_sem=recv_sems.at[k],
                device_id=to, device_id_type=pl.DeviceIdType.MESH)

        mine = pltpu.make_async_copy(xb_ref, rows(me), local_sem)
        mine.start()
        first = [copy(0, me, sibling, src=xb_ref)]
        first += [copy(1 + j, me, (*chip, c), src=xb_ref) for j, chip in enumerate(chips)]
        for cp in first:
            cp.start()
        passed = [copy(4 + j, (*chip, c), sibling) for j, chip in enumerate(chips)]
        for j, chip in enumerate(chips):
            copy(1 + j, (*chip, c), me).wait_recv()
            passed[j].start()
        copy(0, sibling, me).wait_recv()
        for j, chip in enumerate(chips):
            copy(4 + j, (*chip, 1 - c), me).wait_recv()
        for cp in first + passed:
            cp.wait_send()
        mine.wait()

    return pl.pallas_call(
        body, name="gather_weights",
        out_shape=jax.ShapeDtypeStruct((N_DEV * m_per, LANES), BF16),
        in_specs=[pl.BlockSpec(memory_space=pltpu.VMEM)],
        out_specs=pl.BlockSpec(memory_space=pltpu.VMEM),
        scratch_shapes=[pltpu.VMEM((m_per, LANES), BF16), pltpu.SemaphoreType.DMA((7,)),
                        pltpu.SemaphoreType.DMA((7,)), pltpu.SemaphoreType.DMA],
        compiler_params=pltpu.CompilerParams(vmem_limit_bytes=VMEM_LIMIT),
    )(shard)


def _adamw(w, g, m, v):
    m = ADAM_B1 * m + (1.0 - ADAM_B1) * g
    v = ADAM_B2 * v + (1.0 - ADAM_B2) * (g * g)
    m_hat = m / (1.0 - ADAM_B1 ** ADAM_STEP)
    v_hat = v / (1.0 - ADAM_B2 ** ADAM_STEP)
    delta = -ADAM_LR * (m_hat / (jnp.sqrt(v_hat) + ADAM_EPS) + ADAM_WD * w)
    return delta, m, v


def _reduce_adamw_call(parts, small, w_pk, m_pk, v_pk, w_s, m_s, v_s):
    chunk = 16
    n_chunks = R_PACK // chunk

    def body(parts_ref, small_ref, w_ref, m_ref, v_ref, ws_ref, ms_ref, vs_ref,
             g_ref, d_ref, nm_ref, nv_ref, gs_ref, ds_ref, nms_ref, nvs_ref,
             recv_ref, recv_s_ref, send_sems, recv_sems, send_s_sems, recv_s_sems, local_sem):
        pos = _mesh_pos()
        me = _logical(pos)

        def big(k):
            peer = _peer(pos, k)
            return pltpu.make_async_remote_copy(
                src_ref=parts_ref.at[_logical(peer)], dst_ref=recv_ref.at[k],
                send_sem=send_sems.at[k], recv_sem=recv_sems.at[k],
                device_id=peer, device_id_type=pl.DeviceIdType.MESH)

        def tiny(k):
            return pltpu.make_async_remote_copy(
                src_ref=small_ref, dst_ref=recv_s_ref.at[k],
                send_sem=send_s_sems.at[k], recv_sem=recv_s_sems.at[k],
                device_id=_peer(pos, k), device_id_type=pl.DeviceIdType.MESH)

        own = pltpu.make_async_copy(parts_ref.at[me], recv_ref.at[0], local_sem)
        own.start()
        for k in range(1, N_DEV):
            tiny(k).start()
        for k in range(1, N_DEV):
            big(k).start()
        recv_s_ref[0] = small_ref[...]
        for k in range(1, N_DEV):
            tiny(k).wait_recv()
        acc = recv_s_ref[me]
        for a in range(1, N_DEV):
            acc = acc + recv_s_ref[lax.bitwise_xor(me, a)]
        row = lax.broadcasted_iota(jnp.int32, (R_SMALL, LANES), 0)
        gs = jnp.where(row == 8, acc + pltpu.roll(acc, 64, 1), acc)
        gs = jnp.where(row == ROW_LOSS, jnp.sum(acc, axis=1, keepdims=True) * (0.5 / D_MODEL), gs)
        gs_ref[...] = gs
        ds, nms, nvs = _adamw(ws_ref[...], gs, ms_ref[...], vs_ref[...])
        ds_ref[...] = ds
        nms_ref[...] = nms
        nvs_ref[...] = nvs

        own.wait()
        for k in range(1, N_DEV):
            big(k).wait_recv()

        def step(t, carry):
            rows = pl.ds(pl.multiple_of(t * chunk, chunk), chunk)
            g = recv_ref[0, rows, :]
            for k in range(1, N_DEV):
                g = g + recv_ref[k, rows, :]
            d, nm, nv = _adamw(w_ref[rows, :], g, m_ref[rows, :], v_ref[rows, :])
            g_ref[rows, :] = g
            d_ref[rows, :] = d
            nm_ref[rows, :] = nm
            nv_ref[rows, :] = nv
            return carry

        lax.fori_loop(0, n_chunks, step, 0)
        for k in range(1, N_DEV):
            tiny(k).wait_send()
            big(k).wait_send()

    vm = pl.BlockSpec(memory_space=pltpu.VMEM)
    big_shape = jax.ShapeDtypeStruct((R_PACK, LANES), F32)
    small_shape = jax.ShapeDtypeStruct((R_SMALL, LANES), F32)
    return pl.pallas_call(
        body, name="reduce_adamw",
        in_specs=[pl.BlockSpec(memory_space=pl.ANY)] + [vm] * 7,
        out_specs=[vm] * 8,
        out_shape=[big_shape] * 4 + [small_shape] * 4,
        scratch_shapes=[pltpu.VMEM((N_DEV, R_PACK, LANES), F32), pltpu.VMEM((N_DEV, R_SMALL, LANES), F32),
                        pltpu.SemaphoreType.DMA((N_DEV,)), pltpu.SemaphoreType.DMA((N_DEV,)),
                        pltpu.SemaphoreType.DMA((N_DEV,)), pltpu.SemaphoreType.DMA((N_DEV,)),
                        pltpu.SemaphoreType.DMA],
        compiler_params=pltpu.CompilerParams(vmem_limit_bytes=VMEM_LIMIT),
    )(parts, small, w_pk, m_pk, v_pk, w_s, m_s, v_s)


W_BLOCKS = ((D_MODEL, N_IN // N_DEV), (B_Q_RANK // N_DEV, 384), (B_KV_RANK, 768 // N_DEV), (D_MODEL // N_DEV, D_MODEL))
N_W = len(W_BLOCKS)


def _gather_blocks_call(blocks):
    def body(*refs):
        x_refs, out_refs, xb_refs = refs[0:N_W], refs[N_W:2 * N_W], refs[2 * N_W:3 * N_W]
        send_sems, recv_sems, local_sems = refs[3 * N_W:]
        x, y, c = _mesh_pos()
        me, sibling = (x, y, c), (x, y, 1 - c)
        chips = [(1 - x, y), (x, 1 - y), (1 - x, 1 - y)]
        for w in range(N_W):
            xb_refs[w][...] = x_refs[w][...].astype(BF16)

        def slot(w, p):
            return out_refs[w].at[_logical(p)]

        def copy(w, k, block, to, src=None):
            return pltpu.make_async_remote_copy(
                src_ref=slot(w, block) if src is None else src, dst_ref=slot(w, block),
                send_sem=send_sems.at[N_W * k + w], recv_sem=recv_sems.at[N_W * k + w],
                device_id=to, device_id_type=pl.DeviceIdType.MESH)

        mine = [pltpu.make_async_copy(xb_refs[w], slot(w, me), local_sems.at[w]) for w in range(N_W)]
        for cp in mine:
            cp.start()
        first = [copy(w, 0, me, sibling, src=xb_refs[w]) for w in range(N_W)]
        first += [copy(w, 1 + j, me, (*chip, c), src=xb_refs[w]) for j, chip in enumerate(chips) for w in range(N_W)]
        for cp in first:
            cp.start()
        passed = []
        for j, chip in enumerate(chips):
            for w in range(N_W):
                copy(w, 1 + j, (*chip, c), me).wait_recv()
                fwd = copy(w, 4 + j, (*chip, c), sibling)
                fwd.start()
                passed.append(fwd)
        for w in range(N_W):
            copy(w, 0, sibling, me).wait_recv()
        for j, chip in enumerate(chips):
            for w in range(N_W):
                copy(w, 4 + j, (*chip, 1 - c), me).wait_recv()
        for cp in first + passed:
            cp.wait_send()
        for cp in mine:
            cp.wait()

    vm = pl.BlockSpec(memory_space=pltpu.VMEM)
    return pl.pallas_call(
        body, name="gather_weights",
        out_shape=[jax.ShapeDtypeStruct((N_DEV,) + shp, BF16) for shp in W_BLOCKS],
        in_specs=[vm] * N_W, out_specs=[vm] * N_W,
        scratch_shapes=[pltpu.VMEM(shp, BF16) for shp in W_BLOCKS]
        + [pltpu.SemaphoreType.DMA((7 * N_W,)), pltpu.SemaphoreType.DMA((7 * N_W,)), pltpu.SemaphoreType.DMA((N_W,))],
        compiler_params=pltpu.CompilerParams(vmem_limit_bytes=VMEM_LIMIT),
    )(*blocks)


def _reduce_two_level_call(parts, small, w_blk, m_blk, v_blk, w_s, m_s, v_s):
    chunks = (32, 48, 64, 16)
    n_chip = N_DEV // 2

    def body(*refs):
        p_refs = refs[0:4]
        small_ref = refs[4]
        w_refs, m_refs, v_refs = refs[5:9], refs[9:13], refs[13:17]
        ws_ref, ms_ref, vs_ref = refs[17:20]
        g_refs, d_refs, nm_refs, nv_refs = refs[20:24], refs[24:28], refs[28:32], refs[32:36]
        gs_ref, ds_ref, nms_ref, nvs_ref = refs[36:40]
        ra_refs, rb_refs, st_refs = refs[40:44], refs[44:48], refs[48:52]
        recv_s_ref = refs[52]
        send_a, recv_a, send_b, recv_b, send_s_sems, recv_s_sems = refs[53:59]
        pos = _mesh_pos()
        x, y, c = pos
        me = _logical(pos)
        sibling = (x, y, 1 - c)

        def chip(j):
            return _flip(x, j & 1), _flip(y, (j >> 1) & 1)

        def to_sibling(w, j):
            return pltpu.make_async_remote_copy(
                src_ref=p_refs[w].at[_logical((*chip(j), 1 - c))], dst_ref=ra_refs[w].at[j],
                send_sem=send_a.at[N_W * j + w], recv_sem=recv_a.at[N_W * j + w],
                device_id=sibling, device_id_type=pl.DeviceIdType.MESH)

        def to_chip(w, j):
            return pltpu.make_async_remote_copy(
                src_ref=st_refs[w].at[j - 1], dst_ref=rb_refs[w].at[j - 1],
                send_sem=send_b.at[N_W * (j - 1) + w], recv_sem=recv_b.at[N_W * (j - 1) + w],
                device_id=(*chip(j), c), device_id_type=pl.DeviceIdType.MESH)

        def tiny(k):
            return pltpu.make_async_remote_copy(
                src_ref=small_ref, dst_ref=recv_s_ref.at[k],
                send_sem=send_s_sems.at[k], recv_sem=recv_s_sems.at[k],
                device_id=_peer(pos, k), device_id_type=pl.DeviceIdType.MESH)

        order = (0, 3, 2, 1)
        for j in (1, 2, 3, 0):
            for w in order:
                to_sibling(w, j).start()
        for k in range(1, N_DEV):
            tiny(k).start()

        for j in (1, 2, 3):
            d = _logical((*chip(j), c))
            for w in order:
                to_sibling(w, j).wait_recv()
                chunk = chunks[w]

                def pair(t, carry, w=w, j=j, d=d, chunk=chunk):
                    rows = pl.ds(pl.multiple_of(t * chunk, chunk), chunk)
                    s = p_refs[w][d, rows, :].astype(F32) + ra_refs[w][j, rows, :].astype(F32)
                    st_refs[w][j - 1, rows, :] = s.astype(BF16)
                    return carry

                lax.fori_loop(0, W_BLOCKS[w][0] // chunk, pair, 0)
                to_chip(w, j).start()

        recv_s_ref[0] = small_ref[...]
        for k in range(1, N_DEV):
            tiny(k).wait_recv()
        acc = recv_s_ref[me]
        for a in range(1, N_DEV):
            acc = acc + recv_s_ref[lax.bitwise_xor(me, a)]
        row = lax.broadcasted_iota(jnp.int32, (R_SMALL, LANES), 0)
        gs = jnp.where(row == 8, acc + pltpu.roll(acc, 64, 1), acc)
        gs = jnp.where(row == ROW_LOSS, jnp.sum(acc, axis=1, keepdims=True) * (0.5 / D_MODEL), gs)
        gs_ref[...] = gs
        ds, nms, nvs = _adamw(ws_ref[...], gs, ms_ref[...], vs_ref[...])
        ds_ref[...] = ds
        nms_ref[...] = nms
        nvs_ref[...] = nvs

        for w in (1, 2, 3, 0):
            to_sibling(w, 0).wait_recv()
            for j in (1, 2, 3):
                to_chip(w, j).wait_recv()
            chunk = chunks[w]

            def step(t, carry, w=w, chunk=chunk):
                rows = pl.ds(pl.multiple_of(t * chunk, chunk), chunk)
                g = p_refs[w][me, rows, :].astype(F32) + ra_refs[w][0, rows, :].astype(F32)
                for j in range(n_chip - 1):
                    g = g + rb_refs[w][j, rows, :].astype(F32)
                d, nm, nv = _adamw(w_refs[w][rows, :], g, m_refs[w][rows, :], v_refs[w][rows, :])
                g_refs[w][rows, :] = g
                d_refs[w][rows, :] = d
                nm_refs[w][rows, :] = nm
                nv_refs[w][rows, :] = nv
                return carry

            lax.fori_loop(0, W_BLOCKS[w][0] // chunk, step, 0)
        for k in range(1, N_DEV):
            tiny(k).wait_send()
        for w in range(N_W):
            for j in range(n_chip):
                to_sibling(w, j).wait_send()
            for j in (1, 2, 3):
                to_chip(w, j).wait_send()

    vm = pl.BlockSpec(memory_space=pltpu.VMEM)
    blk = [jax.ShapeDtypeStruct(shp, F32) for shp in W_BLOCKS]
    small_shape = jax.ShapeDtypeStruct((R_SMALL, LANES), F32)
    return pl.pallas_call(
        body, name="reduce_adamw",
        in_specs=[vm] * 20, out_specs=[vm] * 20,
        out_shape=blk * 4 + [small_shape] * 4,
        scratch_shapes=[pltpu.VMEM((n_chip,) + shp, BF16) for shp in W_BLOCKS]
        + [pltpu.VMEM((n_chip - 1,) + shp, BF16) for shp in W_BLOCKS] * 2
        + [pltpu.VMEM((N_DEV, R_SMALL, LANES), F32),
           pltpu.SemaphoreType.DMA((n_chip * N_W,)), pltpu.SemaphoreType.DMA((n_chip * N_W,)),
           pltpu.SemaphoreType.DMA(((n_chip - 1) * N_W,)), pltpu.SemaphoreType.DMA(((n_chip - 1) * N_W,)),
           pltpu.SemaphoreType.DMA((N_DEV,)), pltpu.SemaphoreType.DMA((N_DEV,))],
        compiler_params=pltpu.CompilerParams(vmem_limit_bytes=VMEM_LIMIT),
    )(*parts, small, *w_blk, *m_blk, *v_blk, w_s, m_s, v_s)


def _reduce_blocks_call(parts, small, w_blk, m_blk, v_blk, w_s, m_s, v_s):
    chunks = (32, 48, 64, 16)

    def body(*refs):
        p_refs = refs[0:4]
        small_ref = refs[4]
        w_refs, m_refs, v_refs = refs[5:9], refs[9:13], refs[13:17]
        ws_ref, ms_ref, vs_ref = refs[17:20]
        g_refs, d_refs, nm_refs, nv_refs = refs[20:24], refs[24:28], refs[28:32], refs[32:36]
        gs_ref, ds_ref, nms_ref, nvs_ref = refs[36:40]
        r_refs = refs[40:44]
        recv_s_ref = refs[44]
        send_sems, recv_sems, send_s_sems, recv_s_sems, local_sems = refs[45:50]
        pos = _mesh_pos()
        me = _logical(pos)

        def big(w, k):
            peer = _peer(pos, k)
            return pltpu.make_async_remote_copy(
                src_ref=p_refs[w].at[_logical(peer)], dst_ref=r_refs[w].at[k],
                send_sem=send_sems.at[N_W * k + w], recv_sem=recv_sems.at[N_W * k + w],
                device_id=peer, device_id_type=pl.DeviceIdType.MESH)

        def tiny(k):
            return pltpu.make_async_remote_copy(
                src_ref=small_ref, dst_ref=recv_s_ref.at[k],
                send_sem=send_s_sems.at[k], recv_sem=recv_s_sems.at[k],
                device_id=_peer(pos, k), device_id_type=pl.DeviceIdType.MESH)

        own = [pltpu.make_async_copy(p_refs[w].at[me], r_refs[w].at[0], local_sems.at[w]) for w in range(N_W)]
        for cp in own:
            cp.start()
        for k in range(1, N_DEV):
            tiny(k).start()
        for k in range(1, N_DEV):
            for w in range(N_W):
                big(w, k).start()
        recv_s_ref[0] = small_ref[...]
        for k in range(1, N_DEV):
            tiny(k).wait_recv()
        acc = recv_s_ref[me]
        for a in range(1, N_DEV):
            acc = acc + recv_s_ref[lax.bitwise_xor(me, a)]
        row = lax.broadcasted_iota(jnp.int32, (R_SMALL, LANES), 0)
        gs = jnp.where(row == 8, acc + pltpu.roll(acc, 64, 1), acc)
        gs = jnp.where(row == ROW_LOSS, jnp.sum(acc, axis=1, keepdims=True) * (0.5 / D_MODEL), gs)
        gs_ref[...] = gs
        ds, nms, nvs = _adamw(ws_ref[...], gs, ms_ref[...], vs_ref[...])
        ds_ref[...] = ds
        nms_ref[...] = nms
        nvs_ref[...] = nvs

        for w in (3, 2, 1, 0):
            own[w].wait()
            for k in range(1, N_DEV):
                big(w, k).wait_recv()
            chunk = chunks[w]

            def step(t, carry, w=w, chunk=chunk):
                rows = pl.ds(pl.multiple_of(t * chunk, chunk), chunk)
                g = r_refs[w][0, rows, :].astype(F32)
                for k in range(1, N_DEV):
                    g = g + r_refs[w][k, rows, :].astype(F32)
                d, nm, nv = _adamw(w_refs[w][rows, :], g, m_refs[w][rows, :], v_refs[w][rows, :])
                g_refs[w][rows, :] = g
                d_refs[w][rows, :] = d
                nm_refs[w][rows, :] = nm
                nv_refs[w][rows, :] = nv
                return carry

            lax.fori_loop(0, W_BLOCKS[w][0] // chunk, step, 0)
        for k in range(1, N_DEV):
            tiny(k).wait_send()
            for w in range(N_W):
                big(w, k).wait_send()

    vm = pl.BlockSpec(memory_space=pltpu.VMEM)
    blk = [jax.ShapeDtypeStruct(shp, F32) for shp in W_BLOCKS]
    small_shape = jax.ShapeDtypeStruct((R_SMALL, LANES), F32)
    return pl.pallas_call(
        body, name="reduce_adamw",
        in_specs=[pl.BlockSpec(memory_space=pl.ANY)] * N_W + [vm] * 16,
        out_specs=[vm] * 20,
        out_shape=blk * 4 + [small_shape] * 4,
        scratch_shapes=[pltpu.VMEM((N_DEV,) + shp, BF16) for shp in W_BLOCKS]
        + [pltpu.VMEM((N_DEV, R_SMALL, LANES), F32),
           pltpu.SemaphoreType.DMA((N_DEV * N_W,)), pltpu.SemaphoreType.DMA((N_DEV * N_W,)),
           pltpu.SemaphoreType.DMA((N_DEV,)), pltpu.SemaphoreType.DMA((N_DEV,)), pltpu.SemaphoreType.DMA((N_W,))],
        compiler_params=pltpu.CompilerParams(vmem_limit_bytes=VMEM_LIMIT),
    )(*parts, small, *w_blk, *m_blk, *v_blk, w_s, m_s, v_s)


def _pack_shard(w_in, w_uq, w_ukv, w_out):
    return jnp.concatenate([a.reshape(-1, LANES) for a in (w_in, w_uq, w_ukv, w_out)], axis=0)


def _unpack_shard(p):
    w_in = p[0:R_WIN].reshape(1, D_MODEL, N_IN // N_DEV)
    w_uq = p[R_WIN:R_WIN + R_WUQ].reshape(1, B_Q_RANK // N_DEV, 384)
    w_ukv = p[R_WIN + R_WUQ:R_WIN + R_WUQ + R_WUKV].reshape(1, B_KV_RANK, 768 // N_DEV)
    w_out = p[R_WIN + R_WUQ + R_WUKV:].reshape(1, D_MODEL // N_DEV, D_MODEL)
    return w_in, w_uq, w_ukv, w_out


def _pack_small(norm_in, a_q, a_k, b_cq, b_ckv, b_q, b_k):
    def row(v):
        return jnp.pad(v.reshape(1, -1), ((0, 0), (0, LANES - v.size)))
    rows = [norm_in.reshape(8, LANES), row(a_q), row(a_k), b_cq.reshape(3, LANES), b_ckv.reshape(2, LANES),
            row(b_q), row(b_k), jnp.zeros((R_SMALL - 17, LANES), F32)]
    return jnp.concatenate(rows, axis=0)


def _unpack_small(s):
    return (s[0:8].reshape(1, D_MODEL), s[8:9, :64], s[9:10, :64], s[10:13].reshape(1, B_Q_RANK),
            s[13:15].reshape(1, B_KV_RANK), s[15:16, :B_QK_DIM], s[16:17, :B_QK_DIM])


def _full_weights(gathered):
    g = gathered
    w_in = g[:, 0:R_WIN].reshape(N_DEV, D_MODEL, N_IN // N_DEV).transpose(1, 0, 2).reshape(D_MODEL, N_IN)
    w_uq = g[:, R_WIN:R_WIN + R_WUQ].reshape(B_Q_RANK, 384)
    w_ukv = g[:, R_WIN + R_WUQ:R_WIN + R_WUQ + R_WUKV].reshape(N_DEV, B_KV_RANK, 768 // N_DEV)
    w_ukv = w_ukv.transpose(1, 0, 2).reshape(B_KV_RANK, 768)
    w_out = g[:, R_WIN + R_WUQ + R_WUKV:].reshape(D_MODEL, D_MODEL)
    k0, k1 = w_in[:, 512:576], w_in[:, 576:640]
    v0, v1 = w_in[:, 640:704], w_in[:, 704:768]
    kr = w_in[:, 1920:1952]
    z64 = jnp.zeros((D_MODEL, 64), BF16)
    z32 = jnp.zeros((D_MODEL, 32), BF16)
    kr_blk = jnp.concatenate([z64, kr, z32], axis=1)
    w_wide = jnp.concatenate([w_in[:, 0:512], k0, k0, k1, k1, v0, v0, v1, v1, w_in[:, 768:1280], w_in[:, 1280:1664],
                              w_in[:, 1664:1920], kr_blk, kr_blk, kr_blk, kr_blk, w_in[:, 1952:2464]], axis=1)
    wuq = jnp.pad(w_uq.reshape(B_Q_RANK, B_HEADS, B_QK_DIM), ((0, 0), (0, 0), (0, LANES - B_QK_DIM)))
    wuq = wuq.reshape(B_Q_RANK, 512)
    ukv = w_ukv.reshape(B_KV_RANK, B_HEADS, B_NOPE_DIM + B_V_DIM)
    wuk = jnp.pad(ukv[:, :, :B_NOPE_DIM], ((0, 0), (0, 0), (0, LANES - B_NOPE_DIM))).reshape(B_KV_RANK, 512)
    wuv = ukv[:, :, B_NOPE_DIM:].reshape(B_KV_RANK, 512)
    return w_wide, wuq, wuk, wuv, w_out


def _narrow_grads(dw_wide, dwuq, dwuk, dwuv):
    dw_in = jnp.concatenate([
        dw_wide[:, 0:512], dw_wide[:, 512:576], dw_wide[:, 640:704], dw_wide[:, 768:832], dw_wide[:, 896:960],
        dw_wide[:, 1024:1536], dw_wide[:, 1536:1920], dw_wide[:, 1920:2176],
        dw_wide[:, O_KR + 64:O_KR + 96], dw_wide[:, 2688:3200]], axis=1)
    dw_uq = dwuq.reshape(B_Q_RANK, B_HEADS, LANES)[:, :, :B_QK_DIM].reshape(B_Q_RANK, 384)
    dk = dwuk.reshape(B_KV_RANK, B_HEADS, LANES)[:, :, :B_NOPE_DIM]
    dv = dwuv.reshape(B_KV_RANK, B_HEADS, B_V_DIM)
    dw_ukv = jnp.concatenate([dk, dv], axis=2).reshape(B_KV_RANK, 768)
    return dw_in, dw_uq, dw_ukv


C_IN = N_IN // N_DEV
_RUNS = ((0, 512, O_QA), (512, 576, O_KA), (576, 640, O_KA + 128), (640, 704, O_VA), (704, 768, O_VA + 128),
         (768, 1280, O_GA), (1280, 1664, O_CQ), (1664, 1920, O_CKV), (1920, 1952, O_KR + 64), (1952, 2464, O_GB))


def _in_cols(g_in, lo, hi):
    out = []
    for d in range(N_DEV):
        a, b = max(lo, C_IN * d), min(hi, C_IN * (d + 1))
        if a < b:
            out.append(g_in[d][:, a - C_IN * d:b - C_IN * d])
    return out


def _widen_weights(g_in, g_uq, g_ukv, g_out):
    z64 = jnp.zeros((D_MODEL, 64), BF16)
    z32 = jnp.zeros((D_MODEL, 32), BF16)
    k0, k1 = _in_cols(g_in, 512, 576), _in_cols(g_in, 576, 640)
    v0, v1 = _in_cols(g_in, 640, 704), _in_cols(g_in, 704, 768)
    kr_blk = [z64] + _in_cols(g_in, 1920, 1952) + [z32]
    w_wide = jnp.concatenate(
        _in_cols(g_in, 0, 512) + k0 + k0 + k1 + k1 + v0 + v0 + v1 + v1 + _in_cols(g_in, 768, 1920)
        + kr_blk * B_HEADS + _in_cols(g_in, 1952, 2464), axis=1)
    w_uq = g_uq.reshape(B_Q_RANK, 384)
    wuq = jnp.pad(w_uq.reshape(B_Q_RANK, B_HEADS, B_QK_DIM), ((0, 0), (0, 0), (0, LANES - B_QK_DIM)))
    wuq = wuq.reshape(B_Q_RANK, 512)
    ukv = g_ukv.transpose(1, 0, 2).reshape(B_KV_RANK, B_HEADS, B_NOPE_DIM + B_V_DIM)
    wuk = jnp.pad(ukv[:, :, :B_NOPE_DIM], ((0, 0), (0, 0), (0, LANES - B_NOPE_DIM))).reshape(B_KV_RANK, 512)
    wuv = ukv[:, :, B_NOPE_DIM:].reshape(B_KV_RANK, 512)
    return w_wide, wuq, wuk, wuv, g_out.reshape(D_MODEL, D_MODEL)


def _grad_blocks(dw_wide, dwuq, dwuk, dwuv, dw_out):
    blocks = []
    for d in range(N_DEV):
        pieces = []
        for lo, hi, wide in _RUNS:
            a, b = max(lo, C_IN * d), min(hi, C_IN * (d + 1))
            if a < b:
                pieces.append(dw_wide[:, wide + a - lo:wide + b - lo])
        blocks.append(jnp.concatenate(pieces, axis=1))
    p_in = jnp.stack(blocks).astype(BF16)
    dw_uq = dwuq.reshape(B_Q_RANK, B_HEADS, LANES)[:, :, :B_QK_DIM].reshape(N_DEV, B_Q_RANK // N_DEV, 384)
    dk = dwuk.reshape(B_KV_RANK, B_HEADS, LANES)[:, :, :B_NOPE_DIM]
    dv = dwuv.reshape(B_KV_RANK, B_HEADS, B_V_DIM)
    dw_ukv = jnp.concatenate([dk, dv], axis=2).reshape(B_KV_RANK, N_DEV, 768 // N_DEV).transpose(1, 0, 2)
    return (p_in, dw_uq.astype(BF16), dw_ukv.astype(BF16),
            dw_out.reshape(N_DEV, D_MODEL // N_DEV, D_MODEL).astype(BF16))


def _rope_tables(s_len):
    row = jnp.arange(s_len // GRID_W, dtype=F32)
    col = jnp.arange(GRID_W, dtype=F32)

    def parts(dim):
        half = dim // 2
        inv = 1.0 / (ROPE_THETA ** (jnp.arange(0, half, 2, dtype=F32) / half))
        ar, ac = row[:, None] * inv[None, :], col[:, None] * inv[None, :]
        zr, zc = jnp.zeros_like(ar), jnp.zeros_like(ac)
        cos_c = jnp.concatenate([zc, zc, jnp.cos(ac), jnp.cos(ac)], axis=1)
        cos_r = jnp.concatenate([jnp.cos(ar), jnp.cos(ar), zr, zr], axis=1)
        sin_c = jnp.concatenate([zc, zc, -jnp.sin(ac), jnp.sin(ac)], axis=1)
        sin_r = jnp.concatenate([-jnp.sin(ar), jnp.sin(ar), zr, zr], axis=1)
        return cos_c, cos_r, sin_c, sin_r

    tabs = [jnp.tile(t, (1, 2)) for t in parts(A_HEAD_DIM)]
    for n, t in enumerate(parts(B_ROPE_DIM)):
        lead = jnp.full((t.shape[0], B_NOPE_DIM), 1.0 if n == 0 else 0.0, F32)
        tail = jnp.full((t.shape[0], LANES - B_QK_DIM), 1.0 if n == 0 else 0.0, F32)
        tabs.append(jnp.concatenate([lead, t, tail], axis=1))
    return tuple(tabs)


def _lane_major(a, step):
    return a[:, ::step].T[:, None, :]


def kernel(x, norm_in, w_in, a_q_norm, a_k_norm, b_cq_norm, b_ckv_norm, w_uq, w_ukv, b_q_norm, b_k_norm, w_out, loss_target, m_norm_in, m_w_in, m_a_q_norm, m_a_k_norm, m_b_cq_norm, m_b_ckv_norm, m_w_uq, m_w_ukv, m_b_q_norm, m_b_k_norm, m_w_out, v_norm_in, v_w_in, v_a_q_norm, v_a_k_norm, v_b_cq_norm, v_b_ckv_norm, v_w_uq, v_w_ukv, v_b_q_norm, v_b_k_norm, v_w_out):
    s_len = x.shape[1]
    tm = min(256, s_len)
    tq, tk = min(512, s_len), min(2048, s_len)
    ftq, ftk = min(256, s_len), min(1024, s_len)
    x2 = x.reshape(s_len, D_MODEL)
    t2 = loss_target.reshape(s_len, D_MODEL)

    w_blk = (w_in[0], w_uq[0], w_ukv[0], w_out[0])
    w_wide, wuq, wuk, wuv, wout = _widen_weights(*_gather_blocks_call(w_blk))

    def dup(v, pad_to=None):
        v = v.reshape(1, -1)
        if pad_to is None:
            return jnp.concatenate([v, v], axis=1)
        return jnp.pad(v, ((0, 0), (0, pad_to - v.shape[1])))

    g64 = jnp.asarray(np.kron(np.eye(2), np.ones((64, 64))), dtype=BF16)
    consts = dict(gin=norm_in, w=w_wide, wuq=wuq, wuk=wuk, wuv=wuv, gq=dup(a_q_norm), gk=dup(a_k_norm),
                  gcq=b_cq_norm, gckv=b_ckv_norm, gqb=dup(b_q_norm, LANES), gkb=dup(b_k_norm, LANES), g64=g64)
    tabs = _rope_tables(s_len)

    qa, ka, va, va_t, ga, qb, kb, vb, vb_t, gb, raw, xnb = _pre_fwd_call(x2, consts, tabs, min(512, s_len))
    oa, lse_a_t = _attn_fwd_t_call(qa, ka, va_t, groups=A_KV_HEADS, sub=4, masked=True, scale=None, tq=ftq, tk=ftk,
                                   name="attn_fwd_a")
    ob, lse_b_t = _attn_fwd_t_call(qb, kb, vb_t, groups=1, sub=B_HEADS, masked=False, scale=None, tq=ftq,
                                   tk=ftk, name="attn_fwd_b")
    dh, doa, dob, dga, dgb, dw_out, loss_row = _out_call(x2, t2, oa, ob, ga, gb, wout, min(512, s_len))

    dqa, dka_t, dva_t = _attn_bwd_q_call(qa, ka, va, doa, oa, lse_a_t, groups=A_KV_HEADS, sub=4, masked=True,
                                         tq=tq, tk=tk, ck=min(512, s_len), name="attn_bwd_a")
    dqb, dkb_t, dvb_t = _attn_bwd_q_call(qb, kb, vb, dob, ob, lse_b_t, groups=2, sub=2, masked=False,
                                         tq=tq, tk=tk, ck=min(256, s_len), name="attn_bwd_b")
    dx, dproj_b, dwuq, dwuk, dwuv, small = _pre_bwd_call(
        x2, raw, dh, dqa, dka_t, dva_t, dga, dqb, dkb_t, dvb_t, dgb, consts, tabs, tm)
    dw_wide = _dw_in_call(xnb, dproj_b, min(1024, s_len), N_WIDE)

    parts = _grad_blocks(dw_wide, dwuq, dwuk, dwuv, dw_out)
    small = jnp.concatenate([small[:ROW_LOSS], loss_row, small[ROW_LOSS + 1:]], axis=0)

    m_blk = (m_w_in[0], m_w_uq[0], m_w_ukv[0], m_w_out[0])
    v_blk = (v_w_in[0], v_w_uq[0], v_w_ukv[0], v_w_out[0])
    w_s = _pack_small(norm_in, a_q_norm, a_k_norm, b_cq_norm, b_ckv_norm, b_q_norm, b_k_norm)
    m_s = _pack_small(m_norm_in, m_a_q_norm, m_a_k_norm, m_b_cq_norm, m_b_ckv_norm, m_b_q_norm, m_b_k_norm)
    v_s = _pack_small(v_norm_in, v_a_q_norm, v_a_k_norm, v_b_cq_norm, v_b_ckv_norm, v_b_q_norm, v_b_k_norm)
    res = _reduce_two_level_call(parts, small, w_blk, m_blk, v_blk, w_s, m_s, v_s)

    def leaves(blocks, sm):
        wi, uq, ukv, wo = [b[None] for b in blocks]
        n_in, aq, ak, bcq, bckv, bq, bk = _unpack_small(sm)
        return [n_in, wi, aq, ak, bcq, bckv, uq, ukv, bq, bk, wo]

    g_s = res[16]
    loss = g_s[ROW_LOSS, 0]
    grad_x = dx.reshape(1, s_len, D_MODEL)
    return (loss, grad_x, *leaves(res[0:4], res[16]), *leaves(res[4:8], res[17]), *leaves(res[8:12], res[18]),
            *leaves(res[12:16], res[19]))
```

```python
import functools

import numpy as np
import jax
import jax.numpy as jnp
from jax import lax
from jax.experimental import pallas as pl
from jax.experimental.pallas import tpu as pltpu

F32 = jnp.float32
BF16 = jnp.bfloat16

D_MODEL = 1024
GRID_W = 64
ROPE_THETA = 10000.0
EPS = 1e-6
A_HEAD_DIM = 64
A_HEADS = 8
A_KV_HEADS = 2
B_HEADS = 4
B_NOPE_DIM = 64
B_ROPE_DIM = 32
B_QK_DIM = 96
B_V_DIM = 128
B_Q_RANK = 384
B_KV_RANK = 256
N_IN = 2464
N_DEV = 8

ADAM_LR = 0.001
ADAM_B1 = 0.9
ADAM_B2 = 0.999
ADAM_EPS = 1e-08
ADAM_WD = 0.01
ADAM_STEP = 10

QA_SCALE = 0.125
QB_SCALE = 1.0 / float(np.sqrt(B_QK_DIM))

LANES = 128
O_QA, O_KA, O_VA, O_GA, O_CQ, O_CKV, O_KR, O_GB, N_WIDE = 0, 512, 768, 1024, 1536, 1920, 2176, 2688, 3200
R_QA, R_KA, R_CQ, R_CKV, R_KR, R_WIDTH = 0, 512, 768, 1152, 1408, 1920

R_SMALL = 24
ROW_LOSS = 17

VMEM_LIMIT = 56 * 1024 * 1024

NT = (((1,), (1,)), ((), ()))
TN = (((0,), (0,)), ((), ()))


def _dot(a, b):
    return jnp.dot(a, b, preferred_element_type=F32)


def _dot_nt(a, b):
    return lax.dot_general(a, b, NT, preferred_element_type=F32)


def _dot_tn(a, b):
    return lax.dot_general(a, b, TN, preferred_element_type=F32)


def _params(sem=None):
    return pltpu.CompilerParams(dimension_semantics=sem, vmem_limit_bytes=VMEM_LIMIT)


def _full(shape):
    nd = len(shape)
    return pl.BlockSpec(shape, lambda *_: (0,) * nd)


def _swap_sel(rows, shift):
    lane = lax.broadcasted_iota(jnp.int32, (rows, LANES), 1)
    return pltpu.roll(lane, shift, 1) == (lane ^ shift)


def _swap(x, shift, sel):
    return jnp.where(sel, pltpu.roll(x, shift, 1), pltpu.roll(x, LANES - shift, 1))


def _group_sum64(x, g64):
    hi = x.astype(BF16)
    lo = (x - hi.astype(F32)).astype(BF16)
    return _dot(hi, g64) + _dot(lo, g64)


def _row_sum(x):
    return jnp.sum(x, axis=-1, keepdims=True)


def _col_fwd(xs, msum, denom, gain, cos, sin, shift, sel):
    r = lax.rsqrt(msum(xs * xs) * (1.0 / denom) + EPS)
    xh = xs * r
    n = xh * gain
    return n * cos + _swap(n, shift, sel) * sin, xh, r


def _col_bwd(d_out, xh, r, msum, denom, gain, cos, sin, shift, sel):
    dn = d_out * cos + _swap(d_out * sin, shift, sel)
    dgain = jnp.sum(dn * xh, axis=0, keepdims=True)
    dxh = dn * gain
    dx = r * (dxh - xh * (msum(dxh * xh) * (1.0 / denom)))
    return dx, dgain


def _rms_fwd(x, gain):
    r = lax.rsqrt(jnp.mean(x * x, axis=-1, keepdims=True) + EPS)
    xh = x * r
    return xh * gain, xh, r


def _rms_bwd(dy, xh, r, gain):
    dgain = jnp.sum(dy * xh, axis=0, keepdims=True)
    dxh = dy * gain
    dx = r * (dxh - xh * jnp.mean(dxh * xh, axis=-1, keepdims=True))
    return dx, dgain


def _pre_forward(x, gin, w, wuq, wuk, wuv, gq, gk, gcq, gckv, gqb, gkb, ca, sa, cb, sb, g64, tm, raw=None):
    sel16 = _swap_sel(tm, 16)
    sel8 = _swap_sel(tm, 8)
    xn, xh0, r0 = _rms_fwd(x, gin)
    xnb = xn.astype(BF16)
    proj = None
    if raw is None:
        proj = _dot(xnb, w)
        raw = jnp.concatenate([proj[:, O_QA:O_QA + 512], proj[:, O_KA:O_KA + 256], proj[:, O_CQ:O_CQ + B_Q_RANK],
                               proj[:, O_CKV:O_CKV + B_KV_RANK], proj[:, O_KR:O_KR + 512]], axis=1)
    gs64 = functools.partial(_group_sum64, g64=g64)
    qa = [_col_fwd(raw[:, R_QA + LANES * s:R_QA + LANES * (s + 1)], gs64, 64.0, gq, ca, sa, 16, sel16)
          for s in range(4)]
    ka = [_col_fwd(raw[:, R_KA + LANES * s:R_KA + LANES * (s + 1)], _row_sum, 128.0, gk, ca, sa, 16, sel16)
          for s in range(2)]
    cq, cqh, rcq = _rms_fwd(raw[:, R_CQ:R_CQ + B_Q_RANK], gcq)
    cqb = cq.astype(BF16)
    qb_raw = _dot(cqb, wuq)
    qb = [_col_fwd(qb_raw[:, LANES * h:LANES * (h + 1)], _row_sum, float(B_QK_DIM), gqb, cb, sb, 8, sel8)
          for h in range(B_HEADS)]
    ckv, ckvh, rckv = _rms_fwd(raw[:, R_CKV:R_CKV + B_KV_RANK], gckv)
    ckvb = ckv.astype(BF16)
    kb_raw = _dot(ckvb, wuk) + raw[:, R_KR:R_KR + 512]
    vb = _dot(ckvb, wuv)
    kb = [_col_fwd(kb_raw[:, LANES * h:LANES * (h + 1)], _row_sum, float(B_QK_DIM), gkb, cb, sb, 8, sel8)
          for h in range(B_HEADS)]
    return dict(xh0=xh0, r0=r0, xnb=xnb, proj=proj, raw=raw, qa=qa, ka=ka, cqh=cqh, rcq=rcq, cqb=cqb, qb=qb,
                ckvh=ckvh, rckv=rckv, ckvb=ckvb, kb=kb, vb=vb, sel16=sel16, sel8=sel8, gs64=gs64)


def _rope_tiles(tab_refs, i, tm):
    per_tile = tm // GRID_W
    out = []
    for t in range(4):
        col_ref, row_ref = tab_refs[2 * t], tab_refs[2 * t + 1]
        col = col_ref[...]
        out.append(jnp.concatenate([col + row_ref[pl.ds(i * per_tile + b, 1), :] for b in range(per_tile)], axis=0))
    return out


_PRE_IN_NAMES = ("gin", "w", "wuq", "wuk", "wuv", "gq", "gk", "gcq", "gckv", "gqb", "gkb", "g64")


def _pre_const_specs(consts):
    def spec(shape):
        nd = len(shape)
        return pl.BlockSpec(shape, lambda *_: (0,) * nd, pipeline_mode=pl.Buffered(1))
    return [spec(consts[n].shape) for n in _PRE_IN_NAMES]


def _pre_fwd_call(x, consts, tabs, tm):
    s_len = x.shape[0]
    ts = min(256, tm)

    def body(x_ref, *refs):
        tab_refs, refs = refs[:8], refs[8:]
        (gin_ref, w_ref, wuq_ref, wuk_ref, wuv_ref, gq_ref, gk_ref, gcq_ref, gckv_ref, gqb_ref, gkb_ref, g64_ref,
         qa_ref, ka_ref, va_ref, vat_ref, ga_ref, qb_ref, kb_ref, vb_ref, vbt_ref, gb_ref, raw_ref, xnb_ref) = refs
        for part in range(tm // ts):
            r = slice(ts * part, ts * (part + 1))
            ca, sa, cb, sb = _rope_tiles(tab_refs, pl.program_id(0) * (tm // ts) + part, ts)
            f = _pre_forward(x_ref[r, :], gin_ref[...], w_ref[...], wuq_ref[...], wuk_ref[...], wuv_ref[...],
                             gq_ref[...], gk_ref[...], gcq_ref[...], gckv_ref[...], gqb_ref[...], gkb_ref[...],
                             ca, sa, cb, sb, g64_ref[...], ts)
            proj = f["proj"]
            raw_ref[r, :] = f["raw"]
            xnb_ref[r, :] = f["xnb"]
            for s in range(4):
                qa_ref[r, LANES * s:LANES * (s + 1)] = (f["qa"][s][0] * QA_SCALE).astype(BF16)
            for s in range(2):
                ka_ref[r, LANES * s:LANES * (s + 1)] = f["ka"][s][0].astype(BF16)
            va = proj[:, O_VA:O_VA + 256]
            va_ref[r, :] = va.astype(BF16)
            vat_ref[:, r] = va.T.astype(BF16)
            ga_ref[r, :] = proj[:, O_GA:O_GA + 512]
            for h in range(B_HEADS):
                qb_ref[r, LANES * h:LANES * (h + 1)] = (f["qb"][h][0] * QB_SCALE).astype(BF16)
                kb_ref[r, LANES * h:LANES * (h + 1)] = f["kb"][h][0].astype(BF16)
            vb_ref[r, :] = f["vb"].astype(BF16)
            vbt_ref[:, r] = f["vb"].T.astype(BF16)
            gb_ref[r, :] = proj[:, O_GB:O_GB + 512]

    def rows(width):
        return pl.BlockSpec((tm, width), lambda i: (i, 0))

    def cols(height):
        return pl.BlockSpec((height, tm), lambda i: (0, i))

    outs = [((s_len, 512), BF16, rows(512)), ((s_len, 256), BF16, rows(256)), ((s_len, 256), BF16, rows(256)),
            ((256, s_len), BF16, cols(256)), ((s_len, 512), F32, rows(512)), ((s_len, 512), BF16, rows(512)),
            ((s_len, 512), BF16, rows(512)), ((s_len, 512), BF16, rows(512)), ((512, s_len), BF16, cols(512)),
            ((s_len, 512), F32, rows(512)), ((s_len, R_WIDTH), F32, rows(R_WIDTH)),
            ((s_len, D_MODEL), BF16, rows(D_MODEL))]
    return pl.pallas_call(
        body, name="pre_fwd", grid=(s_len // tm,),
        in_specs=[rows(D_MODEL)] + [_full(t.shape) for t in tabs] + _pre_const_specs(consts),
        out_specs=[sp for _, _, sp in outs],
        out_shape=[jax.ShapeDtypeStruct(sh, dt) for sh, dt, _ in outs],
        compiler_params=_params(("arbitrary",)),
    )(x, *tabs, *[consts[n] for n in _PRE_IN_NAMES])


def _head_masks(rows):
    lane = lax.broadcasted_iota(jnp.int32, (rows, LANES), 1)
    return lane < 64, lane >= 64


def _row_fold(x, op):
    return op(x.reshape(x.shape[0] // 8, 8, x.shape[1]), axis=0)


def _attn_fwd_t_call(q, k, vt, *, groups, sub, masked, scale, tq, tk, name, s_bufs=2):
    s_len = q.shape[0]
    qw = LANES * (sub // 2 if masked else sub)
    kvw = LANES if masked else LANES * sub
    n_c = s_len // tk
    kv_mode = pl.Buffered(1) if groups == 1 else None

    def body(q_ref, k_ref, vt_ref, o_ref, lse_ref, s_sc):
        keep = _head_masks(tq) if masked else None

        def kv_of(hh):
            return slice(0, LANES) if masked else slice(LANES * hh, LANES * (hh + 1))

        def q_of(hh):
            if not masked:
                return q_ref[:, LANES * hh:LANES * (hh + 1)]
            qp = q_ref[:, LANES * (hh // 2):LANES * (hh // 2 + 1)]
            return jnp.where(keep[hh % 2], qp, jnp.zeros_like(qp))

        def scores(hh, qm, c, mx):
            s_t = _dot_nt(k_ref[tk * c:tk * (c + 1), kv_of(hh)], qm)
            if scale is not None:
                s_t = s_t * scale
            s_sc[hh % s_bufs, c] = s_t
            return jnp.maximum(mx, _row_fold(s_t, jnp.max))

        neg = jnp.full((8, tq), -jnp.inf, F32)
        qm_next = q_of(0)
        mx_next = neg
        for c in range(n_c):
            mx_next = scores(0, qm_next, c, mx_next)
        outs = []
        for hh in range(sub):
            m = jnp.max(mx_next, axis=0, keepdims=True)
            if hh + 1 < sub:
                qm_next = q_of(hh + 1)
                mx_next = neg
            lsum = jnp.zeros((8, tq), F32)
            acc = jnp.zeros((LANES, tq), F32)
            for c in range(n_c):
                if hh + 1 < sub and s_bufs == 2:
                    mx_next = scores(hh + 1, qm_next, c, mx_next)
                p_t = jnp.exp(s_sc[hh % s_bufs, c] - m)
                if hh + 1 < sub and s_bufs == 1:
                    mx_next = scores(hh + 1, qm_next, c, mx_next)
                lsum = lsum + _row_fold(p_t, jnp.sum)
                acc = acc + _dot(vt_ref[kv_of(hh), tk * c:tk * (c + 1)], p_t.astype(BF16))
            l = jnp.sum(lsum, axis=0, keepdims=True)
            outs.append((acc / l).T)
            lse_ref[hh] = m + jnp.log(l)
        if masked:
            for pr in range(sub // 2):
                o_ref[:, LANES * pr:LANES * (pr + 1)] = jnp.where(keep[0], outs[2 * pr], outs[2 * pr + 1])
        else:
            for hh in range(sub):
                o_ref[:, LANES * hh:LANES * (hh + 1)] = outs[hh]

    return pl.pallas_call(
        body, name=name, grid=(groups, s_len // tq),
        in_specs=[pl.BlockSpec((tq, qw), lambda g, i: (i, g)),
                  pl.BlockSpec((s_len, kvw), lambda g, i: (0, g), pipeline_mode=kv_mode),
                  pl.BlockSpec((kvw, s_len), lambda g, i: (g, 0), pipeline_mode=kv_mode)],
        out_specs=[pl.BlockSpec((tq, qw), lambda g, i: (i, g)),
                   pl.BlockSpec((sub, 1, tq), lambda g, i: (g, 0, i))],
        out_shape=[jax.ShapeDtypeStruct((s_len, groups * qw), F32),
                   jax.ShapeDtypeStruct((groups * sub, 1, s_len), F32)],
        scratch_shapes=[pltpu.VMEM((s_bufs, n_c, tk, tq), F32)],
        compiler_params=_params(("arbitrary", "arbitrary")),
    )(q, k, vt)


def _attn_bwd_q_call(q, k, v, do, o, lse_t, *, groups, sub, masked, tq, tk, ck, name):
    s_len = q.shape[0]
    qw = LANES * (sub // 2 if masked else sub)
    kvw = LANES if masked else LANES * sub
    n_c = tk // ck

    def body(q_ref, k_ref, v_ref, do_ref, o_ref, lse_ref, dq_ref, dkt_ref, dvt_ref):
        j = pl.program_id(1)
        i = pl.program_id(2)

        @pl.when((j == 0) & (i == 0))
        def _():
            dq_ref[...] = jnp.zeros(dq_ref.shape, F32)

        @pl.when(i == 0)
        def _():
            dkt_ref[...] = jnp.zeros(dkt_ref.shape, F32)
            dvt_ref[...] = jnp.zeros(dvt_ref.shape, F32)

        lkeep = _head_masks(tq) if masked else None
        heads = []
        for hh in range(sub):
            if masked:
                cols = slice(LANES * (hh // 2), LANES * (hh // 2 + 1))
                kv = slice(0, LANES)
                qp, dop = q_ref[:, cols], do_ref[:, cols]
                qm = jnp.where(lkeep[hh % 2], qp, jnp.zeros_like(qp))
                dom = jnp.where(lkeep[hh % 2], dop, jnp.zeros_like(dop))
            else:
                cols = kv = slice(LANES * hh, LANES * (hh + 1))
                qm, dom = q_ref[:, cols], do_ref[:, cols]
            delta = jnp.sum(dom.astype(F32) * o_ref[:, cols], axis=1, keepdims=True)
            lse = jnp.broadcast_to(lse_ref[hh], (LANES, tq)).T[:, 0:1]
            heads.append((cols, kv, qm, dom, qm.T, dom.T, delta, lse))

        def products(hh, c):
            _, kv, qm, dom, _, _, _, _ = heads[hh]
            return _dot_nt(qm, k_ref[ck * c:ck * (c + 1), kv]), _dot_nt(dom, v_ref[ck * c:ck * (c + 1), kv])

        items = [(hh, c) for hh in range(sub) for c in range(n_c)]
        dq_acc = [jnp.zeros((tq, LANES), F32) for _ in range(sub)]
        nxt = products(*items[0])
        for n, (hh, c) in enumerate(items):
            s, dp = nxt
            if n + 1 < len(items):
                nxt = products(*items[n + 1])
            _, kv, qm, dom, qmt, domt, delta, lse = heads[hh]
            p = jnp.exp(s - lse)
            ds = p * (dp - delta)
            p_b = p.astype(BF16)
            ds_b = ds.astype(BF16)
            kcols = slice(ck * c, ck * (c + 1))
            dvt_ref[kv, kcols] += _dot(domt, p_b)
            dkt_ref[kv, kcols] += _dot(qmt, ds_b)
            dq_acc[hh] = dq_acc[hh] + _dot(ds_b, k_ref[kcols, kv])
        rows = pl.ds(pl.multiple_of(i * tq, tq), tq)
        if masked:
            for pr in range(sub // 2):
                dq_ref[rows, LANES * pr:LANES * (pr + 1)] += jnp.where(lkeep[0], dq_acc[2 * pr], dq_acc[2 * pr + 1])
        else:
            for hh in range(sub):
                dq_ref[rows, LANES * hh:LANES * (hh + 1)] += dq_acc[hh]

    return pl.pallas_call(
        body, name=name, grid=(groups, s_len // tk, s_len // tq),
        in_specs=[pl.BlockSpec((tq, qw), lambda g, j, i: (i, g)),
                  pl.BlockSpec((tk, kvw), lambda g, j, i: (j, g)),
                  pl.BlockSpec((tk, kvw), lambda g, j, i: (j, g)),
                  pl.BlockSpec((tq, qw), lambda g, j, i: (i, g)),
                  pl.BlockSpec((tq, qw), lambda g, j, i: (i, g)),
                  pl.BlockSpec((sub, 1, tq), lambda g, j, i: (g, 0, i))],
        out_specs=[pl.BlockSpec((s_len, qw), lambda g, j, i: (0, g)),
                   pl.BlockSpec((kvw, tk), lambda g, j, i: (g, j)),
                   pl.BlockSpec((kvw, tk), lambda g, j, i: (g, j))],
        out_shape=[jax.ShapeDtypeStruct((s_len, groups * qw), F32),
                   jax.ShapeDtypeStruct((groups * kvw, s_len), F32),
                   jax.ShapeDtypeStruct((groups * kvw, s_len), F32)],
        compiler_params=_params(("arbitrary", "arbitrary", "arbitrary")),
    )(q, k, v, do, o, lse_t)


def _attn_bwd_p_call(q, k, v, do, o, linv_t, p, *, groups, sub, masked, tq, tk, ck, name):
    s_len = q.shape[0]
    qw = LANES * (sub // 2 if masked else sub)
    kvw = LANES if masked else LANES * sub
    n_c = tk // ck

    def body(q_ref, k_ref, v_ref, do_ref, o_ref, linv_ref, p_ref, dq_ref, dkt_ref, dvt_ref):
        j = pl.program_id(1)
        i = pl.program_id(2)

        @pl.when((j == 0) & (i == 0))
        def _():
            dq_ref[...] = jnp.zeros(dq_ref.shape, F32)

        @pl.when(i == 0)
        def _():
            dkt_ref[...] = jnp.zeros(dkt_ref.shape, F32)
            dvt_ref[...] = jnp.zeros(dvt_ref.shape, F32)

        lkeep = _head_masks(tq) if masked else None
        heads = []
        for hh in range(sub):
            cols = slice(LANES * (hh // 2), LANES * (hh // 2 + 1)) if masked else slice(LANES * hh, LANES * (hh + 1))
            kv = slice(0, LANES) if masked else cols
            linv = jnp.broadcast_to(linv_ref[hh], (LANES, tq)).T
            qm = q_ref[:, cols]
            dof = do_ref[:, cols] * linv
            if masked:
                qm = jnp.where(lkeep[hh % 2], qm, jnp.zeros_like(qm))
                dof = jnp.where(lkeep[hh % 2], dof, jnp.zeros_like(dof))
            delta = jnp.sum(dof * o_ref[:, cols], axis=1, keepdims=True)
            dom = dof.astype(BF16)
            heads.append((kv, qm, dom, qm.T, dom.T, delta))

        def product(hh, c):
            kv, _, dom, _, _, _ = heads[hh]
            return _dot_nt(dom, v_ref[ck * c:ck * (c + 1), kv])

        items = [(hh, c) for hh in range(sub) for c in range(n_c)]
        dq_acc = [jnp.zeros((tq, LANES), F32) for _ in range(sub)]
        nxt = product(*items[0])
        for n, (hh, c) in enumerate(items):
            dp = nxt
            if n + 1 < len(items):
                nxt = product(*items[n + 1])
            kv, qm, dom, qmt, domt, delta = heads[hh]
            kcols = slice(ck * c, ck * (c + 1))
            p_b = p_ref[hh, :, kcols]
            ds_b = (p_b.astype(F32) * (dp - delta)).astype(BF16)
            dvt_ref[kv, kcols] += _dot(domt, p_b)
            dkt_ref[kv, kcols] += _dot(qmt, ds_b)
            dq_acc[hh] = dq_acc[hh] + _dot(ds_b, k_ref[kcols, kv])
        rows = pl.ds(pl.multiple_of(i * tq, tq), tq)
        if masked:
            for pr in range(sub // 2):
                dq_ref[rows, LANES * pr:LANES * (pr + 1)] += jnp.where(lkeep[0], dq_acc[2 * pr], dq_acc[2 * pr + 1])
        else:
            for hh in range(sub):
                dq_ref[rows, LANES * hh:LANES * (hh + 1)] += dq_acc[hh]

    return pl.pallas_call(
        body, name=name, grid=(groups, s_len // tk, s_len // tq),
        in_specs=[pl.BlockSpec((tq, qw), lambda g, j, i: (i, g)),
                  pl.BlockSpec((tk, kvw), lambda g, j, i: (j, g)),
                  pl.BlockSpec((tk, kvw), lambda g, j, i: (j, g)),
                  pl.BlockSpec((tq, qw), lambda g, j, i: (i, g)),
                  pl.BlockSpec((tq, qw), lambda g, j, i: (i, g)),
                  pl.BlockSpec((sub, 1, tq), lambda g, j, i: (g, 0, i)),
                  pl.BlockSpec((sub, tq, tk), lambda g, j, i: (g, i, j))],
        out_specs=[pl.BlockSpec((s_len, qw), lambda g, j, i: (0, g)),
                   pl.BlockSpec((kvw, tk), lambda g, j, i: (g, j)),
                   pl.BlockSpec((kvw, tk), lambda g, j, i: (g, j))],
        out_shape=[jax.ShapeDtypeStruct((s_len, groups * qw), F32),
                   jax.ShapeDtypeStruct((groups * kvw, s_len), F32),
                   jax.ShapeDtypeStruct((groups * kvw, s_len), F32)],
        compiler_params=_params(("arbitrary", "arbitrary", "arbitrary")),
    )(q, k, v, do, o, linv_t, p)


def _attn_bwd_call(q, k, v, do, lse_t, delta_t, *, groups, sub, scale, tq, tk, name):
    s_len = q.shape[0]
    masked = sub > 1
    qw = LANES * (sub // 2 if masked else 1)
    n_k = s_len // tk

    def body(q_ref, k_ref, v_ref, do_ref, lse_ref, dl_ref, dq_ref, dk_ref, dv_ref, dq_sc):
        i = pl.program_id(1)
        j = pl.program_id(2)

        @pl.when((i == 0) & (j == 0))
        def _():
            dk_ref[...] = jnp.zeros(dk_ref.shape, F32)
            dv_ref[...] = jnp.zeros(dv_ref.shape, F32)

        @pl.when(j == 0)
        def _():
            dq_sc[...] = jnp.zeros(dq_sc.shape, F32)

        kk = k_ref[...]
        vv = v_ref[...]
        keep = _head_masks(tq) if masked else None
        dk_t = jnp.zeros((tk, LANES), F32)
        dv_t = jnp.zeros((tk, LANES), F32)
        for hh in range(sub):
            if masked:
                cols = slice(LANES * (hh // 2), LANES * (hh // 2 + 1))
                qp = q_ref[:, cols]
                dop = do_ref[:, cols]
                qm = jnp.where(keep[hh % 2], qp, jnp.zeros_like(qp))
                dom = jnp.where(keep[hh % 2], dop, jnp.zeros_like(dop))
            else:
                cols = slice(0, LANES)
                qm = q_ref[...]
                dom = do_ref[...]
            s_t = _dot_nt(kk, qm)
            if scale is not None:
                s_t = s_t * scale
            p_t = jnp.exp(s_t - lse_ref[hh])
            dp_t = _dot_nt(vv, dom)
            ds_t = p_t * (dp_t - dl_ref[hh])
            if scale is not None:
                ds_t = ds_t * scale
            p_b = p_t.astype(BF16)
            ds_b = ds_t.astype(BF16)
            dv_t = dv_t + _dot(p_b, dom)
            dk_t = dk_t + _dot(ds_b, qm)
            dq_h = _dot_tn(ds_b, kk)
            if masked:
                dq_h = jnp.where(keep[hh % 2], dq_h, jnp.zeros_like(dq_h))
            dq_sc[:, cols] += dq_h
        rows = pl.ds(pl.multiple_of(j * tk, tk), tk)
        dk_ref[rows, :] += dk_t
        dv_ref[rows, :] += dv_t

        @pl.when(j == n_k - 1)
        def _():
            dq_ref[...] = dq_sc[...]

    return pl.pallas_call(
        body, name=name, grid=(groups, s_len // tq, n_k),
        in_specs=[pl.BlockSpec((tq, qw), lambda g, i, j: (i, g)),
                  pl.BlockSpec((tk, LANES), lambda g, i, j: (j, g)),
                  pl.BlockSpec((tk, LANES), lambda g, i, j: (j, g)),
                  pl.BlockSpec((tq, qw), lambda g, i, j: (i, g)),
                  pl.BlockSpec((sub, 1, tq), lambda g, i, j: (g, 0, i)),
                  pl.BlockSpec((sub, 1, tq), lambda g, i, j: (g, 0, i))],
        out_specs=[pl.BlockSpec((tq, qw), lambda g, i, j: (i, g)),
                   pl.BlockSpec((s_len, LANES), lambda g, i, j: (0, g)),
                   pl.BlockSpec((s_len, LANES), lambda g, i, j: (0, g))],
        out_shape=[jax.ShapeDtypeStruct((s_len, groups * qw), F32),
                   jax.ShapeDtypeStruct((s_len, groups * LANES), F32),
                   jax.ShapeDtypeStruct((s_len, groups * LANES), F32)],
        scratch_shapes=[pltpu.VMEM((tq, qw), F32)],
        compiler_params=_params(("arbitrary", "arbitrary", "arbitrary")),
    )(q, k, v, do, lse_t, delta_t)


def _silu_parts(g):
    sig = 1.0 / (1.0 + jnp.exp(-g))
    return g * sig, sig * (1.0 + g * (1.0 - sig))


def _out_call(x, target, oa, ob, ga, gb, wout, tm):
    s_len = x.shape[0]
    n_t = s_len // tm

    def body(x_ref, t_ref, oa_ref, ob_ref, ga_ref, gb_ref, w_ref,
             dh_ref, doa_ref, dob_ref, dga_ref, dgb_ref, dw_ref, loss_ref):
        i = pl.program_id(0)

        @pl.when(i == 0)
        def _():
            dw_ref[...] = jnp.zeros(dw_ref.shape, F32)
            loss_ref[...] = jnp.zeros(loss_ref.shape, F32)

        oa_v, ob_v = oa_ref[...], ob_ref[...]
        silu_a, dsilu_a = _silu_parts(ga_ref[...])
        silu_b, dsilu_b = _silu_parts(gb_ref[...])
        ya = (oa_v * silu_a).astype(BF16)
        yb = (ob_v * silu_b).astype(BF16)
        h = x_ref[...] + _dot(ya, w_ref[0:512, :]) + _dot(yb, w_ref[512:1024, :])
        err = h - t_ref[...]
        part = jnp.sum(err * err, axis=0, keepdims=True)
        acc = part[:, 0:LANES]
        for c in range(1, D_MODEL // LANES):
            acc = acc + part[:, LANES * c:LANES * (c + 1)]
        loss_ref[...] += acc
        dh = err * (1.0 / D_MODEL)
        dh_ref[...] = dh
        dhb = dh.astype(BF16)
        dya = _dot_nt(dhb, w_ref[0:512, :])
        dyb = _dot_nt(dhb, w_ref[512:1024, :])
        doa = dya * silu_a
        dob = dyb * silu_b
        doa_ref[...] = doa.astype(BF16)
        dob_ref[...] = dob.astype(BF16)
        dga_ref[...] = dya * oa_v * dsilu_a
        dgb_ref[...] = dyb * ob_v * dsilu_b
        dw_ref[0:512, :] += _dot_tn(ya, dhb)
        dw_ref[512:1024, :] += _dot_tn(yb, dhb)

    def rows(width):
        return pl.BlockSpec((tm, width), lambda i: (i, 0))

    outs = [(D_MODEL, F32), (512, BF16), (512, BF16), (512, F32), (512, F32)]
    return pl.pallas_call(
        body, name="out_fwd", grid=(n_t,),
        in_specs=[rows(D_MODEL), rows(D_MODEL), rows(512), rows(512), rows(512), rows(512),
                  _full((D_MODEL, D_MODEL))],
        out_specs=[rows(wd) for wd, _ in outs] + [_full((D_MODEL, D_MODEL)), _full((1, LANES))],
        out_shape=[jax.ShapeDtypeStruct((s_len, wd), dt) for wd, dt in outs]
        + [jax.ShapeDtypeStruct((D_MODEL, D_MODEL), F32), jax.ShapeDtypeStruct((1, LANES), F32)],
        compiler_params=_params(("arbitrary",)),
    )(x, target, oa, ob, ga, gb, wout)


def _pre_bwd_call(x, raw, dh, dqa, dka, dva, dga, dqb, dkb, dvb, dgb, consts, tabs, tm):
    s_len = x.shape[0]
    ts = min(256, tm)

    def body(x_ref, raw_ref, dh_ref, dqa_ref, dkat_ref, dvat_ref, dga_ref, dqb_ref, dkbt_ref, dvbt_ref, dgb_ref,
             *refs):
        tab_refs, refs = refs[:8], refs[8:]
        (gin_ref, w_ref, wuq_ref, wuk_ref, wuv_ref, gq_ref, gk_ref, gcq_ref, gckv_ref, gqb_ref, gkb_ref, g64_ref,
         dx_ref, dproj_ref, dwuq_ref, dwuk_ref, dwuv_ref, small_ref) = refs
        i = pl.program_id(0)

        @pl.when(i == 0)
        def _():
            dwuq_ref[...] = jnp.zeros(dwuq_ref.shape, F32)
            dwuk_ref[...] = jnp.zeros(dwuk_ref.shape, F32)
            dwuv_ref[...] = jnp.zeros(dwuv_ref.shape, F32)
            small_ref[...] = jnp.zeros(small_ref.shape, F32)

        gin, gq, gk = gin_ref[...], gq_ref[...], gk_ref[...]
        gcq, gckv, gqb, gkb = gcq_ref[...], gckv_ref[...], gqb_ref[...], gkb_ref[...]
        w, wuq, wuk, wuv = w_ref[...], wuq_ref[...], wuk_ref[...], wuv_ref[...]
        acc_uq, acc_uk, acc_uv, gains = [], [], [], []
        for part in range(tm // ts):
            r_ = slice(ts * part, ts * (part + 1))
            dka_v, dva_v = dkat_ref[:, r_].T, dvat_ref[:, r_].T
            dkb_v, dvb_v = dkbt_ref[:, r_].T, dvbt_ref[:, r_].T
            ca, sa, cb, sb = _rope_tiles(tab_refs, i * (tm // ts) + part, ts)
            f = _pre_forward(x_ref[r_, :], gin, w, wuq, wuk, wuv, gq, gk, gcq, gckv, gqb, gkb,
                             ca, sa, cb, sb, g64_ref[...], ts, raw=raw_ref[r_, :])
            sel16, sel8, gs64 = f["sel16"], f["sel8"], f["gs64"]
            lane = lax.broadcasted_iota(jnp.int32, (ts, LANES), 1)
            low = lane < 64
            zero = jnp.zeros((ts, LANES), F32)
            pieces = []

            dgq = jnp.zeros((1, LANES), F32)
            for s in range(4):
                _, xh, r = f["qa"][s]
                d = dqa_ref[r_, LANES * s:LANES * (s + 1)] * QA_SCALE
                dx, dg = _col_bwd(d, xh, r, gs64, 64.0, gq, ca, sa, 16, sel16)
                pieces.append(dx)
                dgq = dgq + dg
            dgk = jnp.zeros((1, LANES), F32)
            for s in range(2):
                _, xh, r = f["ka"][s]
                d = dka_v[:, LANES * s:LANES * (s + 1)]
                d = d + pltpu.roll(d, 64, 1)
                dx, dg = _col_bwd(d, xh, r, _row_sum, 128.0, gk, ca, sa, 16, sel16)
                pieces.append(jnp.where(low, dx, zero))
                dgk = dgk + dg
            for s in range(2):
                d = dva_v[:, LANES * s:LANES * (s + 1)]
                d = d + pltpu.roll(d, 64, 1)
                pieces.append(jnp.where(low, d, zero))
            pieces.append(dga_ref[r_, :])

            dgqb = jnp.zeros((1, LANES), F32)
            dq_cols = []
            for h in range(B_HEADS):
                _, xh, r = f["qb"][h]
                dx, dg = _col_bwd(dqb_ref[r_, LANES * h:LANES * (h + 1)] * QB_SCALE, xh, r, _row_sum,
                                  float(B_QK_DIM), gqb, cb, sb, 8, sel8)
                dq_cols.append(dx)
                dgqb = dgqb + dg
            dqr_b = jnp.concatenate(dq_cols, axis=1).astype(BF16)
            acc_uq.append(_dot_tn(f["cqb"], dqr_b))
            dcq_raw, dgcq = _rms_bwd(_dot_nt(dqr_b, wuq), f["cqh"], f["rcq"], gcq)
            pieces.append(dcq_raw)

            dgkb = jnp.zeros((1, LANES), F32)
            dk_cols = []
            dkr = zero
            for h in range(B_HEADS):
                _, xh, r = f["kb"][h]
                dx, dg = _col_bwd(dkb_v[:, LANES * h:LANES * (h + 1)], xh, r, _row_sum, float(B_QK_DIM),
                                  gkb, cb, sb, 8, sel8)
                dk_cols.append(dx)
                dkr = dkr + dx
                dgkb = dgkb + dg
            dkr_b = jnp.concatenate(dk_cols, axis=1).astype(BF16)
            dvb_b = dvb_v.astype(BF16)
            acc_uk.append(_dot_tn(f["ckvb"], dkr_b))
            acc_uv.append(_dot_tn(f["ckvb"], dvb_b))
            dckv = _dot_nt(dkr_b, wuk) + _dot_nt(dvb_b, wuv)
            dckv_raw, dgckv = _rms_bwd(dckv, f["ckvh"], f["rckv"], gckv)
            pieces.append(dckv_raw)
            pieces.append(jnp.where((lane >= B_NOPE_DIM) & (lane < B_QK_DIM), dkr, zero))
            pieces += [zero, zero, zero]
            pieces.append(dgb_ref[r_, :])

            dproj_b = jnp.concatenate(pieces, axis=1).astype(BF16)
            dproj_ref[r_, :] = dproj_b
            dxn = _dot_nt(dproj_b, w)
            dx, dgin = _rms_bwd(dxn, f["xh0"], f["r0"], gin)
            dx_ref[r_, :] = dx + dh_ref[r_, :]

            gains.append((dgin, dgq, dgk, dgcq, dgckv, dgqb, dgkb))

        dwuq_ref[...] += functools.reduce(jnp.add, acc_uq)
        dwuk_ref[...] += functools.reduce(jnp.add, acc_uk)
        dwuv_ref[...] += functools.reduce(jnp.add, acc_uv)
        dgin, dgq, dgk, dgcq, dgckv, dgqb, dgkb = [functools.reduce(jnp.add, v) for v in zip(*gains)]
        for c in range(D_MODEL // LANES):
            small_ref[c:c + 1, :] += dgin[:, LANES * c:LANES * (c + 1)]
        small_ref[8:9, :] += dgq
        small_ref[9:10, :] += dgk
        for c in range(3):
            small_ref[10 + c:11 + c, :] += dgcq[:, LANES * c:LANES * (c + 1)]
        for c in range(2):
            small_ref[13 + c:14 + c, :] += dgckv[:, LANES * c:LANES * (c + 1)]
        small_ref[15:16, :] += dgqb
        small_ref[16:17, :] += dgkb

    def rows(width):
        return pl.BlockSpec((tm, width), lambda i: (i, 0))

    def cols(height):
        return pl.BlockSpec((height, tm), lambda i: (0, i))

    return pl.pallas_call(
        body, name="pre_bwd", grid=(s_len // tm,),
        in_specs=[rows(D_MODEL), rows(R_WIDTH), rows(D_MODEL), rows(512), cols(256), cols(256), rows(512), rows(512),
                  cols(512), cols(512), rows(512)] + [_full(t.shape) for t in tabs] + _pre_const_specs(consts),
        out_specs=[rows(D_MODEL), rows(N_WIDE), _full((B_Q_RANK, 512)), _full((B_KV_RANK, 512)),
                   _full((B_KV_RANK, 512)), _full((R_SMALL, LANES))],
        out_shape=[jax.ShapeDtypeStruct((s_len, D_MODEL), F32), jax.ShapeDtypeStruct((s_len, N_WIDE), BF16),
                   jax.ShapeDtypeStruct((B_Q_RANK, 512), F32),
                   jax.ShapeDtypeStruct((B_KV_RANK, 512), F32), jax.ShapeDtypeStruct((B_KV_RANK, 512), F32),
                   jax.ShapeDtypeStruct((R_SMALL, LANES), F32)],
        compiler_params=_params(("arbitrary",)),
    )(x, raw, dh, dqa, dka, dva, dga, dqb, dkb, dvb, dgb, *tabs, *[consts[n] for n in _PRE_IN_NAMES])


def _dw_in_call(xnb, dproj_b, tt, tn):
    s_len = xnb.shape[0]

    def body(a_ref, b_ref, o_ref):
        @pl.when(pl.program_id(1) == 0)
        def _():
            o_ref[...] = jnp.zeros(o_ref.shape, F32)

        o_ref[...] += _dot_tn(a_ref[...], b_ref[...])

    return pl.pallas_call(
        body, name="dw_in", grid=(N_WIDE // tn, s_len // tt),
        in_specs=[pl.BlockSpec((tt, D_MODEL), lambda n, t: (t, 0)), pl.BlockSpec((tt, tn), lambda n, t: (t, n))],
        out_specs=pl.BlockSpec((D_MODEL, tn), lambda n, t: (0, n)),
        out_shape=jax.ShapeDtypeStruct((D_MODEL, N_WIDE), F32),
        compiler_params=_params(("arbitrary", "arbitrary")),
    )(xnb, dproj_b)


def _mesh_pos():
    return lax.axis_index("x"), lax.axis_index("y"), lax.axis_index("c")


def _flip(v, bit):
    return 1 - v if bit else v


def _peer(pos, k):
    x, y, c = pos
    return _flip(x, (k >> 2) & 1), _flip(y, (k >> 1) & 1), _flip(c, k & 1)


def _logical(p):
    return 4 * p[0] + 2 * p[1] + p[2]


def _gather_weights_call(shard):
    m_per = shard.shape[0]

    def body(x_ref, out_ref, xb_ref, send_sems, recv_sems, local_sem):
        x, y, c = _mesh_pos()
        me, sibling = (x, y, c), (x, y, 1 - c)
        chips = [(1 - x, y), (x, 1 - y), (1 - x, 1 - y)]
        xb_ref[...] = x_ref[...].astype(BF16)

        def rows(p):
            return out_ref.at[pl.ds(pl.multiple_of(_logical(p) * m_per, 16), m_per), :]

        def copy(k, block, to, src=None):
            return pltpu.make_async_remote_copy(
                src_ref=rows(block) if src is None else src, dst_ref=rows(block),
                send_sem=send_sems.at[k], recv_sem=recv_sems.at[k],
                device_id=to, device_id_type=pl.DeviceIdType.MESH)

        mine = pltpu.make_async_copy(xb_ref, rows(me), local_sem)
        mine.start()
        first = [copy(0, me, sibling, src=xb_ref)]
        first += [copy(1 + j, me, (*chip, c), src=xb_ref) for j, chip in enumerate(chips)]
        for cp in first:
            cp.start()
        passed = [copy(4 + j, (*chip, c), sibling) for j, chip in enumerate(chips)]
        for j, chip in enumerate(chips):
            copy(1 + j, (*chip, c), me).wait_recv()
            passed[j].start()
        copy(0, sibling, me).wait_recv()
        for j, chip in enumerate(chips):
            copy(4 + j, (*chip, 1 - c), me).wait_recv()
        for cp in first + passed:
            cp.wait_send()
        mine.wait()

    return pl.pallas_call(
        body, name="gather_weights",
        out_shape=jax.ShapeDtypeStruct((N_DEV * m_per, LANES), BF16),
        in_specs=[pl.BlockSpec(memory_space=pltpu.VMEM)],
        out_specs=pl.BlockSpec(memory_space=pltpu.VMEM),
        scratch_shapes=[pltpu.VMEM((m_per, LANES), BF16), pltpu.SemaphoreType.DMA((7,)),
                        pltpu.SemaphoreType.DMA((7,)), pltpu.SemaphoreType.DMA],
        compiler_params=pltpu.CompilerParams(vmem_limit_bytes=VMEM_LIMIT),
    )(shard)


def _adamw(w, g, m, v):
    m = ADAM_B1 * m + (1.0 - ADAM_B1) * g
    v = ADAM_B2 * v + (1.0 - ADAM_B2) * (g * g)
    m_hat = m / (1.0 - ADAM_B1 ** ADAM_STEP)
    v_hat = v / (1.0 - ADAM_B2 ** ADAM_STEP)
    delta = -ADAM_LR * (m_hat / (jnp.sqrt(v_hat) + ADAM_EPS) + ADAM_WD * w)
    return delta, m, v


def _reduce_adamw_call(parts, small, w_pk, m_pk, v_pk, w_s, m_s, v_s):
    chunk = 16
    n_chunks = R_PACK // chunk

    def body(parts_ref, small_ref, w_ref, m_ref, v_ref, ws_ref, ms_ref, vs_ref,
             g_ref, d_ref, nm_ref, nv_ref, gs_ref, ds_ref, nms_ref, nvs_ref,
             recv_ref, recv_s_ref, send_sems, recv_sems, send_s_sems, recv_s_sems, local_sem):
        pos = _mesh_pos()
        me = _logical(pos)

        def big(k):
            peer = _peer(pos, k)
            return pltpu.make_async_remote_copy(
                src_ref=parts_ref.at[_logical(peer)], dst_ref=recv_ref.at[k],
                send_sem=send_sems.at[k], recv_sem=recv_sems.at[k],
                device_id=peer, device_id_type=pl.DeviceIdType.MESH)

        def tiny(k):
            return pltpu.make_async_remote_copy(
                src_ref=small_ref, dst_ref=recv_s_ref.at[k],
                send_sem=send_s_sems.at[k], recv_sem=recv_s_sems.at[k],
                device_id=_peer(pos, k), device_id_type=pl.DeviceIdType.MESH)

        own = pltpu.make_async_copy(parts_ref.at[me], recv_ref.at[0], local_sem)
        own.start()
        for k in range(1, N_DEV):
            tiny(k).start()
        for k in range(1, N_DEV):
            big(k).start()
        recv_s_ref[0] = small_ref[...]
        for k in range(1, N_DEV):
            tiny(k).wait_recv()
        acc = recv_s_ref[me]
        for a in range(1, N_DEV):
            acc = acc + recv_s_ref[lax.bitwise_xor(me, a)]
        row = lax.broadcasted_iota(jnp.int32, (R_SMALL, LANES), 0)
        gs = jnp.where(row == 8, acc + pltpu.roll(acc, 64, 1), acc)
        gs = jnp.where(row == ROW_LOSS, jnp.sum(acc, axis=1, keepdims=True) * (0.5 / D_MODEL), gs)
        gs_ref[...] = gs
        ds, nms, nvs = _adamw(ws_ref[...], gs, ms_ref[...], vs_ref[...])
        ds_ref[...] = ds
        nms_ref[...] = nms
        nvs_ref[...] = nvs

        own.wait()
        for k in range(1, N_DEV):
            big(k).wait_recv()

        def step(t, carry):
            rows = pl.ds(pl.multiple_of(t * chunk, chunk), chunk)
            g = recv_ref[0, rows, :]
            for k in range(1, N_DEV):
                g = g + recv_ref[k, rows, :]
            d, nm, nv = _adamw(w_ref[rows, :], g, m_ref[rows, :], v_ref[rows, :])
            g_ref[rows, :] = g
            d_ref[rows, :] = d
            nm_ref[rows, :] = nm
            nv_ref[rows, :] = nv
            return carry

        lax.fori_loop(0, n_chunks, step, 0)
        for k in range(1, N_DEV):
            tiny(k).wait_send()
            big(k).wait_send()

    vm = pl.BlockSpec(memory_space=pltpu.VMEM)
    big_shape = jax.ShapeDtypeStruct((R_PACK, LANES), F32)
    small_shape = jax.ShapeDtypeStruct((R_SMALL, LANES), F32)
    return pl.pallas_call(
        body, name="reduce_adamw",
        in_specs=[pl.BlockSpec(memory_space=pl.ANY)] + [vm] * 7,
        out_specs=[vm] * 8,
        out_shape=[big_shape] * 4 + [small_shape] * 4,
        scratch_shapes=[pltpu.VMEM((N_DEV, R_PACK, LANES), F32), pltpu.VMEM((N_DEV, R_SMALL, LANES), F32),
                        pltpu.SemaphoreType.DMA((N_DEV,)), pltpu.SemaphoreType.DMA((N_DEV,)),
                        pltpu.SemaphoreType.DMA((N_DEV,)), pltpu.SemaphoreType.DMA((N_DEV,)),
                        pltpu.SemaphoreType.DMA],
        compiler_params=pltpu.CompilerParams(vmem_limit_bytes=VMEM_LIMIT),
    )(parts, small, w_pk, m_pk, v_pk, w_s, m_s, v_s)


W_BLOCKS = ((D_MODEL, N_IN // N_DEV), (B_Q_RANK // N_DEV, 384), (B_KV_RANK, 768 // N_DEV), (D_MODEL // N_DEV, D_MODEL))
N_W = len(W_BLOCKS)


def _gather_blocks_call(blocks):
    def body(*refs):
        x_refs, out_refs, xb_refs = refs[0:N_W], refs[N_W:2 * N_W], refs[2 * N_W:3 * N_W]
        send_sems, recv_sems, local_sems = refs[3 * N_W:]
        x, y, c = _mesh_pos()
        me, sibling = (x, y, c), (x, y, 1 - c)
        chips = [(1 - x, y), (x, 1 - y), (1 - x, 1 - y)]
        for w in range(N_W):
            xb_refs[w][...] = x_refs[w][...].astype(BF16)

        def slot(w, p):
            return out_refs[w].at[_logical(p)]

        def copy(w, k, block, to, src=None):
            return pltpu.make_async_remote_copy(
                src_ref=slot(w, block) if src is None else src, dst_ref=slot(w, block),
                send_sem=send_sems.at[N_W * k + w], recv_sem=recv_sems.at[N_W * k + w],
                device_id=to, device_id_type=pl.DeviceIdType.MESH)

        mine = [pltpu.make_async_copy(xb_refs[w], slot(w, me), local_sems.at[w]) for w in range(N_W)]
        for cp in mine:
            cp.start()
        first = [copy(w, 0, me, sibling, src=xb_refs[w]) for w in range(N_W)]
        first += [copy(w, 1 + j, me, (*chip, c), src=xb_refs[w]) for j, chip in enumerate(chips) for w in range(N_W)]
        for cp in first:
            cp.start()
        passed = []
        for j, chip in enumerate(chips):
            for w in range(N_W):
                copy(w, 1 + j, (*chip, c), me).wait_recv()
                fwd = copy(w, 4 + j, (*chip, c), sibling)
                fwd.start()
                passed.append(fwd)
        for w in range(N_W):
            copy(w, 0, sibling, me).wait_recv()
        for j, chip in enumerate(chips):
            for w in range(N_W):
                copy(w, 4 + j, (*chip, 1 - c), me).wait_recv()
        for cp in first + passed:
            cp.wait_send()
        for cp in mine:
            cp.wait()

    vm = pl.BlockSpec(memory_space=pltpu.VMEM)
    return pl.pallas_call(
        body, name="gather_weights",
        out_shape=[jax.ShapeDtypeStruct((N_DEV,) + shp, BF16) for shp in W_BLOCKS],
        in_specs=[vm] * N_W, out_specs=[vm] * N_W,
        scratch_shapes=[pltpu.VMEM(shp, BF16) for shp in W_BLOCKS]
        + [pltpu.SemaphoreType.DMA((7 * N_W,)), pltpu.SemaphoreType.DMA((7 * N_W,)), pltpu.SemaphoreType.DMA((N_W,))],
        compiler_params=pltpu.CompilerParams(vmem_limit_bytes=VMEM_LIMIT),
    )(*blocks)


def _reduce_two_level_call(parts, small, w_blk, m_blk, v_blk, w_s, m_s, v_s):
    chunks = (32, 48, 64, 16)
    n_chip = N_DEV // 2

    def body(*refs):
        p_refs = refs[0:4]
        small_ref = refs[4]
        w_refs, m_refs, v_refs = refs[5:9], refs[9:13], refs[13:17]
        ws_ref, ms_ref, vs_ref = refs[17:20]
        g_refs, d_refs, nm_refs, nv_refs = refs[20:24], refs[24:28], refs[28:32], refs[32:36]
        gs_ref, ds_ref, nms_ref, nvs_ref = refs[36:40]
        ra_refs, rb_refs, st_refs = refs[40:44], refs[44:48], refs[48:52]
        recv_s_ref = refs[52]
        send_a, recv_a, send_b, recv_b, send_s_sems, recv_s_sems = refs[53:59]
        pos = _mesh_pos()
        x, y, c = pos
        me = _logical(pos)
        sibling = (x, y, 1 - c)

        def chip(j):
            return _flip(x, j & 1), _flip(y, (j >> 1) & 1)

        def to_sibling(w, j):
            return pltpu.make_async_remote_copy(
                src_ref=p_refs[w].at[_logical((*chip(j), 1 - c))], dst_ref=ra_refs[w].at[j],
                send_sem=send_a.at[N_W * j + w], recv_sem=recv_a.at[N_W * j + w],
                device_id=sibling, device_id_type=pl.DeviceIdType.MESH)

        def to_chip(w, j):
            return pltpu.make_async_remote_copy(
                src_ref=st_refs[w].at[j - 1], dst_ref=rb_refs[w].at[j - 1],
                send_sem=send_b.at[N_W * (j - 1) + w], recv_sem=recv_b.at[N_W * (j - 1) + w],
                device_id=(*chip(j), c), device_id_type=pl.DeviceIdType.MESH)

        def tiny(k):
            return pltpu.make_async_remote_copy(
                src_ref=small_ref, dst_ref=recv_s_ref.at[k],
                send_sem=send_s_sems.at[k], recv_sem=recv_s_sems.at[k],
                device_id=_peer(pos, k), device_id_type=pl.DeviceIdType.MESH)

        order = (0, 3, 2, 1)
        for j in (1, 2, 3, 0):
            for w in order:
                to_sibling(w, j).start()
        for k in range(1, N_DEV):
            tiny(k).start()

        for j in (1, 2, 3):
            d = _logical((*chip(j), c))
            for w in order:
                to_sibling(w, j).wait_recv()
                chunk = chunks[w]

                def pair(t, carry, w=w, j=j, d=d, chunk=chunk):
                    rows = pl.ds(pl.multiple_of(t * chunk, chunk), chunk)
                    s = p_refs[w][d, rows, :].astype(F32) + ra_refs[w][j, rows, :].astype(F32)
                    st_refs[w][j - 1, rows, :] = s.astype(BF16)
                    return carry

                lax.fori_loop(0, W_BLOCKS[w][0] // chunk, pair, 0)
                to_chip(w, j).start()

        recv_s_ref[0] = small_ref[...]
        for k in range(1, N_DEV):
            tiny(k).wait_recv()
        acc = recv_s_ref[me]
        for a in range(1, N_DEV):
            acc = acc + recv_s_ref[lax.bitwise_xor(me, a)]
        row = lax.broadcasted_iota(jnp.int32, (R_SMALL, LANES), 0)
        gs = jnp.where(row == 8, acc + pltpu.roll(acc, 64, 1), acc)
        gs = jnp.where(row == ROW_LOSS, jnp.sum(acc, axis=1, keepdims=True) * (0.5 / D_MODEL), gs)
        gs_ref[...] = gs
        ds, nms, nvs = _adamw(ws_ref[...], gs, ms_ref[...], vs_ref[...])
        ds_ref[...] = ds
        nms_ref[...] = nms
        nvs_ref[...] = nvs

        for w in (1, 2, 3, 0):
            to_sibling(w, 0).wait_recv()
            for j in (1, 2, 3):
                to_chip(w, j).wait_recv()
            chunk = chunks[w]

            def step(t, carry, w=w, chunk=chunk):
                rows = pl.ds(pl.multiple_of(t * chunk, chunk), chunk)
                g = p_refs[w][me, rows, :].astype(F32) + ra_refs[w][0, rows, :].astype(F32)
                for j in range(n_chip - 1):
                    g = g + rb_refs[w][j, rows, :].astype(F32)
                d, nm, nv = _adamw(w_refs[w][rows, :], g, m_refs[w][rows, :], v_refs[w][rows, :])
                g_refs[w][rows, :] = g
                d_refs[w][rows, :] = d
                nm_refs[w][rows, :] = nm
                nv_refs[w][rows, :] = nv
                return carry

            lax.fori_loop(0, W_BLOCKS[w][0] // chunk, step, 0)
        for k in range(1, N_DEV):
            tiny(k).wait_send()
        for w in range(N_W):
            for j in range(n_chip):
                to_sibling(w, j).wait_send()
            for j in (1, 2, 3):
                to_chip(w, j).wait_send()

    vm = pl.BlockSpec(memory_space=pltpu.VMEM)
    blk = [jax.ShapeDtypeStruct(shp, F32) for shp in W_BLOCKS]
    small_shape = jax.ShapeDtypeStruct((R_SMALL, LANES), F32)
    return pl.pallas_call(
        body, name="reduce_adamw",
        in_specs=[vm] * 20, out_specs=[vm] * 20,
        out_shape=blk * 4 + [small_shape] * 4,
        scratch_shapes=[pltpu.VMEM((n_chip,) + shp, BF16) for shp in W_BLOCKS]
        + [pltpu.VMEM((n_chip - 1,) + shp, BF16) for shp in W_BLOCKS] * 2
        + [pltpu.VMEM((N_DEV, R_SMALL, LANES), F32),
           pltpu.SemaphoreType.DMA((n_chip * N_W,)), pltpu.SemaphoreType.DMA((n_chip * N_W,)),
           pltpu.SemaphoreType.DMA(((n_chip - 1) * N_W,)), pltpu.SemaphoreType.DMA(((n_chip - 1) * N_W,)),
           pltpu.SemaphoreType.DMA((N_DEV,)), pltpu.SemaphoreType.DMA((N_DEV,))],
        compiler_params=pltpu.CompilerParams(vmem_limit_bytes=VMEM_LIMIT),
    )(*parts, small, *w_blk, *m_blk, *v_blk, w_s, m_s, v_s)


def _reduce_blocks_call(parts, small, w_blk, m_blk, v_blk, w_s, m_s, v_s):
    chunks = (32, 48, 64, 16)

    def body(*refs):
        p_refs = refs[0:4]
        small_ref = refs[4]
        w_refs, m_refs, v_refs = refs[5:9], refs[9:13], refs[13:17]
        ws_ref, ms_ref, vs_ref = refs[17:20]
        g_refs, d_refs, nm_refs, nv_refs = refs[20:24], refs[24:28], refs[28:32], refs[32:36]
        gs_ref, ds_ref, nms_ref, nvs_ref = refs[36:40]
        r_refs = refs[40:44]
        recv_s_ref = refs[44]
        send_sems, recv_sems, send_s_sems, recv_s_sems, local_sems = refs[45:50]
        pos = _mesh_pos()
        me = _logical(pos)

        def big(w, k):
            peer = _peer(pos, k)
            return pltpu.make_async_remote_copy(
                src_ref=p_refs[w].at[_logical(peer)], dst_ref=r_refs[w].at[k],
                send_sem=send_sems.at[N_W * k + w], recv_sem=recv_sems.at[N_W * k + w],
                device_id=peer, device_id_type=pl.DeviceIdType.MESH)

        def tiny(k):
            return pltpu.make_async_remote_copy(
                src_ref=small_ref, dst_ref=recv_s_ref.at[k],
                send_sem=send_s_sems.at[k], recv_sem=recv_s_sems.at[k],
                device_id=_peer(pos, k), device_id_type=pl.DeviceIdType.MESH)

        own = [pltpu.make_async_copy(p_refs[w].at[me], r_refs[w].at[0], local_sems.at[w]) for w in range(N_W)]
        for cp in own:
            cp.start()
        for k in range(1, N_DEV):
            tiny(k).start()
        for k in range(1, N_DEV):
            for w in range(N_W):
                big(w, k).start()
        recv_s_ref[0] = small_ref[...]
        for k in range(1, N_DEV):
            tiny(k).wait_recv()
        acc = recv_s_ref[me]
        for a in range(1, N_DEV):
            acc = acc + recv_s_ref[lax.bitwise_xor(me, a)]
        row = lax.broadcasted_iota(jnp.int32, (R_SMALL, LANES), 0)
        gs = jnp.where(row == 8, acc + pltpu.roll(acc, 64, 1), acc)
        gs = jnp.where(row == ROW_LOSS, jnp.sum(acc, axis=1, keepdims=True) * (0.5 / D_MODEL), gs)
        gs_ref[...] = gs
        ds, nms, nvs = _adamw(ws_ref[...], gs, ms_ref[...], vs_ref[...])
        ds_ref[...] = ds
        nms_ref[...] = nms
        nvs_ref[...] = nvs

        for w in (3, 2, 1, 0):
            own[w].wait()
            for k in range(1, N_DEV):
                big(w, k).wait_recv()
            chunk = chunks[w]

            def step(t, carry, w=w, chunk=chunk):
                rows = pl.ds(pl.multiple_of(t * chunk, chunk), chunk)
                g = r_refs[w][0, rows, :].astype(F32)
                for k in range(1, N_DEV):
                    g = g + r_refs[w][k, rows, :].astype(F32)
                d, nm, nv = _adamw(w_refs[w][rows, :], g, m_refs[w][rows, :], v_refs[w][rows, :])
                g_refs[w][rows, :] = g
                d_refs[w][rows, :] = d
                nm_refs[w][rows, :] = nm
                nv_refs[w][rows, :] = nv
                return carry

            lax.fori_loop(0, W_BLOCKS[w][0] // chunk, step, 0)
        for k in range(1, N_DEV):
            tiny(k).wait_send()
            for w in range(N_W):
                big(w, k).wait_send()

    vm = pl.BlockSpec(memory_space=pltpu.VMEM)
    blk = [jax.ShapeDtypeStruct(shp, F32) for shp in W_BLOCKS]
    small_shape = jax.ShapeDtypeStruct((R_SMALL, LANES), F32)
    return pl.pallas_call(
        body, name="reduce_adamw",
        in_specs=[pl.BlockSpec(memory_space=pl.ANY)] * N_W + [vm] * 16,
        out_specs=[vm] * 20,
        out_shape=blk * 4 + [small_shape] * 4,
        scratch_shapes=[pltpu.VMEM((N_DEV,) + shp, BF16) for shp in W_BLOCKS]
        + [pltpu.VMEM((N_DEV, R_SMALL, LANES), F32),
           pltpu.SemaphoreType.DMA((N_DEV * N_W,)), pltpu.SemaphoreType.DMA((N_DEV * N_W,)),
           pltpu.SemaphoreType.DMA((N_DEV,)), pltpu.SemaphoreType.DMA((N_DEV,)), pltpu.SemaphoreType.DMA((N_W,))],
        compiler_params=pltpu.CompilerParams(vmem_limit_bytes=VMEM_LIMIT),
    )(*parts, small, *w_blk, *m_blk, *v_blk, w_s, m_s, v_s)


def _pack_shard(w_in, w_uq, w_ukv, w_out):
    return jnp.concatenate([a.reshape(-1, LANES) for a in (w_in, w_uq, w_ukv, w_out)], axis=0)


def _unpack_shard(p):
    w_in = p[0:R_WIN].reshape(1, D_MODEL, N_IN // N_DEV)
    w_uq = p[R_WIN:R_WIN + R_WUQ].reshape(1, B_Q_RANK // N_DEV, 384)
    w_ukv = p[R_WIN + R_WUQ:R_WIN + R_WUQ + R_WUKV].reshape(1, B_KV_RANK, 768 // N_DEV)
    w_out = p[R_WIN + R_WUQ + R_WUKV:].reshape(1, D_MODEL // N_DEV, D_MODEL)
    return w_in, w_uq, w_ukv, w_out


def _pack_small(norm_in, a_q, a_k, b_cq, b_ckv, b_q, b_k):
    def row(v):
        return jnp.pad(v.reshape(1, -1), ((0, 0), (0, LANES - v.size)))
    rows = [norm_in.reshape(8, LANES), row(a_q), row(a_k), b_cq.reshape(3, LANES), b_ckv.reshape(2, LANES),
            row(b_q), row(b_k), jnp.zeros((R_SMALL - 17, LANES), F32)]
    return jnp.concatenate(rows, axis=0)


def _unpack_small(s):
    return (s[0:8].reshape(1, D_MODEL), s[8:9, :64], s[9:10, :64], s[10:13].reshape(1, B_Q_RANK),
            s[13:15].reshape(1, B_KV_RANK), s[15:16, :B_QK_DIM], s[16:17, :B_QK_DIM])


def _full_weights(gathered):
    g = gathered
    w_in = g[:, 0:R_WIN].reshape(N_DEV, D_MODEL, N_IN // N_DEV).transpose(1, 0, 2).reshape(D_MODEL, N_IN)
    w_uq = g[:, R_WIN:R_WIN + R_WUQ].reshape(B_Q_RANK, 384)
    w_ukv = g[:, R_WIN + R_WUQ:R_WIN + R_WUQ + R_WUKV].reshape(N_DEV, B_KV_RANK, 768 // N_DEV)
    w_ukv = w_ukv.transpose(1, 0, 2).reshape(B_KV_RANK, 768)
    w_out = g[:, R_WIN + R_WUQ + R_WUKV:].reshape(D_MODEL, D_MODEL)
    k0, k1 = w_in[:, 512:576], w_in[:, 576:640]
    v0, v1 = w_in[:, 640:704], w_in[:, 704:768]
    kr = w_in[:, 1920:1952]
    z64 = jnp.zeros((D_MODEL, 64), BF16)
    z32 = jnp.zeros((D_MODEL, 32), BF16)
    kr_blk = jnp.concatenate([z64, kr, z32], axis=1)
    w_wide = jnp.concatenate([w_in[:, 0:512], k0, k0, k1, k1, v0, v0, v1, v1, w_in[:, 768:1280], w_in[:, 1280:1664],
                              w_in[:, 1664:1920], kr_blk, kr_blk, kr_blk, kr_blk, w_in[:, 1952:2464]], axis=1)
    wuq = jnp.pad(w_uq.reshape(B_Q_RANK, B_HEADS, B_QK_DIM), ((0, 0), (0, 0), (0, LANES - B_QK_DIM)))
    wuq = wuq.reshape(B_Q_RANK, 512)
    ukv = w_ukv.reshape(B_KV_RANK, B_HEADS, B_NOPE_DIM + B_V_DIM)
    wuk = jnp.pad(ukv[:, :, :B_NOPE_DIM], ((0, 0), (0, 0), (0, LANES - B_NOPE_DIM))).reshape(B_KV_RANK, 512)
    wuv = ukv[:, :, B_NOPE_DIM:].reshape(B_KV_RANK, 512)
    return w_wide, wuq, wuk, wuv, w_out


def _narrow_grads(dw_wide, dwuq, dwuk, dwuv):
    dw_in = jnp.concatenate([
        dw_wide[:, 0:512], dw_wide[:, 512:576], dw_wide[:, 640:704], dw_wide[:, 768:832], dw_wide[:, 896:960],
        dw_wide[:, 1024:1536], dw_wide[:, 1536:1920], dw_wide[:, 1920:2176],
        dw_wide[:, O_KR + 64:O_KR + 96], dw_wide[:, 2688:3200]], axis=1)
    dw_uq = dwuq.reshape(B_Q_RANK, B_HEADS, LANES)[:, :, :B_QK_DIM].reshape(B_Q_RANK, 384)
    dk = dwuk.reshape(B_KV_RANK, B_HEADS, LANES)[:, :, :B_NOPE_DIM]
    dv = dwuv.reshape(B_KV_RANK, B_HEADS, B_V_DIM)
    dw_ukv = jnp.concatenate([dk, dv], axis=2).reshape(B_KV_RANK, 768)
    return dw_in, dw_uq, dw_ukv


C_IN = N_IN // N_DEV
_RUNS = ((0, 512, O_QA), (512, 576, O_KA), (576, 640, O_KA + 128), (640, 704, O_VA), (704, 768, O_VA + 128),
         (768, 1280, O_GA), (1280, 1664, O_CQ), (1664, 1920, O_CKV), (1920, 1952, O_KR + 64), (1952, 2464, O_GB))


def _in_cols(g_in, lo, hi):
    out = []
    for d in range(N_DEV):
        a, b = max(lo, C_IN * d), min(hi, C_IN * (d + 1))
        if a < b:
            out.append(g_in[d][:, a - C_IN * d:b - C_IN * d])
    return out


def _widen_weights(g_in, g_uq, g_ukv, g_out):
    z64 = jnp.zeros((D_MODEL, 64), BF16)
    z32 = jnp.zeros((D_MODEL, 32), BF16)
    k0, k1 = _in_cols(g_in, 512, 576), _in_cols(g_in, 576, 640)
    v0, v1 = _in_cols(g_in, 640, 704), _in_cols(g_in, 704, 768)
    kr_blk = [z64] + _in_cols(g_in, 1920, 1952) + [z32]
    w_wide = jnp.concatenate(
        _in_cols(g_in, 0, 512) + k0 + k0 + k1 + k1 + v0 + v0 + v1 + v1 + _in_cols(g_in, 768, 1920)
        + kr_blk * B_HEADS + _in_cols(g_in, 1952, 2464), axis=1)
    w_uq = g_uq.reshape(B_Q_RANK, 384)
    wuq = jnp.pad(w_uq.reshape(B_Q_RANK, B_HEADS, B_QK_DIM), ((0, 0), (0, 0), (0, LANES - B_QK_DIM)))
    wuq = wuq.reshape(B_Q_RANK, 512)
    ukv = g_ukv.transpose(1, 0, 2).reshape(B_KV_RANK, B_HEADS, B_NOPE_DIM + B_V_DIM)
    wuk = jnp.pad(ukv[:, :, :B_NOPE_DIM], ((0, 0), (0, 0), (0, LANES - B_NOPE_DIM))).reshape(B_KV_RANK, 512)
    wuv = ukv[:, :, B_NOPE_DIM:].reshape(B_KV_RANK, 512)
    return w_wide, wuq, wuk, wuv, g_out.reshape(D_MODEL, D_MODEL)


def _grad_blocks(dw_wide, dwuq, dwuk, dwuv, dw_out):
    blocks = []
    for d in range(N_DEV):
        pieces = []
        for lo, hi, wide in _RUNS:
            a, b = max(lo, C_IN * d), min(hi, C_IN * (d + 1))
            if a < b:
                pieces.append(dw_wide[:, wide + a - lo:wide + b - lo])
        blocks.append(jnp.concatenate(pieces, axis=1))
    p_in = jnp.stack(blocks).astype(BF16)
    dw_uq = dwuq.reshape(B_Q_RANK, B_HEADS, LANES)[:, :, :B_QK_DIM].reshape(N_DEV, B_Q_RANK // N_DEV, 384)
    dk = dwuk.reshape(B_KV_RANK, B_HEADS, LANES)[:, :, :B_NOPE_DIM]
    dv = dwuv.reshape(B_KV_RANK, B_HEADS, B_V_DIM)
    dw_ukv = jnp.concatenate([dk, dv], axis=2).reshape(B_KV_RANK, N_DEV, 768 // N_DEV).transpose(1, 0, 2)
    return (p_in, dw_uq.astype(BF16), dw_ukv.astype(BF16),
            dw_out.reshape(N_DEV, D_MODEL // N_DEV, D_MODEL).astype(BF16))


def _rope_tables(s_len):
    row = jnp.arange(s_len // GRID_W, dtype=F32)
    col = jnp.arange(GRID_W, dtype=F32)

    def parts(dim):
        half = dim // 2
        inv = 1.0 / (ROPE_THETA ** (jnp.arange(0, half, 2, dtype=F32) / half))
        ar, ac = row[:, None] * inv[None, :], col[:, None] * inv[None, :]
        zr, zc = jnp.zeros_like(ar), jnp.zeros_like(ac)
        cos_c = jnp.concatenate([zc, zc, jnp.cos(ac), jnp.cos(ac)], axis=1)
        cos_r = jnp.concatenate([jnp.cos(ar), jnp.cos(ar), zr, zr], axis=1)
        sin_c = jnp.concatenate([zc, zc, -jnp.sin(ac), jnp.sin(ac)], axis=1)
        sin_r = jnp.concatenate([-jnp.sin(ar), jnp.sin(ar), zr, zr], axis=1)
        return cos_c, cos_r, sin_c, sin_r

    tabs = [jnp.tile(t, (1, 2)) for t in parts(A_HEAD_DIM)]
    for n, t in enumerate(parts(B_ROPE_DIM)):
        lead = jnp.full((t.shape[0], B_NOPE_DIM), 1.0 if n == 0 else 0.0, F32)
        tail = jnp.full((t.shape[0], LANES - B_QK_DIM), 1.0 if n == 0 else 0.0, F32)
        tabs.append(jnp.concatenate([lead, t, tail], axis=1))
    return tuple(tabs)


def _lane_major(a, step):
    return a[:, ::step].T[:, None, :]


def kernel(x, norm_in, w_in, a_q_norm, a_k_norm, b_cq_norm, b_ckv_norm, w_uq, w_ukv, b_q_norm, b_k_norm, w_out, loss_target, m_norm_in, m_w_in, m_a_q_norm, m_a_k_norm, m_b_cq_norm, m_b_ckv_norm, m_w_uq, m_w_ukv, m_b_q_norm, m_b_k_norm, m_w_out, v_norm_in, v_w_in, v_a_q_norm, v_a_k_norm, v_b_cq_norm, v_b_ckv_norm, v_w_uq, v_w_ukv, v_b_q_norm, v_b_k_norm, v_w_out):
    s_len = x.shape[1]
    tm = min(256, s_len)
    tq, tk = min(512, s_len), min(2048, s_len)
    ftq, ftk, fbufs = min(256, s_len), min(1024, s_len), 2
    x2 = x.reshape(s_len, D_MODEL)
    t2 = loss_target.reshape(s_len, D_MODEL)

    w_blk = (w_in[0], w_uq[0], w_ukv[0], w_out[0])
    w_wide, wuq, wuk, wuv, wout = _widen_weights(*_gather_blocks_call(w_blk))

    def dup(v, pad_to=None):
        v = v.reshape(1, -1)
        if pad_to is None:
            return jnp.concatenate([v, v], axis=1)
        return jnp.pad(v, ((0, 0), (0, pad_to - v.shape[1])))

    g64 = jnp.asarray(np.kron(np.eye(2), np.ones((64, 64))), dtype=BF16)
    consts = dict(gin=norm_in, w=w_wide, wuq=wuq, wuk=wuk, wuv=wuv, gq=dup(a_q_norm), gk=dup(a_k_norm),
                  gcq=b_cq_norm, gckv=b_ckv_norm, gqb=dup(b_q_norm, LANES), gkb=dup(b_k_norm, LANES), g64=g64)
    tabs = _rope_tables(s_len)

    qa, ka, va, va_t, ga, qb, kb, vb, vb_t, gb, raw, xnb = _pre_fwd_call(x2, consts, tabs, min(512, s_len))
    oa, lse_a_t = _attn_fwd_t_call(qa, ka, va_t, groups=A_KV_HEADS, sub=4, masked=True, scale=None, tq=ftq, tk=ftk,
                                   name="attn_fwd_a", s_bufs=fbufs)
    ob, lse_b_t = _attn_fwd_t_call(qb, kb, vb_t, groups=1, sub=B_HEADS, masked=False, scale=None, tq=ftq,
                                   tk=ftk, name="attn_fwd_b", s_bufs=fbufs)
    dh, doa, dob, dga, dgb, dw_out, loss_row = _out_call(x2, t2, oa, ob, ga, gb, wout, min(512, s_len))

    dqa, dka_t, dva_t = _attn_bwd_q_call(qa, ka, va, doa, oa, lse_a_t, groups=A_KV_HEADS, sub=4, masked=True,
                                         tq=tq, tk=s_len, ck=min(512, s_len), name="attn_bwd_a")
    dqb, dkb_t, dvb_t = _attn_bwd_q_call(qb, kb, vb, dob, ob, lse_b_t, groups=2, sub=2, masked=False,
                                         tq=tq, tk=min(4096, s_len), ck=min(256, s_len), name="attn_bwd_b")
    dx, dproj_b, dwuq, dwuk, dwuv, small = _pre_bwd_call(
        x2, raw, dh, dqa, dka_t, dva_t, dga, dqb, dkb_t, dvb_t, dgb, consts, tabs, tm)
    dw_wide = _dw_in_call(xnb, dproj_b, min(1024, s_len), N_WIDE)

    parts = _grad_blocks(dw_wide, dwuq, dwuk, dwuv, dw_out)
    small = jnp.concatenate([small[:ROW_LOSS], loss_row, small[ROW_LOSS + 1:]], axis=0)

    m_blk = (m_w_in[0], m_w_uq[0], m_w_ukv[0], m_w_out[0])
    v_blk = (v_w_in[0], v_w_uq[0], v_w_ukv[0], v_w_out[0])
    w_s = _pack_small(norm_in, a_q_norm, a_k_norm, b_cq_norm, b_ckv_norm, b_q_norm, b_k_norm)
    m_s = _pack_small(m_norm_in, m_a_q_norm, m_a_k_norm, m_b_cq_norm, m_b_ckv_norm, m_b_q_norm, m_b_k_norm)
    v_s = _pack_small(v_norm_in, v_a_q_norm, v_a_k_norm, v_b_cq_norm, v_b_ckv_norm, v_b_q_norm, v_b_k_norm)
    res = _reduce_two_level_call(parts, small, w_blk, m_blk, v_blk, w_s, m_s, v_s)

    def leaves(blocks, sm):
        wi, uq, ukv, wo = [b[None] for b in blocks]
        n_in, aq, ak, bcq, bckv, bq, bk = _unpack_small(sm)
        return [n_in, wi, aq, ak, bcq, bckv, uq, ukv, bq, bk, wo]

    g_s = res[16]
    loss = g_s[ROW_LOSS, 0]
    grad_x = dx.reshape(1, s_len, D_MODEL)
    return (loss, grad_x, *leaves(res[0:4], res[16]), *leaves(res[4:8], res[17]), *leaves(res[8:12], res[18]),
            *leaves(res[12:16], res[19]))
```

```python
import functools

import numpy as np
import jax
import jax.numpy as jnp
from jax import lax
from jax.experimental import pallas as pl
from jax.experimental.pallas import tpu as pltpu

F32 = jnp.float32
BF16 = jnp.bfloat16

D_MODEL = 1024
GRID_W = 64
ROPE_THETA = 10000.0
EPS = 1e-6
A_HEAD_DIM = 64
A_HEADS = 8
A_KV_HEADS = 2
B_HEADS = 4
B_NOPE_DIM = 64
B_ROPE_DIM = 32
B_QK_DIM = 96
B_V_DIM = 128
B_Q_RANK = 384
B_KV_RANK = 256
N_IN = 2464
N_DEV = 8

ADAM_LR = 0.001
ADAM_B1 = 0.9
ADAM_B2 = 0.999
ADAM_EPS = 1e-08
ADAM_WD = 0.01
ADAM_STEP = 10

QA_SCALE = 0.125
QB_SCALE = 1.0 / float(np.sqrt(B_QK_DIM))

LANES = 128
O_QA, O_KA, O_VA, O_GA, O_CQ, O_CKV, O_KR, O_GB, N_WIDE = 0, 512, 768, 1024, 1536, 1920, 2176, 2688, 3200
R_QA, R_KA, R_CQ, R_CKV, R_KR, R_WIDTH = 0, 512, 768, 1152, 1408, 1920

R_SMALL = 24
ROW_LOSS = 17

VMEM_LIMIT = 56 * 1024 * 1024

NT = (((1,), (1,)), ((), ()))
TN = (((0,), (0,)), ((), ()))


def _dot(a, b):
    return jnp.dot(a, b, preferred_element_type=F32)


def _dot_nt(a, b):
    return lax.dot_general(a, b, NT, preferred_element_type=F32)


def _dot_tn(a, b):
    return lax.dot_general(a, b, TN, preferred_element_type=F32)


def _params(sem=None):
    return pltpu.CompilerParams(dimension_semantics=sem, vmem_limit_bytes=VMEM_LIMIT)


def _full(shape):
    nd = len(shape)
    return pl.BlockSpec(shape, lambda *_: (0,) * nd)


def _swap_sel(rows, shift):
    lane = lax.broadcasted_iota(jnp.int32, (rows, LANES), 1)
    return pltpu.roll(lane, shift, 1) == (lane ^ shift)


def _swap(x, shift, sel):
    return jnp.where(sel, pltpu.roll(x, shift, 1), pltpu.roll(x, LANES - shift, 1))


def _group_sum64(x, g64):
    hi = x.astype(BF16)
    lo = (x - hi.astype(F32)).astype(BF16)
    return _dot(hi, g64) + _dot(lo, g64)


def _row_sum(x):
    return jnp.sum(x, axis=-1, keepdims=True)


def _col_fwd(xs, msum, denom, gain, cos, sin, shift, sel):
    r = lax.rsqrt(msum(xs * xs) * (1.0 / denom) + EPS)
    xh = xs * r
    n = xh * gain
    return n * cos + _swap(n, shift, sel) * sin, xh, r


def _col_bwd(d_out, xh, r, msum, denom, gain, cos, sin, shift, sel):
    dn = d_out * cos + _swap(d_out * sin, shift, sel)
    dgain = jnp.sum(dn * xh, axis=0, keepdims=True)
    dxh = dn * gain
    dx = r * (dxh - xh * (msum(dxh * xh) * (1.0 / denom)))
    return dx, dgain


def _rms_fwd(x, gain):
    r = lax.rsqrt(jnp.mean(x * x, axis=-1, keepdims=True) + EPS)
    xh = x * r
    return xh * gain, xh, r


def _rms_bwd(dy, xh, r, gain):
    dgain = jnp.sum(dy * xh, axis=0, keepdims=True)
    dxh = dy * gain
    dx = r * (dxh - xh * jnp.mean(dxh * xh, axis=-1, keepdims=True))
    return dx, dgain


def _pre_forward(x, gin, w, wuq, wuk, wuv, gq, gk, gcq, gckv, gqb, gkb, ca, sa, cb, sb, g64, tm, raw=None):
    sel16 = _swap_sel(tm, 16)
    sel8 = _swap_sel(tm, 8)
    xn, xh0, r0 = _rms_fwd(x, gin)
    xnb = xn.astype(BF16)
    proj = None
    if raw is None:
        proj = _dot(xnb, w)
        raw = jnp.concatenate([proj[:, O_QA:O_QA + 512], proj[:, O_KA:O_KA + 256], proj[:, O_CQ:O_CQ + B_Q_RANK],
                               proj[:, O_CKV:O_CKV + B_KV_RANK], proj[:, O_KR:O_KR + 512]], axis=1)
    gs64 = functools.partial(_group_sum64, g64=g64)
    qa = [_col_fwd(raw[:, R_QA + LANES * s:R_QA + LANES * (s + 1)], gs64, 64.0, gq, ca, sa, 16, sel16)
          for s in range(4)]
    ka = [_col_fwd(raw[:, R_KA + LANES * s:R_KA + LANES * (s + 1)], _row_sum, 128.0, gk, ca, sa, 16, sel16)
          for s in range(2)]
    cq, cqh, rcq = _rms_fwd(raw[:, R_CQ:R_CQ + B_Q_RANK], gcq)
    cqb = cq.astype(BF16)
    qb_raw = _dot(cqb, wuq)
    qb = [_col_fwd(qb_raw[:, LANES * h:LANES * (h + 1)], _row_sum, float(B_QK_DIM), gqb, cb, sb, 8, sel8)
          for h in range(B_HEADS)]
    ckv, ckvh, rckv = _rms_fwd(raw[:, R_CKV:R_CKV + B_KV_RANK], gckv)
    ckvb = ckv.astype(BF16)
    kb_raw = _dot(ckvb, wuk) + raw[:, R_KR:R_KR + 512]
    vb = _dot(ckvb, wuv)
    kb = [_col_fwd(kb_raw[:, LANES * h:LANES * (h + 1)], _row_sum, float(B_QK_DIM), gkb, cb, sb, 8, sel8)
          for h in range(B_HEADS)]
    return dict(xh0=xh0, r0=r0, xnb=xnb, proj=proj, raw=raw, qa=qa, ka=ka, cqh=cqh, rcq=rcq, cqb=cqb, qb=qb,
                ckvh=ckvh, rckv=rckv, ckvb=ckvb, kb=kb, vb=vb, sel16=sel16, sel8=sel8, gs64=gs64)


def _rope_tiles(tab_refs, i, tm):
    per_tile = tm // GRID_W
    out = []
    for t in range(4):
        col_ref, row_ref = tab_refs[2 * t], tab_refs[2 * t + 1]
        col = col_ref[...]
        out.append(jnp.concatenate([col + row_ref[pl.ds(i * per_tile + b, 1), :] for b in range(per_tile)], axis=0))
    return out


_PRE_IN_NAMES = ("gin", "w", "wuq", "wuk", "wuv", "gq", "gk", "gcq", "gckv", "gqb", "gkb", "g64")


def _pre_const_specs(consts):
    return [_full(consts[n].shape) for n in _PRE_IN_NAMES]


def _pre_fwd_call(x, consts, tabs, tm):
    s_len = x.shape[0]
    ts = min(256, tm)

    def body(x_ref, *refs):
        tab_refs, refs = refs[:8], refs[8:]
        (gin_ref, w_ref, wuq_ref, wuk_ref, wuv_ref, gq_ref, gk_ref, gcq_ref, gckv_ref, gqb_ref, gkb_ref, g64_ref,
         qa_ref, ka_ref, va_ref, vat_ref, ga_ref, qb_ref, kb_ref, vb_ref, vbt_ref, gb_ref, raw_ref, xnb_ref) = refs
        for part in range(tm // ts):
            r = slice(ts * part, ts * (part + 1))
            ca, sa, cb, sb = _rope_tiles(tab_refs, pl.program_id(0) * (tm // ts) + part, ts)
            f = _pre_forward(x_ref[r, :], gin_ref[...], w_ref[...], wuq_ref[...], wuk_ref[...], wuv_ref[...],
                             gq_ref[...], gk_ref[...], gcq_ref[...], gckv_ref[...], gqb_ref[...], gkb_ref[...],
                             ca, sa, cb, sb, g64_ref[...], ts)
            proj = f["proj"]
            raw_ref[r, :] = f["raw"]
            xnb_ref[r, :] = f["xnb"]
            for s in range(4):
                qa_ref[r, LANES * s:LANES * (s + 1)] = (f["qa"][s][0] * QA_SCALE).astype(BF16)
            for s in range(2):
                ka_ref[r, LANES * s:LANES * (s + 1)] = f["ka"][s][0].astype(BF16)
            va = proj[:, O_VA:O_VA + 256]
            va_ref[r, :] = va.astype(BF16)
            vat_ref[:, r] = va.T.astype(BF16)
            ga_ref[r, :] = proj[:, O_GA:O_GA + 512]
            for h in range(B_HEADS):
                qb_ref[r, LANES * h:LANES * (h + 1)] = (f["qb"][h][0] * QB_SCALE).astype(BF16)
                kb_ref[r, LANES * h:LANES * (h + 1)] = f["kb"][h][0].astype(BF16)
            vb_ref[r, :] = f["vb"].astype(BF16)
            vbt_ref[:, r] = f["vb"].T.astype(BF16)
            gb_ref[r, :] = proj[:, O_GB:O_GB + 512]

    def rows(width):
        return pl.BlockSpec((tm, width), lambda i: (i, 0))

    def cols(height):
        return pl.BlockSpec((height, tm), lambda i: (0, i))

    outs = [((s_len, 512), BF16, rows(512)), ((s_len, 256), BF16, rows(256)), ((s_len, 256), BF16, rows(256)),
            ((256, s_len), BF16, cols(256)), ((s_len, 512), F32, rows(512)), ((s_len, 512), BF16, rows(512)),
            ((s_len, 512), BF16, rows(512)), ((s_len, 512), BF16, rows(512)), ((512, s_len), BF16, cols(512)),
            ((s_len, 512), F32, rows(512)), ((s_len, R_WIDTH), F32, rows(R_WIDTH)),
            ((s_len, D_MODEL), BF16, rows(D_MODEL))]
    return pl.pallas_call(
        body, name="pre_fwd", grid=(s_len // tm,),
        in_specs=[rows(D_MODEL)] + [_full(t.shape) for t in tabs] + _pre_const_specs(consts),
        out_specs=[sp for _, _, sp in outs],
        out_shape=[jax.ShapeDtypeStruct(sh, dt) for sh, dt, _ in outs],
        compiler_params=_params(("arbitrary",)),
    )(x, *tabs, *[consts[n] for n in _PRE_IN_NAMES])


def _head_masks(rows):
    lane = lax.broadcasted_iota(jnp.int32, (rows, LANES), 1)
    return lane < 64, lane >= 64


def _row_fold(x, op):
    return op(x.reshape(x.shape[0] // 8, 8, x.shape[1]), axis=0)


def _attn_fwd_t_call(q, k, vt, *, groups, sub, masked, scale, tq, tk, name, s_bufs=2):
    s_len = q.shape[0]
    qw = LANES * (sub // 2 if masked else sub)
    kvw = LANES if masked else LANES * sub
    n_c = s_len // tk
    kv_mode = pl.Buffered(1) if groups == 1 else None

    def body(q_ref, k_ref, vt_ref, o_ref, lse_ref, s_sc):
        keep = _head_masks(tq) if masked else None

        def kv_of(hh):
            return slice(0, LANES) if masked else slice(LANES * hh, LANES * (hh + 1))

        def q_of(hh):
            if not masked:
                return q_ref[:, LANES * hh:LANES * (hh + 1)]
            qp = q_ref[:, LANES * (hh // 2):LANES * (hh // 2 + 1)]
            return jnp.where(keep[hh % 2], qp, jnp.zeros_like(qp))

        def scores(hh, qm, c, mx):
            s_t = _dot_nt(k_ref[tk * c:tk * (c + 1), kv_of(hh)], qm)
            if scale is not None:
                s_t = s_t * scale
            s_sc[hh % s_bufs, c] = s_t
            return jnp.maximum(mx, _row_fold(s_t, jnp.max))

        neg = jnp.full((8, tq), -jnp.inf, F32)
        qm_next = q_of(0)
        mx_next = neg
        for c in range(n_c):
            mx_next = scores(0, qm_next, c, mx_next)
        outs = []
        for hh in range(sub):
            m = jnp.max(mx_next, axis=0, keepdims=True)
            if hh + 1 < sub:
                qm_next = q_of(hh + 1)
                mx_next = neg
            lsum = jnp.zeros((8, tq), F32)
            acc = jnp.zeros((LANES, tq), F32)
            for c in range(n_c):
                if hh + 1 < sub and s_bufs == 2:
                    mx_next = scores(hh + 1, qm_next, c, mx_next)
                p_t = jnp.exp(s_sc[hh % s_bufs, c] - m)
                if hh + 1 < sub and s_bufs == 1:
                    mx_next = scores(hh + 1, qm_next, c, mx_next)
                lsum = lsum + _row_fold(p_t, jnp.sum)
                acc = acc + _dot(vt_ref[kv_of(hh), tk * c:tk * (c + 1)], p_t.astype(BF16))
            l = jnp.sum(lsum, axis=0, keepdims=True)
            outs.append((acc / l).T)
            lse_ref[hh] = m + jnp.log(l)
        if masked:
            for pr in range(sub // 2):
                o_ref[:, LANES * pr:LANES * (pr + 1)] = jnp.where(keep[0], outs[2 * pr], outs[2 * pr + 1])
        else:
            for hh in range(sub):
                o_ref[:, LANES * hh:LANES * (hh + 1)] = outs[hh]

    return pl.pallas_call(
        body, name=name, grid=(groups, s_len // tq),
        in_specs=[pl.BlockSpec((tq, qw), lambda g, i: (i, g)),
                  pl.BlockSpec((s_len, kvw), lambda g, i: (0, g), pipeline_mode=kv_mode),
                  pl.BlockSpec((kvw, s_len), lambda g, i: (g, 0), pipeline_mode=kv_mode)],
        out_specs=[pl.BlockSpec((tq, qw), lambda g, i: (i, g)),
                   pl.BlockSpec((sub, 1, tq), lambda g, i: (g, 0, i))],
        out_shape=[jax.ShapeDtypeStruct((s_len, groups * qw), F32),
                   jax.ShapeDtypeStruct((groups * sub, 1, s_len), F32)],
        scratch_shapes=[pltpu.VMEM((s_bufs, n_c, tk, tq), F32)],
        compiler_params=_params(("arbitrary", "arbitrary")),
    )(q, k, vt)


def _attn_bwd_q_call(q, k, v, do, o, lse_t, *, groups, sub, masked, tq, tk, ck, name):
    s_len = q.shape[0]
    qw = LANES * (sub // 2 if masked else sub)
    kvw = LANES if masked else LANES * sub
    n_c = tk // ck

    def body(q_ref, k_ref, v_ref, do_ref, o_ref, lse_ref, dq_ref, dkt_ref, dvt_ref):
        j = pl.program_id(1)
        i = pl.program_id(2)

        @pl.when((j == 0) & (i == 0))
        def _():
            dq_ref[...] = jnp.zeros(dq_ref.shape, F32)

        @pl.when(i == 0)
        def _():
            dkt_ref[...] = jnp.zeros(dkt_ref.shape, F32)
            dvt_ref[...] = jnp.zeros(dvt_ref.shape, F32)

        lkeep = _head_masks(tq) if masked else None
        heads = []
        for hh in range(sub):
            if masked:
                cols = slice(LANES * (hh // 2), LANES * (hh // 2 + 1))
                kv = slice(0, LANES)
                qp, dop = q_ref[:, cols], do_ref[:, cols]
                qm = jnp.where(lkeep[hh % 2], qp, jnp.zeros_like(qp))
                dom = jnp.where(lkeep[hh % 2], dop, jnp.zeros_like(dop))
            else:
                cols = kv = slice(LANES * hh, LANES * (hh + 1))
                qm, dom = q_ref[:, cols], do_ref[:, cols]
            delta = jnp.sum(dom.astype(F32) * o_ref[:, cols], axis=1, keepdims=True)
            lse = jnp.broadcast_to(lse_ref[hh], (LANES, tq)).T[:, 0:1]
            heads.append((cols, kv, qm, dom, qm.T, dom.T, delta, lse))

        def products(hh, c):
            _, kv, qm, dom, _, _, _, _ = heads[hh]
            return _dot_nt(qm, k_ref[ck * c:ck * (c + 1), kv]), _dot_nt(dom, v_ref[ck * c:ck * (c + 1), kv])

        items = [(hh, c) for hh in range(sub) for c in range(n_c)]
        dq_acc = [jnp.zeros((tq, LANES), F32) for _ in range(sub)]
        nxt = products(*items[0])
        for n, (hh, c) in enumerate(items):
            s, dp = nxt
            if n + 1 < len(items):
                nxt = products(*items[n + 1])
            _, kv, qm, dom, qmt, domt, delta, lse = heads[hh]
            p = jnp.exp(s - lse)
            ds = p * (dp - delta)
            p_b = p.astype(BF16)
            ds_b = ds.astype(BF16)
            kcols = slice(ck * c, ck * (c + 1))
            dvt_ref[kv, kcols] += _dot(domt, p_b)
            dkt_ref[kv, kcols] += _dot(qmt, ds_b)
            dq_acc[hh] = dq_acc[hh] + _dot(ds_b, k_ref[kcols, kv])
        rows = pl.ds(pl.multiple_of(i * tq, tq), tq)
        if masked:
            for pr in range(sub // 2):
                dq_ref[rows, LANES * pr:LANES * (pr + 1)] += jnp.where(lkeep[0], dq_acc[2 * pr], dq_acc[2 * pr + 1])
        else:
            for hh in range(sub):
                dq_ref[rows, LANES * hh:LANES * (hh + 1)] += dq_acc[hh]

    return pl.pallas_call(
        body, name=name, grid=(groups, s_len // tk, s_len // tq),
        in_specs=[pl.BlockSpec((tq, qw), lambda g, j, i: (i, g)),
                  pl.BlockSpec((tk, kvw), lambda g, j, i: (j, g)),
                  pl.BlockSpec((tk, kvw), lambda g, j, i: (j, g)),
                  pl.BlockSpec((tq, qw), lambda g, j, i: (i, g)),
                  pl.BlockSpec((tq, qw), lambda g, j, i: (i, g)),
                  pl.BlockSpec((sub, 1, tq), lambda g, j, i: (g, 0, i))],
        out_specs=[pl.BlockSpec((s_len, qw), lambda g, j, i: (0, g)),
                   pl.BlockSpec((kvw, tk), lambda g, j, i: (g, j)),
                   pl.BlockSpec((kvw, tk), lambda g, j, i: (g, j))],
        out_shape=[jax.ShapeDtypeStruct((s_len, groups * qw), F32),
                   jax.ShapeDtypeStruct((groups * kvw, s_len), F32),
                   jax.ShapeDtypeStruct((groups * kvw, s_len), F32)],
        compiler_params=_params(("arbitrary", "arbitrary", "arbitrary")),
    )(q, k, v, do, o, lse_t)


def _attn_bwd_p_call(q, k, v, do, o, linv_t, p, *, groups, sub, masked, tq, tk, ck, name):
    s_len = q.shape[0]
    qw = LANES * (sub // 2 if masked else sub)
    kvw = LANES if masked else LANES * sub
    n_c = tk // ck

    def body(q_ref, k_ref, v_ref, do_ref, o_ref, linv_ref, p_ref, dq_ref, dkt_ref, dvt_ref):
        j = pl.program_id(1)
        i = pl.program_id(2)

        @pl.when((j == 0) & (i == 0))
        def _():
            dq_ref[...] = jnp.zeros(dq_ref.shape, F32)

        @pl.when(i == 0)
        def _():
            dkt_ref[...] = jnp.zeros(dkt_ref.shape, F32)
            dvt_ref[...] = jnp.zeros(dvt_ref.shape, F32)

        lkeep = _head_masks(tq) if masked else None
        heads = []
        for hh in range(sub):
            cols = slice(LANES * (hh // 2), LANES * (hh // 2 + 1)) if masked else slice(LANES * hh, LANES * (hh + 1))
            kv = slice(0, LANES) if masked else cols
            linv = jnp.broadcast_to(linv_ref[hh], (LANES, tq)).T
            qm = q_ref[:, cols]
            dof = do_ref[:, cols] * linv
            if masked:
                qm = jnp.where(lkeep[hh % 2], qm, jnp.zeros_like(qm))
                dof = jnp.where(lkeep[hh % 2], dof, jnp.zeros_like(dof))
            delta = jnp.sum(dof * o_ref[:, cols], axis=1, keepdims=True)
            dom = dof.astype(BF16)
            heads.append((kv, qm, dom, qm.T, dom.T, delta))

        def product(hh, c):
            kv, _, dom, _, _, _ = heads[hh]
            return _dot_nt(dom, v_ref[ck * c:ck * (c + 1), kv])

        items = [(hh, c) for hh in range(sub) for c in range(n_c)]
        dq_acc = [jnp.zeros((tq, LANES), F32) for _ in range(sub)]
        nxt = product(*items[0])
        for n, (hh, c) in enumerate(items):
            dp = nxt
            if n + 1 < len(items):
                nxt = product(*items[n + 1])
            kv, qm, dom, qmt, domt, delta = heads[hh]
            kcols = slice(ck * c, ck * (c + 1))
            p_b = p_ref[hh, :, kcols]
            ds_b = (p_b.astype(F32) * (dp - delta)).astype(BF16)
            dvt_ref[kv, kcols] += _dot(domt, p_b)
            dkt_ref[kv, kcols] += _dot(qmt, ds_b)
            dq_acc[hh] = dq_acc[hh] + _dot(ds_b, k_ref[kcols, kv])
        rows = pl.ds(pl.multiple_of(i * tq, tq), tq)
        if masked:
            for pr in range(sub // 2):
                dq_ref[rows, LANES * pr:LANES * (pr + 1)] += jnp.where(lkeep[0], dq_acc[2 * pr], dq_acc[2 * pr + 1])
        else:
            for hh in range(sub):
                dq_ref[rows, LANES * hh:LANES * (hh + 1)] += dq_acc[hh]

    return pl.pallas_call(
        body, name=name, grid=(groups, s_len // tk, s_len // tq),
        in_specs=[pl.BlockSpec((tq, qw), lambda g, j, i: (i, g)),
                  pl.BlockSpec((tk, kvw), lambda g, j, i: (j, g)),
                  pl.BlockSpec((tk, kvw), lambda g, j, i: (j, g)),
                  pl.BlockSpec((tq, qw), lambda g, j, i: (i, g)),
                  pl.BlockSpec((tq, qw), lambda g, j, i: (i, g)),
                  pl.BlockSpec((sub, 1, tq), lambda g, j, i: (g, 0, i)),
                  pl.BlockSpec((sub, tq, tk), lambda g, j, i: (g, i, j))],
        out_specs=[pl.BlockSpec((s_len, qw), lambda g, j, i: (0, g)),
                   pl.BlockSpec((kvw, tk), lambda g, j, i: (g, j)),
                   pl.BlockSpec((kvw, tk), lambda g, j, i: (g, j))],
        out_shape=[jax.ShapeDtypeStruct((s_len, groups * qw), F32),
                   jax.ShapeDtypeStruct((groups * kvw, s_len), F32),
                   jax.ShapeDtypeStruct((groups * kvw, s_len), F32)],
        compiler_params=_params(("arbitrary", "arbitrary", "arbitrary")),
    )(q, k, v, do, o, linv_t, p)


def _attn_bwd_call(q, k, v, do, lse_t, delta_t, *, groups, sub, scale, tq, tk, name):
    s_len = q.shape[0]
    masked = sub > 1
    qw = LANES * (sub // 2 if masked else 1)
    n_k = s_len // tk

    def body(q_ref, k_ref, v_ref, do_ref, lse_ref, dl_ref, dq_ref, dk_ref, dv_ref, dq_sc):
        i = pl.program_id(1)
        j = pl.program_id(2)

        @pl.when((i == 0) & (j == 0))
        def _():
            dk_ref[...] = jnp.zeros(dk_ref.shape, F32)
            dv_ref[...] = jnp.zeros(dv_ref.shape, F32)

        @pl.when(j == 0)
        def _():
            dq_sc[...] = jnp.zeros(dq_sc.shape, F32)

        kk = k_ref[...]
        vv = v_ref[...]
        keep = _head_masks(tq) if masked else None
        dk_t = jnp.zeros((tk, LANES), F32)
        dv_t = jnp.zeros((tk, LANES), F32)
        for hh in range(sub):
            if masked:
                cols = slice(LANES * (hh // 2), LANES * (hh // 2 + 1))
                qp = q_ref[:, cols]
                dop = do_ref[:, cols]
                qm = jnp.where(keep[hh % 2], qp, jnp.zeros_like(qp))
                dom = jnp.where(keep[hh % 2], dop, jnp.zeros_like(dop))
            else:
                cols = slice(0, LANES)
                qm = q_ref[...]
                dom = do_ref[...]
            s_t = _dot_nt(kk, qm)
            if scale is not None:
                s_t = s_t * scale
            p_t = jnp.exp(s_t - lse_ref[hh])
            dp_t = _dot_nt(vv, dom)
            ds_t = p_t * (dp_t - dl_ref[hh])
            if scale is not None:
                ds_t = ds_t * scale
            p_b = p_t.astype(BF16)
            ds_b = ds_t.astype(BF16)
            dv_t = dv_t + _dot(p_b, dom)
            dk_t = dk_t + _dot(ds_b, qm)
            dq_h = _dot_tn(ds_b, kk)
            if masked:
                dq_h = jnp.where(keep[hh % 2], dq_h, jnp.zeros_like(dq_h))
            dq_sc[:, cols] += dq_h
        rows = pl.ds(pl.multiple_of(j * tk, tk), tk)
        dk_ref[rows, :] += dk_t
        dv_ref[rows, :] += dv_t

        @pl.when(j == n_k - 1)
        def _():
            dq_ref[...] = dq_sc[...]

    return pl.pallas_call(
        body, name=name, grid=(groups, s_len // tq, n_k),
        in_specs=[pl.BlockSpec((tq, qw), lambda g, i, j: (i, g)),
                  pl.BlockSpec((tk, LANES), lambda g, i, j: (j, g)),
                  pl.BlockSpec((tk, LANES), lambda g, i, j: (j, g)),
                  pl.BlockSpec((tq, qw), lambda g, i, j: (i, g)),
                  pl.BlockSpec((sub, 1, tq), lambda g, i, j: (g, 0, i)),
                  pl.BlockSpec((sub, 1, tq), lambda g, i, j: (g, 0, i))],
        out_specs=[pl.BlockSpec((tq, qw), lambda g, i, j: (i, g)),
                   pl.BlockSpec((s_len, LANES), lambda g, i, j: (0, g)),
                   pl.BlockSpec((s_len, LANES), lambda g, i, j: (0, g))],
        out_shape=[jax.ShapeDtypeStruct((s_len, groups * qw), F32),
                   jax.ShapeDtypeStruct((s_len, groups * LANES), F32),
                   jax.ShapeDtypeStruct((s_len, groups * LANES), F32)],
        scratch_shapes=[pltpu.VMEM((tq, qw), F32)],
        compiler_params=_params(("arbitrary", "arbitrary", "arbitrary")),
    )(q, k, v, do, lse_t, delta_t)


def _silu_parts(g):
    sig = 1.0 / (1.0 + jnp.exp(-g))
    return g * sig, sig * (1.0 + g * (1.0 - sig))


def _out_call(x, target, oa, ob, ga, gb, wout, tm):
    s_len = x.shape[0]
    n_t = s_len // tm

    def body(x_ref, t_ref, oa_ref, ob_ref, ga_ref, gb_ref, w_ref,
             dh_ref, doa_ref, dob_ref, dga_ref, dgb_ref, dw_ref, loss_ref):
        i = pl.program_id(0)

        @pl.when(i == 0)
        def _():
            dw_ref[...] = jnp.zeros(dw_ref.shape, F32)
            loss_ref[...] = jnp.zeros(loss_ref.shape, F32)

        oa_v, ob_v = oa_ref[...], ob_ref[...]
        silu_a, dsilu_a = _silu_parts(ga_ref[...])
        silu_b, dsilu_b = _silu_parts(gb_ref[...])
        ya = (oa_v * silu_a).astype(BF16)
        yb = (ob_v * silu_b).astype(BF16)
        h = x_ref[...] + _dot(ya, w_ref[0:512, :]) + _dot(yb, w_ref[512:1024, :])
        err = h - t_ref[...]
        part = jnp.sum(err * err, axis=0, keepdims=True)
        acc = part[:, 0:LANES]
        for c in range(1, D_MODEL // LANES):
            acc = acc + part[:, LANES * c:LANES * (c + 1)]
        loss_ref[...] += acc
        dh = err * (1.0 / D_MODEL)
        dh_ref[...] = dh
        dhb = dh.astype(BF16)
        dya = _dot_nt(dhb, w_ref[0:512, :])
        dyb = _dot_nt(dhb, w_ref[512:1024, :])
        doa = dya * silu_a
        dob = dyb * silu_b
        doa_ref[...] = doa.astype(BF16)
        dob_ref[...] = dob.astype(BF16)
        dga_ref[...] = dya * oa_v * dsilu_a
        dgb_ref[...] = dyb * ob_v * dsilu_b
        dw_ref[0:512, :] += _dot_tn(ya, dhb)
        dw_ref[512:1024, :] += _dot_tn(yb, dhb)

    def rows(width):
        return pl.BlockSpec((tm, width), lambda i: (i, 0))

    outs = [(D_MODEL, F32), (512, BF16), (512, BF16), (512, F32), (512, F32)]
    return pl.pallas_call(
        body, name="out_fwd", grid=(n_t,),
        in_specs=[rows(D_MODEL), rows(D_MODEL), rows(512), rows(512), rows(512), rows(512),
                  _full((D_MODEL, D_MODEL))],
        out_specs=[rows(wd) for wd, _ in outs] + [_full((D_MODEL, D_MODEL)), _full((1, LANES))],
        out_shape=[jax.ShapeDtypeStruct((s_len, wd), dt) for wd, dt in outs]
        + [jax.ShapeDtypeStruct((D_MODEL, D_MODEL), F32), jax.ShapeDtypeStruct((1, LANES), F32)],
        compiler_params=_params(("arbitrary",)),
    )(x, target, oa, ob, ga, gb, wout)


def _pre_bwd_call(x, raw, dh, dqa, dka, dva, dga, dqb, dkb, dvb, dgb, consts, tabs, tm):
    s_len = x.shape[0]
    ts = min(256, tm)

    def body(x_ref, raw_ref, dh_ref, dqa_ref, dkat_ref, dvat_ref, dga_ref, dqb_ref, dkbt_ref, dvbt_ref, dgb_ref,
             *refs):
        tab_refs, refs = refs[:8], refs[8:]
        (gin_ref, w_ref, wuq_ref, wuk_ref, wuv_ref, gq_ref, gk_ref, gcq_ref, gckv_ref, gqb_ref, gkb_ref, g64_ref,
         dx_ref, dproj_ref, dwuq_ref, dwuk_ref, dwuv_ref, small_ref) = refs
        i = pl.program_id(0)

        @pl.when(i == 0)
        def _():
            dwuq_ref[...] = jnp.zeros(dwuq_ref.shape, F32)
            dwuk_ref[...] = jnp.zeros(dwuk_ref.shape, F32)
            dwuv_ref[...] = jnp.zeros(dwuv_ref.shape, F32)
            small_ref[...] = jnp.zeros(small_ref.shape, F32)

        gin, gq, gk = gin_ref[...], gq_ref[...], gk_ref[...]
        gcq, gckv, gqb, gkb = gcq_ref[...], gckv_ref[...], gqb_ref[...], gkb_ref[...]
        w, wuq, wuk, wuv = w_ref[...], wuq_ref[...], wuk_ref[...], wuv_ref[...]
        acc_uq, acc_uk, acc_uv, gains = [], [], [], []
        for part in range(tm // ts):
            r_ = slice(ts * part, ts * (part + 1))
            dka_v, dva_v = dkat_ref[:, r_].T, dvat_ref[:, r_].T
            dkb_v, dvb_v = dkbt_ref[:, r_].T, dvbt_ref[:, r_].T
            ca, sa, cb, sb = _rope_tiles(tab_refs, i * (tm // ts) + part, ts)
            f = _pre_forward(x_ref[r_, :], gin, w, wuq, wuk, wuv, gq, gk, gcq, gckv, gqb, gkb,
                             ca, sa, cb, sb, g64_ref[...], ts, raw=raw_ref[r_, :])
            sel16, sel8, gs64 = f["sel16"], f["sel8"], f["gs64"]
            lane = lax.broadcasted_iota(jnp.int32, (ts, LANES), 1)
            low = lane < 64
            zero = jnp.zeros((ts, LANES), F32)
            pieces = []

            dgq = jnp.zeros((1, LANES), F32)
            for s in range(4):
                _, xh, r = f["qa"][s]
                d = dqa_ref[r_, LANES * s:LANES * (s + 1)] * QA_SCALE
                dx, dg = _col_bwd(d, xh, r, gs64, 64.0, gq, ca, sa, 16, sel16)
                pieces.append(dx)
                dgq = dgq + dg
            dgk = jnp.zeros((1, LANES), F32)
            for s in range(2):
                _, xh, r = f["ka"][s]
                d = dka_v[:, LANES * s:LANES * (s + 1)]
                d = d + pltpu.roll(d, 64, 1)
                dx, dg = _col_bwd(d, xh, r, _row_sum, 128.0, gk, ca, sa, 16, sel16)
                pieces.append(jnp.where(low, dx, zero))
                dgk = dgk + dg
            for s in range(2):
                d = dva_v[:, LANES * s:LANES * (s + 1)]
                d = d + pltpu.roll(d, 64, 1)
                pieces.append(jnp.where(low, d, zero))
            pieces.append(dga_ref[r_, :])

            dgqb = jnp.zeros((1, LANES), F32)
            dq_cols = []
            for h in range(B_HEADS):
                _, xh, r = f["qb"][h]
                dx, dg = _col_bwd(dqb_ref[r_, LANES * h:LANES * (h + 1)] * QB_SCALE, xh, r, _row_sum,
                                  float(B_QK_DIM), gqb, cb, sb, 8, sel8)
                dq_cols.append(dx)
                dgqb = dgqb + dg
            dqr_b = jnp.concatenate(dq_cols, axis=1).astype(BF16)
            acc_uq.append(_dot_tn(f["cqb"], dqr_b))
            dcq_raw, dgcq = _rms_bwd(_dot_nt(dqr_b, wuq), f["cqh"], f["rcq"], gcq)
            pieces.append(dcq_raw)

            dgkb = jnp.zeros((1, LANES), F32)
            dk_cols = []
            dkr = zero
            for h in range(B_HEADS):
                _, xh, r = f["kb"][h]
                dx, dg = _col_bwd(dkb_v[:, LANES * h:LANES * (h + 1)], xh, r, _row_sum, float(B_QK_DIM),
                                  gkb, cb, sb, 8, sel8)
                dk_cols.append(dx)
                dkr = dkr + dx
                dgkb = dgkb + dg
            dkr_b = jnp.concatenate(dk_cols, axis=1).astype(BF16)
            dvb_b = dvb_v.astype(BF16)
            acc_uk.append(_dot_tn(f["ckvb"], dkr_b))
            acc_uv.append(_dot_tn(f["ckvb"], dvb_b))
            dckv = _dot_nt(dkr_b, wuk) + _dot_nt(dvb_b, wuv)
            dckv_raw, dgckv = _rms_bwd(dckv, f["ckvh"], f["rckv"], gckv)
            pieces.append(dckv_raw)
            pieces.append(jnp.where((lane >= B_NOPE_DIM) & (lane < B_QK_DIM), dkr, zero))
            pieces += [zero, zero, zero]
            pieces.append(dgb_ref[r_, :])

            dproj_b = jnp.concatenate(pieces, axis=1).astype(BF16)
            dproj_ref[r_, :] = dproj_b
            dxn = _dot_nt(dproj_b, w)
            dx, dgin = _rms_bwd(dxn, f["xh0"], f["r0"], gin)
            dx_ref[r_, :] = dx + dh_ref[r_, :]

            gains.append((dgin, dgq, dgk, dgcq, dgckv, dgqb, dgkb))

        dwuq_ref[...] += functools.reduce(jnp.add, acc_uq)
        dwuk_ref[...] += functools.reduce(jnp.add, acc_uk)
        dwuv_ref[...] += functools.reduce(jnp.add, acc_uv)
        dgin, dgq, dgk, dgcq, dgckv, dgqb, dgkb = [functools.reduce(jnp.add, v) for v in zip(*gains)]
        for c in range(D_MODEL // LANES):
            small_ref[c:c + 1, :] += dgin[:, LANES * c:LANES * (c + 1)]
        small_ref[8:9, :] += dgq
        small_ref[9:10, :] += dgk
        for c in range(3):
            small_ref[10 + c:11 + c, :] += dgcq[:, LANES * c:LANES * (c + 1)]
        for c in range(2):
            small_ref[13 + c:14 + c, :] += dgckv[:, LANES * c:LANES * (c + 1)]
        small_ref[15:16, :] += dgqb
        small_ref[16:17, :] += dgkb

    def rows(width):
        return pl.BlockSpec((tm, width), lambda i: (i, 0))

    def cols(height):
        return pl.BlockSpec((height, tm), lambda i: (0, i))

    return pl.pallas_call(
        body, name="pre_bwd", grid=(s_len // tm,),
        in_specs=[rows(D_MODEL), rows(R_WIDTH), rows(D_MODEL), rows(512), cols(256), cols(256), rows(512), rows(512),
                  cols(512), cols(512), rows(512)] + [_full(t.shape) for t in tabs] + _pre_const_specs(consts),
        out_specs=[rows(D_MODEL), rows(N_WIDE), _full((B_Q_RANK, 512)), _full((B_KV_RANK, 512)),
                   _full((B_KV_RANK, 512)), _full((R_SMALL, LANES))],
        out_shape=[jax.ShapeDtypeStruct((s_len, D_MODEL), F32), jax.ShapeDtypeStruct((s_len, N_WIDE), BF16),
                   jax.ShapeDtypeStruct((B_Q_RANK, 512), F32),
                   jax.ShapeDtypeStruct((B_KV_RANK, 512), F32), jax.ShapeDtypeStruct((B_KV_RANK, 512), F32),
                   jax.ShapeDtypeStruct((R_SMALL, LANES), F32)],
        compiler_params=_params(("arbitrary",)),
    )(x, raw, dh, dqa, dka, dva, dga, dqb, dkb, dvb, dgb, *tabs, *[consts[n] for n in _PRE_IN_NAMES])


def _dw_in_call(xnb, dproj_b, tt, tn):
    s_len = xnb.shape[0]

    def body(a_ref, b_ref, o_ref):
        @pl.when(pl.program_id(1) == 0)
        def _():
            o_ref[...] = jnp.zeros(o_ref.shape, F32)

        o_ref[...] += _dot_tn(a_ref[...], b_ref[...])

    return pl.pallas_call(
        body, name="dw_in", grid=(N_WIDE // tn, s_len // tt),
        in_specs=[pl.BlockSpec((tt, D_MODEL), lambda n, t: (t, 0)), pl.BlockSpec((tt, tn), lambda n, t: (t, n))],
        out_specs=pl.BlockSpec((D_MODEL, tn), lambda n, t: (0, n)),
        out_shape=jax.ShapeDtypeStruct((D_MODEL, N_WIDE), F32),
        compiler_params=_params(("arbitrary", "arbitrary")),
    )(xnb, dproj_b)


def _mesh_pos():
    return lax.axis_index("x"), lax.axis_index("y"), lax.axis_index("c")


def _flip(v, bit):
    return 1 - v if bit else v


def _peer(pos, k):
    x, y, c = pos
    return _flip(x, (k >> 2) & 1), _flip(y, (k >> 1) & 1), _flip(c, k & 1)


def _logical(p):
    return 4 * p[0] + 2 * p[1] + p[2]


def _gather_weights_call(shard):
    m_per = shard.shape[0]

    def body(x_ref, out_ref, xb_ref, send_sems, recv_sems, local_sem):
        x, y, c = _mesh_pos()
        me, sibling = (x, y, c), (x, y, 1 - c)
        chips = [(1 - x, y), (x, 1 - y), (1 - x, 1 - y)]
        xb_ref[...] = x_ref[...].astype(BF16)

        def rows(p):
            return out_ref.at[pl.ds(pl.multiple_of(_logical(p) * m_per, 16), m_per), :]

        def copy(k, block, to, src=None):
            return pltpu.make_async_remote_copy(
                src_ref=rows(block) if src is None else src, dst_ref=rows(block),
                send_sem=send_sems.at[k], recv_sem=recv_sems.at[k],
                device_id=to, device_id_type=pl.DeviceIdType.MESH)

        mine = pltpu.make_async_copy(xb_ref, rows(me), local_sem)
        mine.start()
        first = [copy(0, me, sibling, src=xb_ref)]
        first += [copy(1 + j, me, (*chip, c), src=xb_ref) for j, chip in enumerate(chips)]
        for cp in first:
            cp.start()
        passed = [copy(4 + j, (*chip, c), sibling) for j, chip in enumerate(chips)]
        for j, chip in enumerate(chips):
            copy(1 + j, (*chip, c), me).wait_recv()
            passed[j].start()
        copy(0, sibling, me).wait_recv()
        for j, chip in enumerate(chips):
            copy(4 + j, (*chip, 1 - c), me).wait_recv()
        for cp in first + passed:
            cp.wait_send()
        mine.wait()

    return pl.pallas_call(
        body, name="gather_weights",
        out_shape=jax.ShapeDtypeStruct((N_DEV * m_per, LANES), BF16),
        in_specs=[pl.BlockSpec(memory_space=pltpu.VMEM)],
        out_specs=pl.BlockSpec(memory_space=pltpu.VMEM),
        scratch_shapes=[pltpu.VMEM((m_per, LANES), BF16), pltpu.SemaphoreType.DMA((7,)),
                        pltpu.SemaphoreType.DMA((7,)), pltpu.SemaphoreType.DMA],
        compiler_params=pltpu.CompilerParams(vmem_limit_bytes=VMEM_LIMIT),
    )(shard)


def _adamw(w, g, m, v):
    m = ADAM_B1 * m + (1.0 - ADAM_B1) * g
    v = ADAM_B2 * v + (1.0 - ADAM_B2) * (g * g)
    m_hat = m / (1.0 - ADAM_B1 ** ADAM_STEP)
    v_hat = v / (1.0 - ADAM_B2 ** ADAM_STEP)
    delta = -ADAM_LR * (m_hat / (jnp.sqrt(v_hat) + ADAM_EPS) + ADAM_WD * w)
    return delta, m, v


def _reduce_adamw_call(parts, small, w_pk, m_pk, v_pk, w_s, m_s, v_s):
    chunk = 16
    n_chunks = R_PACK // chunk

    def body(parts_ref, small_ref, w_ref, m_ref, v_ref, ws_ref, ms_ref, vs_ref,
             g_ref, d_ref, nm_ref, nv_ref, gs_ref, ds_ref, nms_ref, nvs_ref,
             recv_ref, recv_s_ref, send_sems, recv_sems, send_s_sems, recv_s_sems, local_sem):
        pos = _mesh_pos()
        me = _logical(pos)

        def big(k):
            peer = _peer(pos, k)
            return pltpu.make_async_remote_copy(
                src_ref=parts_ref.at[_logical(peer)], dst_ref=recv_ref.at[k],
                send_sem=send_sems.at[k], recv_sem=recv_sems.at[k],
                device_id=peer, device_id_type=pl.DeviceIdType.MESH)

        def tiny(k):
            return pltpu.make_async_remote_copy(
                src_ref=small_ref, dst_ref=recv_s_ref.at[k],
                send_sem=send_s_sems.at[k], recv_sem=recv_s_sems.at[k],
                device_id=_peer(pos, k), device_id_type=pl.DeviceIdType.MESH)

        own = pltpu.make_async_copy(parts_ref.at[me], recv_ref.at[0], local_sem)
        own.start()
        for k in range(1, N_DEV):
            tiny(k).start()
        for k in range(1, N_DEV):
            big(k).start()
        recv_s_ref[0] = small_ref[...]
        for k in range(1, N_DEV):
            tiny(k).wait_recv()
        acc = recv_s_ref[me]
        for a in range(1, N_DEV):
            acc = acc + recv_s_ref[lax.bitwise_xor(me, a)]
        row = lax.broadcasted_iota(jnp.int32, (R_SMALL, LANES), 0)
        gs = jnp.where(row == 8, acc + pltpu.roll(acc, 64, 1), acc)
        gs = jnp.where(row == ROW_LOSS, jnp.sum(acc, axis=1, keepdims=True) * (0.5 / D_MODEL), gs)
        gs_ref[...] = gs
        ds, nms, nvs = _adamw(ws_ref[...], gs, ms_ref[...], vs_ref[...])
        ds_ref[...] = ds
        nms_ref[...] = nms
        nvs_ref[...] = nvs

        own.wait()
        for k in range(1, N_DEV):
            big(k).wait_recv()

        def step(t, carry):
            rows = pl.ds(pl.multiple_of(t * chunk, chunk), chunk)
            g = recv_ref[0, rows, :]
            for k in range(1, N_DEV):
                g = g + recv_ref[k, rows, :]
            d, nm, nv = _adamw(w_ref[rows, :], g, m_ref[rows, :], v_ref[rows, :])
            g_ref[rows, :] = g
            d_ref[rows, :] = d
            nm_ref[rows, :] = nm
            nv_ref[rows, :] = nv
            return carry

        lax.fori_loop(0, n_chunks, step, 0)
        for k in range(1, N_DEV):
            tiny(k).wait_send()
            big(k).wait_send()

    vm = pl.BlockSpec(memory_space=pltpu.VMEM)
    big_shape = jax.ShapeDtypeStruct((R_PACK, LANES), F32)
    small_shape = jax.ShapeDtypeStruct((R_SMALL, LANES), F32)
    return pl.pallas_call(
        body, name="reduce_adamw",
        in_specs=[pl.BlockSpec(memory_space=pl.ANY)] + [vm] * 7,
        out_specs=[vm] * 8,
        out_shape=[big_shape] * 4 + [small_shape] * 4,
        scratch_shapes=[pltpu.VMEM((N_DEV, R_PACK, LANES), F32), pltpu.VMEM((N_DEV, R_SMALL, LANES), F32),
                        pltpu.SemaphoreType.DMA((N_DEV,)), pltpu.SemaphoreType.DMA((N_DEV,)),
                        pltpu.SemaphoreType.DMA((N_DEV,)), pltpu.SemaphoreType.DMA((N_DEV,)),
                        pltpu.SemaphoreType.DMA],
        compiler_params=pltpu.CompilerParams(vmem_limit_bytes=VMEM_LIMIT),
    )(parts, small, w_pk, m_pk, v_pk, w_s, m_s, v_s)


W_BLOCKS = ((D_MODEL, N_IN // N_DEV), (B_Q_RANK // N_DEV, 384), (B_KV_RANK, 768 // N_DEV), (D_MODEL // N_DEV, D_MODEL))
N_W = len(W_BLOCKS)


def _gather_blocks_call(blocks):
    def body(*refs):
        x_refs, out_refs, xb_refs = refs[0:N_W], refs[N_W:2 * N_W], refs[2 * N_W:3 * N_W]
        send_sems, recv_sems, local_sems = refs[3 * N_W:]
        x, y, c = _mesh_pos()
        me, sibling = (x, y, c), (x, y, 1 - c)
        chips = [(1 - x, y), (x, 1 - y), (1 - x, 1 - y)]
        for w in range(N_W):
            xb_refs[w][...] = x_refs[w][...].astype(BF16)

        def slot(w, p):
            return out_refs[w].at[_logical(p)]

        def copy(w, k, block, to, src=None):
            return pltpu.make_async_remote_copy(
                src_ref=slot(w, block) if src is None else src, dst_ref=slot(w, block),
                send_sem=send_sems.at[N_W * k + w], recv_sem=recv_sems.at[N_W * k + w],
                device_id=to, device_id_type=pl.DeviceIdType.MESH)

        mine = [pltpu.make_async_copy(xb_refs[w], slot(w, me), local_sems.at[w]) for w in range(N_W)]
        for cp in mine:
            cp.start()
        first = [copy(w, 0, me, sibling, src=xb_refs[w]) for w in range(N_W)]
        first += [copy(w, 1 + j, me, (*chip, c), src=xb_refs[w]) for j, chip in enumerate(chips) for w in range(N_W)]
        for cp in first:
            cp.start()
        passed = []
        for j, chip in enumerate(chips):
            for w in range(N_W):
                copy(w, 1 + j, (*chip, c), me).wait_recv()
                fwd = copy(w, 4 + j, (*chip, c), sibling)
                fwd.start()
                passed.append(fwd)
        for w in range(N_W):
            copy(w, 0, sibling, me).wait_recv()
        for j, chip in enumerate(chips):
            for w in range(N_W):
                copy(w, 4 + j, (*chip, 1 - c), me).wait_recv()
        for cp in first + passed:
            cp.wait_send()
        for cp in mine:
            cp.wait()

    vm = pl.BlockSpec(memory_space=pltpu.VMEM)
    return pl.pallas_call(
        body, name="gather_weights",
        out_shape=[jax.ShapeDtypeStruct((N_DEV,) + shp, BF16) for shp in W_BLOCKS],
        in_specs=[vm] * N_W, out_specs=[vm] * N_W,
        scratch_shapes=[pltpu.VMEM(shp, BF16) for shp in W_BLOCKS]
        + [pltpu.SemaphoreType.DMA((7 * N_W,)), pltpu.SemaphoreType.DMA((7 * N_W,)), pltpu.SemaphoreType.DMA((N_W,))],
        compiler_params=pltpu.CompilerParams(vmem_limit_bytes=VMEM_LIMIT),
    )(*blocks)


def _reduce_two_level_call(parts, small, w_blk, m_blk, v_blk, w_s, m_s, v_s):
    chunks = (32, 48, 64, 16)
    n_chip = N_DEV // 2

    def body(*refs):
        p_refs = refs[0:4]
        small_ref = refs[4]
        w_refs, m_refs, v_refs = refs[5:9], refs[9:13], refs[13:17]
        ws_ref, ms_ref, vs_ref = refs[17:20]
        g_refs, d_refs, nm_refs, nv_refs = refs[20:24], refs[24:28], refs[28:32], refs[32:36]
        gs_ref, ds_ref, nms_ref, nvs_ref = refs[36:40]
        ra_refs, rb_refs, st_refs = refs[40:44], refs[44:48], refs[48:52]
        recv_s_ref = refs[52]
        send_a, recv_a, send_b, recv_b, send_s_sems, recv_s_sems = refs[53:59]
        pos = _mesh_pos()
        x, y, c = pos
        me = _logical(pos)
        sibling = (x, y, 1 - c)

        def chip(j):
            return _flip(x, j & 1), _flip(y, (j >> 1) & 1)

        def to_sibling(w, j):
            return pltpu.make_async_remote_copy(
                src_ref=p_refs[w].at[_logical((*chip(j), 1 - c))], dst_ref=ra_refs[w].at[j],
                send_sem=send_a.at[N_W * j + w], recv_sem=recv_a.at[N_W * j + w],
                device_id=sibling, device_id_type=pl.DeviceIdType.MESH)

        def to_chip(w, j):
            return pltpu.make_async_remote_copy(
                src_ref=st_refs[w].at[j - 1], dst_ref=rb_refs[w].at[j - 1],
                send_sem=send_b.at[N_W * (j - 1) + w], recv_sem=recv_b.at[N_W * (j - 1) + w],
                device_id=(*chip(j), c), device_id_type=pl.DeviceIdType.MESH)

        def tiny(k):
            return pltpu.make_async_remote_copy(
                src_ref=small_ref, dst_ref=recv_s_ref.at[k],
                send_sem=send_s_sems.at[k], recv_sem=recv_s_sems.at[k],
                device_id=_peer(pos, k), device_id_type=pl.DeviceIdType.MESH)

        order = (0, 3, 2, 1)
        for j in (1, 2, 3, 0):
            for w in order:
                to_sibling(w, j).start()
        for k in range(1, N_DEV):
            tiny(k).start()

        for j in (1, 2, 3):
            d = _logical((*chip(j), c))
            for w in order:
                to_sibling(w, j).wait_recv()
                chunk = chunks[w]

                def pair(t, carry, w=w, j=j, d=d, chunk=chunk):
                    rows = pl.ds(pl.multiple_of(t * chunk, chunk), chunk)
                    s = p_refs[w][d, rows, :].astype(F32) + ra_refs[w][j, rows, :].astype(F32)
                    st_refs[w][j - 1, rows, :] = s.astype(BF16)
                    return carry

                lax.fori_loop(0, W_BLOCKS[w][0] // chunk, pair, 0)
                to_chip(w, j).start()

        recv_s_ref[0] = small_ref[...]
        for k in range(1, N_DEV):
            tiny(k).wait_recv()
        acc = recv_s_ref[me]
        for a in range(1, N_DEV):
            acc = acc + recv_s_ref[lax.bitwise_xor(me, a)]
        row = lax.broadcasted_iota(jnp.int32, (R_SMALL, LANES), 0)
        gs = jnp.where(row == 8, acc + pltpu.roll(acc, 64, 1), acc)
        gs = jnp.where(row == ROW_LOSS, jnp.sum(acc, axis=1, keepdims=True) * (0.5 / D_MODEL), gs)
        gs_ref[...] = gs
        ds, nms, nvs = _adamw(ws_ref[...], gs, ms_ref[...], vs_ref[...])
        ds_ref[...] = ds
        nms_ref[...] = nms
        nvs_ref[...] = nvs

        for w in (1, 2, 3, 0):
            to_sibling(w, 0).wait_recv()
            for j in (1, 2, 3):
                to_chip(w, j).wait_recv()
            chunk = chunks[w]

            def step(t, carry, w=w, chunk=chunk):
                rows = pl.ds(pl.multiple_of(t * chunk, chunk), chunk)
                g = p_refs[w][me, rows, :].astype(F32) + ra_refs[w][0, rows, :].astype(F32)
                for j in range(n_chip - 1):
                    g = g + rb_refs[w][j, rows, :].astype(F32)
                d, nm, nv = _adamw(w_refs[w][rows, :], g, m_refs[w][rows, :], v_refs[w][rows, :])
                g_refs[w][rows, :] = g
                d_refs[w][rows, :] = d
                nm_refs[w][rows, :] = nm
                nv_refs[w][rows, :] = nv
                return carry

            lax.fori_loop(0, W_BLOCKS[w][0] // chunk, step, 0)
        for k in range(1, N_DEV):
            tiny(k).wait_send()
        for w in range(N_W):
            for j in range(n_chip):
                to_sibling(w, j).wait_send()
            for j in (1, 2, 3):
                to_chip(w, j).wait_send()

    vm = pl.BlockSpec(memory_space=pltpu.VMEM)
    blk = [jax.ShapeDtypeStruct(shp, F32) for shp in W_BLOCKS]
    small_shape = jax.ShapeDtypeStruct((R_SMALL, LANES), F32)
    return pl.pallas_call(
        body, name="reduce_adamw",
        in_specs=[vm] * 20, out_specs=[vm] * 20,
        out_shape=blk * 4 + [small_shape] * 4,
        scratch_shapes=[pltpu.VMEM((n_chip,) + shp, BF16) for shp in W_BLOCKS]
        + [pltpu.VMEM((n_chip - 1,) + shp, BF16) for shp in W_BLOCKS] * 2
        + [pltpu.VMEM((N_DEV, R_SMALL, LANES), F32),
           pltpu.SemaphoreType.DMA((n_chip * N_W,)), pltpu.SemaphoreType.DMA((n_chip * N_W,)),
           pltpu.SemaphoreType.DMA(((n_chip - 1) * N_W,)), pltpu.SemaphoreType.DMA(((n_chip - 1) * N_W,)),
           pltpu.SemaphoreType.DMA((N_DEV,)), pltpu.SemaphoreType.DMA((N_DEV,))],
        compiler_params=pltpu.CompilerParams(vmem_limit_bytes=VMEM_LIMIT),
    )(*parts, small, *w_blk, *m_blk, *v_blk, w_s, m_s, v_s)


def _reduce_blocks_call(parts, small, w_blk, m_blk, v_blk, w_s, m_s, v_s):
    chunks = (32, 48, 64, 16)

    def body(*refs):
        p_refs = refs[0:4]
        small_ref = refs[4]
        w_refs, m_refs, v_refs = refs[5:9], refs[9:13], refs[13:17]
        ws_ref, ms_ref, vs_ref = refs[17:20]
        g_refs, d_refs, nm_refs, nv_refs = refs[20:24], refs[24:28], refs[28:32], refs[32:36]
        gs_ref, ds_ref, nms_ref, nvs_ref = refs[36:40]
        r_refs = refs[40:44]
        recv_s_ref = refs[44]
        send_sems, recv_sems, send_s_sems, recv_s_sems, local_sems = refs[45:50]
        pos = _mesh_pos()
        me = _logical(pos)

        def big(w, k):
            peer = _peer(pos, k)
            return pltpu.make_async_remote_copy(
                src_ref=p_refs[w].at[_logical(peer)], dst_ref=r_refs[w].at[k],
                send_sem=send_sems.at[N_W * k + w], recv_sem=recv_sems.at[N_W * k + w],
                device_id=peer, device_id_type=pl.DeviceIdType.MESH)

        def tiny(k):
            return pltpu.make_async_remote_copy(
                src_ref=small_ref, dst_ref=recv_s_ref.at[k],
                send_sem=send_s_sems.at[k], recv_sem=recv_s_sems.at[k],
                device_id=_peer(pos, k), device_id_type=pl.DeviceIdType.MESH)

        own = [pltpu.make_async_copy(p_refs[w].at[me], r_refs[w].at[0], local_sems.at[w]) for w in range(N_W)]
        for cp in own:
            cp.start()
        for k in range(1, N_DEV):
            tiny(k).start()
        for k in range(1, N_DEV):
            for w in range(N_W):
                big(w, k).start()
        recv_s_ref[0] = small_ref[...]
        for k in range(1, N_DEV):
            tiny(k).wait_recv()
        acc = recv_s_ref[me]
        for a in range(1, N_DEV):
            acc = acc + recv_s_ref[lax.bitwise_xor(me, a)]
        row = lax.broadcasted_iota(jnp.int32, (R_SMALL, LANES), 0)
        gs = jnp.where(row == 8, acc + pltpu.roll(acc, 64, 1), acc)
        gs = jnp.where(row == ROW_LOSS, jnp.sum(acc, axis=1, keepdims=True) * (0.5 / D_MODEL), gs)
        gs_ref[...] = gs
        ds, nms, nvs = _adamw(ws_ref[...], gs, ms_ref[...], vs_ref[...])
        ds_ref[...] = ds
        nms_ref[...] = nms
        nvs_ref[...] = nvs

        for w in (3, 2, 1, 0):
            own[w].wait()
            for k in range(1, N_DEV):
                big(w, k).wait_recv()
            chunk = chunks[w]

            def step(t, carry, w=w, chunk=chunk):
                rows = pl.ds(pl.multiple_of(t * chunk, chunk), chunk)
                g = r_refs[w][0, rows, :].astype(F32)
                for k in range(1, N_DEV):
                    g = g + r_refs[w][k, rows, :].astype(F32)
                d, nm, nv = _adamw(w_refs[w][rows, :], g, m_refs[w][rows, :], v_refs[w][rows, :])
                g_refs[w][rows, :] = g
                d_refs[w][rows, :] = d
                nm_refs[w][rows, :] = nm
                nv_refs[w][rows, :] = nv
                return carry

            lax.fori_loop(0, W_BLOCKS[w][0] // chunk, step, 0)
        for k in range(1, N_DEV):
            tiny(k).wait_send()
            for w in range(N_W):
                big(w, k).wait_send()

    vm = pl.BlockSpec(memory_space=pltpu.VMEM)
    blk = [jax.ShapeDtypeStruct(shp, F32) for shp in W_BLOCKS]
    small_shape = jax.ShapeDtypeStruct((R_SMALL, LANES), F32)
    return pl.pallas_call(
        body, name="reduce_adamw",
        in_specs=[pl.BlockSpec(memory_space=pl.ANY)] * N_W + [vm] * 16,
        out_specs=[vm] * 20,
        out_shape=blk * 4 + [small_shape] * 4,
        scratch_shapes=[pltpu.VMEM((N_DEV,) + shp, BF16) for shp in W_BLOCKS]
        + [pltpu.VMEM((N_DEV, R_SMALL, LANES), F32),
           pltpu.SemaphoreType.DMA((N_DEV * N_W,)), pltpu.SemaphoreType.DMA((N_DEV * N_W,)),
           pltpu.SemaphoreType.DMA((N_DEV,)), pltpu.SemaphoreType.DMA((N_DEV,)), pltpu.SemaphoreType.DMA((N_W,))],
        compiler_params=pltpu.CompilerParams(vmem_limit_bytes=VMEM_LIMIT),
    )(*parts, small, *w_blk, *m_blk, *v_blk, w_s, m_s, v_s)


def _pack_shard(w_in, w_uq, w_ukv, w_out):
    return jnp.concatenate([a.reshape(-1, LANES) for a in (w_in, w_uq, w_ukv, w_out)], axis=0)


def _unpack_shard(p):
    w_in = p[0:R_WIN].reshape(1, D_MODEL, N_IN // N_DEV)
    w_uq = p[R_WIN:R_WIN + R_WUQ].reshape(1, B_Q_RANK // N_DEV, 384)
    w_ukv = p[R_WIN + R_WUQ:R_WIN + R_WUQ + R_WUKV].reshape(1, B_KV_RANK, 768 // N_DEV)
    w_out = p[R_WIN + R_WUQ + R_WUKV:].reshape(1, D_MODEL // N_DEV, D_MODEL)
    return w_in, w_uq, w_ukv, w_out


def _pack_small(norm_in, a_q, a_k, b_cq, b_ckv, b_q, b_k):
    def row(v):
        return jnp.pad(v.reshape(1, -1), ((0, 0), (0, LANES - v.size)))
    rows = [norm_in.reshape(8, LANES), row(a_q), row(a_k), b_cq.reshape(3, LANES), b_ckv.reshape(2, LANES),
            row(b_q), row(b_k), jnp.zeros((R_SMALL - 17, LANES), F32)]
    return jnp.concatenate(rows, axis=0)


def _unpack_small(s):
    return (s[0:8].reshape(1, D_MODEL), s[8:9, :64], s[9:10, :64], s[10:13].reshape(1, B_Q_RANK),
            s[13:15].reshape(1, B_KV_RANK), s[15:16, :B_QK_DIM], s[16:17, :B_QK_DIM])


def _full_weights(gathered):
    g = gathered
    w_in = g[:, 0:R_WIN].reshape(N_DEV, D_MODEL, N_IN // N_DEV).transpose(1, 0, 2).reshape(D_MODEL, N_IN)
    w_uq = g[:, R_WIN:R_WIN + R_WUQ].reshape(B_Q_RANK, 384)
    w_ukv = g[:, R_WIN + R_WUQ:R_WIN + R_WUQ + R_WUKV].reshape(N_DEV, B_KV_RANK, 768 // N_DEV)
    w_ukv = w_ukv.transpose(1, 0, 2).reshape(B_KV_RANK, 768)
    w_out = g[:, R_WIN + R_WUQ + R_WUKV:].reshape(D_MODEL, D_MODEL)
    k0, k1 = w_in[:, 512:576], w_in[:, 576:640]
    v0, v1 = w_in[:, 640:704], w_in[:, 704:768]
    kr = w_in[:, 1920:1952]
    z64 = jnp.zeros((D_MODEL, 64), BF16)
    z32 = jnp.zeros((D_MODEL, 32), BF16)
    kr_blk = jnp.concatenate([z64, kr, z32], axis=1)
    w_wide = jnp.concatenate([w_in[:, 0:512], k0, k0, k1, k1, v0, v0, v1, v1, w_in[:, 768:1280], w_in[:, 1280:1664],
                              w_in[:, 1664:1920], kr_blk, kr_blk, kr_blk, kr_blk, w_in[:, 1952:2464]], axis=1)
    wuq = jnp.pad(w_uq.reshape(B_Q_RANK, B_HEADS, B_QK_DIM), ((0, 0), (0, 0), (0, LANES - B_QK_DIM)))
    wuq = wuq.reshape(B_Q_RANK, 512)
    ukv = w_ukv.reshape(B_KV_RANK, B_HEADS, B_NOPE_DIM + B_V_DIM)
    wuk = jnp.pad(ukv[:, :, :B_NOPE_DIM], ((0, 0), (0, 0), (0, LANES - B_NOPE_DIM))).reshape(B_KV_RANK, 512)
    wuv = ukv[:, :, B_NOPE_DIM:].reshape(B_KV_RANK, 512)
    return w_wide, wuq, wuk, wuv, w_out


def _narrow_grads(dw_wide, dwuq, dwuk, dwuv):
    dw_in = jnp.concatenate([
        dw_wide[:, 0:512], dw_wide[:, 512:576], dw_wide[:, 640:704], dw_wide[:, 768:832], dw_wide[:, 896:960],
        dw_wide[:, 1024:1536], dw_wide[:, 1536:1920], dw_wide[:, 1920:2176],
        dw_wide[:, O_KR + 64:O_KR + 96], dw_wide[:, 2688:3200]], axis=1)
    dw_uq = dwuq.reshape(B_Q_RANK, B_HEADS, LANES)[:, :, :B_QK_DIM].reshape(B_Q_RANK, 384)
    dk = dwuk.reshape(B_KV_RANK, B_HEADS, LANES)[:, :, :B_NOPE_DIM]
    dv = dwuv.reshape(B_KV_RANK, B_HEADS, B_V_DIM)
    dw_ukv = jnp.concatenate([dk, dv], axis=2).reshape(B_KV_RANK, 768)
    return dw_in, dw_uq, dw_ukv


C_IN = N_IN // N_DEV
_RUNS = ((0, 512, O_QA), (512, 576, O_KA), (576, 640, O_KA + 128), (640, 704, O_VA), (704, 768, O_VA + 128),
         (768, 1280, O_GA), (1280, 1664, O_CQ), (1664, 1920, O_CKV), (1920, 1952, O_KR + 64), (1952, 2464, O_GB))


def _in_cols(g_in, lo, hi):
    out = []
    for d in range(N_DEV):
        a, b = max(lo, C_IN * d), min(hi, C_IN * (d + 1))
        if a < b:
            out.append(g_in[d][:, a - C_IN * d:b - C_IN * d])
    return out


def _widen_weights(g_in, g_uq, g_ukv, g_out):
    z64 = jnp.zeros((D_MODEL, 64), BF16)
    z32 = jnp.zeros((D_MODEL, 32), BF16)
    k0, k1 = _in_cols(g_in, 512, 576), _in_cols(g_in, 576, 640)
    v0, v1 = _in_cols(g_in, 640, 704), _in_cols(g_in, 704, 768)
    kr_blk = [z64] + _in_cols(g_in, 1920, 1952) + [z32]
    w_wide = jnp.concatenate(
        _in_cols(g_in, 0, 512) + k0 + k0 + k1 + k1 + v0 + v0 + v1 + v1 + _in_cols(g_in, 768, 1920)
        + kr_blk * B_HEADS + _in_cols(g_in, 1952, 2464), axis=1)
    w_uq = g_uq.reshape(B_Q_RANK, 384)
    wuq = jnp.pad(w_uq.reshape(B_Q_RANK, B_HEADS, B_QK_DIM), ((0, 0), (0, 0), (0, LANES - B_QK_DIM)))
    wuq = wuq.reshape(B_Q_RANK, 512)
    ukv = g_ukv.transpose(1, 0, 2).reshape(B_KV_RANK, B_HEADS, B_NOPE_DIM + B_V_DIM)
    wuk = jnp.pad(ukv[:, :, :B_NOPE_DIM], ((0, 0), (0, 0), (0, LANES - B_NOPE_DIM))).reshape(B_KV_RANK, 512)
    wuv = ukv[:, :, B_NOPE_DIM:].reshape(B_KV_RANK, 512)
    return w_wide, wuq, wuk, wuv, g_out.reshape(D_MODEL, D_MODEL)


def _grad_blocks(dw_wide, dwuq, dwuk, dwuv, dw_out):
    blocks = []
    for d in range(N_DEV):
        pieces = []
        for lo, hi, wide in _RUNS:
            a, b = max(lo, C_IN * d), min(hi, C_IN * (d + 1))
            if a < b:
                pieces.append(dw_wide[:, wide + a - lo:wide + b - lo])
        blocks.append(jnp.concatenate(pieces, axis=1))
    p_in = jnp.stack(blocks).astype(BF16)
    dw_uq = dwuq.reshape(B_Q_RANK, B_HEADS, LANES)[:, :, :B_QK_DIM].reshape(N_DEV, B_Q_RANK // N_DEV, 384)
    dk = dwuk.reshape(B_KV_RANK, B_HEADS, LANES)[:, :, :B_NOPE_DIM]
    dv = dwuv.reshape(B_KV_RANK, B_HEADS, B_V_DIM)
    dw_ukv = jnp.concatenate([dk, dv], axis=2).reshape(B_KV_RANK, N_DEV, 768 // N_DEV).transpose(1, 0, 2)
    return (p_in, dw_uq.astype(BF16), dw_ukv.astype(BF16),
            dw_out.reshape(N_DEV, D_MODEL // N_DEV, D_MODEL).astype(BF16))


def _rope_tables(s_len):
    row = jnp.arange(s_len // GRID_W, dtype=F32)
    col = jnp.arange(GRID_W, dtype=F32)

    def parts(dim):
        half = dim // 2
        inv = 1.0 / (ROPE_THETA ** (jnp.arange(0, half, 2, dtype=F32) / half))
        ar, ac = row[:, None] * inv[None, :], col[:, None] * inv[None, :]
        zr, zc = jnp.zeros_like(ar), jnp.zeros_like(ac)
        cos_c = jnp.concatenate([zc, zc, jnp.cos(ac), jnp.cos(ac)], axis=1)
        cos_r = jnp.concatenate([jnp.cos(ar), jnp.cos(ar), zr, zr], axis=1)
        sin_c = jnp.concatenate([zc, zc, -jnp.sin(ac), jnp.sin(ac)], axis=1)
        sin_r = jnp.concatenate([-jnp.sin(ar), jnp.sin(ar), zr, zr], axis=1)
        return cos_c, cos_r, sin_c, sin_r

    tabs = [jnp.tile(t, (1, 2)) for t in parts(A_HEAD_DIM)]
    for n, t in enumerate(parts(B_ROPE_DIM)):
        lead = jnp.full((t.shape[0], B_NOPE_DIM), 1.0 if n == 0 else 0.0, F32)
        tail = jnp.full((t.shape[0], LANES - B_QK_DIM), 1.0 if n == 0 else 0.0, F32)
        tabs.append(jnp.concatenate([lead, t, tail], axis=1))
    return tuple(tabs)


def _lane_major(a, step):
    return a[:, ::step].T[:, None, :]


def kernel(x, norm_in, w_in, a_q_norm, a_k_norm, b_cq_norm, b_ckv_norm, w_uq, w_ukv, b_q_norm, b_k_norm, w_out, loss_target, m_norm_in, m_w_in, m_a_q_norm, m_a_k_norm, m_b_cq_norm, m_b_ckv_norm, m_w_uq, m_w_ukv, m_b_q_norm, m_b_k_norm, m_w_out, v_norm_in, v_w_in, v_a_q_norm, v_a_k_norm, v_b_cq_norm, v_b_ckv_norm, v_w_uq, v_w_ukv, v_b_q_norm, v_b_k_norm, v_w_out):
    s_len = x.shape[1]
    tm = min(256, s_len)
    tq, tk = min(512, s_len), min(2048, s_len)
    ftq, ftk, fbufs = min(256, s_len), min(1024, s_len), 2
    x2 = x.reshape(s_len, D_MODEL)
    t2 = loss_target.reshape(s_len, D_MODEL)

    w_blk = (w_in[0], w_uq[0], w_ukv[0], w_out[0])
    w_wide, wuq, wuk, wuv, wout = _widen_weights(*_gather_blocks_call(w_blk))

    def dup(v, pad_to=None):
        v = v.reshape(1, -1)
        if pad_to is None:
            return jnp.concatenate([v, v], axis=1)
        return jnp.pad(v, ((0, 0), (0, pad_to - v.shape[1])))

    g64 = jnp.asarray(np.kron(np.eye(2), np.ones((64, 64))), dtype=BF16)
    consts = dict(gin=norm_in, w=w_wide, wuq=wuq, wuk=wuk, wuv=wuv, gq=dup(a_q_norm), gk=dup(a_k_norm),
                  gcq=b_cq_norm, gckv=b_ckv_norm, gqb=dup(b_q_norm, LANES), gkb=dup(b_k_norm, LANES), g64=g64)
    tabs = _rope_tables(s_len)

    qa, ka, va, va_t, ga, qb, kb, vb, vb_t, gb, raw, xnb = _pre_fwd_call(x2, consts, tabs, min(512, s_len))
    oa, lse_a_t = _attn_fwd_t_call(qa, ka, va_t, groups=A_KV_HEADS, sub=4, masked=True, scale=None, tq=ftq, tk=ftk,
                                   name="attn_fwd_a", s_bufs=fbufs)
    ob, lse_b_t = _attn_fwd_t_call(qb, kb, vb_t, groups=1, sub=B_HEADS, masked=False, scale=None, tq=ftq,
                                   tk=ftk, name="attn_fwd_b", s_bufs=fbufs)
    dh, doa, dob, dga, dgb, dw_out, loss_row = _out_call(x2, t2, oa, ob, ga, gb, wout, min(512, s_len))

    dqa, dka_t, dva_t = _attn_bwd_q_call(qa, ka, va, doa, oa, lse_a_t, groups=A_KV_HEADS, sub=4, masked=True,
                                         tq=tq, tk=tk, ck=min(512, s_len), name="attn_bwd_a")
    dqb, dkb_t, dvb_t = _attn_bwd_q_call(qb, kb, vb, dob, ob, lse_b_t, groups=2, sub=2, masked=False,
                                         tq=tq, tk=min(4096, s_len), ck=min(256, s_len), name="attn_bwd_b")
    dx, dproj_b, dwuq, dwuk, dwuv, small = _pre_bwd_call(
        x2, raw, dh, dqa, dka_t, dva_t, dga, dqb, dkb_t, dvb_t, dgb, consts, tabs, tm)
    dw_wide = _dw_in_call(xnb, dproj_b, min(1024, s_len), N_WIDE)

    parts = _grad_blocks(dw_wide, dwuq, dwuk, dwuv, dw_out)
    small = jnp.concatenate([small[:ROW_LOSS], loss_row, small[ROW_LOSS + 1:]], axis=0)

    m_blk = (m_w_in[0], m_w_uq[0], m_w_ukv[0], m_w_out[0])
    v_blk = (v_w_in[0], v_w_uq[0], v_w_ukv[0], v_w_out[0])
    w_s = _pack_small(norm_in, a_q_norm, a_k_norm, b_cq_norm, b_ckv_norm, b_q_norm, b_k_norm)
    m_s = _pack_small(m_norm_in, m_a_q_norm, m_a_k_norm, m_b_cq_norm, m_b_ckv_norm, m_b_q_norm, m_b_k_norm)
    v_s = _pack_small(v_norm_in, v_a_q_norm, v_a_k_norm, v_b_cq_norm, v_b_ckv_norm, v_b_q_norm, v_b_k_norm)
    res = _reduce_two_level_call(parts, small, w_blk, m_blk, v_blk, w_s, m_s, v_s)

    def leaves(blocks, sm):
        wi, uq, ukv, wo = [b[None] for b in blocks]
        n_in, aq, ak, bcq, bckv, bq, bk = _unpack_small(sm)
        return [n_in, wi, aq, ak, bcq, bckv, uq, ukv, bq, bk, wo]

    g_s = res[16]
    loss = g_s[ROW_LOSS, 0]
    grad_x = dx.reshape(1, s_len, D_MODEL)
    return (loss, grad_x, *leaves(res[0:4], res[16]), *leaves(res[4:8], res[17]), *leaves(res[8:12], res[18]),
            *leaves(res[12:16], res[19]))
```

```python
import functools

import numpy as np
import jax
import jax.numpy as jnp
from jax import lax
from jax.experimental import pallas as pl
from jax.experimental.pallas import tpu as pltpu

F32 = jnp.float32
BF16 = jnp.bfloat16

D_MODEL = 1024
GRID_W = 64
ROPE_THETA = 10000.0
EPS = 1e-6
A_HEAD_DIM = 64
A_HEADS = 8
A_KV_HEADS = 2
B_HEADS = 4
B_NOPE_DIM = 64
B_ROPE_DIM = 32
B_QK_DIM = 96
B_V_DIM = 128
B_Q_RANK = 384
B_KV_RANK = 256
N_IN = 2464
N_DEV = 8

ADAM_LR = 0.001
ADAM_B1 = 0.9
ADAM_B2 = 0.999
ADAM_EPS = 1e-08
ADAM_WD = 0.01
ADAM_STEP = 10

QA_SCALE = 0.125
QB_SCALE = 1.0 / float(np.sqrt(B_QK_DIM))

LANES = 128
O_QA, O_KA, O_VA, O_GA, O_CQ, O_CKV, O_KR, O_GB, N_WIDE = 0, 512, 768, 1024, 1536, 1920, 2176, 2688, 3200
R_QA, R_KA, R_CQ, R_CKV, R_KR, R_WIDTH = 0, 512, 768, 1152, 1408, 1920

R_SMALL = 24
ROW_LOSS = 17

VMEM_LIMIT = 56 * 1024 * 1024

NT = (((1,), (1,)), ((), ()))
TN = (((0,), (0,)), ((), ()))


def _dot(a, b):
    return jnp.dot(a, b, preferred_element_type=F32)


def _dot_nt(a, b):
    return lax.dot_general(a, b, NT, preferred_element_type=F32)


def _dot_tn(a, b):
    return lax.dot_general(a, b, TN, preferred_element_type=F32)


def _params(sem=None):
    return pltpu.CompilerParams(dimension_semantics=sem, vmem_limit_bytes=VMEM_LIMIT)


def _full(shape):
    nd = len(shape)
    return pl.BlockSpec(shape, lambda *_: (0,) * nd)


def _swap_sel(rows, shift):
    lane = lax.broadcasted_iota(jnp.int32, (rows, LANES), 1)
    return pltpu.roll(lane, shift, 1) == (lane ^ shift)


def _swap(x, shift, sel):
    return jnp.where(sel, pltpu.roll(x, shift, 1), pltpu.roll(x, LANES - shift, 1))


def _group_sum64(x, g64):
    hi = x.astype(BF16)
    lo = (x - hi.astype(F32)).astype(BF16)
    return _dot(hi, g64) + _dot(lo, g64)


def _row_sum(x):
    return jnp.sum(x, axis=-1, keepdims=True)


def _col_fwd(xs, msum, denom, gain, cos, sin, shift, sel):
    r = lax.rsqrt(msum(xs * xs) * (1.0 / denom) + EPS)
    xh = xs * r
    n = xh * gain
    return n * cos + _swap(n, shift, sel) * sin, xh, r


def _col_bwd(d_out, xh, r, msum, denom, gain, cos, sin, shift, sel):
    dn = d_out * cos + _swap(d_out * sin, shift, sel)
    dgain = jnp.sum(dn * xh, axis=0, keepdims=True)
    dxh = dn * gain
    dx = r * (dxh - xh * (msum(dxh * xh) * (1.0 / denom)))
    return dx, dgain


def _rms_fwd(x, gain):
    r = lax.rsqrt(jnp.mean(x * x, axis=-1, keepdims=True) + EPS)
    xh = x * r
    return xh * gain, xh, r


def _rms_bwd(dy, xh, r, gain):
    dgain = jnp.sum(dy * xh, axis=0, keepdims=True)
    dxh = dy * gain
    dx = r * (dxh - xh * jnp.mean(dxh * xh, axis=-1, keepdims=True))
    return dx, dgain


def _pre_forward(x, gin, w, wuq, wuk, wuv, gq, gk, gcq, gckv, gqb, gkb, ca, sa, cb, sb, g64, tm, raw=None):
    sel16 = _swap_sel(tm, 16)
    sel8 = _swap_sel(tm, 8)
    xn, xh0, r0 = _rms_fwd(x, gin)
    xnb = xn.astype(BF16)
    proj = None
    if raw is None:
        proj = _dot(xnb, w)
        raw = jnp.concatenate([proj[:, O_QA:O_QA + 512], proj[:, O_KA:O_KA + 256], proj[:, O_CQ:O_CQ + B_Q_RANK],
                               proj[:, O_CKV:O_CKV + B_KV_RANK], proj[:, O_KR:O_KR + 512]], axis=1)
    gs64 = functools.partial(_group_sum64, g64=g64)
    qa = [_col_fwd(raw[:, R_QA + LANES * s:R_QA + LANES * (s + 1)], gs64, 64.0, gq, ca, sa, 16, sel16)
          for s in range(4)]
    ka = [_col_fwd(raw[:, R_KA + LANES * s:R_KA + LANES * (s + 1)], _row_sum, 128.0, gk, ca, sa, 16, sel16)
          for s in range(2)]
    cq, cqh, rcq = _rms_fwd(raw[:, R_CQ:R_CQ + B_Q_RANK], gcq)
    cqb = cq.astype(BF16)
    qb_raw = _dot(cqb, wuq)
    qb = [_col_fwd(qb_raw[:, LANES * h:LANES * (h + 1)], _row_sum, float(B_QK_DIM), gqb, cb, sb, 8, sel8)
          for h in range(B_HEADS)]
    ckv, ckvh, rckv = _rms_fwd(raw[:, R_CKV:R_CKV + B_KV_RANK], gckv)
    ckvb = ckv.astype(BF16)
    kb_raw = _dot(ckvb, wuk) + raw[:, R_KR:R_KR + 512]
    vb = _dot(ckvb, wuv)
    kb = [_col_fwd(kb_raw[:, LANES * h:LANES * (h + 1)], _row_sum, float(B_QK_DIM), gkb, cb, sb, 8, sel8)
          for h in range(B_HEADS)]
    return dict(xh0=xh0, r0=r0, xnb=xnb, proj=proj, raw=raw, qa=qa, ka=ka, cqh=cqh, rcq=rcq, cqb=cqb, qb=qb,
                ckvh=ckvh, rckv=rckv, ckvb=ckvb, kb=kb, vb=vb, sel16=sel16, sel8=sel8, gs64=gs64)


def _rope_tiles(tab_refs, i, tm):
    per_tile = tm // GRID_W
    out = []
    for t in range(4):
        col_ref, row_ref = tab_refs[2 * t], tab_refs[2 * t + 1]
        col = col_ref[...]
        out.append(jnp.concatenate([col + row_ref[pl.ds(i * per_tile + b, 1), :] for b in range(per_tile)], axis=0))
    return out


_PRE_IN_NAMES = ("gin", "w", "wuq", "wuk", "wuv", "gq", "gk", "gcq", "gckv", "gqb", "gkb", "g64")


def _pre_const_specs(consts):
    return [_full(consts[n].shape) for n in _PRE_IN_NAMES]


def _pre_fwd_call(x, consts, tabs, tm):
    s_len = x.shape[0]
    ts = min(256, tm)

    def body(x_ref, *refs):
        tab_refs, refs = refs[:8], refs[8:]
        (gin_ref, w_ref, wuq_ref, wuk_ref, wuv_ref, gq_ref, gk_ref, gcq_ref, gckv_ref, gqb_ref, gkb_ref, g64_ref,
         qa_ref, ka_ref, va_ref, vat_ref, ga_ref, qb_ref, kb_ref, vb_ref, vbt_ref, gb_ref, raw_ref, xnb_ref) = refs
        for part in range(tm // ts):
            r = slice(ts * part, ts * (part + 1))
            ca, sa, cb, sb = _rope_tiles(tab_refs, pl.program_id(0) * (tm // ts) + part, ts)
            f = _pre_forward(x_ref[r, :], gin_ref[...], w_ref[...], wuq_ref[...], wuk_ref[...], wuv_ref[...],
                             gq_ref[...], gk_ref[...], gcq_ref[...], gckv_ref[...], gqb_ref[...], gkb_ref[...],
                             ca, sa, cb, sb, g64_ref[...], ts)
            proj = f["proj"]
            raw_ref[r, :] = f["raw"]
            xnb_ref[r, :] = f["xnb"]
            for s in range(4):
                qa_ref[r, LANES * s:LANES * (s + 1)] = (f["qa"][s][0] * QA_SCALE).astype(BF16)
            for s in range(2):
                ka_ref[r, LANES * s:LANES * (s + 1)] = f["ka"][s][0].astype(BF16)
            va = proj[:, O_VA:O_VA + 256]
            va_ref[r, :] = va.astype(BF16)
            vat_ref[:, r] = va.T.astype(BF16)
            ga_ref[r, :] = proj[:, O_GA:O_GA + 512]
            for h in range(B_HEADS):
                qb_ref[r, LANES * h:LANES * (h + 1)] = (f["qb"][h][0] * QB_SCALE).astype(BF16)
                kb_ref[r, LANES * h:LANES * (h + 1)] = f["kb"][h][0].astype(BF16)
            vb_ref[r, :] = f["vb"].astype(BF16)
            vbt_ref[:, r] = f["vb"].T.astype(BF16)
            gb_ref[r, :] = proj[:, O_GB:O_GB + 512]

    def rows(width):
        return pl.BlockSpec((tm, width), lambda i: (i, 0))

    def cols(height):
        return pl.BlockSpec((height, tm), lambda i: (0, i))

    outs = [((s_len, 512), BF16, rows(512)), ((s_len, 256), BF16, rows(256)), ((s_len, 256), BF16, rows(256)),
            ((256, s_len), BF16, cols(256)), ((s_len, 512), F32, rows(512)), ((s_len, 512), BF16, rows(512)),
            ((s_len, 512), BF16, rows(512)), ((s_len, 512), BF16, rows(512)), ((512, s_len), BF16, cols(512)),
            ((s_len, 512), F32, rows(512)), ((s_len, R_WIDTH), F32, rows(R_WIDTH)),
            ((s_len, D_MODEL), BF16, rows(D_MODEL))]
    return pl.pallas_call(
        body, name="pre_fwd", grid=(s_len // tm,),
        in_specs=[rows(D_MODEL)] + [_full(t.shape) for t in tabs] + _pre_const_specs(consts),
        out_specs=[sp for _, _, sp in outs],
        out_shape=[jax.ShapeDtypeStruct(sh, dt) for sh, dt, _ in outs],
        compiler_params=_params(("arbitrary",)),
    )(x, *tabs, *[consts[n] for n in _PRE_IN_NAMES])


def _head_masks(rows):
    lane = lax.broadcasted_iota(jnp.int32, (rows, LANES), 1)
    return lane < 64, lane >= 64


def _row_fold(x, op):
    return op(x.reshape(x.shape[0] // 8, 8, x.shape[1]), axis=0)


def _attn_fwd_t_call(q, k, vt, *, groups, sub, masked, scale, tq, tk, name, s_bufs=2):
    s_len = q.shape[0]
    qw = LANES * (sub // 2 if masked else sub)
    kvw = LANES if masked else LANES * sub
    n_c = s_len // tk
    kv_mode = pl.Buffered(1) if groups == 1 else None

    def body(q_ref, k_ref, vt_ref, o_ref, lse_ref, s_sc):
        keep = _head_masks(tq) if masked else None

        def kv_of(hh):
            return slice(0, LANES) if masked else slice(LANES * hh, LANES * (hh + 1))

        def q_of(hh):
            if not masked:
                return q_ref[:, LANES * hh:LANES * (hh + 1)]
            qp = q_ref[:, LANES * (hh // 2):LANES * (hh // 2 + 1)]
            return jnp.where(keep[hh % 2], qp, jnp.zeros_like(qp))

        def scores(hh, qm, c, mx):
            s_t = _dot_nt(k_ref[tk * c:tk * (c + 1), kv_of(hh)], qm)
            if scale is not None:
                s_t = s_t * scale
            s_sc[hh % s_bufs, c] = s_t
            return jnp.maximum(mx, _row_fold(s_t, jnp.max))

        neg = jnp.full((8, tq), -jnp.inf, F32)
        qm_next = q_of(0)
        mx_next = neg
        for c in range(n_c):
            mx_next = scores(0, qm_next, c, mx_next)
        outs = []
        for hh in range(sub):
            m = jnp.max(mx_next, axis=0, keepdims=True)
            if hh + 1 < sub:
                qm_next = q_of(hh + 1)
                mx_next = neg
            lsum = jnp.zeros((8, tq), F32)
            acc = jnp.zeros((LANES, tq), F32)
            for c in range(n_c):
                p_t = jnp.exp(s_sc[hh % s_bufs, c] - m)
                lsum = lsum + _row_fold(p_t, jnp.sum)
                if hh + 1 < sub:
                    mx_next = scores(hh + 1, qm_next, c, mx_next)
                acc = acc + _dot(vt_ref[kv_of(hh), tk * c:tk * (c + 1)], p_t.astype(BF16))
            l = jnp.sum(lsum, axis=0, keepdims=True)
            outs.append((acc / l).T)
            lse_ref[hh] = m + jnp.log(l)
        if masked:
            for pr in range(sub // 2):
                o_ref[:, LANES * pr:LANES * (pr + 1)] = jnp.where(keep[0], outs[2 * pr], outs[2 * pr + 1])
        else:
            for hh in range(sub):
                o_ref[:, LANES * hh:LANES * (hh + 1)] = outs[hh]

    return pl.pallas_call(
        body, name=name, grid=(groups, s_len // tq),
        in_specs=[pl.BlockSpec((tq, qw), lambda g, i: (i, g)),
                  pl.BlockSpec((s_len, kvw), lambda g, i: (0, g), pipeline_mode=kv_mode),
                  pl.BlockSpec((kvw, s_len), lambda g, i: (g, 0), pipeline_mode=kv_mode)],
        out_specs=[pl.BlockSpec((tq, qw), lambda g, i: (i, g)),
                   pl.BlockSpec((sub, 1, tq), lambda g, i: (g, 0, i))],
        out_shape=[jax.ShapeDtypeStruct((s_len, groups * qw), F32),
                   jax.ShapeDtypeStruct((groups * sub, 1, s_len), F32)],
        scratch_shapes=[pltpu.VMEM((s_bufs, n_c, tk, tq), F32)],
        compiler_params=_params(("arbitrary", "arbitrary")),
    )(q, k, vt)


def _attn_bwd_q_call(q, k, v, do, o, lse_t, *, groups, sub, masked, tq, tk, ck, name):
    s_len = q.shape[0]
    qw = LANES * (sub // 2 if masked else sub)
    kvw = LANES if masked else LANES * sub
    n_c = tk // ck

    def body(q_ref, k_ref, v_ref, do_ref, o_ref, lse_ref, dq_ref, dkt_ref, dvt_ref):
        j = pl.program_id(1)
        i = pl.program_id(2)

        @pl.when((j == 0) & (i == 0))
        def _():
            dq_ref[...] = jnp.zeros(dq_ref.shape, F32)

        @pl.when(i == 0)
        def _():
            dkt_ref[...] = jnp.zeros(dkt_ref.shape, F32)
            dvt_ref[...] = jnp.zeros(dvt_ref.shape, F32)

        lkeep = _head_masks(tq) if masked else None
        heads = []
        for hh in range(sub):
            if masked:
                cols = slice(LANES * (hh // 2), LANES * (hh // 2 + 1))
                kv = slice(0, LANES)
                qp, dop = q_ref[:, cols], do_ref[:, cols]
                qm = jnp.where(lkeep[hh % 2], qp, jnp.zeros_like(qp))
                dom = jnp.where(lkeep[hh % 2], dop, jnp.zeros_like(dop))
            else:
                cols = kv = slice(LANES * hh, LANES * (hh + 1))
                qm, dom = q_ref[:, cols], do_ref[:, cols]
            delta = jnp.sum(dom.astype(F32) * o_ref[:, cols], axis=1, keepdims=True)
            lse = jnp.broadcast_to(lse_ref[hh], (LANES, tq)).T[:, 0:1]
            heads.append((cols, kv, qm, dom, qm.T, dom.T, delta, lse))

        def products(hh, c):
            _, kv, qm, dom, _, _, _, _ = heads[hh]
            return _dot_nt(qm, k_ref[ck * c:ck * (c + 1), kv]), _dot_nt(dom, v_ref[ck * c:ck * (c + 1), kv])

        items = [(hh, c) for hh in range(sub) for c in range(n_c)]
        dq_acc = [jnp.zeros((tq, LANES), F32) for _ in range(sub)]
        nxt = products(*items[0])
        for n, (hh, c) in enumerate(items):
            s, dp = nxt
            if n + 1 < len(items):
                nxt = products(*items[n + 1])
            _, kv, qm, dom, qmt, domt, delta, lse = heads[hh]
            p = jnp.exp(s - lse)
            ds = p * (dp - delta)
            p_b = p.astype(BF16)
            ds_b = ds.astype(BF16)
            kcols = slice(ck * c, ck * (c + 1))
            dvt_ref[kv, kcols] += _dot(domt, p_b)
            dkt_ref[kv, kcols] += _dot(qmt, ds_b)
            dq_acc[hh] = dq_acc[hh] + _dot(ds_b, k_ref[kcols, kv])
        rows = pl.ds(pl.multiple_of(i * tq, tq), tq)
        if masked:
            for pr in range(sub // 2):
                dq_ref[rows, LANES * pr:LANES * (pr + 1)] += jnp.where(lkeep[0], dq_acc[2 * pr], dq_acc[2 * pr + 1])
        else:
            for hh in range(sub):
                dq_ref[rows, LANES * hh:LANES * (hh + 1)] += dq_acc[hh]

    return pl.pallas_call(
        body, name=name, grid=(groups, s_len // tk, s_len // tq),
        in_specs=[pl.BlockSpec((tq, qw), lambda g, j, i: (i, g)),
                  pl.BlockSpec((tk, kvw), lambda g, j, i: (j, g)),
                  pl.BlockSpec((tk, kvw), lambda g, j, i: (j, g)),
                  pl.BlockSpec((tq, qw), lambda g, j, i: (i, g)),
                  pl.BlockSpec((tq, qw), lambda g, j, i: (i, g)),
                  pl.BlockSpec((sub, 1, tq), lambda g, j, i: (g, 0, i))],
        out_specs=[pl.BlockSpec((s_len, qw), lambda g, j, i: (0, g)),
                   pl.BlockSpec((kvw, tk), lambda g, j, i: (g, j)),
                   pl.BlockSpec((kvw, tk), lambda g, j, i: (g, j))],
        out_shape=[jax.ShapeDtypeStruct((s_len, groups * qw), F32),
                   jax.ShapeDtypeStruct((groups * kvw, s_len), F32),
                   jax.ShapeDtypeStruct((groups * kvw, s_len), F32)],
        compiler_params=_params(("arbitrary", "arbitrary", "arbitrary")),
    )(q, k, v, do, o, lse_t)


def _silu_parts(g):
    sig = 1.0 / (1.0 + jnp.exp(-g))
    return g * sig, sig * (1.0 + g * (1.0 - sig))


def _out_call(x, target, oa, ob, ga, gb, wout, tm):
    s_len = x.shape[0]
    n_t = s_len // tm

    def body(x_ref, t_ref, oa_ref, ob_ref, ga_ref, gb_ref, w_ref,
             dh_ref, doa_ref, dob_ref, dga_ref, dgb_ref, dwb_ref, loss_ref, dw_ref):
        i = pl.program_id(0)

        @pl.when(i == 0)
        def _():
            dw_ref[...] = jnp.zeros(dw_ref.shape, F32)
            loss_ref[...] = jnp.zeros(loss_ref.shape, F32)

        oa_v, ob_v = oa_ref[...], ob_ref[...]
        silu_a, dsilu_a = _silu_parts(ga_ref[...])
        silu_b, dsilu_b = _silu_parts(gb_ref[...])
        ya = (oa_v * silu_a).astype(BF16)
        yb = (ob_v * silu_b).astype(BF16)
        h = x_ref[...] + _dot(ya, w_ref[0:512, :]) + _dot(yb, w_ref[512:1024, :])
        err = h - t_ref[...]
        part = jnp.sum(err * err, axis=0, keepdims=True)
        acc = part[:, 0:LANES]
        for c in range(1, D_MODEL // LANES):
            acc = acc + part[:, LANES * c:LANES * (c + 1)]
        loss_ref[...] += acc
        dh = err * (1.0 / D_MODEL)
        dh_ref[...] = dh
        dhb = dh.astype(BF16)
        dya = _dot_nt(dhb, w_ref[0:512, :])
        dyb = _dot_nt(dhb, w_ref[512:1024, :])
        doa = dya * silu_a
        dob = dyb * silu_b
        doa_ref[...] = doa.astype(BF16)
        dob_ref[...] = dob.astype(BF16)
        dga_ref[...] = dya * oa_v * dsilu_a
        dgb_ref[...] = dyb * ob_v * dsilu_b
        dw_ref[0:512, :] += _dot_tn(ya, dhb)
        dw_ref[512:1024, :] += _dot_tn(yb, dhb)

        @pl.when(i == n_t - 1)
        def _():
            dwb_ref[...] = dw_ref[...].astype(BF16)

    def rows(width):
        return pl.BlockSpec((tm, width), lambda i: (i, 0))

    outs = [(D_MODEL, F32), (512, BF16), (512, BF16), (512, F32), (512, F32)]
    return pl.pallas_call(
        body, name="out_fwd", grid=(n_t,),
        in_specs=[rows(D_MODEL), rows(D_MODEL), rows(512), rows(512), rows(512), rows(512),
                  _full((D_MODEL, D_MODEL))],
        out_specs=[rows(wd) for wd, _ in outs] + [_full((D_MODEL, D_MODEL)), _full((1, LANES))],
        out_shape=[jax.ShapeDtypeStruct((s_len, wd), dt) for wd, dt in outs]
        + [jax.ShapeDtypeStruct((D_MODEL, D_MODEL), BF16), jax.ShapeDtypeStruct((1, LANES), F32)],
        scratch_shapes=[pltpu.VMEM((D_MODEL, D_MODEL), F32)],
        compiler_params=_params(("arbitrary",)),
    )(x, target, oa, ob, ga, gb, wout)


def _pre_bwd_call(x, raw, dh, dqa, dka, dva, dga, dqb, dkb, dvb, dgb, consts, tabs, tm):
    s_len = x.shape[0]
    ts = min(256, tm)

    def body(x_ref, raw_ref, dh_ref, dqa_ref, dkat_ref, dvat_ref, dga_ref, dqb_ref, dkbt_ref, dvbt_ref, dgb_ref,
             *refs):
        tab_refs, refs = refs[:8], refs[8:]
        (gin_ref, w_ref, wuq_ref, wuk_ref, wuv_ref, gq_ref, gk_ref, gcq_ref, gckv_ref, gqb_ref, gkb_ref, g64_ref,
         dx_ref, dproj_ref, dwuq_ref, dwuk_ref, dwuv_ref, small_ref) = refs
        i = pl.program_id(0)

        @pl.when(i == 0)
        def _():
            dwuq_ref[...] = jnp.zeros(dwuq_ref.shape, F32)
            dwuk_ref[...] = jnp.zeros(dwuk_ref.shape, F32)
            dwuv_ref[...] = jnp.zeros(dwuv_ref.shape, F32)
            small_ref[...] = jnp.zeros(small_ref.shape, F32)

        gin, gq, gk = gin_ref[...], gq_ref[...], gk_ref[...]
        gcq, gckv, gqb, gkb = gcq_ref[...], gckv_ref[...], gqb_ref[...], gkb_ref[...]
        w, wuq, wuk, wuv = w_ref[...], wuq_ref[...], wuk_ref[...], wuv_ref[...]
        for part in range(tm // ts):
            r_ = slice(ts * part, ts * (part + 1))
            dka_v, dva_v = dkat_ref[:, r_].T, dvat_ref[:, r_].T
            dkb_v, dvb_v = dkbt_ref[:, r_].T, dvbt_ref[:, r_].T
            ca, sa, cb, sb = _rope_tiles(tab_refs, i * (tm // ts) + part, ts)
            f = _pre_forward(x_ref[r_, :], gin, w, wuq, wuk, wuv, gq, gk, gcq, gckv, gqb, gkb,
                             ca, sa, cb, sb, g64_ref[...], ts, raw=raw_ref[r_, :])
            sel16, sel8, gs64 = f["sel16"], f["sel8"], f["gs64"]
            lane = lax.broadcasted_iota(jnp.int32, (ts, LANES), 1)
            low = lane < 64
            zero = jnp.zeros((ts, LANES), F32)
            pieces = []

            dgq = jnp.zeros((1, LANES), F32)
            for s in range(4):
                _, xh, r = f["qa"][s]
                d = dqa_ref[r_, LANES * s:LANES * (s + 1)] * QA_SCALE
                dx, dg = _col_bwd(d, xh, r, gs64, 64.0, gq, ca, sa, 16, sel16)
                pieces.append(dx)
                dgq = dgq + dg
            dgk = jnp.zeros((1, LANES), F32)
            for s in range(2):
                _, xh, r = f["ka"][s]
                d = dka_v[:, LANES * s:LANES * (s + 1)]
                d = d + pltpu.roll(d, 64, 1)
                dx, dg = _col_bwd(d, xh, r, _row_sum, 128.0, gk, ca, sa, 16, sel16)
                pieces.append(jnp.where(low, dx, zero))
                dgk = dgk + dg
            for s in range(2):
                d = dva_v[:, LANES * s:LANES * (s + 1)]
                d = d + pltpu.roll(d, 64, 1)
                pieces.append(jnp.where(low, d, zero))
            pieces.append(dga_ref[r_, :])

            dgqb = jnp.zeros((1, LANES), F32)
            dq_cols = []
            for h in range(B_HEADS):
                _, xh, r = f["qb"][h]
                dx, dg = _col_bwd(dqb_ref[r_, LANES * h:LANES * (h + 1)] * QB_SCALE, xh, r, _row_sum,
                                  float(B_QK_DIM), gqb, cb, sb, 8, sel8)
                dq_cols.append(dx)
                dgqb = dgqb + dg
            dqr_b = jnp.concatenate(dq_cols, axis=1).astype(BF16)
            dwuq_ref[...] += _dot_tn(f["cqb"], dqr_b)
            dcq_raw, dgcq = _rms_bwd(_dot_nt(dqr_b, wuq), f["cqh"], f["rcq"], gcq)
            pieces.append(dcq_raw)

            dgkb = jnp.zeros((1, LANES), F32)
            dk_cols = []
            dkr = zero
            for h in range(B_HEADS):
                _, xh, r = f["kb"][h]
                dx, dg = _col_bwd(dkb_v[:, LANES * h:LANES * (h + 1)], xh, r, _row_sum, float(B_QK_DIM),
                                  gkb, cb, sb, 8, sel8)
                dk_cols.append(dx)
                dkr = dkr + dx
                dgkb = dgkb + dg
            dkr_b = jnp.concatenate(dk_cols, axis=1).astype(BF16)
            dvb_b = dvb_v.astype(BF16)
            dwuk_ref[...] += _dot_tn(f["ckvb"], dkr_b)
            dwuv_ref[...] += _dot_tn(f["ckvb"], dvb_b)
            dckv = _dot_nt(dkr_b, wuk) + _dot_nt(dvb_b, wuv)
            dckv_raw, dgckv = _rms_bwd(dckv, f["ckvh"], f["rckv"], gckv)
            pieces.append(dckv_raw)
            pieces.append(jnp.where((lane >= B_NOPE_DIM) & (lane < B_QK_DIM), dkr, zero))
            pieces += [zero, zero, zero]
            pieces.append(dgb_ref[r_, :])

            dproj_b = jnp.concatenate(pieces, axis=1).astype(BF16)
            dproj_ref[r_, :] = dproj_b
            dxn = _dot_nt(dproj_b, w)
            dx, dgin = _rms_bwd(dxn, f["xh0"], f["r0"], gin)
            dx_ref[r_, :] = dx + dh_ref[r_, :]

            for c in range(D_MODEL // LANES):
                small_ref[c:c + 1, :] += dgin[:, LANES * c:LANES * (c + 1)]
            small_ref[8:9, :] += dgq
            small_ref[9:10, :] += dgk
            for c in range(3):
                small_ref[10 + c:11 + c, :] += dgcq[:, LANES * c:LANES * (c + 1)]
            for c in range(2):
                small_ref[13 + c:14 + c, :] += dgckv[:, LANES * c:LANES * (c + 1)]
            small_ref[15:16, :] += dgqb
            small_ref[16:17, :] += dgkb

    def rows(width):
        return pl.BlockSpec((tm, width), lambda i: (i, 0))

    def cols(height):
        return pl.BlockSpec((height, tm), lambda i: (0, i))

    return pl.pallas_call(
        body, name="pre_bwd", grid=(s_len // tm,),
        in_specs=[rows(D_MODEL), rows(R_WIDTH), rows(D_MODEL), rows(512), cols(256), cols(256), rows(512), rows(512),
                  cols(512), cols(512), rows(512)] + [_full(t.shape) for t in tabs] + _pre_const_specs(consts),
        out_specs=[rows(D_MODEL), rows(N_WIDE), _full((B_Q_RANK, 512)), _full((B_KV_RANK, 512)),
                   _full((B_KV_RANK, 512)), _full((R_SMALL, LANES))],
        out_shape=[jax.ShapeDtypeStruct((s_len, D_MODEL), F32), jax.ShapeDtypeStruct((s_len, N_WIDE), BF16),
                   jax.ShapeDtypeStruct((B_Q_RANK, 512), F32),
                   jax.ShapeDtypeStruct((B_KV_RANK, 512), F32), jax.ShapeDtypeStruct((B_KV_RANK, 512), F32),
                   jax.ShapeDtypeStruct((R_SMALL, LANES), F32)],
        compiler_params=_params(("arbitrary",)),
    )(x, raw, dh, dqa, dka, dva, dga, dqb, dkb, dvb, dgb, *tabs, *[consts[n] for n in _PRE_IN_NAMES])


def _dw_in_call(xnb, dproj_b, tt, tn):
    s_len = xnb.shape[0]

    def body(a_ref, b_ref, o_ref):
        @pl.when(pl.program_id(1) == 0)
        def _():
            o_ref[...] = jnp.zeros(o_ref.shape, F32)

        o_ref[...] += _dot_tn(a_ref[...], b_ref[...])

    return pl.pallas_call(
        body, name="dw_in", grid=(N_WIDE // tn, s_len // tt),
        in_specs=[pl.BlockSpec((tt, D_MODEL), lambda n, t: (t, 0)), pl.BlockSpec((tt, tn), lambda n, t: (t, n))],
        out_specs=pl.BlockSpec((D_MODEL, tn), lambda n, t: (0, n)),
        out_shape=jax.ShapeDtypeStruct((D_MODEL, N_WIDE), F32),
        compiler_params=_params(("arbitrary", "arbitrary")),
    )(xnb, dproj_b)


def _mesh_pos():
    return lax.axis_index("x"), lax.axis_index("y"), lax.axis_index("c")


def _flip(v, bit):
    return 1 - v if bit else v


def _peer(pos, k):
    x, y, c = pos
    return _flip(x, (k >> 2) & 1), _flip(y, (k >> 1) & 1), _flip(c, k & 1)


def _logical(p):
    return 4 * p[0] + 2 * p[1] + p[2]


def _adamw(w, g, m, v):
    m = ADAM_B1 * m + (1.0 - ADAM_B1) * g
    v = ADAM_B2 * v + (1.0 - ADAM_B2) * (g * g)
    m_hat = m / (1.0 - ADAM_B1 ** ADAM_STEP)
    v_hat = v / (1.0 - ADAM_B2 ** ADAM_STEP)
    delta = -ADAM_LR * (m_hat / (jnp.sqrt(v_hat) + ADAM_EPS) + ADAM_WD * w)
    return delta, m, v


W_BLOCKS = ((D_MODEL, N_IN // N_DEV), (B_Q_RANK // N_DEV, 384), (B_KV_RANK, 768 // N_DEV), (D_MODEL // N_DEV, D_MODEL))
N_W = len(W_BLOCKS)


def _gather_blocks_call(blocks):
    def body(*refs):
        x_refs, out_refs, xb_refs = refs[0:N_W], refs[N_W:2 * N_W], refs[2 * N_W:3 * N_W]
        send_sems, recv_sems, local_sems = refs[3 * N_W:]
        x, y, c = _mesh_pos()
        me, sibling = (x, y, c), (x, y, 1 - c)
        chips = [(1 - x, y), (x, 1 - y), (1 - x, 1 - y)]
        for w in range(N_W):
            xb_refs[w][...] = x_refs[w][...].astype(BF16)

        def slot(w, p):
            return out_refs[w].at[_logical(p)]

        def copy(w, k, block, to, src=None):
            return pltpu.make_async_remote_copy(
                src_ref=slot(w, block) if src is None else src, dst_ref=slot(w, block),
                send_sem=send_sems.at[N_W * k + w], recv_sem=recv_sems.at[N_W * k + w],
                device_id=to, device_id_type=pl.DeviceIdType.MESH)

        mine = [pltpu.make_async_copy(xb_refs[w], slot(w, me), local_sems.at[w]) for w in range(N_W)]
        for cp in mine:
            cp.start()
        first = [copy(w, 0, me, sibling, src=xb_refs[w]) for w in range(N_W)]
        first += [copy(w, 1 + j, me, (*chip, c), src=xb_refs[w]) for j, chip in enumerate(chips) for w in range(N_W)]
        for cp in first:
            cp.start()
        passed = []
        for j, chip in enumerate(chips):
            for w in range(N_W):
                copy(w, 1 + j, (*chip, c), me).wait_recv()
                fwd = copy(w, 4 + j, (*chip, c), sibling)
                fwd.start()
                passed.append(fwd)
        for w in range(N_W):
            copy(w, 0, sibling, me).wait_recv()
        for j, chip in enumerate(chips):
            for w in range(N_W):
                copy(w, 4 + j, (*chip, 1 - c), me).wait_recv()
        for cp in first + passed:
            cp.wait_send()
        for cp in mine:
            cp.wait()

    vm = pl.BlockSpec(memory_space=pltpu.VMEM)
    return pl.pallas_call(
        body, name="gather_weights",
        out_shape=[jax.ShapeDtypeStruct((N_DEV,) + shp, BF16) for shp in W_BLOCKS],
        in_specs=[vm] * N_W, out_specs=[vm] * N_W,
        scratch_shapes=[pltpu.VMEM(shp, BF16) for shp in W_BLOCKS]
        + [pltpu.SemaphoreType.DMA((7 * N_W,)), pltpu.SemaphoreType.DMA((7 * N_W,)), pltpu.SemaphoreType.DMA((N_W,))],
        compiler_params=pltpu.CompilerParams(vmem_limit_bytes=VMEM_LIMIT),
    )(*blocks)


def _reduce_two_level_call(parts, small, w_blk, m_blk, v_blk, w_s, m_s, v_s):
    chunks = (32, 48, 64, 16)
    n_chip = N_DEV // 2

    def body(*refs):
        p_refs = refs[0:4]
        small_ref = refs[4]
        w_refs, m_refs, v_refs = refs[5:9], refs[9:13], refs[13:17]
        ws_ref, ms_ref, vs_ref = refs[17:20]
        g_refs, d_refs, nm_refs, nv_refs = refs[20:24], refs[24:28], refs[28:32], refs[32:36]
        gs_ref, ds_ref, nms_ref, nvs_ref = refs[36:40]
        ra_refs, rb_refs, st_refs = refs[40:44], refs[44:48], refs[48:52]
        recv_s_ref = refs[52]
        send_a, recv_a, send_b, recv_b, send_s_sems, recv_s_sems = refs[53:59]
        pos = _mesh_pos()
        x, y, c = pos
        me = _logical(pos)
        sibling = (x, y, 1 - c)

        def chip(j):
            return _flip(x, j & 1), _flip(y, (j >> 1) & 1)

        def to_sibling(w, j):
            return pltpu.make_async_remote_copy(
                src_ref=p_refs[w].at[_logical((*chip(j), 1 - c))], dst_ref=ra_refs[w].at[j],
                send_sem=send_a.at[N_W * j + w], recv_sem=recv_a.at[N_W * j + w],
                device_id=sibling, device_id_type=pl.DeviceIdType.MESH)

        def to_chip(w, j):
            return pltpu.make_async_remote_copy(
                src_ref=st_refs[w].at[j - 1], dst_ref=rb_refs[w].at[j - 1],
                send_sem=send_b.at[N_W * (j - 1) + w], recv_sem=recv_b.at[N_W * (j - 1) + w],
                device_id=(*chip(j), c), device_id_type=pl.DeviceIdType.MESH)

        def tiny(k):
            return pltpu.make_async_remote_copy(
                src_ref=small_ref, dst_ref=recv_s_ref.at[k],
                send_sem=send_s_sems.at[k], recv_sem=recv_s_sems.at[k],
                device_id=_peer(pos, k), device_id_type=pl.DeviceIdType.MESH)

        order = (0, 3, 2, 1)
        for j in (1, 2, 3, 0):
            for w in order:
                to_sibling(w, j).start()
        for k in range(1, N_DEV):
            tiny(k).start()

        for j in (1, 2, 3):
            d = _logical((*chip(j), c))
            for w in order:
                to_sibling(w, j).wait_recv()
                chunk = chunks[w]

                def pair(t, carry, w=w, j=j, d=d, chunk=chunk):
                    rows = pl.ds(pl.multiple_of(t * chunk, chunk), chunk)
                    s = p_refs[w][d, rows, :].astype(F32) + ra_refs[w][j, rows, :].astype(F32)
                    st_refs[w][j - 1, rows, :] = s.astype(BF16)
                    return carry

                lax.fori_loop(0, W_BLOCKS[w][0] // chunk, pair, 0)
                to_chip(w, j).start()

        recv_s_ref[0] = small_ref[...]
        for k in range(1, N_DEV):
            tiny(k).wait_recv()
        acc = recv_s_ref[me]
        for a in range(1, N_DEV):
            acc = acc + recv_s_ref[lax.bitwise_xor(me, a)]
        row = lax.broadcasted_iota(jnp.int32, (R_SMALL, LANES), 0)
        gs = jnp.where(row == 8, acc + pltpu.roll(acc, 64, 1), acc)
        gs = jnp.where(row == ROW_LOSS, jnp.sum(acc, axis=1, keepdims=True) * (0.5 / D_MODEL), gs)
        gs_ref[...] = gs
        ds, nms, nvs = _adamw(ws_ref[...], gs, ms_ref[...], vs_ref[...])
        ds_ref[...] = ds
        nms_ref[...] = nms
        nvs_ref[...] = nvs

        for w in (1, 2, 3, 0):
            to_sibling(w, 0).wait_recv()
            for j in (1, 2, 3):
                to_chip(w, j).wait_recv()
            chunk = chunks[w]

            def step(t, carry, w=w, chunk=chunk):
                rows = pl.ds(pl.multiple_of(t * chunk, chunk), chunk)
                g = p_refs[w][me, rows, :].astype(F32) + ra_refs[w][0, rows, :].astype(F32)
                for j in range(n_chip - 1):
                    g = g + rb_refs[w][j, rows, :].astype(F32)
                d, nm, nv = _adamw(w_refs[w][rows, :], g, m_refs[w][rows, :], v_refs[w][rows, :])
                g_refs[w][rows, :] = g
                d_refs[w][rows, :] = d
                nm_refs[w][rows, :] = nm
                nv_refs[w][rows, :] = nv
                return carry

            lax.fori_loop(0, W_BLOCKS[w][0] // chunk, step, 0)
        for k in range(1, N_DEV):
            tiny(k).wait_send()
        for w in range(N_W):
            for j in range(n_chip):
                to_sibling(w, j).wait_send()
            for j in (1, 2, 3):
                to_chip(w, j).wait_send()

    vm = pl.BlockSpec(memory_space=pltpu.VMEM)
    blk = [jax.ShapeDtypeStruct(shp, F32) for shp in W_BLOCKS]
    small_shape = jax.ShapeDtypeStruct((R_SMALL, LANES), F32)
    return pl.pallas_call(
        body, name="reduce_adamw",
        in_specs=[vm] * 20, out_specs=[vm] * 20,
        out_shape=blk * 4 + [small_shape] * 4,
        scratch_shapes=[pltpu.VMEM((n_chip,) + shp, BF16) for shp in W_BLOCKS]
        + [pltpu.VMEM((n_chip - 1,) + shp, BF16) for shp in W_BLOCKS] * 2
        + [pltpu.VMEM((N_DEV, R_SMALL, LANES), F32),
           pltpu.SemaphoreType.DMA((n_chip * N_W,)), pltpu.SemaphoreType.DMA((n_chip * N_W,)),
           pltpu.SemaphoreType.DMA(((n_chip - 1) * N_W,)), pltpu.SemaphoreType.DMA(((n_chip - 1) * N_W,)),
           pltpu.SemaphoreType.DMA((N_DEV,)), pltpu.SemaphoreType.DMA((N_DEV,))],
        compiler_params=pltpu.CompilerParams(vmem_limit_bytes=VMEM_LIMIT),
    )(*parts, small, *w_blk, *m_blk, *v_blk, w_s, m_s, v_s)


def _pack_small(norm_in, a_q, a_k, b_cq, b_ckv, b_q, b_k):
    def row(v):
        return jnp.pad(v.reshape(1, -1), ((0, 0), (0, LANES - v.size)))
    rows = [norm_in.reshape(8, LANES), row(a_q), row(a_k), b_cq.reshape(3, LANES), b_ckv.reshape(2, LANES),
            row(b_q), row(b_k), jnp.zeros((R_SMALL - 17, LANES), F32)]
    return jnp.concatenate(rows, axis=0)


def _unpack_small(s):
    return (s[0:8].reshape(1, D_MODEL), s[8:9, :64], s[9:10, :64], s[10:13].reshape(1, B_Q_RANK),
            s[13:15].reshape(1, B_KV_RANK), s[15:16, :B_QK_DIM], s[16:17, :B_QK_DIM])


C_IN = N_IN // N_DEV
_RUNS = ((0, 512, O_QA), (512, 576, O_KA), (576, 640, O_KA + 128), (640, 704, O_VA), (704, 768, O_VA + 128),
         (768, 1280, O_GA), (1280, 1664, O_CQ), (1664, 1920, O_CKV), (1920, 1952, O_KR + 64), (1952, 2464, O_GB))


def _in_cols(g_in, lo, hi):
    out = []
    for d in range(N_DEV):
        a, b = max(lo, C_IN * d), min(hi, C_IN * (d + 1))
        if a < b:
            out.append(g_in[d][:, a - C_IN * d:b - C_IN * d])
    return out


def _widen_weights(g_in, g_uq, g_ukv, g_out):
    z64 = jnp.zeros((D_MODEL, 64), BF16)
    z32 = jnp.zeros((D_MODEL, 32), BF16)
    k0, k1 = _in_cols(g_in, 512, 576), _in_cols(g_in, 576, 640)
    v0, v1 = _in_cols(g_in, 640, 704), _in_cols(g_in, 704, 768)
    kr_blk = [z64] + _in_cols(g_in, 1920, 1952) + [z32]
    w_wide = jnp.concatenate(
        _in_cols(g_in, 0, 512) + k0 + k0 + k1 + k1 + v0 + v0 + v1 + v1 + _in_cols(g_in, 768, 1920)
        + kr_blk * B_HEADS + _in_cols(g_in, 1952, 2464), axis=1)
    w_uq = g_uq.reshape(B_Q_RANK, 384)
    wuq = jnp.pad(w_uq.reshape(B_Q_RANK, B_HEADS, B_QK_DIM), ((0, 0), (0, 0), (0, LANES - B_QK_DIM)))
    wuq = wuq.reshape(B_Q_RANK, 512)
    ukv = g_ukv.transpose(1, 0, 2).reshape(B_KV_RANK, B_HEADS, B_NOPE_DIM + B_V_DIM)
    wuk = jnp.pad(ukv[:, :, :B_NOPE_DIM], ((0, 0), (0, 0), (0, LANES - B_NOPE_DIM))).reshape(B_KV_RANK, 512)
    wuv = ukv[:, :, B_NOPE_DIM:].reshape(B_KV_RANK, 512)
    return w_wide, wuq, wuk, wuv, g_out.reshape(D_MODEL, D_MODEL)


def _grad_blocks(dw_wide, dwuq, dwuk, dwuv, dw_out):
    blocks = []
    for d in range(N_DEV):
        pieces = []
        for lo, hi, wide in _RUNS:
            a, b = max(lo, C_IN * d), min(hi, C_IN * (d + 1))
            if a < b:
                pieces.append(dw_wide[:, wide + a - lo:wide + b - lo])
        blocks.append(jnp.concatenate(pieces, axis=1))
    p_in = jnp.stack(blocks).astype(BF16)
    dw_uq = dwuq.reshape(B_Q_RANK, B_HEADS, LANES)[:, :, :B_QK_DIM].reshape(N_DEV, B_Q_RANK // N_DEV, 384)
    dk = dwuk.reshape(B_KV_RANK, B_HEADS, LANES)[:, :, :B_NOPE_DIM]
    dv = dwuv.reshape(B_KV_RANK, B_HEADS, B_V_DIM)
    dw_ukv = jnp.concatenate([dk, dv], axis=2).reshape(B_KV_RANK, N_DEV, 768 // N_DEV).transpose(1, 0, 2)
    return (p_in, dw_uq.astype(BF16), dw_ukv.astype(BF16),
            dw_out.reshape(N_DEV, D_MODEL // N_DEV, D_MODEL).astype(BF16))


def _rope_tables(s_len):
    row = jnp.arange(s_len // GRID_W, dtype=F32)
    col = jnp.arange(GRID_W, dtype=F32)

    def parts(dim):
        half = dim // 2
        inv = 1.0 / (ROPE_THETA ** (jnp.arange(0, half, 2, dtype=F32) / half))
        ar, ac = row[:, None] * inv[None, :], col[:, None] * inv[None, :]
        zr, zc = jnp.zeros_like(ar), jnp.zeros_like(ac)
        cos_c = jnp.concatenate([zc, zc, jnp.cos(ac), jnp.cos(ac)], axis=1)
        cos_r = jnp.concatenate([jnp.cos(ar), jnp.cos(ar), zr, zr], axis=1)
        sin_c = jnp.concatenate([zc, zc, -jnp.sin(ac), jnp.sin(ac)], axis=1)
        sin_r = jnp.concatenate([-jnp.sin(ar), jnp.sin(ar), zr, zr], axis=1)
        return cos_c, cos_r, sin_c, sin_r

    tabs = [jnp.tile(t, (1, 2)) for t in parts(A_HEAD_DIM)]
    for n, t in enumerate(parts(B_ROPE_DIM)):
        lead = jnp.full((t.shape[0], B_NOPE_DIM), 1.0 if n == 0 else 0.0, F32)
        tail = jnp.full((t.shape[0], LANES - B_QK_DIM), 1.0 if n == 0 else 0.0, F32)
        tabs.append(jnp.concatenate([lead, t, tail], axis=1))
    return tuple(tabs)


def kernel(x, norm_in, w_in, a_q_norm, a_k_norm, b_cq_norm, b_ckv_norm, w_uq, w_ukv, b_q_norm, b_k_norm, w_out, loss_target, m_norm_in, m_w_in, m_a_q_norm, m_a_k_norm, m_b_cq_norm, m_b_ckv_norm, m_w_uq, m_w_ukv, m_b_q_norm, m_b_k_norm, m_w_out, v_norm_in, v_w_in, v_a_q_norm, v_a_k_norm, v_b_cq_norm, v_b_ckv_norm, v_w_uq, v_w_ukv, v_b_q_norm, v_b_k_norm, v_w_out):
    s_len = x.shape[1]
    tm = min(256, s_len)
    tq, tk = min(512, s_len), min(2048, s_len)
    ftq, ftk, fbufs = min(256, s_len), min(1024, s_len), 2
    x2 = x.reshape(s_len, D_MODEL)
    t2 = loss_target.reshape(s_len, D_MODEL)

    w_blk = (w_in[0], w_uq[0], w_ukv[0], w_out[0])
    w_wide, wuq, wuk, wuv, wout = _widen_weights(*_gather_blocks_call(w_blk))

    def dup(v, pad_to=None):
        v = v.reshape(1, -1)
        if pad_to is None:
            return jnp.concatenate([v, v], axis=1)
        return jnp.pad(v, ((0, 0), (0, pad_to - v.shape[1])))

    g64 = jnp.asarray(np.kron(np.eye(2), np.ones((64, 64))), dtype=BF16)
    consts = dict(gin=norm_in, w=w_wide, wuq=wuq, wuk=wuk, wuv=wuv, gq=dup(a_q_norm), gk=dup(a_k_norm),
                  gcq=b_cq_norm, gckv=b_ckv_norm, gqb=dup(b_q_norm, LANES), gkb=dup(b_k_norm, LANES), g64=g64)
    tabs = _rope_tables(s_len)

    qa, ka, va, va_t, ga, qb, kb, vb, vb_t, gb, raw, xnb = _pre_fwd_call(x2, consts, tabs, min(512, s_len))
    oa, lse_a_t = _attn_fwd_t_call(qa, ka, va_t, groups=A_KV_HEADS, sub=4, masked=True, scale=None, tq=ftq, tk=ftk,
                                   name="attn_fwd_a", s_bufs=fbufs)
    ob, lse_b_t = _attn_fwd_t_call(qb, kb, vb_t, groups=1, sub=B_HEADS, masked=False, scale=None, tq=ftq,
                                   tk=ftk, name="attn_fwd_b", s_bufs=fbufs)
    dh, doa, dob, dga, dgb, dw_out, loss_row = _out_call(x2, t2, oa, ob, ga, gb, wout, min(512, s_len))

    dqa, dka_t, dva_t = _attn_bwd_q_call(qa, ka, va, doa, oa, lse_a_t, groups=A_KV_HEADS, sub=4, masked=True,
                                         tq=tq, tk=tk, ck=min(512, s_len), name="attn_bwd_a")
    dqb, dkb_t, dvb_t = _attn_bwd_q_call(qb, kb, vb, dob, ob, lse_b_t, groups=2, sub=2, masked=False,
                                         tq=tq, tk=min(4096, s_len), ck=min(256, s_len), name="attn_bwd_b")
    dx, dproj_b, dwuq, dwuk, dwuv, small = _pre_bwd_call(
        x2, raw, dh, dqa, dka_t, dva_t, dga, dqb, dkb_t, dvb_t, dgb, consts, tabs, tm)
    dw_wide = _dw_in_call(xnb, dproj_b, min(1024, s_len), N_WIDE)

    parts = _grad_blocks(dw_wide, dwuq, dwuk, dwuv, dw_out)
    small = jnp.concatenate([small[:ROW_LOSS], loss_row, small[ROW_LOSS + 1:]], axis=0)

    m_blk = (m_w_in[0], m_w_uq[0], m_w_ukv[0], m_w_out[0])
    v_blk = (v_w_in[0], v_w_uq[0], v_w_ukv[0], v_w_out[0])
    w_s = _pack_small(norm_in, a_q_norm, a_k_norm, b_cq_norm, b_ckv_norm, b_q_norm, b_k_norm)
    m_s = _pack_small(m_norm_in, m_a_q_norm, m_a_k_norm, m_b_cq_norm, m_b_ckv_norm, m_b_q_norm, m_b_k_norm)
    v_s = _pack_small(v_norm_in, v_a_q_norm, v_a_k_norm, v_b_cq_norm, v_b_ckv_norm, v_b_q_norm, v_b_k_norm)
    res = _reduce_two_level_call(parts, small, w_blk, m_blk, v_blk, w_s, m_s, v_s)

    def leaves(blocks, sm):
        wi, uq, ukv, wo = [b[None] for b in blocks]
        n_in, aq, ak, bcq, bckv, bq, bk = _unpack_small(sm)
        return [n_in, wi, aq, ak, bcq, bckv, uq, ukv, bq, bk, wo]

    g_s = res[16]
    loss = g_s[ROW_LOSS, 0]
    grad_x = dx.reshape(1, s_len, D_MODEL)
    return (loss, grad_x, *leaves(res[0:4], res[16]), *leaves(res[4:8], res[17]), *leaves(res[8:12], res[18]),
            *leaves(res[12:16], res[19]))
```

```python
import functools

import numpy as np
import jax
import jax.numpy as jnp
from jax import lax
from jax.experimental import pallas as pl
from jax.experimental.pallas import tpu as pltpu

F32 = jnp.float32
BF16 = jnp.bfloat16

D_MODEL = 1024
GRID_W = 64
ROPE_THETA = 10000.0
EPS = 1e-6
A_HEAD_DIM = 64
A_HEADS = 8
A_KV_HEADS = 2
B_HEADS = 4
B_NOPE_DIM = 64
B_ROPE_DIM = 32
B_QK_DIM = 96
B_V_DIM = 128
B_Q_RANK = 384
B_KV_RANK = 256
N_IN = 2464
N_DEV = 8

ADAM_LR = 0.001
ADAM_B1 = 0.9
ADAM_B2 = 0.999
ADAM_EPS = 1e-08
ADAM_WD = 0.01
ADAM_STEP = 10

QA_SCALE = 0.125
QB_SCALE = 1.0 / float(np.sqrt(B_QK_DIM))

LANES = 128
O_QA, O_KA, O_VA, O_GA, O_CQ, O_CKV, O_KR, O_GB, N_WIDE = 0, 512, 768, 1024, 1536, 1920, 2176, 2688, 3200
R_QA, R_KA, R_CQ, R_CKV, R_KR, R_WIDTH = 0, 512, 768, 1152, 1408, 1920

R_SMALL = 24
ROW_LOSS = 17

VMEM_LIMIT = 56 * 1024 * 1024

NT = (((1,), (1,)), ((), ()))
TN = (((0,), (0,)), ((), ()))


def _dot(a, b):
    return jnp.dot(a, b, preferred_element_type=F32)


def _dot_nt(a, b):
    return lax.dot_general(a, b, NT, preferred_element_type=F32)


def _dot_tn(a, b):
    return lax.dot_general(a, b, TN, preferred_element_type=F32)


def _params(sem=None):
    return pltpu.CompilerParams(dimension_semantics=sem, vmem_limit_bytes=VMEM_LIMIT)


def _full(shape):
    nd = len(shape)
    return pl.BlockSpec(shape, lambda *_: (0,) * nd)


def _swap_sel(rows, shift):
    lane = lax.broadcasted_iota(jnp.int32, (rows, LANES), 1)
    return pltpu.roll(lane, shift, 1) == (lane ^ shift)


def _swap(x, shift, sel):
    return jnp.where(sel, pltpu.roll(x, shift, 1), pltpu.roll(x, LANES - shift, 1))


def _group_sum64(x, g64):
    hi = x.astype(BF16)
    lo = (x - hi.astype(F32)).astype(BF16)
    return _dot(hi, g64) + _dot(lo, g64)


def _row_sum(x):
    return jnp.sum(x, axis=-1, keepdims=True)


def _col_fwd(xs, msum, denom, gain, cos, sin, shift, sel):
    r = lax.rsqrt(msum(xs * xs) * (1.0 / denom) + EPS)
    xh = xs * r
    n = xh * gain
    return n * cos + _swap(n, shift, sel) * sin, xh, r


def _col_bwd(d_out, xh, r, msum, denom, gain, cos, sin, shift, sel):
    dn = d_out * cos + _swap(d_out * sin, shift, sel)
    dgain = jnp.sum(dn * xh, axis=0, keepdims=True)
    dxh = dn * gain
    dx = r * (dxh - xh * (msum(dxh * xh) * (1.0 / denom)))
    return dx, dgain


def _rms_fwd(x, gain):
    r = lax.rsqrt(jnp.mean(x * x, axis=-1, keepdims=True) + EPS)
    xh = x * r
    return xh * gain, xh, r


def _rms_bwd(dy, xh, r, gain):
    dgain = jnp.sum(dy * xh, axis=0, keepdims=True)
    dxh = dy * gain
    dx = r * (dxh - xh * jnp.mean(dxh * xh, axis=-1, keepdims=True))
    return dx, dgain


def _pre_forward(x, gin, w, wuq, wuk, wuv, gq, gk, gcq, gckv, gqb, gkb, ca, sa, cb, sb, g64, tm, raw=None):
    sel16 = _swap_sel(tm, 16)
    sel8 = _swap_sel(tm, 8)
    xn, xh0, r0 = _rms_fwd(x, gin)
    xnb = xn.astype(BF16)
    proj = None
    if raw is None:
        proj = _dot(xnb, w)
        raw = jnp.concatenate([proj[:, O_QA:O_QA + 512], proj[:, O_KA:O_KA + 256], proj[:, O_CQ:O_CQ + B_Q_RANK],
                               proj[:, O_CKV:O_CKV + B_KV_RANK], proj[:, O_KR:O_KR + 512]], axis=1)
    gs64 = functools.partial(_group_sum64, g64=g64)
    qa = [_col_fwd(raw[:, R_QA + LANES * s:R_QA + LANES * (s + 1)], gs64, 64.0, gq, ca, sa, 16, sel16)
          for s in range(4)]
    ka = [_col_fwd(raw[:, R_KA + LANES * s:R_KA + LANES * (s + 1)], _row_sum, 128.0, gk, ca, sa, 16, sel16)
          for s in range(2)]
    cq, cqh, rcq = _rms_fwd(raw[:, R_CQ:R_CQ + B_Q_RANK], gcq)
    cqb = cq.astype(BF16)
    qb_raw = _dot(cqb, wuq)
    qb = [_col_fwd(qb_raw[:, LANES * h:LANES * (h + 1)], _row_sum, float(B_QK_DIM), gqb, cb, sb, 8, sel8)
          for h in range(B_HEADS)]
    ckv, ckvh, rckv = _rms_fwd(raw[:, R_CKV:R_CKV + B_KV_RANK], gckv)
    ckvb = ckv.astype(BF16)
    kb_raw = _dot(ckvb, wuk) + raw[:, R_KR:R_KR + 512]
    vb = _dot(ckvb, wuv)
    kb = [_col_fwd(kb_raw[:, LANES * h:LANES * (h + 1)], _row_sum, float(B_QK_DIM), gkb, cb, sb, 8, sel8)
          for h in range(B_HEADS)]
    return dict(xh0=xh0, r0=r0, xnb=xnb, proj=proj, raw=raw, qa=qa, ka=ka, cqh=cqh, rcq=rcq, cqb=cqb, qb=qb,
                ckvh=ckvh, rckv=rckv, ckvb=ckvb, kb=kb, vb=vb, sel16=sel16, sel8=sel8, gs64=gs64)


def _rope_tiles(tab_refs, i, tm):
    per_tile = tm // GRID_W
    out = []
    for t in range(4):
        col_ref, row_ref = tab_refs[2 * t], tab_refs[2 * t + 1]
        col = col_ref[...]
        out.append(jnp.concatenate([col + row_ref[pl.ds(i * per_tile + b, 1), :] for b in range(per_tile)], axis=0))
    return out


_PRE_IN_NAMES = ("gin", "w", "wuq", "wuk", "wuv", "gq", "gk", "gcq", "gckv", "gqb", "gkb", "g64")


def _pre_const_specs(consts):
    return [_full(consts[n].shape) for n in _PRE_IN_NAMES]


def _pre_fwd_call(x, consts, tabs, tm):
    s_len = x.shape[0]
    ts = min(256, tm)

    def body(x_ref, *refs):
        tab_refs, refs = refs[:8], refs[8:]
        (gin_ref, w_ref, wuq_ref, wuk_ref, wuv_ref, gq_ref, gk_ref, gcq_ref, gckv_ref, gqb_ref, gkb_ref, g64_ref,
         qa_ref, ka_ref, va_ref, vat_ref, ga_ref, qb_ref, kb_ref, vb_ref, vbt_ref, gb_ref, raw_ref, xnb_ref) = refs
        for part in range(tm // ts):
            r = slice(ts * part, ts * (part + 1))
            ca, sa, cb, sb = _rope_tiles(tab_refs, pl.program_id(0) * (tm // ts) + part, ts)
            f = _pre_forward(x_ref[r, :], gin_ref[...], w_ref[...], wuq_ref[...], wuk_ref[...], wuv_ref[...],
                             gq_ref[...], gk_ref[...], gcq_ref[...], gckv_ref[...], gqb_ref[...], gkb_ref[...],
                             ca, sa, cb, sb, g64_ref[...], ts)
            proj = f["proj"]
            raw_ref[r, :] = f["raw"]
            xnb_ref[r, :] = f["xnb"]
            for s in range(4):
                qa_ref[r, LANES * s:LANES * (s + 1)] = (f["qa"][s][0] * QA_SCALE).astype(BF16)
            for s in range(2):
                ka_ref[r, LANES * s:LANES * (s + 1)] = f["ka"][s][0].astype(BF16)
            va = proj[:, O_VA:O_VA + 256]
            va_ref[r, :] = va.astype(BF16)
            vat_ref[:, r] = va.T.astype(BF16)
            ga_ref[r, :] = proj[:, O_GA:O_GA + 512]
            for h in range(B_HEADS):
                qb_ref[r, LANES * h:LANES * (h + 1)] = (f["qb"][h][0] * QB_SCALE).astype(BF16)
                kb_ref[r, LANES * h:LANES * (h + 1)] = f["kb"][h][0].astype(BF16)
            vb_ref[r, :] = f["vb"].astype(BF16)
            vbt_ref[:, r] = f["vb"].T.astype(BF16)
            gb_ref[r, :] = proj[:, O_GB:O_GB + 512]

    def rows(width):
        return pl.BlockSpec((tm, width), lambda i: (i, 0))

    def cols(height):
        return pl.BlockSpec((height, tm), lambda i: (0, i))

    outs = [((s_len, 512), BF16, rows(512)), ((s_len, 256), BF16, rows(256)), ((s_len, 256), BF16, rows(256)),
            ((256, s_len), BF16, cols(256)), ((s_len, 512), F32, rows(512)), ((s_len, 512), BF16, rows(512)),
            ((s_len, 512), BF16, rows(512)), ((s_len, 512), BF16, rows(512)), ((512, s_len), BF16, cols(512)),
            ((s_len, 512), F32, rows(512)), ((s_len, R_WIDTH), F32, rows(R_WIDTH)),
            ((s_len, D_MODEL), BF16, rows(D_MODEL))]
    return pl.pallas_call(
        body, name="pre_fwd", grid=(s_len // tm,),
        in_specs=[rows(D_MODEL)] + [_full(t.shape) for t in tabs] + _pre_const_specs(consts),
        out_specs=[sp for _, _, sp in outs],
        out_shape=[jax.ShapeDtypeStruct(sh, dt) for sh, dt, _ in outs],
        compiler_params=_params(("arbitrary",)),
    )(x, *tabs, *[consts[n] for n in _PRE_IN_NAMES])


def _head_masks(rows):
    lane = lax.broadcasted_iota(jnp.int32, (rows, LANES), 1)
    return lane < 64, lane >= 64


def _row_fold(x, op):
    return op(x.reshape(x.shape[0] // 8, 8, x.shape[1]), axis=0)


def _attn_fwd_t_call(q, k, vt, *, groups, sub, masked, scale, tq, tk, name):
    s_len = q.shape[0]
    qw = LANES * (sub // 2 if masked else sub)
    kvw = LANES if masked else LANES * sub
    n_c = s_len // tk
    kv_mode = pl.Buffered(1) if groups == 1 else None

    def body(q_ref, qn_ref, k_ref, vt_ref, o_ref, lse_ref, s_sc, mx_sc):
        keep = _head_masks(tq) if masked else None

        def kv_of(hh):
            return slice(0, LANES) if masked else slice(LANES * hh, LANES * (hh + 1))

        def q_of(ref, hh):
            if not masked:
                return ref[:, LANES * hh:LANES * (hh + 1)]
            qp = ref[:, LANES * (hh // 2):LANES * (hh // 2 + 1)]
            return jnp.where(keep[hh % 2], qp, jnp.zeros_like(qp))

        def scores(hh, qm, c, mx):
            s_t = _dot_nt(k_ref[tk * c:tk * (c + 1), kv_of(hh)], qm)
            if scale is not None:
                s_t = s_t * scale
            s_sc[hh % 2, c] = s_t
            return jnp.maximum(mx, _row_fold(s_t, jnp.max))

        neg = jnp.full((8, tq), -jnp.inf, F32)

        @pl.when(pl.program_id(1) == 0)
        def _():
            qm0 = q_of(q_ref, 0)
            mx0 = neg
            for c in range(n_c):
                mx0 = scores(0, qm0, c, mx0)
            mx_sc[...] = mx0

        mx_next = mx_sc[...]
        outs = []
        for hh in range(sub):
            m = jnp.max(mx_next, axis=0, keepdims=True)
            nxt = (hh + 1) % sub
            qm_next = q_of(q_ref if hh + 1 < sub else qn_ref, nxt)
            mx_next = neg
            lsum = jnp.zeros((8, tq), F32)
            acc = jnp.zeros((LANES, tq), F32)
            for c in range(n_c):
                p_t = jnp.exp(s_sc[hh % 2, c] - m)
                lsum = lsum + _row_fold(p_t, jnp.sum)
                mx_next = scores(nxt, qm_next, c, mx_next)
                acc = acc + _dot(vt_ref[kv_of(hh), tk * c:tk * (c + 1)], p_t.astype(BF16))
            l = jnp.sum(lsum, axis=0, keepdims=True)
            outs.append((acc / l).T)
            lse_ref[hh] = m + jnp.log(l)
        mx_sc[...] = mx_next
        if masked:
            for pr in range(sub // 2):
                o_ref[:, LANES * pr:LANES * (pr + 1)] = jnp.where(keep[0], outs[2 * pr], outs[2 * pr + 1])
        else:
            for hh in range(sub):
                o_ref[:, LANES * hh:LANES * (hh + 1)] = outs[hh]

    n_q = s_len // tq
    return pl.pallas_call(
        body, name=name, grid=(groups, n_q),
        in_specs=[pl.BlockSpec((tq, qw), lambda g, i: (i, g)),
                  pl.BlockSpec((tq, qw), lambda g, i: (jnp.minimum(i + 1, n_q - 1), g)),
                  pl.BlockSpec((s_len, kvw), lambda g, i: (0, g), pipeline_mode=kv_mode),
                  pl.BlockSpec((kvw, s_len), lambda g, i: (g, 0), pipeline_mode=kv_mode)],
        out_specs=[pl.BlockSpec((tq, qw), lambda g, i: (i, g)),
                   pl.BlockSpec((sub, 1, tq), lambda g, i: (g, 0, i))],
        out_shape=[jax.ShapeDtypeStruct((s_len, groups * qw), F32),
                   jax.ShapeDtypeStruct((groups * sub, 1, s_len), F32)],
        scratch_shapes=[pltpu.VMEM((2, n_c, tk, tq), F32), pltpu.VMEM((8, tq), F32)],
        compiler_params=_params(("arbitrary", "arbitrary")),
    )(q, q, k, vt)


def _attn_bwd_q_call(q, k, v, do, o, lse_t, *, groups, sub, masked, tq, tk, ck, name):
    s_len = q.shape[0]
    qw = LANES * (sub // 2 if masked else sub)
    kvw = LANES if masked else LANES * sub
    n_c = tk // ck

    def body(q_ref, k_ref, v_ref, do_ref, o_ref, lse_ref, dq_ref, dkt_ref, dvt_ref):
        j = pl.program_id(1)
        i = pl.program_id(2)

        @pl.when((j == 0) & (i == 0))
        def _():
            dq_ref[...] = jnp.zeros(dq_ref.shape, F32)

        @pl.when(i == 0)
        def _():
            dkt_ref[...] = jnp.zeros(dkt_ref.shape, F32)
            dvt_ref[...] = jnp.zeros(dvt_ref.shape, F32)

        lkeep = _head_masks(tq) if masked else None
        heads = []
        for hh in range(sub):
            if masked:
                cols = slice(LANES * (hh // 2), LANES * (hh // 2 + 1))
                kv = slice(0, LANES)
                qp, dop = q_ref[:, cols], do_ref[:, cols]
                qm = jnp.where(lkeep[hh % 2], qp, jnp.zeros_like(qp))
                dom = jnp.where(lkeep[hh % 2], dop, jnp.zeros_like(dop))
            else:
                cols = kv = slice(LANES * hh, LANES * (hh + 1))
                qm, dom = q_ref[:, cols], do_ref[:, cols]
            delta = jnp.sum(dom.astype(F32) * o_ref[:, cols], axis=1, keepdims=True)
            lse = jnp.broadcast_to(lse_ref[hh], (LANES, tq)).T[:, 0:1]
            heads.append((cols, kv, qm, dom, qm.T, dom.T, delta, lse))

        def products(hh, c):
            _, kv, qm, dom, _, _, _, _ = heads[hh]
            return _dot_nt(qm, k_ref[ck * c:ck * (c + 1), kv]), _dot_nt(dom, v_ref[ck * c:ck * (c + 1), kv])

        items = [(hh, c) for hh in range(sub) for c in range(n_c)]
        dq_acc = [jnp.zeros((tq, LANES), F32) for _ in range(sub)]
        nxt = products(*items[0])
        for n, (hh, c) in enumerate(items):
            s, dp = nxt
            if n + 1 < len(items):
                nxt = products(*items[n + 1])
            _, kv, qm, dom, qmt, domt, delta, lse = heads[hh]
            p = jnp.exp(s - lse)
            ds = p * (dp - delta)
            p_b = p.astype(BF16)
            ds_b = ds.astype(BF16)
            kcols = slice(ck * c, ck * (c + 1))
            dvt_ref[kv, kcols] += _dot(domt, p_b)
            dkt_ref[kv, kcols] += _dot(qmt, ds_b)
            dq_acc[hh] = dq_acc[hh] + _dot(ds_b, k_ref[kcols, kv])
        rows = pl.ds(pl.multiple_of(i * tq, tq), tq)
        if masked:
            for pr in range(sub // 2):
                dq_ref[rows, LANES * pr:LANES * (pr + 1)] += jnp.where(lkeep[0], dq_acc[2 * pr], dq_acc[2 * pr + 1])
        else:
            for hh in range(sub):
                dq_ref[rows, LANES * hh:LANES * (hh + 1)] += dq_acc[hh]

    return pl.pallas_call(
        body, name=name, grid=(groups, s_len // tk, s_len // tq),
        in_specs=[pl.BlockSpec((tq, qw), lambda g, j, i: (i, g)),
                  pl.BlockSpec((tk, kvw), lambda g, j, i: (j, g)),
                  pl.BlockSpec((tk, kvw), lambda g, j, i: (j, g)),
                  pl.BlockSpec((tq, qw), lambda g, j, i: (i, g)),
                  pl.BlockSpec((tq, qw), lambda g, j, i: (i, g)),
                  pl.BlockSpec((sub, 1, tq), lambda g, j, i: (g, 0, i))],
        out_specs=[pl.BlockSpec((s_len, qw), lambda g, j, i: (0, g)),
                   pl.BlockSpec((kvw, tk), lambda g, j, i: (g, j)),
                   pl.BlockSpec((kvw, tk), lambda g, j, i: (g, j))],
        out_shape=[jax.ShapeDtypeStruct((s_len, groups * qw), F32),
                   jax.ShapeDtypeStruct((groups * kvw, s_len), F32),
                   jax.ShapeDtypeStruct((groups * kvw, s_len), F32)],
        compiler_params=_params(("arbitrary", "arbitrary", "arbitrary")),
    )(q, k, v, do, o, lse_t)


def _silu_parts(g):
    sig = 1.0 / (1.0 + jnp.exp(-g))
    return g * sig, sig * (1.0 + g * (1.0 - sig))


def _out_call(x, target, oa, ob, ga, gb, wout, tm):
    s_len = x.shape[0]
    n_t = s_len // tm

    def body(x_ref, t_ref, oa_ref, ob_ref, ga_ref, gb_ref, w_ref,
             dh_ref, doa_ref, dob_ref, dga_ref, dgb_ref, dwb_ref, loss_ref, dw_ref):
        i = pl.program_id(0)

        @pl.when(i == 0)
        def _():
            dw_ref[...] = jnp.zeros(dw_ref.shape, F32)
            loss_ref[...] = jnp.zeros(loss_ref.shape, F32)

        oa_v, ob_v = oa_ref[...], ob_ref[...]
        silu_a, dsilu_a = _silu_parts(ga_ref[...])
        silu_b, dsilu_b = _silu_parts(gb_ref[...])
        ya = (oa_v * silu_a).astype(BF16)
        yb = (ob_v * silu_b).astype(BF16)
        h = x_ref[...] + _dot(ya, w_ref[0:512, :]) + _dot(yb, w_ref[512:1024, :])
        err = h - t_ref[...]
        part = jnp.sum(err * err, axis=0, keepdims=True)
        acc = part[:, 0:LANES]
        for c in range(1, D_MODEL // LANES):
            acc = acc + part[:, LANES * c:LANES * (c + 1)]
        loss_ref[...] += acc
        dh = err * (1.0 / D_MODEL)
        dh_ref[...] = dh
        dhb = dh.astype(BF16)
        dya = _dot_nt(dhb, w_ref[0:512, :])
        dyb = _dot_nt(dhb, w_ref[512:1024, :])
        doa = dya * silu_a
        dob = dyb * silu_b
        doa_ref[...] = doa.astype(BF16)
        dob_ref[...] = dob.astype(BF16)
        dga_ref[...] = dya * oa_v * dsilu_a
        dgb_ref[...] = dyb * ob_v * dsilu_b
        dw_ref[0:512, :] += _dot_tn(ya, dhb)
        dw_ref[512:1024, :] += _dot_tn(yb, dhb)

        @pl.when(i == n_t - 1)
        def _():
            dwb_ref[...] = dw_ref[...].astype(BF16)

    def rows(width):
        return pl.BlockSpec((tm, width), lambda i: (i, 0))

    outs = [(D_MODEL, F32), (512, BF16), (512, BF16), (512, F32), (512, F32)]
    return pl.pallas_call(
        body, name="out_fwd", grid=(n_t,),
        in_specs=[rows(D_MODEL), rows(D_MODEL), rows(512), rows(512), rows(512), rows(512),
                  _full((D_MODEL, D_MODEL))],
        out_specs=[rows(wd) for wd, _ in outs] + [_full((D_MODEL, D_MODEL)), _full((1, LANES))],
        out_shape=[jax.ShapeDtypeStruct((s_len, wd), dt) for wd, dt in outs]
        + [jax.ShapeDtypeStruct((D_MODEL, D_MODEL), BF16), jax.ShapeDtypeStruct((1, LANES), F32)],
        scratch_shapes=[pltpu.VMEM((D_MODEL, D_MODEL), F32)],
        compiler_params=_params(("arbitrary",)),
    )(x, target, oa, ob, ga, gb, wout)


def _pre_bwd_call(x, raw, dh, dqa, dka, dva, dga, dqb, dkb, dvb, dgb, consts, tabs, tm):
    s_len = x.shape[0]
    ts = min(256, tm)

    def body(x_ref, raw_ref, dh_ref, dqa_ref, dkat_ref, dvat_ref, dga_ref, dqb_ref, dkbt_ref, dvbt_ref, dgb_ref,
             *refs):
        tab_refs, refs = refs[:8], refs[8:]
        (gin_ref, w_ref, wuq_ref, wuk_ref, wuv_ref, gq_ref, gk_ref, gcq_ref, gckv_ref, gqb_ref, gkb_ref, g64_ref,
         dx_ref, dproj_ref, dwuq_ref, dwuk_ref, dwuv_ref, small_ref) = refs
        i = pl.program_id(0)

        @pl.when(i == 0)
        def _():
            dwuq_ref[...] = jnp.zeros(dwuq_ref.shape, F32)
            dwuk_ref[...] = jnp.zeros(dwuk_ref.shape, F32)
            dwuv_ref[...] = jnp.zeros(dwuv_ref.shape, F32)
            small_ref[...] = jnp.zeros(small_ref.shape, F32)

        gin, gq, gk = gin_ref[...], gq_ref[...], gk_ref[...]
        gcq, gckv, gqb, gkb = gcq_ref[...], gckv_ref[...], gqb_ref[...], gkb_ref[...]
        w, wuq, wuk, wuv = w_ref[...], wuq_ref[...], wuk_ref[...], wuv_ref[...]
        for part in range(tm // ts):
            r_ = slice(ts * part, ts * (part + 1))
            dka_v, dva_v = dkat_ref[:, r_].T, dvat_ref[:, r_].T
            dkb_v, dvb_v = dkbt_ref[:, r_].T, dvbt_ref[:, r_].T
            ca, sa, cb, sb = _rope_tiles(tab_refs, i * (tm // ts) + part, ts)
            f = _pre_forward(x_ref[r_, :], gin, w, wuq, wuk, wuv, gq, gk, gcq, gckv, gqb, gkb,
                             ca, sa, cb, sb, g64_ref[...], ts, raw=raw_ref[r_, :])
            sel16, sel8, gs64 = f["sel16"], f["sel8"], f["gs64"]
            lane = lax.broadcasted_iota(jnp.int32, (ts, LANES), 1)
            low = lane < 64
            zero = jnp.zeros((ts, LANES), F32)
            pieces = []

            dgq = jnp.zeros((1, LANES), F32)
            for s in range(4):
                _, xh, r = f["qa"][s]
                d = dqa_ref[r_, LANES * s:LANES * (s + 1)] * QA_SCALE
                dx, dg = _col_bwd(d, xh, r, gs64, 64.0, gq, ca, sa, 16, sel16)
                pieces.append(dx)
                dgq = dgq + dg
            dgk = jnp.zeros((1, LANES), F32)
            for s in range(2):
                _, xh, r = f["ka"][s]
                d = dka_v[:, LANES * s:LANES * (s + 1)]
                d = d + pltpu.roll(d, 64, 1)
                dx, dg = _col_bwd(d, xh, r, _row_sum, 128.0, gk, ca, sa, 16, sel16)
                pieces.append(jnp.where(low, dx, zero))
                dgk = dgk + dg
            for s in range(2):
                d = dva_v[:, LANES * s:LANES * (s + 1)]
                d = d + pltpu.roll(d, 64, 1)
                pieces.append(jnp.where(low, d, zero))
            pieces.append(dga_ref[r_, :])

            dgqb = jnp.zeros((1, LANES), F32)
            dq_cols = []
            for h in range(B_HEADS):
                _, xh, r = f["qb"][h]
                dx, dg = _col_bwd(dqb_ref[r_, LANES * h:LANES * (h + 1)] * QB_SCALE, xh, r, _row_sum,
                                  float(B_QK_DIM), gqb, cb, sb, 8, sel8)
                dq_cols.append(dx)
                dgqb = dgqb + dg
            dqr_b = jnp.concatenate(dq_cols, axis=1).astype(BF16)
            dwuq_ref[...] += _dot_tn(f["cqb"], dqr_b)
            dcq_raw, dgcq = _rms_bwd(_dot_nt(dqr_b, wuq), f["cqh"], f["rcq"], gcq)
            pieces.append(dcq_raw)

            dgkb = jnp.zeros((1, LANES), F32)
            dk_cols = []
            dkr = zero
            for h in range(B_HEADS):
                _, xh, r = f["kb"][h]
                dx, dg = _col_bwd(dkb_v[:, LANES * h:LANES * (h + 1)], xh, r, _row_sum, float(B_QK_DIM),
                                  gkb, cb, sb, 8, sel8)
                dk_cols.append(dx)
                dkr = dkr + dx
                dgkb = dgkb + dg
            dkr_b = jnp.concatenate(dk_cols, axis=1).astype(BF16)
            dvb_b = dvb_v.astype(BF16)
            dwuk_ref[...] += _dot_tn(f["ckvb"], dkr_b)
            dwuv_ref[...] += _dot_tn(f["ckvb"], dvb_b)
            dckv = _dot_nt(dkr_b, wuk) + _dot_nt(dvb_b, wuv)
            dckv_raw, dgckv = _rms_bwd(dckv, f["ckvh"], f["rckv"], gckv)
            pieces.append(dckv_raw)
            pieces.append(jnp.where((lane >= B_NOPE_DIM) & (lane < B_QK_DIM), dkr, zero))
            pieces += [zero, zero, zero]
            pieces.append(dgb_ref[r_, :])

            dproj_b = jnp.concatenate(pieces, axis=1).astype(BF16)
            dproj_ref[r_, :] = dproj_b
            dxn = _dot_nt(dproj_b, w)
            dx, dgin = _rms_bwd(dxn, f["xh0"], f["r0"], gin)
            dx_ref[r_, :] = dx + dh_ref[r_, :]

            for c in range(D_MODEL // LANES):
                small_ref[c:c + 1, :] += dgin[:, LANES * c:LANES * (c + 1)]
            small_ref[8:9, :] += dgq
            small_ref[9:10, :] += dgk
            for c in range(3):
                small_ref[10 + c:11 + c, :] += dgcq[:, LANES * c:LANES * (c + 1)]
            for c in range(2):
                small_ref[13 + c:14 + c, :] += dgckv[:, LANES * c:LANES * (c + 1)]
            small_ref[15:16, :] += dgqb
            small_ref[16:17, :] += dgkb

    def rows(width):
        return pl.BlockSpec((tm, width), lambda i: (i, 0))

    def cols(height):
        return pl.BlockSpec((height, tm), lambda i: (0, i))

    return pl.pallas_call(
        body, name="pre_bwd", grid=(s_len // tm,),
        in_specs=[rows(D_MODEL), rows(R_WIDTH), rows(D_MODEL), rows(512), cols(256), cols(256), rows(512), rows(512),
                  cols(512), cols(512), rows(512)] + [_full(t.shape) for t in tabs] + _pre_const_specs(consts),
        out_specs=[rows(D_MODEL), rows(N_WIDE), _full((B_Q_RANK, 512)), _full((B_KV_RANK, 512)),
                   _full((B_KV_RANK, 512)), _full((R_SMALL, LANES))],
        out_shape=[jax.ShapeDtypeStruct((s_len, D_MODEL), F32), jax.ShapeDtypeStruct((s_len, N_WIDE), BF16),
                   jax.ShapeDtypeStruct((B_Q_RANK, 512), F32),
                   jax.ShapeDtypeStruct((B_KV_RANK, 512), F32), jax.ShapeDtypeStruct((B_KV_RANK, 512), F32),
                   jax.ShapeDtypeStruct((R_SMALL, LANES), F32)],
        compiler_params=_params(("arbitrary",)),
    )(x, raw, dh, dqa, dka, dva, dga, dqb, dkb, dvb, dgb, *tabs, *[consts[n] for n in _PRE_IN_NAMES])


def _dw_in_call(xnb, dproj_b, tt, tn):
    s_len = xnb.shape[0]

    def body(a_ref, b_ref, o_ref):
        @pl.when(pl.program_id(1) == 0)
        def _():
            o_ref[...] = jnp.zeros(o_ref.shape, F32)

        o_ref[...] += _dot_tn(a_ref[...], b_ref[...])

    return pl.pallas_call(
        body, name="dw_in", grid=(N_WIDE // tn, s_len // tt),
        in_specs=[pl.BlockSpec((tt, D_MODEL), lambda n, t: (t, 0)), pl.BlockSpec((tt, tn), lambda n, t: (t, n))],
        out_specs=pl.BlockSpec((D_MODEL, tn), lambda n, t: (0, n)),
        out_shape=jax.ShapeDtypeStruct((D_MODEL, N_WIDE), F32),
        compiler_params=_params(("arbitrary", "arbitrary")),
    )(xnb, dproj_b)


def _mesh_pos():
    return lax.axis_index("x"), lax.axis_index("y"), lax.axis_index("c")


def _flip(v, bit):
    return 1 - v if bit else v


def _peer(pos, k):
    x, y, c = pos
    return _flip(x, (k >> 2) & 1), _flip(y, (k >> 1) & 1), _flip(c, k & 1)


def _logical(p):
    return 4 * p[0] + 2 * p[1] + p[2]


def _adamw(w, g, m, v):
    m = ADAM_B1 * m + (1.0 - ADAM_B1) * g
    v = ADAM_B2 * v + (1.0 - ADAM_B2) * (g * g)
    m_hat = m / (1.0 - ADAM_B1 ** ADAM_STEP)
    v_hat = v / (1.0 - ADAM_B2 ** ADAM_STEP)
    delta = -ADAM_LR * (m_hat / (jnp.sqrt(v_hat) + ADAM_EPS) + ADAM_WD * w)
    return delta, m, v


W_BLOCKS = ((D_MODEL, N_IN // N_DEV), (B_Q_RANK // N_DEV, 384), (B_KV_RANK, 768 // N_DEV), (D_MODEL // N_DEV, D_MODEL))
N_W = len(W_BLOCKS)


def _gather_blocks_call(blocks):
    def body(*refs):
        x_refs, out_refs, xb_refs = refs[0:N_W], refs[N_W:2 * N_W], refs[2 * N_W:3 * N_W]
        send_sems, recv_sems, local_sems = refs[3 * N_W:]
        x, y, c = _mesh_pos()
        me, sibling = (x, y, c), (x, y, 1 - c)
        chips = [(1 - x, y), (x, 1 - y), (1 - x, 1 - y)]
        for w in range(N_W):
            xb_refs[w][...] = x_refs[w][...].astype(BF16)

        def slot(w, p):
            return out_refs[w].at[_logical(p)]

        def copy(w, k, block, to, src=None):
            return pltpu.make_async_remote_copy(
                src_ref=slot(w, block) if src is None else src, dst_ref=slot(w, block),
                send_sem=send_sems.at[N_W * k + w], recv_sem=recv_sems.at[N_W * k + w],
                device_id=to, device_id_type=pl.DeviceIdType.MESH)

        mine = [pltpu.make_async_copy(xb_refs[w], slot(w, me), local_sems.at[w]) for w in range(N_W)]
        for cp in mine:
            cp.start()
        first = [copy(w, 0, me, sibling, src=xb_refs[w]) for w in range(N_W)]
        first += [copy(w, 1 + j, me, (*chip, c), src=xb_refs[w]) for j, chip in enumerate(chips) for w in range(N_W)]
        for cp in first:
            cp.start()
        passed = []
        for j, chip in enumerate(chips):
            for w in range(N_W):
                copy(w, 1 + j, (*chip, c), me).wait_recv()
                fwd = copy(w, 4 + j, (*chip, c), sibling)
                fwd.start()
                passed.append(fwd)
        for w in range(N_W):
            copy(w, 0, sibling, me).wait_recv()
        for j, chip in enumerate(chips):
            for w in range(N_W):
                copy(w, 4 + j, (*chip, 1 - c), me).wait_recv()
        for cp in first + passed:
            cp.wait_send()
        for cp in mine:
            cp.wait()

    vm = pl.BlockSpec(memory_space=pltpu.VMEM)
    return pl.pallas_call(
        body, name="gather_weights",
        out_shape=[jax.ShapeDtypeStruct((N_DEV,) + shp, BF16) for shp in W_BLOCKS],
        in_specs=[vm] * N_W, out_specs=[vm] * N_W,
        scratch_shapes=[pltpu.VMEM(shp, BF16) for shp in W_BLOCKS]
        + [pltpu.SemaphoreType.DMA((7 * N_W,)), pltpu.SemaphoreType.DMA((7 * N_W,)), pltpu.SemaphoreType.DMA((N_W,))],
        compiler_params=pltpu.CompilerParams(vmem_limit_bytes=VMEM_LIMIT),
    )(*blocks)


def _reduce_two_level_call(parts, small, w_blk, m_blk, v_blk, w_s, m_s, v_s):
    chunks = (32, 48, 64, 16)
    n_chip = N_DEV // 2

    def body(*refs):
        p_refs = refs[0:4]
        small_ref = refs[4]
        w_refs, m_refs, v_refs = refs[5:9], refs[9:13], refs[13:17]
        ws_ref, ms_ref, vs_ref = refs[17:20]
        g_refs, d_refs, nm_refs, nv_refs = refs[20:24], refs[24:28], refs[28:32], refs[32:36]
        gs_ref, ds_ref, nms_ref, nvs_ref = refs[36:40]
        ra_refs, rb_refs, st_refs = refs[40:44], refs[44:48], refs[48:52]
        recv_s_ref = refs[52]
        send_a, recv_a, send_b, recv_b, send_s_sems, recv_s_sems = refs[53:59]
        pos = _mesh_pos()
        x, y, c = pos
        me = _logical(pos)
        sibling = (x, y, 1 - c)

        def chip(j):
            return _flip(x, j & 1), _flip(y, (j >> 1) & 1)

        def to_sibling(w, j):
            return pltpu.make_async_remote_copy(
                src_ref=p_refs[w].at[_logical((*chip(j), 1 - c))], dst_ref=ra_refs[w].at[j],
                send_sem=send_a.at[N_W * j + w], recv_sem=recv_a.at[N_W * j + w],
                device_id=sibling, device_id_type=pl.DeviceIdType.MESH)

        def to_chip(w, j):
            return pltpu.make_async_remote_copy(
                src_ref=st_refs[w].at[j - 1], dst_ref=rb_refs[w].at[j - 1],
                send_sem=send_b.at[N_W * (j - 1) + w], recv_sem=recv_b.at[N_W * (j - 1) + w],
                device_id=(*chip(j), c), device_id_type=pl.DeviceIdType.MESH)

        def tiny(k):
            return pltpu.make_async_remote_copy(
                src_ref=small_ref, dst_ref=recv_s_ref.at[k],
                send_sem=send_s_sems.at[k], recv_sem=recv_s_sems.at[k],
                device_id=_peer(pos, k), device_id_type=pl.DeviceIdType.MESH)

        order = (0, 3, 2, 1)
        for j in (1, 2, 3, 0):
            for w in order:
                to_sibling(w, j).start()
        for k in range(1, N_DEV):
            tiny(k).start()

        for j in (1, 2, 3):
            d = _logical((*chip(j), c))
            for w in order:
                to_sibling(w, j).wait_recv()
                chunk = chunks[w]

                def pair(t, carry, w=w, j=j, d=d, chunk=chunk):
                    rows = pl.ds(pl.multiple_of(t * chunk, chunk), chunk)
                    s = p_refs[w][d, rows, :].astype(F32) + ra_refs[w][j, rows, :].astype(F32)
                    st_refs[w][j - 1, rows, :] = s.astype(BF16)
                    return carry

                lax.fori_loop(0, W_BLOCKS[w][0] // chunk, pair, 0)
                to_chip(w, j).start()

        recv_s_ref[0] = small_ref[...]
        for k in range(1, N_DEV):
            tiny(k).wait_recv()
        acc = recv_s_ref[me]
        for a in range(1, N_DEV):
            acc = acc + recv_s_ref[lax.bitwise_xor(me, a)]
        row = lax.broadcasted_iota(jnp.int32, (R_SMALL, LANES), 0)
        gs = jnp.where(row == 8, acc + pltpu.roll(acc, 64, 1), acc)
        gs = jnp.where(row == ROW_LOSS, jnp.sum(acc, axis=1, keepdims=True) * (0.5 / D_MODEL), gs)
        gs_ref[...] = gs
        ds, nms, nvs = _adamw(ws_ref[...], gs, ms_ref[...], vs_ref[...])
        ds_ref[...] = ds
        nms_ref[...] = nms
        nvs_ref[...] = nvs

        for w in (1, 2, 3, 0):
            to_sibling(w, 0).wait_recv()
            for j in (1, 2, 3):
                to_chip(w, j).wait_recv()
            chunk = chunks[w]

            def step(t, carry, w=w, chunk=chunk):
                rows = pl.ds(pl.multiple_of(t * chunk, chunk), chunk)
                g = p_refs[w][me, rows, :].astype(F32) + ra_refs[w][0, rows, :].astype(F32)
                for j in range(n_chip - 1):
                    g = g + rb_refs[w][j, rows, :].astype(F32)
                d, nm, nv = _adamw(w_refs[w][rows, :], g, m_refs[w][rows, :], v_refs[w][rows, :])
                g_refs[w][rows, :] = g
                d_refs[w][rows, :] = d
                nm_refs[w][rows, :] = nm
                nv_refs[w][rows, :] = nv
                return carry

            lax.fori_loop(0, W_BLOCKS[w][0] // chunk, step, 0)
        for k in range(1, N_DEV):
            tiny(k).wait_send()
        for w in range(N_W):
            for j in range(n_chip):
                to_sibling(w, j).wait_send()
            for j in (1, 2, 3):
                to_chip(w, j).wait_send()

    vm = pl.BlockSpec(memory_space=pltpu.VMEM)
    blk = [jax.ShapeDtypeStruct(shp, F32) for shp in W_BLOCKS]
    small_shape = jax.ShapeDtypeStruct((R_SMALL, LANES), F32)
    return pl.pallas_call(
        body, name="reduce_adamw",
        in_specs=[vm] * 20, out_specs=[vm] * 20,
        out_shape=blk * 4 + [small_shape] * 4,
        scratch_shapes=[pltpu.VMEM((n_chip,) + shp, BF16) for shp in W_BLOCKS]
        + [pltpu.VMEM((n_chip - 1,) + shp, BF16) for shp in W_BLOCKS] * 2
        + [pltpu.VMEM((N_DEV, R_SMALL, LANES), F32),
           pltpu.SemaphoreType.DMA((n_chip * N_W,)), pltpu.SemaphoreType.DMA((n_chip * N_W,)),
           pltpu.SemaphoreType.DMA(((n_chip - 1) * N_W,)), pltpu.SemaphoreType.DMA(((n_chip - 1) * N_W,)),
           pltpu.SemaphoreType.DMA((N_DEV,)), pltpu.SemaphoreType.DMA((N_DEV,))],
        compiler_params=pltpu.CompilerParams(vmem_limit_bytes=VMEM_LIMIT),
    )(*parts, small, *w_blk, *m_blk, *v_blk, w_s, m_s, v_s)


def _pack_small(norm_in, a_q, a_k, b_cq, b_ckv, b_q, b_k):
    def row(v):
        return jnp.pad(v.reshape(1, -1), ((0, 0), (0, LANES - v.size)))
    rows = [norm_in.reshape(8, LANES), row(a_q), row(a_k), b_cq.reshape(3, LANES), b_ckv.reshape(2, LANES),
            row(b_q), row(b_k), jnp.zeros((R_SMALL - 17, LANES), F32)]
    return jnp.concatenate(rows, axis=0)


def _unpack_small(s):
    return (s[0:8].reshape(1, D_MODEL), s[8:9, :64], s[9:10, :64], s[10:13].reshape(1, B_Q_RANK),
            s[13:15].reshape(1, B_KV_RANK), s[15:16, :B_QK_DIM], s[16:17, :B_QK_DIM])


C_IN = N_IN // N_DEV
_RUNS = ((0, 512, O_QA), (512, 576, O_KA), (576, 640, O_KA + 128), (640, 704, O_VA), (704, 768, O_VA + 128),
         (768, 1280, O_GA), (1280, 1664, O_CQ), (1664, 1920, O_CKV), (1920, 1952, O_KR + 64), (1952, 2464, O_GB))


def _in_cols(g_in, lo, hi):
    out = []
    for d in range(N_DEV):
        a, b = max(lo, C_IN * d), min(hi, C_IN * (d + 1))
        if a < b:
            out.append(g_in[d][:, a - C_IN * d:b - C_IN * d])
    return out


def _widen_weights(g_in, g_uq, g_ukv, g_out):
    z64 = jnp.zeros((D_MODEL, 64), BF16)
    z32 = jnp.zeros((D_MODEL, 32), BF16)
    k0, k1 = _in_cols(g_in, 512, 576), _in_cols(g_in, 576, 640)
    v0, v1 = _in_cols(g_in, 640, 704), _in_cols(g_in, 704, 768)
    kr_blk = [z64] + _in_cols(g_in, 1920, 1952) + [z32]
    w_wide = jnp.concatenate(
        _in_cols(g_in, 0, 512) + k0 + k0 + k1 + k1 + v0 + v0 + v1 + v1 + _in_cols(g_in, 768, 1920)
        + kr_blk * B_HEADS + _in_cols(g_in, 1952, 2464), axis=1)
    w_uq = g_uq.reshape(B_Q_RANK, 384)
    wuq = jnp.pad(w_uq.reshape(B_Q_RANK, B_HEADS, B_QK_DIM), ((0, 0), (0, 0), (0, LANES - B_QK_DIM)))
    wuq = wuq.reshape(B_Q_RANK, 512)
    ukv = g_ukv.transpose(1, 0, 2).reshape(B_KV_RANK, B_HEADS, B_NOPE_DIM + B_V_DIM)
    wuk = jnp.pad(ukv[:, :, :B_NOPE_DIM], ((0, 0), (0, 0), (0, LANES - B_NOPE_DIM))).reshape(B_KV_RANK, 512)
    wuv = ukv[:, :, B_NOPE_DIM:].reshape(B_KV_RANK, 512)
    return w_wide, wuq, wuk, wuv, g_out.reshape(D_MODEL, D_MODEL)


def _grad_blocks(dw_wide, dwuq, dwuk, dwuv, dw_out):
    blocks = []
    for d in range(N_DEV):
        pieces = []
        for lo, hi, wide in _RUNS:
            a, b = max(lo, C_IN * d), min(hi, C_IN * (d + 1))
            if a < b:
                pieces.append(dw_wide[:, wide + a - lo:wide + b - lo])
        blocks.append(jnp.concatenate(pieces, axis=1))
    p_in = jnp.stack(blocks).astype(BF16)
    dw_uq = dwuq.reshape(B_Q_RANK, B_HEADS, LANES)[:, :, :B_QK_DIM].reshape(N_DEV, B_Q_RANK // N_DEV, 384)
    dk = dwuk.reshape(B_KV_RANK, B_HEADS, LANES)[:, :, :B_NOPE_DIM]
    dv = dwuv.reshape(B_KV_RANK, B_HEADS, B_V_DIM)
    dw_ukv = jnp.concatenate([dk, dv], axis=2).reshape(B_KV_RANK, N_DEV, 768 // N_DEV).transpose(1, 0, 2)
    return (p_in, dw_uq.astype(BF16), dw_ukv.astype(BF16),
            dw_out.reshape(N_DEV, D_MODEL // N_DEV, D_MODEL).astype(BF16))


def _rope_tables(s_len):
    row = jnp.arange(s_len // GRID_W, dtype=F32)
    col = jnp.arange(GRID_W, dtype=F32)

    def parts(dim):
        half = dim // 2
        inv = 1.0 / (ROPE_THETA ** (jnp.arange(0, half, 2, dtype=F32) / half))
        ar, ac = row[:, None] * inv[None, :], col[:, None] * inv[None, :]
        zr, zc = jnp.zeros_like(ar), jnp.zeros_like(ac)
        cos_c = jnp.concatenate([zc, zc, jnp.cos(ac), jnp.cos(ac)], axis=1)
        cos_r = jnp.concatenate([jnp.cos(ar), jnp.cos(ar), zr, zr], axis=1)
        sin_c = jnp.concatenate([zc, zc, -jnp.sin(ac), jnp.sin(ac)], axis=1)
        sin_r = jnp.concatenate([-jnp.sin(ar), jnp.sin(ar), zr, zr], axis=1)
        return cos_c, cos_r, sin_c, sin_r

    tabs = [jnp.tile(t, (1, 2)) for t in parts(A_HEAD_DIM)]
    for n, t in enumerate(parts(B_ROPE_DIM)):
        lead = jnp.full((t.shape[0], B_NOPE_DIM), 1.0 if n == 0 else 0.0, F32)
        tail = jnp.full((t.shape[0], LANES - B_QK_DIM), 1.0 if n == 0 else 0.0, F32)
        tabs.append(jnp.concatenate([lead, t, tail], axis=1))
    return tuple(tabs)


def kernel(x, norm_in, w_in, a_q_norm, a_k_norm, b_cq_norm, b_ckv_norm, w_uq, w_ukv, b_q_norm, b_k_norm, w_out, loss_target, m_norm_in, m_w_in, m_a_q_norm, m_a_k_norm, m_b_cq_norm, m_b_ckv_norm, m_w_uq, m_w_ukv, m_b_q_norm, m_b_k_norm, m_w_out, v_norm_in, v_w_in, v_a_q_norm, v_a_k_norm, v_b_cq_norm, v_b_ckv_norm, v_w_uq, v_w_ukv, v_b_q_norm, v_b_k_norm, v_w_out):
    s_len = x.shape[1]
    tm = min(256, s_len)
    tq, tk = min(512, s_len), min(2048, s_len)
    ftq, ftk = min(256, s_len), min(1024, s_len)
    x2 = x.reshape(s_len, D_MODEL)
    t2 = loss_target.reshape(s_len, D_MODEL)

    w_blk = (w_in[0], w_uq[0], w_ukv[0], w_out[0])
    w_wide, wuq, wuk, wuv, wout = _widen_weights(*_gather_blocks_call(w_blk))

    def dup(v, pad_to=None):
        v = v.reshape(1, -1)
        if pad_to is None:
            return jnp.concatenate([v, v], axis=1)
        return jnp.pad(v, ((0, 0), (0, pad_to - v.shape[1])))

    g64 = jnp.asarray(np.kron(np.eye(2), np.ones((64, 64))), dtype=BF16)
    consts = dict(gin=norm_in, w=w_wide, wuq=wuq, wuk=wuk, wuv=wuv, gq=dup(a_q_norm), gk=dup(a_k_norm),
                  gcq=b_cq_norm, gckv=b_ckv_norm, gqb=dup(b_q_norm, LANES), gkb=dup(b_k_norm, LANES), g64=g64)
    tabs = _rope_tables(s_len)

    qa, ka, va, va_t, ga, qb, kb, vb, vb_t, gb, raw, xnb = _pre_fwd_call(x2, consts, tabs, min(512, s_len))
    oa, lse_a_t = _attn_fwd_t_call(qa, ka, va_t, groups=A_KV_HEADS, sub=4, masked=True, scale=None, tq=ftq, tk=ftk,
                                   name="attn_fwd_a")
    ob, lse_b_t = _attn_fwd_t_call(qb, kb, vb_t, groups=1, sub=B_HEADS, masked=False, scale=None, tq=ftq,
                                   tk=ftk, name="attn_fwd_b")
    dh, doa, dob, dga, dgb, dw_out, loss_row = _out_call(x2, t2, oa, ob, ga, gb, wout, min(512, s_len))

    dqa, dka_t, dva_t = _attn_bwd_q_call(qa, ka, va, doa, oa, lse_a_t, groups=A_KV_HEADS, sub=4, masked=True,
                                         tq=tq, tk=tk, ck=min(512, s_len), name="attn_bwd_a")
    dqb, dkb_t, dvb_t = _attn_bwd_q_call(qb, kb, vb, dob, ob, lse_b_t, groups=2, sub=2, masked=False,
                                         tq=tq, tk=min(4096, s_len), ck=min(256, s_len), name="attn_bwd_b")
    dx, dproj_b, dwuq, dwuk, dwuv, small = _pre_bwd_call(
        x2, raw, dh, dqa, dka_t, dva_t, dga, dqb, dkb_t, dvb_t, dgb, consts, tabs, tm)
    dw_wide = _dw_in_call(xnb, dproj_b, min(1024, s_len), N_WIDE)

    parts = _grad_blocks(dw_wide, dwuq, dwuk, dwuv, dw_out)
    small = jnp.concatenate([small[:ROW_LOSS], loss_row, small[ROW_LOSS + 1:]], axis=0)

    m_blk = (m_w_in[0], m_w_uq[0], m_w_ukv[0], m_w_out[0])
    v_blk = (v_w_in[0], v_w_uq[0], v_w_ukv[0], v_w_out[0])
    w_s = _pack_small(norm_in, a_q_norm, a_k_norm, b_cq_norm, b_ckv_norm, b_q_norm, b_k_norm)
    m_s = _pack_small(m_norm_in, m_a_q_norm, m_a_k_norm, m_b_cq_norm, m_b_ckv_norm, m_b_q_norm, m_b_k_norm)
    v_s = _pack_small(v_norm_in, v_a_q_norm, v_a_k_norm, v_b_cq_norm, v_b_ckv_norm, v_b_q_norm, v_b_k_norm)
    res = _reduce_two_level_call(parts, small, w_blk, m_blk, v_blk, w_s, m_s, v_s)

    def leaves(blocks, sm):
        wi, uq, ukv, wo = [b[None] for b in blocks]
        n_in, aq, ak, bcq, bckv, bq, bk = _unpack_small(sm)
        return [n_in, wi, aq, ak, bcq, bckv, uq, ukv, bq, bk, wo]

    g_s = res[16]
    loss = g_s[ROW_LOSS, 0]
    grad_x = dx.reshape(1, s_len, D_MODEL)
    return (loss, grad_x, *leaves(res[0:4], res[16]), *leaves(res[4:8], res[17]), *leaves(res[8:12], res[18]),
            *leaves(res[12:16], res[19]))
```

```python
import functools

import numpy as np
import jax
import jax.numpy as jnp
from jax import lax
from jax.experimental import pallas as pl
from jax.experimental.pallas import tpu as pltpu

F32 = jnp.float32
BF16 = jnp.bfloat16

D_MODEL = 1024
GRID_W = 64
ROPE_THETA = 10000.0
EPS = 1e-6
A_HEAD_DIM = 64
A_HEADS = 8
A_KV_HEADS = 2
B_HEADS = 4
B_NOPE_DIM = 64
B_ROPE_DIM = 32
B_QK_DIM = 96
B_V_DIM = 128
B_Q_RANK = 384
B_KV_RANK = 256
N_IN = 2464
N_DEV = 8

ADAM_LR = 0.001
ADAM_B1 = 0.9
ADAM_B2 = 0.999
ADAM_EPS = 1e-08
ADAM_WD = 0.01
ADAM_STEP = 10

QA_SCALE = 0.125
QB_SCALE = 1.0 / float(np.sqrt(B_QK_DIM))

LANES = 128
O_QA, O_KA, O_VA, O_GA, O_CQ, O_CKV, O_KR, O_GB, N_WIDE = 0, 512, 768, 1024, 1536, 1920, 2176, 2688, 3200
R_QA, R_KA, R_CQ, R_CKV, R_KR, R_WIDTH = 0, 512, 768, 1152, 1408, 1920

R_SMALL = 24
ROW_LOSS = 17

VMEM_LIMIT = 56 * 1024 * 1024

NT = (((1,), (1,)), ((), ()))
TN = (((0,), (0,)), ((), ()))


def _dot(a, b):
    return jnp.dot(a, b, preferred_element_type=F32)


def _dot_nt(a, b):
    return lax.dot_general(a, b, NT, preferred_element_type=F32)


def _dot_tn(a, b):
    return lax.dot_general(a, b, TN, preferred_element_type=F32)


def _params(sem=None):
    return pltpu.CompilerParams(dimension_semantics=sem, vmem_limit_bytes=VMEM_LIMIT)


def _full(shape):
    nd = len(shape)
    return pl.BlockSpec(shape, lambda *_: (0,) * nd)


def _swap_sel(rows, shift):
    lane = lax.broadcasted_iota(jnp.int32, (rows, LANES), 1)
    return pltpu.roll(lane, shift, 1) == (lane ^ shift)


def _swap(x, shift, sel):
    return jnp.where(sel, pltpu.roll(x, shift, 1), pltpu.roll(x, LANES - shift, 1))


def _group_sum64(x, g64):
    hi = x.astype(BF16)
    lo = (x - hi.astype(F32)).astype(BF16)
    return _dot(hi, g64) + _dot(lo, g64)


def _row_sum(x):
    return jnp.sum(x, axis=-1, keepdims=True)


def _col_fwd(xs, msum, denom, gain, cos, sin, shift, sel):
    r = lax.rsqrt(msum(xs * xs) * (1.0 / denom) + EPS)
    xh = xs * r
    n = xh * gain
    return n * cos + _swap(n, shift, sel) * sin, xh, r


def _col_bwd(d_out, xh, r, msum, denom, gain, cos, sin, shift, sel):
    dn = d_out * cos + _swap(d_out * sin, shift, sel)
    dgain = jnp.sum(dn * xh, axis=0, keepdims=True)
    dxh = dn * gain
    dx = r * (dxh - xh * (msum(dxh * xh) * (1.0 / denom)))
    return dx, dgain


def _rms_fwd(x, gain):
    r = lax.rsqrt(jnp.mean(x * x, axis=-1, keepdims=True) + EPS)
    xh = x * r
    return xh * gain, xh, r


def _rms_bwd(dy, xh, r, gain):
    dgain = jnp.sum(dy * xh, axis=0, keepdims=True)
    dxh = dy * gain
    dx = r * (dxh - xh * jnp.mean(dxh * xh, axis=-1, keepdims=True))
    return dx, dgain


def _pre_forward(x, gin, w, wuq, wuk, wuv, gq, gk, gcq, gckv, gqb, gkb, ca, sa, cb, sb, g64, tm, raw=None):
    sel16 = _swap_sel(tm, 16)
    sel8 = _swap_sel(tm, 8)
    xn, xh0, r0 = _rms_fwd(x, gin)
    xnb = xn.astype(BF16)
    proj = None
    if raw is None:
        proj = _dot(xnb, w)
        raw = jnp.concatenate([proj[:, O_QA:O_QA + 512], proj[:, O_KA:O_KA + 256], proj[:, O_CQ:O_CQ + B_Q_RANK],
                               proj[:, O_CKV:O_CKV + B_KV_RANK], proj[:, O_KR:O_KR + 512]], axis=1)
    gs64 = functools.partial(_group_sum64, g64=g64)
    qa = [_col_fwd(raw[:, R_QA + LANES * s:R_QA + LANES * (s + 1)], gs64, 64.0, gq, ca, sa, 16, sel16)
          for s in range(4)]
    ka = [_col_fwd(raw[:, R_KA + LANES * s:R_KA + LANES * (s + 1)], _row_sum, 128.0, gk, ca, sa, 16, sel16)
          for s in range(2)]
    cq, cqh, rcq = _rms_fwd(raw[:, R_CQ:R_CQ + B_Q_RANK], gcq)
    cqb = cq.astype(BF16)
    qb_raw = _dot(cqb, wuq)
    qb = [_col_fwd(qb_raw[:, LANES * h:LANES * (h + 1)], _row_sum, float(B_QK_DIM), gqb, cb, sb, 8, sel8)
          for h in range(B_HEADS)]
    ckv, ckvh, rckv = _rms_fwd(raw[:, R_CKV:R_CKV + B_KV_RANK], gckv)
    ckvb = ckv.astype(BF16)
    kb_raw = _dot(ckvb, wuk) + raw[:, R_KR:R_KR + 512]
    vb = _dot(ckvb, wuv)
    kb = [_col_fwd(kb_raw[:, LANES * h:LANES * (h + 1)], _row_sum, float(B_QK_DIM), gkb, cb, sb, 8, sel8)
          for h in range(B_HEADS)]
    return dict(xh0=xh0, r0=r0, xnb=xnb, proj=proj, raw=raw, qa=qa, ka=ka, cqh=cqh, rcq=rcq, cqb=cqb, qb=qb,
                ckvh=ckvh, rckv=rckv, ckvb=ckvb, kb=kb, vb=vb, sel16=sel16, sel8=sel8, gs64=gs64)


def _rope_tiles(tab_refs, i, tm):
    per_tile = tm // GRID_W
    out = []
    for t in range(4):
        col_ref, row_ref = tab_refs[2 * t], tab_refs[2 * t + 1]
        col = col_ref[...]
        out.append(jnp.concatenate([col + row_ref[pl.ds(i * per_tile + b, 1), :] for b in range(per_tile)], axis=0))
    return out


_PRE_IN_NAMES = ("gin", "w", "wuq", "wuk", "wuv", "gq", "gk", "gcq", "gckv", "gqb", "gkb", "g64")


def _pre_const_specs(consts):
    return [_full(consts[n].shape) for n in _PRE_IN_NAMES]


def _pre_fwd_call(x, consts, tabs, tm):
    s_len = x.shape[0]
    ts = min(256, tm)

    def body(x_ref, *refs):
        tab_refs, refs = refs[:8], refs[8:]
        (gin_ref, w_ref, wuq_ref, wuk_ref, wuv_ref, gq_ref, gk_ref, gcq_ref, gckv_ref, gqb_ref, gkb_ref, g64_ref,
         qa_ref, ka_ref, va_ref, vat_ref, ga_ref, qb_ref, kb_ref, vb_ref, vbt_ref, gb_ref, raw_ref, xnb_ref) = refs
        for part in range(tm // ts):
            r = slice(ts * part, ts * (part + 1))
            ca, sa, cb, sb = _rope_tiles(tab_refs, pl.program_id(0) * (tm // ts) + part, ts)
            f = _pre_forward(x_ref[r, :], gin_ref[...], w_ref[...], wuq_ref[...], wuk_ref[...], wuv_ref[...],
                             gq_ref[...], gk_ref[...], gcq_ref[...], gckv_ref[...], gqb_ref[...], gkb_ref[...],
                             ca, sa, cb, sb, g64_ref[...], ts)
            proj = f["proj"]
            raw_ref[r, :] = f["raw"]
            xnb_ref[r, :] = f["xnb"]
            for s in range(4):
                qa_ref[r, LANES * s:LANES * (s + 1)] = (f["qa"][s][0] * QA_SCALE).astype(BF16)
            for s in range(2):
                ka_ref[r, LANES * s:LANES * (s + 1)] = f["ka"][s][0].astype(BF16)
            va = proj[:, O_VA:O_VA + 256]
            va_ref[r, :] = va.astype(BF16)
            vat_ref[:, r] = va.T.astype(BF16)
            ga_ref[r, :] = proj[:, O_GA:O_GA + 512]
            for h in range(B_HEADS):
                qb_ref[r, LANES * h:LANES * (h + 1)] = (f["qb"][h][0] * QB_SCALE).astype(BF16)
                kb_ref[r, LANES * h:LANES * (h + 1)] = f["kb"][h][0].astype(BF16)
            vb_ref[r, :] = f["vb"].astype(BF16)
            vbt_ref[:, r] = f["vb"].T.astype(BF16)
            gb_ref[r, :] = proj[:, O_GB:O_GB + 512]

    def rows(width):
        return pl.BlockSpec((tm, width), lambda i: (i, 0))

    def cols(height):
        return pl.BlockSpec((height, tm), lambda i: (0, i))

    outs = [((s_len, 512), BF16, rows(512)), ((s_len, 256), BF16, rows(256)), ((s_len, 256), BF16, rows(256)),
            ((256, s_len), BF16, cols(256)), ((s_len, 512), F32, rows(512)), ((s_len, 512), BF16, rows(512)),
            ((s_len, 512), BF16, rows(512)), ((s_len, 512), BF16, rows(512)), ((512, s_len), BF16, cols(512)),
            ((s_len, 512), F32, rows(512)), ((s_len, R_WIDTH), F32, rows(R_WIDTH)),
            ((s_len, D_MODEL), BF16, rows(D_MODEL))]
    return pl.pallas_call(
        body, name="pre_fwd", grid=(s_len // tm,),
        in_specs=[rows(D_MODEL)] + [_full(t.shape) for t in tabs] + _pre_const_specs(consts),
        out_specs=[sp for _, _, sp in outs],
        out_shape=[jax.ShapeDtypeStruct(sh, dt) for sh, dt, _ in outs],
        compiler_params=_params(("arbitrary",)),
    )(x, *tabs, *[consts[n] for n in _PRE_IN_NAMES])


def _head_masks(rows):
    lane = lax.broadcasted_iota(jnp.int32, (rows, LANES), 1)
    return lane < 64, lane >= 64


def _row_fold(x, op):
    return op(x.reshape(x.shape[0] // 8, 8, x.shape[1]), axis=0)


def _attn_fwd_t_call(q, k, vt, *, groups, sub, masked, scale, tq, tk, name):
    s_len = q.shape[0]
    qw = LANES * (sub // 2 if masked else sub)
    kvw = LANES if masked else LANES * sub
    n_c = s_len // tk
    kv_mode = pl.Buffered(1) if groups == 1 else None

    def body(q_ref, qn_ref, k_ref, vt_ref, o_ref, lse_ref, s_sc, mx_sc):
        keep = _head_masks(tq) if masked else None

        def kv_of(hh):
            return slice(0, LANES) if masked else slice(LANES * hh, LANES * (hh + 1))

        def q_of(ref, hh):
            if not masked:
                return ref[:, LANES * hh:LANES * (hh + 1)]
            qp = ref[:, LANES * (hh // 2):LANES * (hh // 2 + 1)]
            return jnp.where(keep[hh % 2], qp, jnp.zeros_like(qp))

        def scores(hh, qm, c, mx):
            s_t = _dot_nt(k_ref[tk * c:tk * (c + 1), kv_of(hh)], qm)
            if scale is not None:
                s_t = s_t * scale
            s_sc[hh % 2, c] = s_t
            return jnp.maximum(mx, _row_fold(s_t, jnp.max))

        neg = jnp.full((8, tq), -jnp.inf, F32)

        @pl.when(pl.program_id(1) == 0)
        def _():
            qm0 = q_of(q_ref, 0)
            mx0 = neg
            for c in range(n_c):
                mx0 = scores(0, qm0, c, mx0)
            mx_sc[...] = mx0

        mx_next = mx_sc[...]
        outs = []
        for hh in range(sub):
            m = jnp.max(mx_next, axis=0, keepdims=True)
            nxt = (hh + 1) % sub
            qm_next = q_of(q_ref if hh + 1 < sub else qn_ref, nxt)
            mx_next = neg
            lsum = jnp.zeros((8, tq), F32)
            acc = jnp.zeros((LANES, tq), F32)
            for c in range(n_c):
                p_t = jnp.exp(s_sc[hh % 2, c] - m)
                lsum = lsum + _row_fold(p_t, jnp.sum)
                mx_next = scores(nxt, qm_next, c, mx_next)
                acc = acc + _dot(vt_ref[kv_of(hh), tk * c:tk * (c + 1)], p_t.astype(BF16))
            l = jnp.sum(lsum, axis=0, keepdims=True)
            outs.append((acc / l).T)
            lse_ref[hh] = m + jnp.log(l)
        mx_sc[...] = mx_next
        if masked:
            for pr in range(sub // 2):
                o_ref[:, LANES * pr:LANES * (pr + 1)] = jnp.where(keep[0], outs[2 * pr], outs[2 * pr + 1])
        else:
            for hh in range(sub):
                o_ref[:, LANES * hh:LANES * (hh + 1)] = outs[hh]

    n_q = s_len // tq
    return pl.pallas_call(
        body, name=name, grid=(groups, n_q),
        in_specs=[pl.BlockSpec((tq, qw), lambda g, i: (i, g)),
                  pl.BlockSpec((tq, qw), lambda g, i: (jnp.minimum(i + 1, n_q - 1), g)),
                  pl.BlockSpec((s_len, kvw), lambda g, i: (0, g), pipeline_mode=kv_mode),
                  pl.BlockSpec((kvw, s_len), lambda g, i: (g, 0), pipeline_mode=kv_mode)],
        out_specs=[pl.BlockSpec((tq, qw), lambda g, i: (i, g)),
                   pl.BlockSpec((sub, 1, tq), lambda g, i: (g, 0, i))],
        out_shape=[jax.ShapeDtypeStruct((s_len, groups * qw), F32),
                   jax.ShapeDtypeStruct((groups * sub, 1, s_len), F32)],
        scratch_shapes=[pltpu.VMEM((2, n_c, tk, tq), F32), pltpu.VMEM((8, tq), F32)],
        compiler_params=_params(("arbitrary", "arbitrary")),
    )(q, q, k, vt)


def _attn_bwd_q_call(q, k, v, do, o, lse_t, *, groups, sub, masked, tq, tk, ck, name):
    s_len = q.shape[0]
    qw = LANES * (sub // 2 if masked else sub)
    kvw = LANES if masked else LANES * sub
    n_c = tk // ck

    def body(q_ref, k_ref, v_ref, do_ref, o_ref, lse_ref, dq_ref, dkt_ref, dvt_ref):
        j = pl.program_id(1)
        i = pl.program_id(2)

        @pl.when((j == 0) & (i == 0))
        def _():
            dq_ref[...] = jnp.zeros(dq_ref.shape, F32)

        @pl.when(i == 0)
        def _():
            dkt_ref[...] = jnp.zeros(dkt_ref.shape, F32)
            dvt_ref[...] = jnp.zeros(dvt_ref.shape, F32)

        lkeep = _head_masks(tq) if masked else None
        heads = []
        for hh in range(sub):
            if masked:
                cols = slice(LANES * (hh // 2), LANES * (hh // 2 + 1))
                kv = slice(0, LANES)
                qp, dop = q_ref[:, cols], do_ref[:, cols]
                qm = jnp.where(lkeep[hh % 2], qp, jnp.zeros_like(qp))
                dom = jnp.where(lkeep[hh % 2], dop, jnp.zeros_like(dop))
            else:
                cols = kv = slice(LANES * hh, LANES * (hh + 1))
                qm, dom = q_ref[:, cols], do_ref[:, cols]
            delta = jnp.sum(dom.astype(F32) * o_ref[:, cols], axis=1, keepdims=True)
            lse = jnp.broadcast_to(lse_ref[hh], (LANES, tq)).T[:, 0:1]
            heads.append((cols, kv, qm, dom, qm.T, dom.T, delta, lse))

        def products(hh, c):
            _, kv, qm, dom, _, _, _, _ = heads[hh]
            return _dot_nt(qm, k_ref[ck * c:ck * (c + 1), kv]), _dot_nt(dom, v_ref[ck * c:ck * (c + 1), kv])

        items = [(hh, c) for hh in range(sub) for c in range(n_c)]
        dq_acc = [jnp.zeros((tq, LANES), F32) for _ in range(sub)]
        nxt = products(*items[0])
        for n, (hh, c) in enumerate(items):
            s, dp = nxt
            if n + 1 < len(items):
                nxt = products(*items[n + 1])
            _, kv, qm, dom, qmt, domt, delta, lse = heads[hh]
            p = jnp.exp(s - lse)
            ds = p * (dp - delta)
            p_b = p.astype(BF16)
            ds_b = ds.astype(BF16)
            kcols = slice(ck * c, ck * (c + 1))
            dvt_ref[kv, kcols] += _dot(domt, p_b)
            dkt_ref[kv, kcols] += _dot(qmt, ds_b)
            dq_acc[hh] = dq_acc[hh] + _dot(ds_b, k_ref[kcols, kv])
        rows = pl.ds(pl.multiple_of(i * tq, tq), tq)
        if masked:
            for pr in range(sub // 2):
                dq_ref[rows, LANES * pr:LANES * (pr + 1)] += jnp.where(lkeep[0], dq_acc[2 * pr], dq_acc[2 * pr + 1])
        else:
            for hh in range(sub):
                dq_ref[rows, LANES * hh:LANES * (hh + 1)] += dq_acc[hh]

    return pl.pallas_call(
        body, name=name, grid=(groups, s_len // tk, s_len // tq),
        in_specs=[pl.BlockSpec((tq, qw), lambda g, j, i: (i, g)),
                  pl.BlockSpec((tk, kvw), lambda g, j, i: (j, g)),
                  pl.BlockSpec((tk, kvw), lambda g, j, i: (j, g)),
                  pl.BlockSpec((tq, qw), lambda g, j, i: (i, g)),
                  pl.BlockSpec((tq, qw), lambda g, j, i: (i, g)),
                  pl.BlockSpec((sub, 1, tq), lambda g, j, i: (g, 0, i))],
        out_specs=[pl.BlockSpec((s_len, qw), lambda g, j, i: (0, g)),
                   pl.BlockSpec((kvw, tk), lambda g, j, i: (g, j)),
                   pl.BlockSpec((kvw, tk), lambda g, j, i: (g, j))],
        out_shape=[jax.ShapeDtypeStruct((s_len, groups * qw), F32),
                   jax.ShapeDtypeStruct((groups * kvw, s_len), F32),
                   jax.ShapeDtypeStruct((groups * kvw, s_len), F32)],
        compiler_params=_params(("arbitrary", "arbitrary", "arbitrary")),
    )(q, k, v, do, o, lse_t)


def _silu_parts(g):
    sig = 1.0 / (1.0 + jnp.exp(-g))
    return g * sig, sig * (1.0 + g * (1.0 - sig))


def _out_call(x, target, oa, ob, ga, gb, wout, tm):
    s_len = x.shape[0]
    n_t = s_len // tm

    def body(x_ref, t_ref, oa_ref, ob_ref, ga_ref, gb_ref, w_ref,
             dh_ref, doa_ref, dob_ref, dga_ref, dgb_ref, dwb_ref, loss_ref, dw_ref):
        i = pl.program_id(0)

        @pl.when(i == 0)
        def _():
            dw_ref[...] = jnp.zeros(dw_ref.shape, F32)
            loss_ref[...] = jnp.zeros(loss_ref.shape, F32)

        oa_v, ob_v = oa_ref[...], ob_ref[...]
        silu_a, dsilu_a = _silu_parts(ga_ref[...])
        silu_b, dsilu_b = _silu_parts(gb_ref[...])
        ya = (oa_v * silu_a).astype(BF16)
        yb = (ob_v * silu_b).astype(BF16)
        h = x_ref[...] + _dot(ya, w_ref[0:512, :]) + _dot(yb, w_ref[512:1024, :])
        err = h - t_ref[...]
        part = jnp.sum(err * err, axis=0, keepdims=True)
        acc = part[:, 0:LANES]
        for c in range(1, D_MODEL // LANES):
            acc = acc + part[:, LANES * c:LANES * (c + 1)]
        loss_ref[...] += acc
        dh = err * (1.0 / D_MODEL)
        dh_ref[...] = dh
        dhb = dh.astype(BF16)
        dya = _dot_nt(dhb, w_ref[0:512, :])
        dyb = _dot_nt(dhb, w_ref[512:1024, :])
        doa = dya * silu_a
        dob = dyb * silu_b
        doa_ref[...] = doa.astype(BF16)
        dob_ref[...] = dob.astype(BF16)
        dga_ref[...] = dya * oa_v * dsilu_a
        dgb_ref[...] = dyb * ob_v * dsilu_b
        dw_ref[0:512, :] += _dot_tn(ya, dhb)
        dw_ref[512:1024, :] += _dot_tn(yb, dhb)

        @pl.when(i == n_t - 1)
        def _():
            dwb_ref[...] = dw_ref[...].astype(BF16)

    def rows(width):
        return pl.BlockSpec((tm, width), lambda i: (i, 0))

    outs = [(D_MODEL, F32), (512, BF16), (512, BF16), (512, F32), (512, F32)]
    return pl.pallas_call(
        body, name="out_fwd", grid=(n_t,),
        in_specs=[rows(D_MODEL), rows(D_MODEL), rows(512), rows(512), rows(512), rows(512),
                  _full((D_MODEL, D_MODEL))],
        out_specs=[rows(wd) for wd, _ in outs] + [_full((D_MODEL, D_MODEL)), _full((1, LANES))],
        out_shape=[jax.ShapeDtypeStruct((s_len, wd), dt) for wd, dt in outs]
        + [jax.ShapeDtypeStruct((D_MODEL, D_MODEL), BF16), jax.ShapeDtypeStruct((1, LANES), F32)],
        scratch_shapes=[pltpu.VMEM((D_MODEL, D_MODEL), F32)],
        compiler_params=_params(("arbitrary",)),
    )(x, target, oa, ob, ga, gb, wout)


def _pre_bwd_call(x, raw, dh, dqa, dka, dva, dga, dqb, dkb, dvb, dgb, consts, tabs, tm):
    s_len = x.shape[0]
    ts = min(256, tm)

    def body(x_ref, raw_ref, dh_ref, dqa_ref, dkat_ref, dvat_ref, dga_ref, dqb_ref, dkbt_ref, dvbt_ref, dgb_ref,
             *refs):
        tab_refs, refs = refs[:8], refs[8:]
        (gin_ref, w_ref, wuq_ref, wuk_ref, wuv_ref, gq_ref, gk_ref, gcq_ref, gckv_ref, gqb_ref, gkb_ref, g64_ref,
         dx_ref, dproj_ref, dwuq_ref, dwuk_ref, dwuv_ref, small_ref) = refs
        i = pl.program_id(0)

        @pl.when(i == 0)
        def _():
            dwuq_ref[...] = jnp.zeros(dwuq_ref.shape, F32)
            dwuk_ref[...] = jnp.zeros(dwuk_ref.shape, F32)
            dwuv_ref[...] = jnp.zeros(dwuv_ref.shape, F32)
            small_ref[...] = jnp.zeros(small_ref.shape, F32)

        gin, gq, gk = gin_ref[...], gq_ref[...], gk_ref[...]
        gcq, gckv, gqb, gkb = gcq_ref[...], gckv_ref[...], gqb_ref[...], gkb_ref[...]
        w, wuq, wuk, wuv = w_ref[...], wuq_ref[...], wuk_ref[...], wuv_ref[...]
        for part in range(tm // ts):
            r_ = slice(ts * part, ts * (part + 1))
            dka_v, dva_v = dkat_ref[:, r_].T, dvat_ref[:, r_].T
            dkb_v, dvb_v = dkbt_ref[:, r_].T, dvbt_ref[:, r_].T
            ca, sa, cb, sb = _rope_tiles(tab_refs, i * (tm // ts) + part, ts)
            f = _pre_forward(x_ref[r_, :], gin, w, wuq, wuk, wuv, gq, gk, gcq, gckv, gqb, gkb,
                             ca, sa, cb, sb, g64_ref[...], ts, raw=raw_ref[r_, :])
            sel16, sel8, gs64 = f["sel16"], f["sel8"], f["gs64"]
            lane = lax.broadcasted_iota(jnp.int32, (ts, LANES), 1)
            low = lane < 64
            zero = jnp.zeros((ts, LANES), F32)
            pieces = []

            dgq = jnp.zeros((1, LANES), F32)
            for s in range(4):
                _, xh, r = f["qa"][s]
                d = dqa_ref[r_, LANES * s:LANES * (s + 1)] * QA_SCALE
                dx, dg = _col_bwd(d, xh, r, gs64, 64.0, gq, ca, sa, 16, sel16)
                pieces.append(dx)
                dgq = dgq + dg
            dgk = jnp.zeros((1, LANES), F32)
            for s in range(2):
                _, xh, r = f["ka"][s]
                d = dka_v[:, LANES * s:LANES * (s + 1)]
                d = d + pltpu.roll(d, 64, 1)
                dx, dg = _col_bwd(d, xh, r, _row_sum, 128.0, gk, ca, sa, 16, sel16)
                pieces.append(jnp.where(low, dx, zero))
                dgk = dgk + dg
            for s in range(2):
                d = dva_v[:, LANES * s:LANES * (s + 1)]
                d = d + pltpu.roll(d, 64, 1)
                pieces.append(jnp.where(low, d, zero))
            pieces.append(dga_ref[r_, :])

            dgqb = jnp.zeros((1, LANES), F32)
            dq_cols = []
            for h in range(B_HEADS):
                _, xh, r = f["qb"][h]
                dx, dg = _col_bwd(dqb_ref[r_, LANES * h:LANES * (h + 1)] * QB_SCALE, xh, r, _row_sum,
                                  float(B_QK_DIM), gqb, cb, sb, 8, sel8)
                dq_cols.append(dx)
                dgqb = dgqb + dg
            dqr_b = jnp.concatenate(dq_cols, axis=1).astype(BF16)
            dwuq_ref[...] += _dot_tn(f["cqb"], dqr_b)
            dcq_raw, dgcq = _rms_bwd(_dot_nt(dqr_b, wuq), f["cqh"], f["rcq"], gcq)
            pieces.append(dcq_raw)

            dgkb = jnp.zeros((1, LANES), F32)
            dk_cols = []
            dkr = zero
            for h in range(B_HEADS):
                _, xh, r = f["kb"][h]
                dx, dg = _col_bwd(dkb_v[:, LANES * h:LANES * (h + 1)], xh, r, _row_sum, float(B_QK_DIM),
                                  gkb, cb, sb, 8, sel8)
                dk_cols.append(dx)
                dkr = dkr + dx
                dgkb = dgkb + dg
            dkr_b = jnp.concatenate(dk_cols, axis=1).astype(BF16)
            dvb_b = dvb_v.astype(BF16)
            dwuk_ref[...] += _dot_tn(f["ckvb"], dkr_b)
            dwuv_ref[...] += _dot_tn(f["ckvb"], dvb_b)
            dckv = _dot_nt(dkr_b, wuk) + _dot_nt(dvb_b, wuv)
            dckv_raw, dgckv = _rms_bwd(dckv, f["ckvh"], f["rckv"], gckv)
            pieces.append(dckv_raw)
            pieces.append(jnp.where((lane >= B_NOPE_DIM) & (lane < B_QK_DIM), dkr, zero))
            pieces += [zero, zero, zero]
            pieces.append(dgb_ref[r_, :])

            dproj_b = jnp.concatenate(pieces, axis=1).astype(BF16)
            dproj_ref[r_, :] = dproj_b
            dxn = _dot_nt(dproj_b, w)
            dx, dgin = _rms_bwd(dxn, f["xh0"], f["r0"], gin)
            dx_ref[r_, :] = dx + dh_ref[r_, :]

            for c in range(D_MODEL // LANES):
                small_ref[c:c + 1, :] += dgin[:, LANES * c:LANES * (c + 1)]
            small_ref[8:9, :] += dgq
            small_ref[9:10, :] += dgk
            for c in range(3):
                small_ref[10 + c:11 + c, :] += dgcq[:, LANES * c:LANES * (c + 1)]
            for c in range(2):
                small_ref[13 + c:14 + c, :] += dgckv[:, LANES * c:LANES * (c + 1)]
            small_ref[15:16, :] += dgqb
            small_ref[16:17, :] += dgkb

    def rows(width):
        return pl.BlockSpec((tm, width), lambda i: (i, 0))

    def cols(height):
        return pl.BlockSpec((height, tm), lambda i: (0, i))

    return pl.pallas_call(
        body, name="pre_bwd", grid=(s_len // tm,),
        in_specs=[rows(D_MODEL), rows(R_WIDTH), rows(D_MODEL), rows(512), cols(256), cols(256), rows(512), rows(512),
                  cols(512), cols(512), rows(512)] + [_full(t.shape) for t in tabs] + _pre_const_specs(consts),
        out_specs=[rows(D_MODEL), rows(N_WIDE), _full((B_Q_RANK, 512)), _full((B_KV_RANK, 512)),
                   _full((B_KV_RANK, 512)), _full((R_SMALL, LANES))],
        out_shape=[jax.ShapeDtypeStruct((s_len, D_MODEL), F32), jax.ShapeDtypeStruct((s_len, N_WIDE), BF16),
                   jax.ShapeDtypeStruct((B_Q_RANK, 512), F32),
                   jax.ShapeDtypeStruct((B_KV_RANK, 512), F32), jax.ShapeDtypeStruct((B_KV_RANK, 512), F32),
                   jax.ShapeDtypeStruct((R_SMALL, LANES), F32)],
        compiler_params=_params(("arbitrary",)),
    )(x, raw, dh, dqa, dka, dva, dga, dqb, dkb, dvb, dgb, *tabs, *[consts[n] for n in _PRE_IN_NAMES])


def _dw_in_call(xnb, dproj_b, tt, tn):
    s_len = xnb.shape[0]

    def body(a_ref, b_ref, o_ref):
        @pl.when(pl.program_id(1) == 0)
        def _():
            o_ref[...] = jnp.zeros(o_ref.shape, F32)

        o_ref[...] += _dot_tn(a_ref[...], b_ref[...])

    return pl.pallas_call(
        body, name="dw_in", grid=(N_WIDE // tn, s_len // tt),
        in_specs=[pl.BlockSpec((tt, D_MODEL), lambda n, t: (t, 0)), pl.BlockSpec((tt, tn), lambda n, t: (t, n))],
        out_specs=pl.BlockSpec((D_MODEL, tn), lambda n, t: (0, n)),
        out_shape=jax.ShapeDtypeStruct((D_MODEL, N_WIDE), F32),
        compiler_params=_params(("arbitrary", "arbitrary")),
    )(xnb, dproj_b)


def _mesh_pos():
    return lax.axis_index("x"), lax.axis_index("y"), lax.axis_index("c")


def _flip(v, bit):
    return 1 - v if bit else v


def _peer(pos, k):
    x, y, c = pos
    return _flip(x, (k >> 2) & 1), _flip(y, (k >> 1) & 1), _flip(c, k & 1)


def _logical(p):
    return 4 * p[0] + 2 * p[1] + p[2]


def _adamw(w, g, m, v):
    m = ADAM_B1 * m + (1.0 - ADAM_B1) * g
    v = ADAM_B2 * v + (1.0 - ADAM_B2) * (g * g)
    m_hat = m / (1.0 - ADAM_B1 ** ADAM_STEP)
    v_hat = v / (1.0 - ADAM_B2 ** ADAM_STEP)
    delta = -ADAM_LR * (m_hat / (jnp.sqrt(v_hat) + ADAM_EPS) + ADAM_WD * w)
    return delta, m, v


W_BLOCKS = ((D_MODEL, N_IN // N_DEV), (B_Q_RANK // N_DEV, 384), (B_KV_RANK, 768 // N_DEV), (D_MODEL // N_DEV, D_MODEL))
N_W = len(W_BLOCKS)


def _gather_blocks_call(blocks):
    def body(*refs):
        x_refs, out_refs, xb_refs = refs[0:N_W], refs[N_W:2 * N_W], refs[2 * N_W:3 * N_W]
        send_sems, recv_sems, local_sems = refs[3 * N_W:]
        x, y, c = _mesh_pos()
        me, sibling = (x, y, c), (x, y, 1 - c)
        chips = [(1 - x, y), (x, 1 - y), (1 - x, 1 - y)]
        for w in range(N_W):
            xb_refs[w][...] = x_refs[w][...].astype(BF16)

        def slot(w, p):
            return out_refs[w].at[_logical(p)]

        def copy(w, k, block, to, src=None):
            return pltpu.make_async_remote_copy(
                src_ref=slot(w, block) if src is None else src, dst_ref=slot(w, block),
                send_sem=send_sems.at[N_W * k + w], recv_sem=recv_sems.at[N_W * k + w],
                device_id=to, device_id_type=pl.DeviceIdType.MESH)

        mine = [pltpu.make_async_copy(xb_refs[w], slot(w, me), local_sems.at[w]) for w in range(N_W)]
        for cp in mine:
            cp.start()
        first = [copy(w, 0, me, sibling, src=xb_refs[w]) for w in range(N_W)]
        first += [copy(w, 1 + j, me, (*chip, c), src=xb_refs[w]) for j, chip in enumerate(chips) for w in range(N_W)]
        for cp in first:
            cp.start()
        passed = []
        for j, chip in enumerate(chips):
            for w in range(N_W):
                copy(w, 1 + j, (*chip, c), me).wait_recv()
                fwd = copy(w, 4 + j, (*chip, c), sibling)
                fwd.start()
                passed.append(fwd)
        for w in range(N_W):
            copy(w, 0, sibling, me).wait_recv()
        for j, chip in enumerate(chips):
            for w in range(N_W):
                copy(w, 4 + j, (*chip, 1 - c), me).wait_recv()
        for cp in first + passed:
            cp.wait_send()
        for cp in mine:
            cp.wait()

    vm = pl.BlockSpec(memory_space=pltpu.VMEM)
    return pl.pallas_call(
        body, name="gather_weights",
        out_shape=[jax.ShapeDtypeStruct((N_DEV,) + shp, BF16) for shp in W_BLOCKS],
        in_specs=[vm] * N_W, out_specs=[vm] * N_W,
        scratch_shapes=[pltpu.VMEM(shp, BF16) for shp in W_BLOCKS]
        + [pltpu.SemaphoreType.DMA((7 * N_W,)), pltpu.SemaphoreType.DMA((7 * N_W,)), pltpu.SemaphoreType.DMA((N_W,))],
        compiler_params=pltpu.CompilerParams(vmem_limit_bytes=VMEM_LIMIT),
    )(*blocks)


def _reduce_two_level_call(parts, small, w_blk, m_blk, v_blk, w_s, m_s, v_s):
    chunks = (32, 48, 64, 16)
    n_chip = N_DEV // 2

    def body(*refs):
        p_refs = refs[0:4]
        small_ref = refs[4]
        w_refs, m_refs, v_refs = refs[5:9], refs[9:13], refs[13:17]
        ws_ref, ms_ref, vs_ref = refs[17:20]
        g_refs, d_refs, nm_refs, nv_refs = refs[20:24], refs[24:28], refs[28:32], refs[32:36]
        gs_ref, ds_ref, nms_ref, nvs_ref = refs[36:40]
        ra_refs, rb_refs, st_refs = refs[40:44], refs[44:48], refs[48:52]
        recv_s_ref = refs[52]
        send_a, recv_a, send_b, recv_b, send_s_sems, recv_s_sems = refs[53:59]
        pos = _mesh_pos()
        x, y, c = pos
        me = _logical(pos)
        sibling = (x, y, 1 - c)

        def chip(j):
            return _flip(x, j & 1), _flip(y, (j >> 1) & 1)

        def to_sibling(w, j):
            return pltpu.make_async_remote_copy(
                src_ref=p_refs[w].at[_logical((*chip(j), 1 - c))], dst_ref=ra_refs[w].at[j],
                send_sem=send_a.at[N_W * j + w], recv_sem=recv_a.at[N_W * j + w],
                device_id=sibling, device_id_type=pl.DeviceIdType.MESH)

        def to_chip(w, j):
            return pltpu.make_async_remote_copy(
                src_ref=st_refs[w].at[j - 1], dst_ref=rb_refs[w].at[j - 1],
                send_sem=send_b.at[N_W * (j - 1) + w], recv_sem=recv_b.at[N_W * (j - 1) + w],
                device_id=(*chip(j), c), device_id_type=pl.DeviceIdType.MESH)

        def tiny(k):
            return pltpu.make_async_remote_copy(
                src_ref=small_ref, dst_ref=recv_s_ref.at[k],
                send_sem=send_s_sems.at[k], recv_sem=recv_s_sems.at[k],
                device_id=_peer(pos, k), device_id_type=pl.DeviceIdType.MESH)

        order = (0, 3, 2, 1)
        for j in (1, 2, 3, 0):
            for w in order:
                to_sibling(w, j).start()
        for k in range(1, N_DEV):
            tiny(k).start()

        for j in (1, 2, 3):
            d = _logical((*chip(j), c))
            for w in order:
                to_sibling(w, j).wait_recv()
                chunk = chunks[w]

                def pair(t, carry, w=w, j=j, d=d, chunk=chunk):
                    rows = pl.ds(pl.multiple_of(t * chunk, chunk), chunk)
                    s = p_refs[w][d, rows, :].astype(F32) + ra_refs[w][j, rows, :].astype(F32)
                    st_refs[w][j - 1, rows, :] = s.astype(BF16)
                    return carry

                lax.fori_loop(0, W_BLOCKS[w][0] // chunk, pair, 0)
                to_chip(w, j).start()

        recv_s_ref[0] = small_ref[...]
        for k in range(1, N_DEV):
            tiny(k).wait_recv()
        acc = recv_s_ref[me]
        for a in range(1, N_DEV):
            acc = acc + recv_s_ref[lax.bitwise_xor(me, a)]
        row = lax.broadcasted_iota(jnp.int32, (R_SMALL, LANES), 0)
        gs = jnp.where(row == 8, acc + pltpu.roll(acc, 64, 1), acc)
        gs = jnp.where(row == ROW_LOSS, jnp.sum(acc, axis=1, keepdims=True) * (0.5 / D_MODEL), gs)
        gs_ref[...] = gs
        ds, nms, nvs = _adamw(ws_ref[...], gs, ms_ref[...], vs_ref[...])
        ds_ref[...] = ds
        nms_ref[...] = nms
        nvs_ref[...] = nvs

        for w in (1, 2, 3, 0):
            to_sibling(w, 0).wait_recv()
            for j in (1, 2, 3):
                to_chip(w, j).wait_recv()
            chunk = chunks[w]

            def step(t, carry, w=w, chunk=chunk):
                rows = pl.ds(pl.multiple_of(t * chunk, chunk), chunk)
                g = p_refs[w][me, rows, :].astype(F32) + ra_refs[w][0, rows, :].astype(F32)
                for j in range(n_chip - 1):
                    g = g + rb_refs[w][j, rows, :].astype(F32)
                d, nm, nv = _adamw(w_refs[w][rows, :], g, m_refs[w][rows, :], v_refs[w][rows, :])
                g_refs[w][rows, :] = g
                d_refs[w][rows, :] = d
                nm_refs[w][rows, :] = nm
                nv_refs[w][rows, :] = nv
                return carry

            lax.fori_loop(0, W_BLOCKS[w][0] // chunk, step, 0)
        for k in range(1, N_DEV):
            tiny(k).wait_send()
        for w in range(N_W):
            for j in range(n_chip):
                to_sibling(w, j).wait_send()
            for j in (1, 2, 3):
                to_chip(w, j).wait_send()

    vm = pl.BlockSpec(memory_space=pltpu.VMEM)
    blk = [jax.ShapeDtypeStruct(shp, F32) for shp in W_BLOCKS]
    small_shape = jax.ShapeDtypeStruct((R_SMALL, LANES), F32)
    return pl.pallas_call(
        body, name="reduce_adamw",
        in_specs=[vm] * 20, out_specs=[vm] * 20,
        out_shape=blk * 4 + [small_shape] * 4,
        scratch_shapes=[pltpu.VMEM((n_chip,) + shp, BF16) for shp in W_BLOCKS]
        + [pltpu.VMEM((n_chip - 1,) + shp, BF16) for shp in W_BLOCKS] * 2
        + [pltpu.VMEM((N_DEV, R_SMALL, LANES), F32),
           pltpu.SemaphoreType.DMA((n_chip * N_W,)), pltpu.SemaphoreType.DMA((n_chip * N_W,)),
           pltpu.SemaphoreType.DMA(((n_chip - 1) * N_W,)), pltpu.SemaphoreType.DMA(((n_chip - 1) * N_W,)),
           pltpu.SemaphoreType.DMA((N_DEV,)), pltpu.SemaphoreType.DMA((N_DEV,))],
        compiler_params=pltpu.CompilerParams(vmem_limit_bytes=VMEM_LIMIT),
    )(*parts, small, *w_blk, *m_blk, *v_blk, w_s, m_s, v_s)


def _pack_small(norm_in, a_q, a_k, b_cq, b_ckv, b_q, b_k):
    def row(v):
        return jnp.pad(v.reshape(1, -1), ((0, 0), (0, LANES - v.size)))
    rows = [norm_in.reshape(8, LANES), row(a_q), row(a_k), b_cq.reshape(3, LANES), b_ckv.reshape(2, LANES),
            row(b_q), row(b_k), jnp.zeros((R_SMALL - 17, LANES), F32)]
    return jnp.concatenate(rows, axis=0)


def _unpack_small(s):
    return (s[0:8].reshape(1, D_MODEL), s[8:9, :64], s[9:10, :64], s[10:13].reshape(1, B_Q_RANK),
            s[13:15].reshape(1, B_KV_RANK), s[15:16, :B_QK_DIM], s[16:17, :B_QK_DIM])


C_IN = N_IN // N_DEV
_RUNS = ((0, 512, O_QA), (512, 576, O_KA), (576, 640, O_KA + 128), (640, 704, O_VA), (704, 768, O_VA + 128),
         (768, 1280, O_GA), (1280, 1664, O_CQ), (1664, 1920, O_CKV), (1920, 1952, O_KR + 64), (1952, 2464, O_GB))


def _in_cols(g_in, lo, hi):
    out = []
    for d in range(N_DEV):
        a, b = max(lo, C_IN * d), min(hi, C_IN * (d + 1))
        if a < b:
            out.append(g_in[d][:, a - C_IN * d:b - C_IN * d])
    return out


def _widen_weights(g_in, g_uq, g_ukv, g_out):
    z64 = jnp.zeros((D_MODEL, 64), BF16)
    z32 = jnp.zeros((D_MODEL, 32), BF16)
    k0, k1 = _in_cols(g_in, 512, 576), _in_cols(g_in, 576, 640)
    v0, v1 = _in_cols(g_in, 640, 704), _in_cols(g_in, 704, 768)
    kr_blk = [z64] + _in_cols(g_in, 1920, 1952) + [z32]
    w_wide = jnp.concatenate(
        _in_cols(g_in, 0, 512) + k0 + k0 + k1 + k1 + v0 + v0 + v1 + v1 + _in_cols(g_in, 768, 1920)
        + kr_blk * B_HEADS + _in_cols(g_in, 1952, 2464), axis=1)
    w_uq = g_uq.reshape(B_Q_RANK, 384)
    wuq = jnp.pad(w_uq.reshape(B_Q_RANK, B_HEADS, B_QK_DIM), ((0, 0), (0, 0), (0, LANES - B_QK_DIM)))
    wuq = wuq.reshape(B_Q_RANK, 512)
    ukv = g_ukv.transpose(1, 0, 2).reshape(B_KV_RANK, B_HEADS, B_NOPE_DIM + B_V_DIM)
    wuk = jnp.pad(ukv[:, :, :B_NOPE_DIM], ((0, 0), (0, 0), (0, LANES - B_NOPE_DIM))).reshape(B_KV_RANK, 512)
    wuv = ukv[:, :, B_NOPE_DIM:].reshape(B_KV_RANK, 512)
    return w_wide, wuq, wuk, wuv, g_out.reshape(D_MODEL, D_MODEL)


def _grad_blocks(dw_wide, dwuq, dwuk, dwuv, dw_out):
    blocks = []
    for d in range(N_DEV):
        pieces = []
        for lo, hi, wide in _RUNS:
            a, b = max(lo, C_IN * d), min(hi, C_IN * (d + 1))
            if a < b:
                pieces.append(dw_wide[:, wide + a - lo:wide + b - lo])
        blocks.append(jnp.concatenate(pieces, axis=1))
    p_in = jnp.stack(blocks).astype(BF16)
    dw_uq = dwuq.reshape(B_Q_RANK, B_HEADS, LANES)[:, :, :B_QK_DIM].reshape(N_DEV, B_Q_RANK // N_DEV, 384)
    dk = dwuk.reshape(B_KV_RANK, B_HEADS, LANES)[:, :, :B_NOPE_DIM]
    dv = dwuv.reshape(B_KV_RANK, B_HEADS, B_V_DIM)
    dw_ukv = jnp.concatenate([dk, dv], axis=2).reshape(B_KV_RANK, N_DEV, 768 // N_DEV).transpose(1, 0, 2)
    return (p_in, dw_uq.astype(BF16), dw_ukv.astype(BF16),
            dw_out.reshape(N_DEV, D_MODEL // N_DEV, D_MODEL).astype(BF16))


def _rope_tables(s_len):
    row = jnp.arange(s_len // GRID_W, dtype=F32)
    col = jnp.arange(GRID_W, dtype=F32)

    def parts(dim):
        half = dim // 2
        inv = 1.0 / (ROPE_THETA ** (jnp.arange(0, half, 2, dtype=F32) / half))
        ar, ac = row[:, None] * inv[None, :], col[:, None] * inv[None, :]
        zr, zc = jnp.zeros_like(ar), jnp.zeros_like(ac)
        cos_c = jnp.concatenate([zc, zc, jnp.cos(ac), jnp.cos(ac)], axis=1)
        cos_r = jnp.concatenate([jnp.cos(ar), jnp.cos(ar), zr, zr], axis=1)
        sin_c = jnp.concatenate([zc, zc, -jnp.sin(ac), jnp.sin(ac)], axis=1)
        sin_r = jnp.concatenate([-jnp.sin(ar), jnp.sin(ar), zr, zr], axis=1)
        return cos_c, cos_r, sin_c, sin_r

    tabs = [jnp.tile(t, (1, 2)) for t in parts(A_HEAD_DIM)]
    for n, t in enumerate(parts(B_ROPE_DIM)):
        lead = jnp.full((t.shape[0], B_NOPE_DIM), 1.0 if n == 0 else 0.0, F32)
        tail = jnp.full((t.shape[0], LANES - B_QK_DIM), 1.0 if n == 0 else 0.0, F32)
        tabs.append(jnp.concatenate([lead, t, tail], axis=1))
    return tuple(tabs)


def kernel(x, norm_in, w_in, a_q_norm, a_k_norm, b_cq_norm, b_ckv_norm, w_uq, w_ukv, b_q_norm, b_k_norm, w_out, loss_target, m_norm_in, m_w_in, m_a_q_norm, m_a_k_norm, m_b_cq_norm, m_b_ckv_norm, m_w_uq, m_w_ukv, m_b_q_norm, m_b_k_norm, m_w_out, v_norm_in, v_w_in, v_a_q_norm, v_a_k_norm, v_b_cq_norm, v_b_ckv_norm, v_w_uq, v_w_ukv, v_b_q_norm, v_b_k_norm, v_w_out):
    s_len = x.shape[1]
    tm = min(256, s_len)
    tq, tk = min(512, s_len), min(2048, s_len)
    ftq, ftk = min(256, s_len), min(256, s_len)
    x2 = x.reshape(s_len, D_MODEL)
    t2 = loss_target.reshape(s_len, D_MODEL)

    w_blk = (w_in[0], w_uq[0], w_ukv[0], w_out[0])
    w_wide, wuq, wuk, wuv, wout = _widen_weights(*_gather_blocks_call(w_blk))

    def dup(v, pad_to=None):
        v = v.reshape(1, -1)
        if pad_to is None:
            return jnp.concatenate([v, v], axis=1)
        return jnp.pad(v, ((0, 0), (0, pad_to - v.shape[1])))

    g64 = jnp.asarray(np.kron(np.eye(2), np.ones((64, 64))), dtype=BF16)
    consts = dict(gin=norm_in, w=w_wide, wuq=wuq, wuk=wuk, wuv=wuv, gq=dup(a_q_norm), gk=dup(a_k_norm),
                  gcq=b_cq_norm, gckv=b_ckv_norm, gqb=dup(b_q_norm, LANES), gkb=dup(b_k_norm, LANES), g64=g64)
    tabs = _rope_tables(s_len)

    qa, ka, va, va_t, ga, qb, kb, vb, vb_t, gb, raw, xnb = _pre_fwd_call(x2, consts, tabs, min(512, s_len))
    oa, lse_a_t = _attn_fwd_t_call(qa, ka, va_t, groups=A_KV_HEADS, sub=4, masked=True, scale=None, tq=ftq, tk=ftk,
                                   name="attn_fwd_a")
    ob, lse_b_t = _attn_fwd_t_call(qb, kb, vb_t, groups=1, sub=B_HEADS, masked=False, scale=None, tq=ftq,
                                   tk=ftk, name="attn_fwd_b")
    dh, doa, dob, dga, dgb, dw_out, loss_row = _out_call(x2, t2, oa, ob, ga, gb, wout, min(512, s_len))

    dqa, dka_t, dva_t = _attn_bwd_q_call(qa, ka, va, doa, oa, lse_a_t, groups=A_KV_HEADS, sub=4, masked=True,
                                         tq=tq, tk=tk, ck=min(512, s_len), name="attn_bwd_a")
    dqb, dkb_t, dvb_t = _attn_bwd_q_call(qb, kb, vb, dob, ob, lse_b_t, groups=2, sub=2, masked=False,
                                         tq=tq, tk=min(4096, s_len), ck=min(256, s_len), name="attn_bwd_b")
    dx, dproj_b, dwuq, dwuk, dwuv, small = _pre_bwd_call(
        x2, raw, dh, dqa, dka_t, dva_t, dga, dqb, dkb_t, dvb_t, dgb, consts, tabs, tm)
    dw_wide = _dw_in_call(xnb, dproj_b, min(1024, s_len), N_WIDE)

    parts = _grad_blocks(dw_wide, dwuq, dwuk, dwuv, dw_out)
    small = jnp.concatenate([small[:ROW_LOSS], loss_row, small[ROW_LOSS + 1:]], axis=0)

    m_blk = (m_w_in[0], m_w_uq[0], m_w_ukv[0], m_w_out[0])
    v_blk = (v_w_in[0], v_w_uq[0], v_w_ukv[0], v_w_out[0])
    w_s = _pack_small(norm_in, a_q_norm, a_k_norm, b_cq_norm, b_ckv_norm, b_q_norm, b_k_norm)
    m_s = _pack_small(m_norm_in, m_a_q_norm, m_a_k_norm, m_b_cq_norm, m_b_ckv_norm, m_b_q_norm, m_b_k_norm)
    v_s = _pack_small(v_norm_in, v_a_q_norm, v_a_k_norm, v_b_cq_norm, v_b_ckv_norm, v_b_q_norm, v_b_k_norm)
    res = _reduce_two_level_call(parts, small, w_blk, m_blk, v_blk, w_s, m_s, v_s)

    def leaves(blocks, sm):
        wi, uq, ukv, wo = [b[None] for b in blocks]
        n_in, aq, ak, bcq, bckv, bq, bk = _unpack_small(sm)
        return [n_in, wi, aq, ak, bcq, bckv, uq, ukv, bq, bk, wo]

    g_s = res[16]
    loss = g_s[ROW_LOSS, 0]
    grad_x = dx.reshape(1, s_len, D_MODEL)
    return (loss, grad_x, *leaves(res[0:4], res[16]), *leaves(res[4:8], res[17]), *leaves(res[8:12], res[18]),
            *leaves(res[12:16], res[19]))
```

```python
import functools

import numpy as np
import jax
import jax.numpy as jnp
from jax import lax
from jax.experimental import pallas as pl
from jax.experimental.pallas import tpu as pltpu

F32 = jnp.float32
BF16 = jnp.bfloat16

D_MODEL = 1024
GRID_W = 64
ROPE_THETA = 10000.0
EPS = 1e-6
A_HEAD_DIM = 64
A_HEADS = 8
A_KV_HEADS = 2
B_HEADS = 4
B_NOPE_DIM = 64
B_ROPE_DIM = 32
B_QK_DIM = 96
B_V_DIM = 128
B_Q_RANK = 384
B_KV_RANK = 256
N_IN = 2464
N_DEV = 8

ADAM_LR = 0.001
ADAM_B1 = 0.9
ADAM_B2 = 0.999
ADAM_EPS = 1e-08
ADAM_WD = 0.01
ADAM_STEP = 10

QA_SCALE = 0.125
QB_SCALE = 1.0 / float(np.sqrt(B_QK_DIM))

LANES = 128
O_QA, O_KA, O_VA, O_GA, O_CQ, O_CKV, O_KR, O_GB, N_WIDE = 0, 512, 768, 1024, 1536, 1920, 2176, 2688, 3200
R_QA, R_KA, R_CQ, R_CKV, R_KR, R_WIDTH = 0, 512, 768, 1152, 1408, 1920

R_SMALL = 24
ROW_LOSS = 17

VMEM_LIMIT = 56 * 1024 * 1024

NT = (((1,), (1,)), ((), ()))
TN = (((0,), (0,)), ((), ()))


def _dot(a, b):
    return jnp.dot(a, b, preferred_element_type=F32)


def _dot_nt(a, b):
    return lax.dot_general(a, b, NT, preferred_element_type=F32)


def _dot_tn(a, b):
    return lax.dot_general(a, b, TN, preferred_element_type=F32)


def _params(sem=None):
    return pltpu.CompilerParams(dimension_semantics=sem, vmem_limit_bytes=VMEM_LIMIT)


def _full(shape):
    nd = len(shape)
    return pl.BlockSpec(shape, lambda *_: (0,) * nd)


def _swap_sel(rows, shift):
    lane = lax.broadcasted_iota(jnp.int32, (rows, LANES), 1)
    return pltpu.roll(lane, shift, 1) == (lane ^ shift)


def _swap(x, shift, sel):
    return jnp.where(sel, pltpu.roll(x, shift, 1), pltpu.roll(x, LANES - shift, 1))


def _group_sum64(x, g64):
    hi = x.astype(BF16)
    lo = (x - hi.astype(F32)).astype(BF16)
    return _dot(hi, g64) + _dot(lo, g64)


def _row_sum(x):
    return jnp.sum(x, axis=-1, keepdims=True)


def _col_fwd(xs, msum, denom, gain, cos, sin, shift, sel):
    r = lax.rsqrt(msum(xs * xs) * (1.0 / denom) + EPS)
    xh = xs * r
    n = xh * gain
    return n * cos + _swap(n, shift, sel) * sin, xh, r


def _col_bwd(d_out, xh, r, msum, denom, gain, cos, sin, shift, sel):
    dn = d_out * cos + _swap(d_out * sin, shift, sel)
    dgain = jnp.sum(dn * xh, axis=0, keepdims=True)
    dxh = dn * gain
    dx = r * (dxh - xh * (msum(dxh * xh) * (1.0 / denom)))
    return dx, dgain


def _rms_fwd(x, gain):
    r = lax.rsqrt(jnp.mean(x * x, axis=-1, keepdims=True) + EPS)
    xh = x * r
    return xh * gain, xh, r


def _rms_bwd(dy, xh, r, gain):
    dgain = jnp.sum(dy * xh, axis=0, keepdims=True)
    dxh = dy * gain
    dx = r * (dxh - xh * jnp.mean(dxh * xh, axis=-1, keepdims=True))
    return dx, dgain


def _pre_forward(x, gin, w, wuq, wuk, wuv, gq, gk, gcq, gckv, gqb, gkb, ca, sa, cb, sb, g64, tm, raw=None):
    sel16 = _swap_sel(tm, 16)
    sel8 = _swap_sel(tm, 8)
    xn, xh0, r0 = _rms_fwd(x, gin)
    xnb = xn.astype(BF16)
    proj = None
    if raw is None:
        proj = _dot(xnb, w)
        raw = jnp.concatenate([proj[:, O_QA:O_QA + 512], proj[:, O_KA:O_KA + 256], proj[:, O_CQ:O_CQ + B_Q_RANK],
                               proj[:, O_CKV:O_CKV + B_KV_RANK], proj[:, O_KR:O_KR + 512]], axis=1)
    gs64 = functools.partial(_group_sum64, g64=g64)
    qa = [_col_fwd(raw[:, R_QA + LANES * s:R_QA + LANES * (s + 1)], gs64, 64.0, gq, ca, sa, 16, sel16)
          for s in range(4)]
    ka = [_col_fwd(raw[:, R_KA + LANES * s:R_KA + LANES * (s + 1)], _row_sum, 128.0, gk, ca, sa, 16, sel16)
          for s in range(2)]
    cq, cqh, rcq = _rms_fwd(raw[:, R_CQ:R_CQ + B_Q_RANK], gcq)
    cqb = cq.astype(BF16)
    qb_raw = _dot(cqb, wuq)
    qb = [_col_fwd(qb_raw[:, LANES * h:LANES * (h + 1)], _row_sum, float(B_QK_DIM), gqb, cb, sb, 8, sel8)
          for h in range(B_HEADS)]
    ckv, ckvh, rckv = _rms_fwd(raw[:, R_CKV:R_CKV + B_KV_RANK], gckv)
    ckvb = ckv.astype(BF16)
    kb_raw = _dot(ckvb, wuk) + raw[:, R_KR:R_KR + 512]
    vb = _dot(ckvb, wuv)
    kb = [_col_fwd(kb_raw[:, LANES * h:LANES * (h + 1)], _row_sum, float(B_QK_DIM), gkb, cb, sb, 8, sel8)
          for h in range(B_HEADS)]
    return dict(xh0=xh0, r0=r0, xnb=xnb, proj=proj, raw=raw, qa=qa, ka=ka, cqh=cqh, rcq=rcq, cqb=cqb, qb=qb,
                ckvh=ckvh, rckv=rckv, ckvb=ckvb, kb=kb, vb=vb, sel16=sel16, sel8=sel8, gs64=gs64)


def _rope_tiles(tab_refs, i, tm):
    per_tile = tm // GRID_W
    out = []
    for t in range(4):
        col_ref, row_ref = tab_refs[2 * t], tab_refs[2 * t + 1]
        col = col_ref[...]
        out.append(jnp.concatenate([col + row_ref[pl.ds(i * per_tile + b, 1), :] for b in range(per_tile)], axis=0))
    return out


_PRE_IN_NAMES = ("gin", "w", "wuq", "wuk", "wuv", "gq", "gk", "gcq", "gckv", "gqb", "gkb", "g64")


def _pre_const_specs(consts):
    return [_full(consts[n].shape) for n in _PRE_IN_NAMES]


def _pre_fwd_call(x, consts, tabs, tm):
    s_len = x.shape[0]
    ts = min(256, tm)

    def body(x_ref, *refs):
        tab_refs, refs = refs[:8], refs[8:]
        (gin_ref, w_ref, wuq_ref, wuk_ref, wuv_ref, gq_ref, gk_ref, gcq_ref, gckv_ref, gqb_ref, gkb_ref, g64_ref,
         qa_ref, ka_ref, va_ref, vat_ref, ga_ref, qb_ref, kb_ref, vb_ref, vbt_ref, gb_ref, raw_ref, xnb_ref) = refs
        for part in range(tm // ts):
            r = slice(ts * part, ts * (part + 1))
            ca, sa, cb, sb = _rope_tiles(tab_refs, pl.program_id(0) * (tm // ts) + part, ts)
            f = _pre_forward(x_ref[r, :], gin_ref[...], w_ref[...], wuq_ref[...], wuk_ref[...], wuv_ref[...],
                             gq_ref[...], gk_ref[...], gcq_ref[...], gckv_ref[...], gqb_ref[...], gkb_ref[...],
                             ca, sa, cb, sb, g64_ref[...], ts)
            proj = f["proj"]
            raw_ref[r, :] = f["raw"]
            xnb_ref[r, :] = f["xnb"]
            for s in range(4):
                qa_ref[r, LANES * s:LANES * (s + 1)] = (f["qa"][s][0] * QA_SCALE).astype(BF16)
            for s in range(2):
                ka_ref[r, LANES * s:LANES * (s + 1)] = f["ka"][s][0].astype(BF16)
            va = proj[:, O_VA:O_VA + 256]
            va_ref[r, :] = va.astype(BF16)
            vat_ref[:, r] = va.T.astype(BF16)
            ga_ref[r, :] = proj[:, O_GA:O_GA + 512]
            for h in range(B_HEADS):
                qb_ref[r, LANES * h:LANES * (h + 1)] = (f["qb"][h][0] * QB_SCALE).astype(BF16)
                kb_ref[r, LANES * h:LANES * (h + 1)] = f["kb"][h][0].astype(BF16)
            vb_ref[r, :] = f["vb"].astype(BF16)
            vbt_ref[:, r] = f["vb"].T.astype(BF16)
            gb_ref[r, :] = proj[:, O_GB:O_GB + 512]

    def rows(width):
        return pl.BlockSpec((tm, width), lambda i: (i, 0))

    def cols(height):
        return pl.BlockSpec((height, tm), lambda i: (0, i))

    outs = [((s_len, 512), BF16, rows(512)), ((s_len, 256), BF16, rows(256)), ((s_len, 256), BF16, rows(256)),
            ((256, s_len), BF16, cols(256)), ((s_len, 512), F32, rows(512)), ((s_len, 512), BF16, rows(512)),
            ((s_len, 512), BF16, rows(512)), ((s_len, 512), BF16, rows(512)), ((512, s_len), BF16, cols(512)),
            ((s_len, 512), F32, rows(512)), ((s_len, R_WIDTH), F32, rows(R_WIDTH)),
            ((s_len, D_MODEL), BF16, rows(D_MODEL))]
    return pl.pallas_call(
        body, name="pre_fwd", grid=(s_len // tm,),
        in_specs=[rows(D_MODEL)] + [_full(t.shape) for t in tabs] + _pre_const_specs(consts),
        out_specs=[sp for _, _, sp in outs],
        out_shape=[jax.ShapeDtypeStruct(sh, dt) for sh, dt, _ in outs],
        compiler_params=_params(("arbitrary",)),
    )(x, *tabs, *[consts[n] for n in _PRE_IN_NAMES])


def _head_masks(rows):
    lane = lax.broadcasted_iota(jnp.int32, (rows, LANES), 1)
    return lane < 64, lane >= 64


def _row_fold(x, op):
    return op(x.reshape(x.shape[0] // 8, 8, x.shape[1]), axis=0)


def _attn_fwd_t_call(q, k, vt, *, groups, sub, masked, scale, tq, tk, name):
    s_len = q.shape[0]
    qw = LANES * (sub // 2 if masked else sub)
    kvw = LANES if masked else LANES * sub
    n_c = s_len // tk
    kv_mode = pl.Buffered(1) if groups == 1 else None

    def body(q_ref, qn_ref, k_ref, vt_ref, o_ref, lse_ref, s_sc, mx_sc):
        keep = _head_masks(tq) if masked else None

        def kv_of(hh):
            return slice(0, LANES) if masked else slice(LANES * hh, LANES * (hh + 1))

        def q_of(ref, hh):
            if not masked:
                return ref[:, LANES * hh:LANES * (hh + 1)]
            qp = ref[:, LANES * (hh // 2):LANES * (hh // 2 + 1)]
            return jnp.where(keep[hh % 2], qp, jnp.zeros_like(qp))

        def scores(hh, qm, c, mx):
            s_t = _dot_nt(k_ref[tk * c:tk * (c + 1), kv_of(hh)], qm)
            if scale is not None:
                s_t = s_t * scale
            s_sc[hh % 2, c] = s_t
            return jnp.maximum(mx, _row_fold(s_t, jnp.max))

        neg = jnp.full((8, tq), -jnp.inf, F32)

        @pl.when(pl.program_id(1) == 0)
        def _():
            qm0 = q_of(q_ref, 0)
            mx0 = neg
            for c in range(n_c):
                mx0 = scores(0, qm0, c, mx0)
            mx_sc[...] = mx0

        mx_next = mx_sc[...]
        outs = []
        for hh in range(sub):
            m = jnp.max(mx_next, axis=0, keepdims=True)
            nxt = (hh + 1) % sub
            qm_next = q_of(q_ref if hh + 1 < sub else qn_ref, nxt)
            mx_next = neg
            lsum = jnp.zeros((8, tq), F32)
            acc = jnp.zeros((LANES, tq), F32)
            for c in range(n_c):
                p_t = jnp.exp(s_sc[hh % 2, c] - m)
                lsum = lsum + _row_fold(p_t, jnp.sum)
                mx_next = scores(nxt, qm_next, c, mx_next)
                acc = acc + _dot(vt_ref[kv_of(hh), tk * c:tk * (c + 1)], p_t.astype(BF16))
            l = jnp.sum(lsum, axis=0, keepdims=True)
            outs.append((acc / l).T)
            lse_ref[hh] = m + jnp.log(l)
        mx_sc[...] = mx_next
        if masked:
            for pr in range(sub // 2):
                o_ref[:, LANES * pr:LANES * (pr + 1)] = jnp.where(keep[0], outs[2 * pr], outs[2 * pr + 1])
        else:
            for hh in range(sub):
                o_ref[:, LANES * hh:LANES * (hh + 1)] = outs[hh]

    n_q = s_len // tq
    return pl.pallas_call(
        body, name=name, grid=(groups, n_q),
        in_specs=[pl.BlockSpec((tq, qw), lambda g, i: (i, g)),
                  pl.BlockSpec((tq, qw), lambda g, i: (jnp.minimum(i + 1, n_q - 1), g)),
                  pl.BlockSpec((s_len, kvw), lambda g, i: (0, g), pipeline_mode=kv_mode),
                  pl.BlockSpec((kvw, s_len), lambda g, i: (g, 0), pipeline_mode=kv_mode)],
        out_specs=[pl.BlockSpec((tq, qw), lambda g, i: (i, g)),
                   pl.BlockSpec((sub, 1, tq), lambda g, i: (g, 0, i))],
        out_shape=[jax.ShapeDtypeStruct((s_len, groups * qw), F32),
                   jax.ShapeDtypeStruct((groups * sub, 1, s_len), F32)],
        scratch_shapes=[pltpu.VMEM((2, n_c, tk, tq), F32), pltpu.VMEM((8, tq), F32)],
        compiler_params=_params(("arbitrary", "arbitrary")),
    )(q, q, k, vt)


def _attn_bwd_q_call(q, k, v, do, o, lse_t, *, groups, sub, masked, tq, tk, ck, name):
    s_len = q.shape[0]
    qw = LANES * (sub // 2 if masked else sub)
    kvw = LANES if masked else LANES * sub
    n_c = tk // ck

    def body(q_ref, k_ref, v_ref, do_ref, o_ref, lse_ref, dq_ref, dkt_ref, dvt_ref):
        j = pl.program_id(1)
        i = pl.program_id(2)

        @pl.when((j == 0) & (i == 0))
        def _():
            dq_ref[...] = jnp.zeros(dq_ref.shape, F32)

        @pl.when(i == 0)
        def _():
            dkt_ref[...] = jnp.zeros(dkt_ref.shape, F32)
            dvt_ref[...] = jnp.zeros(dvt_ref.shape, F32)

        lkeep = _head_masks(tq) if masked else None
        heads = []
        for hh in range(sub):
            if masked:
                cols = slice(LANES * (hh // 2), LANES * (hh // 2 + 1))
                kv = slice(0, LANES)
                qp, dop = q_ref[:, cols], do_ref[:, cols]
                qm = jnp.where(lkeep[hh % 2], qp, jnp.zeros_like(qp))
                dom = jnp.where(lkeep[hh % 2], dop, jnp.zeros_like(dop))
            else:
                cols = kv = slice(LANES * hh, LANES * (hh + 1))
                qm, dom = q_ref[:, cols], do_ref[:, cols]
            delta = jnp.sum(dom.astype(F32) * o_ref[:, cols], axis=1, keepdims=True)
            lse = jnp.broadcast_to(lse_ref[hh], (LANES, tq)).T[:, 0:1]
            heads.append((cols, kv, qm, dom, qm.T, dom.T, delta, lse))

        def products(hh, c):
            _, kv, qm, dom, _, _, _, _ = heads[hh]
            return _dot_nt(qm, k_ref[ck * c:ck * (c + 1), kv]), _dot_nt(dom, v_ref[ck * c:ck * (c + 1), kv])

        items = [(hh, c) for hh in range(sub) for c in range(n_c)]
        dq_acc = [jnp.zeros((tq, LANES), F32) for _ in range(sub)]
        nxt = products(*items[0])
        for n, (hh, c) in enumerate(items):
            s, dp = nxt
            if n + 1 < len(items):
                nxt = products(*items[n + 1])
            _, kv, qm, dom, qmt, domt, delta, lse = heads[hh]
            p = jnp.exp(s - lse)
            ds = p * (dp - delta)
            p_b = p.astype(BF16)
            ds_b = ds.astype(BF16)
            kcols = slice(ck * c, ck * (c + 1))
            dvt_ref[kv, kcols] += _dot(domt, p_b)
            dkt_ref[kv, kcols] += _dot(qmt, ds_b)
            dq_acc[hh] = dq_acc[hh] + _dot(ds_b, k_ref[kcols, kv])
        rows = pl.ds(pl.multiple_of(i * tq, tq), tq)
        if masked:
            for pr in range(sub // 2):
                dq_ref[rows, LANES * pr:LANES * (pr + 1)] += jnp.where(lkeep[0], dq_acc[2 * pr], dq_acc[2 * pr + 1])
        else:
            for hh in range(sub):
                dq_ref[rows, LANES * hh:LANES * (hh + 1)] += dq_acc[hh]

    return pl.pallas_call(
        body, name=name, grid=(groups, s_len // tk, s_len // tq),
        in_specs=[pl.BlockSpec((tq, qw), lambda g, j, i: (i, g)),
                  pl.BlockSpec((tk, kvw), lambda g, j, i: (j, g)),
                  pl.BlockSpec((tk, kvw), lambda g, j, i: (j, g)),
                  pl.BlockSpec((tq, qw), lambda g, j, i: (i, g)),
                  pl.BlockSpec((tq, qw), lambda g, j, i: (i, g)),
                  pl.BlockSpec((sub, 1, tq), lambda g, j, i: (g, 0, i))],
        out_specs=[pl.BlockSpec((s_len, qw), lambda g, j, i: (0, g)),
                   pl.BlockSpec((kvw, tk), lambda g, j, i: (g, j)),
                   pl.BlockSpec((kvw, tk), lambda g, j, i: (g, j))],
        out_shape=[jax.ShapeDtypeStruct((s_len, groups * qw), F32),
                   jax.ShapeDtypeStruct((groups * kvw, s_len), F32),
                   jax.ShapeDtypeStruct((groups * kvw, s_len), F32)],
        compiler_params=_params(("arbitrary", "arbitrary", "arbitrary")),
    )(q, k, v, do, o, lse_t)


def _silu_parts(g):
    sig = 1.0 / (1.0 + jnp.exp(-g))
    return g * sig, sig * (1.0 + g * (1.0 - sig))


def _out_call(x, target, oa, ob, ga, gb, wout, tm):
    s_len = x.shape[0]
    n_t = s_len // tm

    def body(x_ref, t_ref, oa_ref, ob_ref, ga_ref, gb_ref, w_ref,
             dh_ref, doa_ref, dob_ref, dga_ref, dgb_ref, dwb_ref, loss_ref, dw_ref):
        i = pl.program_id(0)

        @pl.when(i == 0)
        def _():
            dw_ref[...] = jnp.zeros(dw_ref.shape, F32)
            loss_ref[...] = jnp.zeros(loss_ref.shape, F32)

        oa_v, ob_v = oa_ref[...], ob_ref[...]
        silu_a, dsilu_a = _silu_parts(ga_ref[...])
        silu_b, dsilu_b = _silu_parts(gb_ref[...])
        ya = (oa_v * silu_a).astype(BF16)
        yb = (ob_v * silu_b).astype(BF16)
        h = x_ref[...] + _dot(ya, w_ref[0:512, :]) + _dot(yb, w_ref[512:1024, :])
        err = h - t_ref[...]
        part = jnp.sum(err * err, axis=0, keepdims=True)
        acc = part[:, 0:LANES]
        for c in range(1, D_MODEL // LANES):
            acc = acc + part[:, LANES * c:LANES * (c + 1)]
        loss_ref[...] += acc
        dh = err * (1.0 / D_MODEL)
        dh_ref[...] = dh
        dhb = dh.astype(BF16)
        dya = _dot_nt(dhb, w_ref[0:512, :])
        dyb = _dot_nt(dhb, w_ref[512:1024, :])
        doa = dya * silu_a
        dob = dyb * silu_b
        doa_ref[...] = doa.astype(BF16)
        dob_ref[...] = dob.astype(BF16)
        dga_ref[...] = dya * oa_v * dsilu_a
        dgb_ref[...] = dyb * ob_v * dsilu_b
        dw_ref[0:512, :] += _dot_tn(ya, dhb)
        dw_ref[512:1024, :] += _dot_tn(yb, dhb)

        @pl.when(i == n_t - 1)
        def _():
            dwb_ref[...] = dw_ref[...].astype(BF16)

    def rows(width):
        return pl.BlockSpec((tm, width), lambda i: (i, 0))

    outs = [(D_MODEL, F32), (512, BF16), (512, BF16), (512, F32), (512, F32)]
    return pl.pallas_call(
        body, name="out_fwd", grid=(n_t,),
        in_specs=[rows(D_MODEL), rows(D_MODEL), rows(512), rows(512), rows(512), rows(512),
                  _full((D_MODEL, D_MODEL))],
        out_specs=[rows(wd) for wd, _ in outs] + [_full((D_MODEL, D_MODEL)), _full((1, LANES))],
        out_shape=[jax.ShapeDtypeStruct((s_len, wd), dt) for wd, dt in outs]
        + [jax.ShapeDtypeStruct((D_MODEL, D_MODEL), BF16), jax.ShapeDtypeStruct((1, LANES), F32)],
        scratch_shapes=[pltpu.VMEM((D_MODEL, D_MODEL), F32)],
        compiler_params=_params(("arbitrary",)),
    )(x, target, oa, ob, ga, gb, wout)


def _pre_bwd_call(x, raw, dh, dqa, dka, dva, dga, dqb, dkb, dvb, dgb, consts, tabs, tm):
    s_len = x.shape[0]
    ts = min(256, tm)

    def body(x_ref, raw_ref, dh_ref, dqa_ref, dkat_ref, dvat_ref, dga_ref, dqb_ref, dkbt_ref, dvbt_ref, dgb_ref,
             *refs):
        tab_refs, refs = refs[:8], refs[8:]
        (gin_ref, w_ref, wuq_ref, wuk_ref, wuv_ref, gq_ref, gk_ref, gcq_ref, gckv_ref, gqb_ref, gkb_ref, g64_ref,
         dx_ref, dproj_ref, dwuq_ref, dwuk_ref, dwuv_ref, small_ref) = refs
        i = pl.program_id(0)

        @pl.when(i == 0)
        def _():
            dwuq_ref[...] = jnp.zeros(dwuq_ref.shape, F32)
            dwuk_ref[...] = jnp.zeros(dwuk_ref.shape, F32)
            dwuv_ref[...] = jnp.zeros(dwuv_ref.shape, F32)
            small_ref[...] = jnp.zeros(small_ref.shape, F32)

        gin, gq, gk = gin_ref[...], gq_ref[...], gk_ref[...]
        gcq, gckv, gqb, gkb = gcq_ref[...], gckv_ref[...], gqb_ref[...], gkb_ref[...]
        w, wuq, wuk, wuv = w_ref[...], wuq_ref[...], wuk_ref[...], wuv_ref[...]
        for part in range(tm // ts):
            r_ = slice(ts * part, ts * (part + 1))
            dka_v, dva_v = dkat_ref[:, r_].T, dvat_ref[:, r_].T
            dkb_v, dvb_v = dkbt_ref[:, r_].T, dvbt_ref[:, r_].T
            ca, sa, cb, sb = _rope_tiles(tab_refs, i * (tm // ts) + part, ts)
            f = _pre_forward(x_ref[r_, :], gin, w, wuq, wuk, wuv, gq, gk, gcq, gckv, gqb, gkb,
                             ca, sa, cb, sb, g64_ref[...], ts, raw=raw_ref[r_, :])
            sel16, sel8, gs64 = f["sel16"], f["sel8"], f["gs64"]
            lane = lax.broadcasted_iota(jnp.int32, (ts, LANES), 1)
            low = lane < 64
            zero = jnp.zeros((ts, LANES), F32)
            pieces = []

            dgq = jnp.zeros((1, LANES), F32)
            for s in range(4):
                _, xh, r = f["qa"][s]
                d = dqa_ref[r_, LANES * s:LANES * (s + 1)] * QA_SCALE
                dx, dg = _col_bwd(d, xh, r, gs64, 64.0, gq, ca, sa, 16, sel16)
                pieces.append(dx)
                dgq = dgq + dg
            dgk = jnp.zeros((1, LANES), F32)
            for s in range(2):
                _, xh, r = f["ka"][s]
                d = dka_v[:, LANES * s:LANES * (s + 1)]
                d = d + pltpu.roll(d, 64, 1)
                dx, dg = _col_bwd(d, xh, r, _row_sum, 128.0, gk, ca, sa, 16, sel16)
                pieces.append(jnp.where(low, dx, zero))
                dgk = dgk + dg
            for s in range(2):
                d = dva_v[:, LANES * s:LANES * (s + 1)]
                d = d + pltpu.roll(d, 64, 1)
                pieces.append(jnp.where(low, d, zero))
            pieces.append(dga_ref[r_, :])

            dgqb = jnp.zeros((1, LANES), F32)
            dq_cols = []
            for h in range(B_HEADS):
                _, xh, r = f["qb"][h]
                dx, dg = _col_bwd(dqb_ref[r_, LANES * h:LANES * (h + 1)] * QB_SCALE, xh, r, _row_sum,
                                  float(B_QK_DIM), gqb, cb, sb, 8, sel8)
                dq_cols.append(dx)
                dgqb = dgqb + dg
            dqr_b = jnp.concatenate(dq_cols, axis=1).astype(BF16)
            dwuq_ref[...] += _dot_tn(f["cqb"], dqr_b)
            dcq_raw, dgcq = _rms_bwd(_dot_nt(dqr_b, wuq), f["cqh"], f["rcq"], gcq)
            pieces.append(dcq_raw)

            dgkb = jnp.zeros((1, LANES), F32)
            dk_cols = []
            dkr = zero
            for h in range(B_HEADS):
                _, xh, r = f["kb"][h]
                dx, dg = _col_bwd(dkb_v[:, LANES * h:LANES * (h + 1)], xh, r, _row_sum, float(B_QK_DIM),
                                  gkb, cb, sb, 8, sel8)
                dk_cols.append(dx)
                dkr = dkr + dx
                dgkb = dgkb + dg
            dkr_b = jnp.concatenate(dk_cols, axis=1).astype(BF16)
            dvb_b = dvb_v.astype(BF16)
            dwuk_ref[...] += _dot_tn(f["ckvb"], dkr_b)
            dwuv_ref[...] += _dot_tn(f["ckvb"], dvb_b)
            dckv = _dot_nt(dkr_b, wuk) + _dot_nt(dvb_b, wuv)
            dckv_raw, dgckv = _rms_bwd(dckv, f["ckvh"], f["rckv"], gckv)
            pieces.append(dckv_raw)
            pieces.append(jnp.where((lane >= B_NOPE_DIM) & (lane < B_QK_DIM), dkr, zero))
            pieces += [zero, zero, zero]
            pieces.append(dgb_ref[r_, :])

            dproj_b = jnp.concatenate(pieces, axis=1).astype(BF16)
            dproj_ref[r_, :] = dproj_b
            dxn = _dot_nt(dproj_b, w)
            dx, dgin = _rms_bwd(dxn, f["xh0"], f["r0"], gin)
            dx_ref[r_, :] = dx + dh_ref[r_, :]

            for c in range(D_MODEL // LANES):
                small_ref[c:c + 1, :] += dgin[:, LANES * c:LANES * (c + 1)]
            small_ref[8:9, :] += dgq
            small_ref[9:10, :] += dgk
            for c in range(3):
                small_ref[10 + c:11 + c, :] += dgcq[:, LANES * c:LANES * (c + 1)]
            for c in range(2):
                small_ref[13 + c:14 + c, :] += dgckv[:, LANES * c:LANES * (c + 1)]
            small_ref[15:16, :] += dgqb
            small_ref[16:17, :] += dgkb

    def rows(width):
        return pl.BlockSpec((tm, width), lambda i: (i, 0))

    def cols(height):
        return pl.BlockSpec((height, tm), lambda i: (0, i))

    return pl.pallas_call(
        body, name="pre_bwd", grid=(s_len // tm,),
        in_specs=[rows(D_MODEL), rows(R_WIDTH), rows(D_MODEL), rows(512), cols(256), cols(256), rows(512), rows(512),
                  cols(512), cols(512), rows(512)] + [_full(t.shape) for t in tabs] + _pre_const_specs(consts),
        out_specs=[rows(D_MODEL), rows(N_WIDE), _full((B_Q_RANK, 512)), _full((B_KV_RANK, 512)),
                   _full((B_KV_RANK, 512)), _full((R_SMALL, LANES))],
        out_shape=[jax.ShapeDtypeStruct((s_len, D_MODEL), F32), jax.ShapeDtypeStruct((s_len, N_WIDE), BF16),
                   jax.ShapeDtypeStruct((B_Q_RANK, 512), F32),
                   jax.ShapeDtypeStruct((B_KV_RANK, 512), F32), jax.ShapeDtypeStruct((B_KV_RANK, 512), F32),
                   jax.ShapeDtypeStruct((R_SMALL, LANES), F32)],
        compiler_params=_params(("arbitrary",)),
    )(x, raw, dh, dqa, dka, dva, dga, dqb, dkb, dvb, dgb, *tabs, *[consts[n] for n in _PRE_IN_NAMES])


def _dw_in_call(xnb, dproj_b, tt, tn):
    s_len = xnb.shape[0]
    n_t = s_len // tt

    def body(a_ref, b_ref, o_ref, acc_ref):
        t = pl.program_id(1)

        @pl.when(t == 0)
        def _():
            acc_ref[...] = jnp.zeros(acc_ref.shape, F32)

        acc_ref[...] += _dot_tn(a_ref[...], b_ref[...])

        @pl.when(t == n_t - 1)
        def _():
            o_ref[...] = acc_ref[...].astype(BF16)

    return pl.pallas_call(
        body, name="dw_in", grid=(N_WIDE // tn, n_t),
        in_specs=[pl.BlockSpec((tt, D_MODEL), lambda n, t: (t, 0)), pl.BlockSpec((tt, tn), lambda n, t: (t, n))],
        out_specs=pl.BlockSpec((D_MODEL, tn), lambda n, t: (0, n)),
        out_shape=jax.ShapeDtypeStruct((D_MODEL, N_WIDE), BF16),
        scratch_shapes=[pltpu.VMEM((D_MODEL, tn), F32)],
        compiler_params=_params(("arbitrary", "arbitrary")),
    )(xnb, dproj_b)


def _mesh_pos():
    return lax.axis_index("x"), lax.axis_index("y"), lax.axis_index("c")


def _flip(v, bit):
    return 1 - v if bit else v


def _peer(pos, k):
    x, y, c = pos
    return _flip(x, (k >> 2) & 1), _flip(y, (k >> 1) & 1), _flip(c, k & 1)


def _logical(p):
    return 4 * p[0] + 2 * p[1] + p[2]


def _adamw(w, g, m, v):
    m = ADAM_B1 * m + (1.0 - ADAM_B1) * g
    v = ADAM_B2 * v + (1.0 - ADAM_B2) * (g * g)
    m_hat = m / (1.0 - ADAM_B1 ** ADAM_STEP)
    v_hat = v / (1.0 - ADAM_B2 ** ADAM_STEP)
    delta = -ADAM_LR * (m_hat / (jnp.sqrt(v_hat) + ADAM_EPS) + ADAM_WD * w)
    return delta, m, v


W_BLOCKS = ((D_MODEL, N_IN // N_DEV), (B_Q_RANK // N_DEV, 384), (B_KV_RANK, 768 // N_DEV), (D_MODEL // N_DEV, D_MODEL))
N_W = len(W_BLOCKS)


def _gather_blocks_call(blocks):
    def body(*refs):
        x_refs, out_refs, xb_refs = refs[0:N_W], refs[N_W:2 * N_W], refs[2 * N_W:3 * N_W]
        send_sems, recv_sems, local_sems = refs[3 * N_W:]
        x, y, c = _mesh_pos()
        me, sibling = (x, y, c), (x, y, 1 - c)
        chips = [(1 - x, y), (x, 1 - y), (1 - x, 1 - y)]
        for w in range(N_W):
            xb_refs[w][...] = x_refs[w][...].astype(BF16)

        def slot(w, p):
            return out_refs[w].at[_logical(p)]

        def copy(w, k, block, to, src=None):
            return pltpu.make_async_remote_copy(
                src_ref=slot(w, block) if src is None else src, dst_ref=slot(w, block),
                send_sem=send_sems.at[N_W * k + w], recv_sem=recv_sems.at[N_W * k + w],
                device_id=to, device_id_type=pl.DeviceIdType.MESH)

        mine = [pltpu.make_async_copy(xb_refs[w], slot(w, me), local_sems.at[w]) for w in range(N_W)]
        for cp in mine:
            cp.start()
        first = [copy(w, 0, me, sibling, src=xb_refs[w]) for w in range(N_W)]
        first += [copy(w, 1 + j, me, (*chip, c), src=xb_refs[w]) for j, chip in enumerate(chips) for w in range(N_W)]
        for cp in first:
            cp.start()
        passed = []
        for j, chip in enumerate(chips):
            for w in range(N_W):
                copy(w, 1 + j, (*chip, c), me).wait_recv()
                fwd = copy(w, 4 + j, (*chip, c), sibling)
                fwd.start()
                passed.append(fwd)
        for w in range(N_W):
            copy(w, 0, sibling, me).wait_recv()
        for j, chip in enumerate(chips):
            for w in range(N_W):
                copy(w, 4 + j, (*chip, 1 - c), me).wait_recv()
        for cp in first + passed:
            cp.wait_send()
        for cp in mine:
            cp.wait()

    vm = pl.BlockSpec(memory_space=pltpu.VMEM)
    return pl.pallas_call(
        body, name="gather_weights",
        out_shape=[jax.ShapeDtypeStruct((N_DEV,) + shp, BF16) for shp in W_BLOCKS],
        in_specs=[vm] * N_W, out_specs=[vm] * N_W,
        scratch_shapes=[pltpu.VMEM(shp, BF16) for shp in W_BLOCKS]
        + [pltpu.SemaphoreType.DMA((7 * N_W,)), pltpu.SemaphoreType.DMA((7 * N_W,)), pltpu.SemaphoreType.DMA((N_W,))],
        compiler_params=pltpu.CompilerParams(vmem_limit_bytes=VMEM_LIMIT),
    )(*blocks)


def _reduce_two_level_call(parts, small, w_blk, m_blk, v_blk, w_s, m_s, v_s):
    chunks = (32, 48, 64, 16)
    n_chip = N_DEV // 2

    def body(*refs):
        p_refs = refs[0:4]
        small_ref = refs[4]
        w_refs, m_refs, v_refs = refs[5:9], refs[9:13], refs[13:17]
        ws_ref, ms_ref, vs_ref = refs[17:20]
        g_refs, d_refs, nm_refs, nv_refs = refs[20:24], refs[24:28], refs[28:32], refs[32:36]
        gs_ref, ds_ref, nms_ref, nvs_ref = refs[36:40]
        ra_refs, rb_refs, st_refs = refs[40:44], refs[44:48], refs[48:52]
        recv_s_ref = refs[52]
        send_a, recv_a, send_b, recv_b, send_s_sems, recv_s_sems = refs[53:59]
        pos = _mesh_pos()
        x, y, c = pos
        me = _logical(pos)
        sibling = (x, y, 1 - c)

        def chip(j):
            return _flip(x, j & 1), _flip(y, (j >> 1) & 1)

        def to_sibling(w, j):
            return pltpu.make_async_remote_copy(
                src_ref=p_refs[w].at[_logical((*chip(j), 1 - c))], dst_ref=ra_refs[w].at[j],
                send_sem=send_a.at[N_W * j + w], recv_sem=recv_a.at[N_W * j + w],
                device_id=sibling, device_id_type=pl.DeviceIdType.MESH)

        def to_chip(w, j):
            return pltpu.make_async_remote_copy(
                src_ref=st_refs[w].at[j - 1], dst_ref=rb_refs[w].at[j - 1],
                send_sem=send_b.at[N_W * (j - 1) + w], recv_sem=recv_b.at[N_W * (j - 1) + w],
                device_id=(*chip(j), c), device_id_type=pl.DeviceIdType.MESH)

        def tiny(k):
            return pltpu.make_async_remote_copy(
                src_ref=small_ref, dst_ref=recv_s_ref.at[k],
                send_sem=send_s_sems.at[k], recv_sem=recv_s_sems.at[k],
                device_id=_peer(pos, k), device_id_type=pl.DeviceIdType.MESH)

        order = (0, 3, 2, 1)
        for j in (1, 2, 3, 0):
            for w in order:
                to_sibling(w, j).start()
        for k in range(1, N_DEV):
            tiny(k).start()

        for j in (1, 2, 3):
            d = _logical((*chip(j), c))
            for w in order:
                to_sibling(w, j).wait_recv()
                chunk = chunks[w]

                def pair(t, carry, w=w, j=j, d=d, chunk=chunk):
                    rows = pl.ds(pl.multiple_of(t * chunk, chunk), chunk)
                    s = p_refs[w][d, rows, :].astype(F32) + ra_refs[w][j, rows, :].astype(F32)
                    st_refs[w][j - 1, rows, :] = s.astype(BF16)
                    return carry

                lax.fori_loop(0, W_BLOCKS[w][0] // chunk, pair, 0)
                to_chip(w, j).start()

        recv_s_ref[0] = small_ref[...]
        for k in range(1, N_DEV):
            tiny(k).wait_recv()
        acc = recv_s_ref[me]
        for a in range(1, N_DEV):
            acc = acc + recv_s_ref[lax.bitwise_xor(me, a)]
        row = lax.broadcasted_iota(jnp.int32, (R_SMALL, LANES), 0)
        gs = jnp.where(row == 8, acc + pltpu.roll(acc, 64, 1), acc)
        gs = jnp.where(row == ROW_LOSS, jnp.sum(acc, axis=1, keepdims=True) * (0.5 / D_MODEL), gs)
        gs_ref[...] = gs
        ds, nms, nvs = _adamw(ws_ref[...], gs, ms_ref[...], vs_ref[...])
        ds_ref[...] = ds
        nms_ref[...] = nms
        nvs_ref[...] = nvs

        for w in (1, 2, 3, 0):
            to_sibling(w, 0).wait_recv()
            for j in (1, 2, 3):
                to_chip(w, j).wait_recv()
            chunk = chunks[w]

            def step(t, carry, w=w, chunk=chunk):
                rows = pl.ds(pl.multiple_of(t * chunk, chunk), chunk)
                g = p_refs[w][me, rows, :].astype(F32) + ra_refs[w][0, rows, :].astype(F32)
                for j in range(n_chip - 1):
                    g = g + rb_refs[w][j, rows, :].astype(F32)
                d, nm, nv = _adamw(w_refs[w][rows, :], g, m_refs[w][rows, :], v_refs[w][rows, :])
                g_refs[w][rows, :] = g
                d_refs[w][rows, :] = d
                nm_refs[w][rows, :] = nm
                nv_refs[w][rows, :] = nv
                return carry

            lax.fori_loop(0, W_BLOCKS[w][0] // chunk, step, 0)
        for k in range(1, N_DEV):
            tiny(k).wait_send()
        for w in range(N_W):
            for j in range(n_chip):
                to_sibling(w, j).wait_send()
            for j in (1, 2, 3):
                to_chip(w, j).wait_send()

    vm = pl.BlockSpec(memory_space=pltpu.VMEM)
    blk = [jax.ShapeDtypeStruct(shp, F32) for shp in W_BLOCKS]
    small_shape = jax.ShapeDtypeStruct((R_SMALL, LANES), F32)
    return pl.pallas_call(
        body, name="reduce_adamw",
        in_specs=[vm] * 20, out_specs=[vm] * 20,
        out_shape=blk * 4 + [small_shape] * 4,
        scratch_shapes=[pltpu.VMEM((n_chip,) + shp, BF16) for shp in W_BLOCKS]
        + [pltpu.VMEM((n_chip - 1,) + shp, BF16) for shp in W_BLOCKS] * 2
        + [pltpu.VMEM((N_DEV, R_SMALL, LANES), F32),
           pltpu.SemaphoreType.DMA((n_chip * N_W,)), pltpu.SemaphoreType.DMA((n_chip * N_W,)),
           pltpu.SemaphoreType.DMA(((n_chip - 1) * N_W,)), pltpu.SemaphoreType.DMA(((n_chip - 1) * N_W,)),
           pltpu.SemaphoreType.DMA((N_DEV,)), pltpu.SemaphoreType.DMA((N_DEV,))],
        compiler_params=pltpu.CompilerParams(vmem_limit_bytes=VMEM_LIMIT),
    )(*parts, small, *w_blk, *m_blk, *v_blk, w_s, m_s, v_s)


def _pack_small(norm_in, a_q, a_k, b_cq, b_ckv, b_q, b_k):
    def row(v):
        return jnp.pad(v.reshape(1, -1), ((0, 0), (0, LANES - v.size)))
    rows = [norm_in.reshape(8, LANES), row(a_q), row(a_k), b_cq.reshape(3, LANES), b_ckv.reshape(2, LANES),
            row(b_q), row(b_k), jnp.zeros((R_SMALL - 17, LANES), F32)]
    return jnp.concatenate(rows, axis=0)


def _unpack_small(s):
    return (s[0:8].reshape(1, D_MODEL), s[8:9, :64], s[9:10, :64], s[10:13].reshape(1, B_Q_RANK),
            s[13:15].reshape(1, B_KV_RANK), s[15:16, :B_QK_DIM], s[16:17, :B_QK_DIM])


C_IN = N_IN // N_DEV
_RUNS = ((0, 512, O_QA), (512, 576, O_KA), (576, 640, O_KA + 128), (640, 704, O_VA), (704, 768, O_VA + 128),
         (768, 1280, O_GA), (1280, 1664, O_CQ), (1664, 1920, O_CKV), (1920, 1952, O_KR + 64), (1952, 2464, O_GB))


def _in_cols(g_in, lo, hi):
    out = []
    for d in range(N_DEV):
        a, b = max(lo, C_IN * d), min(hi, C_IN * (d + 1))
        if a < b:
            out.append(g_in[d][:, a - C_IN * d:b - C_IN * d])
    return out


def _widen_weights(g_in, g_uq, g_ukv, g_out):
    z64 = jnp.zeros((D_MODEL, 64), BF16)
    z32 = jnp.zeros((D_MODEL, 32), BF16)
    k0, k1 = _in_cols(g_in, 512, 576), _in_cols(g_in, 576, 640)
    v0, v1 = _in_cols(g_in, 640, 704), _in_cols(g_in, 704, 768)
    kr_blk = [z64] + _in_cols(g_in, 1920, 1952) + [z32]
    w_wide = jnp.concatenate(
        _in_cols(g_in, 0, 512) + k0 + k0 + k1 + k1 + v0 + v0 + v1 + v1 + _in_cols(g_in, 768, 1920)
        + kr_blk * B_HEADS + _in_cols(g_in, 1952, 2464), axis=1)
    w_uq = g_uq.reshape(B_Q_RANK, 384)
    wuq = jnp.pad(w_uq.reshape(B_Q_RANK, B_HEADS, B_QK_DIM), ((0, 0), (0, 0), (0, LANES - B_QK_DIM)))
    wuq = wuq.reshape(B_Q_RANK, 512)
    ukv = g_ukv.transpose(1, 0, 2).reshape(B_KV_RANK, B_HEADS, B_NOPE_DIM + B_V_DIM)
    wuk = jnp.pad(ukv[:, :, :B_NOPE_DIM], ((0, 0), (0, 0), (0, LANES - B_NOPE_DIM))).reshape(B_KV_RANK, 512)
    wuv = ukv[:, :, B_NOPE_DIM:].reshape(B_KV_RANK, 512)
    return w_wide, wuq, wuk, wuv, g_out.reshape(D_MODEL, D_MODEL)


def _grad_blocks(dw_wide, dwuq, dwuk, dwuv, dw_out):
    blocks = []
    for d in range(N_DEV):
        pieces = []
        for lo, hi, wide in _RUNS:
            a, b = max(lo, C_IN * d), min(hi, C_IN * (d + 1))
            if a < b:
                pieces.append(dw_wide[:, wide + a - lo:wide + b - lo])
        blocks.append(jnp.concatenate(pieces, axis=1))
    p_in = jnp.stack(blocks).astype(BF16)
    dw_uq = dwuq.reshape(B_Q_RANK, B_HEADS, LANES)[:, :, :B_QK_DIM].reshape(N_DEV, B_Q_RANK // N_DEV, 384)
    dk = dwuk.reshape(B_KV_RANK, B_HEADS, LANES)[:, :, :B_NOPE_DIM]
    dv = dwuv.reshape(B_KV_RANK, B_HEADS, B_V_DIM)
    dw_ukv = jnp.concatenate([dk, dv], axis=2).reshape(B_KV_RANK, N_DEV, 768 // N_DEV).transpose(1, 0, 2)
    return (p_in, dw_uq.astype(BF16), dw_ukv.astype(BF16),
            dw_out.reshape(N_DEV, D_MODEL // N_DEV, D_MODEL).astype(BF16))


def _rope_tables(s_len):
    row = jnp.arange(s_len // GRID_W, dtype=F32)
    col = jnp.arange(GRID_W, dtype=F32)

    def parts(dim):
        half = dim // 2
        inv = 1.0 / (ROPE_THETA ** (jnp.arange(0, half, 2, dtype=F32) / half))
        ar, ac = row[:, None] * inv[None, :], col[:, None] * inv[None, :]
        zr, zc = jnp.zeros_like(ar), jnp.zeros_like(ac)
        cos_c = jnp.concatenate([zc, zc, jnp.cos(ac), jnp.cos(ac)], axis=1)
        cos_r = jnp.concatenate([jnp.cos(ar), jnp.cos(ar), zr, zr], axis=1)
        sin_c = jnp.concatenate([zc, zc, -jnp.sin(ac), jnp.sin(ac)], axis=1)
        sin_r = jnp.concatenate([-jnp.sin(ar), jnp.sin(ar), zr, zr], axis=1)
        return cos_c, cos_r, sin_c, sin_r

    tabs = [jnp.tile(t, (1, 2)) for t in parts(A_HEAD_DIM)]
    for n, t in enumerate(parts(B_ROPE_DIM)):
        lead = jnp.full((t.shape[0], B_NOPE_DIM), 1.0 if n == 0 else 0.0, F32)
        tail = jnp.full((t.shape[0], LANES - B_QK_DIM), 1.0 if n == 0 else 0.0, F32)
        tabs.append(jnp.concatenate([lead, t, tail], axis=1))
    return tuple(tabs)


def kernel(x, norm_in, w_in, a_q_norm, a_k_norm, b_cq_norm, b_ckv_norm, w_uq, w_ukv, b_q_norm, b_k_norm, w_out, loss_target, m_norm_in, m_w_in, m_a_q_norm, m_a_k_norm, m_b_cq_norm, m_b_ckv_norm, m_w_uq, m_w_ukv, m_b_q_norm, m_b_k_norm, m_w_out, v_norm_in, v_w_in, v_a_q_norm, v_a_k_norm, v_b_cq_norm, v_b_ckv_norm, v_w_uq, v_w_ukv, v_b_q_norm, v_b_k_norm, v_w_out):
    s_len = x.shape[1]
    tm = min(256, s_len)
    tq, tk = min(512, s_len), min(2048, s_len)
    ftq, ftk = min(256, s_len), min(256, s_len)
    x2 = x.reshape(s_len, D_MODEL)
    t2 = loss_target.reshape(s_len, D_MODEL)

    w_blk = (w_in[0], w_uq[0], w_ukv[0], w_out[0])
    w_wide, wuq, wuk, wuv, wout = _widen_weights(*_gather_blocks_call(w_blk))

    def dup(v, pad_to=None):
        v = v.reshape(1, -1)
        if pad_to is None:
            return jnp.concatenate([v, v], axis=1)
        return jnp.pad(v, ((0, 0), (0, pad_to - v.shape[1])))

    g64 = jnp.asarray(np.kron(np.eye(2), np.ones((64, 64))), dtype=BF16)
    consts = dict(gin=norm_in, w=w_wide, wuq=wuq, wuk=wuk, wuv=wuv, gq=dup(a_q_norm), gk=dup(a_k_norm),
                  gcq=b_cq_norm, gckv=b_ckv_norm, gqb=dup(b_q_norm, LANES), gkb=dup(b_k_norm, LANES), g64=g64)
    tabs = _rope_tables(s_len)

    qa, ka, va, va_t, ga, qb, kb, vb, vb_t, gb, raw, xnb = _pre_fwd_call(x2, consts, tabs, min(512, s_len))
    oa, lse_a_t = _attn_fwd_t_call(qa, ka, va_t, groups=A_KV_HEADS, sub=4, masked=True, scale=None, tq=ftq, tk=ftk,
                                   name="attn_fwd_a")
    ob, lse_b_t = _attn_fwd_t_call(qb, kb, vb_t, groups=1, sub=B_HEADS, masked=False, scale=None, tq=ftq,
                                   tk=ftk, name="attn_fwd_b")
    dh, doa, dob, dga, dgb, dw_out, loss_row = _out_call(x2, t2, oa, ob, ga, gb, wout, min(512, s_len))

    dqa, dka_t, dva_t = _attn_bwd_q_call(qa, ka, va, doa, oa, lse_a_t, groups=A_KV_HEADS, sub=4, masked=True,
                                         tq=tq, tk=tk, ck=min(512, s_len), name="attn_bwd_a")
    dqb, dkb_t, dvb_t = _attn_bwd_q_call(qb, kb, vb, dob, ob, lse_b_t, groups=2, sub=2, masked=False,
                                         tq=tq, tk=min(4096, s_len), ck=min(256, s_len), name="attn_bwd_b")
    dx, dproj_b, dwuq, dwuk, dwuv, small = _pre_bwd_call(
        x2, raw, dh, dqa, dka_t, dva_t, dga, dqb, dkb_t, dvb_t, dgb, consts, tabs, tm)
    dw_wide = _dw_in_call(xnb, dproj_b, min(1024, s_len), N_WIDE)

    parts = _grad_blocks(dw_wide, dwuq, dwuk, dwuv, dw_out)
    small = jnp.concatenate([small[:ROW_LOSS], loss_row, small[ROW_LOSS + 1:]], axis=0)

    m_blk = (m_w_in[0], m_w_uq[0], m_w_ukv[0], m_w_out[0])
    v_blk = (v_w_in[0], v_w_uq[0], v_w_ukv[0], v_w_out[0])
    w_s = _pack_small(norm_in, a_q_norm, a_k_norm, b_cq_norm, b_ckv_norm, b_q_norm, b_k_norm)
    m_s = _pack_small(m_norm_in, m_a_q_norm, m_a_k_norm, m_b_cq_norm, m_b_ckv_norm, m_b_q_norm, m_b_k_norm)
    v_s = _pack_small(v_norm_in, v_a_q_norm, v_a_k_norm, v_b_cq_norm, v_b_ckv_norm, v_b_q_norm, v_b_k_norm)
    res = _reduce_two_level_call(parts, small, w_blk, m_blk, v_blk, w_s, m_s, v_s)

    def leaves(blocks, sm):
        wi, uq, ukv, wo = [b[None] for b in blocks]
        n_in, aq, ak, bcq, bckv, bq, bk = _unpack_small(sm)
        return [n_in, wi, aq, ak, bcq, bckv, uq, ukv, bq, bk, wo]

    g_s = res[16]
    loss = g_s[ROW_LOSS, 0]
    grad_x = dx.reshape(1, s_len, D_MODEL)
    return (loss, grad_x, *leaves(res[0:4], res[16]), *leaves(res[4:8], res[17]), *leaves(res[8:12], res[18]),
            *leaves(res[12:16], res[19]))
```

```python
import functools

import numpy as np
import jax
import jax.numpy as jnp
from jax import lax
from jax.experimental import pallas as pl
from jax.experimental.pallas import tpu as pltpu

F32 = jnp.float32
BF16 = jnp.bfloat16

D_MODEL = 1024
GRID_W = 64
ROPE_THETA = 10000.0
EPS = 1e-6
A_HEAD_DIM = 64
A_HEADS = 8
A_KV_HEADS = 2
B_HEADS = 4
B_NOPE_DIM = 64
B_ROPE_DIM = 32
B_QK_DIM = 96
B_V_DIM = 128
B_Q_RANK = 384
B_KV_RANK = 256
N_IN = 2464
N_DEV = 8

ADAM_LR = 0.001
ADAM_B1 = 0.9
ADAM_B2 = 0.999
ADAM_EPS = 1e-08
ADAM_WD = 0.01
ADAM_STEP = 10

QA_SCALE = 0.125
QB_SCALE = 1.0 / float(np.sqrt(B_QK_DIM))

LANES = 128
O_QA, O_KA, O_VA, O_GA, O_CQ, O_CKV, O_KR, O_GB, N_WIDE = 0, 512, 768, 1024, 1536, 1920, 2176, 2688, 3200
R_QA, R_KA, R_CQ, R_CKV, R_KR, R_WIDTH = 0, 512, 768, 1152, 1408, 1920

R_SMALL = 24
ROW_LOSS = 17

VMEM_LIMIT = 56 * 1024 * 1024

NT = (((1,), (1,)), ((), ()))
TN = (((0,), (0,)), ((), ()))


def _dot(a, b):
    return jnp.dot(a, b, preferred_element_type=F32)


def _dot_nt(a, b):
    return lax.dot_general(a, b, NT, preferred_element_type=F32)


def _dot_tn(a, b):
    return lax.dot_general(a, b, TN, preferred_element_type=F32)


def _params(sem=None):
    return pltpu.CompilerParams(dimension_semantics=sem, vmem_limit_bytes=VMEM_LIMIT)


def _full(shape):
    nd = len(shape)
    return pl.BlockSpec(shape, lambda *_: (0,) * nd)


def _swap_sel(rows, shift):
    lane = lax.broadcasted_iota(jnp.int32, (rows, LANES), 1)
    return pltpu.roll(lane, shift, 1) == (lane ^ shift)


def _swap(x, shift, sel):
    return jnp.where(sel, pltpu.roll(x, shift, 1), pltpu.roll(x, LANES - shift, 1))


def _group_sum64(x, g64):
    hi = x.astype(BF16)
    lo = (x - hi.astype(F32)).astype(BF16)
    return _dot(hi, g64) + _dot(lo, g64)


def _row_sum(x):
    return jnp.sum(x, axis=-1, keepdims=True)


def _col_fwd(xs, msum, denom, gain, cos, sin, shift, sel):
    r = lax.rsqrt(msum(xs * xs) * (1.0 / denom) + EPS)
    xh = xs * r
    n = xh * gain
    return n * cos + _swap(n, shift, sel) * sin, xh, r


def _col_bwd(d_out, xh, r, msum, denom, gain, cos, sin, shift, sel):
    dn = d_out * cos + _swap(d_out * sin, shift, sel)
    dgain = jnp.sum(dn * xh, axis=0, keepdims=True)
    dxh = dn * gain
    dx = r * (dxh - xh * (msum(dxh * xh) * (1.0 / denom)))
    return dx, dgain


def _rms_fwd(x, gain):
    r = lax.rsqrt(jnp.mean(x * x, axis=-1, keepdims=True) + EPS)
    xh = x * r
    return xh * gain, xh, r


def _rms_bwd(dy, xh, r, gain):
    dgain = jnp.sum(dy * xh, axis=0, keepdims=True)
    dxh = dy * gain
    dx = r * (dxh - xh * jnp.mean(dxh * xh, axis=-1, keepdims=True))
    return dx, dgain


def _pre_forward(x, gin, w, wuq, wuk, wuv, gq, gk, gcq, gckv, gqb, gkb, ca, sa, cb, sb, g64, tm, raw=None):
    sel16 = _swap_sel(tm, 16)
    sel8 = _swap_sel(tm, 8)
    xn, xh0, r0 = _rms_fwd(x, gin)
    xnb = xn.astype(BF16)
    proj = None
    if raw is None:
        proj = _dot(xnb, w)
        raw = jnp.concatenate([proj[:, O_QA:O_QA + 512], proj[:, O_KA:O_KA + 256], proj[:, O_CQ:O_CQ + B_Q_RANK],
                               proj[:, O_CKV:O_CKV + B_KV_RANK], proj[:, O_KR:O_KR + 512]], axis=1)
    gs64 = functools.partial(_group_sum64, g64=g64)
    qa = [_col_fwd(raw[:, R_QA + LANES * s:R_QA + LANES * (s + 1)], gs64, 64.0, gq, ca, sa, 16, sel16)
          for s in range(4)]
    ka = [_col_fwd(raw[:, R_KA + LANES * s:R_KA + LANES * (s + 1)], _row_sum, 128.0, gk, ca, sa, 16, sel16)
          for s in range(2)]
    cq, cqh, rcq = _rms_fwd(raw[:, R_CQ:R_CQ + B_Q_RANK], gcq)
    cqb = cq.astype(BF16)
    qb_raw = _dot(cqb, wuq)
    qb = [_col_fwd(qb_raw[:, LANES * h:LANES * (h + 1)], _row_sum, float(B_QK_DIM), gqb, cb, sb, 8, sel8)
          for h in range(B_HEADS)]
    ckv, ckvh, rckv = _rms_fwd(raw[:, R_CKV:R_CKV + B_KV_RANK], gckv)
    ckvb = ckv.astype(BF16)
    kb_raw = _dot(ckvb, wuk) + raw[:, R_KR:R_KR + 512]
    vb = _dot(ckvb, wuv)
    kb = [_col_fwd(kb_raw[:, LANES * h:LANES * (h + 1)], _row_sum, float(B_QK_DIM), gkb, cb, sb, 8, sel8)
          for h in range(B_HEADS)]
    return dict(xh0=xh0, r0=r0, xnb=xnb, proj=proj, raw=raw, qa=qa, ka=ka, cqh=cqh, rcq=rcq, cqb=cqb, qb=qb,
                ckvh=ckvh, rckv=rckv, ckvb=ckvb, kb=kb, vb=vb, sel16=sel16, sel8=sel8, gs64=gs64)


def _rope_tiles(tab_refs, i, tm):
    per_tile = tm // GRID_W
    out = []
    for t in range(4):
        col_ref, row_ref = tab_refs[2 * t], tab_refs[2 * t + 1]
        col = col_ref[...]
        out.append(jnp.concatenate([col + row_ref[pl.ds(i * per_tile + b, 1), :] for b in range(per_tile)], axis=0))
    return out


_PRE_IN_NAMES = ("gin", "w", "wuq", "wuk", "wuv", "gq", "gk", "gcq", "gckv", "gqb", "gkb", "g64")


def _pre_const_specs(consts):
    return [_full(consts[n].shape) for n in _PRE_IN_NAMES]


def _pre_fwd_call(x, consts, tabs, tm):
    s_len = x.shape[0]
    ts = min(256, tm)

    def body(x_ref, *refs):
        tab_refs, refs = refs[:8], refs[8:]
        (gin_ref, w_ref, wuq_ref, wuk_ref, wuv_ref, gq_ref, gk_ref, gcq_ref, gckv_ref, gqb_ref, gkb_ref, g64_ref,
         qa_ref, ka_ref, va_ref, vat_ref, ga_ref, qb_ref, kb_ref, vb_ref, vbt_ref, gb_ref, raw_ref, xnb_ref) = refs
        for part in range(tm // ts):
            r = slice(ts * part, ts * (part + 1))
            ca, sa, cb, sb = _rope_tiles(tab_refs, pl.program_id(0) * (tm // ts) + part, ts)
            f = _pre_forward(x_ref[r, :], gin_ref[...], w_ref[...], wuq_ref[...], wuk_ref[...], wuv_ref[...],
                             gq_ref[...], gk_ref[...], gcq_ref[...], gckv_ref[...], gqb_ref[...], gkb_ref[...],
                             ca, sa, cb, sb, g64_ref[...], ts)
            proj = f["proj"]
            raw_ref[r, :] = f["raw"]
            xnb_ref[r, :] = f["xnb"]
            for s in range(4):
                qa_ref[r, LANES * s:LANES * (s + 1)] = (f["qa"][s][0] * QA_SCALE).astype(BF16)
            for s in range(2):
                ka_ref[r, LANES * s:LANES * (s + 1)] = f["ka"][s][0].astype(BF16)
            va = proj[:, O_VA:O_VA + 256]
            va_ref[r, :] = va.astype(BF16)
            vat_ref[:, r] = va.T.astype(BF16)
            ga_ref[r, :] = proj[:, O_GA:O_GA + 512]
            for h in range(B_HEADS):
                qb_ref[r, LANES * h:LANES * (h + 1)] = (f["qb"][h][0] * QB_SCALE).astype(BF16)
                kb_ref[r, LANES * h:LANES * (h + 1)] = f["kb"][h][0].astype(BF16)
            vb_ref[r, :] = f["vb"].astype(BF16)
            vbt_ref[:, r] = f["vb"].T.astype(BF16)
            gb_ref[r, :] = proj[:, O_GB:O_GB + 512]

    def rows(width):
        return pl.BlockSpec((tm, width), lambda i: (i, 0))

    def cols(height):
        return pl.BlockSpec((height, tm), lambda i: (0, i))

    outs = [((s_len, 512), BF16, rows(512)), ((s_len, 256), BF16, rows(256)), ((s_len, 256), BF16, rows(256)),
            ((256, s_len), BF16, cols(256)), ((s_len, 512), F32, rows(512)), ((s_len, 512), BF16, rows(512)),
            ((s_len, 512), BF16, rows(512)), ((s_len, 512), BF16, rows(512)), ((512, s_len), BF16, cols(512)),
            ((s_len, 512), F32, rows(512)), ((s_len, R_WIDTH), F32, rows(R_WIDTH)),
            ((s_len, D_MODEL), BF16, rows(D_MODEL))]
    return pl.pallas_call(
        body, name="pre_fwd", grid=(s_len // tm,),
        in_specs=[rows(D_MODEL)] + [_full(t.shape) for t in tabs] + _pre_const_specs(consts),
        out_specs=[sp for _, _, sp in outs],
        out_shape=[jax.ShapeDtypeStruct(sh, dt) for sh, dt, _ in outs],
        compiler_params=_params(("arbitrary",)),
    )(x, *tabs, *[consts[n] for n in _PRE_IN_NAMES])


def _head_masks(rows):
    lane = lax.broadcasted_iota(jnp.int32, (rows, LANES), 1)
    return lane < 64, lane >= 64


def _row_fold(x, op):
    return op(x.reshape(x.shape[0] // 8, 8, x.shape[1]), axis=0)


def _attn_fwd_t_call(q, k, vt, *, groups, sub, masked, scale, tq, tk, name):
    s_len = q.shape[0]
    qw = LANES * (sub // 2 if masked else sub)
    kvw = LANES if masked else LANES * sub
    n_c = s_len // tk
    kv_mode = pl.Buffered(1) if groups == 1 else None

    def body(q_ref, qn_ref, k_ref, vt_ref, o_ref, lse_ref, s_sc, mx_sc):
        keep = _head_masks(tq) if masked else None

        def kv_of(hh):
            return slice(0, LANES) if masked else slice(LANES * hh, LANES * (hh + 1))

        def q_of(ref, hh):
            if not masked:
                return ref[:, LANES * hh:LANES * (hh + 1)]
            qp = ref[:, LANES * (hh // 2):LANES * (hh // 2 + 1)]
            return jnp.where(keep[hh % 2], qp, jnp.zeros_like(qp))

        def scores(hh, qm, c, mx):
            s_t = _dot_nt(k_ref[tk * c:tk * (c + 1), kv_of(hh)], qm)
            if scale is not None:
                s_t = s_t * scale
            s_sc[hh % 2, c] = s_t
            return jnp.maximum(mx, _row_fold(s_t, jnp.max))

        neg = jnp.full((8, tq), -jnp.inf, F32)

        @pl.when(pl.program_id(1) == 0)
        def _():
            qm0 = q_of(q_ref, 0)
            mx0 = neg
            for c in range(n_c):
                mx0 = scores(0, qm0, c, mx0)
            mx_sc[...] = mx0

        mx_next = mx_sc[...]
        outs = []
        for hh in range(sub):
            m = jnp.max(mx_next, axis=0, keepdims=True)
            nxt = (hh + 1) % sub
            qm_next = q_of(q_ref if hh + 1 < sub else qn_ref, nxt)
            mx_next = neg
            lsum = jnp.zeros((8, tq), F32)
            acc = jnp.zeros((LANES, tq), F32)
            for c in range(n_c):
                p_t = jnp.exp(s_sc[hh % 2, c] - m)
                lsum = lsum + _row_fold(p_t, jnp.sum)
                mx_next = scores(nxt, qm_next, c, mx_next)
                acc = acc + _dot(vt_ref[kv_of(hh), tk * c:tk * (c + 1)], p_t.astype(BF16))
            l = jnp.sum(lsum, axis=0, keepdims=True)
            outs.append((acc / l).T)
            lse_ref[hh] = m + jnp.log(l)
        mx_sc[...] = mx_next
        if masked:
            for pr in range(sub // 2):
                o_ref[:, LANES * pr:LANES * (pr + 1)] = jnp.where(keep[0], outs[2 * pr], outs[2 * pr + 1])
        else:
            for hh in range(sub):
                o_ref[:, LANES * hh:LANES * (hh + 1)] = outs[hh]

    n_q = s_len // tq
    return pl.pallas_call(
        body, name=name, grid=(groups, n_q),
        in_specs=[pl.BlockSpec((tq, qw), lambda g, i: (i, g)),
                  pl.BlockSpec((tq, qw), lambda g, i: (jnp.minimum(i + 1, n_q - 1), g)),
                  pl.BlockSpec((s_len, kvw), lambda g, i: (0, g), pipeline_mode=kv_mode),
                  pl.BlockSpec((kvw, s_len), lambda g, i: (g, 0), pipeline_mode=kv_mode)],
        out_specs=[pl.BlockSpec((tq, qw), lambda g, i: (i, g)),
                   pl.BlockSpec((sub, 1, tq), lambda g, i: (g, 0, i))],
        out_shape=[jax.ShapeDtypeStruct((s_len, groups * qw), F32),
                   jax.ShapeDtypeStruct((groups * sub, 1, s_len), F32)],
        scratch_shapes=[pltpu.VMEM((2, n_c, tk, tq), F32), pltpu.VMEM((8, tq), F32)],
        compiler_params=_params(("arbitrary", "arbitrary")),
    )(q, q, k, vt)


def _attn_bwd_q_call(q, k, v, do, o, lse_t, *, groups, sub, masked, tq, tk, ck, name):
    s_len = q.shape[0]
    qw = LANES * (sub // 2 if masked else sub)
    kvw = LANES if masked else LANES * sub
    n_c = tk // ck

    def body(q_ref, k_ref, v_ref, do_ref, o_ref, lse_ref, dq_ref, dkt_ref, dvt_ref):
        j = pl.program_id(1)
        i = pl.program_id(2)

        @pl.when((j == 0) & (i == 0))
        def _():
            dq_ref[...] = jnp.zeros(dq_ref.shape, F32)

        @pl.when(i == 0)
        def _():
            dkt_ref[...] = jnp.zeros(dkt_ref.shape, F32)
            dvt_ref[...] = jnp.zeros(dvt_ref.shape, F32)

        lkeep = _head_masks(tq) if masked else None
        heads = []
        for hh in range(sub):
            if masked:
                cols = slice(LANES * (hh // 2), LANES * (hh // 2 + 1))
                kv = slice(0, LANES)
                qp, dop = q_ref[:, cols], do_ref[:, cols]
                qm = jnp.where(lkeep[hh % 2], qp, jnp.zeros_like(qp))
                dom = jnp.where(lkeep[hh % 2], dop, jnp.zeros_like(dop))
            else:
                cols = kv = slice(LANES * hh, LANES * (hh + 1))
                qm, dom = q_ref[:, cols], do_ref[:, cols]
            delta = jnp.sum(dom.astype(F32) * o_ref[:, cols], axis=1, keepdims=True)
            lse = jnp.broadcast_to(lse_ref[hh], (LANES, tq)).T[:, 0:1]
            heads.append((cols, kv, qm, dom, qm.T, dom.T, delta, lse))

        def products(hh, c):
            _, kv, qm, dom, _, _, _, _ = heads[hh]
            return _dot_nt(qm, k_ref[ck * c:ck * (c + 1), kv]), _dot_nt(dom, v_ref[ck * c:ck * (c + 1), kv])

        items = [(hh, c) for hh in range(sub) for c in range(n_c)]
        dq_acc = [jnp.zeros((tq, LANES), F32) for _ in range(sub)]
        nxt = products(*items[0])
        for n, (hh, c) in enumerate(items):
            s, dp = nxt
            if n + 1 < len(items):
                nxt = products(*items[n + 1])
            _, kv, qm, dom, qmt, domt, delta, lse = heads[hh]
            p = jnp.exp(s - lse)
            ds = p * (dp - delta)
            p_b = p.astype(BF16)
            ds_b = ds.astype(BF16)
            kcols = slice(ck * c, ck * (c + 1))
            dvt_ref[kv, kcols] += _dot(domt, p_b)
            dkt_ref[kv, kcols] += _dot(qmt, ds_b)
            dq_acc[hh] = dq_acc[hh] + _dot(ds_b, k_ref[kcols, kv])
        rows = pl.ds(pl.multiple_of(i * tq, tq), tq)
        if masked:
            for pr in range(sub // 2):
                dq_ref[rows, LANES * pr:LANES * (pr + 1)] += jnp.where(lkeep[0], dq_acc[2 * pr], dq_acc[2 * pr + 1])
        else:
            for hh in range(sub):
                dq_ref[rows, LANES * hh:LANES * (hh + 1)] += dq_acc[hh]

    return pl.pallas_call(
        body, name=name, grid=(groups, s_len // tk, s_len // tq),
        in_specs=[pl.BlockSpec((tq, qw), lambda g, j, i: (i, g)),
                  pl.BlockSpec((tk, kvw), lambda g, j, i: (j, g)),
                  pl.BlockSpec((tk, kvw), lambda g, j, i: (j, g)),
                  pl.BlockSpec((tq, qw), lambda g, j, i: (i, g)),
                  pl.BlockSpec((tq, qw), lambda g, j, i: (i, g)),
                  pl.BlockSpec((sub, 1, tq), lambda g, j, i: (g, 0, i))],
        out_specs=[pl.BlockSpec((s_len, qw), lambda g, j, i: (0, g)),
                   pl.BlockSpec((kvw, tk), lambda g, j, i: (g, j)),
                   pl.BlockSpec((kvw, tk), lambda g, j, i: (g, j))],
        out_shape=[jax.ShapeDtypeStruct((s_len, groups * qw), F32),
                   jax.ShapeDtypeStruct((groups * kvw, s_len), F32),
                   jax.ShapeDtypeStruct((groups * kvw, s_len), F32)],
        compiler_params=_params(("arbitrary", "arbitrary", "arbitrary")),
    )(q, k, v, do, o, lse_t)


def _silu_parts(g):
    sig = 1.0 / (1.0 + jnp.exp(-g))
    return g * sig, sig * (1.0 + g * (1.0 - sig))


def _out_call(x, target, oa, ob, ga, gb, wout, tm):
    s_len = x.shape[0]
    n_t = s_len // tm

    def body(x_ref, t_ref, oa_ref, ob_ref, ga_ref, gb_ref, w_ref,
             dh_ref, doa_ref, dob_ref, dga_ref, dgb_ref, dwb_ref, loss_ref, dw_ref):
        i = pl.program_id(0)

        @pl.when(i == 0)
        def _():
            dw_ref[...] = jnp.zeros(dw_ref.shape, F32)
            loss_ref[...] = jnp.zeros(loss_ref.shape, F32)

        oa_v, ob_v = oa_ref[...], ob_ref[...]
        silu_a, dsilu_a = _silu_parts(ga_ref[...])
        silu_b, dsilu_b = _silu_parts(gb_ref[...])
        ya = (oa_v * silu_a).astype(BF16)
        yb = (ob_v * silu_b).astype(BF16)
        h = x_ref[...] + _dot(ya, w_ref[0:512, :]) + _dot(yb, w_ref[512:1024, :])
        err = h - t_ref[...]
        part = jnp.sum(err * err, axis=0, keepdims=True)
        acc = part[:, 0:LANES]
        for c in range(1, D_MODEL // LANES):
            acc = acc + part[:, LANES * c:LANES * (c + 1)]
        loss_ref[...] += acc
        dh = err * (1.0 / D_MODEL)
        dh_ref[...] = dh
        dhb = dh.astype(BF16)
        dya = _dot_nt(dhb, w_ref[0:512, :])
        dyb = _dot_nt(dhb, w_ref[512:1024, :])
        doa = dya * silu_a
        dob = dyb * silu_b
        doa_ref[...] = doa.astype(BF16)
        dob_ref[...] = dob.astype(BF16)
        dga_ref[...] = dya * oa_v * dsilu_a
        dgb_ref[...] = dyb * ob_v * dsilu_b
        dw_ref[0:512, :] += _dot_tn(ya, dhb)
        dw_ref[512:1024, :] += _dot_tn(yb, dhb)

        @pl.when(i == n_t - 1)
        def _():
            dwb_ref[...] = dw_ref[...].astype(BF16)

    def rows(width):
        return pl.BlockSpec((tm, width), lambda i: (i, 0))

    outs = [(D_MODEL, F32), (512, BF16), (512, BF16), (512, F32), (512, F32)]
    return pl.pallas_call(
        body, name="out_fwd", grid=(n_t,),
        in_specs=[rows(D_MODEL), rows(D_MODEL), rows(512), rows(512), rows(512), rows(512),
                  _full((D_MODEL, D_MODEL))],
        out_specs=[rows(wd) for wd, _ in outs] + [_full((D_MODEL, D_MODEL)), _full((1, LANES))],
        out_shape=[jax.ShapeDtypeStruct((s_len, wd), dt) for wd, dt in outs]
        + [jax.ShapeDtypeStruct((D_MODEL, D_MODEL), BF16), jax.ShapeDtypeStruct((1, LANES), F32)],
        scratch_shapes=[pltpu.VMEM((D_MODEL, D_MODEL), F32)],
        compiler_params=_params(("arbitrary",)),
    )(x, target, oa, ob, ga, gb, wout)


def _pre_bwd_call(x, raw, dh, dqa, dka, dva, dga, dqb, dkb, dvb, dgb, consts, tabs, tm):
    s_len = x.shape[0]
    ts = min(256, tm)

    def body(x_ref, raw_ref, dh_ref, dqa_ref, dkat_ref, dvat_ref, dga_ref, dqb_ref, dkbt_ref, dvbt_ref, dgb_ref,
             *refs):
        tab_refs, refs = refs[:8], refs[8:]
        (gin_ref, w_ref, wuq_ref, wuk_ref, wuv_ref, gq_ref, gk_ref, gcq_ref, gckv_ref, gqb_ref, gkb_ref, g64_ref,
         dx_ref, dproj_ref, dwuq_ref, dwuk_ref, dwuv_ref, small_ref) = refs
        i = pl.program_id(0)

        @pl.when(i == 0)
        def _():
            dwuq_ref[...] = jnp.zeros(dwuq_ref.shape, F32)
            dwuk_ref[...] = jnp.zeros(dwuk_ref.shape, F32)
            dwuv_ref[...] = jnp.zeros(dwuv_ref.shape, F32)
            small_ref[...] = jnp.zeros(small_ref.shape, F32)

        gin, gq, gk = gin_ref[...], gq_ref[...], gk_ref[...]
        gcq, gckv, gqb, gkb = gcq_ref[...], gckv_ref[...], gqb_ref[...], gkb_ref[...]
        w, wuq, wuk, wuv = w_ref[...], wuq_ref[...], wuk_ref[...], wuv_ref[...]
        for part in range(tm // ts):
            r_ = slice(ts * part, ts * (part + 1))
            dka_v, dva_v = dkat_ref[:, r_].T, dvat_ref[:, r_].T
            dkb_v, dvb_v = dkbt_ref[:, r_].T, dvbt_ref[:, r_].T
            ca, sa, cb, sb = _rope_tiles(tab_refs, i * (tm // ts) + part, ts)
            f = _pre_forward(x_ref[r_, :], gin, w, wuq, wuk, wuv, gq, gk, gcq, gckv, gqb, gkb,
                             ca, sa, cb, sb, g64_ref[...], ts, raw=raw_ref[r_, :])
            sel16, sel8, gs64 = f["sel16"], f["sel8"], f["gs64"]
            lane = lax.broadcasted_iota(jnp.int32, (ts, LANES), 1)
            low = lane < 64
            zero = jnp.zeros((ts, LANES), F32)
            pieces = []

            dgq = jnp.zeros((1, LANES), F32)
            for s in range(4):
                _, xh, r = f["qa"][s]
                d = dqa_ref[r_, LANES * s:LANES * (s + 1)] * QA_SCALE
                dx, dg = _col_bwd(d, xh, r, gs64, 64.0, gq, ca, sa, 16, sel16)
                pieces.append(dx)
                dgq = dgq + dg
            dgk = jnp.zeros((1, LANES), F32)
            for s in range(2):
                _, xh, r = f["ka"][s]
                d = dka_v[:, LANES * s:LANES * (s + 1)]
                d = d + pltpu.roll(d, 64, 1)
                dx, dg = _col_bwd(d, xh, r, _row_sum, 128.0, gk, ca, sa, 16, sel16)
                pieces.append(jnp.where(low, dx, zero))
                dgk = dgk + dg
            for s in range(2):
                d = dva_v[:, LANES * s:LANES * (s + 1)]
                d = d + pltpu.roll(d, 64, 1)
                pieces.append(jnp.where(low, d, zero))
            pieces.append(dga_ref[r_, :])

            dgqb = jnp.zeros((1, LANES), F32)
            dq_cols = []
            for h in range(B_HEADS):
                _, xh, r = f["qb"][h]
                dx, dg = _col_bwd(dqb_ref[r_, LANES * h:LANES * (h + 1)] * QB_SCALE, xh, r, _row_sum,
                                  float(B_QK_DIM), gqb, cb, sb, 8, sel8)
                dq_cols.append(dx)
                dgqb = dgqb + dg
            dqr_b = jnp.concatenate(dq_cols, axis=1).astype(BF16)
            dwuq_ref[...] += _dot_tn(f["cqb"], dqr_b)
            dcq_raw, dgcq = _rms_bwd(_dot_nt(dqr_b, wuq), f["cqh"], f["rcq"], gcq)
            pieces.append(dcq_raw)

            dgkb = jnp.zeros((1, LANES), F32)
            dk_cols = []
            dkr = zero
            for h in range(B_HEADS):
                _, xh, r = f["kb"][h]
                dx, dg = _col_bwd(dkb_v[:, LANES * h:LANES * (h + 1)], xh, r, _row_sum, float(B_QK_DIM),
                                  gkb, cb, sb, 8, sel8)
                dk_cols.append(dx)
                dkr = dkr + dx
                dgkb = dgkb + dg
            dkr_b = jnp.concatenate(dk_cols, axis=1).astype(BF16)
            dvb_b = dvb_v.astype(BF16)
            dwuk_ref[...] += _dot_tn(f["ckvb"], dkr_b)
            dwuv_ref[...] += _dot_tn(f["ckvb"], dvb_b)
            dckv = _dot_nt(dkr_b, wuk) + _dot_nt(dvb_b, wuv)
            dckv_raw, dgckv = _rms_bwd(dckv, f["ckvh"], f["rckv"], gckv)
            pieces.append(dckv_raw)
            pieces.append(jnp.where((lane >= B_NOPE_DIM) & (lane < B_QK_DIM), dkr, zero))
            pieces += [zero, zero, zero]
            pieces.append(dgb_ref[r_, :])

            dproj_b = jnp.concatenate(pieces, axis=1).astype(BF16)
            dproj_ref[r_, :] = dproj_b
            dxn = _dot_nt(dproj_b, w)
            dx, dgin = _rms_bwd(dxn, f["xh0"], f["r0"], gin)
            dx_ref[r_, :] = dx + dh_ref[r_, :]

            for c in range(D_MODEL // LANES):
                small_ref[c:c + 1, :] += dgin[:, LANES * c:LANES * (c + 1)]
            small_ref[8:9, :] += dgq
            small_ref[9:10, :] += dgk
            for c in range(3):
                small_ref[10 + c:11 + c, :] += dgcq[:, LANES * c:LANES * (c + 1)]
            for c in range(2):
                small_ref[13 + c:14 + c, :] += dgckv[:, LANES * c:LANES * (c + 1)]
            small_ref[15:16, :] += dgqb
            small_ref[16:17, :] += dgkb

    def rows(width):
        return pl.BlockSpec((tm, width), lambda i: (i, 0))

    def cols(height):
        return pl.BlockSpec((height, tm), lambda i: (0, i))

    return pl.pallas_call(
        body, name="pre_bwd", grid=(s_len // tm,),
        in_specs=[rows(D_MODEL), rows(R_WIDTH), rows(D_MODEL), rows(512), cols(256), cols(256), rows(512), rows(512),
                  cols(512), cols(512), rows(512)] + [_full(t.shape) for t in tabs] + _pre_const_specs(consts),
        out_specs=[rows(D_MODEL), rows(N_WIDE), _full((B_Q_RANK, 512)), _full((B_KV_RANK, 512)),
                   _full((B_KV_RANK, 512)), _full((R_SMALL, LANES))],
        out_shape=[jax.ShapeDtypeStruct((s_len, D_MODEL), F32), jax.ShapeDtypeStruct((s_len, N_WIDE), BF16),
                   jax.ShapeDtypeStruct((B_Q_RANK, 512), F32),
                   jax.ShapeDtypeStruct((B_KV_RANK, 512), F32), jax.ShapeDtypeStruct((B_KV_RANK, 512), F32),
                   jax.ShapeDtypeStruct((R_SMALL, LANES), F32)],
        compiler_params=_params(("arbitrary",)),
    )(x, raw, dh, dqa, dka, dva, dga, dqb, dkb, dvb, dgb, *tabs, *[consts[n] for n in _PRE_IN_NAMES])


def _dw_in_call(xnb, dproj_b, tt, tn):
    s_len = xnb.shape[0]
    n_t = s_len // tt

    def body(a_ref, b_ref, o_ref, acc_ref):
        t = pl.program_id(1)

        @pl.when(t == 0)
        def _():
            acc_ref[...] = jnp.zeros(acc_ref.shape, F32)

        acc_ref[...] += _dot_tn(a_ref[...], b_ref[...])

        @pl.when(t == n_t - 1)
        def _():
            o_ref[...] = acc_ref[...].astype(BF16)

    return pl.pallas_call(
        body, name="dw_in", grid=(N_WIDE // tn, n_t),
        in_specs=[pl.BlockSpec((tt, D_MODEL), lambda n, t: (t, 0)), pl.BlockSpec((tt, tn), lambda n, t: (t, n))],
        out_specs=pl.BlockSpec((D_MODEL, tn), lambda n, t: (0, n)),
        out_shape=jax.ShapeDtypeStruct((D_MODEL, N_WIDE), BF16),
        scratch_shapes=[pltpu.VMEM((D_MODEL, tn), F32)],
        compiler_params=_params(("arbitrary", "arbitrary")),
    )(xnb, dproj_b)


def _mesh_pos():
    return lax.axis_index("x"), lax.axis_index("y"), lax.axis_index("c")


def _flip(v, bit):
    return 1 - v if bit else v


def _peer(pos, k):
    x, y, c = pos
    return _flip(x, (k >> 2) & 1), _flip(y, (k >> 1) & 1), _flip(c, k & 1)


def _logical(p):
    return 4 * p[0] + 2 * p[1] + p[2]


def _adamw(w, g, m, v):
    m = ADAM_B1 * m + (1.0 - ADAM_B1) * g
    v = ADAM_B2 * v + (1.0 - ADAM_B2) * (g * g)
    m_hat = m / (1.0 - ADAM_B1 ** ADAM_STEP)
    v_hat = v / (1.0 - ADAM_B2 ** ADAM_STEP)
    delta = -ADAM_LR * (m_hat / (jnp.sqrt(v_hat) + ADAM_EPS) + ADAM_WD * w)
    return delta, m, v


W_BLOCKS = ((D_MODEL, N_IN // N_DEV), (B_Q_RANK // N_DEV, 384), (B_KV_RANK, 768 // N_DEV), (D_MODEL // N_DEV, D_MODEL))
N_W = len(W_BLOCKS)


def _gather_blocks_call(blocks):
    def body(*refs):
        x_refs, out_refs, xb_refs = refs[0:N_W], refs[N_W:2 * N_W], refs[2 * N_W:3 * N_W]
        send_sems, recv_sems, local_sems = refs[3 * N_W:]
        x, y, c = _mesh_pos()
        me, sibling = (x, y, c), (x, y, 1 - c)
        chips = [(1 - x, y), (x, 1 - y), (1 - x, 1 - y)]
        for w in range(N_W):
            xb_refs[w][...] = x_refs[w][...].astype(BF16)

        def slot(w, p):
            return out_refs[w].at[_logical(p)]

        def copy(w, k, block, to, src=None):
            return pltpu.make_async_remote_copy(
                src_ref=slot(w, block) if src is None else src, dst_ref=slot(w, block),
                send_sem=send_sems.at[N_W * k + w], recv_sem=recv_sems.at[N_W * k + w],
                device_id=to, device_id_type=pl.DeviceIdType.MESH)

        mine = [pltpu.make_async_copy(xb_refs[w], slot(w, me), local_sems.at[w]) for w in range(N_W)]
        for cp in mine:
            cp.start()
        first = [copy(w, 0, me, sibling, src=xb_refs[w]) for w in range(N_W)]
        first += [copy(w, 1 + j, me, (*chip, c), src=xb_refs[w]) for j, chip in enumerate(chips) for w in range(N_W)]
        for cp in first:
            cp.start()
        passed = []
        for j, chip in enumerate(chips):
            for w in range(N_W):
                copy(w, 1 + j, (*chip, c), me).wait_recv()
                fwd = copy(w, 4 + j, (*chip, c), sibling)
                fwd.start()
                passed.append(fwd)
        for w in range(N_W):
            copy(w, 0, sibling, me).wait_recv()
        for j, chip in enumerate(chips):
            for w in range(N_W):
                copy(w, 4 + j, (*chip, 1 - c), me).wait_recv()
        for cp in first + passed:
            cp.wait_send()
        for cp in mine:
            cp.wait()

    vm = pl.BlockSpec(memory_space=pltpu.VMEM)
    return pl.pallas_call(
        body, name="gather_weights",
        out_shape=[jax.ShapeDtypeStruct((N_DEV,) + shp, BF16) for shp in W_BLOCKS],
        in_specs=[vm] * N_W, out_specs=[vm] * N_W,
        scratch_shapes=[pltpu.VMEM(shp, BF16) for shp in W_BLOCKS]
        + [pltpu.SemaphoreType.DMA((7 * N_W,)), pltpu.SemaphoreType.DMA((7 * N_W,)), pltpu.SemaphoreType.DMA((N_W,))],
        compiler_params=pltpu.CompilerParams(vmem_limit_bytes=VMEM_LIMIT),
    )(*blocks)


def _reduce_two_level_call(parts, small, w_blk, m_blk, v_blk, w_s, m_s, v_s):
    chunks = (32, 48, 64, 16)
    n_chip = N_DEV // 2

    def body(*refs):
        p_refs = refs[0:4]
        small_ref = refs[4]
        w_refs, m_refs, v_refs = refs[5:9], refs[9:13], refs[13:17]
        ws_ref, ms_ref, vs_ref = refs[17:20]
        g_refs, d_refs, nm_refs, nv_refs = refs[20:24], refs[24:28], refs[28:32], refs[32:36]
        gs_ref, ds_ref, nms_ref, nvs_ref = refs[36:40]
        ra_refs, rb_refs, st_refs = refs[40:44], refs[44:48], refs[48:52]
        recv_s_ref = refs[52]
        send_a, recv_a, send_b, recv_b, send_s_sems, recv_s_sems = refs[53:59]
        pos = _mesh_pos()
        x, y, c = pos
        me = _logical(pos)
        sibling = (x, y, 1 - c)

        def chip(j):
            return _flip(x, j & 1), _flip(y, (j >> 1) & 1)

        def to_sibling(w, j):
            return pltpu.make_async_remote_copy(
                src_ref=p_refs[w].at[_logical((*chip(j), 1 - c))], dst_ref=ra_refs[w].at[j],
                send_sem=send_a.at[N_W * j + w], recv_sem=recv_a.at[N_W * j + w],
                device_id=sibling, device_id_type=pl.DeviceIdType.MESH)

        def to_chip(w, j):
            return pltpu.make_async_remote_copy(
                src_ref=st_refs[w].at[j - 1], dst_ref=rb_refs[w].at[j - 1],
                send_sem=send_b.at[N_W * (j - 1) + w], recv_sem=recv_b.at[N_W * (j - 1) + w],
                device_id=(*chip(j), c), device_id_type=pl.DeviceIdType.MESH)

        def tiny(k):
            return pltpu.make_async_remote_copy(
                src_ref=small_ref, dst_ref=recv_s_ref.at[k],
                send_sem=send_s_sems.at[k], recv_sem=recv_s_sems.at[k],
                device_id=_peer(pos, k), device_id_type=pl.DeviceIdType.MESH)

        order = (0, 3, 2, 1)
        for j in (1, 2, 3, 0):
            for w in order:
                to_sibling(w, j).start()
        for k in range(1, N_DEV):
            tiny(k).start()

        for j in (1, 2, 3):
            d = _logical((*chip(j), c))
            for w in order:
                to_sibling(w, j).wait_recv()
                chunk = chunks[w]

                def pair(t, carry, w=w, j=j, d=d, chunk=chunk):
                    rows = pl.ds(pl.multiple_of(t * chunk, chunk), chunk)
                    s = p_refs[w][d, rows, :].astype(F32) + ra_refs[w][j, rows, :].astype(F32)
                    st_refs[w][j - 1, rows, :] = s.astype(BF16)
                    return carry

                lax.fori_loop(0, W_BLOCKS[w][0] // chunk, pair, 0)
                to_chip(w, j).start()

        recv_s_ref[0] = small_ref[...]
        for k in range(1, N_DEV):
            tiny(k).wait_recv()
        acc = recv_s_ref[me]
        for a in range(1, N_DEV):
            acc = acc + recv_s_ref[lax.bitwise_xor(me, a)]
        row = lax.broadcasted_iota(jnp.int32, (R_SMALL, LANES), 0)
        gs = jnp.where(row == 8, acc + pltpu.roll(acc, 64, 1), acc)
        gs = jnp.where(row == ROW_LOSS, jnp.sum(acc, axis=1, keepdims=True) * (0.5 / D_MODEL), gs)
        gs_ref[...] = gs
        ds, nms, nvs = _adamw(ws_ref[...], gs, ms_ref[...], vs_ref[...])
        ds_ref[...] = ds
        nms_ref[...] = nms
        nvs_ref[...] = nvs

        for w in (1, 2, 3, 0):
            to_sibling(w, 0).wait_recv()
            for j in (1, 2, 3):
                to_chip(w, j).wait_recv()
            chunk = chunks[w]

            def step(t, carry, w=w, chunk=chunk):
                rows = pl.ds(pl.multiple_of(t * chunk, chunk), chunk)
                g = p_refs[w][me, rows, :].astype(F32) + ra_refs[w][0, rows, :].astype(F32)
                for j in range(n_chip - 1):
                    g = g + rb_refs[w][j, rows, :].astype(F32)
                d, nm, nv = _adamw(w_refs[w][rows, :], g, m_refs[w][rows, :], v_refs[w][rows, :])
                g_refs[w][rows, :] = g
                d_refs[w][rows, :] = d
                nm_refs[w][rows, :] = nm
                nv_refs[w][rows, :] = nv
                return carry

            lax.fori_loop(0, W_BLOCKS[w][0] // chunk, step, 0)
        for k in range(1, N_DEV):
            tiny(k).wait_send()
        for w in range(N_W):
            for j in range(n_chip):
                to_sibling(w, j).wait_send()
            for j in (1, 2, 3):
                to_chip(w, j).wait_send()

    vm = pl.BlockSpec(memory_space=pltpu.VMEM)
    blk = [jax.ShapeDtypeStruct(shp, F32) for shp in W_BLOCKS]
    small_shape = jax.ShapeDtypeStruct((R_SMALL, LANES), F32)
    return pl.pallas_call(
        body, name="reduce_adamw",
        in_specs=[vm] * 20, out_specs=[vm] * 20,
        out_shape=blk * 4 + [small_shape] * 4,
        scratch_shapes=[pltpu.VMEM((n_chip,) + shp, BF16) for shp in W_BLOCKS]
        + [pltpu.VMEM((n_chip - 1,) + shp, BF16) for shp in W_BLOCKS] * 2
        + [pltpu.VMEM((N_DEV, R_SMALL, LANES), F32),
           pltpu.SemaphoreType.DMA((n_chip * N_W,)), pltpu.SemaphoreType.DMA((n_chip * N_W,)),
           pltpu.SemaphoreType.DMA(((n_chip - 1) * N_W,)), pltpu.SemaphoreType.DMA(((n_chip - 1) * N_W,)),
           pltpu.SemaphoreType.DMA((N_DEV,)), pltpu.SemaphoreType.DMA((N_DEV,))],
        compiler_params=pltpu.CompilerParams(vmem_limit_bytes=VMEM_LIMIT),
    )(*parts, small, *w_blk, *m_blk, *v_blk, w_s, m_s, v_s)


def _pack_small(norm_in, a_q, a_k, b_cq, b_ckv, b_q, b_k):
    def row(v):
        return jnp.pad(v.reshape(1, -1), ((0, 0), (0, LANES - v.size)))
    rows = [norm_in.reshape(8, LANES), row(a_q), row(a_k), b_cq.reshape(3, LANES), b_ckv.reshape(2, LANES),
            row(b_q), row(b_k), jnp.zeros((R_SMALL - 17, LANES), F32)]
    return jnp.concatenate(rows, axis=0)


def _unpack_small(s):
    return (s[0:8].reshape(1, D_MODEL), s[8:9, :64], s[9:10, :64], s[10:13].reshape(1, B_Q_RANK),
            s[13:15].reshape(1, B_KV_RANK), s[15:16, :B_QK_DIM], s[16:17, :B_QK_DIM])


C_IN = N_IN // N_DEV
_RUNS = ((0, 512, O_QA), (512, 576, O_KA), (576, 640, O_KA + 128), (640, 704, O_VA), (704, 768, O_VA + 128),
         (768, 1280, O_GA), (1280, 1664, O_CQ), (1664, 1920, O_CKV), (1920, 1952, O_KR + 64), (1952, 2464, O_GB))


def _in_cols(g_in, lo, hi):
    out = []
    for d in range(N_DEV):
        a, b = max(lo, C_IN * d), min(hi, C_IN * (d + 1))
        if a < b:
            out.append(g_in[d][:, a - C_IN * d:b - C_IN * d])
    return out


def _widen_weights(g_in, g_uq, g_ukv, g_out):
    z64 = jnp.zeros((D_MODEL, 64), BF16)
    z32 = jnp.zeros((D_MODEL, 32), BF16)
    k0, k1 = _in_cols(g_in, 512, 576), _in_cols(g_in, 576, 640)
    v0, v1 = _in_cols(g_in, 640, 704), _in_cols(g_in, 704, 768)
    kr_blk = [z64] + _in_cols(g_in, 1920, 1952) + [z32]
    w_wide = jnp.concatenate(
        _in_cols(g_in, 0, 512) + k0 + k0 + k1 + k1 + v0 + v0 + v1 + v1 + _in_cols(g_in, 768, 1920)
        + kr_blk * B_HEADS + _in_cols(g_in, 1952, 2464), axis=1)
    w_uq = g_uq.reshape(B_Q_RANK, 384)
    wuq = jnp.pad(w_uq.reshape(B_Q_RANK, B_HEADS, B_QK_DIM), ((0, 0), (0, 0), (0, LANES - B_QK_DIM)))
    wuq = wuq.reshape(B_Q_RANK, 512)
    ukv = g_ukv.transpose(1, 0, 2).reshape(B_KV_RANK, B_HEADS, B_NOPE_DIM + B_V_DIM)
    wuk = jnp.pad(ukv[:, :, :B_NOPE_DIM], ((0, 0), (0, 0), (0, LANES - B_NOPE_DIM))).reshape(B_KV_RANK, 512)
    wuv = ukv[:, :, B_NOPE_DIM:].reshape(B_KV_RANK, 512)
    return w_wide, wuq, wuk, wuv, g_out.reshape(D_MODEL, D_MODEL)


def _grad_blocks(dw_wide, dwuq, dwuk, dwuv, dw_out):
    blocks = []
    for d in range(N_DEV):
        pieces = []
        for lo, hi, wide in _RUNS:
            a, b = max(lo, C_IN * d), min(hi, C_IN * (d + 1))
            if a < b:
                pieces.append(dw_wide[:, wide + a - lo:wide + b - lo])
        blocks.append(jnp.concatenate(pieces, axis=1))
    p_in = jnp.stack(blocks).astype(BF16)
    dw_uq = dwuq.reshape(B_Q_RANK, B_HEADS, LANES)[:, :, :B_QK_DIM].reshape(N_DEV, B_Q_RANK // N_DEV, 384)
    dk = dwuk.reshape(B_KV_RANK, B_HEADS, LANES)[:, :, :B_NOPE_DIM]
    dv = dwuv.reshape(B_KV_RANK, B_HEADS, B_V_DIM)
    dw_ukv = jnp.concatenate([dk, dv], axis=2).reshape(B_KV_RANK, N_DEV, 768 // N_DEV).transpose(1, 0, 2)
    return (p_in, dw_uq.astype(BF16), dw_ukv.astype(BF16),
            dw_out.reshape(N_DEV, D_MODEL // N_DEV, D_MODEL).astype(BF16))


def _rope_tables(s_len):
    row = jnp.arange(s_len // GRID_W, dtype=F32)
    col = jnp.arange(GRID_W, dtype=F32)

    def parts(dim):
        half = dim // 2
        inv = 1.0 / (ROPE_THETA ** (jnp.arange(0, half, 2, dtype=F32) / half))
        ar, ac = row[:, None] * inv[None, :], col[:, None] * inv[None, :]
        zr, zc = jnp.zeros_like(ar), jnp.zeros_like(ac)
        cos_c = jnp.concatenate([zc, zc, jnp.cos(ac), jnp.cos(ac)], axis=1)
        cos_r = jnp.concatenate([jnp.cos(ar), jnp.cos(ar), zr, zr], axis=1)
        sin_c = jnp.concatenate([zc, zc, -jnp.sin(ac), jnp.sin(ac)], axis=1)
        sin_r = jnp.concatenate([-jnp.sin(ar), jnp.sin(ar), zr, zr], axis=1)
        return cos_c, cos_r, sin_c, sin_r

    tabs = [jnp.tile(t, (1, 2)) for t in parts(A_HEAD_DIM)]
    for n, t in enumerate(parts(B_ROPE_DIM)):
        lead = jnp.full((t.shape[0], B_NOPE_DIM), 1.0 if n == 0 else 0.0, F32)
        tail = jnp.full((t.shape[0], LANES - B_QK_DIM), 1.0 if n == 0 else 0.0, F32)
        tabs.append(jnp.concatenate([lead, t, tail], axis=1))
    return tuple(tabs)


def kernel(x, norm_in, w_in, a_q_norm, a_k_norm, b_cq_norm, b_ckv_norm, w_uq, w_ukv, b_q_norm, b_k_norm, w_out, loss_target, m_norm_in, m_w_in, m_a_q_norm, m_a_k_norm, m_b_cq_norm, m_b_ckv_norm, m_w_uq, m_w_ukv, m_b_q_norm, m_b_k_norm, m_w_out, v_norm_in, v_w_in, v_a_q_norm, v_a_k_norm, v_b_cq_norm, v_b_ckv_norm, v_w_uq, v_w_ukv, v_b_q_norm, v_b_k_norm, v_w_out):
    s_len = x.shape[1]
    tm = min(256, s_len)
    tq, tk = min(512, s_len), min(2048, s_len)
    ftq, ftk = min(256, s_len), min(256, s_len)
    x2 = x.reshape(s_len, D_MODEL)
    t2 = loss_target.reshape(s_len, D_MODEL)

    w_blk = (w_in[0], w_uq[0], w_ukv[0], w_out[0])
    w_wide, wuq, wuk, wuv, wout = _widen_weights(*_gather_blocks_call(w_blk))

    def dup(v, pad_to=None):
        v = v.reshape(1, -1)
        if pad_to is None:
            return jnp.concatenate([v, v], axis=1)
        return jnp.pad(v, ((0, 0), (0, pad_to - v.shape[1])))

    g64 = jnp.asarray(np.kron(np.eye(2), np.ones((64, 64))), dtype=BF16)
    consts = dict(gin=norm_in, w=w_wide, wuq=wuq, wuk=wuk, wuv=wuv, gq=dup(a_q_norm), gk=dup(a_k_norm),
                  gcq=b_cq_norm, gckv=b_ckv_norm, gqb=dup(b_q_norm, LANES), gkb=dup(b_k_norm, LANES), g64=g64)
    tabs = _rope_tables(s_len)

    qa, ka, va, va_t, ga, qb, kb, vb, vb_t, gb, raw, xnb = _pre_fwd_call(x2, consts, tabs, min(512, s_len))
    oa, lse_a_t = _attn_fwd_t_call(qa, ka, va_t, groups=A_KV_HEADS, sub=4, masked=True, scale=None, tq=ftq, tk=ftk,
                                   name="attn_fwd_a")
    ob, lse_b_t = _attn_fwd_t_call(qb, kb, vb_t, groups=1, sub=B_HEADS, masked=False, scale=None, tq=ftq,
                                   tk=ftk, name="attn_fwd_b")
    dh, doa, dob, dga, dgb, dw_out, loss_row = _out_call(x2, t2, oa, ob, ga, gb, wout, min(512, s_len))

    dqa, dka_t, dva_t = _attn_bwd_q_call(qa, ka, va, doa, oa, lse_a_t, groups=A_KV_HEADS, sub=4, masked=True,
                                         tq=tq, tk=min(4096, s_len), ck=min(512, s_len), name="attn_bwd_a")
    dqb, dkb_t, dvb_t = _attn_bwd_q_call(qb, kb, vb, dob, ob, lse_b_t, groups=2, sub=2, masked=False,
                                         tq=tq, tk=min(4096, s_len), ck=min(256, s_len), name="attn_bwd_b")
    dx, dproj_b, dwuq, dwuk, dwuv, small = _pre_bwd_call(
        x2, raw, dh, dqa, dka_t, dva_t, dga, dqb, dkb_t, dvb_t, dgb, consts, tabs, tm)
    dw_wide = _dw_in_call(xnb, dproj_b, min(1024, s_len), N_WIDE)

    parts = _grad_blocks(dw_wide, dwuq, dwuk, dwuv, dw_out)
    small = jnp.concatenate([small[:ROW_LOSS], loss_row, small[ROW_LOSS + 1:]], axis=0)

    m_blk = (m_w_in[0], m_w_uq[0], m_w_ukv[0], m_w_out[0])
    v_blk = (v_w_in[0], v_w_uq[0], v_w_ukv[0], v_w_out[0])
    w_s = _pack_small(norm_in, a_q_norm, a_k_norm, b_cq_norm, b_ckv_norm, b_q_norm, b_k_norm)
    m_s = _pack_small(m_norm_in, m_a_q_norm, m_a_k_norm, m_b_cq_norm, m_b_ckv_norm, m_b_q_norm, m_b_k_norm)
    v_s = _pack_small(v_norm_in, v_a_q_norm, v_a_k_norm, v_b_cq_norm, v_b_ckv_norm, v_b_q_norm, v_b_k_norm)
    res = _reduce_two_level_call(parts, small, w_blk, m_blk, v_blk, w_s, m_s, v_s)

    def leaves(blocks, sm):
        wi, uq, ukv, wo = [b[None] for b in blocks]
        n_in, aq, ak, bcq, bckv, bq, bk = _unpack_small(sm)
        return [n_in, wi, aq, ak, bcq, bckv, uq, ukv, bq, bk, wo]

    g_s = res[16]
    loss = g_s[ROW_LOSS, 0]
    grad_x = dx.reshape(1, s_len, D_MODEL)
    return (loss, grad_x, *leaves(res[0:4], res[16]), *leaves(res[4:8], res[17]), *leaves(res[8:12], res[18]),
            *leaves(res[12:16], res[19]))
```

```python
import functools

import numpy as np
import jax
import jax.numpy as jnp
from jax import lax
from jax.experimental import pallas as pl
from jax.experimental.pallas import tpu as pltpu

F32 = jnp.float32
BF16 = jnp.bfloat16

D_MODEL = 1024
GRID_W = 64
ROPE_THETA = 10000.0
EPS = 1e-6
A_HEAD_DIM = 64
A_HEADS = 8
A_KV_HEADS = 2
B_HEADS = 4
B_NOPE_DIM = 64
B_ROPE_DIM = 32
B_QK_DIM = 96
B_V_DIM = 128
B_Q_RANK = 384
B_KV_RANK = 256
N_IN = 2464
N_DEV = 8

ADAM_LR = 0.001
ADAM_B1 = 0.9
ADAM_B2 = 0.999
ADAM_EPS = 1e-08
ADAM_WD = 0.01
ADAM_STEP = 10

QA_SCALE = 0.125
QB_SCALE = 1.0 / float(np.sqrt(B_QK_DIM))

LANES = 128
O_QA, O_KA, O_VA, O_GA, O_CQ, O_CKV, O_KR, O_GB, N_WIDE = 0, 512, 768, 1024, 1536, 1920, 2176, 2688, 3200
R_QA, R_KA, R_CQ, R_CKV, R_KR, R_WIDTH = 0, 512, 768, 1152, 1408, 1920

R_SMALL = 24
ROW_LOSS = 17

VMEM_LIMIT = 56 * 1024 * 1024

NT = (((1,), (1,)), ((), ()))
TN = (((0,), (0,)), ((), ()))


def _dot(a, b):
    return jnp.dot(a, b, preferred_element_type=F32)


def _dot_nt(a, b):
    return lax.dot_general(a, b, NT, preferred_element_type=F32)


def _dot_tn(a, b):
    return lax.dot_general(a, b, TN, preferred_element_type=F32)


def _params(sem=None):
    return pltpu.CompilerParams(dimension_semantics=sem, vmem_limit_bytes=VMEM_LIMIT)


def _full(shape):
    nd = len(shape)
    return pl.BlockSpec(shape, lambda *_: (0,) * nd)


def _swap_sel(rows, shift):
    lane = lax.broadcasted_iota(jnp.int32, (rows, LANES), 1)
    return pltpu.roll(lane, shift, 1) == (lane ^ shift)


def _swap(x, shift, sel):
    return jnp.where(sel, pltpu.roll(x, shift, 1), pltpu.roll(x, LANES - shift, 1))


def _group_sum64(x, g64):
    hi = x.astype(BF16)
    lo = (x - hi.astype(F32)).astype(BF16)
    return _dot(hi, g64) + _dot(lo, g64)


def _row_sum(x):
    return jnp.sum(x, axis=-1, keepdims=True)


def _col_fwd(xs, msum, denom, gain, cos, sin, shift, sel):
    r = lax.rsqrt(msum(xs * xs) * (1.0 / denom) + EPS)
    xh = xs * r
    n = xh * gain
    return n * cos + _swap(n, shift, sel) * sin, xh, r


def _col_bwd(d_out, xh, r, msum, denom, gain, cos, sin, shift, sel):
    dn = d_out * cos + _swap(d_out * sin, shift, sel)
    dgain = jnp.sum(dn * xh, axis=0, keepdims=True)
    dxh = dn * gain
    dx = r * (dxh - xh * (msum(dxh * xh) * (1.0 / denom)))
    return dx, dgain


def _rms_fwd(x, gain):
    r = lax.rsqrt(jnp.mean(x * x, axis=-1, keepdims=True) + EPS)
    xh = x * r
    return xh * gain, xh, r


def _rms_bwd(dy, xh, r, gain):
    dgain = jnp.sum(dy * xh, axis=0, keepdims=True)
    dxh = dy * gain
    dx = r * (dxh - xh * jnp.mean(dxh * xh, axis=-1, keepdims=True))
    return dx, dgain


def _pre_forward(x, gin, w, wuq, wuk, wuv, gq, gk, gcq, gckv, gqb, gkb, ca, sa, cb, sb, g64, tm, raw=None):
    sel16 = _swap_sel(tm, 16)
    sel8 = _swap_sel(tm, 8)
    xn, xh0, r0 = _rms_fwd(x, gin)
    xnb = xn.astype(BF16)
    proj = None
    if raw is None:
        proj = _dot(xnb, w)
        raw = jnp.concatenate([proj[:, O_QA:O_QA + 512], proj[:, O_KA:O_KA + 256], proj[:, O_CQ:O_CQ + B_Q_RANK],
                               proj[:, O_CKV:O_CKV + B_KV_RANK], proj[:, O_KR:O_KR + 512]], axis=1)
    gs64 = functools.partial(_group_sum64, g64=g64)
    qa = [_col_fwd(raw[:, R_QA + LANES * s:R_QA + LANES * (s + 1)], gs64, 64.0, gq, ca, sa, 16, sel16)
          for s in range(4)]
    ka = [_col_fwd(raw[:, R_KA + LANES * s:R_KA + LANES * (s + 1)], _row_sum, 128.0, gk, ca, sa, 16, sel16)
          for s in range(2)]
    cq, cqh, rcq = _rms_fwd(raw[:, R_CQ:R_CQ + B_Q_RANK], gcq)
    cqb = cq.astype(BF16)
    qb_raw = _dot(cqb, wuq)
    qb = [_col_fwd(qb_raw[:, LANES * h:LANES * (h + 1)], _row_sum, float(B_QK_DIM), gqb, cb, sb, 8, sel8)
          for h in range(B_HEADS)]
    ckv, ckvh, rckv = _rms_fwd(raw[:, R_CKV:R_CKV + B_KV_RANK], gckv)
    ckvb = ckv.astype(BF16)
    kb_raw = _dot(ckvb, wuk) + raw[:, R_KR:R_KR + 512]
    vb = _dot(ckvb, wuv)
    kb = [_col_fwd(kb_raw[:, LANES * h:LANES * (h + 1)], _row_sum, float(B_QK_DIM), gkb, cb, sb, 8, sel8)
          for h in range(B_HEADS)]
    return dict(xh0=xh0, r0=r0, xnb=xnb, proj=proj, raw=raw, qa=qa, ka=ka, cqh=cqh, rcq=rcq, cqb=cqb, qb=qb,
                ckvh=ckvh, rckv=rckv, ckvb=ckvb, kb=kb, vb=vb, sel16=sel16, sel8=sel8, gs64=gs64)


def _rope_tiles(tab_refs, i, tm):
    per_tile = tm // GRID_W
    out = []
    for t in range(4):
        col_ref, row_ref = tab_refs[2 * t], tab_refs[2 * t + 1]
        col = col_ref[...]
        out.append(jnp.concatenate([col + row_ref[pl.ds(i * per_tile + b, 1), :] for b in range(per_tile)], axis=0))
    return out


_PRE_IN_NAMES = ("gin", "w", "wuq", "wuk", "wuv", "gq", "gk", "gcq", "gckv", "gqb", "gkb", "g64")


def _pre_const_specs(consts):
    return [_full(consts[n].shape) for n in _PRE_IN_NAMES]


def _pre_fwd_call(x, consts, tabs, tm):
    s_len = x.shape[0]
    ts = min(256, tm)

    def body(x_ref, *refs):
        tab_refs, refs = refs[:8], refs[8:]
        (gin_ref, w_ref, wuq_ref, wuk_ref, wuv_ref, gq_ref, gk_ref, gcq_ref, gckv_ref, gqb_ref, gkb_ref, g64_ref,
         qa_ref, ka_ref, va_ref, vat_ref, ga_ref, qb_ref, kb_ref, vb_ref, vbt_ref, gb_ref, raw_ref, xnb_ref) = refs
        for part in range(tm // ts):
            r = slice(ts * part, ts * (part + 1))
            ca, sa, cb, sb = _rope_tiles(tab_refs, pl.program_id(0) * (tm // ts) + part, ts)
            f = _pre_forward(x_ref[r, :], gin_ref[...], w_ref[...], wuq_ref[...], wuk_ref[...], wuv_ref[...],
                             gq_ref[...], gk_ref[...], gcq_ref[...], gckv_ref[...], gqb_ref[...], gkb_ref[...],
                             ca, sa, cb, sb, g64_ref[...], ts)
            proj = f["proj"]
            raw_ref[r, :] = f["raw"]
            xnb_ref[r, :] = f["xnb"]
            for s in range(4):
                qa_ref[r, LANES * s:LANES * (s + 1)] = (f["qa"][s][0] * QA_SCALE).astype(BF16)
            for s in range(2):
                ka_ref[r, LANES * s:LANES * (s + 1)] = f["ka"][s][0].astype(BF16)
            va = proj[:, O_VA:O_VA + 256]
            va_ref[r, :] = va.astype(BF16)
            vat_ref[:, r] = va.T.astype(BF16)
            ga_ref[r, :] = proj[:, O_GA:O_GA + 512]
            for h in range(B_HEADS):
                qb_ref[r, LANES * h:LANES * (h + 1)] = (f["qb"][h][0] * QB_SCALE).astype(BF16)
                kb_ref[r, LANES * h:LANES * (h + 1)] = f["kb"][h][0].astype(BF16)
            vb_ref[r, :] = f["vb"].astype(BF16)
            vbt_ref[:, r] = f["vb"].T.astype(BF16)
            gb_ref[r, :] = proj[:, O_GB:O_GB + 512]

    def rows(width):
        return pl.BlockSpec((tm, width), lambda i: (i, 0))

    def cols(height):
        return pl.BlockSpec((height, tm), lambda i: (0, i))

    outs = [((s_len, 512), BF16, rows(512)), ((s_len, 256), BF16, rows(256)), ((s_len, 256), BF16, rows(256)),
            ((256, s_len), BF16, cols(256)), ((s_len, 512), F32, rows(512)), ((s_len, 512), BF16, rows(512)),
            ((s_len, 512), BF16, rows(512)), ((s_len, 512), BF16, rows(512)), ((512, s_len), BF16, cols(512)),
            ((s_len, 512), F32, rows(512)), ((s_len, R_WIDTH), F32, rows(R_WIDTH)),
            ((s_len, D_MODEL), BF16, rows(D_MODEL))]
    return pl.pallas_call(
        body, name="pre_fwd", grid=(s_len // tm,),
        in_specs=[rows(D_MODEL)] + [_full(t.shape) for t in tabs] + _pre_const_specs(consts),
        out_specs=[sp for _, _, sp in outs],
        out_shape=[jax.ShapeDtypeStruct(sh, dt) for sh, dt, _ in outs],
        compiler_params=_params(("arbitrary",)),
    )(x, *tabs, *[consts[n] for n in _PRE_IN_NAMES])


def _head_masks(rows):
    lane = lax.broadcasted_iota(jnp.int32, (rows, LANES), 1)
    return lane < 64, lane >= 64


def _row_fold(x, op):
    return op(x.reshape(x.shape[0] // 8, 8, x.shape[1]), axis=0)


def _attn_fwd_t_call(q, k, vt, *, groups, sub, masked, scale, tq, tk, name):
    s_len = q.shape[0]
    qw = LANES * (sub // 2 if masked else sub)
    kvw = LANES if masked else LANES * sub
    n_c = s_len // tk
    kv_mode = pl.Buffered(1) if groups == 1 else None
    nt = 2 if s_len % (2 * tq) == 0 else 1

    def body(q_ref, qn_ref, k_ref, vt_ref, o_ref, lse_ref, s_sc, mx_sc):
        keep = _head_masks(tq) if masked else None

        def kv_of(hh):
            return slice(0, LANES) if masked else slice(LANES * hh, LANES * (hh + 1))

        def q_of(ref, t, hh):
            r = slice(tq * t, tq * (t + 1))
            if not masked:
                return ref[r, LANES * hh:LANES * (hh + 1)]
            qp = ref[r, LANES * (hh // 2):LANES * (hh // 2 + 1)]
            return jnp.where(keep[hh % 2], qp, jnp.zeros_like(qp))

        def scores(hh, qm, c, mx):
            s_t = _dot_nt(k_ref[tk * c:tk * (c + 1), kv_of(hh)], qm)
            if scale is not None:
                s_t = s_t * scale
            s_sc[hh % 2, c] = s_t
            return jnp.maximum(mx, _row_fold(s_t, jnp.max))

        neg = jnp.full((8, tq), -jnp.inf, F32)

        @pl.when(pl.program_id(1) == 0)
        def _():
            qm0 = q_of(q_ref, 0, 0)
            mx0 = neg
            for c in range(n_c):
                mx0 = scores(0, qm0, c, mx0)
            mx_sc[...] = mx0

        mx_next = mx_sc[...]
        for t in range(nt):
            r = slice(tq * t, tq * (t + 1))
            outs = []
            for hh in range(sub):
                m = jnp.max(mx_next, axis=0, keepdims=True)
                nxt = (hh + 1) % sub
                if hh + 1 < sub:
                    qm_next = q_of(q_ref, t, nxt)
                else:
                    qm_next = q_of(q_ref, t + 1, 0) if t + 1 < nt else q_of(qn_ref, 0, 0)
                mx_next = neg
                lsum = jnp.zeros((8, tq), F32)
                acc = jnp.zeros((LANES, tq), F32)
                for c in range(n_c):
                    p_t = jnp.exp(s_sc[hh % 2, c] - m)
                    lsum = lsum + _row_fold(p_t, jnp.sum)
                    mx_next = scores(nxt, qm_next, c, mx_next)
                    acc = acc + _dot(vt_ref[kv_of(hh), tk * c:tk * (c + 1)], p_t.astype(BF16))
                l = jnp.sum(lsum, axis=0, keepdims=True)
                outs.append((acc / l).T)
                lse_ref[hh, :, r] = m + jnp.log(l)
            if masked:
                for pr in range(sub // 2):
                    o_ref[r, LANES * pr:LANES * (pr + 1)] = jnp.where(keep[0], outs[2 * pr], outs[2 * pr + 1])
            else:
                for hh in range(sub):
                    o_ref[r, LANES * hh:LANES * (hh + 1)] = outs[hh]
        mx_sc[...] = mx_next

    tb = nt * tq
    n_q = s_len // tb
    return pl.pallas_call(
        body, name=name, grid=(groups, n_q),
        in_specs=[pl.BlockSpec((tb, qw), lambda g, i: (i, g)),
                  pl.BlockSpec((tb, qw), lambda g, i: (jnp.minimum(i + 1, n_q - 1), g)),
                  pl.BlockSpec((s_len, kvw), lambda g, i: (0, g), pipeline_mode=kv_mode),
                  pl.BlockSpec((kvw, s_len), lambda g, i: (g, 0), pipeline_mode=kv_mode)],
        out_specs=[pl.BlockSpec((tb, qw), lambda g, i: (i, g)),
                   pl.BlockSpec((sub, 1, tb), lambda g, i: (g, 0, i))],
        out_shape=[jax.ShapeDtypeStruct((s_len, groups * qw), F32),
                   jax.ShapeDtypeStruct((groups * sub, 1, s_len), F32)],
        scratch_shapes=[pltpu.VMEM((2, n_c, tk, tq), F32), pltpu.VMEM((8, tq), F32)],
        compiler_params=_params(("arbitrary", "arbitrary")),
    )(q, q, k, vt)


def _attn_bwd_q_call(q, k, v, do, o, lse_t, *, groups, sub, masked, tq, tk, ck, name):
    s_len = q.shape[0]
    qw = LANES * (sub // 2 if masked else sub)
    kvw = LANES if masked else LANES * sub
    n_c = tk // ck

    def body(q_ref, k_ref, v_ref, do_ref, o_ref, lse_ref, dq_ref, dkt_ref, dvt_ref):
        j = pl.program_id(1)
        i = pl.program_id(2)

        @pl.when((j == 0) & (i == 0))
        def _():
            dq_ref[...] = jnp.zeros(dq_ref.shape, F32)

        @pl.when(i == 0)
        def _():
            dkt_ref[...] = jnp.zeros(dkt_ref.shape, F32)
            dvt_ref[...] = jnp.zeros(dvt_ref.shape, F32)

        lkeep = _head_masks(tq) if masked else None
        heads = []
        for hh in range(sub):
            if masked:
                cols = slice(LANES * (hh // 2), LANES * (hh // 2 + 1))
                kv = slice(0, LANES)
                qp, dop = q_ref[:, cols], do_ref[:, cols]
                qm = jnp.where(lkeep[hh % 2], qp, jnp.zeros_like(qp))
                dom = jnp.where(lkeep[hh % 2], dop, jnp.zeros_like(dop))
            else:
                cols = kv = slice(LANES * hh, LANES * (hh + 1))
                qm, dom = q_ref[:, cols], do_ref[:, cols]
            delta = jnp.sum(dom.astype(F32) * o_ref[:, cols], axis=1, keepdims=True)
            lse = jnp.broadcast_to(lse_ref[hh], (LANES, tq)).T[:, 0:1]
            heads.append((cols, kv, qm, dom, qm.T, dom.T, delta, lse))

        def products(hh, c):
            _, kv, qm, dom, _, _, _, _ = heads[hh]
            return _dot_nt(qm, k_ref[ck * c:ck * (c + 1), kv]), _dot_nt(dom, v_ref[ck * c:ck * (c + 1), kv])

        items = [(hh, c) for hh in range(sub) for c in range(n_c)]
        dq_acc = [jnp.zeros((tq, LANES), F32) for _ in range(sub)]
        nxt = products(*items[0])
        for n, (hh, c) in enumerate(items):
            s, dp = nxt
            if n + 1 < len(items):
                nxt = products(*items[n + 1])
            _, kv, qm, dom, qmt, domt, delta, lse = heads[hh]
            p = jnp.exp(s - lse)
            ds = p * (dp - delta)
            p_b = p.astype(BF16)
            ds_b = ds.astype(BF16)
            kcols = slice(ck * c, ck * (c + 1))
            dvt_ref[kv, kcols] += _dot(domt, p_b)
            dkt_ref[kv, kcols] += _dot(qmt, ds_b)
            dq_acc[hh] = dq_acc[hh] + _dot(ds_b, k_ref[kcols, kv])
        rows = pl.ds(pl.multiple_of(i * tq, tq), tq)
        if masked:
            for pr in range(sub // 2):
                dq_ref[rows, LANES * pr:LANES * (pr + 1)] += jnp.where(lkeep[0], dq_acc[2 * pr], dq_acc[2 * pr + 1])
        else:
            for hh in range(sub):
                dq_ref[rows, LANES * hh:LANES * (hh + 1)] += dq_acc[hh]

    return pl.pallas_call(
        body, name=name, grid=(groups, s_len // tk, s_len // tq),
        in_specs=[pl.BlockSpec((tq, qw), lambda g, j, i: (i, g)),
                  pl.BlockSpec((tk, kvw), lambda g, j, i: (j, g)),
                  pl.BlockSpec((tk, kvw), lambda g, j, i: (j, g)),
                  pl.BlockSpec((tq, qw), lambda g, j, i: (i, g)),
                  pl.BlockSpec((tq, qw), lambda g, j, i: (i, g)),
                  pl.BlockSpec((sub, 1, tq), lambda g, j, i: (g, 0, i))],
        out_specs=[pl.BlockSpec((s_len, qw), lambda g, j, i: (0, g)),
                   pl.BlockSpec((kvw, tk), lambda g, j, i: (g, j)),
                   pl.BlockSpec((kvw, tk), lambda g, j, i: (g, j))],
        out_shape=[jax.ShapeDtypeStruct((s_len, groups * qw), F32),
                   jax.ShapeDtypeStruct((groups * kvw, s_len), F32),
                   jax.ShapeDtypeStruct((groups * kvw, s_len), F32)],
        compiler_params=_params(("arbitrary", "arbitrary", "arbitrary")),
    )(q, k, v, do, o, lse_t)


def _silu_parts(g):
    sig = 1.0 / (1.0 + jnp.exp(-g))
    return g * sig, sig * (1.0 + g * (1.0 - sig))


def _out_call(x, target, oa, ob, ga, gb, wout, tm):
    s_len = x.shape[0]
    n_t = s_len // tm

    def body(x_ref, t_ref, oa_ref, ob_ref, ga_ref, gb_ref, w_ref,
             dh_ref, doa_ref, dob_ref, dga_ref, dgb_ref, dwb_ref, loss_ref, dw_ref):
        i = pl.program_id(0)

        @pl.when(i == 0)
        def _():
            dw_ref[...] = jnp.zeros(dw_ref.shape, F32)
            loss_ref[...] = jnp.zeros(loss_ref.shape, F32)

        oa_v, ob_v = oa_ref[...], ob_ref[...]
        silu_a, dsilu_a = _silu_parts(ga_ref[...])
        silu_b, dsilu_b = _silu_parts(gb_ref[...])
        ya = (oa_v * silu_a).astype(BF16)
        yb = (ob_v * silu_b).astype(BF16)
        h = x_ref[...] + _dot(ya, w_ref[0:512, :]) + _dot(yb, w_ref[512:1024, :])
        err = h - t_ref[...]
        part = jnp.sum(err * err, axis=0, keepdims=True)
        acc = part[:, 0:LANES]
        for c in range(1, D_MODEL // LANES):
            acc = acc + part[:, LANES * c:LANES * (c + 1)]
        loss_ref[...] += acc
        dh = err * (1.0 / D_MODEL)
        dh_ref[...] = dh
        dhb = dh.astype(BF16)
        dya = _dot_nt(dhb, w_ref[0:512, :])
        dyb = _dot_nt(dhb, w_ref[512:1024, :])
        doa = dya * silu_a
        dob = dyb * silu_b
        doa_ref[...] = doa.astype(BF16)
        dob_ref[...] = dob.astype(BF16)
        dga_ref[...] = dya * oa_v * dsilu_a
        dgb_ref[...] = dyb * ob_v * dsilu_b
        dw_ref[0:512, :] += _dot_tn(ya, dhb)
        dw_ref[512:1024, :] += _dot_tn(yb, dhb)

        @pl.when(i == n_t - 1)
        def _():
            dwb_ref[...] = dw_ref[...].astype(BF16)

    def rows(width):
        return pl.BlockSpec((tm, width), lambda i: (i, 0))

    outs = [(D_MODEL, F32), (512, BF16), (512, BF16), (512, F32), (512, F32)]
    return pl.pallas_call(
        body, name="out_fwd", grid=(n_t,),
        in_specs=[rows(D_MODEL), rows(D_MODEL), rows(512), rows(512), rows(512), rows(512),
                  _full((D_MODEL, D_MODEL))],
        out_specs=[rows(wd) for wd, _ in outs] + [_full((D_MODEL, D_MODEL)), _full((1, LANES))],
        out_shape=[jax.ShapeDtypeStruct((s_len, wd), dt) for wd, dt in outs]
        + [jax.ShapeDtypeStruct((D_MODEL, D_MODEL), BF16), jax.ShapeDtypeStruct((1, LANES), F32)],
        scratch_shapes=[pltpu.VMEM((D_MODEL, D_MODEL), F32)],
        compiler_params=_params(("arbitrary",)),
    )(x, target, oa, ob, ga, gb, wout)


def _pre_bwd_call(x, raw, dh, dqa, dka, dva, dga, dqb, dkb, dvb, dgb, consts, tabs, tm):
    s_len = x.shape[0]
    ts = min(256, tm)

    def body(x_ref, raw_ref, dh_ref, dqa_ref, dkat_ref, dvat_ref, dga_ref, dqb_ref, dkbt_ref, dvbt_ref, dgb_ref,
             *refs):
        tab_refs, refs = refs[:8], refs[8:]
        (gin_ref, w_ref, wuq_ref, wuk_ref, wuv_ref, gq_ref, gk_ref, gcq_ref, gckv_ref, gqb_ref, gkb_ref, g64_ref,
         dx_ref, dproj_ref, dwuq_ref, dwuk_ref, dwuv_ref, small_ref) = refs
        i = pl.program_id(0)

        @pl.when(i == 0)
        def _():
            dwuq_ref[...] = jnp.zeros(dwuq_ref.shape, F32)
            dwuk_ref[...] = jnp.zeros(dwuk_ref.shape, F32)
            dwuv_ref[...] = jnp.zeros(dwuv_ref.shape, F32)
            small_ref[...] = jnp.zeros(small_ref.shape, F32)

        gin, gq, gk = gin_ref[...], gq_ref[...], gk_ref[...]
        gcq, gckv, gqb, gkb = gcq_ref[...], gckv_ref[...], gqb_ref[...], gkb_ref[...]
        w, wuq, wuk, wuv = w_ref[...], wuq_ref[...], wuk_ref[...], wuv_ref[...]
        for part in range(tm // ts):
            r_ = slice(ts * part, ts * (part + 1))
            dka_v, dva_v = dkat_ref[:, r_].T, dvat_ref[:, r_].T
            dkb_v, dvb_v = dkbt_ref[:, r_].T, dvbt_ref[:, r_].T
            ca, sa, cb, sb = _rope_tiles(tab_refs, i * (tm // ts) + part, ts)
            f = _pre_forward(x_ref[r_, :], gin, w, wuq, wuk, wuv, gq, gk, gcq, gckv, gqb, gkb,
                             ca, sa, cb, sb, g64_ref[...], ts, raw=raw_ref[r_, :])
            sel16, sel8, gs64 = f["sel16"], f["sel8"], f["gs64"]
            lane = lax.broadcasted_iota(jnp.int32, (ts, LANES), 1)
            low = lane < 64
            zero = jnp.zeros((ts, LANES), F32)
            pieces = []

            dgq = jnp.zeros((1, LANES), F32)
            for s in range(4):
                _, xh, r = f["qa"][s]
                d = dqa_ref[r_, LANES * s:LANES * (s + 1)] * QA_SCALE
                dx, dg = _col_bwd(d, xh, r, gs64, 64.0, gq, ca, sa, 16, sel16)
                pieces.append(dx)
                dgq = dgq + dg
            dgk = jnp.zeros((1, LANES), F32)
            for s in range(2):
                _, xh, r = f["ka"][s]
                d = dka_v[:, LANES * s:LANES * (s + 1)]
                d = d + pltpu.roll(d, 64, 1)
                dx, dg = _col_bwd(d, xh, r, _row_sum, 128.0, gk, ca, sa, 16, sel16)
                pieces.append(jnp.where(low, dx, zero))
                dgk = dgk + dg
            for s in range(2):
                d = dva_v[:, LANES * s:LANES * (s + 1)]
                d = d + pltpu.roll(d, 64, 1)
                pieces.append(jnp.where(low, d, zero))
            pieces.append(dga_ref[r_, :])

            dgqb = jnp.zeros((1, LANES), F32)
            dq_cols = []
            for h in range(B_HEADS):
                _, xh, r = f["qb"][h]
                dx, dg = _col_bwd(dqb_ref[r_, LANES * h:LANES * (h + 1)] * QB_SCALE, xh, r, _row_sum,
                                  float(B_QK_DIM), gqb, cb, sb, 8, sel8)
                dq_cols.append(dx)
                dgqb = dgqb + dg
            dqr_b = jnp.concatenate(dq_cols, axis=1).astype(BF16)
            dwuq_ref[...] += _dot_tn(f["cqb"], dqr_b)
            dcq_raw, dgcq = _rms_bwd(_dot_nt(dqr_b, wuq), f["cqh"], f["rcq"], gcq)
            pieces.append(dcq_raw)

            dgkb = jnp.zeros((1, LANES), F32)
            dk_cols = []
            dkr = zero
            for h in range(B_HEADS):
                _, xh, r = f["kb"][h]
                dx, dg = _col_bwd(dkb_v[:, LANES * h:LANES * (h + 1)], xh, r, _row_sum, float(B_QK_DIM),
                                  gkb, cb, sb, 8, sel8)
                dk_cols.append(dx)
                dkr = dkr + dx
                dgkb = dgkb + dg
            dkr_b = jnp.concatenate(dk_cols, axis=1).astype(BF16)
            dvb_b = dvb_v.astype(BF16)
            dwuk_ref[...] += _dot_tn(f["ckvb"], dkr_b)
            dwuv_ref[...] += _dot_tn(f["ckvb"], dvb_b)
            dckv = _dot_nt(dkr_b, wuk) + _dot_nt(dvb_b, wuv)
            dckv_raw, dgckv = _rms_bwd(dckv, f["ckvh"], f["rckv"], gckv)
            pieces.append(dckv_raw)
            pieces.append(jnp.where((lane >= B_NOPE_DIM) & (lane < B_QK_DIM), dkr, zero))
            pieces += [zero, zero, zero]
            pieces.append(dgb_ref[r_, :])

            dproj_b = jnp.concatenate(pieces, axis=1).astype(BF16)
            dproj_ref[r_, :] = dproj_b
            dxn = _dot_nt(dproj_b, w)
            dx, dgin = _rms_bwd(dxn, f["xh0"], f["r0"], gin)
            dx_ref[r_, :] = dx + dh_ref[r_, :]

            for c in range(D_MODEL // LANES):
                small_ref[c:c + 1, :] += dgin[:, LANES * c:LANES * (c + 1)]
            small_ref[8:9, :] += dgq
            small_ref[9:10, :] += dgk
            for c in range(3):
                small_ref[10 + c:11 + c, :] += dgcq[:, LANES * c:LANES * (c + 1)]
            for c in range(2):
                small_ref[13 + c:14 + c, :] += dgckv[:, LANES * c:LANES * (c + 1)]
            small_ref[15:16, :] += dgqb
            small_ref[16:17, :] += dgkb

    def rows(width):
        return pl.BlockSpec((tm, width), lambda i: (i, 0))

    def cols(height):
        return pl.BlockSpec((height, tm), lambda i: (0, i))

    return pl.pallas_call(
        body, name="pre_bwd", grid=(s_len // tm,),
        in_specs=[rows(D_MODEL), rows(R_WIDTH), rows(D_MODEL), rows(512), cols(256), cols(256), rows(512), rows(512),
                  cols(512), cols(512), rows(512)] + [_full(t.shape) for t in tabs] + _pre_const_specs(consts),
        out_specs=[rows(D_MODEL), rows(N_WIDE), _full((B_Q_RANK, 512)), _full((B_KV_RANK, 512)),
                   _full((B_KV_RANK, 512)), _full((R_SMALL, LANES))],
        out_shape=[jax.ShapeDtypeStruct((s_len, D_MODEL), F32), jax.ShapeDtypeStruct((s_len, N_WIDE), BF16),
                   jax.ShapeDtypeStruct((B_Q_RANK, 512), F32),
                   jax.ShapeDtypeStruct((B_KV_RANK, 512), F32), jax.ShapeDtypeStruct((B_KV_RANK, 512), F32),
                   jax.ShapeDtypeStruct((R_SMALL, LANES), F32)],
        compiler_params=_params(("arbitrary",)),
    )(x, raw, dh, dqa, dka, dva, dga, dqb, dkb, dvb, dgb, *tabs, *[consts[n] for n in _PRE_IN_NAMES])


def _dw_in_call(xnb, dproj_b, tt, tn):
    s_len = xnb.shape[0]
    n_t = s_len // tt

    def body(a_ref, b_ref, o_ref, acc_ref):
        t = pl.program_id(1)

        @pl.when(t == 0)
        def _():
            acc_ref[...] = jnp.zeros(acc_ref.shape, F32)

        acc_ref[...] += _dot_tn(a_ref[...], b_ref[...])

        @pl.when(t == n_t - 1)
        def _():
            o_ref[...] = acc_ref[...].astype(BF16)

    return pl.pallas_call(
        body, name="dw_in", grid=(N_WIDE // tn, n_t),
        in_specs=[pl.BlockSpec((tt, D_MODEL), lambda n, t: (t, 0)), pl.BlockSpec((tt, tn), lambda n, t: (t, n))],
        out_specs=pl.BlockSpec((D_MODEL, tn), lambda n, t: (0, n)),
        out_shape=jax.ShapeDtypeStruct((D_MODEL, N_WIDE), BF16),
        scratch_shapes=[pltpu.VMEM((D_MODEL, tn), F32)],
        compiler_params=_params(("arbitrary", "arbitrary")),
    )(xnb, dproj_b)


def _mesh_pos():
    return lax.axis_index("x"), lax.axis_index("y"), lax.axis_index("c")


def _flip(v, bit):
    return 1 - v if bit else v


def _peer(pos, k):
    x, y, c = pos
    return _flip(x, (k >> 2) & 1), _flip(y, (k >> 1) & 1), _flip(c, k & 1)


def _logical(p):
    return 4 * p[0] + 2 * p[1] + p[2]


def _adamw(w, g, m, v):
    m = ADAM_B1 * m + (1.0 - ADAM_B1) * g
    v = ADAM_B2 * v + (1.0 - ADAM_B2) * (g * g)
    m_hat = m / (1.0 - ADAM_B1 ** ADAM_STEP)
    v_hat = v / (1.0 - ADAM_B2 ** ADAM_STEP)
    delta = -ADAM_LR * (m_hat / (jnp.sqrt(v_hat) + ADAM_EPS) + ADAM_WD * w)
    return delta, m, v


W_BLOCKS = ((D_MODEL, N_IN // N_DEV), (B_Q_RANK // N_DEV, 384), (B_KV_RANK, 768 // N_DEV), (D_MODEL // N_DEV, D_MODEL))
N_W = len(W_BLOCKS)


def _gather_blocks_call(blocks):
    def body(*refs):
        x_refs, out_refs, xb_refs = refs[0:N_W], refs[N_W:2 * N_W], refs[2 * N_W:3 * N_W]
        send_sems, recv_sems, local_sems = refs[3 * N_W:]
        x, y, c = _mesh_pos()
        me, sibling = (x, y, c), (x, y, 1 - c)
        chips = [(1 - x, y), (x, 1 - y), (1 - x, 1 - y)]
        for w in range(N_W):
            xb_refs[w][...] = x_refs[w][...].astype(BF16)

        def slot(w, p):
            return out_refs[w].at[_logical(p)]

        def copy(w, k, block, to, src=None):
            return pltpu.make_async_remote_copy(
                src_ref=slot(w, block) if src is None else src, dst_ref=slot(w, block),
                send_sem=send_sems.at[N_W * k + w], recv_sem=recv_sems.at[N_W * k + w],
                device_id=to, device_id_type=pl.DeviceIdType.MESH)

        mine = [pltpu.make_async_copy(xb_refs[w], slot(w, me), local_sems.at[w]) for w in range(N_W)]
        for cp in mine:
            cp.start()
        first = [copy(w, 0, me, sibling, src=xb_refs[w]) for w in range(N_W)]
        first += [copy(w, 1 + j, me, (*chip, c), src=xb_refs[w]) for j, chip in enumerate(chips) for w in range(N_W)]
        for cp in first:
            cp.start()
        passed = []
        for j, chip in enumerate(chips):
            for w in range(N_W):
                copy(w, 1 + j, (*chip, c), me).wait_recv()
                fwd = copy(w, 4 + j, (*chip, c), sibling)
                fwd.start()
                passed.append(fwd)
        for w in range(N_W):
            copy(w, 0, sibling, me).wait_recv()
        for j, chip in enumerate(chips):
            for w in range(N_W):
                copy(w, 4 + j, (*chip, 1 - c), me).wait_recv()
        for cp in first + passed:
            cp.wait_send()
        for cp in mine:
            cp.wait()

    vm = pl.BlockSpec(memory_space=pltpu.VMEM)
    return pl.pallas_call(
        body, name="gather_weights",
        out_shape=[jax.ShapeDtypeStruct((N_DEV,) + shp, BF16) for shp in W_BLOCKS],
        in_specs=[vm] * N_W, out_specs=[vm] * N_W,
        scratch_shapes=[pltpu.VMEM(shp, BF16) for shp in W_BLOCKS]
        + [pltpu.SemaphoreType.DMA((7 * N_W,)), pltpu.SemaphoreType.DMA((7 * N_W,)), pltpu.SemaphoreType.DMA((N_W,))],
        compiler_params=pltpu.CompilerParams(vmem_limit_bytes=VMEM_LIMIT),
    )(*blocks)


def _reduce_two_level_call(parts, small, w_blk, m_blk, v_blk, w_s, m_s, v_s):
    chunks = (32, 48, 64, 16)
    n_chip = N_DEV // 2

    def body(*refs):
        p_refs = refs[0:4]
        small_ref = refs[4]
        w_refs, m_refs, v_refs = refs[5:9], refs[9:13], refs[13:17]
        ws_ref, ms_ref, vs_ref = refs[17:20]
        g_refs, d_refs, nm_refs, nv_refs = refs[20:24], refs[24:28], refs[28:32], refs[32:36]
        gs_ref, ds_ref, nms_ref, nvs_ref = refs[36:40]
        ra_refs, rb_refs, st_refs = refs[40:44], refs[44:48], refs[48:52]
        recv_s_ref = refs[52]
        send_a, recv_a, send_b, recv_b, send_s_sems, recv_s_sems = refs[53:59]
        pos = _mesh_pos()
        x, y, c = pos
        me = _logical(pos)
        sibling = (x, y, 1 - c)

        def chip(j):
            return _flip(x, j & 1), _flip(y, (j >> 1) & 1)

        def to_sibling(w, j):
            return pltpu.make_async_remote_copy(
                src_ref=p_refs[w].at[_logical((*chip(j), 1 - c))], dst_ref=ra_refs[w].at[j],
                send_sem=send_a.at[N_W * j + w], recv_sem=recv_a.at[N_W * j + w],
                device_id=sibling, device_id_type=pl.DeviceIdType.MESH)

        def to_chip(w, j):
            return pltpu.make_async_remote_copy(
                src_ref=st_refs[w].at[j - 1], dst_ref=rb_refs[w].at[j - 1],
                send_sem=send_b.at[N_W * (j - 1) + w], recv_sem=recv_b.at[N_W * (j - 1) + w],
                device_id=(*chip(j), c), device_id_type=pl.DeviceIdType.MESH)

        def tiny(k):
            return pltpu.make_async_remote_copy(
                src_ref=small_ref, dst_ref=recv_s_ref.at[k],
                send_sem=send_s_sems.at[k], recv_sem=recv_s_sems.at[k],
                device_id=_peer(pos, k), device_id_type=pl.DeviceIdType.MESH)

        order = (0, 3, 2, 1)
        for j in (1, 2, 3, 0):
            for w in order:
                to_sibling(w, j).start()
        for k in range(1, N_DEV):
            tiny(k).start()

        for j in (1, 2, 3):
            d = _logical((*chip(j), c))
            for w in order:
                to_sibling(w, j).wait_recv()
                chunk = chunks[w]

                def pair(t, carry, w=w, j=j, d=d, chunk=chunk):
                    rows = pl.ds(pl.multiple_of(t * chunk, chunk), chunk)
                    s = p_refs[w][d, rows, :].astype(F32) + ra_refs[w][j, rows, :].astype(F32)
                    st_refs[w][j - 1, rows, :] = s.astype(BF16)
                    return carry

                lax.fori_loop(0, W_BLOCKS[w][0] // chunk, pair, 0)
                to_chip(w, j).start()

        recv_s_ref[0] = small_ref[...]
        for k in range(1, N_DEV):
            tiny(k).wait_recv()
        acc = recv_s_ref[me]
        for a in range(1, N_DEV):
            acc = acc + recv_s_ref[lax.bitwise_xor(me, a)]
        row = lax.broadcasted_iota(jnp.int32, (R_SMALL, LANES), 0)
        gs = jnp.where(row == 8, acc + pltpu.roll(acc, 64, 1), acc)
        gs = jnp.where(row == ROW_LOSS, jnp.sum(acc, axis=1, keepdims=True) * (0.5 / D_MODEL), gs)
        gs_ref[...] = gs
        ds, nms, nvs = _adamw(ws_ref[...], gs, ms_ref[...], vs_ref[...])
        ds_ref[...] = ds
        nms_ref[...] = nms
        nvs_ref[...] = nvs

        for w in (1, 2, 3, 0):
            to_sibling(w, 0).wait_recv()
            for j in (1, 2, 3):
                to_chip(w, j).wait_recv()
            chunk = chunks[w]

            def step(t, carry, w=w, chunk=chunk):
                rows = pl.ds(pl.multiple_of(t * chunk, chunk), chunk)
                g = p_refs[w][me, rows, :].astype(F32) + ra_refs[w][0, rows, :].astype(F32)
                for j in range(n_chip - 1):
                    g = g + rb_refs[w][j, rows, :].astype(F32)
                d, nm, nv = _adamw(w_refs[w][rows, :], g, m_refs[w][rows, :], v_refs[w][rows, :])
                g_refs[w][rows, :] = g
                d_refs[w][rows, :] = d
                nm_refs[w][rows, :] = nm
                nv_refs[w][rows, :] = nv
                return carry

            lax.fori_loop(0, W_BLOCKS[w][0] // chunk, step, 0)
        for k in range(1, N_DEV):
            tiny(k).wait_send()
        for w in range(N_W):
            for j in range(n_chip):
                to_sibling(w, j).wait_send()
            for j in (1, 2, 3):
                to_chip(w, j).wait_send()

    vm = pl.BlockSpec(memory_space=pltpu.VMEM)
    blk = [jax.ShapeDtypeStruct(shp, F32) for shp in W_BLOCKS]
    small_shape = jax.ShapeDtypeStruct((R_SMALL, LANES), F32)
    return pl.pallas_call(
        body, name="reduce_adamw",
        in_specs=[vm] * 20, out_specs=[vm] * 20,
        out_shape=blk * 4 + [small_shape] * 4,
        scratch_shapes=[pltpu.VMEM((n_chip,) + shp, BF16) for shp in W_BLOCKS]
        + [pltpu.VMEM((n_chip - 1,) + shp, BF16) for shp in W_BLOCKS] * 2
        + [pltpu.VMEM((N_DEV, R_SMALL, LANES), F32),
           pltpu.SemaphoreType.DMA((n_chip * N_W,)), pltpu.SemaphoreType.DMA((n_chip * N_W,)),
           pltpu.SemaphoreType.DMA(((n_chip - 1) * N_W,)), pltpu.SemaphoreType.DMA(((n_chip - 1) * N_W,)),
           pltpu.SemaphoreType.DMA((N_DEV,)), pltpu.SemaphoreType.DMA((N_DEV,))],
        compiler_params=pltpu.CompilerParams(vmem_limit_bytes=VMEM_LIMIT),
    )(*parts, small, *w_blk, *m_blk, *v_blk, w_s, m_s, v_s)


def _pack_small(norm_in, a_q, a_k, b_cq, b_ckv, b_q, b_k):
    def row(v):
        return jnp.pad(v.reshape(1, -1), ((0, 0), (0, LANES - v.size)))
    rows = [norm_in.reshape(8, LANES), row(a_q), row(a_k), b_cq.reshape(3, LANES), b_ckv.reshape(2, LANES),
            row(b_q), row(b_k), jnp.zeros((R_SMALL - 17, LANES), F32)]
    return jnp.concatenate(rows, axis=0)


def _unpack_small(s):
    return (s[0:8].reshape(1, D_MODEL), s[8:9, :64], s[9:10, :64], s[10:13].reshape(1, B_Q_RANK),
            s[13:15].reshape(1, B_KV_RANK), s[15:16, :B_QK_DIM], s[16:17, :B_QK_DIM])


C_IN = N_IN // N_DEV
_RUNS = ((0, 512, O_QA), (512, 576, O_KA), (576, 640, O_KA + 128), (640, 704, O_VA), (704, 768, O_VA + 128),
         (768, 1280, O_GA), (1280, 1664, O_CQ), (1664, 1920, O_CKV), (1920, 1952, O_KR + 64), (1952, 2464, O_GB))


def _in_cols(g_in, lo, hi):
    out = []
    for d in range(N_DEV):
        a, b = max(lo, C_IN * d), min(hi, C_IN * (d + 1))
        if a < b:
            out.append(g_in[d][:, a - C_IN * d:b - C_IN * d])
    return out


def _widen_weights(g_in, g_uq, g_ukv, g_out):
    z64 = jnp.zeros((D_MODEL, 64), BF16)
    z32 = jnp.zeros((D_MODEL, 32), BF16)
    k0, k1 = _in_cols(g_in, 512, 576), _in_cols(g_in, 576, 640)
    v0, v1 = _in_cols(g_in, 640, 704), _in_cols(g_in, 704, 768)
    kr_blk = [z64] + _in_cols(g_in, 1920, 1952) + [z32]
    w_wide = jnp.concatenate(
        _in_cols(g_in, 0, 512) + k0 + k0 + k1 + k1 + v0 + v0 + v1 + v1 + _in_cols(g_in, 768, 1920)
        + kr_blk * B_HEADS + _in_cols(g_in, 1952, 2464), axis=1)
    w_uq = g_uq.reshape(B_Q_RANK, 384)
    wuq = jnp.pad(w_uq.reshape(B_Q_RANK, B_HEADS, B_QK_DIM), ((0, 0), (0, 0), (0, LANES - B_QK_DIM)))
    wuq = wuq.reshape(B_Q_RANK, 512)
    ukv = g_ukv.transpose(1, 0, 2).reshape(B_KV_RANK, B_HEADS, B_NOPE_DIM + B_V_DIM)
    wuk = jnp.pad(ukv[:, :, :B_NOPE_DIM], ((0, 0), (0, 0), (0, LANES - B_NOPE_DIM))).reshape(B_KV_RANK, 512)
    wuv = ukv[:, :, B_NOPE_DIM:].reshape(B_KV_RANK, 512)
    return w_wide, wuq, wuk, wuv, g_out.reshape(D_MODEL, D_MODEL)


def _grad_blocks(dw_wide, dwuq, dwuk, dwuv, dw_out):
    blocks = []
    for d in range(N_DEV):
        pieces = []
        for lo, hi, wide in _RUNS:
            a, b = max(lo, C_IN * d), min(hi, C_IN * (d + 1))
            if a < b:
                pieces.append(dw_wide[:, wide + a - lo:wide + b - lo])
        blocks.append(jnp.concatenate(pieces, axis=1))
    p_in = jnp.stack(blocks).astype(BF16)
    dw_uq = dwuq.reshape(B_Q_RANK, B_HEADS, LANES)[:, :, :B_QK_DIM].reshape(N_DEV, B_Q_RANK // N_DEV, 384)
    dk = dwuk.reshape(B_KV_RANK, B_HEADS, LANES)[:, :, :B_NOPE_DIM]
    dv = dwuv.reshape(B_KV_RANK, B_HEADS, B_V_DIM)
    dw_ukv = jnp.concatenate([dk, dv], axis=2).reshape(B_KV_RANK, N_DEV, 768 // N_DEV).transpose(1, 0, 2)
    return (p_in, dw_uq.astype(BF16), dw_ukv.astype(BF16),
            dw_out.reshape(N_DEV, D_MODEL // N_DEV, D_MODEL).astype(BF16))


def _rope_tables(s_len):
    row = jnp.arange(s_len // GRID_W, dtype=F32)
    col = jnp.arange(GRID_W, dtype=F32)

    def parts(dim):
        half = dim // 2
        inv = 1.0 / (ROPE_THETA ** (jnp.arange(0, half, 2, dtype=F32) / half))
        ar, ac = row[:, None] * inv[None, :], col[:, None] * inv[None, :]
        zr, zc = jnp.zeros_like(ar), jnp.zeros_like(ac)
        cos_c = jnp.concatenate([zc, zc, jnp.cos(ac), jnp.cos(ac)], axis=1)
        cos_r = jnp.concatenate([jnp.cos(ar), jnp.cos(ar), zr, zr], axis=1)
        sin_c = jnp.concatenate([zc, zc, -jnp.sin(ac), jnp.sin(ac)], axis=1)
        sin_r = jnp.concatenate([-jnp.sin(ar), jnp.sin(ar), zr, zr], axis=1)
        return cos_c, cos_r, sin_c, sin_r

    tabs = [jnp.tile(t, (1, 2)) for t in parts(A_HEAD_DIM)]
    for n, t in enumerate(parts(B_ROPE_DIM)):
        lead = jnp.full((t.shape[0], B_NOPE_DIM), 1.0 if n == 0 else 0.0, F32)
        tail = jnp.full((t.shape[0], LANES - B_QK_DIM), 1.0 if n == 0 else 0.0, F32)
        tabs.append(jnp.concatenate([lead, t, tail], axis=1))
    return tuple(tabs)


def kernel(x, norm_in, w_in, a_q_norm, a_k_norm, b_cq_norm, b_ckv_norm, w_uq, w_ukv, b_q_norm, b_k_norm, w_out, loss_target, m_norm_in, m_w_in, m_a_q_norm, m_a_k_norm, m_b_cq_norm, m_b_ckv_norm, m_w_uq, m_w_ukv, m_b_q_norm, m_b_k_norm, m_w_out, v_norm_in, v_w_in, v_a_q_norm, v_a_k_norm, v_b_cq_norm, v_b_ckv_norm, v_w_uq, v_w_ukv, v_b_q_norm, v_b_k_norm, v_w_out):
    s_len = x.shape[1]
    tm = min(256, s_len)
    tq, tk = min(512, s_len), min(2048, s_len)
    ftq, ftk = min(256, s_len), min(256, s_len)
    x2 = x.reshape(s_len, D_MODEL)
    t2 = loss_target.reshape(s_len, D_MODEL)

    w_blk = (w_in[0], w_uq[0], w_ukv[0], w_out[0])
    w_wide, wuq, wuk, wuv, wout = _widen_weights(*_gather_blocks_call(w_blk))

    def dup(v, pad_to=None):
        v = v.reshape(1, -1)
        if pad_to is None:
            return jnp.concatenate([v, v], axis=1)
        return jnp.pad(v, ((0, 0), (0, pad_to - v.shape[1])))

    g64 = jnp.asarray(np.kron(np.eye(2), np.ones((64, 64))), dtype=BF16)
    consts = dict(gin=norm_in, w=w_wide, wuq=wuq, wuk=wuk, wuv=wuv, gq=dup(a_q_norm), gk=dup(a_k_norm),
                  gcq=b_cq_norm, gckv=b_ckv_norm, gqb=dup(b_q_norm, LANES), gkb=dup(b_k_norm, LANES), g64=g64)
    tabs = _rope_tables(s_len)

    qa, ka, va, va_t, ga, qb, kb, vb, vb_t, gb, raw, xnb = _pre_fwd_call(x2, consts, tabs, min(512, s_len))
    oa, lse_a_t = _attn_fwd_t_call(qa, ka, va_t, groups=A_KV_HEADS, sub=4, masked=True, scale=None, tq=ftq, tk=ftk,
                                   name="attn_fwd_a")
    ob, lse_b_t = _attn_fwd_t_call(qb, kb, vb_t, groups=1, sub=B_HEADS, masked=False, scale=None, tq=ftq,
                                   tk=ftk, name="attn_fwd_b")
    dh, doa, dob, dga, dgb, dw_out, loss_row = _out_call(x2, t2, oa, ob, ga, gb, wout, min(512, s_len))

    dqa, dka_t, dva_t = _attn_bwd_q_call(qa, ka, va, doa, oa, lse_a_t, groups=A_KV_HEADS, sub=4, masked=True,
                                         tq=tq, tk=min(4096, s_len), ck=min(512, s_len), name="attn_bwd_a")
    dqb, dkb_t, dvb_t = _attn_bwd_q_call(qb, kb, vb, dob, ob, lse_b_t, groups=2, sub=2, masked=False,
                                         tq=tq, tk=min(4096, s_len), ck=min(256, s_len), name="attn_bwd_b")
    dx, dproj_b, dwuq, dwuk, dwuv, small = _pre_bwd_call(
        x2, raw, dh, dqa, dka_t, dva_t, dga, dqb, dkb_t, dvb_t, dgb, consts, tabs, tm)
    dw_wide = _dw_in_call(xnb, dproj_b, min(1024, s_len), N_WIDE)

    parts = _grad_blocks(dw_wide, dwuq, dwuk, dwuv, dw_out)
    small = jnp.concatenate([small[:ROW_LOSS], loss_row, small[ROW_LOSS + 1:]], axis=0)

    m_blk = (m_w_in[0], m_w_uq[0], m_w_ukv[0], m_w_out[0])
    v_blk = (v_w_in[0], v_w_uq[0], v_w_ukv[0], v_w_out[0])
    w_s = _pack_small(norm_in, a_q_norm, a_k_norm, b_cq_norm, b_ckv_norm, b_q_norm, b_k_norm)
    m_s = _pack_small(m_norm_in, m_a_q_norm, m_a_k_norm, m_b_cq_norm, m_b_ckv_norm, m_b_q_norm, m_b_k_norm)
    v_s = _pack_small(v_norm_in, v_a_q_norm, v_a_k_norm, v_b_cq_norm, v_b_ckv_norm, v_b_q_norm, v_b_k_norm)
    res = _reduce_two_level_call(parts, small, w_blk, m_blk, v_blk, w_s, m_s, v_s)

    def leaves(blocks, sm):
        wi, uq, ukv, wo = [b[None] for b in blocks]
        n_in, aq, ak, bcq, bckv, bq, bk = _unpack_small(sm)
        return [n_in, wi, aq, ak, bcq, bckv, uq, ukv, bq, bk, wo]

    g_s = res[16]
    loss = g_s[ROW_LOSS, 0]
    grad_x = dx.reshape(1, s_len, D_MODEL)
    return (loss, grad_x, *leaves(res[0:4], res[16]), *leaves(res[4:8], res[17]), *leaves(res[8:12], res[18]),
            *leaves(res[12:16], res[19]))
```
